```python
import math
import jax
import jax.numpy as jnp
from jax import lax
import numpy as np

D_MODEL = 2048
BATCH = 4
SEQ = 2048
DEPTH = 1

ATTN_HEAD_DIM = 128
ATTN_WIDTH = D_MODEL // 2
N_ATTN_HEADS = ATTN_WIDTH // ATTN_HEAD_DIM
Q_BLOCK = 128
GMLP_GROUP_DIM = 128
GMLP_WIDTH = D_MODEL // 2
N_GMLP_GROUPS = GMLP_WIDTH // GMLP_GROUP_DIM
CHUNK = 128
PROJ_SIZES = (ATTN_WIDTH, ATTN_WIDTH, ATTN_WIDTH, N_ATTN_HEADS, 2 * GMLP_WIDTH, 2 * D_MODEL)
PROJ_WIDTH = sum(PROJ_SIZES)
N_EXPERTS = 32
TOP_K = 4
D_FF = D_MODEL
SWIGLU_ALPHA = 1.702
SWIGLU_LIMIT = 7.0
MOE_BLOCK = 128
NORM_EPS = 1e-5

kernel_name = "fox_gmlp_gated_hybrid_moe"


def rms_norm(x, g):
    xf = x.astype(jnp.float32)
    y = xf * lax.rsqrt(jnp.mean(xf * xf, axis=-1, keepdims=True) + NORM_EPS)
    return (y * g.astype(jnp.float32)).astype(x.dtype)


def layer_norm(x, g, b):
    xf = x.astype(jnp.float32)
    mu = jnp.mean(xf, axis=-1, keepdims=True)
    var = jnp.mean(jnp.square(xf - mu), axis=-1, keepdims=True)
    y = (xf - mu) * lax.rsqrt(var + NORM_EPS)
    return (y * g.astype(jnp.float32) + b.astype(jnp.float32)).astype(x.dtype)


def forgetting_attention(q, k, v, f_logit):
    B, S, H, Dh = q.shape
    scale = 1.0 / math.sqrt(Dh)
    c = jnp.cumsum(jax.nn.log_sigmoid(f_logit.astype(jnp.float32)), axis=1)
    c = c.transpose(0, 2, 1)
    qh = q.transpose(0, 2, 1, 3)
    kh = k.transpose(0, 2, 1, 3)
    vh = v.transpose(0, 2, 1, 3)
    nq = S // Q_BLOCK
    qb = qh.reshape(B, H, nq, Q_BLOCK, Dh).transpose(2, 0, 1, 3, 4)
    cb = c.reshape(B, H, nq, Q_BLOCK).transpose(2, 0, 1, 3)
    kpos = jnp.arange(S)

    def one_block(args):
        qi, ci, i = args
        qpos = i * Q_BLOCK + jnp.arange(Q_BLOCK)
        s = jnp.einsum('bhqd,bhkd->bhqk', qi, kh).astype(jnp.float32) * scale
        s = s + (ci[..., :, None] - c[:, :, None, :])
        s = jnp.where(kpos[None, :] <= qpos[:, None], s, -jnp.inf)
        p = jax.nn.softmax(s, axis=-1)
        return jnp.einsum('bhqk,bhkd->bhqd', p.astype(vh.dtype), vh)

    o = lax.map(one_block, (qb, cb, jnp.arange(nq)))
    return o.transpose(1, 0, 3, 2, 4).reshape(B, S, H * Dh)


def chunked_spatial_gating(z, ln_g, ln_b, w_s, b_s):
    u, v = jnp.split(z, 2, axis=-1)
    v = layer_norm(v, ln_g, ln_b)
    B, S, _ = v.shape
    nc = S // CHUNK
    vg = v.reshape(B, nc, CHUNK, N_GMLP_GROUPS, GMLP_GROUP_DIM)
    causal = jnp.tril(jnp.ones((CHUNK, CHUNK), dtype=bool))
    w = jnp.where(causal[None], w_s, jnp.zeros_like(w_s))
    mixed = jnp.einsum('gts,bcsgd->bctgd', w, vg) + b_s.T[None, None, :, :, None]
    return u * mixed.reshape(B, S, GMLP_WIDTH)


def clamped_swiglu(gu):
    x_glu = jnp.minimum(gu[..., ::2], SWIGLU_LIMIT)
    x_lin = jnp.clip(gu[..., 1::2], -SWIGLU_LIMIT, SWIGLU_LIMIT)
    return x_glu * jax.nn.sigmoid(SWIGLU_ALPHA * x_glu) * (x_lin + 1.0)


def moe_ffn(h, w_router, b_router, w_up, b_up, w_down, b_down):
    B, S, D = h.shape
    T = B * S
    hf = h.reshape(T, D)
    logits = (hf @ w_router + b_router).astype(jnp.float32)
    top_vals, top_idx = lax.top_k(logits, TOP_K)
    gates = jax.nn.softmax(top_vals, axis=-1).astype(h.dtype)

    n_assign = T * TOP_K
    e_flat = top_idx.reshape(-1).astype(jnp.int32)
    tok_flat = jnp.arange(n_assign, dtype=jnp.int32) // TOP_K
    g_flat = gates.reshape(-1)
    order = jnp.argsort(e_flat)
    e_sorted = e_flat[order]
    counts = jnp.bincount(e_flat, length=N_EXPERTS).astype(jnp.int32)
    padded = (counts + MOE_BLOCK - 1) // MOE_BLOCK * MOE_BLOCK
    pad_end = jnp.cumsum(padded)
    pad_start = pad_end - padded
    start = jnp.cumsum(counts) - counts
    rank = jnp.arange(n_assign, dtype=jnp.int32) - start[e_sorted]
    dest = pad_start[e_sorted] + rank
    n_rows = n_assign + N_EXPERTS * MOE_BLOCK
    n_blocks = n_rows // MOE_BLOCK
    row_tok = jnp.full((n_rows,), T, dtype=jnp.int32).at[dest].set(tok_flat[order])
    row_gate = jnp.zeros((n_rows,), dtype=h.dtype).at[dest].set(g_flat[order])
    blk_start = jnp.arange(n_blocks, dtype=jnp.int32) * MOE_BLOCK
    blk_exp = jnp.minimum(jnp.searchsorted(pad_end, blk_start, side='right'), N_EXPERTS - 1)
    h_pad = jnp.concatenate([hf, jnp.zeros((1, D), dtype=hf.dtype)], axis=0)

    def expert_block(args):
        toks, e = args
        xb = h_pad[toks]
        gu = xb @ w_up[e] + b_up[e]
        return clamped_swiglu(gu) @ w_down[e] + b_down[e]

    y = lax.map(expert_block, (row_tok.reshape(n_blocks, MOE_BLOCK), blk_exp))
    y = y.reshape(n_rows, D) * row_gate[:, None]
    out = jnp.zeros((T + 1, D), dtype=y.dtype).at[row_tok].add(y)[:T]
    return out.reshape(B, S, D)


def setup_inputs(seed: int = 0) -> dict:
    key = jax.random.key(seed)
    ks = jax.random.split(key, 24)
    n = lambda k, shape, s: jax.random.normal(k, shape, jnp.float32) * s
    D = D_MODEL
    return {
        "x": n(ks[0], (BATCH, SEQ, D), 1.0),
        "g_mix": 1.0 + n(ks[1], (D,), 0.02),
        "w_in": n(ks[2], (D, PROJ_WIDTH), D ** -0.5),
        "b_forget": 2.0 + n(ks[3], (N_ATTN_HEADS,), 0.5),
        "g_v_ln": 1.0 + n(ks[4], (GMLP_WIDTH,), 0.02),
        "b_v_ln": n(ks[5], (GMLP_WIDTH,), 0.02),
        "w_spatial": n(ks[6], (N_GMLP_GROUPS, CHUNK, CHUNK), CHUNK ** -0.5),
        "b_spatial": 1.0 + n(ks[7], (N_GMLP_GROUPS, CHUNK), 0.02),
        "w_branch_attn": n(ks[8], (ATTN_WIDTH, D), ATTN_WIDTH ** -0.5),
        "w_branch_gmlp": n(ks[9], (GMLP_WIDTH, D), GMLP_WIDTH ** -0.5),
        "w_out": n(ks[10], (D, D), D ** -0.5),
        "g_ffn": 1.0 + n(ks[11], (D,), 0.02),
        "w_router": n(ks[12], (D, N_EXPERTS), D ** -0.5),
        "b_router": n(ks[13], (N_EXPERTS,), 0.01),
        "w_expert_up": n(ks[14], (N_EXPERTS, D, 2 * D_FF), D ** -0.5),
        "b_expert_up": n(ks[15], (N_EXPERTS, 2 * D_FF), 0.01),
        "w_expert_down": n(ks[16], (N_EXPERTS, D_FF, D), D_FF ** -0.5),
        "b_expert_down": n(ks[17], (N_EXPERTS, D), 0.01),
        "g_final": 1.0 + n(ks[18], (D,), 0.02),
    }


def reference(x, g_mix, w_in, b_forget, g_v_ln, b_v_ln, w_spatial, b_spatial,
              w_branch_attn, w_branch_gmlp, w_out, g_ffn, w_router, b_router,
              w_expert_up, b_expert_up, w_expert_down, b_expert_down, g_final):
    B, S, D = x.shape
    split_at = list(np.cumsum(PROJ_SIZES)[:-1])
    for _ in range(DEPTH):
        h = rms_norm(x, g_mix)
        proj = h @ w_in
        q, k, v, f_logit, z, gate_logits = jnp.split(proj, split_at, axis=-1)
        q = q.reshape(B, S, N_ATTN_HEADS, ATTN_HEAD_DIM)
        k = k.reshape(B, S, N_ATTN_HEADS, ATTN_HEAD_DIM)
        v = v.reshape(B, S, N_ATTN_HEADS, ATTN_HEAD_DIM)
        attn = forgetting_attention(q, k, v, f_logit + b_forget)
        sg = chunked_spatial_gating(jax.nn.gelu(z, approximate=False),
                                    g_v_ln, b_v_ln, w_spatial, b_spatial)
        gate_a, gate_b = jnp.split(gate_logits, 2, axis=-1)
        merged = (jax.nn.sigmoid(gate_a) * (attn @ w_branch_attn)
                  + jax.nn.sigmoid(gate_b) * (sg @ w_branch_gmlp))
        x = x + merged @ w_out
        h2 = rms_norm(x, g_ffn)
        x = x + moe_ffn(h2, w_router, b_router, w_expert_up, b_expert_up,
                        w_expert_down, b_expert_down)
    return rms_norm(x, g_final)
```

```python
import functools
import math

import jax
import jax.numpy as jnp
import numpy as np
from jax import lax
from jax.experimental import pallas as pl
from jax.experimental.pallas import tpu as pltpu

F32 = jnp.float32
BF16 = jnp.bfloat16
I32 = jnp.int32

NORM_EPS = 1e-5
HEAD_DIM = 128
CHUNK = 128
GROUP_DIM = 128
TOP_K = 4
SWIGLU_ALPHA = 1.702
SWIGLU_LIMIT = 7.0

VMEM_LIMIT_BYTES = 56 * 1024 * 1024

ROW_BLOCK = 256
ITEM_BLOCKS = 6
FF_TILE = 256


def _cparams(sem, **kw):
    return pltpu.CompilerParams(dimension_semantics=sem, vmem_limit_bytes=VMEM_LIMIT_BYTES, **kw)


def _rmsnorm_kernel(x_ref, g_ref, o_ref):
    x = x_ref[...]
    ms = jnp.mean(x * x, axis=-1, keepdims=True)
    o_ref[...] = (x * lax.rsqrt(ms + NORM_EPS) * g_ref[...]).astype(o_ref.dtype)


def _rmsnorm(x, g, out_dtype, tm=512):
    T, D = x.shape
    return pl.pallas_call(
        _rmsnorm_kernel,
        grid=(T // tm,),
        in_specs=[pl.BlockSpec((tm, D), lambda i: (i, 0)), pl.BlockSpec((1, D), lambda i: (0, 0))],
        out_specs=pl.BlockSpec((tm, D), lambda i: (i, 0)),
        out_shape=jax.ShapeDtypeStruct((T, D), out_dtype),
        compiler_params=_cparams(("parallel",)),
        name="rmsnorm",
    )(x, g.reshape(1, D))


def _proj_kernel(h_ref, w_ref, o_ref, wbf_ref):
    @pl.when(pl.program_id(1) == 0)
    def _():
        wbf_ref[...] = w_ref[...].astype(BF16)

    o_ref[...] = jnp.dot(h_ref[...], wbf_ref[...], preferred_element_type=F32).astype(o_ref.dtype)


def _project(h, w, n_cols, out_dtype, tm=1024, tn=512):
    T, D = h.shape
    return pl.pallas_call(
        _proj_kernel,
        grid=(n_cols // tn, T // tm),
        in_specs=[pl.BlockSpec((tm, D), lambda j, i: (i, 0)),
                  pl.BlockSpec((D, tn), lambda j, i: (0, j))],
        out_specs=pl.BlockSpec((tm, tn), lambda j, i: (i, j)),
        out_shape=jax.ShapeDtypeStruct((T, n_cols), out_dtype),
        scratch_shapes=[pltpu.VMEM((D, tn), BF16)],
        compiler_params=_cparams(("arbitrary", "arbitrary")),
        name="qkv_proj",
    )(h, w)


def _forget_kernel(h_ref, wft_ref, bf_ref, c_ref):
    ft = lax.dot_general(wft_ref[...], h_ref[...], (((1,), (1,)), ((), ())),
                         preferred_element_type=F32)
    c = jax.nn.log_sigmoid(ft + bf_ref[...])
    S = c.shape[1]
    lane = lax.broadcasted_iota(I32, c.shape, 1)
    shift = 1
    while shift < S:
        c = c + jnp.where(lane >= shift, pltpu.roll(c, shift, axis=1), 0.0)
        shift *= 2
    c_ref[0] = c


def _forget_cumsum(h, wft, b_forget, B, S):
    T, D = h.shape
    H = wft.shape[0]
    return pl.pallas_call(
        _forget_kernel,
        grid=(B,),
        in_specs=[pl.BlockSpec((S, D), lambda b: (b, 0)),
                  pl.BlockSpec((H, D), lambda b: (0, 0)),
                  pl.BlockSpec((H, 1), lambda b: (0, 0))],
        out_specs=pl.BlockSpec((1, H, S), lambda b: (b, 0, 0)),
        out_shape=jax.ShapeDtypeStruct((B, H, S), F32),
        compiler_params=_cparams(("parallel",)),
        name="forget_cumsum",
    )(h, wft, b_forget.reshape(H, 1))


def _attn_kernel(q_ref, k_ref, v_ref, ccol_ref, crow_ref, o_ref, *, n_heads, tq):
    i = pl.program_id(1)
    scale = 1.0 / math.sqrt(HEAD_DIM)
    row = lax.broadcasted_iota(I32, (tq, tq), 0)
    col = lax.broadcasted_iota(I32, (tq, tq), 1)
    causal = col <= row
    for h in range(n_heads):
        hs = slice(h * HEAD_DIM, (h + 1) * HEAD_DIM)
        qh = q_ref[:, hs]
        cq = ccol_ref[0, :, h:h + 1]

        def scores(j):
            kj = k_ref[pl.ds(pl.multiple_of(j * tq, tq), tq), hs]
            s = lax.dot_general(qh, kj, (((1,), (1,)), ((), ())), preferred_element_type=F32)
            return s * scale + (cq - crow_ref[0, h, j])

        def update(j, s, carry):
            m, l, acc = carry
            m_new = jnp.maximum(m, jnp.max(s, axis=-1, keepdims=True))
            alpha = jnp.exp(m - m_new)
            p = jnp.exp(s - m_new)
            vj = v_ref[pl.ds(pl.multiple_of(j * tq, tq), tq), hs]
            acc = alpha * acc + jnp.dot(p.astype(BF16), vj, preferred_element_type=F32)
            return m_new, alpha * l + jnp.sum(p, axis=-1, keepdims=True), acc

        def body(j, carry):
            return update(j, scores(j), carry)

        init = (jnp.full((tq, 1), -jnp.inf, F32), jnp.zeros((tq, 1), F32),
                jnp.zeros((tq, HEAD_DIM), F32))
        carry = lax.fori_loop(0, i, body, init)
        s_diag = jnp.where(causal, scores(i), -jnp.inf)
        m, l, acc = update(i, s_diag, carry)
        o_ref[:, hs] = (acc / l).astype(o_ref.dtype)


def _attention(qkv, c_row, B, S, n_heads, tq=256):
    T = qkv.shape[0]
    W = n_heads * HEAD_DIM
    nq = S // tq
    c_col = c_row.transpose(0, 2, 1)
    c_row5 = c_row.reshape(B, n_heads, nq, 1, tq)
    return pl.pallas_call(
        functools.partial(_attn_kernel, n_heads=n_heads, tq=tq),
        grid=(B, nq),
        in_specs=[pl.BlockSpec((tq, W), lambda b, i: (b * nq + i, 0)),
                  pl.BlockSpec((S, W), lambda b, i: (b, 1)),
                  pl.BlockSpec((S, W), lambda b, i: (b, 2)),
                  pl.BlockSpec((1, tq, n_heads), lambda b, i: (b, i, 0)),
                  pl.BlockSpec((1, n_heads, nq, 1, tq), lambda b, i: (b, 0, 0, 0, 0))],
        out_specs=pl.BlockSpec((tq, W), lambda b, i: (b * nq + i, 0)),
        out_shape=jax.ShapeDtypeStruct((T, W), BF16),
        compiler_params=_cparams(("parallel", "arbitrary")),
        name="fox_attention",
    )(qkv, qkv, qkv, c_col, c_row5)


def _gmlp_kernel(h_ref, wz_ref, g_ref, b_ref, ws_ref, bst_ref, o_ref, *, n_groups):
    z = jnp.dot(h_ref[...], wz_ref[...], preferred_element_type=F32)
    z = 0.5 * z * (1.0 + lax.erf(z * (1.0 / math.sqrt(2.0))))
    W = z.shape[1] // 2
    u = z[:, :W]
    v = z[:, W:]
    mu = jnp.mean(v, axis=-1, keepdims=True)
    var = jnp.mean(jnp.square(v - mu), axis=-1, keepdims=True)
    vn = (v - mu) * lax.rsqrt(var + NORM_EPS) * g_ref[...] + b_ref[...]
    row = lax.broadcasted_iota(I32, (CHUNK, CHUNK), 0)
    col = lax.broadcasted_iota(I32, (CHUNK, CHUNK), 1)
    tril = col <= row
    tg = z.shape[0]
    for g in range(n_groups):
        gs = slice(g * GROUP_DIM, (g + 1) * GROUP_DIM)
        wg = jnp.where(tril, ws_ref[g], 0.0).astype(BF16)
        bias = bst_ref[:, g:g + 1]
        for c in range(tg // CHUNK):
            cs = slice(c * CHUNK, (c + 1) * CHUNK)
            mixed = jnp.dot(wg, vn[cs, gs].astype(BF16), preferred_element_type=F32) + bias
            o_ref[cs, gs] = (u[cs, gs] * mixed).astype(o_ref.dtype)


def _gmlp(h, wz, g_v_ln, b_v_ln, w_spatial, b_spatial, tg=512):
    T, D = h.shape
    W2 = wz.shape[1]
    W = W2 // 2
    G = w_spatial.shape[0]
    return pl.pallas_call(
        functools.partial(_gmlp_kernel, n_groups=G),
        grid=(T // tg,),
        in_specs=[pl.BlockSpec((tg, D), lambda i: (i, 0)),
                  pl.BlockSpec((D, W2), lambda i: (0, 0)),
                  pl.BlockSpec((1, W), lambda i: (0, 0)),
                  pl.BlockSpec((1, W), lambda i: (0, 0)),
                  pl.BlockSpec((G, CHUNK, CHUNK), lambda i: (0, 0, 0)),
                  pl.BlockSpec((CHUNK, G), lambda i: (0, 0))],
        out_specs=pl.BlockSpec((tg, W), lambda i: (i, 0)),
        out_shape=jax.ShapeDtypeStruct((T, W), BF16),
        compiler_params=_cparams(("parallel",)),
        name="gmlp",
    )(h, wz, g_v_ln.reshape(1, W), b_v_ln.reshape(1, W), w_spatial, b_spatial.T)


def _merge_kernel(attn_ref, sg_ref, h_ref, wa_ref, wb_ref, wga_ref, wgb_ref, o_ref):
    h = h_ref[...]
    a = jnp.dot(attn_ref[...], wa_ref[...], preferred_element_type=F32)
    ga = jnp.dot(h, wga_ref[...], preferred_element_type=F32)
    m = jax.nn.sigmoid(ga) * a
    b = jnp.dot(sg_ref[...], wb_ref[...], preferred_element_type=F32)
    gb = jnp.dot(h, wgb_ref[...], preferred_element_type=F32)
    o_ref[...] = (m + jax.nn.sigmoid(gb) * b).astype(o_ref.dtype)


def _merge(attn, sg, h, wa, wb, wg, tm=512, tn=512):
    T, D = h.shape
    Wa = attn.shape[1]
    Wb = sg.shape[1]
    nt = D // tn
    return pl.pallas_call(
        _merge_kernel,
        grid=(nt, T // tm),
        in_specs=[pl.BlockSpec((tm, Wa), lambda j, i: (i, 0)),
                  pl.BlockSpec((tm, Wb), lambda j, i: (i, 0)),
                  pl.BlockSpec((tm, D), lambda j, i: (i, 0)),
                  pl.BlockSpec((Wa, tn), lambda j, i: (0, j)),
                  pl.BlockSpec((Wb, tn), lambda j, i: (0, j)),
                  pl.BlockSpec((D, tn), lambda j, i: (0, j)),
                  pl.BlockSpec((D, tn), lambda j, i: (0, j + nt))],
        out_specs=pl.BlockSpec((tm, tn), lambda j, i: (i, j)),
        out_shape=jax.ShapeDtypeStruct((T, D), BF16),
        compiler_params=_cparams(("arbitrary", "arbitrary")),
        name="gated_merge",
    )(attn, sg, h, wa, wb, wg, wg)


def _out_router_kernel(m_ref, x_ref, wo_ref, g_ref, wr_ref, br_ref,
                       x1_ref, h2_ref, sel_ref, gate_ref):
    x1 = x_ref[...] + jnp.dot(m_ref[...], wo_ref[...], preferred_element_type=F32)
    x1_ref[...] = x1
    ms = jnp.mean(x1 * x1, axis=-1, keepdims=True)
    h2 = x1 * lax.rsqrt(ms + NORM_EPS) * g_ref[...]
    h2_ref[...] = h2
    logits = jnp.dot(h2, wr_ref[...], preferred_element_type=F32,
                     precision=lax.Precision.HIGHEST) + br_ref[...]
    E = logits.shape[1]
    lane = lax.broadcasted_iota(I32, logits.shape, 1)
    work = logits
    sel = jnp.zeros(logits.shape, F32)
    num = jnp.zeros(logits.shape, F32)
    denom = jnp.zeros((logits.shape[0], 1), F32)
    m0 = None
    for _ in range(TOP_K):
        m = jnp.max(work, axis=-1, keepdims=True)
        idx = jnp.min(jnp.where(work == m, lane, E), axis=-1, keepdims=True)
        onehot = lane == idx
        if m0 is None:
            m0 = m
        e = jnp.exp(m - m0)
        sel = jnp.where(onehot, 1.0, sel)
        num = jnp.where(onehot, e, num)
        denom = denom + e
        work = jnp.where(onehot, -jnp.inf, work)
    sel_ref[...] = sel
    gate_ref[...] = num / denom


def _out_router(merged, x, wo, g_ffn, w_router, b_router, to=256):
    T, D = x.shape
    E = w_router.shape[1]
    row = lambda i: (i, 0)
    fixed = lambda i: (0, 0)
    return pl.pallas_call(
        _out_router_kernel,
        grid=(T // to,),
        in_specs=[pl.BlockSpec((to, D), row), pl.BlockSpec((to, D), row),
                  pl.BlockSpec((D, D), fixed), pl.BlockSpec((1, D), fixed),
                  pl.BlockSpec((D, E), fixed), pl.BlockSpec((1, E), fixed)],
        out_specs=[pl.BlockSpec((to, D), row), pl.BlockSpec((to, D), row),
                   pl.BlockSpec((to, E), row), pl.BlockSpec((to, E), row)],
        out_shape=[jax.ShapeDtypeStruct((T, D), F32), jax.ShapeDtypeStruct((T, D), F32),
                   jax.ShapeDtypeStruct((T, E), F32), jax.ShapeDtypeStruct((T, E), F32)],
        compiler_params=_cparams(("parallel",)),
        name="out_router",
    )(merged, x, wo, g_ffn.reshape(1, D), w_router, b_router.reshape(1, E))


def _routing_kernel(sel_ref, gate_ref, dest_ref, g4_ref, nblk_ref, start_ref, rank_ref, *, tile):
    T, E = sel_ref.shape
    nt = T // tile
    r = lax.broadcasted_iota(I32, (tile, tile), 0)
    c = lax.broadcasted_iota(I32, (tile, tile), 1)
    strict_lower = (c < r).astype(BF16)
    er = lax.broadcasted_iota(I32, (E, E), 0)
    ec = lax.broadcasted_iota(I32, (E, E), 1)
    strict_upper = (er < ec).astype(BF16)

    def pass1(t, carry):
        rows = pl.ds(pl.multiple_of(t * tile, tile), tile)
        a = sel_ref[rows, :]
        rank_ref[rows, :] = jnp.dot(strict_lower, a.astype(BF16), preferred_element_type=F32) + carry
        return carry + jnp.sum(a, axis=0, keepdims=True)

    counts = lax.fori_loop(0, nt, pass1, jnp.zeros((1, E), F32))
    nblk = jnp.floor((counts + (ROW_BLOCK - 1)) * (1.0 / ROW_BLOCK))
    start_blk = jnp.dot(nblk.astype(BF16), strict_upper, preferred_element_type=F32)
    nblk_ref[...] = nblk.astype(I32)
    start_ref[...] = start_blk.astype(I32)
    start_row = start_blk * float(ROW_BLOCK)
    lane = lax.broadcasted_iota(I32, (tile, 128), 1)

    def pass2(t, _):
        rows = pl.ds(pl.multiple_of(t * tile, tile), tile)
        a = sel_ref[rows, :]
        g = gate_ref[rows, :]
        dest_e = rank_ref[rows, :] + start_row
        slot = jnp.dot(a.astype(BF16), strict_upper, preferred_element_type=F32)
        d4 = jnp.zeros((tile, 128), F32)
        g4 = jnp.zeros((tile, 128), F32)
        for s in range(TOP_K):
            pick = (a > 0.5) & (slot == float(s))
            d4 = jnp.where(lane == s, jnp.sum(jnp.where(pick, dest_e, 0.0), axis=-1, keepdims=True), d4)
            g4 = jnp.where(lane == s, jnp.sum(jnp.where(pick, g, 0.0), axis=-1, keepdims=True), g4)
        dest_ref[rows, :] = d4.astype(I32)
        g4_ref[rows, :] = g4
        return 0

    lax.fori_loop(0, nt, pass2, 0)


def _routing(sel, gate, tile=256):
    T, E = sel.shape
    return pl.pallas_call(
        functools.partial(_routing_kernel, tile=tile),
        out_shape=[jax.ShapeDtypeStruct((T, 128), I32), jax.ShapeDtypeStruct((T, 128), F32),
                   jax.ShapeDtypeStruct((1, E), I32), jax.ShapeDtypeStruct((1, E), I32)],
        scratch_shapes=[pltpu.VMEM((T, E), F32)],
        compiler_params=pltpu.CompilerParams(vmem_limit_bytes=VMEM_LIMIT_BYTES),
        name="routing_ranks",
    )(sel, gate)


def _rowtok_kernel(dest_ref, out_ref, *, n_tok):
    def zero(r, _):
        out_ref[r] = 0
        return 0

    lax.fori_loop(0, out_ref.shape[0], zero, 0, unroll=8)

    def body(t, _):
        for k in range(TOP_K):
            out_ref[dest_ref[t * TOP_K + k]] = t
        return 0

    lax.fori_loop(0, n_tok, body, 0, unroll=8)


def _row_tokens(dest_flat, n_rows, n_tok):
    smem = pl.BlockSpec(memory_space=pltpu.SMEM)
    return pl.pallas_call(
        functools.partial(_rowtok_kernel, n_tok=n_tok),
        in_specs=[smem],
        out_specs=smem,
        out_shape=jax.ShapeDtypeStruct((n_rows,), I32),
        name="row_tokens",
    )(dest_flat)


def _expert_kernel(ie_ref, ib_ref, ins_ref, rt_ref,
                   h2_hbm, wup_ref, bup_ref, wdn_ref, bdn_ref, perm_ref, ys_hbm,
                   xs_ref, acc_ref, gsem, osem, *, n_ff_tiles):
    i = pl.program_id(0)
    j = pl.program_id(1)
    nsub = ins_ref[i]
    blk0 = ib_ref[i]
    D = xs_ref.shape[1]

    def gather_wait(s):
        pltpu.make_async_copy(h2_hbm.at[pl.ds(0, ROW_BLOCK)],
                              xs_ref.at[pl.ds(s * ROW_BLOCK, ROW_BLOCK)], gsem).wait()

    @pl.when((j == 0) & (nsub > 0))
    def _():
        row0 = blk0 * ROW_BLOCK

        def issue(r, _):
            tok = rt_ref[row0 + r]
            pltpu.make_async_copy(h2_hbm.at[pl.ds(tok, 1)], xs_ref.at[pl.ds(r, 1)], gsem).start()
            return 0

        lax.fori_loop(0, nsub * ROW_BLOCK, issue, 0)
        acc_ref[...] = jnp.zeros_like(acc_ref)
        for s in range(ITEM_BLOCKS):
            @pl.when(s < nsub)
            def _():
                gather_wait(s)

    @pl.when(nsub > 0)
    def _():
        wup = wup_ref[0].astype(BF16)
        wdn = wdn_ref[0].astype(BF16)
        bup = bup_ref[0]
        perm = perm_ref[...]
        half = perm.shape[0] // 2
        for s in range(ITEM_BLOCKS):
            @pl.when(s < nsub)
            def _():
                rows = slice(s * ROW_BLOCK, (s + 1) * ROW_BLOCK)
                xb = xs_ref[rows, :].astype(BF16)
                gu = (jnp.dot(xb, wup, preferred_element_type=F32) + bup).astype(BF16)
                glu_parts, lin_parts = [], []
                for p in range(gu.shape[1] // perm.shape[0]):
                    gp = jnp.dot(gu[:, p * perm.shape[0]:(p + 1) * perm.shape[0]], perm,
                                 preferred_element_type=F32)
                    glu_parts.append(gp[:, :half])
                    lin_parts.append(gp[:, half:])
                x_glu = jnp.minimum(jnp.concatenate(glu_parts, axis=1), SWIGLU_LIMIT)
                x_lin = jnp.clip(jnp.concatenate(lin_parts, axis=1), -SWIGLU_LIMIT, SWIGLU_LIMIT)
                act = x_glu * jax.nn.sigmoid(SWIGLU_ALPHA * x_glu) * (x_lin + 1.0)
                acc_ref[rows, :] += jnp.dot(act.astype(BF16), wdn, preferred_element_type=F32)

    @pl.when((j == n_ff_tiles - 1) & (nsub > 0))
    def _():
        def out_copy(s):
            return pltpu.make_async_copy(
                acc_ref.at[pl.ds(s * ROW_BLOCK, ROW_BLOCK)],
                ys_hbm.at[pl.ds(pl.multiple_of((blk0 + s) * ROW_BLOCK, ROW_BLOCK), ROW_BLOCK)], osem)

        for s in range(ITEM_BLOCKS):
            @pl.when(s < nsub)
            def _():
                rows = slice(s * ROW_BLOCK, (s + 1) * ROW_BLOCK)
                acc_ref[rows, :] += bdn_ref[0]
                out_copy(s).start()
        for s in range(ITEM_BLOCKS):
            @pl.when(s < nsub)
            def _():
                out_copy(s).wait()


def _experts(h2, row_tok, item_e, item_b, item_n, w_up, b_up, w_down, b_down, n_rows):
    T, D = h2.shape
    E, _, F2 = w_up.shape
    F = F2 // 2
    J = F // FF_TILE
    n_items = item_e.shape[0]
    half = 128
    perm = np.zeros((2 * half, 2 * half), np.float32)
    perm[2 * np.arange(half), np.arange(half)] = 1.0
    perm[2 * np.arange(half) + 1, half + np.arange(half)] = 1.0

    def jj(i, j, ins):
        return jnp.where(ins[i] > 0, j, J - 1)

    grid_spec = pltpu.PrefetchScalarGridSpec(
        num_scalar_prefetch=4,
        grid=(n_items, J),
        in_specs=[pl.BlockSpec(memory_space=pl.ANY),
                  pl.BlockSpec((1, D, 2 * FF_TILE), lambda i, j, ie, ib, ins, rt: (ie[i], 0, jj(i, j, ins))),
                  pl.BlockSpec((1, 1, 2 * FF_TILE), lambda i, j, ie, ib, ins, rt: (ie[i], 0, jj(i, j, ins))),
                  pl.BlockSpec((1, FF_TILE, D), lambda i, j, ie, ib, ins, rt: (ie[i], jj(i, j, ins), 0)),
                  pl.BlockSpec((1, 1, D), lambda i, j, ie, ib, ins, rt: (ie[i], 0, 0)),
                  pl.BlockSpec((2 * half, 2 * half), lambda i, j, ie, ib, ins, rt: (0, 0))],
        out_specs=pl.BlockSpec(memory_space=pl.ANY),
        scratch_shapes=[pltpu.VMEM((ITEM_BLOCKS * ROW_BLOCK, D), F32),
                        pltpu.VMEM((ITEM_BLOCKS * ROW_BLOCK, D), F32),
                        pltpu.SemaphoreType.DMA(()), pltpu.SemaphoreType.DMA(())],
    )
    return pl.pallas_call(
        functools.partial(_expert_kernel, n_ff_tiles=J),
        grid_spec=grid_spec,
        out_shape=jax.ShapeDtypeStruct((n_rows, D), F32),
        compiler_params=_cparams(("arbitrary", "arbitrary")),
        name="expert_ffn",
    )(item_e, item_b, item_n, row_tok,
      h2, w_up, b_up.reshape(E, 1, F2), w_down, b_down.reshape(E, 1, D), jnp.asarray(perm, BF16))


def _combine_kernel(dest_ref, ys_hbm, g4_ref, x1_ref, g_ref, o_ref, buf_ref, sem, *, tc):
    i = pl.program_id(0)

    def issue(t, _):
        base = (i * tc + t) * TOP_K
        for k in range(TOP_K):
            pltpu.make_async_copy(ys_hbm.at[pl.ds(dest_ref[base + k], 1)],
                                  buf_ref.at[k, pl.ds(t, 1)], sem).start()
        return 0

    lax.fori_loop(0, tc, issue, 0)
    for k in range(TOP_K):
        pltpu.make_async_copy(ys_hbm.at[pl.ds(0, tc)], buf_ref.at[k], sem).wait()
    y = x1_ref[...]
    g4 = g4_ref[...]
    for k in range(TOP_K):
        y = y + g4[:, k:k + 1] * buf_ref[k]
    ms = jnp.mean(y * y, axis=-1, keepdims=True)
    o_ref[...] = y * lax.rsqrt(ms + NORM_EPS) * g_ref[...]


def _combine(dest_flat, ys, g4, x1, g_final, tc=128):
    T, D = x1.shape
    grid_spec = pltpu.PrefetchScalarGridSpec(
        num_scalar_prefetch=1,
        grid=(T // tc,),
        in_specs=[pl.BlockSpec(memory_space=pl.ANY),
                  pl.BlockSpec((tc, 128), lambda i, d: (i, 0)),
                  pl.BlockSpec((tc, D), lambda i, d: (i, 0)),
                  pl.BlockSpec((1, D), lambda i, d: (0, 0))],
        out_specs=pl.BlockSpec((tc, D), lambda i, d: (i, 0)),
        scratch_shapes=[pltpu.VMEM((TOP_K, tc, D), F32), pltpu.SemaphoreType.DMA(())],
    )
    return pl.pallas_call(
        functools.partial(_combine_kernel, tc=tc),
        grid_spec=grid_spec,
        out_shape=jax.ShapeDtypeStruct((T, D), F32),
        compiler_params=_cparams(("arbitrary",)),
        name="combine_norm",
    )(dest_flat, ys, g4, x1, g_final.reshape(1, D))


def _work_items(nblk, start_blk, n_items):
    E = nblk.shape[0]
    per_e = (nblk + ITEM_BLOCKS - 1) // ITEM_BLOCKS
    ends = jnp.cumsum(per_e)
    total = ends[-1]
    idx = jnp.arange(n_items, dtype=I32)
    e = jnp.minimum(jnp.searchsorted(ends, idx, side="right"), E - 1).astype(I32)
    local = idx - (ends[e] - per_e[e])
    active = idx < total
    last_e = e[jnp.maximum(total - 1, 0)]
    item_e = jnp.where(active, e, last_e).astype(I32)
    item_b = jnp.where(active, start_blk[e] + local * ITEM_BLOCKS, 0).astype(I32)
    item_n = jnp.where(active, jnp.clip(nblk[e] - local * ITEM_BLOCKS, 0, ITEM_BLOCKS), 0).astype(I32)
    return item_e, item_b, item_n


def kernel(x, g_mix, w_in, b_forget, g_v_ln, b_v_ln, w_spatial, b_spatial, w_branch_attn, w_branch_gmlp, w_out, g_ffn, w_router, b_router, w_expert_up, b_expert_up, w_expert_down, b_expert_down, g_final):
    B, S, D = x.shape
    T = B * S
    n_heads = b_forget.shape[0]
    attn_w = n_heads * HEAD_DIM
    gmlp_w = g_v_ln.shape[0]
    E = w_router.shape[1]
    off_f = 3 * attn_w
    off_z = off_f + n_heads
    off_g = off_z + 2 * gmlp_w

    x2 = x.reshape(T, D)
    wft = w_in[:, off_f:off_z].T.astype(BF16)
    wz = w_in[:, off_z:off_g].astype(BF16)
    wg = w_in[:, off_g:].astype(BF16)
    wa = w_branch_attn.astype(BF16)
    wb = w_branch_gmlp.astype(BF16)
    wo = w_out.astype(BF16)

    h = _rmsnorm(x2, g_mix, BF16)
    qkv = _project(h, w_in, 3 * attn_w, BF16)
    c_row = _forget_cumsum(h, wft, b_forget, B, S)
    attn = _attention(qkv, c_row, B, S, n_heads)
    sg = _gmlp(h, wz, g_v_ln, b_v_ln, w_spatial, b_spatial)
    merged = _merge(attn, sg, h, wa, wb, wg)
    x1, h2, sel, gate = _out_router(merged, x2, wo, g_ffn, w_router, b_router)

    dest4, g4, nblk, start_blk = _routing(sel, gate)
    n_rows = T * TOP_K + E * ROW_BLOCK
    dest_flat = dest4[:, :TOP_K].reshape(T * TOP_K)
    row_tok = _row_tokens(dest_flat, n_rows, T)
    max_blocks = n_rows // ROW_BLOCK
    n_items = E + -(-(max_blocks - E) // ITEM_BLOCKS) + 1
    item_e, item_b, item_n = _work_items(nblk[0], start_blk[0], n_items)
    ys = _experts(h2, row_tok, item_e, item_b, item_n,
                  w_expert_up, b_expert_up, w_expert_down, b_expert_down, n_rows)
    out = _combine(dest_flat, ys, g4, x1, g_final)
    return out.reshape(B, S, D)
```

```python
import functools
import math

import jax
import jax.numpy as jnp
import numpy as np
from jax import lax
from jax.experimental import pallas as pl
from jax.experimental.pallas import tpu as pltpu

F32 = jnp.float32
BF16 = jnp.bfloat16
I32 = jnp.int32

NORM_EPS = 1e-5
HEAD_DIM = 128
CHUNK = 128
GROUP_DIM = 128
TOP_K = 4
SWIGLU_ALPHA = 1.702
SWIGLU_LIMIT = 7.0
LOG2E = math.log2(math.e)

VMEM_LIMIT_BYTES = 56 * 1024 * 1024

ROW_BLOCK = 256
ITEM_BLOCKS = 6
FF_TILE = 256


def _cparams(sem, **kw):
    return pltpu.CompilerParams(dimension_semantics=sem, vmem_limit_bytes=VMEM_LIMIT_BYTES, **kw)


def _rmsnorm_kernel(x_ref, g_ref, o_ref):
    x = x_ref[...]
    ms = jnp.mean(x * x, axis=-1, keepdims=True)
    o_ref[...] = (x * lax.rsqrt(ms + NORM_EPS) * g_ref[...]).astype(o_ref.dtype)


def _rmsnorm(x, g, out_dtype, tm=512):
    T, D = x.shape
    return pl.pallas_call(
        _rmsnorm_kernel,
        grid=(T // tm,),
        in_specs=[pl.BlockSpec((tm, D), lambda i: (i, 0)), pl.BlockSpec((1, D), lambda i: (0, 0))],
        out_specs=pl.BlockSpec((tm, D), lambda i: (i, 0)),
        out_shape=jax.ShapeDtypeStruct((T, D), out_dtype),
        compiler_params=_cparams(("parallel",)),
        name="rmsnorm",
    )(x, g.reshape(1, D))


def _proj_kernel(h_ref, w_ref, o_ref, wbf_ref, *, n_scaled, scale):
    j = pl.program_id(0)

    @pl.when(pl.program_id(1) == 0)
    def _():
        wbf_ref[...] = w_ref[...].astype(BF16)

    acc = jnp.dot(h_ref[...], wbf_ref[...], preferred_element_type=F32)
    o_ref[...] = (acc * jnp.where(j < n_scaled, scale, 1.0)).astype(o_ref.dtype)


def _project(h, w, n_cols, n_scaled_cols, scale, out_dtype, tm=1024, tn=512):
    T, D = h.shape
    return pl.pallas_call(
        functools.partial(_proj_kernel, n_scaled=n_scaled_cols // tn, scale=scale),
        grid=(n_cols // tn, T // tm),
        in_specs=[pl.BlockSpec((tm, D), lambda j, i: (i, 0)),
                  pl.BlockSpec((D, tn), lambda j, i: (0, j))],
        out_specs=pl.BlockSpec((tm, tn), lambda j, i: (i, j)),
        out_shape=jax.ShapeDtypeStruct((T, n_cols), out_dtype),
        scratch_shapes=[pltpu.VMEM((D, tn), BF16)],
        compiler_params=_cparams(("arbitrary", "arbitrary")),
        name="qkv_proj",
    )(h, w)


def _forget_kernel(h_ref, wft_ref, bf_ref, c_ref):
    ft = lax.dot_general(wft_ref[...], h_ref[...], (((1,), (1,)), ((), ())),
                         preferred_element_type=F32)
    c = jax.nn.log_sigmoid(ft + bf_ref[...])
    S = c.shape[1]
    lane = lax.broadcasted_iota(I32, c.shape, 1)
    shift = 1
    while shift < S:
        c = c + jnp.where(lane >= shift, pltpu.roll(c, shift, axis=1), 0.0)
        shift *= 2
    c_ref[0] = c * LOG2E


def _forget_cumsum(h, wft, b_forget, B, S):
    T, D = h.shape
    H = wft.shape[0]
    return pl.pallas_call(
        _forget_kernel,
        grid=(B,),
        in_specs=[pl.BlockSpec((S, D), lambda b: (b, 0)),
                  pl.BlockSpec((H, D), lambda b: (0, 0)),
                  pl.BlockSpec((H, 1), lambda b: (0, 0))],
        out_specs=pl.BlockSpec((1, H, S), lambda b: (b, 0, 0)),
        out_shape=jax.ShapeDtypeStruct((B, H, S), F32),
        compiler_params=_cparams(("parallel",)),
        name="forget_cumsum",
    )(h, wft, b_forget.reshape(H, 1))


def _attn_kernel(q_ref, k_ref, v_ref, crow_ref, o_ref, vaug_ref, m_ref, acc_ref, *, n_heads, tq):
    i = pl.program_id(1)

    @pl.when(i == 0)
    def _():
        ones = jnp.ones((v_ref.shape[0], HEAD_DIM), BF16)
        for h in range(n_heads):
            vaug_ref[h, :, :HEAD_DIM] = v_ref[:, h * HEAD_DIM:(h + 1) * HEAD_DIM]
            vaug_ref[h, :, HEAD_DIM:] = ones

    m_ref[...] = jnp.full(m_ref.shape, -jnp.inf, F32)
    acc_ref[...] = jnp.zeros(acc_ref.shape, F32)
    row = lax.broadcasted_iota(I32, (tq, tq), 0)
    col = lax.broadcasted_iota(I32, (tq, tq), 1)
    causal = col <= row

    def step(j, masked):
        keys = pl.ds(pl.multiple_of(j * tq, tq), tq)
        for h in range(n_heads):
            hs = slice(h * HEAD_DIM, (h + 1) * HEAD_DIM)
            s = lax.dot_general(q_ref[:, hs], k_ref[keys, hs], (((1,), (1,)), ((), ())),
                                preferred_element_type=F32) - crow_ref[0, h, j]
            if masked:
                s = jnp.where(causal, s, -jnp.inf)
            m_old = m_ref[h]
            m_new = jnp.maximum(m_old, jnp.max(s, axis=-1, keepdims=True))
            alpha = jnp.exp2(m_old - m_new)
            p = jnp.exp2(s - jnp.concatenate([m_new] * (tq // HEAD_DIM), axis=1))
            m_ref[h] = m_new
            pv = jnp.dot(p.astype(BF16), vaug_ref[h, keys, :], preferred_element_type=F32)
            acc_ref[h] = jnp.concatenate([alpha, alpha], axis=1) * acc_ref[h] + pv

    def body(j, _):
        step(j, False)
        return 0

    lax.fori_loop(0, i, body, 0)
    step(i, True)
    for h in range(n_heads):
        acc = acc_ref[h]
        o_ref[:, h * HEAD_DIM:(h + 1) * HEAD_DIM] = (acc[:, :HEAD_DIM] / acc[:, HEAD_DIM:]).astype(o_ref.dtype)


def _attention(qkv, c_row, B, S, n_heads, tq=256):
    T = qkv.shape[0]
    W = n_heads * HEAD_DIM
    nq = S // tq
    c_row5 = c_row.reshape(B, n_heads, nq, 1, tq)
    return pl.pallas_call(
        functools.partial(_attn_kernel, n_heads=n_heads, tq=tq),
        grid=(B, nq),
        in_specs=[pl.BlockSpec((tq, W), lambda b, i: (b * nq + i, 0)),
                  pl.BlockSpec((S, W), lambda b, i: (b, 1)),
                  pl.BlockSpec((S, W), lambda b, i: (b, 2)),
                  pl.BlockSpec((1, n_heads, nq, 1, tq), lambda b, i: (b, 0, 0, 0, 0))],
        out_specs=pl.BlockSpec((tq, W), lambda b, i: (b * nq + i, 0)),
        out_shape=jax.ShapeDtypeStruct((T, W), BF16),
        scratch_shapes=[pltpu.VMEM((n_heads, S, 2 * HEAD_DIM), BF16),
                        pltpu.VMEM((n_heads, tq, HEAD_DIM), F32),
                        pltpu.VMEM((n_heads, tq, 2 * HEAD_DIM), F32)],
        compiler_params=_cparams(("arbitrary", "arbitrary")),
        name="fox_attention",
    )(qkv, qkv, qkv, c_row5)


def _gmlp_kernel(h_ref, wz_ref, g_ref, b_ref, ws_ref, bst_ref, o_ref, *, n_groups):
    z = jnp.dot(h_ref[...], wz_ref[...], preferred_element_type=F32)
    z = 0.5 * z * (1.0 + lax.erf(z * (1.0 / math.sqrt(2.0))))
    W = z.shape[1] // 2
    u = z[:, :W]
    v = z[:, W:]
    mu = jnp.mean(v, axis=-1, keepdims=True)
    var = jnp.mean(jnp.square(v - mu), axis=-1, keepdims=True)
    vn = (v - mu) * lax.rsqrt(var + NORM_EPS) * g_ref[...] + b_ref[...]
    row = lax.broadcasted_iota(I32, (CHUNK, CHUNK), 0)
    col = lax.broadcasted_iota(I32, (CHUNK, CHUNK), 1)
    tril = col <= row
    tg = z.shape[0]
    for g in range(n_groups):
        gs = slice(g * GROUP_DIM, (g + 1) * GROUP_DIM)
        wg = jnp.where(tril, ws_ref[g], 0.0).astype(BF16)
        bias = bst_ref[:, g:g + 1]
        for c in range(tg // CHUNK):
            cs = slice(c * CHUNK, (c + 1) * CHUNK)
            mixed = jnp.dot(wg, vn[cs, gs].astype(BF16), preferred_element_type=F32) + bias
            o_ref[cs, gs] = (u[cs, gs] * mixed).astype(o_ref.dtype)


def _gmlp(h, wz, g_v_ln, b_v_ln, w_spatial, b_spatial, tg=512):
    T, D = h.shape
    W2 = wz.shape[1]
    W = W2 // 2
    G = w_spatial.shape[0]
    return pl.pallas_call(
        functools.partial(_gmlp_kernel, n_groups=G),
        grid=(T // tg,),
        in_specs=[pl.BlockSpec((tg, D), lambda i: (i, 0)),
                  pl.BlockSpec((D, W2), lambda i: (0, 0)),
                  pl.BlockSpec((1, W), lambda i: (0, 0)),
                  pl.BlockSpec((1, W), lambda i: (0, 0)),
                  pl.BlockSpec((G, CHUNK, CHUNK), lambda i: (0, 0, 0)),
                  pl.BlockSpec((CHUNK, G), lambda i: (0, 0))],
        out_specs=pl.BlockSpec((tg, W), lambda i: (i, 0)),
        out_shape=jax.ShapeDtypeStruct((T, W), BF16),
        compiler_params=_cparams(("parallel",)),
        name="gmlp",
    )(h, wz, g_v_ln.reshape(1, W), b_v_ln.reshape(1, W), w_spatial, b_spatial.T)


def _merge_kernel(attn_ref, sg_ref, h_ref, wa_ref, wb_ref, wga_ref, wgb_ref, o_ref):
    h = h_ref[...]
    a = jnp.dot(attn_ref[...], wa_ref[...], preferred_element_type=F32)
    ga = jnp.dot(h, wga_ref[...], preferred_element_type=F32)
    m = jax.nn.sigmoid(ga) * a
    b = jnp.dot(sg_ref[...], wb_ref[...], preferred_element_type=F32)
    gb = jnp.dot(h, wgb_ref[...], preferred_element_type=F32)
    o_ref[...] = (m + jax.nn.sigmoid(gb) * b).astype(o_ref.dtype)


def _merge(attn, sg, h, wa, wb, wg, tm=512, tn=512):
    T, D = h.shape
    Wa = attn.shape[1]
    Wb = sg.shape[1]
    nt = D // tn
    return pl.pallas_call(
        _merge_kernel,
        grid=(nt, T // tm),
        in_specs=[pl.BlockSpec((tm, Wa), lambda j, i: (i, 0)),
                  pl.BlockSpec((tm, Wb), lambda j, i: (i, 0)),
                  pl.BlockSpec((tm, D), lambda j, i: (i, 0)),
                  pl.BlockSpec((Wa, tn), lambda j, i: (0, j)),
                  pl.BlockSpec((Wb, tn), lambda j, i: (0, j)),
                  pl.BlockSpec((D, tn), lambda j, i: (0, j)),
                  pl.BlockSpec((D, tn), lambda j, i: (0, j + nt))],
        out_specs=pl.BlockSpec((tm, tn), lambda j, i: (i, j)),
        out_shape=jax.ShapeDtypeStruct((T, D), BF16),
        compiler_params=_cparams(("arbitrary", "arbitrary")),
        name="gated_merge",
    )(attn, sg, h, wa, wb, wg, wg)


def _out_router_kernel(m_ref, x_ref, wo_ref, g_ref, wr_ref, br_ref,
                       x1_ref, h2_ref, sel_ref, gate_ref):
    x1 = x_ref[...] + jnp.dot(m_ref[...], wo_ref[...], preferred_element_type=F32)
    x1_ref[...] = x1
    ms = jnp.mean(x1 * x1, axis=-1, keepdims=True)
    h2 = x1 * lax.rsqrt(ms + NORM_EPS) * g_ref[...]
    h2_ref[...] = h2
    logits = jnp.dot(h2, wr_ref[...], preferred_element_type=F32,
                     precision=lax.Precision.HIGHEST) + br_ref[...]
    E = logits.shape[1]
    lane = lax.broadcasted_iota(I32, logits.shape, 1)
    work = logits
    sel = jnp.zeros(logits.shape, F32)
    num = jnp.zeros(logits.shape, F32)
    denom = jnp.zeros((logits.shape[0], 1), F32)
    m0 = None
    for _ in range(TOP_K):
        m = jnp.max(work, axis=-1, keepdims=True)
        idx = jnp.min(jnp.where(work == m, lane, E), axis=-1, keepdims=True)
        onehot = lane == idx
        if m0 is None:
            m0 = m
        e = jnp.exp(m - m0)
        sel = jnp.where(onehot, 1.0, sel)
        num = jnp.where(onehot, e, num)
        denom = denom + e
        work = jnp.where(onehot, -jnp.inf, work)
    sel_ref[...] = sel
    gate_ref[...] = num / denom


def _out_router(merged, x, wo, g_ffn, w_router, b_router, to=256):
    T, D = x.shape
    E = w_router.shape[1]
    row = lambda i: (i, 0)
    fixed = lambda i: (0, 0)
    return pl.pallas_call(
        _out_router_kernel,
        grid=(T // to,),
        in_specs=[pl.BlockSpec((to, D), row), pl.BlockSpec((to, D), row),
                  pl.BlockSpec((D, D), fixed), pl.BlockSpec((1, D), fixed),
                  pl.BlockSpec((D, E), fixed), pl.BlockSpec((1, E), fixed)],
        out_specs=[pl.BlockSpec((to, D), row), pl.BlockSpec((to, D), row),
                   pl.BlockSpec((to, E), row), pl.BlockSpec((to, E), row)],
        out_shape=[jax.ShapeDtypeStruct((T, D), F32), jax.ShapeDtypeStruct((T, D), F32),
                   jax.ShapeDtypeStruct((T, E), F32), jax.ShapeDtypeStruct((T, E), F32)],
        compiler_params=_cparams(("parallel",)),
        name="out_router",
    )(merged, x, wo, g_ffn.reshape(1, D), w_router, b_router.reshape(1, E))


def _routing_kernel(sel_ref, gate_ref, dest_ref, g4_ref, nblk_ref, start_ref, rank_ref, *, tile):
    T, E = sel_ref.shape
    nt = T // tile
    r = lax.broadcasted_iota(I32, (tile, tile), 0)
    c = lax.broadcasted_iota(I32, (tile, tile), 1)
    strict_lower = (c < r).astype(BF16)
    er = lax.broadcasted_iota(I32, (E, E), 0)
    ec = lax.broadcasted_iota(I32, (E, E), 1)
    strict_upper = (er < ec).astype(BF16)

    def pass1(t, carry):
        rows = pl.ds(pl.multiple_of(t * tile, tile), tile)
        a = sel_ref[rows, :]
        rank_ref[rows, :] = jnp.dot(strict_lower, a.astype(BF16), preferred_element_type=F32) + carry
        return carry + jnp.sum(a, axis=0, keepdims=True)

    counts = lax.fori_loop(0, nt, pass1, jnp.zeros((1, E), F32))
    nblk = jnp.floor((counts + (ROW_BLOCK - 1)) * (1.0 / ROW_BLOCK))
    start_blk = jnp.dot(nblk.astype(BF16), strict_upper, preferred_element_type=F32)
    nblk_ref[...] = nblk.astype(I32)
    start_ref[...] = start_blk.astype(I32)
    start_row = start_blk * float(ROW_BLOCK)
    lane = lax.broadcasted_iota(I32, (tile, 128), 1)

    def pass2(t, _):
        rows = pl.ds(pl.multiple_of(t * tile, tile), tile)
        a = sel_ref[rows, :]
        g = gate_ref[rows, :]
        dest_e = rank_ref[rows, :] + start_row
        slot = jnp.dot(a.astype(BF16), strict_upper, preferred_element_type=F32)
        d4 = jnp.zeros((tile, 128), F32)
        g4 = jnp.zeros((tile, 128), F32)
        for s in range(TOP_K):
            pick = (a > 0.5) & (slot == float(s))
            d4 = jnp.where(lane == s, jnp.sum(jnp.where(pick, dest_e, 0.0), axis=-1, keepdims=True), d4)
            g4 = jnp.where(lane == s, jnp.sum(jnp.where(pick, g, 0.0), axis=-1, keepdims=True), g4)
        dest_ref[rows, :] = d4.astype(I32)
        g4_ref[rows, :] = g4
        return 0

    lax.fori_loop(0, nt, pass2, 0)


def _routing(sel, gate, tile=256):
    T, E = sel.shape
    return pl.pallas_call(
        functools.partial(_routing_kernel, tile=tile),
        out_shape=[jax.ShapeDtypeStruct((T, 128), I32), jax.ShapeDtypeStruct((T, 128), F32),
                   jax.ShapeDtypeStruct((1, E), I32), jax.ShapeDtypeStruct((1, E), I32)],
        scratch_shapes=[pltpu.VMEM((T, E), F32)],
        compiler_params=pltpu.CompilerParams(vmem_limit_bytes=VMEM_LIMIT_BYTES),
        name="routing_ranks",
    )(sel, gate)


def _rowtok_kernel(dest_ref, out_ref, *, n_tok):
    def zero(r, _):
        out_ref[r] = 0
        return 0

    lax.fori_loop(0, out_ref.shape[0], zero, 0, unroll=8)

    def body(t, _):
        for k in range(TOP_K):
            out_ref[dest_ref[t * TOP_K + k]] = t
        return 0

    lax.fori_loop(0, n_tok, body, 0, unroll=8)


def _row_tokens(dest_flat, n_rows, n_tok):
    smem = pl.BlockSpec(memory_space=pltpu.SMEM)
    return pl.pallas_call(
        functools.partial(_rowtok_kernel, n_tok=n_tok),
        in_specs=[smem],
        out_specs=smem,
        out_shape=jax.ShapeDtypeStruct((n_rows,), I32),
        name="row_tokens",
    )(dest_flat)


def _expert_kernel(ie_ref, ib_ref, ins_ref, rt_ref,
                   h2_hbm, wup_ref, bup_ref, wdn_ref, bdn_ref, perm_ref, ys_hbm,
                   xs_ref, acc_ref, gsem, osem, *, n_ff_tiles):
    i = pl.program_id(0)
    j = pl.program_id(1)
    nsub = ins_ref[i]
    blk0 = ib_ref[i]
    D = xs_ref.shape[1]

    def gather_wait(s):
        pltpu.make_async_copy(h2_hbm.at[pl.ds(0, ROW_BLOCK)],
                              xs_ref.at[pl.ds(s * ROW_BLOCK, ROW_BLOCK)], gsem).wait()

    @pl.when((j == 0) & (nsub > 0))
    def _():
        row0 = blk0 * ROW_BLOCK

        def issue(r, _):
            tok = rt_ref[row0 + r]
            pltpu.make_async_copy(h2_hbm.at[pl.ds(tok, 1)], xs_ref.at[pl.ds(r, 1)], gsem).start()
            return 0

        lax.fori_loop(0, nsub * ROW_BLOCK, issue, 0)
        acc_ref[...] = jnp.zeros_like(acc_ref)
        for s in range(ITEM_BLOCKS):
            @pl.when(s < nsub)
            def _():
                gather_wait(s)

    @pl.when(nsub > 0)
    def _():
        wup = wup_ref[0].astype(BF16)
        wdn = wdn_ref[0].astype(BF16)
        bup = bup_ref[0]
        perm = perm_ref[...]
        half = perm.shape[0] // 2
        for s in range(ITEM_BLOCKS):
            @pl.when(s < nsub)
            def _():
                rows = slice(s * ROW_BLOCK, (s + 1) * ROW_BLOCK)
                xb = xs_ref[rows, :].astype(BF16)
                gu = (jnp.dot(xb, wup, preferred_element_type=F32) + bup).astype(BF16)
                glu_parts, lin_parts = [], []
                for p in range(gu.shape[1] // perm.shape[0]):
                    gp = jnp.dot(gu[:, p * perm.shape[0]:(p + 1) * perm.shape[0]], perm,
                                 preferred_element_type=F32)
                    glu_parts.append(gp[:, :half])
                    lin_parts.append(gp[:, half:])
                x_glu = jnp.minimum(jnp.concatenate(glu_parts, axis=1), SWIGLU_LIMIT)
                x_lin = jnp.clip(jnp.concatenate(lin_parts, axis=1), -SWIGLU_LIMIT, SWIGLU_LIMIT)
                act = x_glu * jax.nn.sigmoid(SWIGLU_ALPHA * x_glu) * (x_lin + 1.0)
                acc_ref[rows, :] += jnp.dot(act.astype(BF16), wdn, preferred_element_type=F32)

    @pl.when((j == n_ff_tiles - 1) & (nsub > 0))
    def _():
        def out_copy(s):
            return pltpu.make_async_copy(
                acc_ref.at[pl.ds(s * ROW_BLOCK, ROW_BLOCK)],
                ys_hbm.at[pl.ds(pl.multiple_of((blk0 + s) * ROW_BLOCK, ROW_BLOCK), ROW_BLOCK)], osem)

        for s in range(ITEM_BLOCKS):
            @pl.when(s < nsub)
            def _():
                rows = slice(s * ROW_BLOCK, (s + 1) * ROW_BLOCK)
                acc_ref[rows, :] += bdn_ref[0]
                out_copy(s).start()
        for s in range(ITEM_BLOCKS):
            @pl.when(s < nsub)
            def _():
                out_copy(s).wait()


def _experts(h2, row_tok, item_e, item_b, item_n, w_up, b_up, w_down, b_down, n_rows):
    T, D = h2.shape
    E, _, F2 = w_up.shape
    F = F2 // 2
    J = F // FF_TILE
    n_items = item_e.shape[0]
    half = 128
    perm = np.zeros((2 * half, 2 * half), np.float32)
    perm[2 * np.arange(half), np.arange(half)] = 1.0
    perm[2 * np.arange(half) + 1, half + np.arange(half)] = 1.0

    def jj(i, j, ins):
        return jnp.where(ins[i] > 0, j, J - 1)

    grid_spec = pltpu.PrefetchScalarGridSpec(
        num_scalar_prefetch=4,
        grid=(n_items, J),
        in_specs=[pl.BlockSpec(memory_space=pl.ANY),
                  pl.BlockSpec((1, D, 2 * FF_TILE), lambda i, j, ie, ib, ins, rt: (ie[i], 0, jj(i, j, ins))),
                  pl.BlockSpec((1, 1, 2 * FF_TILE), lambda i, j, ie, ib, ins, rt: (ie[i], 0, jj(i, j, ins))),
                  pl.BlockSpec((1, FF_TILE, D), lambda i, j, ie, ib, ins, rt: (ie[i], jj(i, j, ins), 0)),
                  pl.BlockSpec((1, 1, D), lambda i, j, ie, ib, ins, rt: (ie[i], 0, 0)),
                  pl.BlockSpec((2 * half, 2 * half), lambda i, j, ie, ib, ins, rt: (0, 0))],
        out_specs=pl.BlockSpec(memory_space=pl.ANY),
        scratch_shapes=[pltpu.VMEM((ITEM_BLOCKS * ROW_BLOCK, D), F32),
                        pltpu.VMEM((ITEM_BLOCKS * ROW_BLOCK, D), F32),
                        pltpu.SemaphoreType.DMA(()), pltpu.SemaphoreType.DMA(())],
    )
    return pl.pallas_call(
        functools.partial(_expert_kernel, n_ff_tiles=J),
        grid_spec=grid_spec,
        out_shape=jax.ShapeDtypeStruct((n_rows, D), F32),
        compiler_params=_cparams(("arbitrary", "arbitrary")),
        name="expert_ffn",
    )(item_e, item_b, item_n, row_tok,
      h2, w_up, b_up.reshape(E, 1, F2), w_down, b_down.reshape(E, 1, D), jnp.asarray(perm, BF16))


def _combine_kernel(dest_ref, ys_hbm, g4_ref, x1_ref, g_ref, o_ref, buf_ref, sem, *, tc):
    i = pl.program_id(0)

    def issue(t, _):
        base = (i * tc + t) * TOP_K
        for k in range(TOP_K):
            pltpu.make_async_copy(ys_hbm.at[pl.ds(dest_ref[base + k], 1)],
                                  buf_ref.at[k, pl.ds(t, 1)], sem).start()
        return 0

    lax.fori_loop(0, tc, issue, 0)
    for k in range(TOP_K):
        pltpu.make_async_copy(ys_hbm.at[pl.ds(0, tc)], buf_ref.at[k], sem).wait()
    y = x1_ref[...]
    g4 = g4_ref[...]
    for k in range(TOP_K):
        y = y + g4[:, k:k + 1] * buf_ref[k]
    ms = jnp.mean(y * y, axis=-1, keepdims=True)
    o_ref[...] = y * lax.rsqrt(ms + NORM_EPS) * g_ref[...]


def _combine(dest_flat, ys, g4, x1, g_final, tc=128):
    T, D = x1.shape
    grid_spec = pltpu.PrefetchScalarGridSpec(
        num_scalar_prefetch=1,
        grid=(T // tc,),
        in_specs=[pl.BlockSpec(memory_space=pl.ANY),
                  pl.BlockSpec((tc, 128), lambda i, d: (i, 0)),
                  pl.BlockSpec((tc, D), lambda i, d: (i, 0)),
                  pl.BlockSpec((1, D), lambda i, d: (0, 0))],
        out_specs=pl.BlockSpec((tc, D), lambda i, d: (i, 0)),
        scratch_shapes=[pltpu.VMEM((TOP_K, tc, D), F32), pltpu.SemaphoreType.DMA(())],
    )
    return pl.pallas_call(
        functools.partial(_combine_kernel, tc=tc),
        grid_spec=grid_spec,
        out_shape=jax.ShapeDtypeStruct((T, D), F32),
        compiler_params=_cparams(("arbitrary",)),
        name="combine_norm",
    )(dest_flat, ys, g4, x1, g_final.reshape(1, D))


def _work_items(nblk, start_blk, n_items):
    E = nblk.shape[0]
    per_e = (nblk + ITEM_BLOCKS - 1) // ITEM_BLOCKS
    ends = jnp.cumsum(per_e)
    total = ends[-1]
    idx = jnp.arange(n_items, dtype=I32)
    e = jnp.minimum(jnp.searchsorted(ends, idx, side="right"), E - 1).astype(I32)
    local = idx - (ends[e] - per_e[e])
    active = idx < total
    last_e = e[jnp.maximum(total - 1, 0)]
    item_e = jnp.where(active, e, last_e).astype(I32)
    item_b = jnp.where(active, start_blk[e] + local * ITEM_BLOCKS, 0).astype(I32)
    item_n = jnp.where(active, jnp.clip(nblk[e] - local * ITEM_BLOCKS, 0, ITEM_BLOCKS), 0).astype(I32)
    return item_e, item_b, item_n


def kernel(x, g_mix, w_in, b_forget, g_v_ln, b_v_ln, w_spatial, b_spatial, w_branch_attn, w_branch_gmlp, w_out, g_ffn, w_router, b_router, w_expert_up, b_expert_up, w_expert_down, b_expert_down, g_final):
    B, S, D = x.shape
    T = B * S
    n_heads = b_forget.shape[0]
    attn_w = n_heads * HEAD_DIM
    gmlp_w = g_v_ln.shape[0]
    E = w_router.shape[1]
    off_f = 3 * attn_w
    off_z = off_f + n_heads
    off_g = off_z + 2 * gmlp_w

    x2 = x.reshape(T, D)
    wft = w_in[:, off_f:off_z].T.astype(BF16)
    wz = w_in[:, off_z:off_g].astype(BF16)
    wg = w_in[:, off_g:].astype(BF16)
    wa = w_branch_attn.astype(BF16)
    wb = w_branch_gmlp.astype(BF16)
    wo = w_out.astype(BF16)

    h = _rmsnorm(x2, g_mix, BF16)
    qkv = _project(h, w_in, 3 * attn_w, attn_w, LOG2E / math.sqrt(HEAD_DIM), BF16)
    c_row = _forget_cumsum(h, wft, b_forget, B, S)
    attn = _attention(qkv, c_row, B, S, n_heads)
    sg = _gmlp(h, wz, g_v_ln, b_v_ln, w_spatial, b_spatial)
    merged = _merge(attn, sg, h, wa, wb, wg)
    x1, h2, sel, gate = _out_router(merged, x2, wo, g_ffn, w_router, b_router)

    dest4, g4, nblk, start_blk = _routing(sel, gate)
    n_rows = T * TOP_K + E * ROW_BLOCK
    dest_flat = dest4[:, :TOP_K].reshape(T * TOP_K)
    row_tok = _row_tokens(dest_flat, n_rows, T)
    max_blocks = n_rows // ROW_BLOCK
    n_items = E + -(-(max_blocks - E) // ITEM_BLOCKS) + 1
    item_e, item_b, item_n = _work_items(nblk[0], start_blk[0], n_items)
    ys = _experts(h2, row_tok, item_e, item_b, item_n,
                  w_expert_up, b_expert_up, w_expert_down, b_expert_down, n_rows)
    out = _combine(dest_flat, ys, g4, x1, g_final)
    return out.reshape(B, S, D)
```

```python
import functools
import math

import jax
import jax.numpy as jnp
import numpy as np
from jax import lax
from jax.experimental import pallas as pl
from jax.experimental.pallas import tpu as pltpu

F32 = jnp.float32
BF16 = jnp.bfloat16
I32 = jnp.int32

NORM_EPS = 1e-5
HEAD_DIM = 128
CHUNK = 128
GROUP_DIM = 128
TOP_K = 4
SWIGLU_ALPHA = 1.702
SWIGLU_LIMIT = 7.0
LOG2E = math.log2(math.e)

VMEM_LIMIT_BYTES = 56 * 1024 * 1024

ROW_BLOCK = 256
ITEM_BLOCKS = 5
ITEM_ROWS = ITEM_BLOCKS * ROW_BLOCK
FF_TILE = 256
DMA_UNROLL = 8


def _cparams(sem, **kw):
    return pltpu.CompilerParams(dimension_semantics=sem, vmem_limit_bytes=VMEM_LIMIT_BYTES, **kw)


def _pack_pairs(x):
    c = x.shape[1] // 2
    hi = lax.bitcast_convert_type(x[:, :c].astype(BF16).astype(F32), jnp.uint32)
    lo = lax.bitcast_convert_type(x[:, c:].astype(BF16).astype(F32), jnp.uint32)
    return hi | (lo >> 16)


def _unpack_pairs(w):
    hi = lax.bitcast_convert_type(w & jnp.uint32(0xFFFF0000), F32)
    lo = lax.bitcast_convert_type(w << 16, F32)
    return jnp.concatenate([hi, lo], axis=1)


def _rmsnorm_kernel(x_ref, g_ref, o_ref):
    x = x_ref[...]
    ms = jnp.mean(x * x, axis=-1, keepdims=True)
    o_ref[...] = (x * lax.rsqrt(ms + NORM_EPS) * g_ref[...]).astype(o_ref.dtype)


def _rmsnorm(x, g, out_dtype, tm=512):
    T, D = x.shape
    return pl.pallas_call(
        _rmsnorm_kernel,
        grid=(T // tm,),
        in_specs=[pl.BlockSpec((tm, D), lambda i: (i, 0)), pl.BlockSpec((1, D), lambda i: (0, 0))],
        out_specs=pl.BlockSpec((tm, D), lambda i: (i, 0)),
        out_shape=jax.ShapeDtypeStruct((T, D), out_dtype),
        compiler_params=_cparams(("parallel",)),
        name="rmsnorm",
    )(x, g.reshape(1, D))


def _proj_kernel(h_ref, w_ref, o_ref, wbf_ref, *, n_scaled, scale):
    j = pl.program_id(0)

    @pl.when(pl.program_id(1) == 0)
    def _():
        wbf_ref[...] = w_ref[...].astype(BF16)

    acc = jnp.dot(h_ref[...], wbf_ref[...], preferred_element_type=F32)
    o_ref[...] = (acc * jnp.where(j < n_scaled, scale, 1.0)).astype(o_ref.dtype)


def _project(h, w, n_cols, n_scaled_cols, scale, out_dtype, tm=1024, tn=512):
    T, D = h.shape
    return pl.pallas_call(
        functools.partial(_proj_kernel, n_scaled=n_scaled_cols // tn, scale=scale),
        grid=(n_cols // tn, T // tm),
        in_specs=[pl.BlockSpec((tm, D), lambda j, i: (i, 0)),
                  pl.BlockSpec((D, tn), lambda j, i: (0, j))],
        out_specs=pl.BlockSpec((tm, tn), lambda j, i: (i, j)),
        out_shape=jax.ShapeDtypeStruct((T, n_cols), out_dtype),
        scratch_shapes=[pltpu.VMEM((D, tn), BF16)],
        compiler_params=_cparams(("arbitrary", "arbitrary")),
        name="qkv_proj",
    )(h, w)


def _forget_kernel(h_ref, wft_ref, bf_ref, c_ref):
    ft = lax.dot_general(wft_ref[...], h_ref[...], (((1,), (1,)), ((), ())),
                         preferred_element_type=F32)
    c = jax.nn.log_sigmoid(ft + bf_ref[...])
    S = c.shape[1]
    lane = lax.broadcasted_iota(I32, c.shape, 1)
    shift = 1
    while shift < S:
        c = c + jnp.where(lane >= shift, pltpu.roll(c, shift, axis=1), 0.0)
        shift *= 2
    c_ref[0] = c * LOG2E


def _forget_cumsum(h, wft, b_forget, B, S):
    T, D = h.shape
    H = wft.shape[0]
    return pl.pallas_call(
        _forget_kernel,
        grid=(B,),
        in_specs=[pl.BlockSpec((S, D), lambda b: (b, 0)),
                  pl.BlockSpec((H, D), lambda b: (0, 0)),
                  pl.BlockSpec((H, 1), lambda b: (0, 0))],
        out_specs=pl.BlockSpec((1, H, S), lambda b: (b, 0, 0)),
        out_shape=jax.ShapeDtypeStruct((B, H, S), F32),
        compiler_params=_cparams(("parallel",)),
        name="forget_cumsum",
    )(h, wft, b_forget.reshape(H, 1))


def _attn_kernel(q_ref, k_ref, v_ref, crow_ref, o_ref, vaug_ref, m_ref, acc_ref, *, n_heads, tq):
    i = pl.program_id(1)

    @pl.when(i == 0)
    def _():
        ones = jnp.ones((v_ref.shape[0], HEAD_DIM), BF16)
        for h in range(n_heads):
            vaug_ref[h, :, :HEAD_DIM] = v_ref[:, h * HEAD_DIM:(h + 1) * HEAD_DIM]
            vaug_ref[h, :, HEAD_DIM:] = ones

    m_ref[...] = jnp.full(m_ref.shape, -jnp.inf, F32)
    acc_ref[...] = jnp.zeros(acc_ref.shape, F32)
    row = lax.broadcasted_iota(I32, (tq, tq), 0)
    col = lax.broadcasted_iota(I32, (tq, tq), 1)
    causal = col <= row

    def step(j, masked):
        keys = pl.ds(pl.multiple_of(j * tq, tq), tq)
        for h in range(n_heads):
            hs = slice(h * HEAD_DIM, (h + 1) * HEAD_DIM)
            s = lax.dot_general(q_ref[:, hs], k_ref[keys, hs], (((1,), (1,)), ((), ())),
                                preferred_element_type=F32) - crow_ref[0, h, j]
            if masked:
                s = jnp.where(causal, s, -jnp.inf)
            m_old = m_ref[h]
            m_new = jnp.maximum(m_old, jnp.max(s, axis=-1, keepdims=True))
            alpha = jnp.exp2(m_old - m_new)
            p = jnp.exp2(s - jnp.concatenate([m_new] * (tq // HEAD_DIM), axis=1))
            m_ref[h] = m_new
            pv = jnp.dot(p.astype(BF16), vaug_ref[h, keys, :], preferred_element_type=F32)
            acc_ref[h] = jnp.concatenate([alpha, alpha], axis=1) * acc_ref[h] + pv

    def body(j, _):
        step(j, False)
        return 0

    lax.fori_loop(0, i, body, 0)
    step(i, True)
    for h in range(n_heads):
        acc = acc_ref[h]
        o_ref[:, h * HEAD_DIM:(h + 1) * HEAD_DIM] = (acc[:, :HEAD_DIM] / acc[:, HEAD_DIM:]).astype(o_ref.dtype)


def _attention(qkv, c_row, B, S, n_heads, tq=256):
    T = qkv.shape[0]
    W = n_heads * HEAD_DIM
    nq = S // tq
    c_row5 = c_row.reshape(B, n_heads, nq, 1, tq)
    return pl.pallas_call(
        functools.partial(_attn_kernel, n_heads=n_heads, tq=tq),
        grid=(B, nq),
        in_specs=[pl.BlockSpec((tq, W), lambda b, i: (b * nq + i, 0)),
                  pl.BlockSpec((S, W), lambda b, i: (b, 1)),
                  pl.BlockSpec((S, W), lambda b, i: (b, 2)),
                  pl.BlockSpec((1, n_heads, nq, 1, tq), lambda b, i: (b, 0, 0, 0, 0))],
        out_specs=pl.BlockSpec((tq, W), lambda b, i: (b * nq + i, 0)),
        out_shape=jax.ShapeDtypeStruct((T, W), BF16),
        scratch_shapes=[pltpu.VMEM((n_heads, S, 2 * HEAD_DIM), BF16),
                        pltpu.VMEM((n_heads, tq, HEAD_DIM), F32),
                        pltpu.VMEM((n_heads, tq, 2 * HEAD_DIM), F32)],
        compiler_params=_cparams(("arbitrary", "arbitrary")),
        name="fox_attention",
    )(qkv, qkv, qkv, c_row5)


def _gmlp_kernel(h_ref, wz_ref, g_ref, b_ref, ws_ref, bst_ref, o_ref, *, n_groups):
    z = jnp.dot(h_ref[...], wz_ref[...], preferred_element_type=F32)
    z = 0.5 * z * (1.0 + lax.erf(z * (1.0 / math.sqrt(2.0))))
    W = z.shape[1] // 2
    u = z[:, :W]
    v = z[:, W:]
    mu = jnp.mean(v, axis=-1, keepdims=True)
    var = jnp.mean(jnp.square(v - mu), axis=-1, keepdims=True)
    vn = (v - mu) * lax.rsqrt(var + NORM_EPS) * g_ref[...] + b_ref[...]
    row = lax.broadcasted_iota(I32, (CHUNK, CHUNK), 0)
    col = lax.broadcasted_iota(I32, (CHUNK, CHUNK), 1)
    tril = col <= row
    tg = z.shape[0]
    for g in range(n_groups):
        gs = slice(g * GROUP_DIM, (g + 1) * GROUP_DIM)
        wg = jnp.where(tril, ws_ref[g], 0.0).astype(BF16)
        bias = bst_ref[:, g:g + 1]
        for c in range(tg // CHUNK):
            cs = slice(c * CHUNK, (c + 1) * CHUNK)
            mixed = jnp.dot(wg, vn[cs, gs].astype(BF16), preferred_element_type=F32) + bias
            o_ref[cs, gs] = (u[cs, gs] * mixed).astype(o_ref.dtype)


def _gmlp(h, wz, g_v_ln, b_v_ln, w_spatial, b_spatial, tg=512):
    T, D = h.shape
    W2 = wz.shape[1]
    W = W2 // 2
    G = w_spatial.shape[0]
    return pl.pallas_call(
        functools.partial(_gmlp_kernel, n_groups=G),
        grid=(T // tg,),
        in_specs=[pl.BlockSpec((tg, D), lambda i: (i, 0)),
                  pl.BlockSpec((D, W2), lambda i: (0, 0)),
                  pl.BlockSpec((1, W), lambda i: (0, 0)),
                  pl.BlockSpec((1, W), lambda i: (0, 0)),
                  pl.BlockSpec((G, CHUNK, CHUNK), lambda i: (0, 0, 0)),
                  pl.BlockSpec((CHUNK, G), lambda i: (0, 0))],
        out_specs=pl.BlockSpec((tg, W), lambda i: (i, 0)),
        out_shape=jax.ShapeDtypeStruct((T, W), BF16),
        compiler_params=_cparams(("parallel",)),
        name="gmlp",
    )(h, wz, g_v_ln.reshape(1, W), b_v_ln.reshape(1, W), w_spatial, b_spatial.T)


def _merge_kernel(attn_ref, sg_ref, h_ref, wa_ref, wb_ref, wga_ref, wgb_ref, o_ref):
    h = h_ref[...]
    a = jnp.dot(attn_ref[...], wa_ref[...], preferred_element_type=F32)
    ga = jnp.dot(h, wga_ref[...], preferred_element_type=F32)
    m = jax.nn.sigmoid(ga) * a
    b = jnp.dot(sg_ref[...], wb_ref[...], preferred_element_type=F32)
    gb = jnp.dot(h, wgb_ref[...], preferred_element_type=F32)
    o_ref[...] = (m + jax.nn.sigmoid(gb) * b).astype(o_ref.dtype)


def _merge(attn, sg, h, wa, wb, wg, tm=512, tn=512):
    T, D = h.shape
    Wa = attn.shape[1]
    Wb = sg.shape[1]
    nt = D // tn
    return pl.pallas_call(
        _merge_kernel,
        grid=(nt, T // tm),
        in_specs=[pl.BlockSpec((tm, Wa), lambda j, i: (i, 0)),
                  pl.BlockSpec((tm, Wb), lambda j, i: (i, 0)),
                  pl.BlockSpec((tm, D), lambda j, i: (i, 0)),
                  pl.BlockSpec((Wa, tn), lambda j, i: (0, j)),
                  pl.BlockSpec((Wb, tn), lambda j, i: (0, j)),
                  pl.BlockSpec((D, tn), lambda j, i: (0, j)),
                  pl.BlockSpec((D, tn), lambda j, i: (0, j + nt))],
        out_specs=pl.BlockSpec((tm, tn), lambda j, i: (i, j)),
        out_shape=jax.ShapeDtypeStruct((T, D), BF16),
        compiler_params=_cparams(("arbitrary", "arbitrary")),
        name="gated_merge",
    )(attn, sg, h, wa, wb, wg, wg)


def _out_router_kernel(m_ref, x_ref, wo_ref, g_ref, wr_ref, br_ref,
                       x1_ref, h2_ref, sel_ref, gate_ref):
    x1 = x_ref[...] + jnp.dot(m_ref[...], wo_ref[...], preferred_element_type=F32)
    x1_ref[...] = x1
    ms = jnp.mean(x1 * x1, axis=-1, keepdims=True)
    h2 = x1 * lax.rsqrt(ms + NORM_EPS) * g_ref[...]
    h2_ref[...] = _pack_pairs(h2)
    logits = jnp.dot(h2, wr_ref[...], preferred_element_type=F32,
                     precision=lax.Precision.HIGHEST) + br_ref[...]
    E = logits.shape[1]
    lane = lax.broadcasted_iota(I32, logits.shape, 1)
    work = logits
    sel = jnp.zeros(logits.shape, F32)
    num = jnp.zeros(logits.shape, F32)
    denom = jnp.zeros((logits.shape[0], 1), F32)
    m0 = None
    for _ in range(TOP_K):
        m = jnp.max(work, axis=-1, keepdims=True)
        idx = jnp.min(jnp.where(work == m, lane, E), axis=-1, keepdims=True)
        onehot = lane == idx
        if m0 is None:
            m0 = m
        e = jnp.exp(m - m0)
        sel = jnp.where(onehot, 1.0, sel)
        num = jnp.where(onehot, e, num)
        denom = denom + e
        work = jnp.where(onehot, -jnp.inf, work)
    sel_ref[...] = sel
    gate_ref[...] = num / denom


def _out_router(merged, x, wo, g_ffn, w_router, b_router, to=256):
    T, D = x.shape
    E = w_router.shape[1]
    row = lambda i: (i, 0)
    fixed = lambda i: (0, 0)
    return pl.pallas_call(
        _out_router_kernel,
        grid=(T // to,),
        in_specs=[pl.BlockSpec((to, D), row), pl.BlockSpec((to, D), row),
                  pl.BlockSpec((D, D), fixed), pl.BlockSpec((1, D), fixed),
                  pl.BlockSpec((D, E), fixed), pl.BlockSpec((1, E), fixed)],
        out_specs=[pl.BlockSpec((to, D), row), pl.BlockSpec((to, D // 2), row),
                   pl.BlockSpec((to, E), row), pl.BlockSpec((to, E), row)],
        out_shape=[jax.ShapeDtypeStruct((T, D), F32), jax.ShapeDtypeStruct((T, D // 2), jnp.uint32),
                   jax.ShapeDtypeStruct((T, E), F32), jax.ShapeDtypeStruct((T, E), F32)],
        compiler_params=_cparams(("parallel",)),
        name="out_router",
    )(merged, x, wo, g_ffn.reshape(1, D), w_router, b_router.reshape(1, E))


def _routing_kernel(sel_ref, gate_ref, dest_ref, g4_ref, nblk_ref, start_ref, rank_ref, *, tile):
    T, E = sel_ref.shape
    nt = T // tile
    r = lax.broadcasted_iota(I32, (tile, tile), 0)
    c = lax.broadcasted_iota(I32, (tile, tile), 1)
    strict_lower = (c < r).astype(BF16)
    er = lax.broadcasted_iota(I32, (E, E), 0)
    ec = lax.broadcasted_iota(I32, (E, E), 1)
    strict_upper = (er < ec).astype(BF16)

    def pass1(t, carry):
        rows = pl.ds(pl.multiple_of(t * tile, tile), tile)
        a = sel_ref[rows, :]
        rank_ref[rows, :] = jnp.dot(strict_lower, a.astype(BF16), preferred_element_type=F32) + carry
        return carry + jnp.sum(a, axis=0, keepdims=True)

    counts = lax.fori_loop(0, nt, pass1, jnp.zeros((1, E), F32))
    nblk = jnp.floor((counts + (ROW_BLOCK - 1)) * (1.0 / ROW_BLOCK))
    start_blk = jnp.dot(nblk.astype(BF16), strict_upper, preferred_element_type=F32)
    nblk_ref[...] = nblk.astype(I32)
    start_ref[...] = start_blk.astype(I32)
    start_row = start_blk * float(ROW_BLOCK)
    lane = lax.broadcasted_iota(I32, (tile, 128), 1)

    def pass2(t, _):
        rows = pl.ds(pl.multiple_of(t * tile, tile), tile)
        a = sel_ref[rows, :]
        g = gate_ref[rows, :]
        dest_e = rank_ref[rows, :] + start_row
        slot = jnp.dot(a.astype(BF16), strict_upper, preferred_element_type=F32)
        d4 = jnp.zeros((tile, 128), F32)
        g4 = jnp.zeros((tile, 128), F32)
        for s in range(TOP_K):
            pick = (a > 0.5) & (slot == float(s))
            d4 = jnp.where(lane == s, jnp.sum(jnp.where(pick, dest_e, 0.0), axis=-1, keepdims=True), d4)
            g4 = jnp.where(lane == s, jnp.sum(jnp.where(pick, g, 0.0), axis=-1, keepdims=True), g4)
        dest_ref[rows, :] = d4.astype(I32)
        g4_ref[rows, :] = g4
        return 0

    lax.fori_loop(0, nt, pass2, 0)


def _routing(sel, gate, tile=256):
    T, E = sel.shape
    return pl.pallas_call(
        functools.partial(_routing_kernel, tile=tile),
        out_shape=[jax.ShapeDtypeStruct((T, 128), I32), jax.ShapeDtypeStruct((T, 128), F32),
                   jax.ShapeDtypeStruct((1, E), I32), jax.ShapeDtypeStruct((1, E), I32)],
        scratch_shapes=[pltpu.VMEM((T, E), F32)],
        compiler_params=pltpu.CompilerParams(vmem_limit_bytes=VMEM_LIMIT_BYTES),
        name="routing_ranks",
    )(sel, gate)


def _rowtok_kernel(dest_ref, out_ref, *, n_tok):
    def zero(r, _):
        out_ref[r] = 0
        return 0

    lax.fori_loop(0, out_ref.shape[0], zero, 0, unroll=8)

    def body(t, _):
        for k in range(TOP_K):
            out_ref[dest_ref[t * TOP_K + k]] = t
        return 0

    lax.fori_loop(0, n_tok, body, 0, unroll=8)


def _row_tokens(dest_flat, n_rows, n_tok):
    smem = pl.BlockSpec(memory_space=pltpu.SMEM)
    return pl.pallas_call(
        functools.partial(_rowtok_kernel, n_tok=n_tok),
        in_specs=[smem],
        out_specs=smem,
        out_shape=jax.ShapeDtypeStruct((n_rows,), I32),
        name="row_tokens",
    )(dest_flat)


def _expert_kernel(ie_ref, ib_ref, ins_ref, rt_ref,
                   h2p_hbm, wup_ref, bup_ref, wdn_ref, bdn_ref, perm_ref, ys_hbm,
                   xg_ref, acc_ref, yst_ref, gsem, osem, *, n_items, n_ff_tiles):
    i = pl.program_id(0)
    j = pl.program_id(1)
    nsub = ins_ref[i]
    slot = i % 2
    part = ITEM_ROWS // n_ff_tiles

    def issue_rows(item, n_rows, dst_slot, first, count):
        row0 = ib_ref[item] * ROW_BLOCK

        def group(g, _):
            base = first + g * DMA_UNROLL

            @pl.when(base < n_rows)
            def _():
                for u in range(DMA_UNROLL):
                    tok = rt_ref[row0 + base + u]
                    pltpu.make_async_copy(h2p_hbm.at[pl.ds(tok, 1)],
                                          xg_ref.at[dst_slot, pl.ds(base + u, 1)],
                                          gsem.at[dst_slot]).start()
            return 0

        lax.fori_loop(0, count // DMA_UNROLL, group, 0)

    @pl.when((i == 0) & (j == 0))
    def _():
        issue_rows(0, nsub * ROW_BLOCK, 0, 0, ITEM_ROWS)

    nxt = jnp.minimum(i + 1, n_items - 1)
    nxt_rows = jnp.where(i + 1 < n_items, ins_ref[nxt], 0) * ROW_BLOCK
    issue_rows(nxt, nxt_rows, 1 - slot, j * part, part)

    @pl.when(j == 0)
    def _():
        for s in range(ITEM_BLOCKS):
            @pl.when(s < nsub)
            def _():
                rows = pl.ds(s * ROW_BLOCK, ROW_BLOCK)
                pltpu.make_async_copy(h2p_hbm.at[pl.ds(0, ROW_BLOCK)], xg_ref.at[slot, rows],
                                      gsem.at[slot]).wait()
                acc_ref[rows, :] = jnp.broadcast_to(bdn_ref[0], (ROW_BLOCK, acc_ref.shape[1]))

    @pl.when(nsub > 0)
    def _():
        wup = wup_ref[0].astype(BF16)
        wdn = wdn_ref[0].astype(BF16)
        bup = bup_ref[0]
        perm = perm_ref[...]
        half = perm.shape[0] // 2

        def sub_block(s):
            rows = pl.ds(s * ROW_BLOCK, ROW_BLOCK)
            xb = _unpack_pairs(xg_ref[slot, rows, :]).astype(BF16)
            gu = (jnp.dot(xb, wup, preferred_element_type=F32) + bup).astype(BF16)
            glu_parts, lin_parts = [], []
            for p in range(gu.shape[1] // perm.shape[0]):
                gp = jnp.dot(gu[:, p * perm.shape[0]:(p + 1) * perm.shape[0]], perm,
                             preferred_element_type=F32)
                glu_parts.append(gp[:, :half])
                lin_parts.append(gp[:, half:])
            x_glu = jnp.minimum(jnp.concatenate(glu_parts, axis=1), SWIGLU_LIMIT)
            x_lin = jnp.clip(jnp.concatenate(lin_parts, axis=1), -SWIGLU_LIMIT, SWIGLU_LIMIT)
            act = x_glu * jax.nn.sigmoid(SWIGLU_ALPHA * x_glu) * (x_lin + 1.0)
            acc_ref[rows, :] += jnp.dot(act.astype(BF16), wdn, preferred_element_type=F32)

        sub_block(0)
        for s in range(1, ITEM_BLOCKS):
            pl.when(s < nsub)(functools.partial(sub_block, s))

    def out_copy(item, s):
        dst = pl.multiple_of((ib_ref[item] + s) * ROW_BLOCK, ROW_BLOCK)
        return pltpu.make_async_copy(yst_ref.at[pl.ds(s * ROW_BLOCK, ROW_BLOCK)],
                                     ys_hbm.at[pl.ds(dst, ROW_BLOCK)], osem)

    @pl.when(j == n_ff_tiles - 1)
    def _():
        prev = jnp.maximum(i - 1, 0)
        prev_sub = jnp.where(i > 0, ins_ref[prev], 0)
        for s in range(ITEM_BLOCKS):
            @pl.when(s < prev_sub)
            def _():
                out_copy(prev, s).wait()
        for s in range(ITEM_BLOCKS):
            @pl.when(s < nsub)
            def _():
                rows = pl.ds(s * ROW_BLOCK, ROW_BLOCK)
                yst_ref[rows, :] = _pack_pairs(acc_ref[rows, :])
                out_copy(i, s).start()

        @pl.when(i == n_items - 1)
        def _():
            for s in range(ITEM_BLOCKS):
                @pl.when(s < nsub)
                def _():
                    out_copy(i, s).wait()


def _experts(h2p, row_tok, item_e, item_b, item_n, w_up, b_up, w_down, b_down, n_rows):
    T, Dp = h2p.shape
    D = 2 * Dp
    E, _, F2 = w_up.shape
    F = F2 // 2
    J = F // FF_TILE
    n_items = item_e.shape[0]
    half = 128
    perm = np.zeros((2 * half, 2 * half), np.float32)
    perm[2 * np.arange(half), np.arange(half)] = 1.0
    perm[2 * np.arange(half) + 1, half + np.arange(half)] = 1.0

    def jj(i, j, ins):
        return jnp.where(ins[i] > 0, j, J - 1)

    grid_spec = pltpu.PrefetchScalarGridSpec(
        num_scalar_prefetch=4,
        grid=(n_items, J),
        in_specs=[pl.BlockSpec(memory_space=pl.ANY),
                  pl.BlockSpec((1, D, 2 * FF_TILE), lambda i, j, ie, ib, ins, rt: (ie[i], 0, jj(i, j, ins))),
                  pl.BlockSpec((1, 1, 2 * FF_TILE), lambda i, j, ie, ib, ins, rt: (ie[i], 0, jj(i, j, ins))),
                  pl.BlockSpec((1, FF_TILE, D), lambda i, j, ie, ib, ins, rt: (ie[i], jj(i, j, ins), 0)),
                  pl.BlockSpec((1, 1, D), lambda i, j, ie, ib, ins, rt: (ie[i], 0, 0)),
                  pl.BlockSpec((2 * half, 2 * half), lambda i, j, ie, ib, ins, rt: (0, 0))],
        out_specs=pl.BlockSpec(memory_space=pl.ANY),
        scratch_shapes=[pltpu.VMEM((2, ITEM_ROWS, Dp), jnp.uint32),
                        pltpu.VMEM((ITEM_ROWS, D), F32),
                        pltpu.VMEM((ITEM_ROWS, Dp), jnp.uint32),
                        pltpu.SemaphoreType.DMA((2,)), pltpu.SemaphoreType.DMA(())],
    )
    return pl.pallas_call(
        functools.partial(_expert_kernel, n_items=n_items, n_ff_tiles=J),
        grid_spec=grid_spec,
        out_shape=jax.ShapeDtypeStruct((n_rows, Dp), jnp.uint32),
        compiler_params=_cparams(("arbitrary", "arbitrary")),
        name="expert_ffn",
    )(item_e, item_b, item_n, row_tok,
      h2p, w_up, b_up.reshape(E, 1, F2), w_down, b_down.reshape(E, 1, D), jnp.asarray(perm, BF16))


def _combine_kernel(dest_ref, ys_hbm, g4_ref, x1_ref, g_ref, o_ref, buf_ref, sem, *, tc, n_tiles):
    i = pl.program_id(0)
    slot = i % 2

    def issue(tile, dst_slot):
        def body(t, _):
            base = (tile * tc + t) * TOP_K
            for k in range(TOP_K):
                pltpu.make_async_copy(ys_hbm.at[pl.ds(dest_ref[base + k], 1)],
                                      buf_ref.at[dst_slot, k, pl.ds(t, 1)], sem.at[dst_slot]).start()
            return 0

        lax.fori_loop(0, tc, body, 0, unroll=2)

    @pl.when(i == 0)
    def _():
        issue(0, 0)

    @pl.when(i + 1 < n_tiles)
    def _():
        issue(i + 1, 1 - slot)

    for k in range(TOP_K):
        pltpu.make_async_copy(ys_hbm.at[pl.ds(0, tc)], buf_ref.at[slot, k], sem.at[slot]).wait()
    y = x1_ref[...]
    g4 = g4_ref[...]
    for k in range(TOP_K):
        y = y + g4[:, k:k + 1] * _unpack_pairs(buf_ref[slot, k])
    ms = jnp.mean(y * y, axis=-1, keepdims=True)
    o_ref[...] = y * lax.rsqrt(ms + NORM_EPS) * g_ref[...]


def _combine(dest_flat, ys, g4, x1, g_final, tc=128):
    T, D = x1.shape
    grid_spec = pltpu.PrefetchScalarGridSpec(
        num_scalar_prefetch=1,
        grid=(T // tc,),
        in_specs=[pl.BlockSpec(memory_space=pl.ANY),
                  pl.BlockSpec((tc, 128), lambda i, d: (i, 0)),
                  pl.BlockSpec((tc, D), lambda i, d: (i, 0)),
                  pl.BlockSpec((1, D), lambda i, d: (0, 0))],
        out_specs=pl.BlockSpec((tc, D), lambda i, d: (i, 0)),
        scratch_shapes=[pltpu.VMEM((2, TOP_K, tc, D // 2), jnp.uint32), pltpu.SemaphoreType.DMA((2,))],
    )
    return pl.pallas_call(
        functools.partial(_combine_kernel, tc=tc, n_tiles=T // tc),
        grid_spec=grid_spec,
        out_shape=jax.ShapeDtypeStruct((T, D), F32),
        compiler_params=_cparams(("arbitrary",)),
        name="combine_norm",
    )(dest_flat, ys, g4, x1, g_final.reshape(1, D))


def _work_items(nblk, start_blk, n_items):
    E = nblk.shape[0]
    per_e = (nblk + ITEM_BLOCKS - 1) // ITEM_BLOCKS
    ends = jnp.cumsum(per_e)
    total = ends[-1]
    idx = jnp.arange(n_items, dtype=I32)
    e = jnp.minimum(jnp.searchsorted(ends, idx, side="right"), E - 1).astype(I32)
    local = idx - (ends[e] - per_e[e])
    active = idx < total
    last_e = e[jnp.maximum(total - 1, 0)]
    item_e = jnp.where(active, e, last_e).astype(I32)
    item_b = jnp.where(active, start_blk[e] + local * ITEM_BLOCKS, 0).astype(I32)
    item_n = jnp.where(active, jnp.clip(nblk[e] - local * ITEM_BLOCKS, 0, ITEM_BLOCKS), 0).astype(I32)
    return item_e, item_b, item_n


def kernel(x, g_mix, w_in, b_forget, g_v_ln, b_v_ln, w_spatial, b_spatial, w_branch_attn, w_branch_gmlp, w_out, g_ffn, w_router, b_router, w_expert_up, b_expert_up, w_expert_down, b_expert_down, g_final):
    B, S, D = x.shape
    T = B * S
    n_heads = b_forget.shape[0]
    attn_w = n_heads * HEAD_DIM
    gmlp_w = g_v_ln.shape[0]
    E = w_router.shape[1]
    off_f = 3 * attn_w
    off_z = off_f + n_heads
    off_g = off_z + 2 * gmlp_w

    x2 = x.reshape(T, D)
    wft = w_in[:, off_f:off_z].T.astype(BF16)
    wz = w_in[:, off_z:off_g].astype(BF16)
    wg = w_in[:, off_g:].astype(BF16)
    wa = w_branch_attn.astype(BF16)
    wb = w_branch_gmlp.astype(BF16)
    wo = w_out.astype(BF16)

    h = _rmsnorm(x2, g_mix, BF16)
    qkv = _project(h, w_in, 3 * attn_w, attn_w, LOG2E / math.sqrt(HEAD_DIM), BF16)
    c_row = _forget_cumsum(h, wft, b_forget, B, S)
    attn = _attention(qkv, c_row, B, S, n_heads)
    sg = _gmlp(h, wz, g_v_ln, b_v_ln, w_spatial, b_spatial)
    merged = _merge(attn, sg, h, wa, wb, wg)
    x1, h2p, sel, gate = _out_router(merged, x2, wo, g_ffn, w_router, b_router)

    dest4, g4, nblk, start_blk = _routing(sel, gate)
    n_rows = T * TOP_K + E * ROW_BLOCK
    dest_flat = dest4[:, :TOP_K].reshape(T * TOP_K)
    row_tok = _row_tokens(dest_flat, n_rows, T)
    max_blocks = n_rows // ROW_BLOCK
    n_items = E + -(-(max_blocks - E) // ITEM_BLOCKS) + 1
    item_e, item_b, item_n = _work_items(nblk[0], start_blk[0], n_items)
    ys = _experts(h2p, row_tok, item_e, item_b, item_n,
                  w_expert_up, b_expert_up, w_expert_down, b_expert_down, n_rows)
    out = _combine(dest_flat, ys, g4, x1, g_final)
    return out.reshape(B, S, D)
```

```python
import functools
import math

import jax
import jax.numpy as jnp
import numpy as np
from jax import lax
from jax.experimental import pallas as pl
from jax.experimental.pallas import tpu as pltpu
from jax.experimental.pallas import tpu_sc as plsc

F32 = jnp.float32
BF16 = jnp.bfloat16
I32 = jnp.int32

NORM_EPS = 1e-5
HEAD_DIM = 128
CHUNK = 128
GROUP_DIM = 128
TOP_K = 4
SWIGLU_ALPHA = 1.702
SWIGLU_LIMIT = 7.0
LOG2E = math.log2(math.e)

VMEM_LIMIT_BYTES = 56 * 1024 * 1024

ROW_BLOCK = 256
ITEM_BLOCKS = 5
ITEM_ROWS = ITEM_BLOCKS * ROW_BLOCK
FF_TILE = 256
DMA_UNROLL = 8


def _cparams(sem, **kw):
    return pltpu.CompilerParams(dimension_semantics=sem, vmem_limit_bytes=VMEM_LIMIT_BYTES, **kw)


def _pack_pairs(x):
    c = x.shape[1] // 2
    hi = lax.bitcast_convert_type(x[:, :c].astype(BF16).astype(F32), jnp.uint32)
    lo = lax.bitcast_convert_type(x[:, c:].astype(BF16).astype(F32), jnp.uint32)
    return hi | (lo >> 16)


def _unpack_pairs(w):
    hi = lax.bitcast_convert_type(w & jnp.uint32(0xFFFF0000), F32)
    lo = lax.bitcast_convert_type(w << 16, F32)
    return jnp.concatenate([hi, lo], axis=1)


def _rmsnorm_kernel(x_ref, g_ref, o_ref):
    x = x_ref[...]
    ms = jnp.mean(x * x, axis=-1, keepdims=True)
    o_ref[...] = (x * lax.rsqrt(ms + NORM_EPS) * g_ref[...]).astype(o_ref.dtype)


def _rmsnorm(x, g, out_dtype, tm=512):
    T, D = x.shape
    return pl.pallas_call(
        _rmsnorm_kernel,
        grid=(T // tm,),
        in_specs=[pl.BlockSpec((tm, D), lambda i: (i, 0)), pl.BlockSpec((1, D), lambda i: (0, 0))],
        out_specs=pl.BlockSpec((tm, D), lambda i: (i, 0)),
        out_shape=jax.ShapeDtypeStruct((T, D), out_dtype),
        compiler_params=_cparams(("parallel",)),
        name="rmsnorm",
    )(x, g.reshape(1, D))


def _proj_kernel(h_ref, w_ref, o_ref, wbf_ref, *, n_scaled, scale):
    j = pl.program_id(0)

    @pl.when(pl.program_id(1) == 0)
    def _():
        wbf_ref[...] = w_ref[...].astype(BF16)

    acc = jnp.dot(h_ref[...], wbf_ref[...], preferred_element_type=F32)
    o_ref[...] = (acc * jnp.where(j < n_scaled, scale, 1.0)).astype(o_ref.dtype)


def _project(h, w, n_cols, n_scaled_cols, scale, out_dtype, tm=1024, tn=512):
    T, D = h.shape
    return pl.pallas_call(
        functools.partial(_proj_kernel, n_scaled=n_scaled_cols // tn, scale=scale),
        grid=(n_cols // tn, T // tm),
        in_specs=[pl.BlockSpec((tm, D), lambda j, i: (i, 0)),
                  pl.BlockSpec((D, tn), lambda j, i: (0, j))],
        out_specs=pl.BlockSpec((tm, tn), lambda j, i: (i, j)),
        out_shape=jax.ShapeDtypeStruct((T, n_cols), out_dtype),
        scratch_shapes=[pltpu.VMEM((D, tn), BF16)],
        compiler_params=_cparams(("arbitrary", "arbitrary")),
        name="qkv_proj",
    )(h, w)


def _forget_kernel(h_ref, wft_ref, bf_ref, c_ref):
    ft = lax.dot_general(wft_ref[...], h_ref[...], (((1,), (1,)), ((), ())),
                         preferred_element_type=F32)
    c = jax.nn.log_sigmoid(ft + bf_ref[...])
    S = c.shape[1]
    lane = lax.broadcasted_iota(I32, c.shape, 1)
    shift = 1
    while shift < S:
        c = c + jnp.where(lane >= shift, pltpu.roll(c, shift, axis=1), 0.0)
        shift *= 2
    c_ref[0] = c * LOG2E


def _forget_cumsum(h, wft, b_forget, B, S):
    T, D = h.shape
    H = wft.shape[0]
    return pl.pallas_call(
        _forget_kernel,
        grid=(B,),
        in_specs=[pl.BlockSpec((S, D), lambda b: (b, 0)),
                  pl.BlockSpec((H, D), lambda b: (0, 0)),
                  pl.BlockSpec((H, 1), lambda b: (0, 0))],
        out_specs=pl.BlockSpec((1, H, S), lambda b: (b, 0, 0)),
        out_shape=jax.ShapeDtypeStruct((B, H, S), F32),
        compiler_params=_cparams(("parallel",)),
        name="forget_cumsum",
    )(h, wft, b_forget.reshape(H, 1))


def _attn_kernel(q_ref, k_ref, v_ref, crow_ref, o_ref, vaug_ref, m_ref, acc_ref, *, n_heads, tq):
    i = pl.program_id(1)

    @pl.when(i == 0)
    def _():
        ones = jnp.ones((v_ref.shape[0], HEAD_DIM), BF16)
        for h in range(n_heads):
            vaug_ref[h, :, :HEAD_DIM] = v_ref[:, h * HEAD_DIM:(h + 1) * HEAD_DIM]
            vaug_ref[h, :, HEAD_DIM:] = ones

    m_ref[...] = jnp.full(m_ref.shape, -jnp.inf, F32)
    acc_ref[...] = jnp.zeros(acc_ref.shape, F32)
    row = lax.broadcasted_iota(I32, (tq, tq), 0)
    col = lax.broadcasted_iota(I32, (tq, tq), 1)
    causal = col <= row

    def step(j, masked):
        keys = pl.ds(pl.multiple_of(j * tq, tq), tq)
        for h in range(n_heads):
            hs = slice(h * HEAD_DIM, (h + 1) * HEAD_DIM)
            s = lax.dot_general(q_ref[:, hs], k_ref[keys, hs], (((1,), (1,)), ((), ())),
                                preferred_element_type=F32) - crow_ref[0, h, j]
            if masked:
                s = jnp.where(causal, s, -jnp.inf)
            m_old = m_ref[h]
            m_new = jnp.maximum(m_old, jnp.max(s, axis=-1, keepdims=True))
            alpha = jnp.exp2(m_old - m_new)
            p = jnp.exp2(s - jnp.concatenate([m_new] * (tq // HEAD_DIM), axis=1))
            m_ref[h] = m_new
            pv = jnp.dot(p.astype(BF16), vaug_ref[h, keys, :], preferred_element_type=F32)
            acc_ref[h] = jnp.concatenate([alpha, alpha], axis=1) * acc_ref[h] + pv

    def body(j, _):
        step(j, False)
        return 0

    lax.fori_loop(0, i, body, 0)
    step(i, True)
    for h in range(n_heads):
        acc = acc_ref[h]
        o_ref[:, h * HEAD_DIM:(h + 1) * HEAD_DIM] = (acc[:, :HEAD_DIM] / acc[:, HEAD_DIM:]).astype(o_ref.dtype)


def _attention(qkv, c_row, B, S, n_heads, tq=256):
    T = qkv.shape[0]
    W = n_heads * HEAD_DIM
    nq = S // tq
    c_row5 = c_row.reshape(B, n_heads, nq, 1, tq)
    return pl.pallas_call(
        functools.partial(_attn_kernel, n_heads=n_heads, tq=tq),
        grid=(B, nq),
        in_specs=[pl.BlockSpec((tq, W), lambda b, i: (b * nq + i, 0)),
                  pl.BlockSpec((S, W), lambda b, i: (b, 1)),
                  pl.BlockSpec((S, W), lambda b, i: (b, 2)),
                  pl.BlockSpec((1, n_heads, nq, 1, tq), lambda b, i: (b, 0, 0, 0, 0))],
        out_specs=pl.BlockSpec((tq, W), lambda b, i: (b * nq + i, 0)),
        out_shape=jax.ShapeDtypeStruct((T, W), BF16),
        scratch_shapes=[pltpu.VMEM((n_heads, S, 2 * HEAD_DIM), BF16),
                        pltpu.VMEM((n_heads, tq, HEAD_DIM), F32),
                        pltpu.VMEM((n_heads, tq, 2 * HEAD_DIM), F32)],
        compiler_params=_cparams(("arbitrary", "arbitrary")),
        name="fox_attention",
    )(qkv, qkv, qkv, c_row5)


def _gmlp_kernel(h_ref, wz_ref, g_ref, b_ref, ws_ref, bst_ref, o_ref, *, n_groups):
    z = jnp.dot(h_ref[...], wz_ref[...], preferred_element_type=F32)
    z = 0.5 * z * (1.0 + lax.erf(z * (1.0 / math.sqrt(2.0))))
    W = z.shape[1] // 2
    u = z[:, :W]
    v = z[:, W:]
    mu = jnp.mean(v, axis=-1, keepdims=True)
    var = jnp.mean(jnp.square(v - mu), axis=-1, keepdims=True)
    vn = (v - mu) * lax.rsqrt(var + NORM_EPS) * g_ref[...] + b_ref[...]
    row = lax.broadcasted_iota(I32, (CHUNK, CHUNK), 0)
    col = lax.broadcasted_iota(I32, (CHUNK, CHUNK), 1)
    tril = col <= row
    tg = z.shape[0]
    for g in range(n_groups):
        gs = slice(g * GROUP_DIM, (g + 1) * GROUP_DIM)
        wg = jnp.where(tril, ws_ref[g], 0.0).astype(BF16)
        bias = bst_ref[:, g:g + 1]
        for c in range(tg // CHUNK):
            cs = slice(c * CHUNK, (c + 1) * CHUNK)
            mixed = jnp.dot(wg, vn[cs, gs].astype(BF16), preferred_element_type=F32) + bias
            o_ref[cs, gs] = (u[cs, gs] * mixed).astype(o_ref.dtype)


def _gmlp(h, wz, g_v_ln, b_v_ln, w_spatial, b_spatial, tg=512):
    T, D = h.shape
    W2 = wz.shape[1]
    W = W2 // 2
    G = w_spatial.shape[0]
    return pl.pallas_call(
        functools.partial(_gmlp_kernel, n_groups=G),
        grid=(T // tg,),
        in_specs=[pl.BlockSpec((tg, D), lambda i: (i, 0)),
                  pl.BlockSpec((D, W2), lambda i: (0, 0)),
                  pl.BlockSpec((1, W), lambda i: (0, 0)),
                  pl.BlockSpec((1, W), lambda i: (0, 0)),
                  pl.BlockSpec((G, CHUNK, CHUNK), lambda i: (0, 0, 0)),
                  pl.BlockSpec((CHUNK, G), lambda i: (0, 0))],
        out_specs=pl.BlockSpec((tg, W), lambda i: (i, 0)),
        out_shape=jax.ShapeDtypeStruct((T, W), BF16),
        compiler_params=_cparams(("parallel",)),
        name="gmlp",
    )(h, wz, g_v_ln.reshape(1, W), b_v_ln.reshape(1, W), w_spatial, b_spatial.T)


def _merge_kernel(attn_ref, sg_ref, h_ref, wa_ref, wb_ref, wga_ref, wgb_ref, o_ref):
    h = h_ref[...]
    a = jnp.dot(attn_ref[...], wa_ref[...], preferred_element_type=F32)
    ga = jnp.dot(h, wga_ref[...], preferred_element_type=F32)
    m = jax.nn.sigmoid(ga) * a
    b = jnp.dot(sg_ref[...], wb_ref[...], preferred_element_type=F32)
    gb = jnp.dot(h, wgb_ref[...], preferred_element_type=F32)
    o_ref[...] = (m + jax.nn.sigmoid(gb) * b).astype(o_ref.dtype)


def _merge(attn, sg, h, wa, wb, wg, tm=512, tn=512):
    T, D = h.shape
    Wa = attn.shape[1]
    Wb = sg.shape[1]
    nt = D // tn
    return pl.pallas_call(
        _merge_kernel,
        grid=(nt, T // tm),
        in_specs=[pl.BlockSpec((tm, Wa), lambda j, i: (i, 0)),
                  pl.BlockSpec((tm, Wb), lambda j, i: (i, 0)),
                  pl.BlockSpec((tm, D), lambda j, i: (i, 0)),
                  pl.BlockSpec((Wa, tn), lambda j, i: (0, j)),
                  pl.BlockSpec((Wb, tn), lambda j, i: (0, j)),
                  pl.BlockSpec((D, tn), lambda j, i: (0, j)),
                  pl.BlockSpec((D, tn), lambda j, i: (0, j + nt))],
        out_specs=pl.BlockSpec((tm, tn), lambda j, i: (i, j)),
        out_shape=jax.ShapeDtypeStruct((T, D), BF16),
        compiler_params=_cparams(("arbitrary", "arbitrary")),
        name="gated_merge",
    )(attn, sg, h, wa, wb, wg, wg)


def _out_router_kernel(m_ref, x_ref, wo_ref, g_ref, wr_ref, br_ref,
                       x1_ref, h2_ref, sel_ref, gate_ref):
    x1 = x_ref[...] + jnp.dot(m_ref[...], wo_ref[...], preferred_element_type=F32)
    x1_ref[...] = x1
    ms = jnp.mean(x1 * x1, axis=-1, keepdims=True)
    h2 = x1 * lax.rsqrt(ms + NORM_EPS) * g_ref[...]
    h2_ref[...] = _pack_pairs(h2)
    logits = jnp.dot(h2, wr_ref[...], preferred_element_type=F32,
                     precision=lax.Precision.HIGHEST) + br_ref[...]
    E = logits.shape[1]
    lane = lax.broadcasted_iota(I32, logits.shape, 1)
    work = logits
    sel = jnp.zeros(logits.shape, F32)
    num = jnp.zeros(logits.shape, F32)
    denom = jnp.zeros((logits.shape[0], 1), F32)
    m0 = None
    for _ in range(TOP_K):
        m = jnp.max(work, axis=-1, keepdims=True)
        idx = jnp.min(jnp.where(work == m, lane, E), axis=-1, keepdims=True)
        onehot = lane == idx
        if m0 is None:
            m0 = m
        e = jnp.exp(m - m0)
        sel = jnp.where(onehot, 1.0, sel)
        num = jnp.where(onehot, e, num)
        denom = denom + e
        work = jnp.where(onehot, -jnp.inf, work)
    sel_ref[...] = sel
    gate_ref[...] = num / denom


def _out_router(merged, x, wo, g_ffn, w_router, b_router, to=256):
    T, D = x.shape
    E = w_router.shape[1]
    row = lambda i: (i, 0)
    fixed = lambda i: (0, 0)
    return pl.pallas_call(
        _out_router_kernel,
        grid=(T // to,),
        in_specs=[pl.BlockSpec((to, D), row), pl.BlockSpec((to, D), row),
                  pl.BlockSpec((D, D), fixed), pl.BlockSpec((1, D), fixed),
                  pl.BlockSpec((D, E), fixed), pl.BlockSpec((1, E), fixed)],
        out_specs=[pl.BlockSpec((to, D), row), pl.BlockSpec((to, D // 2), row),
                   pl.BlockSpec((to, E), row), pl.BlockSpec((to, E), row)],
        out_shape=[jax.ShapeDtypeStruct((T, D), F32), jax.ShapeDtypeStruct((T, D // 2), jnp.uint32),
                   jax.ShapeDtypeStruct((T, E), F32), jax.ShapeDtypeStruct((T, E), F32)],
        compiler_params=_cparams(("parallel",)),
        name="out_router",
    )(merged, x, wo, g_ffn.reshape(1, D), w_router, b_router.reshape(1, E))


def _routing_kernel(sel_ref, gate_ref, dest_ref, g4_ref, nblk_ref, start_ref, rank_ref, *, tile):
    T, E = sel_ref.shape
    nt = T // tile
    r = lax.broadcasted_iota(I32, (tile, tile), 0)
    c = lax.broadcasted_iota(I32, (tile, tile), 1)
    strict_lower = (c < r).astype(BF16)
    er = lax.broadcasted_iota(I32, (E, E), 0)
    ec = lax.broadcasted_iota(I32, (E, E), 1)
    strict_upper = (er < ec).astype(BF16)

    def pass1(t, carry):
        rows = pl.ds(pl.multiple_of(t * tile, tile), tile)
        a = sel_ref[rows, :]
        rank_ref[rows, :] = jnp.dot(strict_lower, a.astype(BF16), preferred_element_type=F32) + carry
        return carry + jnp.sum(a, axis=0, keepdims=True)

    counts = lax.fori_loop(0, nt, pass1, jnp.zeros((1, E), F32))
    nblk = jnp.floor((counts + (ROW_BLOCK - 1)) * (1.0 / ROW_BLOCK))
    start_blk = jnp.dot(nblk.astype(BF16), strict_upper, preferred_element_type=F32)
    nblk_ref[...] = nblk.astype(I32)
    start_ref[...] = start_blk.astype(I32)
    start_row = start_blk * float(ROW_BLOCK)
    lane = lax.broadcasted_iota(I32, (tile, 128), 1)

    def pass2(t, _):
        rows = pl.ds(pl.multiple_of(t * tile, tile), tile)
        a = sel_ref[rows, :]
        g = gate_ref[rows, :]
        dest_e = rank_ref[rows, :] + start_row
        slot = jnp.dot(a.astype(BF16), strict_upper, preferred_element_type=F32)
        d4 = jnp.zeros((tile, 128), F32)
        g4 = jnp.zeros((tile, 128), F32)
        for s in range(TOP_K):
            pick = (a > 0.5) & (slot == float(s))
            d4 = jnp.where(lane == s, jnp.sum(jnp.where(pick, dest_e, 0.0), axis=-1, keepdims=True), d4)
            g4 = jnp.where(lane == s, jnp.sum(jnp.where(pick, g, 0.0), axis=-1, keepdims=True), g4)
        dest_ref[rows, :] = d4.astype(I32)
        g4_ref[rows, :] = g4
        return 0

    lax.fori_loop(0, nt, pass2, 0)


def _routing(sel, gate, tile=256):
    T, E = sel.shape
    return pl.pallas_call(
        functools.partial(_routing_kernel, tile=tile),
        out_shape=[jax.ShapeDtypeStruct((T, 128), I32), jax.ShapeDtypeStruct((T, 128), F32),
                   jax.ShapeDtypeStruct((1, E), I32), jax.ShapeDtypeStruct((1, E), I32)],
        scratch_shapes=[pltpu.VMEM((T, E), F32)],
        compiler_params=pltpu.CompilerParams(vmem_limit_bytes=VMEM_LIMIT_BYTES),
        name="routing_ranks",
    )(sel, gate)


def _rowtok_kernel(dest_ref, out_ref, *, n_tok):
    def zero(r, _):
        out_ref[r] = 0
        return 0

    lax.fori_loop(0, out_ref.shape[0], zero, 0, unroll=8)

    def body(t, _):
        for k in range(TOP_K):
            out_ref[dest_ref[t * TOP_K + k]] = t
        return 0

    lax.fori_loop(0, n_tok, body, 0, unroll=8)


def _row_tokens(dest_flat, n_rows, n_tok):
    smem = pl.BlockSpec(memory_space=pltpu.SMEM)
    return pl.pallas_call(
        functools.partial(_rowtok_kernel, n_tok=n_tok),
        in_specs=[smem],
        out_specs=smem,
        out_shape=jax.ShapeDtypeStruct((n_rows,), I32),
        name="row_tokens",
    )(dest_flat)


SC_CORES = 2
SC_SUBCORES = 16
SC_CHUNK = 32


def _sc_gather_rows(table, idx):
    n = idx.shape[0]
    W = table.shape[1]
    n_workers = SC_CORES * SC_SUBCORES
    per_worker = n // n_workers
    assert per_worker * n_workers == n and per_worker % SC_CHUNK == 0
    mesh = plsc.VectorSubcoreMesh(core_axis_name="c", subcore_axis_name="s",
                                  num_cores=SC_CORES, num_subcores=SC_SUBCORES)

    @functools.partial(
        pl.kernel, mesh=mesh,
        out_type=jax.ShapeDtypeStruct((n, W), table.dtype),
        scratch_types=[pltpu.VMEM((SC_CHUNK,), I32), pltpu.VMEM((SC_CHUNK, W), table.dtype),
                       pltpu.SemaphoreType.DMA],
        name="sc_gather_rows",
    )
    def gather(table_hbm, idx_hbm, out_hbm, idx_v, rows_v, sem):
        wid = lax.axis_index("s") * SC_CORES + lax.axis_index("c")
        base = wid * per_worker

        @pl.loop(0, per_worker // SC_CHUNK)
        def _(c):
            off = pl.multiple_of(base + c * SC_CHUNK, 8)
            pltpu.sync_copy(idx_hbm.at[pl.ds(off, SC_CHUNK)], idx_v)
            pltpu.async_copy(table_hbm.at[idx_v], rows_v, sem).wait()
            pltpu.sync_copy(rows_v, out_hbm.at[pl.ds(off, SC_CHUNK)])

    return gather(table, idx)


def _expert_kernel(ie_ref, ib_ref, ins_ref,
                   xs_hbm, wup_ref, bup_ref, wdn_ref, bdn_ref, perm_ref, ys_hbm,
                   xg_ref, acc_ref, yst_ref, gsem, osem, *, n_items, n_ff_tiles):
    i = pl.program_id(0)
    j = pl.program_id(1)
    nsub = ins_ref[i]
    slot = i % 2

    def in_copy(item, dst_slot, s):
        src = pl.multiple_of((ib_ref[item] + s) * ROW_BLOCK, ROW_BLOCK)
        return pltpu.make_async_copy(xs_hbm.at[pl.ds(src, ROW_BLOCK)],
                                     xg_ref.at[dst_slot, pl.ds(s * ROW_BLOCK, ROW_BLOCK)],
                                     gsem.at[dst_slot])

    def fetch_item(item, n_sub, dst_slot):
        for s in range(ITEM_BLOCKS):
            @pl.when(s < n_sub)
            def _():
                in_copy(item, dst_slot, s).start()

    @pl.when(j == 0)
    def _():
        @pl.when(i == 0)
        def _():
            fetch_item(0, nsub, 0)

        nxt = jnp.minimum(i + 1, n_items - 1)
        fetch_item(nxt, jnp.where(i + 1 < n_items, ins_ref[nxt], 0), 1 - slot)
        for s in range(ITEM_BLOCKS):
            @pl.when(s < nsub)
            def _():
                in_copy(i, slot, s).wait()
                rows = pl.ds(s * ROW_BLOCK, ROW_BLOCK)
                acc_ref[rows, :] = jnp.broadcast_to(bdn_ref[0], (ROW_BLOCK, acc_ref.shape[1]))

    @pl.when(nsub > 0)
    def _():
        wup = wup_ref[0].astype(BF16)
        wdn = wdn_ref[0].astype(BF16)
        bup = bup_ref[0]
        perm = perm_ref[...]
        half = perm.shape[0] // 2

        def sub_block(s):
            rows = pl.ds(s * ROW_BLOCK, ROW_BLOCK)
            xb = _unpack_pairs(xg_ref[slot, rows, :]).astype(BF16)
            gu = (jnp.dot(xb, wup, preferred_element_type=F32) + bup).astype(BF16)
            glu_parts, lin_parts = [], []
            for p in range(gu.shape[1] // perm.shape[0]):
                gp = jnp.dot(gu[:, p * perm.shape[0]:(p + 1) * perm.shape[0]], perm,
                             preferred_element_type=F32)
                glu_parts.append(gp[:, :half])
                lin_parts.append(gp[:, half:])
            x_glu = jnp.minimum(jnp.concatenate(glu_parts, axis=1), SWIGLU_LIMIT)
            x_lin = jnp.clip(jnp.concatenate(lin_parts, axis=1), -SWIGLU_LIMIT, SWIGLU_LIMIT)
            act = x_glu * jax.nn.sigmoid(SWIGLU_ALPHA * x_glu) * (x_lin + 1.0)
            acc_ref[rows, :] += jnp.dot(act.astype(BF16), wdn, preferred_element_type=F32)

        sub_block(0)
        for s in range(1, ITEM_BLOCKS):
            pl.when(s < nsub)(functools.partial(sub_block, s))

    def out_copy(item, s):
        dst = pl.multiple_of((ib_ref[item] + s) * ROW_BLOCK, ROW_BLOCK)
        return pltpu.make_async_copy(yst_ref.at[pl.ds(s * ROW_BLOCK, ROW_BLOCK)],
                                     ys_hbm.at[pl.ds(dst, ROW_BLOCK)], osem)

    @pl.when(j == n_ff_tiles - 1)
    def _():
        prev = jnp.maximum(i - 1, 0)
        prev_sub = jnp.where(i > 0, ins_ref[prev], 0)
        for s in range(ITEM_BLOCKS):
            @pl.when(s < prev_sub)
            def _():
                out_copy(prev, s).wait()
        for s in range(ITEM_BLOCKS):
            @pl.when(s < nsub)
            def _():
                rows = pl.ds(s * ROW_BLOCK, ROW_BLOCK)
                yst_ref[rows, :] = _pack_pairs(acc_ref[rows, :])
                out_copy(i, s).start()

        @pl.when(i == n_items - 1)
        def _():
            for s in range(ITEM_BLOCKS):
                @pl.when(s < nsub)
                def _():
                    out_copy(i, s).wait()


def _experts(xs, item_e, item_b, item_n, w_up, b_up, w_down, b_down):
    n_rows, Dp = xs.shape
    D = 2 * Dp
    E, _, F2 = w_up.shape
    F = F2 // 2
    J = F // FF_TILE
    n_items = item_e.shape[0]
    half = 128
    perm = np.zeros((2 * half, 2 * half), np.float32)
    perm[2 * np.arange(half), np.arange(half)] = 1.0
    perm[2 * np.arange(half) + 1, half + np.arange(half)] = 1.0

    def jj(i, j, ins):
        return jnp.where(ins[i] > 0, j, J - 1)

    grid_spec = pltpu.PrefetchScalarGridSpec(
        num_scalar_prefetch=3,
        grid=(n_items, J),
        in_specs=[pl.BlockSpec(memory_space=pl.ANY),
                  pl.BlockSpec((1, D, 2 * FF_TILE), lambda i, j, ie, ib, ins: (ie[i], 0, jj(i, j, ins))),
                  pl.BlockSpec((1, 1, 2 * FF_TILE), lambda i, j, ie, ib, ins: (ie[i], 0, jj(i, j, ins))),
                  pl.BlockSpec((1, FF_TILE, D), lambda i, j, ie, ib, ins: (ie[i], jj(i, j, ins), 0)),
                  pl.BlockSpec((1, 1, D), lambda i, j, ie, ib, ins: (ie[i], 0, 0)),
                  pl.BlockSpec((2 * half, 2 * half), lambda i, j, ie, ib, ins: (0, 0))],
        out_specs=pl.BlockSpec(memory_space=pl.ANY),
        scratch_shapes=[pltpu.VMEM((2, ITEM_ROWS, Dp), jnp.uint32),
                        pltpu.VMEM((ITEM_ROWS, D), F32),
                        pltpu.VMEM((ITEM_ROWS, Dp), jnp.uint32),
                        pltpu.SemaphoreType.DMA((2,)), pltpu.SemaphoreType.DMA(())],
    )
    return pl.pallas_call(
        functools.partial(_expert_kernel, n_items=n_items, n_ff_tiles=J),
        grid_spec=grid_spec,
        out_shape=jax.ShapeDtypeStruct((n_rows, Dp), jnp.uint32),
        compiler_params=_cparams(("arbitrary", "arbitrary")),
        name="expert_ffn",
    )(item_e, item_b, item_n,
      xs, w_up, b_up.reshape(E, 1, F2), w_down, b_down.reshape(E, 1, D), jnp.asarray(perm, BF16))


def _combine_kernel(yk_ref, g4_ref, x1_ref, g_ref, o_ref):
    y = x1_ref[...]
    g4 = g4_ref[...]
    Dp = yk_ref.shape[1] // TOP_K
    for k in range(TOP_K):
        y = y + g4[:, k:k + 1] * _unpack_pairs(yk_ref[:, k * Dp:(k + 1) * Dp])
    ms = jnp.mean(y * y, axis=-1, keepdims=True)
    o_ref[...] = y * lax.rsqrt(ms + NORM_EPS) * g_ref[...]


def _combine(yk, g4, x1, g_final, tc=256):
    T, D = x1.shape
    return pl.pallas_call(
        _combine_kernel,
        grid=(T // tc,),
        in_specs=[pl.BlockSpec((tc, yk.shape[1]), lambda i: (i, 0)),
                  pl.BlockSpec((tc, 128), lambda i: (i, 0)),
                  pl.BlockSpec((tc, D), lambda i: (i, 0)),
                  pl.BlockSpec((1, D), lambda i: (0, 0))],
        out_specs=pl.BlockSpec((tc, D), lambda i: (i, 0)),
        out_shape=jax.ShapeDtypeStruct((T, D), F32),
        compiler_params=_cparams(("parallel",)),
        name="combine_norm",
    )(yk, g4, x1, g_final.reshape(1, D))


def _work_items(nblk, start_blk, n_items):
    E = nblk.shape[0]
    per_e = (nblk + ITEM_BLOCKS - 1) // ITEM_BLOCKS
    ends = jnp.cumsum(per_e)
    total = ends[-1]
    idx = jnp.arange(n_items, dtype=I32)
    e = jnp.minimum(jnp.searchsorted(ends, idx, side="right"), E - 1).astype(I32)
    local = idx - (ends[e] - per_e[e])
    active = idx < total
    last_e = e[jnp.maximum(total - 1, 0)]
    item_e = jnp.where(active, e, last_e).astype(I32)
    item_b = jnp.where(active, start_blk[e] + local * ITEM_BLOCKS, 0).astype(I32)
    item_n = jnp.where(active, jnp.clip(nblk[e] - local * ITEM_BLOCKS, 0, ITEM_BLOCKS), 0).astype(I32)
    return item_e, item_b, item_n


def kernel(x, g_mix, w_in, b_forget, g_v_ln, b_v_ln, w_spatial, b_spatial, w_branch_attn, w_branch_gmlp, w_out, g_ffn, w_router, b_router, w_expert_up, b_expert_up, w_expert_down, b_expert_down, g_final):
    B, S, D = x.shape
    T = B * S
    n_heads = b_forget.shape[0]
    attn_w = n_heads * HEAD_DIM
    gmlp_w = g_v_ln.shape[0]
    E = w_router.shape[1]
    off_f = 3 * attn_w
    off_z = off_f + n_heads
    off_g = off_z + 2 * gmlp_w

    x2 = x.reshape(T, D)
    wft = w_in[:, off_f:off_z].T.astype(BF16)
    wz = w_in[:, off_z:off_g].astype(BF16)
    wg = w_in[:, off_g:].astype(BF16)
    wa = w_branch_attn.astype(BF16)
    wb = w_branch_gmlp.astype(BF16)
    wo = w_out.astype(BF16)

    h = _rmsnorm(x2, g_mix, BF16)
    qkv = _project(h, w_in, 3 * attn_w, attn_w, LOG2E / math.sqrt(HEAD_DIM), BF16)
    c_row = _forget_cumsum(h, wft, b_forget, B, S)
    attn = _attention(qkv, c_row, B, S, n_heads)
    sg = _gmlp(h, wz, g_v_ln, b_v_ln, w_spatial, b_spatial)
    merged = _merge(attn, sg, h, wa, wb, wg)
    x1, h2p, sel, gate = _out_router(merged, x2, wo, g_ffn, w_router, b_router)

    dest4, g4, nblk, start_blk = _routing(sel, gate)
    n_rows = T * TOP_K + E * ROW_BLOCK
    dest_flat = dest4[:, :TOP_K].reshape(T * TOP_K)
    row_tok = _row_tokens(dest_flat, n_rows, T)
    max_blocks = n_rows // ROW_BLOCK
    n_items = E + -(-(max_blocks - E) // ITEM_BLOCKS) + 1
    item_e, item_b, item_n = _work_items(nblk[0], start_blk[0], n_items)
    xs = _sc_gather_rows(h2p, row_tok)
    ys = _experts(xs, item_e, item_b, item_n,
                  w_expert_up, b_expert_up, w_expert_down, b_expert_down)
    yk = _sc_gather_rows(ys, dest_flat).reshape(T, TOP_K * (D // 2))
    out = _combine(yk, g4, x1, g_final)
    return out.reshape(B, S, D)
```

```python
import functools
import math

import jax
import jax.numpy as jnp
import numpy as np
from jax import lax
from jax.experimental import pallas as pl
from jax.experimental.pallas import tpu as pltpu
from jax.experimental.pallas import tpu_sc as plsc

F32 = jnp.float32
BF16 = jnp.bfloat16
I32 = jnp.int32

NORM_EPS = 1e-5
HEAD_DIM = 128
CHUNK = 128
GROUP_DIM = 128
TOP_K = 4
SWIGLU_ALPHA = 1.702
SWIGLU_LIMIT = 7.0
LOG2E = math.log2(math.e)

VMEM_LIMIT_BYTES = 56 * 1024 * 1024

ROW_BLOCK = 256
ITEM_BLOCKS = 5
ITEM_ROWS = ITEM_BLOCKS * ROW_BLOCK
FF_TILE = 256
DMA_UNROLL = 8


def _cparams(sem, **kw):
    return pltpu.CompilerParams(dimension_semantics=sem, vmem_limit_bytes=VMEM_LIMIT_BYTES, **kw)


def _pack_pairs(x):
    c = x.shape[1] // 2
    hi = lax.bitcast_convert_type(x[:, :c].astype(BF16).astype(F32), jnp.uint32)
    lo = lax.bitcast_convert_type(x[:, c:].astype(BF16).astype(F32), jnp.uint32)
    return hi | (lo >> 16)


def _unpack_pairs(w):
    hi = lax.bitcast_convert_type(w & jnp.uint32(0xFFFF0000), F32)
    lo = lax.bitcast_convert_type(w << 16, F32)
    return jnp.concatenate([hi, lo], axis=1)


def _rmsnorm_kernel(x_ref, g_ref, o_ref):
    x = x_ref[...]
    ms = jnp.mean(x * x, axis=-1, keepdims=True)
    o_ref[...] = (x * lax.rsqrt(ms + NORM_EPS) * g_ref[...]).astype(o_ref.dtype)


def _rmsnorm(x, g, out_dtype, tm=512):
    T, D = x.shape
    return pl.pallas_call(
        _rmsnorm_kernel,
        grid=(T // tm,),
        in_specs=[pl.BlockSpec((tm, D), lambda i: (i, 0)), pl.BlockSpec((1, D), lambda i: (0, 0))],
        out_specs=pl.BlockSpec((tm, D), lambda i: (i, 0)),
        out_shape=jax.ShapeDtypeStruct((T, D), out_dtype),
        compiler_params=_cparams(("parallel",)),
        name="rmsnorm",
    )(x, g.reshape(1, D))


def _proj_kernel(h_ref, w_ref, o_ref, wbf_ref, *, n_scaled, scale):
    j = pl.program_id(0)

    @pl.when(pl.program_id(1) == 0)
    def _():
        wbf_ref[...] = w_ref[...].astype(BF16)

    acc = jnp.dot(h_ref[...], wbf_ref[...], preferred_element_type=F32)
    o_ref[...] = (acc * jnp.where(j < n_scaled, scale, 1.0)).astype(o_ref.dtype)


def _project(h, w, n_cols, n_scaled_cols, scale, out_dtype, tm=1024, tn=512):
    T, D = h.shape
    return pl.pallas_call(
        functools.partial(_proj_kernel, n_scaled=n_scaled_cols // tn, scale=scale),
        grid=(n_cols // tn, T // tm),
        in_specs=[pl.BlockSpec((tm, D), lambda j, i: (i, 0)),
                  pl.BlockSpec((D, tn), lambda j, i: (0, j))],
        out_specs=pl.BlockSpec((tm, tn), lambda j, i: (i, j)),
        out_shape=jax.ShapeDtypeStruct((T, n_cols), out_dtype),
        scratch_shapes=[pltpu.VMEM((D, tn), BF16)],
        compiler_params=_cparams(("arbitrary", "arbitrary")),
        name="qkv_proj",
    )(h, w)


def _forget_kernel(h_ref, wft_ref, bf_ref, c_ref):
    ft = lax.dot_general(wft_ref[...], h_ref[...], (((1,), (1,)), ((), ())),
                         preferred_element_type=F32)
    c = jax.nn.log_sigmoid(ft + bf_ref[...])
    S = c.shape[1]
    lane = lax.broadcasted_iota(I32, c.shape, 1)
    shift = 1
    while shift < S:
        c = c + jnp.where(lane >= shift, pltpu.roll(c, shift, axis=1), 0.0)
        shift *= 2
    c_ref[0] = c * LOG2E


def _forget_cumsum(h, wft, b_forget, B, S):
    T, D = h.shape
    H = wft.shape[0]
    return pl.pallas_call(
        _forget_kernel,
        grid=(B,),
        in_specs=[pl.BlockSpec((S, D), lambda b: (b, 0)),
                  pl.BlockSpec((H, D), lambda b: (0, 0)),
                  pl.BlockSpec((H, 1), lambda b: (0, 0))],
        out_specs=pl.BlockSpec((1, H, S), lambda b: (b, 0, 0)),
        out_shape=jax.ShapeDtypeStruct((B, H, S), F32),
        compiler_params=_cparams(("parallel",)),
        name="forget_cumsum",
    )(h, wft, b_forget.reshape(H, 1))


def _attn_kernel(q_ref, k_ref, v_ref, crow_ref, o_ref, vaug_ref, m_ref, acc_ref, *, n_heads, tq):
    i = pl.program_id(1)

    @pl.when(i == 0)
    def _():
        ones = jnp.ones((v_ref.shape[0], HEAD_DIM), BF16)
        for h in range(n_heads):
            vaug_ref[h, :, :HEAD_DIM] = v_ref[:, h * HEAD_DIM:(h + 1) * HEAD_DIM]
            vaug_ref[h, :, HEAD_DIM:] = ones

    m_ref[...] = jnp.full(m_ref.shape, -jnp.inf, F32)
    acc_ref[...] = jnp.zeros(acc_ref.shape, F32)
    row = lax.broadcasted_iota(I32, (tq, tq), 0)
    col = lax.broadcasted_iota(I32, (tq, tq), 1)
    causal = col <= row

    def step(j, masked):
        keys = pl.ds(pl.multiple_of(j * tq, tq), tq)
        for h in range(n_heads):
            hs = slice(h * HEAD_DIM, (h + 1) * HEAD_DIM)
            s = lax.dot_general(q_ref[:, hs], k_ref[keys, hs], (((1,), (1,)), ((), ())),
                                preferred_element_type=F32) - crow_ref[0, h, j]
            if masked:
                s = jnp.where(causal, s, -jnp.inf)
            m_old = m_ref[h]
            m_new = jnp.maximum(m_old, jnp.max(s, axis=-1, keepdims=True))
            alpha = jnp.exp2(m_old - m_new)
            p = jnp.exp2(s - jnp.concatenate([m_new] * (tq // HEAD_DIM), axis=1))
            m_ref[h] = m_new
            pv = jnp.dot(p.astype(BF16), vaug_ref[h, keys, :], preferred_element_type=F32)
            acc_ref[h] = jnp.concatenate([alpha, alpha], axis=1) * acc_ref[h] + pv

    def body(j, _):
        step(j, False)
        return 0

    lax.fori_loop(0, i, body, 0)
    step(i, True)
    for h in range(n_heads):
        acc = acc_ref[h]
        o_ref[:, h * HEAD_DIM:(h + 1) * HEAD_DIM] = (acc[:, :HEAD_DIM] / acc[:, HEAD_DIM:]).astype(o_ref.dtype)


def _attention(qkv, c_row, B, S, n_heads, tq=256):
    T = qkv.shape[0]
    W = n_heads * HEAD_DIM
    nq = S // tq
    c_row5 = c_row.reshape(B, n_heads, nq, 1, tq)
    return pl.pallas_call(
        functools.partial(_attn_kernel, n_heads=n_heads, tq=tq),
        grid=(B, nq),
        in_specs=[pl.BlockSpec((tq, W), lambda b, i: (b * nq + i, 0)),
                  pl.BlockSpec((S, W), lambda b, i: (b, 1)),
                  pl.BlockSpec((S, W), lambda b, i: (b, 2)),
                  pl.BlockSpec((1, n_heads, nq, 1, tq), lambda b, i: (b, 0, 0, 0, 0))],
        out_specs=pl.BlockSpec((tq, W), lambda b, i: (b * nq + i, 0)),
        out_shape=jax.ShapeDtypeStruct((T, W), BF16),
        scratch_shapes=[pltpu.VMEM((n_heads, S, 2 * HEAD_DIM), BF16),
                        pltpu.VMEM((n_heads, tq, HEAD_DIM), F32),
                        pltpu.VMEM((n_heads, tq, 2 * HEAD_DIM), F32)],
        compiler_params=_cparams(("arbitrary", "arbitrary")),
        name="fox_attention",
    )(qkv, qkv, qkv, c_row5)


def _gmlp_kernel(h_ref, wz_ref, g_ref, b_ref, ws_ref, bst_ref, o_ref, *, n_groups):
    z = jnp.dot(h_ref[...], wz_ref[...], preferred_element_type=F32)
    z = 0.5 * z * (1.0 + lax.erf(z * (1.0 / math.sqrt(2.0))))
    W = z.shape[1] // 2
    u = z[:, :W]
    v = z[:, W:]
    mu = jnp.mean(v, axis=-1, keepdims=True)
    var = jnp.mean(jnp.square(v - mu), axis=-1, keepdims=True)
    vn = (v - mu) * lax.rsqrt(var + NORM_EPS) * g_ref[...] + b_ref[...]
    row = lax.broadcasted_iota(I32, (CHUNK, CHUNK), 0)
    col = lax.broadcasted_iota(I32, (CHUNK, CHUNK), 1)
    tril = col <= row
    tg = z.shape[0]
    for g in range(n_groups):
        gs = slice(g * GROUP_DIM, (g + 1) * GROUP_DIM)
        wg = jnp.where(tril, ws_ref[g], 0.0).astype(BF16)
        bias = bst_ref[:, g:g + 1]
        for c in range(tg // CHUNK):
            cs = slice(c * CHUNK, (c + 1) * CHUNK)
            mixed = jnp.dot(wg, vn[cs, gs].astype(BF16), preferred_element_type=F32) + bias
            o_ref[cs, gs] = (u[cs, gs] * mixed).astype(o_ref.dtype)


def _gmlp(h, wz, g_v_ln, b_v_ln, w_spatial, b_spatial, tg=512):
    T, D = h.shape
    W2 = wz.shape[1]
    W = W2 // 2
    G = w_spatial.shape[0]
    return pl.pallas_call(
        functools.partial(_gmlp_kernel, n_groups=G),
        grid=(T // tg,),
        in_specs=[pl.BlockSpec((tg, D), lambda i: (i, 0)),
                  pl.BlockSpec((D, W2), lambda i: (0, 0)),
                  pl.BlockSpec((1, W), lambda i: (0, 0)),
                  pl.BlockSpec((1, W), lambda i: (0, 0)),
                  pl.BlockSpec((G, CHUNK, CHUNK), lambda i: (0, 0, 0)),
                  pl.BlockSpec((CHUNK, G), lambda i: (0, 0))],
        out_specs=pl.BlockSpec((tg, W), lambda i: (i, 0)),
        out_shape=jax.ShapeDtypeStruct((T, W), BF16),
        compiler_params=_cparams(("parallel",)),
        name="gmlp",
    )(h, wz, g_v_ln.reshape(1, W), b_v_ln.reshape(1, W), w_spatial, b_spatial.T)


def _merge_kernel(attn_ref, sg_ref, h_ref, wa_ref, wb_ref, wga_ref, wgb_ref, o_ref):
    h = h_ref[...]
    a = jnp.dot(attn_ref[...], wa_ref[...], preferred_element_type=F32)
    ga = jnp.dot(h, wga_ref[...], preferred_element_type=F32)
    m = jax.nn.sigmoid(ga) * a
    b = jnp.dot(sg_ref[...], wb_ref[...], preferred_element_type=F32)
    gb = jnp.dot(h, wgb_ref[...], preferred_element_type=F32)
    o_ref[...] = (m + jax.nn.sigmoid(gb) * b).astype(o_ref.dtype)


def _merge(attn, sg, h, wa, wb, wg, tm=512, tn=512):
    T, D = h.shape
    Wa = attn.shape[1]
    Wb = sg.shape[1]
    nt = D // tn
    return pl.pallas_call(
        _merge_kernel,
        grid=(nt, T // tm),
        in_specs=[pl.BlockSpec((tm, Wa), lambda j, i: (i, 0)),
                  pl.BlockSpec((tm, Wb), lambda j, i: (i, 0)),
                  pl.BlockSpec((tm, D), lambda j, i: (i, 0)),
                  pl.BlockSpec((Wa, tn), lambda j, i: (0, j)),
                  pl.BlockSpec((Wb, tn), lambda j, i: (0, j)),
                  pl.BlockSpec((D, tn), lambda j, i: (0, j)),
                  pl.BlockSpec((D, tn), lambda j, i: (0, j + nt))],
        out_specs=pl.BlockSpec((tm, tn), lambda j, i: (i, j)),
        out_shape=jax.ShapeDtypeStruct((T, D), BF16),
        compiler_params=_cparams(("arbitrary", "arbitrary")),
        name="gated_merge",
    )(attn, sg, h, wa, wb, wg, wg)


def _out_router_kernel(m_ref, x_ref, wo_ref, g_ref, wr_ref, br_ref,
                       x1_ref, h2_ref, sel_ref, gate_ref, *, n_parts):
    E = br_ref.shape[1]
    part = m_ref.shape[0] // n_parts
    for p in range(n_parts):
        rows = pl.ds(p * part, part)
        x1 = x_ref[rows, :] + jnp.dot(m_ref[rows, :], wo_ref[...], preferred_element_type=F32)
        x1_ref[rows, :] = x1
        ms = jnp.mean(x1 * x1, axis=-1, keepdims=True)
        h2 = x1 * lax.rsqrt(ms + NORM_EPS) * g_ref[...]
        h2_ref[rows, :] = _pack_pairs(h2)
        h2_hi = h2.astype(BF16)
        h2_lo = (h2 - h2_hi.astype(F32)).astype(BF16)
        pa = jnp.dot(h2_hi, wr_ref[...], preferred_element_type=F32)
        pb = jnp.dot(h2_lo, wr_ref[:, :E], preferred_element_type=F32)
        logits = pa[:, :E] + (pa[:, E:] + pb) + br_ref[...]
        lane = lax.broadcasted_iota(I32, logits.shape, 1)
        work = logits
        sel = jnp.zeros(logits.shape, F32)
        num = jnp.zeros(logits.shape, F32)
        denom = jnp.zeros((part, 1), F32)
        m0 = None
        for _ in range(TOP_K):
            m = jnp.max(work, axis=-1, keepdims=True)
            idx = jnp.min(jnp.where(work == m, lane, E), axis=-1, keepdims=True)
            onehot = lane == idx
            if m0 is None:
                m0 = m
            e = jnp.exp(m - m0)
            sel = jnp.where(onehot, 1.0, sel)
            num = jnp.where(onehot, e, num)
            denom = denom + e
            work = jnp.where(onehot, -jnp.inf, work)
        sel_ref[rows, :] = sel
        gate_ref[rows, :] = num / denom


def _out_router(merged, x, wo, g_ffn, w_router, b_router, to=512, n_parts=2):
    T, D = x.shape
    E = w_router.shape[1]
    w_hi = w_router.astype(BF16)
    w_lo = (w_router - w_hi.astype(F32)).astype(BF16)
    w_router = jnp.concatenate([w_hi, w_lo], axis=1)
    row = lambda i: (i, 0)
    fixed = lambda i: (0, 0)
    return pl.pallas_call(
        functools.partial(_out_router_kernel, n_parts=n_parts),
        grid=(T // to,),
        in_specs=[pl.BlockSpec((to, D), row), pl.BlockSpec((to, D), row),
                  pl.BlockSpec((D, D), fixed), pl.BlockSpec((1, D), fixed),
                  pl.BlockSpec((D, 2 * E), fixed), pl.BlockSpec((1, E), fixed)],
        out_specs=[pl.BlockSpec((to, D), row), pl.BlockSpec((to, D // 2), row),
                   pl.BlockSpec((to, E), row), pl.BlockSpec((to, E), row)],
        out_shape=[jax.ShapeDtypeStruct((T, D), F32), jax.ShapeDtypeStruct((T, D // 2), jnp.uint32),
                   jax.ShapeDtypeStruct((T, E), F32), jax.ShapeDtypeStruct((T, E), F32)],
        compiler_params=_cparams(("parallel",)),
        name="out_router",
    )(merged, x, wo, g_ffn.reshape(1, D), w_router, b_router.reshape(1, E))


def _routing_kernel(sel_ref, gate_ref, dest_ref, g4_ref, nblk_ref, start_ref, rank_ref, *, tile):
    T, E = sel_ref.shape
    nt = T // tile
    r = lax.broadcasted_iota(I32, (tile, tile), 0)
    c = lax.broadcasted_iota(I32, (tile, tile), 1)
    strict_lower = (c < r).astype(BF16)
    er = lax.broadcasted_iota(I32, (E, E), 0)
    ec = lax.broadcasted_iota(I32, (E, E), 1)
    strict_upper = (er < ec).astype(BF16)

    def pass1(t, carry):
        rows = pl.ds(pl.multiple_of(t * tile, tile), tile)
        a = sel_ref[rows, :]
        rank_ref[rows, :] = jnp.dot(strict_lower, a.astype(BF16), preferred_element_type=F32) + carry
        return carry + jnp.sum(a, axis=0, keepdims=True)

    counts = lax.fori_loop(0, nt, pass1, jnp.zeros((1, E), F32))
    nblk = jnp.floor((counts + (ROW_BLOCK - 1)) * (1.0 / ROW_BLOCK))
    start_blk = jnp.dot(nblk.astype(BF16), strict_upper, preferred_element_type=F32)
    nblk_ref[...] = nblk.astype(I32)
    start_ref[...] = start_blk.astype(I32)
    start_row = start_blk * float(ROW_BLOCK)
    lane = lax.broadcasted_iota(I32, (tile, 128), 1)

    def pass2(t, _):
        rows = pl.ds(pl.multiple_of(t * tile, tile), tile)
        a = sel_ref[rows, :]
        g = gate_ref[rows, :]
        dest_e = rank_ref[rows, :] + start_row
        slot = jnp.dot(a.astype(BF16), strict_upper, preferred_element_type=F32)
        d4 = jnp.zeros((tile, 128), F32)
        g4 = jnp.zeros((tile, 128), F32)
        for s in range(TOP_K):
            pick = (a > 0.5) & (slot == float(s))
            d4 = jnp.where(lane == s, jnp.sum(jnp.where(pick, dest_e, 0.0), axis=-1, keepdims=True), d4)
            g4 = jnp.where(lane == s, jnp.sum(jnp.where(pick, g, 0.0), axis=-1, keepdims=True), g4)
        dest_ref[rows, :] = d4.astype(I32)
        g4_ref[rows, :] = g4
        return 0

    lax.fori_loop(0, nt, pass2, 0)


def _routing(sel, gate, tile=256):
    T, E = sel.shape
    return pl.pallas_call(
        functools.partial(_routing_kernel, tile=tile),
        out_shape=[jax.ShapeDtypeStruct((T, 128), I32), jax.ShapeDtypeStruct((T, 128), F32),
                   jax.ShapeDtypeStruct((1, E), I32), jax.ShapeDtypeStruct((1, E), I32)],
        scratch_shapes=[pltpu.VMEM((T, E), F32)],
        compiler_params=pltpu.CompilerParams(vmem_limit_bytes=VMEM_LIMIT_BYTES),
        name="routing_ranks",
    )(sel, gate)


def _rowtok_kernel(dest_ref, out_ref, *, n_tok):
    def fill(r, _):
        out_ref[r] = lax.rem(r, n_tok)
        return 0

    lax.fori_loop(0, out_ref.shape[0], fill, 0, unroll=8)

    def body(t, _):
        for k in range(TOP_K):
            out_ref[dest_ref[t * TOP_K + k]] = t
        return 0

    lax.fori_loop(0, n_tok, body, 0, unroll=8)


def _row_tokens(dest_flat, n_rows, n_tok):
    smem = pl.BlockSpec(memory_space=pltpu.SMEM)
    return pl.pallas_call(
        functools.partial(_rowtok_kernel, n_tok=n_tok),
        in_specs=[smem],
        out_specs=smem,
        out_shape=jax.ShapeDtypeStruct((n_rows,), I32),
        name="row_tokens",
    )(dest_flat)


SC_CORES = 2
SC_SUBCORES = 16
SC_CHUNK = 32


def _sc_gather_rows(table, idx):
    n = idx.shape[0]
    W = table.shape[1]
    n_workers = SC_CORES * SC_SUBCORES
    per_worker = n // n_workers
    assert per_worker * n_workers == n and per_worker % SC_CHUNK == 0
    mesh = plsc.VectorSubcoreMesh(core_axis_name="c", subcore_axis_name="s",
                                  num_cores=SC_CORES, num_subcores=SC_SUBCORES)

    @functools.partial(
        pl.kernel, mesh=mesh,
        out_type=jax.ShapeDtypeStruct((n, W), table.dtype),
        scratch_types=[pltpu.VMEM((SC_CHUNK,), I32), pltpu.VMEM((SC_CHUNK, W), table.dtype),
                       pltpu.SemaphoreType.DMA],
        name="sc_gather_rows",
    )
    def gather(table_hbm, idx_hbm, out_hbm, idx_v, rows_v, sem):
        wid = lax.axis_index("s") * SC_CORES + lax.axis_index("c")
        base = wid * per_worker

        @pl.loop(0, per_worker // SC_CHUNK)
        def _(c):
            off = pl.multiple_of(base + c * SC_CHUNK, 8)
            pltpu.sync_copy(idx_hbm.at[pl.ds(off, SC_CHUNK)], idx_v)
            pltpu.async_copy(table_hbm.at[idx_v], rows_v, sem).wait()
            pltpu.sync_copy(rows_v, out_hbm.at[pl.ds(off, SC_CHUNK)])

    return gather(table, idx)


def _expert_kernel(ie_ref, ib_ref, ins_ref,
                   xs_hbm, wup_ref, bup_ref, wdn_ref, bdn_ref, perm_ref, ys_hbm,
                   xg_ref, acc_ref, yst_ref, gsem, osem, *, n_items, n_ff_tiles):
    i = pl.program_id(0)
    j = pl.program_id(1)
    nsub = ins_ref[i]
    slot = i % 2

    def in_copy(item, dst_slot, s):
        src = pl.multiple_of((ib_ref[item] + s) * ROW_BLOCK, ROW_BLOCK)
        return pltpu.make_async_copy(xs_hbm.at[pl.ds(src, ROW_BLOCK)],
                                     xg_ref.at[dst_slot, pl.ds(s * ROW_BLOCK, ROW_BLOCK)],
                                     gsem.at[dst_slot])

    def fetch_item(item, n_sub, dst_slot):
        for s in range(ITEM_BLOCKS):
            @pl.when(s < n_sub)
            def _():
                in_copy(item, dst_slot, s).start()

    @pl.when(j == 0)
    def _():
        @pl.when(i == 0)
        def _():
            fetch_item(0, nsub, 0)

        nxt = jnp.minimum(i + 1, n_items - 1)
        fetch_item(nxt, jnp.where(i + 1 < n_items, ins_ref[nxt], 0), 1 - slot)
        for s in range(ITEM_BLOCKS):
            @pl.when(s < nsub)
            def _():
                in_copy(i, slot, s).wait()
                rows = pl.ds(s * ROW_BLOCK, ROW_BLOCK)
                acc_ref[rows, :] = jnp.broadcast_to(bdn_ref[0], (ROW_BLOCK, acc_ref.shape[1]))

    @pl.when(nsub > 0)
    def _():
        wup = wup_ref[0].astype(BF16)
        wdn = wdn_ref[0].astype(BF16)
        bup = bup_ref[0]
        perm = perm_ref[...]
        half = perm.shape[0] // 2

        def sub_block(s):
            rows = pl.ds(s * ROW_BLOCK, ROW_BLOCK)
            xb = _unpack_pairs(xg_ref[slot, rows, :]).astype(BF16)
            gu = (jnp.dot(xb, wup, preferred_element_type=F32) + bup).astype(BF16)
            glu_parts, lin_parts = [], []
            for p in range(gu.shape[1] // perm.shape[0]):
                gp = jnp.dot(gu[:, p * perm.shape[0]:(p + 1) * perm.shape[0]], perm,
                             preferred_element_type=F32)
                glu_parts.append(gp[:, :half])
                lin_parts.append(gp[:, half:])
            x_glu = jnp.minimum(jnp.concatenate(glu_parts, axis=1), SWIGLU_LIMIT)
            x_lin = jnp.clip(jnp.concatenate(lin_parts, axis=1), -SWIGLU_LIMIT, SWIGLU_LIMIT)
            act = x_glu * jax.nn.sigmoid(SWIGLU_ALPHA * x_glu) * (x_lin + 1.0)
            acc_ref[rows, :] += jnp.dot(act.astype(BF16), wdn, preferred_element_type=F32)

        sub_block(0)
        for s in range(1, ITEM_BLOCKS):
            pl.when(s < nsub)(functools.partial(sub_block, s))

    def out_copy(item, s):
        dst = pl.multiple_of((ib_ref[item] + s) * ROW_BLOCK, ROW_BLOCK)
        return pltpu.make_async_copy(yst_ref.at[pl.ds(s * ROW_BLOCK, ROW_BLOCK)],
                                     ys_hbm.at[pl.ds(dst, ROW_BLOCK)], osem)

    @pl.when(j == n_ff_tiles - 1)
    def _():
        prev = jnp.maximum(i - 1, 0)
        prev_sub = jnp.where(i > 0, ins_ref[prev], 0)
        for s in range(ITEM_BLOCKS):
            @pl.when(s < prev_sub)
            def _():
                out_copy(prev, s).wait()
        for s in range(ITEM_BLOCKS):
            @pl.when(s < nsub)
            def _():
                rows = pl.ds(s * ROW_BLOCK, ROW_BLOCK)
                yst_ref[rows, :] = _pack_pairs(acc_ref[rows, :])
                out_copy(i, s).start()

        @pl.when(i == n_items - 1)
        def _():
            for s in range(ITEM_BLOCKS):
                @pl.when(s < nsub)
                def _():
                    out_copy(i, s).wait()


def _experts(xs, item_e, item_b, item_n, w_up, b_up, w_down, b_down):
    n_rows, Dp = xs.shape
    D = 2 * Dp
    E, _, F2 = w_up.shape
    F = F2 // 2
    J = F // FF_TILE
    n_items = item_e.shape[0]
    half = 128
    perm = np.zeros((2 * half, 2 * half), np.float32)
    perm[2 * np.arange(half), np.arange(half)] = 1.0
    perm[2 * np.arange(half) + 1, half + np.arange(half)] = 1.0

    def jj(i, j, ins):
        return jnp.where(ins[i] > 0, j, J - 1)

    grid_spec = pltpu.PrefetchScalarGridSpec(
        num_scalar_prefetch=3,
        grid=(n_items, J),
        in_specs=[pl.BlockSpec(memory_space=pl.ANY),
                  pl.BlockSpec((1, D, 2 * FF_TILE), lambda i, j, ie, ib, ins: (ie[i], 0, jj(i, j, ins))),
                  pl.BlockSpec((1, 1, 2 * FF_TILE), lambda i, j, ie, ib, ins: (ie[i], 0, jj(i, j, ins))),
                  pl.BlockSpec((1, FF_TILE, D), lambda i, j, ie, ib, ins: (ie[i], jj(i, j, ins), 0)),
                  pl.BlockSpec((1, 1, D), lambda i, j, ie, ib, ins: (ie[i], 0, 0)),
                  pl.BlockSpec((2 * half, 2 * half), lambda i, j, ie, ib, ins: (0, 0))],
        out_specs=pl.BlockSpec(memory_space=pl.ANY),
        scratch_shapes=[pltpu.VMEM((2, ITEM_ROWS, Dp), jnp.uint32),
                        pltpu.VMEM((ITEM_ROWS, D), F32),
                        pltpu.VMEM((ITEM_ROWS, Dp), jnp.uint32),
                        pltpu.SemaphoreType.DMA((2,)), pltpu.SemaphoreType.DMA(())],
    )
    return pl.pallas_call(
        functools.partial(_expert_kernel, n_items=n_items, n_ff_tiles=J),
        grid_spec=grid_spec,
        out_shape=jax.ShapeDtypeStruct((n_rows, Dp), jnp.uint32),
        compiler_params=_cparams(("arbitrary", "arbitrary")),
        name="expert_ffn",
    )(item_e, item_b, item_n,
      xs, w_up, b_up.reshape(E, 1, F2), w_down, b_down.reshape(E, 1, D), jnp.asarray(perm, BF16))


def _combine_kernel(*refs):
    yk_refs, (g4_ref, x1_ref, g_ref, o_ref) = refs[:TOP_K], refs[TOP_K:]
    y = x1_ref[...]
    g4 = g4_ref[...]
    for k in range(TOP_K):
        y = y + g4[:, k:k + 1] * _unpack_pairs(yk_refs[k][...])
    ms = jnp.mean(y * y, axis=-1, keepdims=True)
    o_ref[...] = y * lax.rsqrt(ms + NORM_EPS) * g_ref[...]


def _combine(yk, g4, x1, g_final, tc=256):
    T, D = x1.shape
    nt = T // tc
    slot_specs = [pl.BlockSpec((tc, D // 2), functools.partial(lambda i, k: (k * nt + i, 0), k=k))
                  for k in range(TOP_K)]
    return pl.pallas_call(
        _combine_kernel,
        grid=(nt,),
        in_specs=slot_specs + [pl.BlockSpec((tc, 128), lambda i: (i, 0)),
                               pl.BlockSpec((tc, D), lambda i: (i, 0)),
                               pl.BlockSpec((1, D), lambda i: (0, 0))],
        out_specs=pl.BlockSpec((tc, D), lambda i: (i, 0)),
        out_shape=jax.ShapeDtypeStruct((T, D), F32),
        compiler_params=_cparams(("parallel",)),
        name="combine_norm",
    )(*([yk] * TOP_K), g4, x1, g_final.reshape(1, D))


def _work_items(nblk, start_blk, n_items):
    E = nblk.shape[0]
    per_e = (nblk + ITEM_BLOCKS - 1) // ITEM_BLOCKS
    ends = jnp.cumsum(per_e)
    total = ends[-1]
    idx = jnp.arange(n_items, dtype=I32)
    e = jnp.minimum(jnp.searchsorted(ends, idx, side="right"), E - 1).astype(I32)
    local = idx - (ends[e] - per_e[e])
    active = idx < total
    last_e = e[jnp.maximum(total - 1, 0)]
    item_e = jnp.where(active, e, last_e).astype(I32)
    item_b = jnp.where(active, start_blk[e] + local * ITEM_BLOCKS, 0).astype(I32)
    item_n = jnp.where(active, jnp.clip(nblk[e] - local * ITEM_BLOCKS, 0, ITEM_BLOCKS), 0).astype(I32)
    return item_e, item_b, item_n


def kernel(x, g_mix, w_in, b_forget, g_v_ln, b_v_ln, w_spatial, b_spatial, w_branch_attn, w_branch_gmlp, w_out, g_ffn, w_router, b_router, w_expert_up, b_expert_up, w_expert_down, b_expert_down, g_final):
    B, S, D = x.shape
    T = B * S
    n_heads = b_forget.shape[0]
    attn_w = n_heads * HEAD_DIM
    gmlp_w = g_v_ln.shape[0]
    E = w_router.shape[1]
    off_f = 3 * attn_w
    off_z = off_f + n_heads
    off_g = off_z + 2 * gmlp_w

    x2 = x.reshape(T, D)
    wft = w_in[:, off_f:off_z].T.astype(BF16)
    wz = w_in[:, off_z:off_g].astype(BF16)
    wg = w_in[:, off_g:].astype(BF16)
    wa = w_branch_attn.astype(BF16)
    wb = w_branch_gmlp.astype(BF16)
    wo = w_out.astype(BF16)

    h = _rmsnorm(x2, g_mix, BF16)
    qkv = _project(h, w_in, 3 * attn_w, attn_w, LOG2E / math.sqrt(HEAD_DIM), BF16)
    c_row = _forget_cumsum(h, wft, b_forget, B, S)
    attn = _attention(qkv, c_row, B, S, n_heads)
    sg = _gmlp(h, wz, g_v_ln, b_v_ln, w_spatial, b_spatial)
    merged = _merge(attn, sg, h, wa, wb, wg)
    x1, h2p, sel, gate = _out_router(merged, x2, wo, g_ffn, w_router, b_router)

    dest4, g4, nblk, start_blk = _routing(sel, gate)
    n_rows = T * TOP_K + E * ROW_BLOCK
    dest_flat = dest4[:, :TOP_K].reshape(T * TOP_K)
    row_tok = _row_tokens(dest_flat, n_rows, T)
    max_blocks = n_rows // ROW_BLOCK
    n_items = E + -(-(max_blocks - E) // ITEM_BLOCKS) + 1
    item_e, item_b, item_n = _work_items(nblk[0], start_blk[0], n_items)
    xs = _sc_gather_rows(h2p, row_tok)
    ys = _experts(xs, item_e, item_b, item_n,
                  w_expert_up, b_expert_up, w_expert_down, b_expert_down)
    yk = _sc_gather_rows(ys, dest4[:, :TOP_K].T.reshape(TOP_K * T))
    out = _combine(yk, g4, x1, g_final)
    return out.reshape(B, S, D)
```

```python
import functools
import math

import jax
import jax.numpy as jnp
import numpy as np
from jax import lax
from jax.experimental import pallas as pl
from jax.experimental.pallas import tpu as pltpu
from jax.experimental.pallas import tpu_sc as plsc

F32 = jnp.float32
BF16 = jnp.bfloat16
I32 = jnp.int32

NORM_EPS = 1e-5
HEAD_DIM = 128
CHUNK = 128
GROUP_DIM = 128
TOP_K = 4
SWIGLU_ALPHA = 1.702
SWIGLU_LIMIT = 7.0
LOG2E = math.log2(math.e)

VMEM_LIMIT_BYTES = 56 * 1024 * 1024

ROW_BLOCK = 256
ITEM_BLOCKS = 5
ITEM_ROWS = ITEM_BLOCKS * ROW_BLOCK
FF_TILE = 256
DMA_UNROLL = 8


def _cparams(sem, **kw):
    return pltpu.CompilerParams(dimension_semantics=sem, vmem_limit_bytes=VMEM_LIMIT_BYTES, **kw)


def _pack_pairs(x):
    c = x.shape[1] // 2
    hi = lax.bitcast_convert_type(x[:, :c].astype(BF16).astype(F32), jnp.uint32)
    lo = lax.bitcast_convert_type(x[:, c:].astype(BF16).astype(F32), jnp.uint32)
    return hi | (lo >> 16)


def _unpack_pairs(w):
    hi = lax.bitcast_convert_type(w & jnp.uint32(0xFFFF0000), F32)
    lo = lax.bitcast_convert_type(w << 16, F32)
    return jnp.concatenate([hi, lo], axis=1)


def _rmsnorm_kernel(x_ref, g_ref, o_ref):
    x = x_ref[...]
    ms = jnp.mean(x * x, axis=-1, keepdims=True)
    o_ref[...] = (x * lax.rsqrt(ms + NORM_EPS) * g_ref[...]).astype(o_ref.dtype)


def _rmsnorm(x, g, out_dtype, tm=512):
    T, D = x.shape
    return pl.pallas_call(
        _rmsnorm_kernel,
        grid=(T // tm,),
        in_specs=[pl.BlockSpec((tm, D), lambda i: (i, 0)), pl.BlockSpec((1, D), lambda i: (0, 0))],
        out_specs=pl.BlockSpec((tm, D), lambda i: (i, 0)),
        out_shape=jax.ShapeDtypeStruct((T, D), out_dtype),
        compiler_params=_cparams(("parallel",)),
        name="rmsnorm",
    )(x, g.reshape(1, D))


def _proj_kernel(h_ref, w_ref, o_ref, wbf_ref, *, n_scaled, scale):
    j = pl.program_id(0)

    @pl.when(pl.program_id(1) == 0)
    def _():
        wbf_ref[...] = w_ref[...].astype(BF16)

    acc = jnp.dot(h_ref[...], wbf_ref[...], preferred_element_type=F32)
    o_ref[...] = (acc * jnp.where(j < n_scaled, scale, 1.0)).astype(o_ref.dtype)


def _project(h, w, n_cols, n_scaled_cols, scale, out_dtype, tm=1024, tn=512):
    T, D = h.shape
    return pl.pallas_call(
        functools.partial(_proj_kernel, n_scaled=n_scaled_cols // tn, scale=scale),
        grid=(n_cols // tn, T // tm),
        in_specs=[pl.BlockSpec((tm, D), lambda j, i: (i, 0)),
                  pl.BlockSpec((D, tn), lambda j, i: (0, j))],
        out_specs=pl.BlockSpec((tm, tn), lambda j, i: (i, j)),
        out_shape=jax.ShapeDtypeStruct((T, n_cols), out_dtype),
        scratch_shapes=[pltpu.VMEM((D, tn), BF16)],
        compiler_params=_cparams(("arbitrary", "arbitrary")),
        name="qkv_proj",
    )(h, w)


def _forget_kernel(h_ref, wft_ref, bf_ref, c_ref):
    ft = lax.dot_general(wft_ref[...], h_ref[...], (((1,), (1,)), ((), ())),
                         preferred_element_type=F32)
    c = jax.nn.log_sigmoid(ft + bf_ref[...])
    S = c.shape[1]
    lane = lax.broadcasted_iota(I32, c.shape, 1)
    shift = 1
    while shift < S:
        c = c + jnp.where(lane >= shift, pltpu.roll(c, shift, axis=1), 0.0)
        shift *= 2
    c_ref[0] = c * LOG2E


def _forget_cumsum(h, wft, b_forget, B, S):
    T, D = h.shape
    H = wft.shape[0]
    return pl.pallas_call(
        _forget_kernel,
        grid=(B,),
        in_specs=[pl.BlockSpec((S, D), lambda b: (b, 0)),
                  pl.BlockSpec((H, D), lambda b: (0, 0)),
                  pl.BlockSpec((H, 1), lambda b: (0, 0))],
        out_specs=pl.BlockSpec((1, H, S), lambda b: (b, 0, 0)),
        out_shape=jax.ShapeDtypeStruct((B, H, S), F32),
        compiler_params=_cparams(("parallel",)),
        name="forget_cumsum",
    )(h, wft, b_forget.reshape(H, 1))


def _attn_kernel(q_ref, k_ref, v_ref, crow_ref, o_ref, vaug_ref, m_ref, acc_ref, *, n_heads, tq):
    i = pl.program_id(1)

    @pl.when(i == 0)
    def _():
        ones = jnp.ones((v_ref.shape[0], HEAD_DIM), BF16)
        for h in range(n_heads):
            vaug_ref[h, :, :HEAD_DIM] = v_ref[:, h * HEAD_DIM:(h + 1) * HEAD_DIM]
            vaug_ref[h, :, HEAD_DIM:] = ones

    m_ref[...] = jnp.full(m_ref.shape, -jnp.inf, F32)
    acc_ref[...] = jnp.zeros(acc_ref.shape, F32)
    row = lax.broadcasted_iota(I32, (tq, tq), 0)
    col = lax.broadcasted_iota(I32, (tq, tq), 1)
    causal = col <= row

    def step(j, masked):
        keys = pl.ds(pl.multiple_of(j * tq, tq), tq)
        for h in range(n_heads):
            hs = slice(h * HEAD_DIM, (h + 1) * HEAD_DIM)
            s = lax.dot_general(q_ref[:, hs], k_ref[keys, hs], (((1,), (1,)), ((), ())),
                                preferred_element_type=F32) - crow_ref[0, h, j]
            if masked:
                s = jnp.where(causal, s, -jnp.inf)
            m_old = m_ref[h]
            m_new = jnp.maximum(m_old, jnp.max(s, axis=-1, keepdims=True))
            alpha = jnp.exp2(m_old - m_new)
            p = jnp.exp2(s - jnp.concatenate([m_new] * (tq // HEAD_DIM), axis=1))
            m_ref[h] = m_new
            pv = jnp.dot(p.astype(BF16), vaug_ref[h, keys, :], preferred_element_type=F32)
            acc_ref[h] = jnp.concatenate([alpha, alpha], axis=1) * acc_ref[h] + pv

    def body(j, _):
        step(j, False)
        return 0

    lax.fori_loop(0, i, body, 0)
    step(i, True)
    for h in range(n_heads):
        acc = acc_ref[h]
        o_ref[:, h * HEAD_DIM:(h + 1) * HEAD_DIM] = (acc[:, :HEAD_DIM] / acc[:, HEAD_DIM:]).astype(o_ref.dtype)


def _attention(qkv, c_row, B, S, n_heads, tq=256):
    T = qkv.shape[0]
    W = n_heads * HEAD_DIM
    nq = S // tq
    c_row5 = c_row.reshape(B, n_heads, nq, 1, tq)
    return pl.pallas_call(
        functools.partial(_attn_kernel, n_heads=n_heads, tq=tq),
        grid=(B, nq),
        in_specs=[pl.BlockSpec((tq, W), lambda b, i: (b * nq + i, 0)),
                  pl.BlockSpec((S, W), lambda b, i: (b, 1)),
                  pl.BlockSpec((S, W), lambda b, i: (b, 2)),
                  pl.BlockSpec((1, n_heads, nq, 1, tq), lambda b, i: (b, 0, 0, 0, 0))],
        out_specs=pl.BlockSpec((tq, W), lambda b, i: (b * nq + i, 0)),
        out_shape=jax.ShapeDtypeStruct((T, W), BF16),
        scratch_shapes=[pltpu.VMEM((n_heads, S, 2 * HEAD_DIM), BF16),
                        pltpu.VMEM((n_heads, tq, HEAD_DIM), F32),
                        pltpu.VMEM((n_heads, tq, 2 * HEAD_DIM), F32)],
        compiler_params=_cparams(("arbitrary", "arbitrary")),
        name="fox_attention",
    )(qkv, qkv, qkv, c_row5)


def _gmlp_kernel(h_ref, wz_ref, g_ref, b_ref, ws_ref, bst_ref, o_ref, *, n_groups):
    z = jnp.dot(h_ref[...], wz_ref[...], preferred_element_type=F32)
    z = 0.5 * z * (1.0 + lax.erf(z * (1.0 / math.sqrt(2.0))))
    W = z.shape[1] // 2
    u = z[:, :W]
    v = z[:, W:]
    mu = jnp.mean(v, axis=-1, keepdims=True)
    var = jnp.mean(jnp.square(v - mu), axis=-1, keepdims=True)
    vn = (v - mu) * lax.rsqrt(var + NORM_EPS) * g_ref[...] + b_ref[...]
    row = lax.broadcasted_iota(I32, (CHUNK, CHUNK), 0)
    col = lax.broadcasted_iota(I32, (CHUNK, CHUNK), 1)
    tril = col <= row
    tg = z.shape[0]
    for g in range(n_groups):
        gs = slice(g * GROUP_DIM, (g + 1) * GROUP_DIM)
        wg = jnp.where(tril, ws_ref[g], 0.0).astype(BF16)
        bias = bst_ref[:, g:g + 1]
        for c in range(tg // CHUNK):
            cs = slice(c * CHUNK, (c + 1) * CHUNK)
            mixed = jnp.dot(wg, vn[cs, gs].astype(BF16), preferred_element_type=F32) + bias
            o_ref[cs, gs] = (u[cs, gs] * mixed).astype(o_ref.dtype)


def _gmlp(h, wz, g_v_ln, b_v_ln, w_spatial, b_spatial, tg=512):
    T, D = h.shape
    W2 = wz.shape[1]
    W = W2 // 2
    G = w_spatial.shape[0]
    return pl.pallas_call(
        functools.partial(_gmlp_kernel, n_groups=G),
        grid=(T // tg,),
        in_specs=[pl.BlockSpec((tg, D), lambda i: (i, 0)),
                  pl.BlockSpec((D, W2), lambda i: (0, 0)),
                  pl.BlockSpec((1, W), lambda i: (0, 0)),
                  pl.BlockSpec((1, W), lambda i: (0, 0)),
                  pl.BlockSpec((G, CHUNK, CHUNK), lambda i: (0, 0, 0)),
                  pl.BlockSpec((CHUNK, G), lambda i: (0, 0))],
        out_specs=pl.BlockSpec((tg, W), lambda i: (i, 0)),
        out_shape=jax.ShapeDtypeStruct((T, W), BF16),
        compiler_params=_cparams(("parallel",)),
        name="gmlp",
    )(h, wz, g_v_ln.reshape(1, W), b_v_ln.reshape(1, W), w_spatial, b_spatial.T)


def _merge_kernel(attn_ref, sg_ref, h_ref, wa_ref, wb_ref, wga_ref, wgb_ref, o_ref):
    h = h_ref[...]
    a = jnp.dot(attn_ref[...], wa_ref[...], preferred_element_type=F32)
    ga = jnp.dot(h, wga_ref[...], preferred_element_type=F32)
    m = jax.nn.sigmoid(ga) * a
    b = jnp.dot(sg_ref[...], wb_ref[...], preferred_element_type=F32)
    gb = jnp.dot(h, wgb_ref[...], preferred_element_type=F32)
    o_ref[...] = (m + jax.nn.sigmoid(gb) * b).astype(o_ref.dtype)


def _merge(attn, sg, h, wa, wb, wg, tm=512, tn=512):
    T, D = h.shape
    Wa = attn.shape[1]
    Wb = sg.shape[1]
    nt = D // tn
    return pl.pallas_call(
        _merge_kernel,
        grid=(nt, T // tm),
        in_specs=[pl.BlockSpec((tm, Wa), lambda j, i: (i, 0)),
                  pl.BlockSpec((tm, Wb), lambda j, i: (i, 0)),
                  pl.BlockSpec((tm, D), lambda j, i: (i, 0)),
                  pl.BlockSpec((Wa, tn), lambda j, i: (0, j)),
                  pl.BlockSpec((Wb, tn), lambda j, i: (0, j)),
                  pl.BlockSpec((D, tn), lambda j, i: (0, j)),
                  pl.BlockSpec((D, tn), lambda j, i: (0, j + nt))],
        out_specs=pl.BlockSpec((tm, tn), lambda j, i: (i, j)),
        out_shape=jax.ShapeDtypeStruct((T, D), BF16),
        compiler_params=_cparams(("arbitrary", "arbitrary")),
        name="gated_merge",
    )(attn, sg, h, wa, wb, wg, wg)


def _out_router_kernel(m_ref, x_ref, wo_ref, g_ref, wr_ref, br_ref,
                       x1_ref, h2_ref, sel_ref, gate_ref, *, n_parts):
    E = br_ref.shape[1]
    part = m_ref.shape[0] // n_parts
    for p in range(n_parts):
        rows = pl.ds(p * part, part)
        x1 = x_ref[rows, :] + jnp.dot(m_ref[rows, :], wo_ref[...], preferred_element_type=F32)
        x1_ref[rows, :] = x1
        ms = jnp.mean(x1 * x1, axis=-1, keepdims=True)
        h2 = x1 * lax.rsqrt(ms + NORM_EPS) * g_ref[...]
        h2_ref[rows, :] = _pack_pairs(h2)
        h2_hi = h2.astype(BF16)
        h2_lo = (h2 - h2_hi.astype(F32)).astype(BF16)
        pa = jnp.dot(h2_hi, wr_ref[...], preferred_element_type=F32)
        pb = jnp.dot(h2_lo, wr_ref[:, :E], preferred_element_type=F32)
        logits = pa[:, :E] + (pa[:, E:] + pb) + br_ref[...]
        lane = lax.broadcasted_iota(I32, logits.shape, 1)
        work = logits
        sel = jnp.zeros(logits.shape, F32)
        num = jnp.zeros(logits.shape, F32)
        denom = jnp.zeros((part, 1), F32)
        m0 = None
        for _ in range(TOP_K):
            m = jnp.max(work, axis=-1, keepdims=True)
            idx = jnp.min(jnp.where(work == m, lane, E), axis=-1, keepdims=True)
            onehot = lane == idx
            if m0 is None:
                m0 = m
            e = jnp.exp(m - m0)
            sel = jnp.where(onehot, 1.0, sel)
            num = jnp.where(onehot, e, num)
            denom = denom + e
            work = jnp.where(onehot, -jnp.inf, work)
        sel_ref[rows, :] = sel
        gate_ref[rows, :] = num / denom


def _out_router(merged, x, wo, g_ffn, w_router, b_router, to=512, n_parts=2):
    T, D = x.shape
    E = w_router.shape[1]
    w_hi = w_router.astype(BF16)
    w_lo = (w_router - w_hi.astype(F32)).astype(BF16)
    w_router = jnp.concatenate([w_hi, w_lo], axis=1)
    row = lambda i: (i, 0)
    fixed = lambda i: (0, 0)
    return pl.pallas_call(
        functools.partial(_out_router_kernel, n_parts=n_parts),
        grid=(T // to,),
        in_specs=[pl.BlockSpec((to, D), row), pl.BlockSpec((to, D), row),
                  pl.BlockSpec((D, D), fixed), pl.BlockSpec((1, D), fixed),
                  pl.BlockSpec((D, 2 * E), fixed), pl.BlockSpec((1, E), fixed)],
        out_specs=[pl.BlockSpec((to, D), row), pl.BlockSpec((to, D // 2), row),
                   pl.BlockSpec((to, E), row), pl.BlockSpec((to, E), row)],
        out_shape=[jax.ShapeDtypeStruct((T, D), F32), jax.ShapeDtypeStruct((T, D // 2), jnp.uint32),
                   jax.ShapeDtypeStruct((T, E), F32), jax.ShapeDtypeStruct((T, E), F32)],
        compiler_params=_cparams(("parallel",)),
        name="out_router",
    )(merged, x, wo, g_ffn.reshape(1, D), w_router, b_router.reshape(1, E))


def _routing_kernel(sel_ref, gate_ref, dest_ref, g4_ref, nblk_ref, start_ref, rank_ref, *, tile):
    T, E = sel_ref.shape
    nt = T // tile
    r = lax.broadcasted_iota(I32, (tile, tile), 0)
    c = lax.broadcasted_iota(I32, (tile, tile), 1)
    strict_lower = (c < r).astype(BF16)
    er = lax.broadcasted_iota(I32, (E, E), 0)
    ec = lax.broadcasted_iota(I32, (E, E), 1)
    strict_upper = (er < ec).astype(BF16)

    def pass1(t, carry):
        rows = pl.ds(pl.multiple_of(t * tile, tile), tile)
        a = sel_ref[rows, :]
        rank_ref[rows, :] = jnp.dot(strict_lower, a.astype(BF16), preferred_element_type=F32) + carry
        return carry + jnp.sum(a, axis=0, keepdims=True)

    counts = lax.fori_loop(0, nt, pass1, jnp.zeros((1, E), F32))
    nblk = jnp.floor((counts + (ROW_BLOCK - 1)) * (1.0 / ROW_BLOCK))
    start_blk = jnp.dot(nblk.astype(BF16), strict_upper, preferred_element_type=F32)
    nblk_ref[...] = nblk.astype(I32)
    start_ref[...] = start_blk.astype(I32)
    start_row = start_blk * float(ROW_BLOCK)
    lane = lax.broadcasted_iota(I32, (tile, 128), 1)

    def pass2(t, _):
        rows = pl.ds(pl.multiple_of(t * tile, tile), tile)
        a = sel_ref[rows, :]
        g = gate_ref[rows, :]
        dest_e = rank_ref[rows, :] + start_row
        slot = jnp.dot(a.astype(BF16), strict_upper, preferred_element_type=F32)
        d4 = jnp.zeros((tile, 128), F32)
        g4 = jnp.zeros((tile, 128), F32)
        for s in range(TOP_K):
            pick = (a > 0.5) & (slot == float(s))
            d4 = jnp.where(lane == s, jnp.sum(jnp.where(pick, dest_e, 0.0), axis=-1, keepdims=True), d4)
            g4 = jnp.where(lane == s, jnp.sum(jnp.where(pick, g, 0.0), axis=-1, keepdims=True), g4)
        dest_ref[rows, :] = d4.astype(I32)
        g4_ref[rows, :] = g4
        return 0

    lax.fori_loop(0, nt, pass2, 0)


def _routing(sel, gate, tile=256):
    T, E = sel.shape
    return pl.pallas_call(
        functools.partial(_routing_kernel, tile=tile),
        out_shape=[jax.ShapeDtypeStruct((T, 128), I32), jax.ShapeDtypeStruct((T, 128), F32),
                   jax.ShapeDtypeStruct((1, E), I32), jax.ShapeDtypeStruct((1, E), I32)],
        scratch_shapes=[pltpu.VMEM((T, E), F32)],
        compiler_params=pltpu.CompilerParams(vmem_limit_bytes=VMEM_LIMIT_BYTES),
        name="routing_ranks",
    )(sel, gate)


SC_CORES = 2
SC_SUBCORES = 16
SC_CHUNK = 32


def _sc_gather_rows(table, idx):
    n = idx.shape[0]
    W = table.shape[1]
    n_workers = SC_CORES * SC_SUBCORES
    per_worker = n // n_workers
    assert per_worker * n_workers == n and per_worker % SC_CHUNK == 0
    mesh = plsc.VectorSubcoreMesh(core_axis_name="c", subcore_axis_name="s",
                                  num_cores=SC_CORES, num_subcores=SC_SUBCORES)

    @functools.partial(
        pl.kernel, mesh=mesh,
        out_type=jax.ShapeDtypeStruct((n, W), table.dtype),
        scratch_types=[pltpu.VMEM((SC_CHUNK,), I32), pltpu.VMEM((SC_CHUNK, W), table.dtype),
                       pltpu.SemaphoreType.DMA],
        name="sc_gather_rows",
    )
    def gather(table_hbm, idx_hbm, out_hbm, idx_v, rows_v, sem):
        wid = lax.axis_index("s") * SC_CORES + lax.axis_index("c")
        base = wid * per_worker

        @pl.loop(0, per_worker // SC_CHUNK)
        def _(c):
            off = pl.multiple_of(base + c * SC_CHUNK, 8)
            pltpu.sync_copy(idx_hbm.at[pl.ds(off, SC_CHUNK)], idx_v)
            pltpu.async_copy(table_hbm.at[idx_v], rows_v, sem).wait()
            pltpu.sync_copy(rows_v, out_hbm.at[pl.ds(off, SC_CHUNK)])

    return gather(table, idx)


def _sc_scatter_rows(rows, idx, n_out, n_slots):
    T, W = rows.shape
    n_workers = SC_CORES * SC_SUBCORES
    per_worker = T // n_workers
    assert per_worker * n_workers == T and per_worker % SC_CHUNK == 0
    mesh = plsc.VectorSubcoreMesh(core_axis_name="c", subcore_axis_name="s",
                                  num_cores=SC_CORES, num_subcores=SC_SUBCORES)

    @functools.partial(
        pl.kernel, mesh=mesh,
        out_type=jax.ShapeDtypeStruct((n_out, W), rows.dtype),
        scratch_types=[pltpu.VMEM((SC_CHUNK,), I32), pltpu.VMEM((SC_CHUNK, W), rows.dtype)],
        name="sc_scatter_rows",
    )
    def scatter(rows_hbm, idx_hbm, out_hbm, idx_v, rows_v):
        wid = lax.axis_index("s") * SC_CORES + lax.axis_index("c")
        base = wid * per_worker

        @pl.loop(0, per_worker // SC_CHUNK)
        def _(c):
            off = pl.multiple_of(base + c * SC_CHUNK, 8)
            pltpu.sync_copy(rows_hbm.at[pl.ds(off, SC_CHUNK)], rows_v)
            for k in range(n_slots):
                pltpu.sync_copy(idx_hbm.at[pl.ds(pl.multiple_of(k * T + off, 8), SC_CHUNK)], idx_v)
                pltpu.sync_copy(rows_v, out_hbm.at[idx_v])

    return scatter(rows, idx)


def _expert_kernel(ie_ref, ib_ref, ins_ref,
                   xs_hbm, wup_ref, bup_ref, wdn_ref, bdn_ref, perm_ref, ys_hbm,
                   xg_ref, acc_ref, yst_ref, gsem, osem, *, n_items, n_ff_tiles):
    i = pl.program_id(0)
    j = pl.program_id(1)
    nsub = ins_ref[i]
    slot = i % 2

    def in_copy(item, dst_slot, s):
        src = pl.multiple_of((ib_ref[item] + s) * ROW_BLOCK, ROW_BLOCK)
        return pltpu.make_async_copy(xs_hbm.at[pl.ds(src, ROW_BLOCK)],
                                     xg_ref.at[dst_slot, pl.ds(s * ROW_BLOCK, ROW_BLOCK)],
                                     gsem.at[dst_slot])

    def fetch_item(item, n_sub, dst_slot):
        for s in range(ITEM_BLOCKS):
            @pl.when(s < n_sub)
            def _():
                in_copy(item, dst_slot, s).start()

    @pl.when(j == 0)
    def _():
        @pl.when(i == 0)
        def _():
            fetch_item(0, nsub, 0)

        nxt = jnp.minimum(i + 1, n_items - 1)
        fetch_item(nxt, jnp.where(i + 1 < n_items, ins_ref[nxt], 0), 1 - slot)
        for s in range(ITEM_BLOCKS):
            @pl.when(s < nsub)
            def _():
                in_copy(i, slot, s).wait()
                rows = pl.ds(s * ROW_BLOCK, ROW_BLOCK)
                acc_ref[rows, :] = jnp.broadcast_to(bdn_ref[0], (ROW_BLOCK, acc_ref.shape[1]))

    @pl.when(nsub > 0)
    def _():
        wup = wup_ref[0].astype(BF16)
        wdn = wdn_ref[0].astype(BF16)
        bup = bup_ref[0]
        perm = perm_ref[...]
        half = perm.shape[0] // 2

        def run_blocks(first, count):
            rows = pl.ds(first * ROW_BLOCK, count * ROW_BLOCK)
            xb = _unpack_pairs(xg_ref[slot, rows, :]).astype(BF16)
            gu = (jnp.dot(xb, wup, preferred_element_type=F32) + bup).astype(BF16)
            glu_parts, lin_parts = [], []
            for p in range(gu.shape[1] // perm.shape[0]):
                gp = jnp.dot(gu[:, p * perm.shape[0]:(p + 1) * perm.shape[0]], perm,
                             preferred_element_type=F32)
                glu_parts.append(gp[:, :half])
                lin_parts.append(gp[:, half:])
            x_glu = jnp.minimum(jnp.concatenate(glu_parts, axis=1), SWIGLU_LIMIT)
            x_lin = jnp.clip(jnp.concatenate(lin_parts, axis=1), -SWIGLU_LIMIT, SWIGLU_LIMIT)
            act = x_glu * jax.nn.sigmoid(SWIGLU_ALPHA * x_glu) * (x_lin + 1.0)
            acc_ref[rows, :] += jnp.dot(act.astype(BF16), wdn, preferred_element_type=F32)

        for q in range(ITEM_BLOCKS // 4):
            pl.when(nsub >= 4 * (q + 1))(functools.partial(run_blocks, 4 * q, 4))
        for q in range(ITEM_BLOCKS // 4 + 1):
            base = 4 * q
            if base + 2 <= ITEM_BLOCKS:
                pl.when((nsub // 4 == q) & (nsub % 4 >= 2))(functools.partial(run_blocks, base, 2))
            if base + 3 <= ITEM_BLOCKS:
                pl.when((nsub // 4 == q) & (nsub % 4 == 3))(functools.partial(run_blocks, base + 2, 1))
            if base + 1 <= ITEM_BLOCKS:
                pl.when((nsub // 4 == q) & (nsub % 4 == 1))(functools.partial(run_blocks, base, 1))

    def out_copy(item, s):
        dst = pl.multiple_of((ib_ref[item] + s) * ROW_BLOCK, ROW_BLOCK)
        return pltpu.make_async_copy(yst_ref.at[pl.ds(s * ROW_BLOCK, ROW_BLOCK)],
                                     ys_hbm.at[pl.ds(dst, ROW_BLOCK)], osem)

    @pl.when(j == n_ff_tiles - 1)
    def _():
        prev = jnp.maximum(i - 1, 0)
        prev_sub = jnp.where(i > 0, ins_ref[prev], 0)
        for s in range(ITEM_BLOCKS):
            @pl.when(s < prev_sub)
            def _():
                out_copy(prev, s).wait()
        for s in range(ITEM_BLOCKS):
            @pl.when(s < nsub)
            def _():
                rows = pl.ds(s * ROW_BLOCK, ROW_BLOCK)
                yst_ref[rows, :] = _pack_pairs(acc_ref[rows, :])
                out_copy(i, s).start()

        @pl.when(i == n_items - 1)
        def _():
            for s in range(ITEM_BLOCKS):
                @pl.when(s < nsub)
                def _():
                    out_copy(i, s).wait()


def _experts(xs, item_e, item_b, item_n, w_up, b_up, w_down, b_down):
    n_rows, Dp = xs.shape
    D = 2 * Dp
    E, _, F2 = w_up.shape
    F = F2 // 2
    J = F // FF_TILE
    n_items = item_e.shape[0]
    half = 128
    perm = np.zeros((2 * half, 2 * half), np.float32)
    perm[2 * np.arange(half), np.arange(half)] = 1.0
    perm[2 * np.arange(half) + 1, half + np.arange(half)] = 1.0

    def jj(i, j, ins):
        return jnp.where(ins[i] > 0, j, J - 1)

    grid_spec = pltpu.PrefetchScalarGridSpec(
        num_scalar_prefetch=3,
        grid=(n_items, J),
        in_specs=[pl.BlockSpec(memory_space=pl.ANY),
                  pl.BlockSpec((1, D, 2 * FF_TILE), lambda i, j, ie, ib, ins: (ie[i], 0, jj(i, j, ins))),
                  pl.BlockSpec((1, 1, 2 * FF_TILE), lambda i, j, ie, ib, ins: (ie[i], 0, jj(i, j, ins))),
                  pl.BlockSpec((1, FF_TILE, D), lambda i, j, ie, ib, ins: (ie[i], jj(i, j, ins), 0)),
                  pl.BlockSpec((1, 1, D), lambda i, j, ie, ib, ins: (ie[i], 0, 0)),
                  pl.BlockSpec((2 * half, 2 * half), lambda i, j, ie, ib, ins: (0, 0))],
        out_specs=pl.BlockSpec(memory_space=pl.ANY),
        scratch_shapes=[pltpu.VMEM((2, ITEM_ROWS, Dp), jnp.uint32),
                        pltpu.VMEM((ITEM_ROWS, D), F32),
                        pltpu.VMEM((ITEM_ROWS, Dp), jnp.uint32),
                        pltpu.SemaphoreType.DMA((2,)), pltpu.SemaphoreType.DMA(())],
    )
    return pl.pallas_call(
        functools.partial(_expert_kernel, n_items=n_items, n_ff_tiles=J),
        grid_spec=grid_spec,
        out_shape=jax.ShapeDtypeStruct((n_rows, Dp), jnp.uint32),
        compiler_params=_cparams(("arbitrary", "arbitrary")),
        name="expert_ffn",
    )(item_e, item_b, item_n,
      xs, w_up, b_up.reshape(E, 1, F2), w_down, b_down.reshape(E, 1, D), jnp.asarray(perm, BF16))


def _combine_kernel(*refs):
    yk_refs, (g4_ref, x1_ref, g_ref, o_ref) = refs[:TOP_K], refs[TOP_K:]
    y = x1_ref[...]
    g4 = g4_ref[...]
    for k in range(TOP_K):
        y = y + g4[:, k:k + 1] * _unpack_pairs(yk_refs[k][...])
    ms = jnp.mean(y * y, axis=-1, keepdims=True)
    o_ref[...] = y * lax.rsqrt(ms + NORM_EPS) * g_ref[...]


def _combine(yk, g4, x1, g_final, tc=256):
    T, D = x1.shape
    nt = T // tc
    slot_specs = [pl.BlockSpec((tc, D // 2), functools.partial(lambda i, k: (k * nt + i, 0), k=k))
                  for k in range(TOP_K)]
    return pl.pallas_call(
        _combine_kernel,
        grid=(nt,),
        in_specs=slot_specs + [pl.BlockSpec((tc, 128), lambda i: (i, 0)),
                               pl.BlockSpec((tc, D), lambda i: (i, 0)),
                               pl.BlockSpec((1, D), lambda i: (0, 0))],
        out_specs=pl.BlockSpec((tc, D), lambda i: (i, 0)),
        out_shape=jax.ShapeDtypeStruct((T, D), F32),
        compiler_params=_cparams(("parallel",)),
        name="combine_norm",
    )(*([yk] * TOP_K), g4, x1, g_final.reshape(1, D))


def _work_items(nblk, start_blk, n_items):
    E = nblk.shape[0]
    per_e = (nblk + ITEM_BLOCKS - 1) // ITEM_BLOCKS
    ends = jnp.cumsum(per_e)
    total = ends[-1]
    idx = jnp.arange(n_items, dtype=I32)
    e = jnp.minimum(jnp.searchsorted(ends, idx, side="right"), E - 1).astype(I32)
    local = idx - (ends[e] - per_e[e])
    active = idx < total
    last_e = e[jnp.maximum(total - 1, 0)]
    item_e = jnp.where(active, e, last_e).astype(I32)
    item_b = jnp.where(active, start_blk[e] + local * ITEM_BLOCKS, 0).astype(I32)
    item_n = jnp.where(active, jnp.clip(nblk[e] - local * ITEM_BLOCKS, 0, ITEM_BLOCKS), 0).astype(I32)
    return item_e, item_b, item_n


def kernel(x, g_mix, w_in, b_forget, g_v_ln, b_v_ln, w_spatial, b_spatial, w_branch_attn, w_branch_gmlp, w_out, g_ffn, w_router, b_router, w_expert_up, b_expert_up, w_expert_down, b_expert_down, g_final):
    B, S, D = x.shape
    T = B * S
    n_heads = b_forget.shape[0]
    attn_w = n_heads * HEAD_DIM
    gmlp_w = g_v_ln.shape[0]
    E = w_router.shape[1]
    off_f = 3 * attn_w
    off_z = off_f + n_heads
    off_g = off_z + 2 * gmlp_w

    x2 = x.reshape(T, D)
    wft = w_in[:, off_f:off_z].T.astype(BF16)
    wz = w_in[:, off_z:off_g].astype(BF16)
    wg = w_in[:, off_g:].astype(BF16)
    wa = w_branch_attn.astype(BF16)
    wb = w_branch_gmlp.astype(BF16)
    wo = w_out.astype(BF16)

    h = _rmsnorm(x2, g_mix, BF16)
    qkv = _project(h, w_in, 3 * attn_w, attn_w, LOG2E / math.sqrt(HEAD_DIM), BF16)
    c_row = _forget_cumsum(h, wft, b_forget, B, S)
    attn = _attention(qkv, c_row, B, S, n_heads)
    sg = _gmlp(h, wz, g_v_ln, b_v_ln, w_spatial, b_spatial)
    merged = _merge(attn, sg, h, wa, wb, wg)
    x1, h2p, sel, gate = _out_router(merged, x2, wo, g_ffn, w_router, b_router)

    dest4, g4, nblk, start_blk = _routing(sel, gate)
    n_rows = T * TOP_K + E * ROW_BLOCK
    dest_slots = dest4[:, :TOP_K].T.reshape(TOP_K * T)
    max_blocks = n_rows // ROW_BLOCK
    n_items = E + -(-(max_blocks - E) // ITEM_BLOCKS) + 1
    item_e, item_b, item_n = _work_items(nblk[0], start_blk[0], n_items)
    xs = _sc_scatter_rows(h2p, dest_slots, n_rows, TOP_K)
    ys = _experts(xs, item_e, item_b, item_n,
                  w_expert_up, b_expert_up, w_expert_down, b_expert_down)
    yk = _sc_gather_rows(ys, dest_slots)
    out = _combine(yk, g4, x1, g_final)
    return out.reshape(B, S, D)
```

```python
import functools
import math

import jax
import jax.numpy as jnp
import numpy as np
from jax import lax
from jax.experimental import pallas as pl
from jax.experimental.pallas import tpu as pltpu
from jax.experimental.pallas import tpu_sc as plsc

F32 = jnp.float32
BF16 = jnp.bfloat16
I32 = jnp.int32

NORM_EPS = 1e-5
HEAD_DIM = 128
CHUNK = 128
GROUP_DIM = 128
TOP_K = 4
SWIGLU_ALPHA = 1.702
SWIGLU_LIMIT = 7.0
LOG2E = math.log2(math.e)

VMEM_LIMIT_BYTES = 56 * 1024 * 1024

ROW_BLOCK = 256
ITEM_BLOCKS = 5
ITEM_ROWS = ITEM_BLOCKS * ROW_BLOCK
FF_TILE = 256
DMA_UNROLL = 8


def _cparams(sem, **kw):
    return pltpu.CompilerParams(dimension_semantics=sem, vmem_limit_bytes=VMEM_LIMIT_BYTES, **kw)


def _pack_pairs(x):
    c = x.shape[1] // 2
    hi = lax.bitcast_convert_type(x[:, :c].astype(BF16).astype(F32), jnp.uint32)
    lo = lax.bitcast_convert_type(x[:, c:].astype(BF16).astype(F32), jnp.uint32)
    return hi | (lo >> 16)


def _unpack_pairs(w):
    hi = lax.bitcast_convert_type(w & jnp.uint32(0xFFFF0000), F32)
    lo = lax.bitcast_convert_type(w << 16, F32)
    return jnp.concatenate([hi, lo], axis=1)


def _rmsnorm_kernel(x_ref, g_ref, o_ref):
    x = x_ref[...]
    ms = jnp.mean(x * x, axis=-1, keepdims=True)
    o_ref[...] = (x * lax.rsqrt(ms + NORM_EPS) * g_ref[...]).astype(o_ref.dtype)


def _rmsnorm(x, g, out_dtype, tm=512):
    T, D = x.shape
    return pl.pallas_call(
        _rmsnorm_kernel,
        grid=(T // tm,),
        in_specs=[pl.BlockSpec((tm, D), lambda i: (i, 0)), pl.BlockSpec((1, D), lambda i: (0, 0))],
        out_specs=pl.BlockSpec((tm, D), lambda i: (i, 0)),
        out_shape=jax.ShapeDtypeStruct((T, D), out_dtype),
        compiler_params=_cparams(("parallel",)),
        name="rmsnorm",
    )(x, g.reshape(1, D))


def _proj_kernel(h_ref, w_ref, o_ref, wbf_ref, *, n_scaled, scale):
    j = pl.program_id(0)

    @pl.when(pl.program_id(1) == 0)
    def _():
        wbf_ref[...] = w_ref[...].astype(BF16)

    acc = jnp.dot(h_ref[...], wbf_ref[...], preferred_element_type=F32)
    o_ref[...] = (acc * jnp.where(j < n_scaled, scale, 1.0)).astype(o_ref.dtype)


def _project(h, w, n_cols, n_scaled_cols, scale, out_dtype, tm=1024, tn=512):
    T, D = h.shape
    return pl.pallas_call(
        functools.partial(_proj_kernel, n_scaled=n_scaled_cols // tn, scale=scale),
        grid=(n_cols // tn, T // tm),
        in_specs=[pl.BlockSpec((tm, D), lambda j, i: (i, 0)),
                  pl.BlockSpec((D, tn), lambda j, i: (0, j))],
        out_specs=pl.BlockSpec((tm, tn), lambda j, i: (i, j)),
        out_shape=jax.ShapeDtypeStruct((T, n_cols), out_dtype),
        scratch_shapes=[pltpu.VMEM((D, tn), BF16)],
        compiler_params=_cparams(("arbitrary", "arbitrary")),
        name="qkv_proj",
    )(h, w)


def _forget_kernel(h_ref, wft_ref, bf_ref, c_ref):
    ft = lax.dot_general(wft_ref[...], h_ref[...], (((1,), (1,)), ((), ())),
                         preferred_element_type=F32)
    c = jax.nn.log_sigmoid(ft + bf_ref[...])
    S = c.shape[1]
    lane = lax.broadcasted_iota(I32, c.shape, 1)
    shift = 1
    while shift < S:
        c = c + jnp.where(lane >= shift, pltpu.roll(c, shift, axis=1), 0.0)
        shift *= 2
    c_ref[0] = c * LOG2E


def _forget_cumsum(h, wft, b_forget, B, S):
    T, D = h.shape
    H = wft.shape[0]
    return pl.pallas_call(
        _forget_kernel,
        grid=(B,),
        in_specs=[pl.BlockSpec((S, D), lambda b: (b, 0)),
                  pl.BlockSpec((H, D), lambda b: (0, 0)),
                  pl.BlockSpec((H, 1), lambda b: (0, 0))],
        out_specs=pl.BlockSpec((1, H, S), lambda b: (b, 0, 0)),
        out_shape=jax.ShapeDtypeStruct((B, H, S), F32),
        compiler_params=_cparams(("parallel",)),
        name="forget_cumsum",
    )(h, wft, b_forget.reshape(H, 1))


def _attn_kernel(q_ref, k_ref, v_ref, crow_ref, o_ref, vaug_ref, m_ref, acc_ref, *, n_heads, tq):
    i = pl.program_id(1)

    @pl.when(i == 0)
    def _():
        ones = jnp.ones((v_ref.shape[0], HEAD_DIM), BF16)
        for h in range(n_heads):
            vaug_ref[h, :, :HEAD_DIM] = v_ref[:, h * HEAD_DIM:(h + 1) * HEAD_DIM]
            vaug_ref[h, :, HEAD_DIM:] = ones

    m_ref[...] = jnp.full(m_ref.shape, -jnp.inf, F32)
    acc_ref[...] = jnp.zeros(acc_ref.shape, F32)
    row = lax.broadcasted_iota(I32, (tq, tq), 0)
    col = lax.broadcasted_iota(I32, (tq, tq), 1)
    causal = col <= row

    def step(j, masked):
        keys = pl.ds(pl.multiple_of(j * tq, tq), tq)
        for h in range(n_heads):
            hs = slice(h * HEAD_DIM, (h + 1) * HEAD_DIM)
            s = lax.dot_general(q_ref[:, hs], k_ref[keys, hs], (((1,), (1,)), ((), ())),
                                preferred_element_type=F32) - crow_ref[0, h, j]
            if masked:
                s = jnp.where(causal, s, -jnp.inf)
            m_old = m_ref[h]
            m_new = jnp.maximum(m_old, jnp.max(s, axis=-1, keepdims=True))
            alpha = jnp.exp2(m_old - m_new)
            p = jnp.exp2(s - jnp.concatenate([m_new] * (tq // HEAD_DIM), axis=1))
            m_ref[h] = m_new
            pv = jnp.dot(p.astype(BF16), vaug_ref[h, keys, :], preferred_element_type=F32)
            acc_ref[h] = jnp.concatenate([alpha, alpha], axis=1) * acc_ref[h] + pv

    def body(j, _):
        step(j, False)
        return 0

    lax.fori_loop(0, i, body, 0)
    step(i, True)
    for h in range(n_heads):
        acc = acc_ref[h]
        o_ref[:, h * HEAD_DIM:(h + 1) * HEAD_DIM] = (acc[:, :HEAD_DIM] / acc[:, HEAD_DIM:]).astype(o_ref.dtype)


def _attention(qkv, c_row, B, S, n_heads, tq=256):
    T = qkv.shape[0]
    W = n_heads * HEAD_DIM
    nq = S // tq
    c_row5 = c_row.reshape(B, n_heads, nq, 1, tq)
    return pl.pallas_call(
        functools.partial(_attn_kernel, n_heads=n_heads, tq=tq),
        grid=(B, nq),
        in_specs=[pl.BlockSpec((tq, W), lambda b, i: (b * nq + i, 0)),
                  pl.BlockSpec((S, W), lambda b, i: (b, 1)),
                  pl.BlockSpec((S, W), lambda b, i: (b, 2)),
                  pl.BlockSpec((1, n_heads, nq, 1, tq), lambda b, i: (b, 0, 0, 0, 0))],
        out_specs=pl.BlockSpec((tq, W), lambda b, i: (b * nq + i, 0)),
        out_shape=jax.ShapeDtypeStruct((T, W), BF16),
        scratch_shapes=[pltpu.VMEM((n_heads, S, 2 * HEAD_DIM), BF16),
                        pltpu.VMEM((n_heads, tq, HEAD_DIM), F32),
                        pltpu.VMEM((n_heads, tq, 2 * HEAD_DIM), F32)],
        compiler_params=_cparams(("arbitrary", "arbitrary")),
        name="fox_attention",
    )(qkv, qkv, qkv, c_row5)


def _gmlp_kernel(h_ref, wz_ref, g_ref, b_ref, ws_ref, bst_ref, o_ref, *, n_groups):
    z = jnp.dot(h_ref[...], wz_ref[...], preferred_element_type=F32)
    z = 0.5 * z * (1.0 + lax.erf(z * (1.0 / math.sqrt(2.0))))
    W = z.shape[1] // 2
    u = z[:, :W]
    v = z[:, W:]
    mu = jnp.mean(v, axis=-1, keepdims=True)
    var = jnp.mean(jnp.square(v - mu), axis=-1, keepdims=True)
    vn = (v - mu) * lax.rsqrt(var + NORM_EPS) * g_ref[...] + b_ref[...]
    row = lax.broadcasted_iota(I32, (CHUNK, CHUNK), 0)
    col = lax.broadcasted_iota(I32, (CHUNK, CHUNK), 1)
    tril = col <= row
    tg = z.shape[0]
    for g in range(n_groups):
        gs = slice(g * GROUP_DIM, (g + 1) * GROUP_DIM)
        wg = jnp.where(tril, ws_ref[g], 0.0).astype(BF16)
        bias = bst_ref[:, g:g + 1]
        for c in range(tg // CHUNK):
            cs = slice(c * CHUNK, (c + 1) * CHUNK)
            mixed = jnp.dot(wg, vn[cs, gs].astype(BF16), preferred_element_type=F32) + bias
            o_ref[cs, gs] = (u[cs, gs] * mixed).astype(o_ref.dtype)


def _gmlp(h, wz, g_v_ln, b_v_ln, w_spatial, b_spatial, tg=512):
    T, D = h.shape
    W2 = wz.shape[1]
    W = W2 // 2
    G = w_spatial.shape[0]
    return pl.pallas_call(
        functools.partial(_gmlp_kernel, n_groups=G),
        grid=(T // tg,),
        in_specs=[pl.BlockSpec((tg, D), lambda i: (i, 0)),
                  pl.BlockSpec((D, W2), lambda i: (0, 0)),
                  pl.BlockSpec((1, W), lambda i: (0, 0)),
                  pl.BlockSpec((1, W), lambda i: (0, 0)),
                  pl.BlockSpec((G, CHUNK, CHUNK), lambda i: (0, 0, 0)),
                  pl.BlockSpec((CHUNK, G), lambda i: (0, 0))],
        out_specs=pl.BlockSpec((tg, W), lambda i: (i, 0)),
        out_shape=jax.ShapeDtypeStruct((T, W), BF16),
        compiler_params=_cparams(("parallel",)),
        name="gmlp",
    )(h, wz, g_v_ln.reshape(1, W), b_v_ln.reshape(1, W), w_spatial, b_spatial.T)


def _merge_kernel(attn_ref, sg_ref, h_ref, wa_ref, wb_ref, wga_ref, wgb_ref, o_ref):
    h = h_ref[...]
    a = jnp.dot(attn_ref[...], wa_ref[...], preferred_element_type=F32)
    ga = jnp.dot(h, wga_ref[...], preferred_element_type=F32)
    m = jax.nn.sigmoid(ga) * a
    b = jnp.dot(sg_ref[...], wb_ref[...], preferred_element_type=F32)
    gb = jnp.dot(h, wgb_ref[...], preferred_element_type=F32)
    o_ref[...] = (m + jax.nn.sigmoid(gb) * b).astype(o_ref.dtype)


def _merge(attn, sg, h, wa, wb, wg, tm=512, tn=512):
    T, D = h.shape
    Wa = attn.shape[1]
    Wb = sg.shape[1]
    nt = D // tn
    return pl.pallas_call(
        _merge_kernel,
        grid=(nt, T // tm),
        in_specs=[pl.BlockSpec((tm, Wa), lambda j, i: (i, 0)),
                  pl.BlockSpec((tm, Wb), lambda j, i: (i, 0)),
                  pl.BlockSpec((tm, D), lambda j, i: (i, 0)),
                  pl.BlockSpec((Wa, tn), lambda j, i: (0, j)),
                  pl.BlockSpec((Wb, tn), lambda j, i: (0, j)),
                  pl.BlockSpec((D, tn), lambda j, i: (0, j)),
                  pl.BlockSpec((D, tn), lambda j, i: (0, j + nt))],
        out_specs=pl.BlockSpec((tm, tn), lambda j, i: (i, j)),
        out_shape=jax.ShapeDtypeStruct((T, D), BF16),
        compiler_params=_cparams(("arbitrary", "arbitrary")),
        name="gated_merge",
    )(attn, sg, h, wa, wb, wg, wg)


def _out_router_kernel(m_ref, x_ref, wo_ref, g_ref, wr_ref, br_ref,
                       x1_ref, h2_ref, sel_ref, gate_ref, *, n_parts):
    E = br_ref.shape[1]
    part = m_ref.shape[0] // n_parts
    for p in range(n_parts):
        rows = pl.ds(p * part, part)
        x1 = x_ref[rows, :] + jnp.dot(m_ref[rows, :], wo_ref[...], preferred_element_type=F32)
        x1_ref[rows, :] = x1
        ms = jnp.mean(x1 * x1, axis=-1, keepdims=True)
        h2 = x1 * lax.rsqrt(ms + NORM_EPS) * g_ref[...]
        h2_ref[rows, :] = _pack_pairs(h2)
        h2_hi = h2.astype(BF16)
        h2_lo = (h2 - h2_hi.astype(F32)).astype(BF16)
        pa = jnp.dot(h2_hi, wr_ref[...], preferred_element_type=F32)
        pb = jnp.dot(h2_lo, wr_ref[:, :E], preferred_element_type=F32)
        logits = pa[:, :E] + (pa[:, E:] + pb) + br_ref[...]
        lane = lax.broadcasted_iota(I32, logits.shape, 1)
        work = logits
        sel = jnp.zeros(logits.shape, F32)
        num = jnp.zeros(logits.shape, F32)
        denom = jnp.zeros((part, 1), F32)
        m0 = None
        for _ in range(TOP_K):
            m = jnp.max(work, axis=-1, keepdims=True)
            idx = jnp.min(jnp.where(work == m, lane, E), axis=-1, keepdims=True)
            onehot = lane == idx
            if m0 is None:
                m0 = m
            e = jnp.exp(m - m0)
            sel = jnp.where(onehot, 1.0, sel)
            num = jnp.where(onehot, e, num)
            denom = denom + e
            work = jnp.where(onehot, -jnp.inf, work)
        sel_ref[rows, :] = sel
        gate_ref[rows, :] = num / denom


def _out_router(merged, x, wo, g_ffn, w_router, b_router, to=512, n_parts=2):
    T, D = x.shape
    E = w_router.shape[1]
    w_hi = w_router.astype(BF16)
    w_lo = (w_router - w_hi.astype(F32)).astype(BF16)
    w_router = jnp.concatenate([w_hi, w_lo], axis=1)
    row = lambda i: (i, 0)
    fixed = lambda i: (0, 0)
    return pl.pallas_call(
        functools.partial(_out_router_kernel, n_parts=n_parts),
        grid=(T // to,),
        in_specs=[pl.BlockSpec((to, D), row), pl.BlockSpec((to, D), row),
                  pl.BlockSpec((D, D), fixed), pl.BlockSpec((1, D), fixed),
                  pl.BlockSpec((D, 2 * E), fixed), pl.BlockSpec((1, E), fixed)],
        out_specs=[pl.BlockSpec((to, D), row), pl.BlockSpec((to, D // 2), row),
                   pl.BlockSpec((to, E), row), pl.BlockSpec((to, E), row)],
        out_shape=[jax.ShapeDtypeStruct((T, D), F32), jax.ShapeDtypeStruct((T, D // 2), jnp.uint32),
                   jax.ShapeDtypeStruct((T, E), F32), jax.ShapeDtypeStruct((T, E), F32)],
        compiler_params=_cparams(("parallel",)),
        name="out_router",
    )(merged, x, wo, g_ffn.reshape(1, D), w_router, b_router.reshape(1, E))


def _routing_kernel(sel_ref, gate_ref, dest_ref, g4_ref, nblk_ref, start_ref, rank_ref, *, tile):
    T, E = sel_ref.shape
    nt = T // tile
    r = lax.broadcasted_iota(I32, (tile, tile), 0)
    c = lax.broadcasted_iota(I32, (tile, tile), 1)
    strict_lower = (c < r).astype(BF16)
    er = lax.broadcasted_iota(I32, (E, E), 0)
    ec = lax.broadcasted_iota(I32, (E, E), 1)
    strict_upper = (er < ec).astype(BF16)

    def pass1(t, carry):
        rows = pl.ds(pl.multiple_of(t * tile, tile), tile)
        a = sel_ref[rows, :]
        rank_ref[rows, :] = jnp.dot(strict_lower, a.astype(BF16), preferred_element_type=F32) + carry
        return carry + jnp.sum(a, axis=0, keepdims=True)

    counts = lax.fori_loop(0, nt, pass1, jnp.zeros((1, E), F32))
    nblk = jnp.floor((counts + (ROW_BLOCK - 1)) * (1.0 / ROW_BLOCK))
    start_blk = jnp.dot(nblk.astype(BF16), strict_upper, preferred_element_type=F32)
    nblk_ref[...] = nblk.astype(I32)
    start_ref[...] = start_blk.astype(I32)
    start_row = start_blk * float(ROW_BLOCK)
    lane = lax.broadcasted_iota(I32, (tile, 128), 1)

    def pass2(t, _):
        rows = pl.ds(pl.multiple_of(t * tile, tile), tile)
        a = sel_ref[rows, :]
        g = gate_ref[rows, :]
        dest_e = rank_ref[rows, :] + start_row
        slot = jnp.dot(a.astype(BF16), strict_upper, preferred_element_type=F32)
        d4 = jnp.zeros((tile, 128), F32)
        g4 = jnp.zeros((tile, 128), F32)
        for s in range(TOP_K):
            pick = (a > 0.5) & (slot == float(s))
            d4 = jnp.where(lane == s, jnp.sum(jnp.where(pick, dest_e, 0.0), axis=-1, keepdims=True), d4)
            g4 = jnp.where(lane == s, jnp.sum(jnp.where(pick, g, 0.0), axis=-1, keepdims=True), g4)
        dest_ref[rows, :] = d4.astype(I32)
        g4_ref[rows, :] = g4
        return 0

    lax.fori_loop(0, nt, pass2, 0)


def _routing(sel, gate, tile=256):
    T, E = sel.shape
    return pl.pallas_call(
        functools.partial(_routing_kernel, tile=tile),
        out_shape=[jax.ShapeDtypeStruct((T, 128), I32), jax.ShapeDtypeStruct((T, 128), F32),
                   jax.ShapeDtypeStruct((1, E), I32), jax.ShapeDtypeStruct((1, E), I32)],
        scratch_shapes=[pltpu.VMEM((T, E), F32)],
        compiler_params=pltpu.CompilerParams(vmem_limit_bytes=VMEM_LIMIT_BYTES),
        name="routing_ranks",
    )(sel, gate)


SC_CORES = 2
SC_SUBCORES = 16
SC_CHUNK = 32


def _sc_gather_rows(table, idx):
    n = idx.shape[0]
    W = table.shape[1]
    n_workers = SC_CORES * SC_SUBCORES
    per_worker = n // n_workers
    assert per_worker * n_workers == n and per_worker % SC_CHUNK == 0
    mesh = plsc.VectorSubcoreMesh(core_axis_name="c", subcore_axis_name="s",
                                  num_cores=SC_CORES, num_subcores=SC_SUBCORES)

    @functools.partial(
        pl.kernel, mesh=mesh,
        out_type=jax.ShapeDtypeStruct((n, W), table.dtype),
        scratch_types=[pltpu.VMEM((SC_CHUNK,), I32), pltpu.VMEM((SC_CHUNK, W), table.dtype),
                       pltpu.SemaphoreType.DMA],
        name="sc_gather_rows",
    )
    def gather(table_hbm, idx_hbm, out_hbm, idx_v, rows_v, sem):
        wid = lax.axis_index("s") * SC_CORES + lax.axis_index("c")
        base = wid * per_worker

        @pl.loop(0, per_worker // SC_CHUNK)
        def _(c):
            off = pl.multiple_of(base + c * SC_CHUNK, 8)
            pltpu.sync_copy(idx_hbm.at[pl.ds(off, SC_CHUNK)], idx_v)
            pltpu.async_copy(table_hbm.at[idx_v], rows_v, sem).wait()
            pltpu.sync_copy(rows_v, out_hbm.at[pl.ds(off, SC_CHUNK)])

    return gather(table, idx)


def _sc_scatter_rows(rows, idx, n_out, n_slots):
    T, W = rows.shape
    n_workers = SC_CORES * SC_SUBCORES
    per_worker = T // n_workers
    assert per_worker * n_workers == T and per_worker % SC_CHUNK == 0
    mesh = plsc.VectorSubcoreMesh(core_axis_name="c", subcore_axis_name="s",
                                  num_cores=SC_CORES, num_subcores=SC_SUBCORES)

    @functools.partial(
        pl.kernel, mesh=mesh,
        out_type=jax.ShapeDtypeStruct((n_out, W), rows.dtype),
        scratch_types=[pltpu.VMEM((SC_CHUNK,), I32), pltpu.VMEM((SC_CHUNK, W), rows.dtype)],
        name="sc_scatter_rows",
    )
    def scatter(rows_hbm, idx_hbm, out_hbm, idx_v, rows_v):
        wid = lax.axis_index("s") * SC_CORES + lax.axis_index("c")
        base = wid * per_worker

        @pl.loop(0, per_worker // SC_CHUNK)
        def _(c):
            off = pl.multiple_of(base + c * SC_CHUNK, 8)
            pltpu.sync_copy(rows_hbm.at[pl.ds(off, SC_CHUNK)], rows_v)
            for k in range(n_slots):
                pltpu.sync_copy(idx_hbm.at[pl.ds(pl.multiple_of(k * T + off, 8), SC_CHUNK)], idx_v)
                pltpu.sync_copy(rows_v, out_hbm.at[idx_v])

    return scatter(rows, idx)


def _expert_kernel(ie_ref, ib_ref, ins_ref,
                   xs_hbm, wup_ref, bup_ref, wdn_ref, bdn_ref, perm_ref, ys_hbm,
                   xg_ref, acc_ref, yst_ref, wupb_ref, wdnb_ref, gsem, osem, *, n_ff_tiles):
    i = pl.program_id(0)
    j = pl.program_id(1)
    n_items = pl.num_programs(0)
    nsub = ins_ref[i]
    slot = i % 2

    def in_copy(item, dst_slot, s):
        src = pl.multiple_of((ib_ref[item] + s) * ROW_BLOCK, ROW_BLOCK)
        return pltpu.make_async_copy(xs_hbm.at[pl.ds(src, ROW_BLOCK)],
                                     xg_ref.at[dst_slot, pl.ds(s * ROW_BLOCK, ROW_BLOCK)],
                                     gsem.at[dst_slot])

    def fetch_item(item, n_sub, dst_slot):
        for s in range(ITEM_BLOCKS):
            @pl.when(s < n_sub)
            def _():
                in_copy(item, dst_slot, s).start()

    @pl.when(j == 0)
    def _():
        @pl.when(i == 0)
        def _():
            fetch_item(0, nsub, 0)

        nxt = jnp.minimum(i + 1, n_items - 1)
        fetch_item(nxt, jnp.where(i + 1 < n_items, ins_ref[nxt], 0), 1 - slot)
        for s in range(ITEM_BLOCKS):
            @pl.when(s < nsub)
            def _():
                in_copy(i, slot, s).wait()
                rows = pl.ds(s * ROW_BLOCK, ROW_BLOCK)
                acc_ref[rows, :] = jnp.broadcast_to(bdn_ref[0], (ROW_BLOCK, acc_ref.shape[1]))

    @pl.when(nsub > 0)
    def _():
        bup = bup_ref[0]
        perm = perm_ref[...]
        half = perm.shape[0] // 2

        def run_blocks(first, count):
            if first == 0:
                wup = wup_ref[0].astype(BF16)
                wdn = wdn_ref[0].astype(BF16)
                wupb_ref[...] = wup
                wdnb_ref[...] = wdn
            else:
                wup = wupb_ref[...]
                wdn = wdnb_ref[...]
            rows = pl.ds(first * ROW_BLOCK, count * ROW_BLOCK)
            xb = _unpack_pairs(xg_ref[slot, rows, :]).astype(BF16)
            gu = (jnp.dot(xb, wup, preferred_element_type=F32) + bup).astype(BF16)
            glu_parts, lin_parts = [], []
            for p in range(gu.shape[1] // perm.shape[0]):
                gp = jnp.dot(gu[:, p * perm.shape[0]:(p + 1) * perm.shape[0]], perm,
                             preferred_element_type=F32)
                glu_parts.append(gp[:, :half])
                lin_parts.append(gp[:, half:])
            x_glu = jnp.minimum(jnp.concatenate(glu_parts, axis=1), SWIGLU_LIMIT)
            x_lin = jnp.clip(jnp.concatenate(lin_parts, axis=1), -SWIGLU_LIMIT, SWIGLU_LIMIT)
            act = x_glu * jax.nn.sigmoid(SWIGLU_ALPHA * x_glu) * (x_lin + 1.0)
            acc_ref[rows, :] += jnp.dot(act.astype(BF16), wdn, preferred_element_type=F32)

        for q in range(ITEM_BLOCKS // 4):
            pl.when(nsub >= 4 * (q + 1))(functools.partial(run_blocks, 4 * q, 4))
        for q in range(ITEM_BLOCKS // 4 + 1):
            base = 4 * q
            if base + 2 <= ITEM_BLOCKS:
                pl.when((nsub // 4 == q) & (nsub % 4 >= 2))(functools.partial(run_blocks, base, 2))
            if base + 3 <= ITEM_BLOCKS:
                pl.when((nsub // 4 == q) & (nsub % 4 == 3))(functools.partial(run_blocks, base + 2, 1))
            if base + 1 <= ITEM_BLOCKS:
                pl.when((nsub // 4 == q) & (nsub % 4 == 1))(functools.partial(run_blocks, base, 1))

    def out_copy(item, s):
        dst = pl.multiple_of((ib_ref[item] + s) * ROW_BLOCK, ROW_BLOCK)
        return pltpu.make_async_copy(yst_ref.at[pl.ds(s * ROW_BLOCK, ROW_BLOCK)],
                                     ys_hbm.at[pl.ds(dst, ROW_BLOCK)], osem)

    @pl.when(j == n_ff_tiles - 1)
    def _():
        prev = jnp.maximum(i - 1, 0)
        prev_sub = jnp.where(i > 0, ins_ref[prev], 0)
        for s in range(ITEM_BLOCKS):
            @pl.when(s < prev_sub)
            def _():
                out_copy(prev, s).wait()
        for s in range(ITEM_BLOCKS):
            @pl.when(s < nsub)
            def _():
                rows = pl.ds(s * ROW_BLOCK, ROW_BLOCK)
                yst_ref[rows, :] = _pack_pairs(acc_ref[rows, :])
                out_copy(i, s).start()

        @pl.when(i == n_items - 1)
        def _():
            for s in range(ITEM_BLOCKS):
                @pl.when(s < nsub)
                def _():
                    out_copy(i, s).wait()


def _experts(xs, item_e, item_b, item_n, n_active, w_up, b_up, w_down, b_down):
    n_rows, Dp = xs.shape
    D = 2 * Dp
    E, _, F2 = w_up.shape
    F = F2 // 2
    J = F // FF_TILE
    half = 128
    perm = np.zeros((2 * half, 2 * half), np.float32)
    perm[2 * np.arange(half), np.arange(half)] = 1.0
    perm[2 * np.arange(half) + 1, half + np.arange(half)] = 1.0

    def jj(i, j, ins):
        return jnp.where(ins[i] > 0, j, J - 1)

    grid_spec = pltpu.PrefetchScalarGridSpec(
        num_scalar_prefetch=3,
        grid=(n_active, J),
        in_specs=[pl.BlockSpec(memory_space=pl.ANY),
                  pl.BlockSpec((1, D, 2 * FF_TILE), lambda i, j, ie, ib, ins: (ie[i], 0, jj(i, j, ins))),
                  pl.BlockSpec((1, 1, 2 * FF_TILE), lambda i, j, ie, ib, ins: (ie[i], 0, jj(i, j, ins))),
                  pl.BlockSpec((1, FF_TILE, D), lambda i, j, ie, ib, ins: (ie[i], jj(i, j, ins), 0)),
                  pl.BlockSpec((1, 1, D), lambda i, j, ie, ib, ins: (ie[i], 0, 0)),
                  pl.BlockSpec((2 * half, 2 * half), lambda i, j, ie, ib, ins: (0, 0))],
        out_specs=pl.BlockSpec(memory_space=pl.ANY),
        scratch_shapes=[pltpu.VMEM((2, ITEM_ROWS, Dp), jnp.uint32),
                        pltpu.VMEM((ITEM_ROWS, D), F32),
                        pltpu.VMEM((ITEM_ROWS, Dp), jnp.uint32),
                        pltpu.VMEM((D, 2 * FF_TILE), BF16),
                        pltpu.VMEM((FF_TILE, D), BF16),
                        pltpu.SemaphoreType.DMA((2,)), pltpu.SemaphoreType.DMA(())],
    )
    return pl.pallas_call(
        functools.partial(_expert_kernel, n_ff_tiles=J),
        grid_spec=grid_spec,
        out_shape=jax.ShapeDtypeStruct((n_rows, Dp), jnp.uint32),
        compiler_params=_cparams(("arbitrary", "arbitrary")),
        name="expert_ffn",
    )(item_e, item_b, item_n,
      xs, w_up, b_up.reshape(E, 1, F2), w_down, b_down.reshape(E, 1, D), jnp.asarray(perm, BF16))


def _combine_kernel(*refs):
    yk_refs, (g4_ref, x1_ref, g_ref, o_ref) = refs[:TOP_K], refs[TOP_K:]
    y = x1_ref[...]
    g4 = g4_ref[...]
    for k in range(TOP_K):
        y = y + g4[:, k:k + 1] * _unpack_pairs(yk_refs[k][...])
    ms = jnp.mean(y * y, axis=-1, keepdims=True)
    o_ref[...] = y * lax.rsqrt(ms + NORM_EPS) * g_ref[...]


def _combine(yk, g4, x1, g_final, tc=256):
    T, D = x1.shape
    nt = T // tc
    slot_specs = [pl.BlockSpec((tc, D // 2), functools.partial(lambda i, k: (k * nt + i, 0), k=k))
                  for k in range(TOP_K)]
    return pl.pallas_call(
        _combine_kernel,
        grid=(nt,),
        in_specs=slot_specs + [pl.BlockSpec((tc, 128), lambda i: (i, 0)),
                               pl.BlockSpec((tc, D), lambda i: (i, 0)),
                               pl.BlockSpec((1, D), lambda i: (0, 0))],
        out_specs=pl.BlockSpec((tc, D), lambda i: (i, 0)),
        out_shape=jax.ShapeDtypeStruct((T, D), F32),
        compiler_params=_cparams(("parallel",)),
        name="combine_norm",
    )(*([yk] * TOP_K), g4, x1, g_final.reshape(1, D))


def _work_items(nblk, start_blk, n_items):
    E = nblk.shape[0]
    per_e = (nblk + ITEM_BLOCKS - 1) // ITEM_BLOCKS
    ends = jnp.cumsum(per_e)
    total = ends[-1]
    idx = jnp.arange(n_items, dtype=I32)
    e = jnp.minimum(jnp.searchsorted(ends, idx, side="right"), E - 1).astype(I32)
    local = idx - (ends[e] - per_e[e])
    active = idx < total
    last_e = e[jnp.maximum(total - 1, 0)]
    item_e = jnp.where(active, e, last_e).astype(I32)
    item_b = jnp.where(active, start_blk[e] + local * ITEM_BLOCKS, 0).astype(I32)
    item_n = jnp.where(active, jnp.clip(nblk[e] - local * ITEM_BLOCKS, 0, ITEM_BLOCKS), 0).astype(I32)
    return item_e, item_b, item_n, jnp.maximum(total, 1).astype(I32)


def kernel(x, g_mix, w_in, b_forget, g_v_ln, b_v_ln, w_spatial, b_spatial, w_branch_attn, w_branch_gmlp, w_out, g_ffn, w_router, b_router, w_expert_up, b_expert_up, w_expert_down, b_expert_down, g_final):
    B, S, D = x.shape
    T = B * S
    n_heads = b_forget.shape[0]
    attn_w = n_heads * HEAD_DIM
    gmlp_w = g_v_ln.shape[0]
    E = w_router.shape[1]
    off_f = 3 * attn_w
    off_z = off_f + n_heads
    off_g = off_z + 2 * gmlp_w

    x2 = x.reshape(T, D)
    wft = w_in[:, off_f:off_z].T.astype(BF16)
    wz = w_in[:, off_z:off_g].astype(BF16)
    wg = w_in[:, off_g:].astype(BF16)
    wa = w_branch_attn.astype(BF16)
    wb = w_branch_gmlp.astype(BF16)
    wo = w_out.astype(BF16)

    h = _rmsnorm(x2, g_mix, BF16)
    qkv = _project(h, w_in, 3 * attn_w, attn_w, LOG2E / math.sqrt(HEAD_DIM), BF16)
    c_row = _forget_cumsum(h, wft, b_forget, B, S)
    attn = _attention(qkv, c_row, B, S, n_heads)
    sg = _gmlp(h, wz, g_v_ln, b_v_ln, w_spatial, b_spatial)
    merged = _merge(attn, sg, h, wa, wb, wg)
    x1, h2p, sel, gate = _out_router(merged, x2, wo, g_ffn, w_router, b_router)

    dest4, g4, nblk, start_blk = _routing(sel, gate)
    n_rows = T * TOP_K + E * ROW_BLOCK
    dest_slots = dest4[:, :TOP_K].T.reshape(TOP_K * T)
    max_blocks = n_rows // ROW_BLOCK
    n_items = E + -(-(max_blocks - E) // ITEM_BLOCKS) + 1
    item_e, item_b, item_n, n_active = _work_items(nblk[0], start_blk[0], n_items)
    xs = _sc_scatter_rows(h2p, dest_slots, n_rows, TOP_K)
    ys = _experts(xs, item_e, item_b, item_n, n_active,
                  w_expert_up, b_expert_up, w_expert_down, b_expert_down)
    yk = _sc_gather_rows(ys, dest_slots)
    out = _combine(yk, g4, x1, g_final)
    return out.reshape(B, S, D)
```

```python
import functools
import math

import jax
import jax.numpy as jnp
import numpy as np
from jax import lax
from jax.experimental import pallas as pl
from jax.experimental.pallas import tpu as pltpu
from jax.experimental.pallas import tpu_sc as plsc

F32 = jnp.float32
BF16 = jnp.bfloat16
I32 = jnp.int32

NORM_EPS = 1e-5
LANES = 128
HEAD_DIM = 128
CHUNK = 128
GROUP_DIM = 128
TOP_K = 4
SWIGLU_ALPHA = 1.702
SWIGLU_LIMIT = 7.0
LOG2E = math.log2(math.e)

VMEM_LIMIT_BYTES = 56 * 1024 * 1024

ROW_BLOCK = 256
ITEM_BLOCKS = 5
ITEM_ROWS = ITEM_BLOCKS * ROW_BLOCK
FF_TILE = 256
DMA_UNROLL = 8


def _cparams(sem, **kw):
    return pltpu.CompilerParams(dimension_semantics=sem, vmem_limit_bytes=VMEM_LIMIT_BYTES, **kw)


def _pack_pairs(x):
    c = x.shape[1] // 2
    hi = lax.bitcast_convert_type(x[:, :c].astype(BF16).astype(F32), jnp.uint32)
    lo = lax.bitcast_convert_type(x[:, c:].astype(BF16).astype(F32), jnp.uint32)
    return hi | (lo >> 16)


def _unpack_pairs(w):
    hi = lax.bitcast_convert_type(w & jnp.uint32(0xFFFF0000), F32)
    lo = lax.bitcast_convert_type(w << 16, F32)
    return jnp.concatenate([hi, lo], axis=1)


def _rmsnorm_kernel(x_ref, g_ref, o_ref):
    x = x_ref[...]
    ms = jnp.mean(x * x, axis=-1, keepdims=True)
    o_ref[...] = (x * lax.rsqrt(ms + NORM_EPS) * g_ref[...]).astype(o_ref.dtype)


def _rmsnorm(x, g, out_dtype, tm=512):
    T, D = x.shape
    return pl.pallas_call(
        _rmsnorm_kernel,
        grid=(T // tm,),
        in_specs=[pl.BlockSpec((tm, D), lambda i: (i, 0)), pl.BlockSpec((1, D), lambda i: (0, 0))],
        out_specs=pl.BlockSpec((tm, D), lambda i: (i, 0)),
        out_shape=jax.ShapeDtypeStruct((T, D), out_dtype),
        compiler_params=_cparams(("parallel",)),
        name="rmsnorm",
    )(x, g.reshape(1, D))


def _proj_kernel(h_ref, w_ref, o_ref, wbf_ref, *, n_scaled, scale):
    j = pl.program_id(0)

    @pl.when(pl.program_id(1) == 0)
    def _():
        wbf_ref[...] = w_ref[...].astype(BF16)

    acc = jnp.dot(h_ref[...], wbf_ref[...], preferred_element_type=F32)
    o_ref[...] = (acc * jnp.where(j < n_scaled, scale, 1.0)).astype(o_ref.dtype)


def _project(h, w, n_cols, n_scaled_cols, scale, out_dtype, tm=1024, tn=512):
    T, D = h.shape
    return pl.pallas_call(
        functools.partial(_proj_kernel, n_scaled=n_scaled_cols // tn, scale=scale),
        grid=(n_cols // tn, T // tm),
        in_specs=[pl.BlockSpec((tm, D), lambda j, i: (i, 0)),
                  pl.BlockSpec((D, tn), lambda j, i: (0, j))],
        out_specs=pl.BlockSpec((tm, tn), lambda j, i: (i, j)),
        out_shape=jax.ShapeDtypeStruct((T, n_cols), out_dtype),
        scratch_shapes=[pltpu.VMEM((D, tn), BF16)],
        compiler_params=_cparams(("arbitrary", "arbitrary")),
        name="qkv_proj",
    )(h, w)


def _shift_cast_kernel(main_ref, next_ref, o_ref, *, shift):
    full = jnp.concatenate([main_ref[...], next_ref[...]], axis=1)
    o_ref[...] = full[:, shift:shift + o_ref.shape[1]].astype(o_ref.dtype)


def _shifted_columns_bf16(w, start, n_cols, tr=256, tc=1024):
    R = w.shape[0]
    base = start // LANES * LANES
    shift = start - base
    assert base % tc == 0 and n_cols % tc == 0 and R % tr == 0 and 0 < shift < LANES
    return pl.pallas_call(
        functools.partial(_shift_cast_kernel, shift=shift),
        grid=(R // tr, n_cols // tc),
        in_specs=[pl.BlockSpec((tr, tc), lambda i, c: (i, base // tc + c)),
                  pl.BlockSpec((tr, LANES), lambda i, c: (i, (base + (c + 1) * tc) // LANES))],
        out_specs=pl.BlockSpec((tr, tc), lambda i, c: (i, c)),
        out_shape=jax.ShapeDtypeStruct((R, n_cols), BF16),
        compiler_params=_cparams(("parallel", "parallel")),
        name="shift_cast_columns",
    )(w, w)


def _forget_kernel(h_ref, wft_ref, bf_ref, c_ref):
    ft = lax.dot_general(wft_ref[...], h_ref[...], (((1,), (1,)), ((), ())),
                         preferred_element_type=F32)
    c = jax.nn.log_sigmoid(ft + bf_ref[...])
    S = c.shape[1]
    lane = lax.broadcasted_iota(I32, c.shape, 1)
    shift = 1
    while shift < S:
        c = c + jnp.where(lane >= shift, pltpu.roll(c, shift, axis=1), 0.0)
        shift *= 2
    c_ref[0] = c * LOG2E


def _forget_cumsum(h, wft, b_forget, B, S):
    T, D = h.shape
    H = wft.shape[0]
    return pl.pallas_call(
        _forget_kernel,
        grid=(B,),
        in_specs=[pl.BlockSpec((S, D), lambda b: (b, 0)),
                  pl.BlockSpec((H, D), lambda b: (0, 0)),
                  pl.BlockSpec((H, 1), lambda b: (0, 0))],
        out_specs=pl.BlockSpec((1, H, S), lambda b: (b, 0, 0)),
        out_shape=jax.ShapeDtypeStruct((B, H, S), F32),
        compiler_params=_cparams(("parallel",)),
        name="forget_cumsum",
    )(h, wft, b_forget.reshape(H, 1))


def _attn_kernel(q_ref, k_ref, v_ref, crow_ref, o_ref, vaug_ref, m_ref, acc_ref, *, n_heads, tq):
    i = pl.program_id(1)

    @pl.when(i == 0)
    def _():
        ones = jnp.ones((v_ref.shape[0], HEAD_DIM), BF16)
        for h in range(n_heads):
            vaug_ref[h, :, :HEAD_DIM] = v_ref[:, h * HEAD_DIM:(h + 1) * HEAD_DIM]
            vaug_ref[h, :, HEAD_DIM:] = ones

    m_ref[...] = jnp.full(m_ref.shape, -jnp.inf, F32)
    acc_ref[...] = jnp.zeros(acc_ref.shape, F32)
    row = lax.broadcasted_iota(I32, (tq, tq), 0)
    col = lax.broadcasted_iota(I32, (tq, tq), 1)
    causal = col <= row

    def step(j, masked):
        keys = pl.ds(pl.multiple_of(j * tq, tq), tq)
        for h in range(n_heads):
            hs = slice(h * HEAD_DIM, (h + 1) * HEAD_DIM)
            s = lax.dot_general(q_ref[:, hs], k_ref[keys, hs], (((1,), (1,)), ((), ())),
                                preferred_element_type=F32) - crow_ref[0, h, j]
            if masked:
                s = jnp.where(causal, s, -jnp.inf)
            m_old = m_ref[h]
            m_new = jnp.maximum(m_old, jnp.max(s, axis=-1, keepdims=True))
            alpha = jnp.exp2(m_old - m_new)
            p = jnp.exp2(s - jnp.concatenate([m_new] * (tq // HEAD_DIM), axis=1))
            m_ref[h] = m_new
            pv = jnp.dot(p.astype(BF16), vaug_ref[h, keys, :], preferred_element_type=F32)
            acc_ref[h] = jnp.concatenate([alpha, alpha], axis=1) * acc_ref[h] + pv

    def body(j, _):
        step(j, False)
        return 0

    lax.fori_loop(0, i, body, 0)
    step(i, True)
    for h in range(n_heads):
        acc = acc_ref[h]
        o_ref[:, h * HEAD_DIM:(h + 1) * HEAD_DIM] = (acc[:, :HEAD_DIM] / acc[:, HEAD_DIM:]).astype(o_ref.dtype)


def _attention(qkv, c_row, B, S, n_heads, tq=256):
    T = qkv.shape[0]
    W = n_heads * HEAD_DIM
    nq = S // tq
    c_row5 = c_row.reshape(B, n_heads, nq, 1, tq)
    return pl.pallas_call(
        functools.partial(_attn_kernel, n_heads=n_heads, tq=tq),
        grid=(B, nq),
        in_specs=[pl.BlockSpec((tq, W), lambda b, i: (b * nq + i, 0)),
                  pl.BlockSpec((S, W), lambda b, i: (b, 1)),
                  pl.BlockSpec((S, W), lambda b, i: (b, 2)),
                  pl.BlockSpec((1, n_heads, nq, 1, tq), lambda b, i: (b, 0, 0, 0, 0))],
        out_specs=pl.BlockSpec((tq, W), lambda b, i: (b * nq + i, 0)),
        out_shape=jax.ShapeDtypeStruct((T, W), BF16),
        scratch_shapes=[pltpu.VMEM((n_heads, S, 2 * HEAD_DIM), BF16),
                        pltpu.VMEM((n_heads, tq, HEAD_DIM), F32),
                        pltpu.VMEM((n_heads, tq, 2 * HEAD_DIM), F32)],
        compiler_params=_cparams(("arbitrary", "arbitrary")),
        name="fox_attention",
    )(qkv, qkv, qkv, c_row5)


def _gmlp_kernel(h_ref, wz_ref, g_ref, b_ref, ws_ref, bst_ref, o_ref, *, n_groups):
    z = jnp.dot(h_ref[...], wz_ref[...], preferred_element_type=F32)
    z = 0.5 * z * (1.0 + lax.erf(z * (1.0 / math.sqrt(2.0))))
    W = z.shape[1] // 2
    u = z[:, :W]
    v = z[:, W:]
    mu = jnp.mean(v, axis=-1, keepdims=True)
    var = jnp.mean(jnp.square(v - mu), axis=-1, keepdims=True)
    vn = (v - mu) * lax.rsqrt(var + NORM_EPS) * g_ref[...] + b_ref[...]
    row = lax.broadcasted_iota(I32, (CHUNK, CHUNK), 0)
    col = lax.broadcasted_iota(I32, (CHUNK, CHUNK), 1)
    tril = col <= row
    tg = z.shape[0]
    for g in range(n_groups):
        gs = slice(g * GROUP_DIM, (g + 1) * GROUP_DIM)
        wg = jnp.where(tril, ws_ref[g], 0.0).astype(BF16)
        bias = bst_ref[:, g:g + 1]
        for c in range(tg // CHUNK):
            cs = slice(c * CHUNK, (c + 1) * CHUNK)
            mixed = jnp.dot(wg, vn[cs, gs].astype(BF16), preferred_element_type=F32) + bias
            o_ref[cs, gs] = (u[cs, gs] * mixed).astype(o_ref.dtype)


def _gmlp(h, wz, g_v_ln, b_v_ln, w_spatial, b_spatial, tg=512):
    T, D = h.shape
    W = g_v_ln.shape[0]
    W2 = 2 * W
    G = w_spatial.shape[0]
    return pl.pallas_call(
        functools.partial(_gmlp_kernel, n_groups=G),
        grid=(T // tg,),
        in_specs=[pl.BlockSpec((tg, D), lambda i: (i, 0)),
                  pl.BlockSpec((D, W2), lambda i: (0, 0)),
                  pl.BlockSpec((1, W), lambda i: (0, 0)),
                  pl.BlockSpec((1, W), lambda i: (0, 0)),
                  pl.BlockSpec((G, CHUNK, CHUNK), lambda i: (0, 0, 0)),
                  pl.BlockSpec((CHUNK, G), lambda i: (0, 0))],
        out_specs=pl.BlockSpec((tg, W), lambda i: (i, 0)),
        out_shape=jax.ShapeDtypeStruct((T, W), BF16),
        compiler_params=_cparams(("parallel",)),
        name="gmlp",
    )(h, wz, g_v_ln.reshape(1, W), b_v_ln.reshape(1, W), w_spatial, b_spatial.T)


def _merge_kernel(attn_ref, sg_ref, h_ref, wa_ref, wb_ref, wga_ref, wgb_ref, o_ref):
    h = h_ref[...]
    a = jnp.dot(attn_ref[...], wa_ref[...], preferred_element_type=F32)
    ga = jnp.dot(h, wga_ref[...], preferred_element_type=F32)
    m = jax.nn.sigmoid(ga) * a
    b = jnp.dot(sg_ref[...], wb_ref[...], preferred_element_type=F32)
    gb = jnp.dot(h, wgb_ref[...], preferred_element_type=F32)
    o_ref[...] = (m + jax.nn.sigmoid(gb) * b).astype(o_ref.dtype)


def _merge(attn, sg, h, wa, wb, wg, g_off, tm=512, tn=512):
    T, D = h.shape
    Wa = attn.shape[1]
    Wb = sg.shape[1]
    nt = D // tn
    g0 = g_off // tn
    return pl.pallas_call(
        _merge_kernel,
        grid=(nt, T // tm),
        in_specs=[pl.BlockSpec((tm, Wa), lambda j, i: (i, 0)),
                  pl.BlockSpec((tm, Wb), lambda j, i: (i, 0)),
                  pl.BlockSpec((tm, D), lambda j, i: (i, 0)),
                  pl.BlockSpec((Wa, tn), lambda j, i: (0, j)),
                  pl.BlockSpec((Wb, tn), lambda j, i: (0, j)),
                  pl.BlockSpec((D, tn), lambda j, i: (0, g0 + j)),
                  pl.BlockSpec((D, tn), lambda j, i: (0, g0 + nt + j))],
        out_specs=pl.BlockSpec((tm, tn), lambda j, i: (i, j)),
        out_shape=jax.ShapeDtypeStruct((T, D), BF16),
        compiler_params=_cparams(("arbitrary", "arbitrary")),
        name="gated_merge",
    )(attn, sg, h, wa, wb, wg, wg)


def _out_router_kernel(m_ref, x_ref, wo_ref, g_ref, wr_ref, br_ref,
                       x1_ref, h2_ref, sel_ref, gate_ref, *, n_parts):
    E = br_ref.shape[1]
    part = m_ref.shape[0] // n_parts
    for p in range(n_parts):
        rows = pl.ds(p * part, part)
        x1 = x_ref[rows, :] + jnp.dot(m_ref[rows, :], wo_ref[...], preferred_element_type=F32)
        x1_ref[rows, :] = x1
        ms = jnp.mean(x1 * x1, axis=-1, keepdims=True)
        h2 = x1 * lax.rsqrt(ms + NORM_EPS) * g_ref[...]
        h2_ref[rows, :] = _pack_pairs(h2)
        h2_hi = h2.astype(BF16)
        h2_lo = (h2 - h2_hi.astype(F32)).astype(BF16)
        pa = jnp.dot(h2_hi, wr_ref[...], preferred_element_type=F32)
        pb = jnp.dot(h2_lo, wr_ref[:, :E], preferred_element_type=F32)
        logits = pa[:, :E] + (pa[:, E:] + pb) + br_ref[...]
        lane = lax.broadcasted_iota(I32, logits.shape, 1)
        work = logits
        sel = jnp.zeros(logits.shape, F32)
        num = jnp.zeros(logits.shape, F32)
        denom = jnp.zeros((part, 1), F32)
        m0 = None
        for _ in range(TOP_K):
            m = jnp.max(work, axis=-1, keepdims=True)
            idx = jnp.min(jnp.where(work == m, lane, E), axis=-1, keepdims=True)
            onehot = lane == idx
            if m0 is None:
                m0 = m
            e = jnp.exp(m - m0)
            sel = jnp.where(onehot, 1.0, sel)
            num = jnp.where(onehot, e, num)
            denom = denom + e
            work = jnp.where(onehot, -jnp.inf, work)
        sel_ref[rows, :] = sel
        gate_ref[rows, :] = num / denom


def _out_router(merged, x, wo, g_ffn, w_router, b_router, to=512, n_parts=2):
    T, D = x.shape
    E = w_router.shape[1]
    w_hi = w_router.astype(BF16)
    w_lo = (w_router - w_hi.astype(F32)).astype(BF16)
    w_router = jnp.concatenate([w_hi, w_lo], axis=1)
    row = lambda i: (i, 0)
    fixed = lambda i: (0, 0)
    return pl.pallas_call(
        functools.partial(_out_router_kernel, n_parts=n_parts),
        grid=(T // to,),
        in_specs=[pl.BlockSpec((to, D), row), pl.BlockSpec((to, D), row),
                  pl.BlockSpec((D, D), fixed), pl.BlockSpec((1, D), fixed),
                  pl.BlockSpec((D, 2 * E), fixed), pl.BlockSpec((1, E), fixed)],
        out_specs=[pl.BlockSpec((to, D), row), pl.BlockSpec((to, D // 2), row),
                   pl.BlockSpec((to, E), row), pl.BlockSpec((to, E), row)],
        out_shape=[jax.ShapeDtypeStruct((T, D), F32), jax.ShapeDtypeStruct((T, D // 2), jnp.uint32),
                   jax.ShapeDtypeStruct((T, E), F32), jax.ShapeDtypeStruct((T, E), F32)],
        compiler_params=_cparams(("parallel",)),
        name="out_router",
    )(merged, x, wo, g_ffn.reshape(1, D), w_router, b_router.reshape(1, E))


def _routing_kernel(sel_ref, gate_ref, dest_ref, g4_ref, nblk_ref, start_ref, rank_ref, *, tile):
    T, E = sel_ref.shape
    nt = T // tile
    r = lax.broadcasted_iota(I32, (tile, tile), 0)
    c = lax.broadcasted_iota(I32, (tile, tile), 1)
    strict_lower = (c < r).astype(BF16)
    er = lax.broadcasted_iota(I32, (E, E), 0)
    ec = lax.broadcasted_iota(I32, (E, E), 1)
    strict_upper = (er < ec).astype(BF16)

    def pass1(t, carry):
        rows = pl.ds(pl.multiple_of(t * tile, tile), tile)
        a = sel_ref[rows, :]
        rank_ref[rows, :] = jnp.dot(strict_lower, a.astype(BF16), preferred_element_type=F32) + carry
        return carry + jnp.sum(a, axis=0, keepdims=True)

    counts = lax.fori_loop(0, nt, pass1, jnp.zeros((1, E), F32))
    nblk = jnp.floor((counts + (ROW_BLOCK - 1)) * (1.0 / ROW_BLOCK))
    start_blk = jnp.dot(nblk.astype(BF16), strict_upper, preferred_element_type=F32)
    nblk_ref[...] = nblk.astype(I32)
    start_ref[...] = start_blk.astype(I32)
    start_row = start_blk * float(ROW_BLOCK)
    lane = lax.broadcasted_iota(I32, (tile, 128), 1)

    def pass2(t, _):
        rows = pl.ds(pl.multiple_of(t * tile, tile), tile)
        a = sel_ref[rows, :]
        g = gate_ref[rows, :]
        dest_e = rank_ref[rows, :] + start_row
        slot = jnp.dot(a.astype(BF16), strict_upper, preferred_element_type=F32)
        d4 = jnp.zeros((tile, 128), F32)
        g4 = jnp.zeros((tile, 128), F32)
        for s in range(TOP_K):
            pick = (a > 0.5) & (slot == float(s))
            d4 = jnp.where(lane == s, jnp.sum(jnp.where(pick, dest_e, 0.0), axis=-1, keepdims=True), d4)
            g4 = jnp.where(lane == s, jnp.sum(jnp.where(pick, g, 0.0), axis=-1, keepdims=True), g4)
        dest_ref[rows, :] = d4.astype(I32)
        g4_ref[rows, :] = g4
        return 0

    lax.fori_loop(0, nt, pass2, 0)


def _routing(sel, gate, tile=256):
    T, E = sel.shape
    return pl.pallas_call(
        functools.partial(_routing_kernel, tile=tile),
        out_shape=[jax.ShapeDtypeStruct((T, 128), I32), jax.ShapeDtypeStruct((T, 128), F32),
                   jax.ShapeDtypeStruct((1, E), I32), jax.ShapeDtypeStruct((1, E), I32)],
        scratch_shapes=[pltpu.VMEM((T, E), F32)],
        compiler_params=pltpu.CompilerParams(vmem_limit_bytes=VMEM_LIMIT_BYTES),
        name="routing_ranks",
    )(sel, gate)


SC_CORES = 2
SC_SUBCORES = 16
SC_CHUNK = 32


def _sc_gather_rows(table, idx):
    n = idx.shape[0]
    W = table.shape[1]
    n_workers = SC_CORES * SC_SUBCORES
    per_worker = n // n_workers
    assert per_worker * n_workers == n and per_worker % SC_CHUNK == 0
    mesh = plsc.VectorSubcoreMesh(core_axis_name="c", subcore_axis_name="s",
                                  num_cores=SC_CORES, num_subcores=SC_SUBCORES)

    @functools.partial(
        pl.kernel, mesh=mesh,
        out_type=jax.ShapeDtypeStruct((n, W), table.dtype),
        scratch_types=[pltpu.VMEM((SC_CHUNK,), I32), pltpu.VMEM((SC_CHUNK, W), table.dtype),
                       pltpu.SemaphoreType.DMA],
        name="sc_gather_rows",
    )
    def gather(table_hbm, idx_hbm, out_hbm, idx_v, rows_v, sem):
        wid = lax.axis_index("s") * SC_CORES + lax.axis_index("c")
        base = wid * per_worker

        @pl.loop(0, per_worker // SC_CHUNK)
        def _(c):
            off = pl.multiple_of(base + c * SC_CHUNK, 8)
            pltpu.sync_copy(idx_hbm.at[pl.ds(off, SC_CHUNK)], idx_v)
            pltpu.async_copy(table_hbm.at[idx_v], rows_v, sem).wait()
            pltpu.sync_copy(rows_v, out_hbm.at[pl.ds(off, SC_CHUNK)])

    return gather(table, idx)


def _sc_scatter_rows(rows, idx, n_out, n_slots):
    T, W = rows.shape
    n_workers = SC_CORES * SC_SUBCORES
    per_worker = T // n_workers
    assert per_worker * n_workers == T and per_worker % SC_CHUNK == 0
    mesh = plsc.VectorSubcoreMesh(core_axis_name="c", subcore_axis_name="s",
                                  num_cores=SC_CORES, num_subcores=SC_SUBCORES)

    @functools.partial(
        pl.kernel, mesh=mesh,
        out_type=jax.ShapeDtypeStruct((n_out, W), rows.dtype),
        scratch_types=[pltpu.VMEM((SC_CHUNK,), I32), pltpu.VMEM((SC_CHUNK, W), rows.dtype)],
        name="sc_scatter_rows",
    )
    def scatter(rows_hbm, idx_hbm, out_hbm, idx_v, rows_v):
        wid = lax.axis_index("s") * SC_CORES + lax.axis_index("c")
        base = wid * per_worker

        @pl.loop(0, per_worker // SC_CHUNK)
        def _(c):
            off = pl.multiple_of(base + c * SC_CHUNK, 8)
            pltpu.sync_copy(rows_hbm.at[pl.ds(off, SC_CHUNK)], rows_v)
            for k in range(n_slots):
                pltpu.sync_copy(idx_hbm.at[pl.ds(pl.multiple_of(k * T + off, 8), SC_CHUNK)], idx_v)
                pltpu.sync_copy(rows_v, out_hbm.at[idx_v])

    return scatter(rows, idx)


def _expert_kernel(ie_ref, ib_ref, ins_ref,
                   xs_hbm, wup_ref, bup_ref, wdn_ref, bdn_ref, perm_ref, ys_hbm,
                   xg_ref, acc_ref, yst_ref, wupb_ref, wdnb_ref, gsem, osem, *, n_ff_tiles):
    i = pl.program_id(0)
    j = pl.program_id(1)
    n_items = pl.num_programs(0)
    nsub = ins_ref[i]
    slot = i % 2

    def in_copy(item, dst_slot, s):
        src = pl.multiple_of((ib_ref[item] + s) * ROW_BLOCK, ROW_BLOCK)
        return pltpu.make_async_copy(xs_hbm.at[pl.ds(src, ROW_BLOCK)],
                                     xg_ref.at[dst_slot, pl.ds(s * ROW_BLOCK, ROW_BLOCK)],
                                     gsem.at[dst_slot])

    def fetch_item(item, n_sub, dst_slot):
        for s in range(ITEM_BLOCKS):
            @pl.when(s < n_sub)
            def _():
                in_copy(item, dst_slot, s).start()

    @pl.when(j == 0)
    def _():
        @pl.when(i == 0)
        def _():
            fetch_item(0, nsub, 0)

        nxt = jnp.minimum(i + 1, n_items - 1)
        fetch_item(nxt, jnp.where(i + 1 < n_items, ins_ref[nxt], 0), 1 - slot)
        for s in range(ITEM_BLOCKS):
            @pl.when(s < nsub)
            def _():
                in_copy(i, slot, s).wait()
                rows = pl.ds(s * ROW_BLOCK, ROW_BLOCK)
                acc_ref[rows, :] = jnp.broadcast_to(bdn_ref[0], (ROW_BLOCK, acc_ref.shape[1]))

    @pl.when(nsub > 0)
    def _():
        bup = bup_ref[0]
        perm = perm_ref[...]
        half = perm.shape[0] // 2

        def run_blocks(first, count):
            if first == 0:
                wup = wup_ref[0].astype(BF16)
                wdn = wdn_ref[0].astype(BF16)
                wupb_ref[...] = wup
                wdnb_ref[...] = wdn
            else:
                wup = wupb_ref[...]
                wdn = wdnb_ref[...]
            rows = pl.ds(first * ROW_BLOCK, count * ROW_BLOCK)
            xb = _unpack_pairs(xg_ref[slot, rows, :]).astype(BF16)
            gu = (jnp.dot(xb, wup, preferred_element_type=F32) + bup).astype(BF16)
            glu_parts, lin_parts = [], []
            for p in range(gu.shape[1] // perm.shape[0]):
                gp = jnp.dot(gu[:, p * perm.shape[0]:(p + 1) * perm.shape[0]], perm,
                             preferred_element_type=F32)
                glu_parts.append(gp[:, :half])
                lin_parts.append(gp[:, half:])
            x_glu = jnp.minimum(jnp.concatenate(glu_parts, axis=1), SWIGLU_LIMIT)
            x_lin = jnp.clip(jnp.concatenate(lin_parts, axis=1), -SWIGLU_LIMIT, SWIGLU_LIMIT)
            act = x_glu * jax.nn.sigmoid(SWIGLU_ALPHA * x_glu) * (x_lin + 1.0)
            acc_ref[rows, :] += jnp.dot(act.astype(BF16), wdn, preferred_element_type=F32)

        for q in range(ITEM_BLOCKS // 4):
            pl.when(nsub >= 4 * (q + 1))(functools.partial(run_blocks, 4 * q, 4))
        for q in range(ITEM_BLOCKS // 4 + 1):
            base = 4 * q
            if base + 2 <= ITEM_BLOCKS:
                pl.when((nsub // 4 == q) & (nsub % 4 >= 2))(functools.partial(run_blocks, base, 2))
            if base + 3 <= ITEM_BLOCKS:
                pl.when((nsub // 4 == q) & (nsub % 4 == 3))(functools.partial(run_blocks, base + 2, 1))
            if base + 1 <= ITEM_BLOCKS:
                pl.when((nsub // 4 == q) & (nsub % 4 == 1))(functools.partial(run_blocks, base, 1))

    def out_copy(item, s):
        dst = pl.multiple_of((ib_ref[item] + s) * ROW_BLOCK, ROW_BLOCK)
        return pltpu.make_async_copy(yst_ref.at[pl.ds(s * ROW_BLOCK, ROW_BLOCK)],
                                     ys_hbm.at[pl.ds(dst, ROW_BLOCK)], osem)

    @pl.when(j == n_ff_tiles - 1)
    def _():
        prev = jnp.maximum(i - 1, 0)
        prev_sub = jnp.where(i > 0, ins_ref[prev], 0)
        for s in range(ITEM_BLOCKS):
            @pl.when(s < prev_sub)
            def _():
                out_copy(prev, s).wait()
        for s in range(ITEM_BLOCKS):
            @pl.when(s < nsub)
            def _():
                rows = pl.ds(s * ROW_BLOCK, ROW_BLOCK)
                yst_ref[rows, :] = _pack_pairs(acc_ref[rows, :])
                out_copy(i, s).start()

        @pl.when(i == n_items - 1)
        def _():
            for s in range(ITEM_BLOCKS):
                @pl.when(s < nsub)
                def _():
                    out_copy(i, s).wait()


def _experts(xs, item_e, item_b, item_n, n_active, w_up, b_up, w_down, b_down):
    n_rows, Dp = xs.shape
    D = 2 * Dp
    E, _, F2 = w_up.shape
    F = F2 // 2
    J = F // FF_TILE
    half = 128
    perm = np.zeros((2 * half, 2 * half), np.float32)
    perm[2 * np.arange(half), np.arange(half)] = 1.0
    perm[2 * np.arange(half) + 1, half + np.arange(half)] = 1.0

    def jj(i, j, ins):
        return jnp.where(ins[i] > 0, j, J - 1)

    grid_spec = pltpu.PrefetchScalarGridSpec(
        num_scalar_prefetch=3,
        grid=(n_active, J),
        in_specs=[pl.BlockSpec(memory_space=pl.ANY),
                  pl.BlockSpec((1, D, 2 * FF_TILE), lambda i, j, ie, ib, ins: (ie[i], 0, jj(i, j, ins))),
                  pl.BlockSpec((1, 1, 2 * FF_TILE), lambda i, j, ie, ib, ins: (ie[i], 0, jj(i, j, ins))),
                  pl.BlockSpec((1, FF_TILE, D), lambda i, j, ie, ib, ins: (ie[i], jj(i, j, ins), 0)),
                  pl.BlockSpec((1, 1, D), lambda i, j, ie, ib, ins: (ie[i], 0, 0)),
                  pl.BlockSpec((2 * half, 2 * half), lambda i, j, ie, ib, ins: (0, 0))],
        out_specs=pl.BlockSpec(memory_space=pl.ANY),
        scratch_shapes=[pltpu.VMEM((2, ITEM_ROWS, Dp), jnp.uint32),
                        pltpu.VMEM((ITEM_ROWS, D), F32),
                        pltpu.VMEM((ITEM_ROWS, Dp), jnp.uint32),
                        pltpu.VMEM((D, 2 * FF_TILE), BF16),
                        pltpu.VMEM((FF_TILE, D), BF16),
                        pltpu.SemaphoreType.DMA((2,)), pltpu.SemaphoreType.DMA(())],
    )
    return pl.pallas_call(
        functools.partial(_expert_kernel, n_ff_tiles=J),
        grid_spec=grid_spec,
        out_shape=jax.ShapeDtypeStruct((n_rows, Dp), jnp.uint32),
        compiler_params=_cparams(("arbitrary", "arbitrary")),
        name="expert_ffn",
    )(item_e, item_b, item_n,
      xs, w_up, b_up.reshape(E, 1, F2), w_down, b_down.reshape(E, 1, D), jnp.asarray(perm, BF16))


def _combine_kernel(*refs):
    yk_refs, (g4_ref, x1_ref, g_ref, o_ref) = refs[:TOP_K], refs[TOP_K:]
    y = x1_ref[...]
    g4 = g4_ref[...]
    for k in range(TOP_K):
        y = y + g4[:, k:k + 1] * _unpack_pairs(yk_refs[k][...])
    ms = jnp.mean(y * y, axis=-1, keepdims=True)
    o_ref[...] = y * lax.rsqrt(ms + NORM_EPS) * g_ref[...]


def _combine(yk, g4, x1, g_final, tc=256):
    T, D = x1.shape
    nt = T // tc
    slot_specs = [pl.BlockSpec((tc, D // 2), functools.partial(lambda i, k: (k * nt + i, 0), k=k))
                  for k in range(TOP_K)]
    return pl.pallas_call(
        _combine_kernel,
        grid=(nt,),
        in_specs=slot_specs + [pl.BlockSpec((tc, 128), lambda i: (i, 0)),
                               pl.BlockSpec((tc, D), lambda i: (i, 0)),
                               pl.BlockSpec((1, D), lambda i: (0, 0))],
        out_specs=pl.BlockSpec((tc, D), lambda i: (i, 0)),
        out_shape=jax.ShapeDtypeStruct((T, D), F32),
        compiler_params=_cparams(("parallel",)),
        name="combine_norm",
    )(*([yk] * TOP_K), g4, x1, g_final.reshape(1, D))


def _work_items(nblk, start_blk, n_items):
    E = nblk.shape[0]
    per_e = (nblk + ITEM_BLOCKS - 1) // ITEM_BLOCKS
    ends = jnp.cumsum(per_e)
    total = ends[-1]
    idx = jnp.arange(n_items, dtype=I32)
    e = jnp.minimum(jnp.searchsorted(ends, idx, side="right"), E - 1).astype(I32)
    local = idx - (ends[e] - per_e[e])
    active = idx < total
    last_e = e[jnp.maximum(total - 1, 0)]
    item_e = jnp.where(active, e, last_e).astype(I32)
    item_b = jnp.where(active, start_blk[e] + local * ITEM_BLOCKS, 0).astype(I32)
    item_n = jnp.where(active, jnp.clip(nblk[e] - local * ITEM_BLOCKS, 0, ITEM_BLOCKS), 0).astype(I32)
    return item_e, item_b, item_n, jnp.maximum(total, 1).astype(I32)


def kernel(x, g_mix, w_in, b_forget, g_v_ln, b_v_ln, w_spatial, b_spatial, w_branch_attn, w_branch_gmlp, w_out, g_ffn, w_router, b_router, w_expert_up, b_expert_up, w_expert_down, b_expert_down, g_final):
    B, S, D = x.shape
    T = B * S
    n_heads = b_forget.shape[0]
    attn_w = n_heads * HEAD_DIM
    gmlp_w = g_v_ln.shape[0]
    E = w_router.shape[1]
    off_f = 3 * attn_w
    off_z = off_f + n_heads
    off_g = off_z + 2 * gmlp_w

    x2 = x.reshape(T, D)
    wft = w_in[:, off_f:off_z].T.astype(BF16)
    wzg = _shifted_columns_bf16(w_in, off_z, w_in.shape[1] - off_z)
    wa = w_branch_attn.astype(BF16)
    wb = w_branch_gmlp.astype(BF16)
    wo = w_out.astype(BF16)

    h = _rmsnorm(x2, g_mix, BF16)
    qkv = _project(h, w_in, 3 * attn_w, attn_w, LOG2E / math.sqrt(HEAD_DIM), BF16)
    c_row = _forget_cumsum(h, wft, b_forget, B, S)
    attn = _attention(qkv, c_row, B, S, n_heads)
    sg = _gmlp(h, wzg, g_v_ln, b_v_ln, w_spatial, b_spatial)
    merged = _merge(attn, sg, h, wa, wb, wzg, off_g - off_z)
    x1, h2p, sel, gate = _out_router(merged, x2, wo, g_ffn, w_router, b_router)

    dest4, g4, nblk, start_blk = _routing(sel, gate)
    n_rows = T * TOP_K + E * ROW_BLOCK
    dest_slots = dest4[:, :TOP_K].T.reshape(TOP_K * T)
    max_blocks = n_rows // ROW_BLOCK
    n_items = E + -(-(max_blocks - E) // ITEM_BLOCKS) + 1
    item_e, item_b, item_n, n_active = _work_items(nblk[0], start_blk[0], n_items)
    xs = _sc_scatter_rows(h2p, dest_slots, n_rows, TOP_K)
    ys = _experts(xs, item_e, item_b, item_n, n_active,
                  w_expert_up, b_expert_up, w_expert_down, b_expert_down)
    yk = _sc_gather_rows(ys, dest_slots)
    out = _combine(yk, g4, x1, g_final)
    return out.reshape(B, S, D)
```

```python
import functools
import math

import jax
import jax.numpy as jnp
import numpy as np
from jax import lax
from jax.experimental import pallas as pl
from jax.experimental.pallas import tpu as pltpu
from jax.experimental.pallas import tpu_sc as plsc

F32 = jnp.float32
BF16 = jnp.bfloat16
I32 = jnp.int32

NORM_EPS = 1e-5
LANES = 128
HEAD_DIM = 128
CHUNK = 128
GROUP_DIM = 128
TOP_K = 4
SWIGLU_ALPHA = 1.702
SWIGLU_LIMIT = 7.0
LOG2E = math.log2(math.e)

VMEM_LIMIT_BYTES = 56 * 1024 * 1024

ROW_BLOCK = 64
ITEM_BLOCKS = 20
ITEM_ROWS = ITEM_BLOCKS * ROW_BLOCK
REGION_BLOCKS = (16, 8, 4, 2, 1)
assert sum(REGION_BLOCKS) >= ITEM_BLOCKS
FF_TILE = 256
DMA_UNROLL = 8


def _cparams(sem, **kw):
    return pltpu.CompilerParams(dimension_semantics=sem, vmem_limit_bytes=VMEM_LIMIT_BYTES, **kw)


def _pack_pairs(x):
    c = x.shape[1] // 2
    hi = lax.bitcast_convert_type(x[:, :c].astype(BF16).astype(F32), jnp.uint32)
    lo = lax.bitcast_convert_type(x[:, c:].astype(BF16).astype(F32), jnp.uint32)
    return hi | (lo >> 16)


def _unpack_pairs(w):
    hi = lax.bitcast_convert_type(w & jnp.uint32(0xFFFF0000), F32)
    lo = lax.bitcast_convert_type(w << 16, F32)
    return jnp.concatenate([hi, lo], axis=1)


def _rmsnorm_kernel(x_ref, g_ref, o_ref):
    x = x_ref[...]
    ms = jnp.mean(x * x, axis=-1, keepdims=True)
    o_ref[...] = (x * lax.rsqrt(ms + NORM_EPS) * g_ref[...]).astype(o_ref.dtype)


def _rmsnorm(x, g, out_dtype, tm=512):
    T, D = x.shape
    return pl.pallas_call(
        _rmsnorm_kernel,
        grid=(T // tm,),
        in_specs=[pl.BlockSpec((tm, D), lambda i: (i, 0)), pl.BlockSpec((1, D), lambda i: (0, 0))],
        out_specs=pl.BlockSpec((tm, D), lambda i: (i, 0)),
        out_shape=jax.ShapeDtypeStruct((T, D), out_dtype),
        compiler_params=_cparams(("parallel",)),
        name="rmsnorm",
    )(x, g.reshape(1, D))


def _proj_kernel(h_ref, w_ref, o_ref, wbf_ref, *, n_scaled, scale):
    j = pl.program_id(0)

    @pl.when(pl.program_id(1) == 0)
    def _():
        wbf_ref[...] = w_ref[...].astype(BF16)

    acc = jnp.dot(h_ref[...], wbf_ref[...], preferred_element_type=F32)
    o_ref[...] = (acc * jnp.where(j < n_scaled, scale, 1.0)).astype(o_ref.dtype)


def _project(h, w, n_cols, n_scaled_cols, scale, out_dtype, tm=1024, tn=512):
    T, D = h.shape
    return pl.pallas_call(
        functools.partial(_proj_kernel, n_scaled=n_scaled_cols // tn, scale=scale),
        grid=(n_cols // tn, T // tm),
        in_specs=[pl.BlockSpec((tm, D), lambda j, i: (i, 0)),
                  pl.BlockSpec((D, tn), lambda j, i: (0, j))],
        out_specs=pl.BlockSpec((tm, tn), lambda j, i: (i, j)),
        out_shape=jax.ShapeDtypeStruct((T, n_cols), out_dtype),
        scratch_shapes=[pltpu.VMEM((D, tn), BF16)],
        compiler_params=_cparams(("arbitrary", "arbitrary")),
        name="qkv_proj",
    )(h, w)


def _shift_cast_kernel(main_ref, next_ref, o_ref, *, shift):
    full = jnp.concatenate([main_ref[...], next_ref[...]], axis=1)
    o_ref[...] = full[:, shift:shift + o_ref.shape[1]].astype(o_ref.dtype)


def _shifted_columns_bf16(w, start, n_cols, tr=256, tc=1024):
    R = w.shape[0]
    base = start // LANES * LANES
    shift = start - base
    assert base % tc == 0 and n_cols % tc == 0 and R % tr == 0 and 0 < shift < LANES
    return pl.pallas_call(
        functools.partial(_shift_cast_kernel, shift=shift),
        grid=(R // tr, n_cols // tc),
        in_specs=[pl.BlockSpec((tr, tc), lambda i, c: (i, base // tc + c)),
                  pl.BlockSpec((tr, LANES), lambda i, c: (i, (base + (c + 1) * tc) // LANES))],
        out_specs=pl.BlockSpec((tr, tc), lambda i, c: (i, c)),
        out_shape=jax.ShapeDtypeStruct((R, n_cols), BF16),
        compiler_params=_cparams(("parallel", "parallel")),
        name="shift_cast_columns",
    )(w, w)


def _forget_kernel(h_ref, wft_ref, bf_ref, c_ref):
    ft = lax.dot_general(wft_ref[...], h_ref[...], (((1,), (1,)), ((), ())),
                         preferred_element_type=F32)
    c = jax.nn.log_sigmoid(ft + bf_ref[...])
    S = c.shape[1]
    lane = lax.broadcasted_iota(I32, c.shape, 1)
    shift = 1
    while shift < S:
        c = c + jnp.where(lane >= shift, pltpu.roll(c, shift, axis=1), 0.0)
        shift *= 2
    c_ref[0] = c * LOG2E


def _forget_cumsum(h, wft, b_forget, B, S):
    T, D = h.shape
    H = wft.shape[0]
    return pl.pallas_call(
        _forget_kernel,
        grid=(B,),
        in_specs=[pl.BlockSpec((S, D), lambda b: (b, 0)),
                  pl.BlockSpec((H, D), lambda b: (0, 0)),
                  pl.BlockSpec((H, 1), lambda b: (0, 0))],
        out_specs=pl.BlockSpec((1, H, S), lambda b: (b, 0, 0)),
        out_shape=jax.ShapeDtypeStruct((B, H, S), F32),
        compiler_params=_cparams(("parallel",)),
        name="forget_cumsum",
    )(h, wft, b_forget.reshape(H, 1))


def _attn_kernel(q_ref, k_ref, v_ref, crow_ref, o_ref, vaug_ref, m_ref, acc_ref, *, n_heads, tq):
    i = pl.program_id(1)

    @pl.when(i == 0)
    def _():
        ones = jnp.ones((v_ref.shape[0], HEAD_DIM), BF16)
        for h in range(n_heads):
            vaug_ref[h, :, :HEAD_DIM] = v_ref[:, h * HEAD_DIM:(h + 1) * HEAD_DIM]
            vaug_ref[h, :, HEAD_DIM:] = ones

    m_ref[...] = jnp.full(m_ref.shape, -jnp.inf, F32)
    acc_ref[...] = jnp.zeros(acc_ref.shape, F32)
    row = lax.broadcasted_iota(I32, (tq, tq), 0)
    col = lax.broadcasted_iota(I32, (tq, tq), 1)
    causal = col <= row

    def step(j, masked):
        keys = pl.ds(pl.multiple_of(j * tq, tq), tq)
        for h in range(n_heads):
            hs = slice(h * HEAD_DIM, (h + 1) * HEAD_DIM)
            s = lax.dot_general(q_ref[:, hs], k_ref[keys, hs], (((1,), (1,)), ((), ())),
                                preferred_element_type=F32) - crow_ref[0, h, j]
            if masked:
                s = jnp.where(causal, s, -jnp.inf)
            m_old = m_ref[h]
            m_new = jnp.maximum(m_old, jnp.max(s, axis=-1, keepdims=True))
            alpha = jnp.exp2(m_old - m_new)
            p = jnp.exp2(s - jnp.concatenate([m_new] * (tq // HEAD_DIM), axis=1))
            m_ref[h] = m_new
            pv = jnp.dot(p.astype(BF16), vaug_ref[h, keys, :], preferred_element_type=F32)
            acc_ref[h] = jnp.concatenate([alpha, alpha], axis=1) * acc_ref[h] + pv

    def body(j, _):
        step(j, False)
        return 0

    lax.fori_loop(0, i, body, 0)
    step(i, True)
    for h in range(n_heads):
        acc = acc_ref[h]
        o_ref[:, h * HEAD_DIM:(h + 1) * HEAD_DIM] = (acc[:, :HEAD_DIM] / acc[:, HEAD_DIM:]).astype(o_ref.dtype)


def _attention(qkv, c_row, B, S, n_heads, tq=256):
    T = qkv.shape[0]
    W = n_heads * HEAD_DIM
    nq = S // tq
    c_row5 = c_row.reshape(B, n_heads, nq, 1, tq)
    return pl.pallas_call(
        functools.partial(_attn_kernel, n_heads=n_heads, tq=tq),
        grid=(B, nq),
        in_specs=[pl.BlockSpec((tq, W), lambda b, i: (b * nq + i, 0)),
                  pl.BlockSpec((S, W), lambda b, i: (b, 1)),
                  pl.BlockSpec((S, W), lambda b, i: (b, 2)),
                  pl.BlockSpec((1, n_heads, nq, 1, tq), lambda b, i: (b, 0, 0, 0, 0))],
        out_specs=pl.BlockSpec((tq, W), lambda b, i: (b * nq + i, 0)),
        out_shape=jax.ShapeDtypeStruct((T, W), BF16),
        scratch_shapes=[pltpu.VMEM((n_heads, S, 2 * HEAD_DIM), BF16),
                        pltpu.VMEM((n_heads, tq, HEAD_DIM), F32),
                        pltpu.VMEM((n_heads, tq, 2 * HEAD_DIM), F32)],
        compiler_params=_cparams(("arbitrary", "arbitrary")),
        name="fox_attention",
    )(qkv, qkv, qkv, c_row5)


def _gmlp_kernel(h_ref, wz_ref, g_ref, b_ref, ws_ref, bst_ref, o_ref, *, n_groups):
    z = jnp.dot(h_ref[...], wz_ref[...], preferred_element_type=F32)
    z = 0.5 * z * (1.0 + lax.erf(z * (1.0 / math.sqrt(2.0))))
    W = z.shape[1] // 2
    u = z[:, :W]
    v = z[:, W:]
    mu = jnp.mean(v, axis=-1, keepdims=True)
    var = jnp.mean(jnp.square(v - mu), axis=-1, keepdims=True)
    vn = (v - mu) * lax.rsqrt(var + NORM_EPS) * g_ref[...] + b_ref[...]
    row = lax.broadcasted_iota(I32, (CHUNK, CHUNK), 0)
    col = lax.broadcasted_iota(I32, (CHUNK, CHUNK), 1)
    tril = col <= row
    tg = z.shape[0]
    for g in range(n_groups):
        gs = slice(g * GROUP_DIM, (g + 1) * GROUP_DIM)
        wg = jnp.where(tril, ws_ref[g], 0.0).astype(BF16)
        bias = bst_ref[:, g:g + 1]
        for c in range(tg // CHUNK):
            cs = slice(c * CHUNK, (c + 1) * CHUNK)
            mixed = jnp.dot(wg, vn[cs, gs].astype(BF16), preferred_element_type=F32) + bias
            o_ref[cs, gs] = (u[cs, gs] * mixed).astype(o_ref.dtype)


def _gmlp(h, wz, g_v_ln, b_v_ln, w_spatial, b_spatial, tg=512):
    T, D = h.shape
    W = g_v_ln.shape[0]
    W2 = 2 * W
    G = w_spatial.shape[0]
    return pl.pallas_call(
        functools.partial(_gmlp_kernel, n_groups=G),
        grid=(T // tg,),
        in_specs=[pl.BlockSpec((tg, D), lambda i: (i, 0)),
                  pl.BlockSpec((D, W2), lambda i: (0, 0)),
                  pl.BlockSpec((1, W), lambda i: (0, 0)),
                  pl.BlockSpec((1, W), lambda i: (0, 0)),
                  pl.BlockSpec((G, CHUNK, CHUNK), lambda i: (0, 0, 0)),
                  pl.BlockSpec((CHUNK, G), lambda i: (0, 0))],
        out_specs=pl.BlockSpec((tg, W), lambda i: (i, 0)),
        out_shape=jax.ShapeDtypeStruct((T, W), BF16),
        compiler_params=_cparams(("parallel",)),
        name="gmlp",
    )(h, wz, g_v_ln.reshape(1, W), b_v_ln.reshape(1, W), w_spatial, b_spatial.T)


def _merge_kernel(attn_ref, sg_ref, h_ref, wa_ref, wb_ref, wga_ref, wgb_ref, o_ref):
    h = h_ref[...]
    a = jnp.dot(attn_ref[...], wa_ref[...], preferred_element_type=F32)
    ga = jnp.dot(h, wga_ref[...], preferred_element_type=F32)
    m = jax.nn.sigmoid(ga) * a
    b = jnp.dot(sg_ref[...], wb_ref[...], preferred_element_type=F32)
    gb = jnp.dot(h, wgb_ref[...], preferred_element_type=F32)
    o_ref[...] = (m + jax.nn.sigmoid(gb) * b).astype(o_ref.dtype)


def _merge(attn, sg, h, wa, wb, wg, g_off, tm=512, tn=512):
    T, D = h.shape
    Wa = attn.shape[1]
    Wb = sg.shape[1]
    nt = D // tn
    g0 = g_off // tn
    return pl.pallas_call(
        _merge_kernel,
        grid=(nt, T // tm),
        in_specs=[pl.BlockSpec((tm, Wa), lambda j, i: (i, 0)),
                  pl.BlockSpec((tm, Wb), lambda j, i: (i, 0)),
                  pl.BlockSpec((tm, D), lambda j, i: (i, 0)),
                  pl.BlockSpec((Wa, tn), lambda j, i: (0, j)),
                  pl.BlockSpec((Wb, tn), lambda j, i: (0, j)),
                  pl.BlockSpec((D, tn), lambda j, i: (0, g0 + j)),
                  pl.BlockSpec((D, tn), lambda j, i: (0, g0 + nt + j))],
        out_specs=pl.BlockSpec((tm, tn), lambda j, i: (i, j)),
        out_shape=jax.ShapeDtypeStruct((T, D), BF16),
        compiler_params=_cparams(("arbitrary", "arbitrary")),
        name="gated_merge",
    )(attn, sg, h, wa, wb, wg, wg)


def _out_router_kernel(m_ref, x_ref, wo_ref, g_ref, wr_ref, br_ref,
                       x1_ref, h2_ref, sel_ref, gate_ref, *, n_parts):
    E = br_ref.shape[1]
    part = m_ref.shape[0] // n_parts
    for p in range(n_parts):
        rows = pl.ds(p * part, part)
        x1 = x_ref[rows, :] + jnp.dot(m_ref[rows, :], wo_ref[...], preferred_element_type=F32)
        x1_ref[rows, :] = x1
        ms = jnp.mean(x1 * x1, axis=-1, keepdims=True)
        h2 = x1 * lax.rsqrt(ms + NORM_EPS) * g_ref[...]
        h2_ref[rows, :] = _pack_pairs(h2)
        h2_hi = h2.astype(BF16)
        h2_lo = (h2 - h2_hi.astype(F32)).astype(BF16)
        pa = jnp.dot(h2_hi, wr_ref[...], preferred_element_type=F32)
        pb = jnp.dot(h2_lo, wr_ref[:, :E], preferred_element_type=F32)
        logits = pa[:, :E] + (pa[:, E:] + pb) + br_ref[...]
        lane = lax.broadcasted_iota(I32, logits.shape, 1)
        work = logits
        sel = jnp.zeros(logits.shape, F32)
        num = jnp.zeros(logits.shape, F32)
        denom = jnp.zeros((part, 1), F32)
        m0 = None
        for _ in range(TOP_K):
            m = jnp.max(work, axis=-1, keepdims=True)
            idx = jnp.min(jnp.where(work == m, lane, E), axis=-1, keepdims=True)
            onehot = lane == idx
            if m0 is None:
                m0 = m
            e = jnp.exp(m - m0)
            sel = jnp.where(onehot, 1.0, sel)
            num = jnp.where(onehot, e, num)
            denom = denom + e
            work = jnp.where(onehot, -jnp.inf, work)
        sel_ref[rows, :] = sel
        gate_ref[rows, :] = num / denom


def _out_router(merged, x, wo, g_ffn, w_router, b_router, to=512, n_parts=2):
    T, D = x.shape
    E = w_router.shape[1]
    w_hi = w_router.astype(BF16)
    w_lo = (w_router - w_hi.astype(F32)).astype(BF16)
    w_router = jnp.concatenate([w_hi, w_lo], axis=1)
    row = lambda i: (i, 0)
    fixed = lambda i: (0, 0)
    return pl.pallas_call(
        functools.partial(_out_router_kernel, n_parts=n_parts),
        grid=(T // to,),
        in_specs=[pl.BlockSpec((to, D), row), pl.BlockSpec((to, D), row),
                  pl.BlockSpec((D, D), fixed), pl.BlockSpec((1, D), fixed),
                  pl.BlockSpec((D, 2 * E), fixed), pl.BlockSpec((1, E), fixed)],
        out_specs=[pl.BlockSpec((to, D), row), pl.BlockSpec((to, D // 2), row),
                   pl.BlockSpec((to, E), row), pl.BlockSpec((to, E), row)],
        out_shape=[jax.ShapeDtypeStruct((T, D), F32), jax.ShapeDtypeStruct((T, D // 2), jnp.uint32),
                   jax.ShapeDtypeStruct((T, E), F32), jax.ShapeDtypeStruct((T, E), F32)],
        compiler_params=_cparams(("parallel",)),
        name="out_router",
    )(merged, x, wo, g_ffn.reshape(1, D), w_router, b_router.reshape(1, E))


def _routing_kernel(sel_ref, gate_ref, dest_ref, g4_ref, nblk_ref, start_ref, rank_ref, *, tile):
    T, E = sel_ref.shape
    nt = T // tile
    r = lax.broadcasted_iota(I32, (tile, tile), 0)
    c = lax.broadcasted_iota(I32, (tile, tile), 1)
    strict_lower = (c < r).astype(BF16)
    er = lax.broadcasted_iota(I32, (E, E), 0)
    ec = lax.broadcasted_iota(I32, (E, E), 1)
    strict_upper = (er < ec).astype(BF16)

    def pass1(t, carry):
        rows = pl.ds(pl.multiple_of(t * tile, tile), tile)
        a = sel_ref[rows, :]
        rank_ref[rows, :] = jnp.dot(strict_lower, a.astype(BF16), preferred_element_type=F32) + carry
        return carry + jnp.sum(a, axis=0, keepdims=True)

    counts = lax.fori_loop(0, nt, pass1, jnp.zeros((1, E), F32))
    nblk = jnp.floor((counts + (ROW_BLOCK - 1)) * (1.0 / ROW_BLOCK))
    start_blk = jnp.dot(nblk.astype(BF16), strict_upper, preferred_element_type=F32)
    nblk_ref[...] = nblk.astype(I32)
    start_ref[...] = start_blk.astype(I32)
    start_row = start_blk * float(ROW_BLOCK)
    lane = lax.broadcasted_iota(I32, (tile, 128), 1)

    def pass2(t, _):
        rows = pl.ds(pl.multiple_of(t * tile, tile), tile)
        a = sel_ref[rows, :]
        g = gate_ref[rows, :]
        dest_e = rank_ref[rows, :] + start_row
        slot = jnp.dot(a.astype(BF16), strict_upper, preferred_element_type=F32)
        d4 = jnp.zeros((tile, 128), F32)
        g4 = jnp.zeros((tile, 128), F32)
        for s in range(TOP_K):
            pick = (a > 0.5) & (slot == float(s))
            d4 = jnp.where(lane == s, jnp.sum(jnp.where(pick, dest_e, 0.0), axis=-1, keepdims=True), d4)
            g4 = jnp.where(lane == s, jnp.sum(jnp.where(pick, g, 0.0), axis=-1, keepdims=True), g4)
        dest_ref[rows, :] = d4.astype(I32)
        g4_ref[rows, :] = g4
        return 0

    lax.fori_loop(0, nt, pass2, 0)


def _routing(sel, gate, tile=256):
    T, E = sel.shape
    return pl.pallas_call(
        functools.partial(_routing_kernel, tile=tile),
        out_shape=[jax.ShapeDtypeStruct((T, 128), I32), jax.ShapeDtypeStruct((T, 128), F32),
                   jax.ShapeDtypeStruct((1, E), I32), jax.ShapeDtypeStruct((1, E), I32)],
        scratch_shapes=[pltpu.VMEM((T, E), F32)],
        compiler_params=pltpu.CompilerParams(vmem_limit_bytes=VMEM_LIMIT_BYTES),
        name="routing_ranks",
    )(sel, gate)


SC_CORES = 2
SC_SUBCORES = 16
SC_CHUNK = 32


def _sc_gather_rows(table, idx):
    n = idx.shape[0]
    W = table.shape[1]
    n_workers = SC_CORES * SC_SUBCORES
    per_worker = n // n_workers
    assert per_worker * n_workers == n and per_worker % SC_CHUNK == 0
    mesh = plsc.VectorSubcoreMesh(core_axis_name="c", subcore_axis_name="s",
                                  num_cores=SC_CORES, num_subcores=SC_SUBCORES)

    @functools.partial(
        pl.kernel, mesh=mesh,
        out_type=jax.ShapeDtypeStruct((n, W), table.dtype),
        scratch_types=[pltpu.VMEM((SC_CHUNK,), I32), pltpu.VMEM((SC_CHUNK, W), table.dtype),
                       pltpu.SemaphoreType.DMA],
        name="sc_gather_rows",
    )
    def gather(table_hbm, idx_hbm, out_hbm, idx_v, rows_v, sem):
        wid = lax.axis_index("s") * SC_CORES + lax.axis_index("c")
        base = wid * per_worker

        @pl.loop(0, per_worker // SC_CHUNK)
        def _(c):
            off = pl.multiple_of(base + c * SC_CHUNK, 8)
            pltpu.sync_copy(idx_hbm.at[pl.ds(off, SC_CHUNK)], idx_v)
            pltpu.async_copy(table_hbm.at[idx_v], rows_v, sem).wait()
            pltpu.sync_copy(rows_v, out_hbm.at[pl.ds(off, SC_CHUNK)])

    return gather(table, idx)


def _sc_scatter_rows(rows, idx, n_out, n_slots):
    T, W = rows.shape
    n_workers = SC_CORES * SC_SUBCORES
    per_worker = T // n_workers
    assert per_worker * n_workers == T and per_worker % SC_CHUNK == 0
    mesh = plsc.VectorSubcoreMesh(core_axis_name="c", subcore_axis_name="s",
                                  num_cores=SC_CORES, num_subcores=SC_SUBCORES)

    @functools.partial(
        pl.kernel, mesh=mesh,
        out_type=jax.ShapeDtypeStruct((n_out, W), rows.dtype),
        scratch_types=[pltpu.VMEM((SC_CHUNK,), I32), pltpu.VMEM((SC_CHUNK, W), rows.dtype)],
        name="sc_scatter_rows",
    )
    def scatter(rows_hbm, idx_hbm, out_hbm, idx_v, rows_v):
        wid = lax.axis_index("s") * SC_CORES + lax.axis_index("c")
        base = wid * per_worker

        @pl.loop(0, per_worker // SC_CHUNK)
        def _(c):
            off = pl.multiple_of(base + c * SC_CHUNK, 8)
            pltpu.sync_copy(rows_hbm.at[pl.ds(off, SC_CHUNK)], rows_v)
            for k in range(n_slots):
                pltpu.sync_copy(idx_hbm.at[pl.ds(pl.multiple_of(k * T + off, 8), SC_CHUNK)], idx_v)
                pltpu.sync_copy(rows_v, out_hbm.at[idx_v])

    return scatter(rows, idx)


def _expert_kernel(ie_ref, ib_ref, ins_ref,
                   xs_hbm, wup_ref, bup_ref, wdn_ref, bdn_ref, perm_ref, ys_hbm,
                   xg_ref, acc_ref, yst_ref, wupb_ref, wdnb_ref, gsem, osem, *, n_ff_tiles):
    i = pl.program_id(0)
    j = pl.program_id(1)
    n_items = pl.num_programs(0)
    nsub = ins_ref[i]
    slot = i % 2

    def for_regions(n_blocks, fn):
        first = jnp.int32(0)
        for count in REGION_BLOCKS:
            present = (n_blocks & count) != 0
            pl.when(present)(functools.partial(fn, first, count))
            first = first + jnp.where(present, count, 0)

    def rows_of(first, count):
        return pl.ds(pl.multiple_of(first * ROW_BLOCK, ROW_BLOCK), count * ROW_BLOCK)

    def in_copy(item, dst_slot, first, count):
        src = pl.multiple_of((ib_ref[item] + first) * ROW_BLOCK, ROW_BLOCK)
        return pltpu.make_async_copy(xs_hbm.at[pl.ds(src, count * ROW_BLOCK)],
                                     xg_ref.at[dst_slot, rows_of(first, count)], gsem.at[dst_slot])

    def fetch_item(item, n_blocks, dst_slot):
        for_regions(n_blocks, lambda first, count: in_copy(item, dst_slot, first, count).start())

    @pl.when(j == 0)
    def _():
        @pl.when(i == 0)
        def _():
            fetch_item(0, nsub, 0)

        nxt = jnp.minimum(i + 1, n_items - 1)
        fetch_item(nxt, jnp.where(i + 1 < n_items, ins_ref[nxt], 0), 1 - slot)

        def arrive(first, count):
            in_copy(i, slot, first, count).wait()
            acc_ref[rows_of(first, count), :] = jnp.broadcast_to(
                bdn_ref[0], (count * ROW_BLOCK, acc_ref.shape[1]))

        for_regions(nsub, arrive)

    @pl.when(nsub > 0)
    def _():
        bup = bup_ref[0]
        perm = perm_ref[...]
        half = perm.shape[0] // 2

        def cast_weights():
            wup = wup_ref[0].astype(BF16)
            wdn = wdn_ref[0].astype(BF16)
            wupb_ref[...] = wup
            wdnb_ref[...] = wdn
            return wup, wdn

        big = REGION_BLOCKS[0]

        @pl.when((nsub & big) == 0)
        def _():
            cast_weights()

        def run_blocks(first, count):
            if count == big:
                wup, wdn = cast_weights()
            else:
                wup = wupb_ref[...]
                wdn = wdnb_ref[...]
            rows = rows_of(first, count)
            xb = _unpack_pairs(xg_ref[slot, rows, :]).astype(BF16)
            gu = (jnp.dot(xb, wup, preferred_element_type=F32) + bup).astype(BF16)
            glu_parts, lin_parts = [], []
            for p in range(gu.shape[1] // perm.shape[0]):
                gp = jnp.dot(gu[:, p * perm.shape[0]:(p + 1) * perm.shape[0]], perm,
                             preferred_element_type=F32)
                glu_parts.append(gp[:, :half])
                lin_parts.append(gp[:, half:])
            x_glu = jnp.minimum(jnp.concatenate(glu_parts, axis=1), SWIGLU_LIMIT)
            x_lin = jnp.clip(jnp.concatenate(lin_parts, axis=1), -SWIGLU_LIMIT, SWIGLU_LIMIT)
            act = x_glu * jax.nn.sigmoid(SWIGLU_ALPHA * x_glu) * (x_lin + 1.0)
            acc_ref[rows, :] += jnp.dot(act.astype(BF16), wdn, preferred_element_type=F32)

        for_regions(nsub, run_blocks)

    def out_copy(item, first, count):
        dst = pl.multiple_of((ib_ref[item] + first) * ROW_BLOCK, ROW_BLOCK)
        return pltpu.make_async_copy(yst_ref.at[rows_of(first, count)],
                                     ys_hbm.at[pl.ds(dst, count * ROW_BLOCK)], osem)

    @pl.when(j == n_ff_tiles - 1)
    def _():
        prev = jnp.maximum(i - 1, 0)
        for_regions(jnp.where(i > 0, ins_ref[prev], 0),
                    lambda first, count: out_copy(prev, first, count).wait())

        def leave(first, count):
            rows = rows_of(first, count)
            yst_ref[rows, :] = _pack_pairs(acc_ref[rows, :])
            out_copy(i, first, count).start()

        for_regions(nsub, leave)

        @pl.when(i == n_items - 1)
        def _():
            for_regions(nsub, lambda first, count: out_copy(i, first, count).wait())


def _experts(xs, item_e, item_b, item_n, n_active, w_up, b_up, w_down, b_down):
    n_rows, Dp = xs.shape
    D = 2 * Dp
    E, _, F2 = w_up.shape
    F = F2 // 2
    J = F // FF_TILE
    half = 128
    perm = np.zeros((2 * half, 2 * half), np.float32)
    perm[2 * np.arange(half), np.arange(half)] = 1.0
    perm[2 * np.arange(half) + 1, half + np.arange(half)] = 1.0

    def jj(i, j, ins):
        return jnp.where(ins[i] > 0, j, J - 1)

    grid_spec = pltpu.PrefetchScalarGridSpec(
        num_scalar_prefetch=3,
        grid=(n_active, J),
        in_specs=[pl.BlockSpec(memory_space=pl.ANY),
                  pl.BlockSpec((1, D, 2 * FF_TILE), lambda i, j, ie, ib, ins: (ie[i], 0, jj(i, j, ins))),
                  pl.BlockSpec((1, 1, 2 * FF_TILE), lambda i, j, ie, ib, ins: (ie[i], 0, jj(i, j, ins))),
                  pl.BlockSpec((1, FF_TILE, D), lambda i, j, ie, ib, ins: (ie[i], jj(i, j, ins), 0)),
                  pl.BlockSpec((1, 1, D), lambda i, j, ie, ib, ins: (ie[i], 0, 0)),
                  pl.BlockSpec((2 * half, 2 * half), lambda i, j, ie, ib, ins: (0, 0))],
        out_specs=pl.BlockSpec(memory_space=pl.ANY),
        scratch_shapes=[pltpu.VMEM((2, ITEM_ROWS, Dp), jnp.uint32),
                        pltpu.VMEM((ITEM_ROWS, D), F32),
                        pltpu.VMEM((ITEM_ROWS, Dp), jnp.uint32),
                        pltpu.VMEM((D, 2 * FF_TILE), BF16),
                        pltpu.VMEM((FF_TILE, D), BF16),
                        pltpu.SemaphoreType.DMA((2,)), pltpu.SemaphoreType.DMA(())],
    )
    return pl.pallas_call(
        functools.partial(_expert_kernel, n_ff_tiles=J),
        grid_spec=grid_spec,
        out_shape=jax.ShapeDtypeStruct((n_rows, Dp), jnp.uint32),
        compiler_params=_cparams(("arbitrary", "arbitrary")),
        name="expert_ffn",
    )(item_e, item_b, item_n,
      xs, w_up, b_up.reshape(E, 1, F2), w_down, b_down.reshape(E, 1, D), jnp.asarray(perm, BF16))


def _combine_kernel(*refs):
    yk_refs, (g4_ref, x1_ref, g_ref, o_ref) = refs[:TOP_K], refs[TOP_K:]
    y = x1_ref[...]
    g4 = g4_ref[...]
    for k in range(TOP_K):
        y = y + g4[:, k:k + 1] * _unpack_pairs(yk_refs[k][...])
    ms = jnp.mean(y * y, axis=-1, keepdims=True)
    o_ref[...] = y * lax.rsqrt(ms + NORM_EPS) * g_ref[...]


def _combine(yk, g4, x1, g_final, tc=256):
    T, D = x1.shape
    nt = T // tc
    slot_specs = [pl.BlockSpec((tc, D // 2), functools.partial(lambda i, k: (k * nt + i, 0), k=k))
                  for k in range(TOP_K)]
    return pl.pallas_call(
        _combine_kernel,
        grid=(nt,),
        in_specs=slot_specs + [pl.BlockSpec((tc, 128), lambda i: (i, 0)),
                               pl.BlockSpec((tc, D), lambda i: (i, 0)),
                               pl.BlockSpec((1, D), lambda i: (0, 0))],
        out_specs=pl.BlockSpec((tc, D), lambda i: (i, 0)),
        out_shape=jax.ShapeDtypeStruct((T, D), F32),
        compiler_params=_cparams(("parallel",)),
        name="combine_norm",
    )(*([yk] * TOP_K), g4, x1, g_final.reshape(1, D))


def _work_items(nblk, start_blk, n_items):
    E = nblk.shape[0]
    per_e = (nblk + ITEM_BLOCKS - 1) // ITEM_BLOCKS
    ends = jnp.cumsum(per_e)
    total = ends[-1]
    idx = jnp.arange(n_items, dtype=I32)
    e = jnp.minimum(jnp.searchsorted(ends, idx, side="right"), E - 1).astype(I32)
    local = idx - (ends[e] - per_e[e])
    active = idx < total
    last_e = e[jnp.maximum(total - 1, 0)]
    item_e = jnp.where(active, e, last_e).astype(I32)
    item_b = jnp.where(active, start_blk[e] + local * ITEM_BLOCKS, 0).astype(I32)
    item_n = jnp.where(active, jnp.clip(nblk[e] - local * ITEM_BLOCKS, 0, ITEM_BLOCKS), 0).astype(I32)
    return item_e, item_b, item_n, jnp.maximum(total, 1).astype(I32)


def kernel(x, g_mix, w_in, b_forget, g_v_ln, b_v_ln, w_spatial, b_spatial, w_branch_attn, w_branch_gmlp, w_out, g_ffn, w_router, b_router, w_expert_up, b_expert_up, w_expert_down, b_expert_down, g_final):
    B, S, D = x.shape
    T = B * S
    n_heads = b_forget.shape[0]
    attn_w = n_heads * HEAD_DIM
    gmlp_w = g_v_ln.shape[0]
    E = w_router.shape[1]
    off_f = 3 * attn_w
    off_z = off_f + n_heads
    off_g = off_z + 2 * gmlp_w

    x2 = x.reshape(T, D)
    wft = w_in[:, off_f:off_z].T.astype(BF16)
    wzg = _shifted_columns_bf16(w_in, off_z, w_in.shape[1] - off_z)
    wa = w_branch_attn.astype(BF16)
    wb = w_branch_gmlp.astype(BF16)
    wo = w_out.astype(BF16)

    h = _rmsnorm(x2, g_mix, BF16)
    qkv = _project(h, w_in, 3 * attn_w, attn_w, LOG2E / math.sqrt(HEAD_DIM), BF16)
    c_row = _forget_cumsum(h, wft, b_forget, B, S)
    attn = _attention(qkv, c_row, B, S, n_heads)
    sg = _gmlp(h, wzg, g_v_ln, b_v_ln, w_spatial, b_spatial)
    merged = _merge(attn, sg, h, wa, wb, wzg, off_g - off_z)
    x1, h2p, sel, gate = _out_router(merged, x2, wo, g_ffn, w_router, b_router)

    dest4, g4, nblk, start_blk = _routing(sel, gate)
    n_rows = T * TOP_K + E * ROW_BLOCK
    dest_slots = dest4[:, :TOP_K].T.reshape(TOP_K * T)
    max_blocks = n_rows // ROW_BLOCK
    n_items = E + -(-(max_blocks - E) // ITEM_BLOCKS) + 1
    item_e, item_b, item_n, n_active = _work_items(nblk[0], start_blk[0], n_items)
    xs = _sc_scatter_rows(h2p, dest_slots, n_rows, TOP_K)
    ys = _experts(xs, item_e, item_b, item_n, n_active,
                  w_expert_up, b_expert_up, w_expert_down, b_expert_down)
    yk = _sc_gather_rows(ys, dest_slots)
    out = _combine(yk, g4, x1, g_final)
    return out.reshape(B, S, D)
```

```python
import functools
import math

import jax
import jax.numpy as jnp
import numpy as np
from jax import lax
from jax.experimental import pallas as pl
from jax.experimental.pallas import tpu as pltpu
from jax.experimental.pallas import tpu_sc as plsc

F32 = jnp.float32
BF16 = jnp.bfloat16
I32 = jnp.int32

NORM_EPS = 1e-5
LANES = 128
HEAD_DIM = 128
CHUNK = 128
GROUP_DIM = 128
TOP_K = 4
SWIGLU_ALPHA = 1.702
SWIGLU_LIMIT = 7.0
LOG2E = math.log2(math.e)

VMEM_LIMIT_BYTES = 56 * 1024 * 1024

ROW_BLOCK = 256
ITEM_BLOCKS = 5
ITEM_ROWS = ITEM_BLOCKS * ROW_BLOCK
REGION_BLOCKS = (4, 2, 1)
assert sum(REGION_BLOCKS) >= ITEM_BLOCKS
COMPUTE_PLANS = {5: ((0, 5),), 4: ((0, 4),), 3: ((0, 2), (2, 1)), 2: ((0, 2),), 1: ((0, 1),)}
assert set(COMPUTE_PLANS) == set(range(1, ITEM_BLOCKS + 1))
FF_TILE = 256
DMA_UNROLL = 8


def _cparams(sem, **kw):
    return pltpu.CompilerParams(dimension_semantics=sem, vmem_limit_bytes=VMEM_LIMIT_BYTES, **kw)


def _pack_pairs(x):
    c = x.shape[1] // 2
    hi = lax.bitcast_convert_type(x[:, :c].astype(BF16).astype(F32), jnp.uint32)
    lo = lax.bitcast_convert_type(x[:, c:].astype(BF16).astype(F32), jnp.uint32)
    return hi | (lo >> 16)


def _unpack_pairs(w):
    hi = lax.bitcast_convert_type(w & jnp.uint32(0xFFFF0000), F32)
    lo = lax.bitcast_convert_type(w << 16, F32)
    return jnp.concatenate([hi, lo], axis=1)


def _rmsnorm_kernel(x_ref, g_ref, o_ref):
    x = x_ref[...]
    ms = jnp.mean(x * x, axis=-1, keepdims=True)
    o_ref[...] = (x * lax.rsqrt(ms + NORM_EPS) * g_ref[...]).astype(o_ref.dtype)


def _rmsnorm(x, g, out_dtype, tm=512):
    T, D = x.shape
    return pl.pallas_call(
        _rmsnorm_kernel,
        grid=(T // tm,),
        in_specs=[pl.BlockSpec((tm, D), lambda i: (i, 0)), pl.BlockSpec((1, D), lambda i: (0, 0))],
        out_specs=pl.BlockSpec((tm, D), lambda i: (i, 0)),
        out_shape=jax.ShapeDtypeStruct((T, D), out_dtype),
        compiler_params=_cparams(("parallel",)),
        name="rmsnorm",
    )(x, g.reshape(1, D))


def _proj_kernel(h_ref, w_ref, o_ref, wbf_ref, *, n_scaled, scale):
    j = pl.program_id(0)

    @pl.when(pl.program_id(1) == 0)
    def _():
        wbf_ref[...] = w_ref[...].astype(BF16)

    acc = jnp.dot(h_ref[...], wbf_ref[...], preferred_element_type=F32)
    o_ref[...] = (acc * jnp.where(j < n_scaled, scale, 1.0)).astype(o_ref.dtype)


def _project(h, w, n_cols, n_scaled_cols, scale, out_dtype, tm=1024, tn=512):
    T, D = h.shape
    return pl.pallas_call(
        functools.partial(_proj_kernel, n_scaled=n_scaled_cols // tn, scale=scale),
        grid=(n_cols // tn, T // tm),
        in_specs=[pl.BlockSpec((tm, D), lambda j, i: (i, 0)),
                  pl.BlockSpec((D, tn), lambda j, i: (0, j))],
        out_specs=pl.BlockSpec((tm, tn), lambda j, i: (i, j)),
        out_shape=jax.ShapeDtypeStruct((T, n_cols), out_dtype),
        scratch_shapes=[pltpu.VMEM((D, tn), BF16)],
        compiler_params=_cparams(("arbitrary", "arbitrary")),
        name="qkv_proj",
    )(h, w)


def _forget_kernel(h_ref, wft_ref, bf_ref, c_ref):
    ft = lax.dot_general(wft_ref[...], h_ref[...], (((1,), (1,)), ((), ())),
                         preferred_element_type=F32)
    c = jax.nn.log_sigmoid(ft + bf_ref[...])
    S = c.shape[1]
    lane = lax.broadcasted_iota(I32, c.shape, 1)
    shift = 1
    while shift < S:
        c = c + jnp.where(lane >= shift, pltpu.roll(c, shift, axis=1), 0.0)
        shift *= 2
    c_ref[0] = c * LOG2E


def _forget_cumsum(h, wft, b_forget, B, S):
    T, D = h.shape
    H = wft.shape[0]
    return pl.pallas_call(
        _forget_kernel,
        grid=(B,),
        in_specs=[pl.BlockSpec((S, D), lambda b: (b, 0)),
                  pl.BlockSpec((H, D), lambda b: (0, 0)),
                  pl.BlockSpec((H, 1), lambda b: (0, 0))],
        out_specs=pl.BlockSpec((1, H, S), lambda b: (b, 0, 0)),
        out_shape=jax.ShapeDtypeStruct((B, H, S), F32),
        compiler_params=_cparams(("parallel",)),
        name="forget_cumsum",
    )(h, wft, b_forget.reshape(H, 1))


def _attn_kernel(q_ref, k_ref, v_ref, crow_ref, o_ref, vaug_ref, m_ref, acc_ref, *, n_heads, tq):
    i = pl.program_id(1)

    @pl.when(i == 0)
    def _():
        ones = jnp.ones((v_ref.shape[0], HEAD_DIM), BF16)
        for h in range(n_heads):
            vaug_ref[h, :, :HEAD_DIM] = v_ref[:, h * HEAD_DIM:(h + 1) * HEAD_DIM]
            vaug_ref[h, :, HEAD_DIM:] = ones

    m_ref[...] = jnp.full(m_ref.shape, -jnp.inf, F32)
    acc_ref[...] = jnp.zeros(acc_ref.shape, F32)
    row = lax.broadcasted_iota(I32, (tq, tq), 0)
    col = lax.broadcasted_iota(I32, (tq, tq), 1)
    causal = col <= row

    def step(j, masked):
        keys = pl.ds(pl.multiple_of(j * tq, tq), tq)
        for h in range(n_heads):
            hs = slice(h * HEAD_DIM, (h + 1) * HEAD_DIM)
            s = lax.dot_general(q_ref[:, hs], k_ref[keys, hs], (((1,), (1,)), ((), ())),
                                preferred_element_type=F32) - crow_ref[0, h, j]
            if masked:
                s = jnp.where(causal, s, -jnp.inf)
            m_old = m_ref[h]
            m_new = jnp.maximum(m_old, jnp.max(s, axis=-1, keepdims=True))
            alpha = jnp.exp2(m_old - m_new)
            p = jnp.exp2(s - jnp.concatenate([m_new] * (tq // HEAD_DIM), axis=1))
            m_ref[h] = m_new
            pv = jnp.dot(p.astype(BF16), vaug_ref[h, keys, :], preferred_element_type=F32)
            acc_ref[h] = jnp.concatenate([alpha, alpha], axis=1) * acc_ref[h] + pv

    def body(j, _):
        step(j, False)
        return 0

    lax.fori_loop(0, i, body, 0)
    step(i, True)
    for h in range(n_heads):
        acc = acc_ref[h]
        o_ref[:, h * HEAD_DIM:(h + 1) * HEAD_DIM] = (acc[:, :HEAD_DIM] / acc[:, HEAD_DIM:]).astype(o_ref.dtype)


def _attention(qkv, c_row, B, S, n_heads, tq=256):
    T = qkv.shape[0]
    W = n_heads * HEAD_DIM
    nq = S // tq
    c_row5 = c_row.reshape(B, n_heads, nq, 1, tq)
    return pl.pallas_call(
        functools.partial(_attn_kernel, n_heads=n_heads, tq=tq),
        grid=(B, nq),
        in_specs=[pl.BlockSpec((tq, W), lambda b, i: (b * nq + i, 0)),
                  pl.BlockSpec((S, W), lambda b, i: (b, 1)),
                  pl.BlockSpec((S, W), lambda b, i: (b, 2)),
                  pl.BlockSpec((1, n_heads, nq, 1, tq), lambda b, i: (b, 0, 0, 0, 0))],
        out_specs=pl.BlockSpec((tq, W), lambda b, i: (b * nq + i, 0)),
        out_shape=jax.ShapeDtypeStruct((T, W), BF16),
        scratch_shapes=[pltpu.VMEM((n_heads, S, 2 * HEAD_DIM), BF16),
                        pltpu.VMEM((n_heads, tq, HEAD_DIM), F32),
                        pltpu.VMEM((n_heads, tq, 2 * HEAD_DIM), F32)],
        compiler_params=_cparams(("arbitrary", "arbitrary")),
        name="fox_attention",
    )(qkv, qkv, qkv, c_row5)


def _gmlp_kernel(h_ref, wz_ref, g_ref, b_ref, ws_ref, bst_ref, o_ref, *, n_groups):
    z = jnp.dot(h_ref[...], wz_ref[...], preferred_element_type=F32)
    z = 0.5 * z * (1.0 + lax.erf(z * (1.0 / math.sqrt(2.0))))
    W = z.shape[1] // 2
    u = z[:, :W]
    v = z[:, W:]
    mu = jnp.mean(v, axis=-1, keepdims=True)
    var = jnp.mean(jnp.square(v - mu), axis=-1, keepdims=True)
    vn = (v - mu) * lax.rsqrt(var + NORM_EPS) * g_ref[...] + b_ref[...]
    row = lax.broadcasted_iota(I32, (CHUNK, CHUNK), 0)
    col = lax.broadcasted_iota(I32, (CHUNK, CHUNK), 1)
    tril = col <= row
    tg = z.shape[0]
    for g in range(n_groups):
        gs = slice(g * GROUP_DIM, (g + 1) * GROUP_DIM)
        wg = jnp.where(tril, ws_ref[g], 0.0).astype(BF16)
        bias = bst_ref[:, g:g + 1]
        for c in range(tg // CHUNK):
            cs = slice(c * CHUNK, (c + 1) * CHUNK)
            mixed = jnp.dot(wg, vn[cs, gs].astype(BF16), preferred_element_type=F32) + bias
            o_ref[cs, gs] = (u[cs, gs] * mixed).astype(o_ref.dtype)


def _gmlp(h, wz, g_v_ln, b_v_ln, w_spatial, b_spatial, tg=512):
    T, D = h.shape
    W = g_v_ln.shape[0]
    W2 = 2 * W
    G = w_spatial.shape[0]
    return pl.pallas_call(
        functools.partial(_gmlp_kernel, n_groups=G),
        grid=(T // tg,),
        in_specs=[pl.BlockSpec((tg, D), lambda i: (i, 0)),
                  pl.BlockSpec((D, W2), lambda i: (0, 0)),
                  pl.BlockSpec((1, W), lambda i: (0, 0)),
                  pl.BlockSpec((1, W), lambda i: (0, 0)),
                  pl.BlockSpec((G, CHUNK, CHUNK), lambda i: (0, 0, 0)),
                  pl.BlockSpec((CHUNK, G), lambda i: (0, 0))],
        out_specs=pl.BlockSpec((tg, W), lambda i: (i, 0)),
        out_shape=jax.ShapeDtypeStruct((T, W), BF16),
        compiler_params=_cparams(("parallel",)),
        name="gmlp",
    )(h, wz, g_v_ln.reshape(1, W), b_v_ln.reshape(1, W), w_spatial, b_spatial.T)


def _merge_kernel(attn_ref, sg_ref, h_ref, wa_ref, wb_ref, wga_ref, wgb_ref, o_ref):
    h = h_ref[...]
    a = jnp.dot(attn_ref[...], wa_ref[...], preferred_element_type=F32)
    ga = jnp.dot(h, wga_ref[...], preferred_element_type=F32)
    m = jax.nn.sigmoid(ga) * a
    b = jnp.dot(sg_ref[...], wb_ref[...], preferred_element_type=F32)
    gb = jnp.dot(h, wgb_ref[...], preferred_element_type=F32)
    o_ref[...] = (m + jax.nn.sigmoid(gb) * b).astype(o_ref.dtype)


def _merge(attn, sg, h, wa, wb, wg, g_off, tm=512, tn=512):
    T, D = h.shape
    Wa = attn.shape[1]
    Wb = sg.shape[1]
    nt = D // tn
    g0 = g_off // tn
    return pl.pallas_call(
        _merge_kernel,
        grid=(nt, T // tm),
        in_specs=[pl.BlockSpec((tm, Wa), lambda j, i: (i, 0)),
                  pl.BlockSpec((tm, Wb), lambda j, i: (i, 0)),
                  pl.BlockSpec((tm, D), lambda j, i: (i, 0)),
                  pl.BlockSpec((Wa, tn), lambda j, i: (0, j)),
                  pl.BlockSpec((Wb, tn), lambda j, i: (0, j)),
                  pl.BlockSpec((D, tn), lambda j, i: (0, g0 + j)),
                  pl.BlockSpec((D, tn), lambda j, i: (0, g0 + nt + j))],
        out_specs=pl.BlockSpec((tm, tn), lambda j, i: (i, j)),
        out_shape=jax.ShapeDtypeStruct((T, D), BF16),
        compiler_params=_cparams(("arbitrary", "arbitrary")),
        name="gated_merge",
    )(attn, sg, h, wa, wb, wg, wg)


def _out_router_kernel(m_ref, x_ref, wo_ref, g_ref, wr_ref, br_ref,
                       x1_ref, h2_ref, sel_ref, gate_ref, *, n_parts):
    E = br_ref.shape[1]
    part = m_ref.shape[0] // n_parts
    for p in range(n_parts):
        rows = pl.ds(p * part, part)
        x1 = x_ref[rows, :] + jnp.dot(m_ref[rows, :], wo_ref[...], preferred_element_type=F32)
        x1_ref[rows, :] = x1
        ms = jnp.mean(x1 * x1, axis=-1, keepdims=True)
        h2 = x1 * lax.rsqrt(ms + NORM_EPS) * g_ref[...]
        h2_ref[rows, :] = _pack_pairs(h2)
        h2_hi = h2.astype(BF16)
        h2_lo = (h2 - h2_hi.astype(F32)).astype(BF16)
        pa = jnp.dot(h2_hi, wr_ref[...], preferred_element_type=F32)
        pb = jnp.dot(h2_lo, wr_ref[:, :E], preferred_element_type=F32)
        logits = pa[:, :E] + (pa[:, E:] + pb) + br_ref[...]
        lane = lax.broadcasted_iota(I32, logits.shape, 1)
        work = logits
        sel = jnp.zeros(logits.shape, F32)
        num = jnp.zeros(logits.shape, F32)
        denom = jnp.zeros((part, 1), F32)
        m0 = None
        for _ in range(TOP_K):
            m = jnp.max(work, axis=-1, keepdims=True)
            idx = jnp.min(jnp.where(work == m, lane, E), axis=-1, keepdims=True)
            onehot = lane == idx
            if m0 is None:
                m0 = m
            e = jnp.exp(m - m0)
            sel = jnp.where(onehot, 1.0, sel)
            num = jnp.where(onehot, e, num)
            denom = denom + e
            work = jnp.where(onehot, -jnp.inf, work)
        sel_ref[rows, :] = sel
        gate_ref[rows, :] = num / denom


def _out_router(merged, x, wo, g_ffn, w_router, b_router, to=512, n_parts=2):
    T, D = x.shape
    E = w_router.shape[1]
    w_hi = w_router.astype(BF16)
    w_lo = (w_router - w_hi.astype(F32)).astype(BF16)
    w_router = jnp.concatenate([w_hi, w_lo], axis=1)
    row = lambda i: (i, 0)
    fixed = lambda i: (0, 0)
    return pl.pallas_call(
        functools.partial(_out_router_kernel, n_parts=n_parts),
        grid=(T // to,),
        in_specs=[pl.BlockSpec((to, D), row), pl.BlockSpec((to, D), row),
                  pl.BlockSpec((D, D), fixed), pl.BlockSpec((1, D), fixed),
                  pl.BlockSpec((D, 2 * E), fixed), pl.BlockSpec((1, E), fixed)],
        out_specs=[pl.BlockSpec((to, D), row), pl.BlockSpec((to, D // 2), row),
                   pl.BlockSpec((to, E), row), pl.BlockSpec((to, E), row)],
        out_shape=[jax.ShapeDtypeStruct((T, D), F32), jax.ShapeDtypeStruct((T, D // 2), jnp.uint32),
                   jax.ShapeDtypeStruct((T, E), F32), jax.ShapeDtypeStruct((T, E), F32)],
        compiler_params=_cparams(("parallel",)),
        name="out_router",
    )(merged, x, wo, g_ffn.reshape(1, D), w_router, b_router.reshape(1, E))


def _routing_kernel(sel_ref, gate_ref, dest_ref, g4_ref, nblk_ref, start_ref, rank_ref, *, tile):
    T, E = sel_ref.shape
    nt = T // tile
    r = lax.broadcasted_iota(I32, (tile, tile), 0)
    c = lax.broadcasted_iota(I32, (tile, tile), 1)
    strict_lower = (c < r).astype(BF16)
    er = lax.broadcasted_iota(I32, (E, E), 0)
    ec = lax.broadcasted_iota(I32, (E, E), 1)
    strict_upper = (er < ec).astype(BF16)

    def pass1(t, carry):
        rows = pl.ds(pl.multiple_of(t * tile, tile), tile)
        a = sel_ref[rows, :]
        rank_ref[rows, :] = jnp.dot(strict_lower, a.astype(BF16), preferred_element_type=F32) + carry
        return carry + jnp.sum(a, axis=0, keepdims=True)

    counts = lax.fori_loop(0, nt, pass1, jnp.zeros((1, E), F32))
    nblk = jnp.floor((counts + (ROW_BLOCK - 1)) * (1.0 / ROW_BLOCK))
    start_blk = jnp.dot(nblk.astype(BF16), strict_upper, preferred_element_type=F32)
    nblk_ref[...] = nblk.astype(I32)
    start_ref[...] = start_blk.astype(I32)
    start_row = start_blk * float(ROW_BLOCK)
    lane = lax.broadcasted_iota(I32, (tile, 128), 1)

    def pass2(t, _):
        rows = pl.ds(pl.multiple_of(t * tile, tile), tile)
        a = sel_ref[rows, :]
        g = gate_ref[rows, :]
        dest_e = rank_ref[rows, :] + start_row
        slot = jnp.dot(a.astype(BF16), strict_upper, preferred_element_type=F32)
        d4 = jnp.zeros((tile, 128), F32)
        g4 = jnp.zeros((tile, 128), F32)
        for s in range(TOP_K):
            pick = (a > 0.5) & (slot == float(s))
            d4 = jnp.where(lane == s, jnp.sum(jnp.where(pick, dest_e, 0.0), axis=-1, keepdims=True), d4)
            g4 = jnp.where(lane == s, jnp.sum(jnp.where(pick, g, 0.0), axis=-1, keepdims=True), g4)
        dest_ref[rows, :] = d4.astype(I32)
        g4_ref[rows, :] = g4
        return 0

    lax.fori_loop(0, nt, pass2, 0)


def _routing(sel, gate, tile=256):
    T, E = sel.shape
    return pl.pallas_call(
        functools.partial(_routing_kernel, tile=tile),
        out_shape=[jax.ShapeDtypeStruct((T, 128), I32), jax.ShapeDtypeStruct((T, 128), F32),
                   jax.ShapeDtypeStruct((1, E), I32), jax.ShapeDtypeStruct((1, E), I32)],
        scratch_shapes=[pltpu.VMEM((T, E), F32)],
        compiler_params=pltpu.CompilerParams(vmem_limit_bytes=VMEM_LIMIT_BYTES),
        name="routing_ranks",
    )(sel, gate)


SC_CORES = 2
SC_SUBCORES = 16
SC_CHUNK = 32


def _sc_gather_rows(table, idx):
    n = idx.shape[0]
    W = table.shape[1]
    n_workers = SC_CORES * SC_SUBCORES
    per_worker = n // n_workers
    assert per_worker * n_workers == n and per_worker % SC_CHUNK == 0
    mesh = plsc.VectorSubcoreMesh(core_axis_name="c", subcore_axis_name="s",
                                  num_cores=SC_CORES, num_subcores=SC_SUBCORES)

    @functools.partial(
        pl.kernel, mesh=mesh,
        out_type=jax.ShapeDtypeStruct((n, W), table.dtype),
        scratch_types=[pltpu.VMEM((SC_CHUNK,), I32), pltpu.VMEM((SC_CHUNK, W), table.dtype),
                       pltpu.SemaphoreType.DMA],
        name="sc_gather_rows",
    )
    def gather(table_hbm, idx_hbm, out_hbm, idx_v, rows_v, sem):
        wid = lax.axis_index("s") * SC_CORES + lax.axis_index("c")
        base = wid * per_worker

        @pl.loop(0, per_worker // SC_CHUNK)
        def _(c):
            off = pl.multiple_of(base + c * SC_CHUNK, 8)
            pltpu.sync_copy(idx_hbm.at[pl.ds(off, SC_CHUNK)], idx_v)
            pltpu.async_copy(table_hbm.at[idx_v], rows_v, sem).wait()
            pltpu.sync_copy(rows_v, out_hbm.at[pl.ds(off, SC_CHUNK)])

    return gather(table, idx)


def _sc_scatter_rows(rows, idx, n_out, n_slots):
    T, W = rows.shape
    n_workers = SC_CORES * SC_SUBCORES
    per_worker = T // n_workers
    assert per_worker * n_workers == T and per_worker % SC_CHUNK == 0
    mesh = plsc.VectorSubcoreMesh(core_axis_name="c", subcore_axis_name="s",
                                  num_cores=SC_CORES, num_subcores=SC_SUBCORES)

    @functools.partial(
        pl.kernel, mesh=mesh,
        out_type=jax.ShapeDtypeStruct((n_out, W), rows.dtype),
        scratch_types=[pltpu.VMEM((SC_CHUNK,), I32), pltpu.VMEM((SC_CHUNK, W), rows.dtype)],
        name="sc_scatter_rows",
    )
    def scatter(rows_hbm, idx_hbm, out_hbm, idx_v, rows_v):
        wid = lax.axis_index("s") * SC_CORES + lax.axis_index("c")
        base = wid * per_worker

        @pl.loop(0, per_worker // SC_CHUNK)
        def _(c):
            off = pl.multiple_of(base + c * SC_CHUNK, 8)
            pltpu.sync_copy(rows_hbm.at[pl.ds(off, SC_CHUNK)], rows_v)
            for k in range(n_slots):
                pltpu.sync_copy(idx_hbm.at[pl.ds(pl.multiple_of(k * T + off, 8), SC_CHUNK)], idx_v)
                pltpu.sync_copy(rows_v, out_hbm.at[idx_v])

    return scatter(rows, idx)


def _expert_kernel(ie_ref, ib_ref, ins_ref,
                   xs_hbm, wup_ref, bup_ref, wdn_ref, bdn_ref, perm_ref, ys_hbm,
                   xg_ref, acc_ref, yst_ref, wupb_ref, wdnb_ref, gsem, osem, *, n_ff_tiles):
    i = pl.program_id(0)
    j = pl.program_id(1)
    n_items = pl.num_programs(0)
    nsub = ins_ref[i]
    slot = i % 2

    def for_regions(n_blocks, fn):
        first = jnp.int32(0)
        for count in REGION_BLOCKS:
            present = (n_blocks & count) != 0
            pl.when(present)(functools.partial(fn, first, count))
            first = first + jnp.where(present, count, 0)

    def rows_of(first, count):
        return pl.ds(pl.multiple_of(first * ROW_BLOCK, ROW_BLOCK), count * ROW_BLOCK)

    def in_copy(item, dst_slot, first, count):
        src = pl.multiple_of((ib_ref[item] + first) * ROW_BLOCK, ROW_BLOCK)
        return pltpu.make_async_copy(xs_hbm.at[pl.ds(src, count * ROW_BLOCK)],
                                     xg_ref.at[dst_slot, rows_of(first, count)], gsem.at[dst_slot])

    def fetch_item(item, n_blocks, dst_slot):
        for_regions(n_blocks, lambda first, count: in_copy(item, dst_slot, first, count).start())

    @pl.when(j == 0)
    def _():
        @pl.when(i == 0)
        def _():
            fetch_item(0, nsub, 0)

        nxt = jnp.minimum(i + 1, n_items - 1)
        fetch_item(nxt, jnp.where(i + 1 < n_items, ins_ref[nxt], 0), 1 - slot)

        def arrive(first, count):
            in_copy(i, slot, first, count).wait()
            acc_ref[rows_of(first, count), :] = jnp.broadcast_to(
                bdn_ref[0], (count * ROW_BLOCK, acc_ref.shape[1]))

        for_regions(nsub, arrive)

    @pl.when(nsub > 0)
    def _():
        bup = bup_ref[0]
        perm = perm_ref[...]
        half = perm.shape[0] // 2

        def run_blocks(first, count):
            if first == 0:
                wup = wup_ref[0].astype(BF16)
                wdn = wdn_ref[0].astype(BF16)
                wupb_ref[...] = wup
                wdnb_ref[...] = wdn
            else:
                wup = wupb_ref[...]
                wdn = wdnb_ref[...]
            rows = pl.ds(first * ROW_BLOCK, count * ROW_BLOCK)
            xb = _unpack_pairs(xg_ref[slot, rows, :]).astype(BF16)
            gu = (jnp.dot(xb, wup, preferred_element_type=F32) + bup).astype(BF16)
            glu_parts, lin_parts = [], []
            for p in range(gu.shape[1] // perm.shape[0]):
                gp = jnp.dot(gu[:, p * perm.shape[0]:(p + 1) * perm.shape[0]], perm,
                             preferred_element_type=F32)
                glu_parts.append(gp[:, :half])
                lin_parts.append(gp[:, half:])
            x_glu = jnp.minimum(jnp.concatenate(glu_parts, axis=1), SWIGLU_LIMIT)
            x_lin = jnp.clip(jnp.concatenate(lin_parts, axis=1), -SWIGLU_LIMIT, SWIGLU_LIMIT)
            act = x_glu * jax.nn.sigmoid(SWIGLU_ALPHA * x_glu) * (x_lin + 1.0)
            acc_ref[rows, :] += jnp.dot(act.astype(BF16), wdn, preferred_element_type=F32)

        for n_blocks, plan in COMPUTE_PLANS.items():
            for first, count in plan:
                shared = [n for n, p in COMPUTE_PLANS.items() if (first, count) in p]
                if n_blocks == shared[0]:
                    cond = functools.reduce(jnp.logical_or, [nsub == n for n in shared])
                    pl.when(cond)(functools.partial(run_blocks, first, count))

    def out_copy(item, first, count):
        dst = pl.multiple_of((ib_ref[item] + first) * ROW_BLOCK, ROW_BLOCK)
        return pltpu.make_async_copy(yst_ref.at[rows_of(first, count)],
                                     ys_hbm.at[pl.ds(dst, count * ROW_BLOCK)], osem)

    @pl.when(j == n_ff_tiles - 1)
    def _():
        prev = jnp.maximum(i - 1, 0)
        for_regions(jnp.where(i > 0, ins_ref[prev], 0),
                    lambda first, count: out_copy(prev, first, count).wait())

        def leave(first, count):
            rows = rows_of(first, count)
            yst_ref[rows, :] = _pack_pairs(acc_ref[rows, :])
            out_copy(i, first, count).start()

        for_regions(nsub, leave)

        @pl.when(i == n_items - 1)
        def _():
            for_regions(nsub, lambda first, count: out_copy(i, first, count).wait())


def _experts(xs, item_e, item_b, item_n, n_active, w_up, b_up, w_down, b_down):
    n_rows, Dp = xs.shape
    D = 2 * Dp
    E, _, F2 = w_up.shape
    F = F2 // 2
    J = F // FF_TILE
    half = 128
    perm = np.zeros((2 * half, 2 * half), np.float32)
    perm[2 * np.arange(half), np.arange(half)] = 1.0
    perm[2 * np.arange(half) + 1, half + np.arange(half)] = 1.0

    def jj(i, j, ins):
        return jnp.where(ins[i] > 0, j, J - 1)

    grid_spec = pltpu.PrefetchScalarGridSpec(
        num_scalar_prefetch=3,
        grid=(n_active, J),
        in_specs=[pl.BlockSpec(memory_space=pl.ANY),
                  pl.BlockSpec((1, D, 2 * FF_TILE), lambda i, j, ie, ib, ins: (ie[i], 0, jj(i, j, ins))),
                  pl.BlockSpec((1, 1, 2 * FF_TILE), lambda i, j, ie, ib, ins: (ie[i], 0, jj(i, j, ins))),
                  pl.BlockSpec((1, FF_TILE, D), lambda i, j, ie, ib, ins: (ie[i], jj(i, j, ins), 0)),
                  pl.BlockSpec((1, 1, D), lambda i, j, ie, ib, ins: (ie[i], 0, 0)),
                  pl.BlockSpec((2 * half, 2 * half), lambda i, j, ie, ib, ins: (0, 0))],
        out_specs=pl.BlockSpec(memory_space=pl.ANY),
        scratch_shapes=[pltpu.VMEM((2, ITEM_ROWS, Dp), jnp.uint32),
                        pltpu.VMEM((ITEM_ROWS, D), F32),
                        pltpu.VMEM((ITEM_ROWS, Dp), jnp.uint32),
                        pltpu.VMEM((D, 2 * FF_TILE), BF16),
                        pltpu.VMEM((FF_TILE, D), BF16),
                        pltpu.SemaphoreType.DMA((2,)), pltpu.SemaphoreType.DMA(())],
    )
    return pl.pallas_call(
        functools.partial(_expert_kernel, n_ff_tiles=J),
        grid_spec=grid_spec,
        out_shape=jax.ShapeDtypeStruct((n_rows, Dp), jnp.uint32),
        compiler_params=_cparams(("arbitrary", "arbitrary")),
        name="expert_ffn",
    )(item_e, item_b, item_n,
      xs, w_up, b_up.reshape(E, 1, F2), w_down, b_down.reshape(E, 1, D), jnp.asarray(perm, BF16))


def _combine_kernel(*refs):
    yk_refs, (g4_ref, x1_ref, g_ref, o_ref) = refs[:TOP_K], refs[TOP_K:]
    y = x1_ref[...]
    g4 = g4_ref[...]
    for k in range(TOP_K):
        y = y + g4[:, k:k + 1] * _unpack_pairs(yk_refs[k][...])
    ms = jnp.mean(y * y, axis=-1, keepdims=True)
    o_ref[...] = y * lax.rsqrt(ms + NORM_EPS) * g_ref[...]


def _combine(yk, g4, x1, g_final, tc=256):
    T, D = x1.shape
    nt = T // tc
    slot_specs = [pl.BlockSpec((tc, D // 2), functools.partial(lambda i, k: (k * nt + i, 0), k=k))
                  for k in range(TOP_K)]
    return pl.pallas_call(
        _combine_kernel,
        grid=(nt,),
        in_specs=slot_specs + [pl.BlockSpec((tc, 128), lambda i: (i, 0)),
                               pl.BlockSpec((tc, D), lambda i: (i, 0)),
                               pl.BlockSpec((1, D), lambda i: (0, 0))],
        out_specs=pl.BlockSpec((tc, D), lambda i: (i, 0)),
        out_shape=jax.ShapeDtypeStruct((T, D), F32),
        compiler_params=_cparams(("parallel",)),
        name="combine_norm",
    )(*([yk] * TOP_K), g4, x1, g_final.reshape(1, D))


def _work_items(nblk, start_blk, n_items):
    E = nblk.shape[0]
    per_e = (nblk + ITEM_BLOCKS - 1) // ITEM_BLOCKS
    ends = jnp.cumsum(per_e)
    total = ends[-1]
    idx = jnp.arange(n_items, dtype=I32)
    e = jnp.minimum(jnp.searchsorted(ends, idx, side="right"), E - 1).astype(I32)
    local = idx - (ends[e] - per_e[e])
    active = idx < total
    last_e = e[jnp.maximum(total - 1, 0)]
    item_e = jnp.where(active, e, last_e).astype(I32)
    item_b = jnp.where(active, start_blk[e] + local * ITEM_BLOCKS, 0).astype(I32)
    item_n = jnp.where(active, jnp.clip(nblk[e] - local * ITEM_BLOCKS, 0, ITEM_BLOCKS), 0).astype(I32)
    return item_e, item_b, item_n, jnp.maximum(total, 1).astype(I32)


def kernel(x, g_mix, w_in, b_forget, g_v_ln, b_v_ln, w_spatial, b_spatial, w_branch_attn, w_branch_gmlp, w_out, g_ffn, w_router, b_router, w_expert_up, b_expert_up, w_expert_down, b_expert_down, g_final):
    B, S, D = x.shape
    T = B * S
    n_heads = b_forget.shape[0]
    attn_w = n_heads * HEAD_DIM
    gmlp_w = g_v_ln.shape[0]
    E = w_router.shape[1]
    off_f = 3 * attn_w
    off_z = off_f + n_heads
    off_g = off_z + 2 * gmlp_w

    x2 = x.reshape(T, D)
    wft = w_in[:, off_f:off_z].T.astype(BF16)
    wqkv = w_in[:, :off_f]
    wzg = w_in[:, off_z:].astype(BF16)
    wa = w_branch_attn.astype(BF16)
    wb = w_branch_gmlp.astype(BF16)
    wo = w_out.astype(BF16)

    h = _rmsnorm(x2, g_mix, BF16)
    qkv = _project(h, wqkv, 3 * attn_w, attn_w, LOG2E / math.sqrt(HEAD_DIM), BF16)
    c_row = _forget_cumsum(h, wft, b_forget, B, S)
    attn = _attention(qkv, c_row, B, S, n_heads)
    sg = _gmlp(h, wzg, g_v_ln, b_v_ln, w_spatial, b_spatial)
    merged = _merge(attn, sg, h, wa, wb, wzg, off_g - off_z)
    x1, h2p, sel, gate = _out_router(merged, x2, wo, g_ffn, w_router, b_router)

    dest4, g4, nblk, start_blk = _routing(sel, gate)
    n_rows = T * TOP_K + E * ROW_BLOCK
    dest_slots = dest4[:, :TOP_K].T.reshape(TOP_K * T)
    max_blocks = n_rows // ROW_BLOCK
    n_items = E + -(-(max_blocks - E) // ITEM_BLOCKS) + 1
    item_e, item_b, item_n, n_active = _work_items(nblk[0], start_blk[0], n_items)
    xs = _sc_scatter_rows(h2p, dest_slots, n_rows, TOP_K)
    ys = _experts(xs, item_e, item_b, item_n, n_active,
                  w_expert_up, b_expert_up, w_expert_down, b_expert_down)
    yk = _sc_gather_rows(ys, dest_slots)
    out = _combine(yk, g4, x1, g_final)
    return out.reshape(B, S, D)
```

```python
import functools
import math

import jax
import jax.numpy as jnp
import numpy as np
from jax import lax
from jax.experimental import pallas as pl
from jax.experimental.pallas import tpu as pltpu
from jax.experimental.pallas import tpu_sc as plsc

F32 = jnp.float32
BF16 = jnp.bfloat16
I32 = jnp.int32

NORM_EPS = 1e-5
LANES = 128
HEAD_DIM = 128
CHUNK = 128
GROUP_DIM = 128
TOP_K = 4
SWIGLU_ALPHA = 1.702
SWIGLU_LIMIT = 7.0
LOG2E = math.log2(math.e)

VMEM_LIMIT_BYTES = 56 * 1024 * 1024

ROW_BLOCK = 256
ITEM_BLOCKS = 5
ITEM_ROWS = ITEM_BLOCKS * ROW_BLOCK
REGION_BLOCKS = (4, 2, 1)
assert sum(REGION_BLOCKS) >= ITEM_BLOCKS
COMPUTE_PLANS = {5: ((0, 5),), 4: ((0, 4),), 3: ((0, 2), (2, 1)), 2: ((0, 2),), 1: ((0, 1),)}
assert set(COMPUTE_PLANS) == set(range(1, ITEM_BLOCKS + 1))
FF_TILE = 256
DMA_UNROLL = 8


def _cparams(sem, **kw):
    return pltpu.CompilerParams(dimension_semantics=sem, vmem_limit_bytes=VMEM_LIMIT_BYTES, **kw)


def _pack_pairs(x):
    c = x.shape[1] // 2
    hi = lax.bitcast_convert_type(x[:, :c].astype(BF16).astype(F32), jnp.uint32)
    lo = lax.bitcast_convert_type(x[:, c:].astype(BF16).astype(F32), jnp.uint32)
    return hi | (lo >> 16)


def _unpack_pairs(w):
    hi = lax.bitcast_convert_type(w & jnp.uint32(0xFFFF0000), F32)
    lo = lax.bitcast_convert_type(w << 16, F32)
    return jnp.concatenate([hi, lo], axis=1)


def _rmsnorm_kernel(x_ref, g_ref, o_ref):
    x = x_ref[...]
    ms = jnp.mean(x * x, axis=-1, keepdims=True)
    o_ref[...] = (x * lax.rsqrt(ms + NORM_EPS) * g_ref[...]).astype(o_ref.dtype)


def _rmsnorm(x, g, out_dtype, tm=512):
    T, D = x.shape
    return pl.pallas_call(
        _rmsnorm_kernel,
        grid=(T // tm,),
        in_specs=[pl.BlockSpec((tm, D), lambda i: (i, 0)), pl.BlockSpec((1, D), lambda i: (0, 0))],
        out_specs=pl.BlockSpec((tm, D), lambda i: (i, 0)),
        out_shape=jax.ShapeDtypeStruct((T, D), out_dtype),
        compiler_params=_cparams(("parallel",)),
        name="rmsnorm",
    )(x, g.reshape(1, D))


def _proj_kernel(h_ref, w_ref, o_ref, wbf_ref, *, n_scaled, scale):
    j = pl.program_id(0)

    @pl.when(pl.program_id(1) == 0)
    def _():
        wbf_ref[...] = w_ref[...].astype(BF16)

    acc = jnp.dot(h_ref[...], wbf_ref[...], preferred_element_type=F32)
    o_ref[...] = (acc * jnp.where(j < n_scaled, scale, 1.0)).astype(o_ref.dtype)


def _project(h, w, n_cols, n_scaled_cols, scale, out_dtype, tm=1024, tn=512):
    T, D = h.shape
    return pl.pallas_call(
        functools.partial(_proj_kernel, n_scaled=n_scaled_cols // tn, scale=scale),
        grid=(n_cols // tn, T // tm),
        in_specs=[pl.BlockSpec((tm, D), lambda j, i: (i, 0)),
                  pl.BlockSpec((D, tn), lambda j, i: (0, j))],
        out_specs=pl.BlockSpec((tm, tn), lambda j, i: (i, j)),
        out_shape=jax.ShapeDtypeStruct((T, n_cols), out_dtype),
        scratch_shapes=[pltpu.VMEM((D, tn), BF16)],
        compiler_params=_cparams(("arbitrary", "arbitrary")),
        name="qkv_proj",
    )(h, w)


def _forget_kernel(h_ref, wf_ref, bf_ref, c_ref):
    H = bf_ref.shape[0]
    f = jnp.dot(h_ref[...], wf_ref[...], preferred_element_type=F32)
    ft = f.T[:H]
    c = jax.nn.log_sigmoid(ft + bf_ref[...])
    S = c.shape[1]
    lane = lax.broadcasted_iota(I32, c.shape, 1)
    shift = 1
    while shift < S:
        c = c + jnp.where(lane >= shift, pltpu.roll(c, shift, axis=1), 0.0)
        shift *= 2
    c_ref[0] = c * LOG2E


def _forget_cumsum(h, wf, b_forget, B, S):
    T, D = h.shape
    H = b_forget.shape[0]
    return pl.pallas_call(
        _forget_kernel,
        grid=(B,),
        in_specs=[pl.BlockSpec((S, D), lambda b: (b, 0)),
                  pl.BlockSpec((D, LANES), lambda b: (0, 0)),
                  pl.BlockSpec((H, 1), lambda b: (0, 0))],
        out_specs=pl.BlockSpec((1, H, S), lambda b: (b, 0, 0)),
        out_shape=jax.ShapeDtypeStruct((B, H, S), F32),
        compiler_params=_cparams(("parallel",)),
        name="forget_cumsum",
    )(h, wf, b_forget.reshape(H, 1))


def _attn_kernel(q_ref, k_ref, v_ref, crow_ref, o_ref, vaug_ref, m_ref, acc_ref, *, n_heads, tq):
    i = pl.program_id(1)

    @pl.when(i == 0)
    def _():
        ones = jnp.ones((v_ref.shape[0], HEAD_DIM), BF16)
        for h in range(n_heads):
            vaug_ref[h, :, :HEAD_DIM] = v_ref[:, h * HEAD_DIM:(h + 1) * HEAD_DIM]
            vaug_ref[h, :, HEAD_DIM:] = ones

    m_ref[...] = jnp.full(m_ref.shape, -jnp.inf, F32)
    acc_ref[...] = jnp.zeros(acc_ref.shape, F32)
    row = lax.broadcasted_iota(I32, (tq, tq), 0)
    col = lax.broadcasted_iota(I32, (tq, tq), 1)
    causal = col <= row

    def step(j, masked):
        keys = pl.ds(pl.multiple_of(j * tq, tq), tq)
        for h in range(n_heads):
            hs = slice(h * HEAD_DIM, (h + 1) * HEAD_DIM)
            s = lax.dot_general(q_ref[:, hs], k_ref[keys, hs], (((1,), (1,)), ((), ())),
                                preferred_element_type=F32) - crow_ref[0, h, j]
            if masked:
                s = jnp.where(causal, s, -jnp.inf)
            m_old = m_ref[h]
            m_new = jnp.maximum(m_old, jnp.max(s, axis=-1, keepdims=True))
            alpha = jnp.exp2(m_old - m_new)
            p = jnp.exp2(s - jnp.concatenate([m_new] * (tq // HEAD_DIM), axis=1))
            m_ref[h] = m_new
            pv = jnp.dot(p.astype(BF16), vaug_ref[h, keys, :], preferred_element_type=F32)
            acc_ref[h] = jnp.concatenate([alpha, alpha], axis=1) * acc_ref[h] + pv

    def body(j, _):
        step(j, False)
        return 0

    lax.fori_loop(0, i, body, 0)
    step(i, True)
    for h in range(n_heads):
        acc = acc_ref[h]
        o_ref[:, h * HEAD_DIM:(h + 1) * HEAD_DIM] = (acc[:, :HEAD_DIM] / acc[:, HEAD_DIM:]).astype(o_ref.dtype)


def _attention(qkv, c_row, B, S, n_heads, tq=256):
    T = qkv.shape[0]
    W = n_heads * HEAD_DIM
    nq = S // tq
    c_row5 = c_row.reshape(B, n_heads, nq, 1, tq)
    return pl.pallas_call(
        functools.partial(_attn_kernel, n_heads=n_heads, tq=tq),
        grid=(B, nq),
        in_specs=[pl.BlockSpec((tq, W), lambda b, i: (b * nq + i, 0)),
                  pl.BlockSpec((S, W), lambda b, i: (b, 1)),
                  pl.BlockSpec((S, W), lambda b, i: (b, 2)),
                  pl.BlockSpec((1, n_heads, nq, 1, tq), lambda b, i: (b, 0, 0, 0, 0))],
        out_specs=pl.BlockSpec((tq, W), lambda b, i: (b * nq + i, 0)),
        out_shape=jax.ShapeDtypeStruct((T, W), BF16),
        scratch_shapes=[pltpu.VMEM((n_heads, S, 2 * HEAD_DIM), BF16),
                        pltpu.VMEM((n_heads, tq, HEAD_DIM), F32),
                        pltpu.VMEM((n_heads, tq, 2 * HEAD_DIM), F32)],
        compiler_params=_cparams(("arbitrary", "arbitrary")),
        name="fox_attention",
    )(qkv, qkv, qkv, c_row5)


def _gmlp_kernel(h_ref, wz_ref, g_ref, b_ref, ws_ref, bst_ref, o_ref, *, n_groups):
    z = jnp.dot(h_ref[...], wz_ref[...], preferred_element_type=F32)
    z = 0.5 * z * (1.0 + lax.erf(z * (1.0 / math.sqrt(2.0))))
    W = z.shape[1] // 2
    u = z[:, :W]
    v = z[:, W:]
    mu = jnp.mean(v, axis=-1, keepdims=True)
    var = jnp.mean(jnp.square(v - mu), axis=-1, keepdims=True)
    vn = (v - mu) * lax.rsqrt(var + NORM_EPS) * g_ref[...] + b_ref[...]
    row = lax.broadcasted_iota(I32, (CHUNK, CHUNK), 0)
    col = lax.broadcasted_iota(I32, (CHUNK, CHUNK), 1)
    tril = col <= row
    tg = z.shape[0]
    for g in range(n_groups):
        gs = slice(g * GROUP_DIM, (g + 1) * GROUP_DIM)
        wg = jnp.where(tril, ws_ref[g], 0.0).astype(BF16)
        bias = bst_ref[:, g:g + 1]
        for c in range(tg // CHUNK):
            cs = slice(c * CHUNK, (c + 1) * CHUNK)
            mixed = jnp.dot(wg, vn[cs, gs].astype(BF16), preferred_element_type=F32) + bias
            o_ref[cs, gs] = (u[cs, gs] * mixed).astype(o_ref.dtype)


def _gmlp(h, wz, g_v_ln, b_v_ln, w_spatial, b_spatial, tg=512):
    T, D = h.shape
    W = g_v_ln.shape[0]
    W2 = 2 * W
    G = w_spatial.shape[0]
    return pl.pallas_call(
        functools.partial(_gmlp_kernel, n_groups=G),
        grid=(T // tg,),
        in_specs=[pl.BlockSpec((tg, D), lambda i: (i, 0)),
                  pl.BlockSpec((D, W2), lambda i: (0, 0)),
                  pl.BlockSpec((1, W), lambda i: (0, 0)),
                  pl.BlockSpec((1, W), lambda i: (0, 0)),
                  pl.BlockSpec((G, CHUNK, CHUNK), lambda i: (0, 0, 0)),
                  pl.BlockSpec((CHUNK, G), lambda i: (0, 0))],
        out_specs=pl.BlockSpec((tg, W), lambda i: (i, 0)),
        out_shape=jax.ShapeDtypeStruct((T, W), BF16),
        compiler_params=_cparams(("parallel",)),
        name="gmlp",
    )(h, wz, g_v_ln.reshape(1, W), b_v_ln.reshape(1, W), w_spatial, b_spatial.T)


def _merge_kernel(attn_ref, sg_ref, h_ref, wa_ref, wb_ref, wga_ref, wgb_ref, o_ref):
    h = h_ref[...]
    a = jnp.dot(attn_ref[...], wa_ref[...], preferred_element_type=F32)
    ga = jnp.dot(h, wga_ref[...], preferred_element_type=F32)
    m = jax.nn.sigmoid(ga) * a
    b = jnp.dot(sg_ref[...], wb_ref[...], preferred_element_type=F32)
    gb = jnp.dot(h, wgb_ref[...], preferred_element_type=F32)
    o_ref[...] = (m + jax.nn.sigmoid(gb) * b).astype(o_ref.dtype)


def _merge(attn, sg, h, wa, wb, wg, g_off, tm=512, tn=512):
    T, D = h.shape
    Wa = attn.shape[1]
    Wb = sg.shape[1]
    nt = D // tn
    g0 = g_off // tn
    return pl.pallas_call(
        _merge_kernel,
        grid=(nt, T // tm),
        in_specs=[pl.BlockSpec((tm, Wa), lambda j, i: (i, 0)),
                  pl.BlockSpec((tm, Wb), lambda j, i: (i, 0)),
                  pl.BlockSpec((tm, D), lambda j, i: (i, 0)),
                  pl.BlockSpec((Wa, tn), lambda j, i: (0, j)),
                  pl.BlockSpec((Wb, tn), lambda j, i: (0, j)),
                  pl.BlockSpec((D, tn), lambda j, i: (0, g0 + j)),
                  pl.BlockSpec((D, tn), lambda j, i: (0, g0 + nt + j))],
        out_specs=pl.BlockSpec((tm, tn), lambda j, i: (i, j)),
        out_shape=jax.ShapeDtypeStruct((T, D), BF16),
        compiler_params=_cparams(("arbitrary", "arbitrary")),
        name="gated_merge",
    )(attn, sg, h, wa, wb, wg, wg)


def _out_router_kernel(m_ref, x_ref, wo_ref, g_ref, wr_ref, br_ref,
                       x1_ref, h2_ref, sel_ref, gate_ref, *, n_parts):
    E = br_ref.shape[1]
    part = m_ref.shape[0] // n_parts
    for p in range(n_parts):
        rows = pl.ds(p * part, part)
        x1 = x_ref[rows, :] + jnp.dot(m_ref[rows, :], wo_ref[...], preferred_element_type=F32)
        x1_ref[rows, :] = x1
        ms = jnp.mean(x1 * x1, axis=-1, keepdims=True)
        h2 = x1 * lax.rsqrt(ms + NORM_EPS) * g_ref[...]
        h2_ref[rows, :] = _pack_pairs(h2)
        h2_hi = h2.astype(BF16)
        h2_lo = (h2 - h2_hi.astype(F32)).astype(BF16)
        pa = jnp.dot(h2_hi, wr_ref[...], preferred_element_type=F32)
        pb = jnp.dot(h2_lo, wr_ref[:, :E], preferred_element_type=F32)
        logits = pa[:, :E] + (pa[:, E:] + pb) + br_ref[...]
        lane = lax.broadcasted_iota(I32, logits.shape, 1)
        work = logits
        sel = jnp.zeros(logits.shape, F32)
        num = jnp.zeros(logits.shape, F32)
        denom = jnp.zeros((part, 1), F32)
        m0 = None
        for _ in range(TOP_K):
            m = jnp.max(work, axis=-1, keepdims=True)
            idx = jnp.min(jnp.where(work == m, lane, E), axis=-1, keepdims=True)
            onehot = lane == idx
            if m0 is None:
                m0 = m
            e = jnp.exp(m - m0)
            sel = jnp.where(onehot, 1.0, sel)
            num = jnp.where(onehot, e, num)
            denom = denom + e
            work = jnp.where(onehot, -jnp.inf, work)
        sel_ref[rows, :] = sel
        gate_ref[rows, :] = num / denom


def _out_router(merged, x, wo, g_ffn, w_router, b_router, to=512, n_parts=2):
    T, D = x.shape
    E = w_router.shape[1]
    w_hi = w_router.astype(BF16)
    w_lo = (w_router - w_hi.astype(F32)).astype(BF16)
    w_router = jnp.concatenate([w_hi, w_lo], axis=1)
    row = lambda i: (i, 0)
    fixed = lambda i: (0, 0)
    return pl.pallas_call(
        functools.partial(_out_router_kernel, n_parts=n_parts),
        grid=(T // to,),
        in_specs=[pl.BlockSpec((to, D), row), pl.BlockSpec((to, D), row),
                  pl.BlockSpec((D, D), fixed), pl.BlockSpec((1, D), fixed),
                  pl.BlockSpec((D, 2 * E), fixed), pl.BlockSpec((1, E), fixed)],
        out_specs=[pl.BlockSpec((to, D), row), pl.BlockSpec((to, D // 2), row),
                   pl.BlockSpec((to, E), row), pl.BlockSpec((to, E), row)],
        out_shape=[jax.ShapeDtypeStruct((T, D), F32), jax.ShapeDtypeStruct((T, D // 2), jnp.uint32),
                   jax.ShapeDtypeStruct((T, E), F32), jax.ShapeDtypeStruct((T, E), F32)],
        compiler_params=_cparams(("parallel",)),
        name="out_router",
    )(merged, x, wo, g_ffn.reshape(1, D), w_router, b_router.reshape(1, E))


def _routing_kernel(sel_ref, gate_ref, dest_ref, g4_ref, nblk_ref, start_ref, rank_ref, *, tile):
    T, E = sel_ref.shape
    nt = T // tile
    r = lax.broadcasted_iota(I32, (tile, tile), 0)
    c = lax.broadcasted_iota(I32, (tile, tile), 1)
    strict_lower = (c < r).astype(BF16)
    er = lax.broadcasted_iota(I32, (E, E), 0)
    ec = lax.broadcasted_iota(I32, (E, E), 1)
    strict_upper = (er < ec).astype(BF16)

    def pass1(t, carry):
        rows = pl.ds(pl.multiple_of(t * tile, tile), tile)
        a = sel_ref[rows, :]
        rank_ref[rows, :] = jnp.dot(strict_lower, a.astype(BF16), preferred_element_type=F32) + carry
        return carry + jnp.sum(a, axis=0, keepdims=True)

    counts = lax.fori_loop(0, nt, pass1, jnp.zeros((1, E), F32))
    nblk = jnp.floor((counts + (ROW_BLOCK - 1)) * (1.0 / ROW_BLOCK))
    start_blk = jnp.dot(nblk.astype(BF16), strict_upper, preferred_element_type=F32)
    nblk_ref[...] = nblk.astype(I32)
    start_ref[...] = start_blk.astype(I32)
    start_row = start_blk * float(ROW_BLOCK)
    lane = lax.broadcasted_iota(I32, (tile, 128), 1)

    def pass2(t, _):
        rows = pl.ds(pl.multiple_of(t * tile, tile), tile)
        a = sel_ref[rows, :]
        g = gate_ref[rows, :]
        dest_e = rank_ref[rows, :] + start_row
        slot = jnp.dot(a.astype(BF16), strict_upper, preferred_element_type=F32)
        d4 = jnp.zeros((tile, 128), F32)
        g4 = jnp.zeros((tile, 128), F32)
        for s in range(TOP_K):
            pick = (a > 0.5) & (slot == float(s))
            d4 = jnp.where(lane == s, jnp.sum(jnp.where(pick, dest_e, 0.0), axis=-1, keepdims=True), d4)
            g4 = jnp.where(lane == s, jnp.sum(jnp.where(pick, g, 0.0), axis=-1, keepdims=True), g4)
        dest_ref[rows, :] = d4.astype(I32)
        g4_ref[rows, :] = g4
        return 0

    lax.fori_loop(0, nt, pass2, 0)


def _routing(sel, gate, tile=256):
    T, E = sel.shape
    return pl.pallas_call(
        functools.partial(_routing_kernel, tile=tile),
        out_shape=[jax.ShapeDtypeStruct((T, 128), I32), jax.ShapeDtypeStruct((T, 128), F32),
                   jax.ShapeDtypeStruct((1, E), I32), jax.ShapeDtypeStruct((1, E), I32)],
        scratch_shapes=[pltpu.VMEM((T, E), F32)],
        compiler_params=pltpu.CompilerParams(vmem_limit_bytes=VMEM_LIMIT_BYTES),
        name="routing_ranks",
    )(sel, gate)


SC_CORES = 2
SC_SUBCORES = 16
SC_CHUNK = 64


def _sc_gather_rows(table, idx):
    n = idx.shape[0]
    W = table.shape[1]
    n_workers = SC_CORES * SC_SUBCORES
    per_worker = n // n_workers
    assert per_worker * n_workers == n and per_worker % SC_CHUNK == 0
    mesh = plsc.VectorSubcoreMesh(core_axis_name="c", subcore_axis_name="s",
                                  num_cores=SC_CORES, num_subcores=SC_SUBCORES)

    @functools.partial(
        pl.kernel, mesh=mesh,
        out_type=jax.ShapeDtypeStruct((n, W), table.dtype),
        scratch_types=[pltpu.VMEM((SC_CHUNK,), I32), pltpu.VMEM((SC_CHUNK, W), table.dtype),
                       pltpu.SemaphoreType.DMA],
        name="sc_gather_rows",
    )
    def gather(table_hbm, idx_hbm, out_hbm, idx_v, rows_v, sem):
        wid = lax.axis_index("s") * SC_CORES + lax.axis_index("c")
        base = wid * per_worker

        @pl.loop(0, per_worker // SC_CHUNK)
        def _(c):
            off = pl.multiple_of(base + c * SC_CHUNK, 8)
            pltpu.sync_copy(idx_hbm.at[pl.ds(off, SC_CHUNK)], idx_v)
            pltpu.async_copy(table_hbm.at[idx_v], rows_v, sem).wait()
            pltpu.sync_copy(rows_v, out_hbm.at[pl.ds(off, SC_CHUNK)])

    return gather(table, idx)


def _sc_scatter_rows(rows, idx, n_out, n_slots):
    T, W = rows.shape
    n_workers = SC_CORES * SC_SUBCORES
    per_worker = T // n_workers
    assert per_worker * n_workers == T and per_worker % SC_CHUNK == 0
    mesh = plsc.VectorSubcoreMesh(core_axis_name="c", subcore_axis_name="s",
                                  num_cores=SC_CORES, num_subcores=SC_SUBCORES)

    @functools.partial(
        pl.kernel, mesh=mesh,
        out_type=jax.ShapeDtypeStruct((n_out, W), rows.dtype),
        scratch_types=[pltpu.VMEM((SC_CHUNK,), I32), pltpu.VMEM((SC_CHUNK, W), rows.dtype)],
        name="sc_scatter_rows",
    )
    def scatter(rows_hbm, idx_hbm, out_hbm, idx_v, rows_v):
        wid = lax.axis_index("s") * SC_CORES + lax.axis_index("c")
        base = wid * per_worker

        @pl.loop(0, per_worker // SC_CHUNK)
        def _(c):
            off = pl.multiple_of(base + c * SC_CHUNK, 8)
            pltpu.sync_copy(rows_hbm.at[pl.ds(off, SC_CHUNK)], rows_v)
            for k in range(n_slots):
                pltpu.sync_copy(idx_hbm.at[pl.ds(pl.multiple_of(k * T + off, 8), SC_CHUNK)], idx_v)
                pltpu.sync_copy(rows_v, out_hbm.at[idx_v])

    return scatter(rows, idx)


def _expert_kernel(ie_ref, ib_ref, ins_ref,
                   xs_hbm, wup_ref, bup_ref, wdn_ref, bdn_ref, perm_ref, ys_hbm,
                   xg_ref, acc_ref, yst_ref, wupb_ref, wdnb_ref, gsem, osem, *, n_ff_tiles):
    i = pl.program_id(0)
    j = pl.program_id(1)
    n_items = pl.num_programs(0)
    nsub = ins_ref[i]
    slot = i % 2

    def for_regions(n_blocks, fn):
        first = jnp.int32(0)
        for count in REGION_BLOCKS:
            present = (n_blocks & count) != 0
            pl.when(present)(functools.partial(fn, first, count))
            first = first + jnp.where(present, count, 0)

    def rows_of(first, count):
        return pl.ds(pl.multiple_of(first * ROW_BLOCK, ROW_BLOCK), count * ROW_BLOCK)

    def in_copy(item, dst_slot, first, count):
        src = pl.multiple_of((ib_ref[item] + first) * ROW_BLOCK, ROW_BLOCK)
        return pltpu.make_async_copy(xs_hbm.at[pl.ds(src, count * ROW_BLOCK)],
                                     xg_ref.at[dst_slot, rows_of(first, count)], gsem.at[dst_slot])

    def fetch_item(item, n_blocks, dst_slot):
        for_regions(n_blocks, lambda first, count: in_copy(item, dst_slot, first, count).start())

    @pl.when(j == 0)
    def _():
        @pl.when(i == 0)
        def _():
            fetch_item(0, nsub, 0)

        nxt = jnp.minimum(i + 1, n_items - 1)
        fetch_item(nxt, jnp.where(i + 1 < n_items, ins_ref[nxt], 0), 1 - slot)

        def arrive(first, count):
            in_copy(i, slot, first, count).wait()
            acc_ref[rows_of(first, count), :] = jnp.broadcast_to(
                bdn_ref[0], (count * ROW_BLOCK, acc_ref.shape[1]))

        for_regions(nsub, arrive)

    @pl.when(nsub > 0)
    def _():
        bup = bup_ref[0]
        perm = perm_ref[...]
        half = perm.shape[0] // 2

        def run_blocks(first, count):
            if first == 0:
                wup = wup_ref[0].astype(BF16)
                wdn = wdn_ref[0].astype(BF16)
                wupb_ref[...] = wup
                wdnb_ref[...] = wdn
            else:
                wup = wupb_ref[...]
                wdn = wdnb_ref[...]
            rows = pl.ds(first * ROW_BLOCK, count * ROW_BLOCK)
            xb = _unpack_pairs(xg_ref[slot, rows, :]).astype(BF16)
            gu = (jnp.dot(xb, wup, preferred_element_type=F32) + bup).astype(BF16)
            glu_parts, lin_parts = [], []
            for p in range(gu.shape[1] // perm.shape[0]):
                gp = jnp.dot(gu[:, p * perm.shape[0]:(p + 1) * perm.shape[0]], perm,
                             preferred_element_type=F32)
                glu_parts.append(gp[:, :half])
                lin_parts.append(gp[:, half:])
            x_glu = jnp.minimum(jnp.concatenate(glu_parts, axis=1), SWIGLU_LIMIT)
            x_lin = jnp.clip(jnp.concatenate(lin_parts, axis=1), -SWIGLU_LIMIT, SWIGLU_LIMIT)
            act = x_glu * jax.nn.sigmoid(SWIGLU_ALPHA * x_glu) * (x_lin + 1.0)
            acc_ref[rows, :] += jnp.dot(act.astype(BF16), wdn, preferred_element_type=F32)

        for n_blocks, plan in COMPUTE_PLANS.items():
            for first, count in plan:
                shared = [n for n, p in COMPUTE_PLANS.items() if (first, count) in p]
                if n_blocks == shared[0]:
                    cond = functools.reduce(jnp.logical_or, [nsub == n for n in shared])
                    pl.when(cond)(functools.partial(run_blocks, first, count))

    def out_copy(item, first, count):
        dst = pl.multiple_of((ib_ref[item] + first) * ROW_BLOCK, ROW_BLOCK)
        return pltpu.make_async_copy(yst_ref.at[rows_of(first, count)],
                                     ys_hbm.at[pl.ds(dst, count * ROW_BLOCK)], osem)

    @pl.when(j == n_ff_tiles - 1)
    def _():
        prev = jnp.maximum(i - 1, 0)
        for_regions(jnp.where(i > 0, ins_ref[prev], 0),
                    lambda first, count: out_copy(prev, first, count).wait())

        def leave(first, count):
            rows = rows_of(first, count)
            yst_ref[rows, :] = _pack_pairs(acc_ref[rows, :])
            out_copy(i, first, count).start()

        for_regions(nsub, leave)

        @pl.when(i == n_items - 1)
        def _():
            for_regions(nsub, lambda first, count: out_copy(i, first, count).wait())


def _experts(xs, item_e, item_b, item_n, n_active, w_up, b_up, w_down, b_down):
    n_rows, Dp = xs.shape
    D = 2 * Dp
    E, _, F2 = w_up.shape
    F = F2 // 2
    J = F // FF_TILE
    half = 128
    perm = np.zeros((2 * half, 2 * half), np.float32)
    perm[2 * np.arange(half), np.arange(half)] = 1.0
    perm[2 * np.arange(half) + 1, half + np.arange(half)] = 1.0

    def jj(i, j, ins):
        return jnp.where(ins[i] > 0, j, J - 1)

    grid_spec = pltpu.PrefetchScalarGridSpec(
        num_scalar_prefetch=3,
        grid=(n_active, J),
        in_specs=[pl.BlockSpec(memory_space=pl.ANY),
                  pl.BlockSpec((1, D, 2 * FF_TILE), lambda i, j, ie, ib, ins: (ie[i], 0, jj(i, j, ins))),
                  pl.BlockSpec((1, 1, 2 * FF_TILE), lambda i, j, ie, ib, ins: (ie[i], 0, jj(i, j, ins))),
                  pl.BlockSpec((1, FF_TILE, D), lambda i, j, ie, ib, ins: (ie[i], jj(i, j, ins), 0)),
                  pl.BlockSpec((1, 1, D), lambda i, j, ie, ib, ins: (ie[i], 0, 0)),
                  pl.BlockSpec((2 * half, 2 * half), lambda i, j, ie, ib, ins: (0, 0))],
        out_specs=pl.BlockSpec(memory_space=pl.ANY),
        scratch_shapes=[pltpu.VMEM((2, ITEM_ROWS, Dp), jnp.uint32),
                        pltpu.VMEM((ITEM_ROWS, D), F32),
                        pltpu.VMEM((ITEM_ROWS, Dp), jnp.uint32),
                        pltpu.VMEM((D, 2 * FF_TILE), BF16),
                        pltpu.VMEM((FF_TILE, D), BF16),
                        pltpu.SemaphoreType.DMA((2,)), pltpu.SemaphoreType.DMA(())],
    )
    return pl.pallas_call(
        functools.partial(_expert_kernel, n_ff_tiles=J),
        grid_spec=grid_spec,
        out_shape=jax.ShapeDtypeStruct((n_rows, Dp), jnp.uint32),
        compiler_params=_cparams(("arbitrary", "arbitrary")),
        name="expert_ffn",
    )(item_e, item_b, item_n,
      xs, w_up, b_up.reshape(E, 1, F2), w_down, b_down.reshape(E, 1, D), jnp.asarray(perm, BF16))


def _combine_kernel(*refs):
    yk_refs, (g4_ref, x1_ref, g_ref, o_ref) = refs[:TOP_K], refs[TOP_K:]
    y = x1_ref[...]
    g4 = g4_ref[...]
    for k in range(TOP_K):
        y = y + g4[:, k:k + 1] * _unpack_pairs(yk_refs[k][...])
    ms = jnp.mean(y * y, axis=-1, keepdims=True)
    o_ref[...] = y * lax.rsqrt(ms + NORM_EPS) * g_ref[...]


def _combine(yk, g4, x1, g_final, tc=256):
    T, D = x1.shape
    nt = T // tc
    slot_specs = [pl.BlockSpec((tc, D // 2), functools.partial(lambda i, k: (k * nt + i, 0), k=k))
                  for k in range(TOP_K)]
    return pl.pallas_call(
        _combine_kernel,
        grid=(nt,),
        in_specs=slot_specs + [pl.BlockSpec((tc, 128), lambda i: (i, 0)),
                               pl.BlockSpec((tc, D), lambda i: (i, 0)),
                               pl.BlockSpec((1, D), lambda i: (0, 0))],
        out_specs=pl.BlockSpec((tc, D), lambda i: (i, 0)),
        out_shape=jax.ShapeDtypeStruct((T, D), F32),
        compiler_params=_cparams(("parallel",)),
        name="combine_norm",
    )(*([yk] * TOP_K), g4, x1, g_final.reshape(1, D))


def _work_items(nblk, start_blk, n_items):
    E = nblk.shape[0]
    per_e = (nblk + ITEM_BLOCKS - 1) // ITEM_BLOCKS
    ends = jnp.cumsum(per_e)
    total = ends[-1]
    idx = jnp.arange(n_items, dtype=I32)
    e = jnp.minimum(jnp.searchsorted(ends, idx, side="right"), E - 1).astype(I32)
    local = idx - (ends[e] - per_e[e])
    active = idx < total
    last_e = e[jnp.maximum(total - 1, 0)]
    item_e = jnp.where(active, e, last_e).astype(I32)
    item_b = jnp.where(active, start_blk[e] + local * ITEM_BLOCKS, 0).astype(I32)
    item_n = jnp.where(active, jnp.clip(nblk[e] - local * ITEM_BLOCKS, 0, ITEM_BLOCKS), 0).astype(I32)
    return item_e, item_b, item_n, jnp.maximum(total, 1).astype(I32)


def kernel(x, g_mix, w_in, b_forget, g_v_ln, b_v_ln, w_spatial, b_spatial, w_branch_attn, w_branch_gmlp, w_out, g_ffn, w_router, b_router, w_expert_up, b_expert_up, w_expert_down, b_expert_down, g_final):
    B, S, D = x.shape
    T = B * S
    n_heads = b_forget.shape[0]
    attn_w = n_heads * HEAD_DIM
    gmlp_w = g_v_ln.shape[0]
    E = w_router.shape[1]
    off_f = 3 * attn_w
    off_z = off_f + n_heads
    off_g = off_z + 2 * gmlp_w

    x2 = x.reshape(T, D)
    wf = jnp.pad(w_in[:, off_f:off_z].astype(BF16), ((0, 0), (0, LANES - n_heads)))
    wqkv = w_in[:, :off_f]
    wzg = w_in[:, off_z:].astype(BF16)
    wa = w_branch_attn.astype(BF16)
    wb = w_branch_gmlp.astype(BF16)
    wo = w_out.astype(BF16)

    h = _rmsnorm(x2, g_mix, BF16)
    qkv = _project(h, wqkv, 3 * attn_w, attn_w, LOG2E / math.sqrt(HEAD_DIM), BF16)
    c_row = _forget_cumsum(h, wf, b_forget, B, S)
    attn = _attention(qkv, c_row, B, S, n_heads)
    sg = _gmlp(h, wzg, g_v_ln, b_v_ln, w_spatial, b_spatial)
    merged = _merge(attn, sg, h, wa, wb, wzg, off_g - off_z)
    x1, h2p, sel, gate = _out_router(merged, x2, wo, g_ffn, w_router, b_router)

    dest4, g4, nblk, start_blk = _routing(sel, gate)
    n_rows = T * TOP_K + E * ROW_BLOCK
    dest_slots = dest4[:, :TOP_K].T.reshape(TOP_K * T)
    max_blocks = n_rows // ROW_BLOCK
    n_items = E + -(-(max_blocks - E) // ITEM_BLOCKS) + 1
    item_e, item_b, item_n, n_active = _work_items(nblk[0], start_blk[0], n_items)
    xs = _sc_scatter_rows(h2p, dest_slots, n_rows, TOP_K)
    ys = _experts(xs, item_e, item_b, item_n, n_active,
                  w_expert_up, b_expert_up, w_expert_down, b_expert_down)
    yk = _sc_gather_rows(ys, dest_slots)
    out = _combine(yk, g4, x1, g_final)
    return out.reshape(B, S, D)
```

```python
import functools
import math

import jax
import jax.numpy as jnp
import numpy as np
from jax import lax
from jax.experimental import pallas as pl
from jax.experimental.pallas import tpu as pltpu
from jax.experimental.pallas import tpu_sc as plsc

F32 = jnp.float32
BF16 = jnp.bfloat16
I32 = jnp.int32

NORM_EPS = 1e-5
LANES = 128
HEAD_DIM = 128
CHUNK = 128
GROUP_DIM = 128
TOP_K = 4
SWIGLU_ALPHA = 1.702
SWIGLU_LIMIT = 7.0
LOG2E = math.log2(math.e)

VMEM_LIMIT_BYTES = 56 * 1024 * 1024

ROW_BLOCK = 256
ITEM_BLOCKS = 5
ITEM_ROWS = ITEM_BLOCKS * ROW_BLOCK
REGION_BLOCKS = (4, 2, 1)
assert sum(REGION_BLOCKS) >= ITEM_BLOCKS
COMPUTE_PLANS = {5: ((0, 5),), 4: ((0, 4),), 3: ((0, 2), (2, 1)), 2: ((0, 2),), 1: ((0, 1),)}
assert set(COMPUTE_PLANS) == set(range(1, ITEM_BLOCKS + 1))
FF_TILE = 256
DMA_UNROLL = 8


def _cparams(sem, **kw):
    return pltpu.CompilerParams(dimension_semantics=sem, vmem_limit_bytes=VMEM_LIMIT_BYTES, **kw)


def _pack_pairs(x):
    c = x.shape[1] // 2
    hi = lax.bitcast_convert_type(x[:, :c].astype(BF16).astype(F32), jnp.uint32)
    lo = lax.bitcast_convert_type(x[:, c:].astype(BF16).astype(F32), jnp.uint32)
    return hi | (lo >> 16)


def _unpack_pairs(w):
    hi = lax.bitcast_convert_type(w & jnp.uint32(0xFFFF0000), F32)
    lo = lax.bitcast_convert_type(w << 16, F32)
    return jnp.concatenate([hi, lo], axis=1)


def _rmsnorm_kernel(x_ref, g_ref, o_ref):
    x = x_ref[...]
    ms = jnp.mean(x * x, axis=-1, keepdims=True)
    o_ref[...] = (x * lax.rsqrt(ms + NORM_EPS) * g_ref[...]).astype(o_ref.dtype)


def _rmsnorm(x, g, out_dtype, tm=512):
    T, D = x.shape
    return pl.pallas_call(
        _rmsnorm_kernel,
        grid=(T // tm,),
        in_specs=[pl.BlockSpec((tm, D), lambda i: (i, 0)), pl.BlockSpec((1, D), lambda i: (0, 0))],
        out_specs=pl.BlockSpec((tm, D), lambda i: (i, 0)),
        out_shape=jax.ShapeDtypeStruct((T, D), out_dtype),
        compiler_params=_cparams(("parallel",)),
        name="rmsnorm",
    )(x, g.reshape(1, D))


def _proj_kernel(h_ref, w_ref, o_ref, wbf_ref, *, n_scaled, scale):
    j = pl.program_id(0)

    @pl.when(pl.program_id(1) == 0)
    def _():
        wbf_ref[...] = w_ref[...].astype(BF16)

    acc = jnp.dot(h_ref[...], wbf_ref[...], preferred_element_type=F32)
    o_ref[...] = (acc * jnp.where(j < n_scaled, scale, 1.0)).astype(o_ref.dtype)


def _project(h, w, n_cols, n_scaled_cols, scale, out_dtype, tm=1024, tn=512):
    T, D = h.shape
    return pl.pallas_call(
        functools.partial(_proj_kernel, n_scaled=n_scaled_cols // tn, scale=scale),
        grid=(n_cols // tn, T // tm),
        in_specs=[pl.BlockSpec((tm, D), lambda j, i: (i, 0)),
                  pl.BlockSpec((D, tn), lambda j, i: (0, j))],
        out_specs=pl.BlockSpec((tm, tn), lambda j, i: (i, j)),
        out_shape=jax.ShapeDtypeStruct((T, n_cols), out_dtype),
        scratch_shapes=[pltpu.VMEM((D, tn), BF16)],
        compiler_params=_cparams(("arbitrary", "arbitrary")),
        name="qkv_proj",
    )(h, w)


def _shift_cast_kernel(main_ref, next_ref, o_ref, *, shift):
    full = jnp.concatenate([main_ref[...], next_ref[...]], axis=1)
    o_ref[...] = full[:, shift:shift + o_ref.shape[1]].astype(o_ref.dtype)


def _shifted_columns_bf16(w, start, n_cols, tr=256, tc=1024):
    R = w.shape[0]
    base = start // LANES * LANES
    shift = start - base
    assert base % tc == 0 and n_cols % tc == 0 and R % tr == 0 and 0 < shift < LANES
    return pl.pallas_call(
        functools.partial(_shift_cast_kernel, shift=shift),
        grid=(R // tr, n_cols // tc),
        in_specs=[pl.BlockSpec((tr, tc), lambda i, c: (i, base // tc + c)),
                  pl.BlockSpec((tr, LANES), lambda i, c: (i, (base + (c + 1) * tc) // LANES))],
        out_specs=pl.BlockSpec((tr, tc), lambda i, c: (i, c)),
        out_shape=jax.ShapeDtypeStruct((R, n_cols), BF16),
        compiler_params=_cparams(("parallel", "parallel")),
        name="shift_cast_columns",
    )(w, w)


def _forget_kernel(h_ref, wf_ref, bf_ref, c_ref):
    H = bf_ref.shape[0]
    lane_w = lax.broadcasted_iota(I32, wf_ref.shape, 1)
    wf = jnp.where(lane_w < H, wf_ref[...], 0.0).astype(BF16)
    f = jnp.dot(h_ref[...], wf, preferred_element_type=F32)
    ft = f.T[:H]
    c = jax.nn.log_sigmoid(ft + bf_ref[...])
    S = c.shape[1]
    lane = lax.broadcasted_iota(I32, c.shape, 1)
    shift = 1
    while shift < S:
        c = c + jnp.where(lane >= shift, pltpu.roll(c, shift, axis=1), 0.0)
        shift *= 2
    c_ref[0] = c * LOG2E


def _forget_cumsum(h, w, f_off, b_forget, B, S):
    T, D = h.shape
    H = b_forget.shape[0]
    assert f_off % LANES == 0
    return pl.pallas_call(
        _forget_kernel,
        grid=(B,),
        in_specs=[pl.BlockSpec((S, D), lambda b: (b, 0)),
                  pl.BlockSpec((D, LANES), lambda b: (0, f_off // LANES)),
                  pl.BlockSpec((H, 1), lambda b: (0, 0))],
        out_specs=pl.BlockSpec((1, H, S), lambda b: (b, 0, 0)),
        out_shape=jax.ShapeDtypeStruct((B, H, S), F32),
        compiler_params=_cparams(("parallel",)),
        name="forget_cumsum",
    )(h, w, b_forget.reshape(H, 1))


def _attn_kernel(q_ref, k_ref, v_ref, crow_ref, o_ref, vaug_ref, m_ref, acc_ref, *, n_heads, tq):
    i = pl.program_id(1)

    @pl.when(i == 0)
    def _():
        ones = jnp.ones((v_ref.shape[0], HEAD_DIM), BF16)
        for h in range(n_heads):
            vaug_ref[h, :, :HEAD_DIM] = v_ref[:, h * HEAD_DIM:(h + 1) * HEAD_DIM]
            vaug_ref[h, :, HEAD_DIM:] = ones

    m_ref[...] = jnp.full(m_ref.shape, -jnp.inf, F32)
    acc_ref[...] = jnp.zeros(acc_ref.shape, F32)
    row = lax.broadcasted_iota(I32, (tq, tq), 0)
    col = lax.broadcasted_iota(I32, (tq, tq), 1)
    causal = col <= row

    def step(j, masked):
        keys = pl.ds(pl.multiple_of(j * tq, tq), tq)
        for h in range(n_heads):
            hs = slice(h * HEAD_DIM, (h + 1) * HEAD_DIM)
            s = lax.dot_general(q_ref[:, hs], k_ref[keys, hs], (((1,), (1,)), ((), ())),
                                preferred_element_type=F32) - crow_ref[0, h, j]
            if masked:
                s = jnp.where(causal, s, -jnp.inf)
            m_old = m_ref[h]
            m_new = jnp.maximum(m_old, jnp.max(s, axis=-1, keepdims=True))
            alpha = jnp.exp2(m_old - m_new)
            p = jnp.exp2(s - jnp.concatenate([m_new] * (tq // HEAD_DIM), axis=1))
            m_ref[h] = m_new
            pv = jnp.dot(p.astype(BF16), vaug_ref[h, keys, :], preferred_element_type=F32)
            acc_ref[h] = jnp.concatenate([alpha, alpha], axis=1) * acc_ref[h] + pv

    def body(j, _):
        step(j, False)
        return 0

    lax.fori_loop(0, i, body, 0)
    step(i, True)
    for h in range(n_heads):
        acc = acc_ref[h]
        o_ref[:, h * HEAD_DIM:(h + 1) * HEAD_DIM] = (acc[:, :HEAD_DIM] / acc[:, HEAD_DIM:]).astype(o_ref.dtype)


def _attention(qkv, c_row, B, S, n_heads, tq=256):
    T = qkv.shape[0]
    W = n_heads * HEAD_DIM
    nq = S // tq
    c_row5 = c_row.reshape(B, n_heads, nq, 1, tq)
    return pl.pallas_call(
        functools.partial(_attn_kernel, n_heads=n_heads, tq=tq),
        grid=(B, nq),
        in_specs=[pl.BlockSpec((tq, W), lambda b, i: (b * nq + i, 0)),
                  pl.BlockSpec((S, W), lambda b, i: (b, 1)),
                  pl.BlockSpec((S, W), lambda b, i: (b, 2)),
                  pl.BlockSpec((1, n_heads, nq, 1, tq), lambda b, i: (b, 0, 0, 0, 0))],
        out_specs=pl.BlockSpec((tq, W), lambda b, i: (b * nq + i, 0)),
        out_shape=jax.ShapeDtypeStruct((T, W), BF16),
        scratch_shapes=[pltpu.VMEM((n_heads, S, 2 * HEAD_DIM), BF16),
                        pltpu.VMEM((n_heads, tq, HEAD_DIM), F32),
                        pltpu.VMEM((n_heads, tq, 2 * HEAD_DIM), F32)],
        compiler_params=_cparams(("arbitrary", "arbitrary")),
        name="fox_attention",
    )(qkv, qkv, qkv, c_row5)


def _gmlp_kernel(h_ref, wz_ref, g_ref, b_ref, ws_ref, bst_ref, o_ref, *, n_groups):
    z = jnp.dot(h_ref[...], wz_ref[...], preferred_element_type=F32)
    z = 0.5 * z * (1.0 + lax.erf(z * (1.0 / math.sqrt(2.0))))
    W = z.shape[1] // 2
    u = z[:, :W]
    v = z[:, W:]
    mu = jnp.mean(v, axis=-1, keepdims=True)
    var = jnp.mean(jnp.square(v - mu), axis=-1, keepdims=True)
    vn = (v - mu) * lax.rsqrt(var + NORM_EPS) * g_ref[...] + b_ref[...]
    row = lax.broadcasted_iota(I32, (CHUNK, CHUNK), 0)
    col = lax.broadcasted_iota(I32, (CHUNK, CHUNK), 1)
    tril = col <= row
    tg = z.shape[0]
    for g in range(n_groups):
        gs = slice(g * GROUP_DIM, (g + 1) * GROUP_DIM)
        wg = jnp.where(tril, ws_ref[g], 0.0).astype(BF16)
        bias = bst_ref[:, g:g + 1]
        for c in range(tg // CHUNK):
            cs = slice(c * CHUNK, (c + 1) * CHUNK)
            mixed = jnp.dot(wg, vn[cs, gs].astype(BF16), preferred_element_type=F32) + bias
            o_ref[cs, gs] = (u[cs, gs] * mixed).astype(o_ref.dtype)


def _gmlp(h, wz, g_v_ln, b_v_ln, w_spatial, b_spatial, tg=512):
    T, D = h.shape
    W = g_v_ln.shape[0]
    W2 = 2 * W
    G = w_spatial.shape[0]
    return pl.pallas_call(
        functools.partial(_gmlp_kernel, n_groups=G),
        grid=(T // tg,),
        in_specs=[pl.BlockSpec((tg, D), lambda i: (i, 0)),
                  pl.BlockSpec((D, W2), lambda i: (0, 0)),
                  pl.BlockSpec((1, W), lambda i: (0, 0)),
                  pl.BlockSpec((1, W), lambda i: (0, 0)),
                  pl.BlockSpec((G, CHUNK, CHUNK), lambda i: (0, 0, 0)),
                  pl.BlockSpec((CHUNK, G), lambda i: (0, 0))],
        out_specs=pl.BlockSpec((tg, W), lambda i: (i, 0)),
        out_shape=jax.ShapeDtypeStruct((T, W), BF16),
        compiler_params=_cparams(("parallel",)),
        name="gmlp",
    )(h, wz, g_v_ln.reshape(1, W), b_v_ln.reshape(1, W), w_spatial, b_spatial.T)


def _merge_kernel(attn_ref, sg_ref, h_ref, wa_ref, wb_ref, wga_ref, wgb_ref, o_ref):
    h = h_ref[...]
    a = jnp.dot(attn_ref[...], wa_ref[...], preferred_element_type=F32)
    ga = jnp.dot(h, wga_ref[...], preferred_element_type=F32)
    m = jax.nn.sigmoid(ga) * a
    b = jnp.dot(sg_ref[...], wb_ref[...], preferred_element_type=F32)
    gb = jnp.dot(h, wgb_ref[...], preferred_element_type=F32)
    o_ref[...] = (m + jax.nn.sigmoid(gb) * b).astype(o_ref.dtype)


def _merge(attn, sg, h, wa, wb, wg, g_off, tm=512, tn=512):
    T, D = h.shape
    Wa = attn.shape[1]
    Wb = sg.shape[1]
    nt = D // tn
    g0 = g_off // tn
    return pl.pallas_call(
        _merge_kernel,
        grid=(nt, T // tm),
        in_specs=[pl.BlockSpec((tm, Wa), lambda j, i: (i, 0)),
                  pl.BlockSpec((tm, Wb), lambda j, i: (i, 0)),
                  pl.BlockSpec((tm, D), lambda j, i: (i, 0)),
                  pl.BlockSpec((Wa, tn), lambda j, i: (0, j)),
                  pl.BlockSpec((Wb, tn), lambda j, i: (0, j)),
                  pl.BlockSpec((D, tn), lambda j, i: (0, g0 + j)),
                  pl.BlockSpec((D, tn), lambda j, i: (0, g0 + nt + j))],
        out_specs=pl.BlockSpec((tm, tn), lambda j, i: (i, j)),
        out_shape=jax.ShapeDtypeStruct((T, D), BF16),
        compiler_params=_cparams(("arbitrary", "arbitrary")),
        name="gated_merge",
    )(attn, sg, h, wa, wb, wg, wg)


def _out_router_kernel(m_ref, x_ref, wo_ref, g_ref, wr_ref, br_ref,
                       x1_ref, h2_ref, sel_ref, gate_ref, *, n_parts):
    E = br_ref.shape[1]
    part = m_ref.shape[0] // n_parts
    for p in range(n_parts):
        rows = pl.ds(p * part, part)
        x1 = x_ref[rows, :] + jnp.dot(m_ref[rows, :], wo_ref[...], preferred_element_type=F32)
        x1_ref[rows, :] = x1
        ms = jnp.mean(x1 * x1, axis=-1, keepdims=True)
        h2 = x1 * lax.rsqrt(ms + NORM_EPS) * g_ref[...]
        h2_ref[rows, :] = _pack_pairs(h2)
        h2_hi = h2.astype(BF16)
        h2_lo = (h2 - h2_hi.astype(F32)).astype(BF16)
        pa = jnp.dot(h2_hi, wr_ref[...], preferred_element_type=F32)
        pb = jnp.dot(h2_lo, wr_ref[:, :E], preferred_element_type=F32)
        logits = pa[:, :E] + (pa[:, E:] + pb) + br_ref[...]
        lane = lax.broadcasted_iota(I32, logits.shape, 1)
        work = logits
        sel = jnp.zeros(logits.shape, F32)
        num = jnp.zeros(logits.shape, F32)
        denom = jnp.zeros((part, 1), F32)
        m0 = None
        for _ in range(TOP_K):
            m = jnp.max(work, axis=-1, keepdims=True)
            idx = jnp.min(jnp.where(work == m, lane, E), axis=-1, keepdims=True)
            onehot = lane == idx
            if m0 is None:
                m0 = m
            e = jnp.exp(m - m0)
            sel = jnp.where(onehot, 1.0, sel)
            num = jnp.where(onehot, e, num)
            denom = denom + e
            work = jnp.where(onehot, -jnp.inf, work)
        sel_ref[rows, :] = sel
        gate_ref[rows, :] = num / denom


def _out_router(merged, x, wo, g_ffn, w_router, b_router, to=512, n_parts=2):
    T, D = x.shape
    E = w_router.shape[1]
    w_hi = w_router.astype(BF16)
    w_lo = (w_router - w_hi.astype(F32)).astype(BF16)
    w_router = jnp.concatenate([w_hi, w_lo], axis=1)
    row = lambda i: (i, 0)
    fixed = lambda i: (0, 0)
    return pl.pallas_call(
        functools.partial(_out_router_kernel, n_parts=n_parts),
        grid=(T // to,),
        in_specs=[pl.BlockSpec((to, D), row), pl.BlockSpec((to, D), row),
                  pl.BlockSpec((D, D), fixed), pl.BlockSpec((1, D), fixed),
                  pl.BlockSpec((D, 2 * E), fixed), pl.BlockSpec((1, E), fixed)],
        out_specs=[pl.BlockSpec((to, D), row), pl.BlockSpec((to, D // 2), row),
                   pl.BlockSpec((to, E), row), pl.BlockSpec((to, E), row)],
        out_shape=[jax.ShapeDtypeStruct((T, D), F32), jax.ShapeDtypeStruct((T, D // 2), jnp.uint32),
                   jax.ShapeDtypeStruct((T, E), F32), jax.ShapeDtypeStruct((T, E), F32)],
        compiler_params=_cparams(("parallel",)),
        name="out_router",
    )(merged, x, wo, g_ffn.reshape(1, D), w_router, b_router.reshape(1, E))


def _routing_kernel(sel_ref, gate_ref, dest_ref, g4_ref, nblk_ref, start_ref, rank_ref, *, tile):
    T, E = sel_ref.shape
    nt = T // tile
    r = lax.broadcasted_iota(I32, (tile, tile), 0)
    c = lax.broadcasted_iota(I32, (tile, tile), 1)
    strict_lower = (c < r).astype(BF16)
    er = lax.broadcasted_iota(I32, (E, E), 0)
    ec = lax.broadcasted_iota(I32, (E, E), 1)
    strict_upper = (er < ec).astype(BF16)

    def pass1(t, carry):
        rows = pl.ds(pl.multiple_of(t * tile, tile), tile)
        a = sel_ref[rows, :]
        rank_ref[rows, :] = jnp.dot(strict_lower, a.astype(BF16), preferred_element_type=F32) + carry
        return carry + jnp.sum(a, axis=0, keepdims=True)

    counts = lax.fori_loop(0, nt, pass1, jnp.zeros((1, E), F32))
    nblk = jnp.floor((counts + (ROW_BLOCK - 1)) * (1.0 / ROW_BLOCK))
    start_blk = jnp.dot(nblk.astype(BF16), strict_upper, preferred_element_type=F32)
    nblk_ref[...] = nblk.astype(I32)
    start_ref[...] = start_blk.astype(I32)
    start_row = start_blk * float(ROW_BLOCK)
    lane = lax.broadcasted_iota(I32, (tile, 128), 1)

    def pass2(t, _):
        rows = pl.ds(pl.multiple_of(t * tile, tile), tile)
        a = sel_ref[rows, :]
        g = gate_ref[rows, :]
        dest_e = rank_ref[rows, :] + start_row
        slot = jnp.dot(a.astype(BF16), strict_upper, preferred_element_type=F32)
        d4 = jnp.zeros((tile, 128), F32)
        g4 = jnp.zeros((tile, 128), F32)
        for s in range(TOP_K):
            pick = (a > 0.5) & (slot == float(s))
            d4 = jnp.where(lane == s, jnp.sum(jnp.where(pick, dest_e, 0.0), axis=-1, keepdims=True), d4)
            g4 = jnp.where(lane == s, jnp.sum(jnp.where(pick, g, 0.0), axis=-1, keepdims=True), g4)
        dest_ref[rows, :] = d4.astype(I32)
        g4_ref[rows, :] = g4
        return 0

    lax.fori_loop(0, nt, pass2, 0)


def _routing(sel, gate, tile=256):
    T, E = sel.shape
    return pl.pallas_call(
        functools.partial(_routing_kernel, tile=tile),
        out_shape=[jax.ShapeDtypeStruct((T, 128), I32), jax.ShapeDtypeStruct((T, 128), F32),
                   jax.ShapeDtypeStruct((1, E), I32), jax.ShapeDtypeStruct((1, E), I32)],
        scratch_shapes=[pltpu.VMEM((T, E), F32)],
        compiler_params=pltpu.CompilerParams(vmem_limit_bytes=VMEM_LIMIT_BYTES),
        name="routing_ranks",
    )(sel, gate)


SC_CORES = 2
SC_SUBCORES = 16
SC_CHUNK = 64


def _sc_gather_rows(table, idx):
    n = idx.shape[0]
    W = table.shape[1]
    n_workers = SC_CORES * SC_SUBCORES
    per_worker = n // n_workers
    assert per_worker * n_workers == n and per_worker % SC_CHUNK == 0
    mesh = plsc.VectorSubcoreMesh(core_axis_name="c", subcore_axis_name="s",
                                  num_cores=SC_CORES, num_subcores=SC_SUBCORES)

    @functools.partial(
        pl.kernel, mesh=mesh,
        out_type=jax.ShapeDtypeStruct((n, W), table.dtype),
        scratch_types=[pltpu.VMEM((SC_CHUNK,), I32), pltpu.VMEM((SC_CHUNK, W), table.dtype),
                       pltpu.SemaphoreType.DMA],
        name="sc_gather_rows",
    )
    def gather(table_hbm, idx_hbm, out_hbm, idx_v, rows_v, sem):
        wid = lax.axis_index("s") * SC_CORES + lax.axis_index("c")
        base = wid * per_worker

        @pl.loop(0, per_worker // SC_CHUNK)
        def _(c):
            off = pl.multiple_of(base + c * SC_CHUNK, 8)
            pltpu.sync_copy(idx_hbm.at[pl.ds(off, SC_CHUNK)], idx_v)
            pltpu.async_copy(table_hbm.at[idx_v], rows_v, sem).wait()
            pltpu.sync_copy(rows_v, out_hbm.at[pl.ds(off, SC_CHUNK)])

    return gather(table, idx)


def _sc_scatter_rows(rows, idx, n_out, n_slots):
    T, W = rows.shape
    n_workers = SC_CORES * SC_SUBCORES
    per_worker = T // n_workers
    assert per_worker * n_workers == T and per_worker % SC_CHUNK == 0
    mesh = plsc.VectorSubcoreMesh(core_axis_name="c", subcore_axis_name="s",
                                  num_cores=SC_CORES, num_subcores=SC_SUBCORES)

    @functools.partial(
        pl.kernel, mesh=mesh,
        out_type=jax.ShapeDtypeStruct((n_out, W), rows.dtype),
        scratch_types=[pltpu.VMEM((SC_CHUNK,), I32), pltpu.VMEM((SC_CHUNK, W), rows.dtype)],
        name="sc_scatter_rows",
    )
    def scatter(rows_hbm, idx_hbm, out_hbm, idx_v, rows_v):
        wid = lax.axis_index("s") * SC_CORES + lax.axis_index("c")
        base = wid * per_worker

        @pl.loop(0, per_worker // SC_CHUNK)
        def _(c):
            off = pl.multiple_of(base + c * SC_CHUNK, 8)
            pltpu.sync_copy(rows_hbm.at[pl.ds(off, SC_CHUNK)], rows_v)
            for k in range(n_slots):
                pltpu.sync_copy(idx_hbm.at[pl.ds(pl.multiple_of(k * T + off, 8), SC_CHUNK)], idx_v)
                pltpu.sync_copy(rows_v, out_hbm.at[idx_v])

    return scatter(rows, idx)


def _expert_kernel(ie_ref, ib_ref, ins_ref,
                   xs_hbm, wup_ref, bup_ref, wdn_ref, bdn_ref, perm_ref, ys_hbm,
                   xg_ref, acc_ref, yst_ref, wupb_ref, wdnb_ref, gsem, osem, *, n_ff_tiles):
    i = pl.program_id(0)
    j = pl.program_id(1)
    n_items = pl.num_programs(0)
    nsub = ins_ref[i]
    slot = i % 2

    def for_regions(n_blocks, fn):
        first = jnp.int32(0)
        for count in REGION_BLOCKS:
            present = (n_blocks & count) != 0
            pl.when(present)(functools.partial(fn, first, count))
            first = first + jnp.where(present, count, 0)

    def rows_of(first, count):
        return pl.ds(pl.multiple_of(first * ROW_BLOCK, ROW_BLOCK), count * ROW_BLOCK)

    def in_copy(item, dst_slot, first, count):
        src = pl.multiple_of((ib_ref[item] + first) * ROW_BLOCK, ROW_BLOCK)
        return pltpu.make_async_copy(xs_hbm.at[pl.ds(src, count * ROW_BLOCK)],
                                     xg_ref.at[dst_slot, rows_of(first, count)], gsem.at[dst_slot])

    def fetch_item(item, n_blocks, dst_slot):
        for_regions(n_blocks, lambda first, count: in_copy(item, dst_slot, first, count).start())

    @pl.when(j == 0)
    def _():
        @pl.when(i == 0)
        def _():
            fetch_item(0, nsub, 0)

        nxt = jnp.minimum(i + 1, n_items - 1)
        fetch_item(nxt, jnp.where(i + 1 < n_items, ins_ref[nxt], 0), 1 - slot)

        def arrive(first, count):
            in_copy(i, slot, first, count).wait()
            acc_ref[rows_of(first, count), :] = jnp.broadcast_to(
                bdn_ref[0], (count * ROW_BLOCK, acc_ref.shape[1]))

        for_regions(nsub, arrive)

    @pl.when(nsub > 0)
    def _():
        bup = bup_ref[0]
        perm = perm_ref[...]
        half = perm.shape[0] // 2

        def run_blocks(first, count):
            if first == 0:
                wup = wup_ref[0].astype(BF16)
                wdn = wdn_ref[0].astype(BF16)
                wupb_ref[...] = wup
                wdnb_ref[...] = wdn
            else:
                wup = wupb_ref[...]
                wdn = wdnb_ref[...]
            rows = pl.ds(first * ROW_BLOCK, count * ROW_BLOCK)
            xb = _unpack_pairs(xg_ref[slot, rows, :]).astype(BF16)
            gu = (jnp.dot(xb, wup, preferred_element_type=F32) + bup).astype(BF16)
            glu_parts, lin_parts = [], []
            for p in range(gu.shape[1] // perm.shape[0]):
                gp = jnp.dot(gu[:, p * perm.shape[0]:(p + 1) * perm.shape[0]], perm,
                             preferred_element_type=F32)
                glu_parts.append(gp[:, :half])
                lin_parts.append(gp[:, half:])
            x_glu = jnp.minimum(jnp.concatenate(glu_parts, axis=1), SWIGLU_LIMIT)
            x_lin = jnp.clip(jnp.concatenate(lin_parts, axis=1), -SWIGLU_LIMIT, SWIGLU_LIMIT)
            act = x_glu * jax.nn.sigmoid(SWIGLU_ALPHA * x_glu) * (x_lin + 1.0)
            acc_ref[rows, :] += jnp.dot(act.astype(BF16), wdn, preferred_element_type=F32)

        for n_blocks, plan in COMPUTE_PLANS.items():
            for first, count in plan:
                shared = [n for n, p in COMPUTE_PLANS.items() if (first, count) in p]
                if n_blocks == shared[0]:
                    cond = functools.reduce(jnp.logical_or, [nsub == n for n in shared])
                    pl.when(cond)(functools.partial(run_blocks, first, count))

    def out_copy(item, first, count):
        dst = pl.multiple_of((ib_ref[item] + first) * ROW_BLOCK, ROW_BLOCK)
        return pltpu.make_async_copy(yst_ref.at[rows_of(first, count)],
                                     ys_hbm.at[pl.ds(dst, count * ROW_BLOCK)], osem)

    @pl.when(j == n_ff_tiles - 1)
    def _():
        prev = jnp.maximum(i - 1, 0)
        for_regions(jnp.where(i > 0, ins_ref[prev], 0),
                    lambda first, count: out_copy(prev, first, count).wait())

        def leave(first, count):
            rows = rows_of(first, count)
            yst_ref[rows, :] = _pack_pairs(acc_ref[rows, :])
            out_copy(i, first, count).start()

        for_regions(nsub, leave)

        @pl.when(i == n_items - 1)
        def _():
            for_regions(nsub, lambda first, count: out_copy(i, first, count).wait())


def _experts(xs, item_e, item_b, item_n, n_active, w_up, b_up, w_down, b_down):
    n_rows, Dp = xs.shape
    D = 2 * Dp
    E, _, F2 = w_up.shape
    F = F2 // 2
    J = F // FF_TILE
    half = 128
    perm = np.zeros((2 * half, 2 * half), np.float32)
    perm[2 * np.arange(half), np.arange(half)] = 1.0
    perm[2 * np.arange(half) + 1, half + np.arange(half)] = 1.0

    def jj(i, j, ins):
        return jnp.where(ins[i] > 0, j, J - 1)

    grid_spec = pltpu.PrefetchScalarGridSpec(
        num_scalar_prefetch=3,
        grid=(n_active, J),
        in_specs=[pl.BlockSpec(memory_space=pl.ANY),
                  pl.BlockSpec((1, D, 2 * FF_TILE), lambda i, j, ie, ib, ins: (ie[i], 0, jj(i, j, ins))),
                  pl.BlockSpec((1, 1, 2 * FF_TILE), lambda i, j, ie, ib, ins: (ie[i], 0, jj(i, j, ins))),
                  pl.BlockSpec((1, FF_TILE, D), lambda i, j, ie, ib, ins: (ie[i], jj(i, j, ins), 0)),
                  pl.BlockSpec((1, 1, D), lambda i, j, ie, ib, ins: (ie[i], 0, 0)),
                  pl.BlockSpec((2 * half, 2 * half), lambda i, j, ie, ib, ins: (0, 0))],
        out_specs=pl.BlockSpec(memory_space=pl.ANY),
        scratch_shapes=[pltpu.VMEM((2, ITEM_ROWS, Dp), jnp.uint32),
                        pltpu.VMEM((ITEM_ROWS, D), F32),
                        pltpu.VMEM((ITEM_ROWS, Dp), jnp.uint32),
                        pltpu.VMEM((D, 2 * FF_TILE), BF16),
                        pltpu.VMEM((FF_TILE, D), BF16),
                        pltpu.SemaphoreType.DMA((2,)), pltpu.SemaphoreType.DMA(())],
    )
    return pl.pallas_call(
        functools.partial(_expert_kernel, n_ff_tiles=J),
        grid_spec=grid_spec,
        out_shape=jax.ShapeDtypeStruct((n_rows, Dp), jnp.uint32),
        compiler_params=_cparams(("arbitrary", "arbitrary")),
        name="expert_ffn",
    )(item_e, item_b, item_n,
      xs, w_up, b_up.reshape(E, 1, F2), w_down, b_down.reshape(E, 1, D), jnp.asarray(perm, BF16))


def _combine_kernel(*refs):
    yk_refs, (g4_ref, x1_ref, g_ref, o_ref) = refs[:TOP_K], refs[TOP_K:]
    y = x1_ref[...]
    g4 = g4_ref[...]
    for k in range(TOP_K):
        y = y + g4[:, k:k + 1] * _unpack_pairs(yk_refs[k][...])
    ms = jnp.mean(y * y, axis=-1, keepdims=True)
    o_ref[...] = y * lax.rsqrt(ms + NORM_EPS) * g_ref[...]


def _combine(yk, g4, x1, g_final, tc=256):
    T, D = x1.shape
    nt = T // tc
    slot_specs = [pl.BlockSpec((tc, D // 2), functools.partial(lambda i, k: (k * nt + i, 0), k=k))
                  for k in range(TOP_K)]
    return pl.pallas_call(
        _combine_kernel,
        grid=(nt,),
        in_specs=slot_specs + [pl.BlockSpec((tc, 128), lambda i: (i, 0)),
                               pl.BlockSpec((tc, D), lambda i: (i, 0)),
                               pl.BlockSpec((1, D), lambda i: (0, 0))],
        out_specs=pl.BlockSpec((tc, D), lambda i: (i, 0)),
        out_shape=jax.ShapeDtypeStruct((T, D), F32),
        compiler_params=_cparams(("parallel",)),
        name="combine_norm",
    )(*([yk] * TOP_K), g4, x1, g_final.reshape(1, D))


def _work_items(nblk, start_blk, n_items):
    E = nblk.shape[0]
    per_e = (nblk + ITEM_BLOCKS - 1) // ITEM_BLOCKS
    ends = jnp.cumsum(per_e)
    total = ends[-1]
    idx = jnp.arange(n_items, dtype=I32)
    e = jnp.minimum(jnp.searchsorted(ends, idx, side="right"), E - 1).astype(I32)
    local = idx - (ends[e] - per_e[e])
    active = idx < total
    last_e = e[jnp.maximum(total - 1, 0)]
    item_e = jnp.where(active, e, last_e).astype(I32)
    item_b = jnp.where(active, start_blk[e] + local * ITEM_BLOCKS, 0).astype(I32)
    item_n = jnp.where(active, jnp.clip(nblk[e] - local * ITEM_BLOCKS, 0, ITEM_BLOCKS), 0).astype(I32)
    return item_e, item_b, item_n, jnp.maximum(total, 1).astype(I32)


def kernel(x, g_mix, w_in, b_forget, g_v_ln, b_v_ln, w_spatial, b_spatial, w_branch_attn, w_branch_gmlp, w_out, g_ffn, w_router, b_router, w_expert_up, b_expert_up, w_expert_down, b_expert_down, g_final):
    B, S, D = x.shape
    T = B * S
    n_heads = b_forget.shape[0]
    attn_w = n_heads * HEAD_DIM
    gmlp_w = g_v_ln.shape[0]
    E = w_router.shape[1]
    off_f = 3 * attn_w
    off_z = off_f + n_heads
    off_g = off_z + 2 * gmlp_w

    x2 = x.reshape(T, D)
    wzg = _shifted_columns_bf16(w_in, off_z, w_in.shape[1] - off_z)
    wa = w_branch_attn.astype(BF16)
    wb = w_branch_gmlp.astype(BF16)
    wo = w_out.astype(BF16)

    h = _rmsnorm(x2, g_mix, BF16)
    qkv = _project(h, w_in, 3 * attn_w, attn_w, LOG2E / math.sqrt(HEAD_DIM), BF16)
    c_row = _forget_cumsum(h, w_in, off_f, b_forget, B, S)
    attn = _attention(qkv, c_row, B, S, n_heads)
    sg = _gmlp(h, wzg, g_v_ln, b_v_ln, w_spatial, b_spatial)
    merged = _merge(attn, sg, h, wa, wb, wzg, off_g - off_z)
    x1, h2p, sel, gate = _out_router(merged, x2, wo, g_ffn, w_router, b_router)

    dest4, g4, nblk, start_blk = _routing(sel, gate)
    n_rows = T * TOP_K + E * ROW_BLOCK
    dest_slots = dest4[:, :TOP_K].T.reshape(TOP_K * T)
    max_blocks = n_rows // ROW_BLOCK
    n_items = E + -(-(max_blocks - E) // ITEM_BLOCKS) + 1
    item_e, item_b, item_n, n_active = _work_items(nblk[0], start_blk[0], n_items)
    xs = _sc_scatter_rows(h2p, dest_slots, n_rows, TOP_K)
    ys = _experts(xs, item_e, item_b, item_n, n_active,
                  w_expert_up, b_expert_up, w_expert_down, b_expert_down)
    yk = _sc_gather_rows(ys, dest_slots)
    out = _combine(yk, g4, x1, g_final)
    return out.reshape(B, S, D)
```

```python
import functools
import math

import jax
import jax.numpy as jnp
import numpy as np
from jax import lax
from jax.experimental import pallas as pl
from jax.experimental.pallas import tpu as pltpu
from jax.experimental.pallas import tpu_sc as plsc

F32 = jnp.float32
BF16 = jnp.bfloat16
I32 = jnp.int32

NORM_EPS = 1e-5
LANES = 128
SUBLANES = 8
NT_DIMS = (((1,), (1,)), ((), ()))
HEAD_DIM = 128
CHUNK = 128
GROUP_DIM = 128
TOP_K = 4
SWIGLU_ALPHA = 1.702
SWIGLU_LIMIT = 7.0
LOG2E = math.log2(math.e)

VMEM_LIMIT_BYTES = 56 * 1024 * 1024

ROW_BLOCK = 256
ITEM_BLOCKS = 5
ITEM_ROWS = ITEM_BLOCKS * ROW_BLOCK
REGION_BLOCKS = (4, 2, 1)
assert sum(REGION_BLOCKS) >= ITEM_BLOCKS
COMPUTE_PLANS = {5: ((0, 5),), 4: ((0, 4),), 3: ((0, 2), (2, 1)), 2: ((0, 2),), 1: ((0, 1),)}
assert set(COMPUTE_PLANS) == set(range(1, ITEM_BLOCKS + 1))
FF_TILE = 256
DMA_UNROLL = 8


def _cparams(sem, **kw):
    return pltpu.CompilerParams(dimension_semantics=sem, vmem_limit_bytes=VMEM_LIMIT_BYTES, **kw)


def _pack_pairs(x):
    c = x.shape[1] // 2
    hi = lax.bitcast_convert_type(x[:, :c].astype(BF16).astype(F32), jnp.uint32)
    lo = lax.bitcast_convert_type(x[:, c:].astype(BF16).astype(F32), jnp.uint32)
    return hi | (lo >> 16)


def _unpack_pairs(w):
    hi = lax.bitcast_convert_type(w & jnp.uint32(0xFFFF0000), F32)
    lo = lax.bitcast_convert_type(w << 16, F32)
    return jnp.concatenate([hi, lo], axis=1)


def _rmsnorm_kernel(x_ref, g_ref, o_ref):
    x = x_ref[...]
    ms = jnp.mean(x * x, axis=-1, keepdims=True)
    o_ref[...] = (x * lax.rsqrt(ms + NORM_EPS) * g_ref[...]).astype(o_ref.dtype)


def _rmsnorm(x, g, out_dtype, tm=512):
    T, D = x.shape
    return pl.pallas_call(
        _rmsnorm_kernel,
        grid=(T // tm,),
        in_specs=[pl.BlockSpec((tm, D), lambda i: (i, 0)), pl.BlockSpec((1, D), lambda i: (0, 0))],
        out_specs=pl.BlockSpec((tm, D), lambda i: (i, 0)),
        out_shape=jax.ShapeDtypeStruct((T, D), out_dtype),
        compiler_params=_cparams(("parallel",)),
        name="rmsnorm",
    )(x, g.reshape(1, D))


def _proj_kernel(h_ref, w_ref, o_ref, wbf_ref, *, n_scaled, scale):
    j = pl.program_id(0)

    @pl.when(pl.program_id(1) == 0)
    def _():
        wbf_ref[...] = w_ref[...].astype(BF16)

    acc = lax.dot_general(h_ref[...], wbf_ref[...], NT_DIMS, preferred_element_type=F32)
    o_ref[...] = (acc * jnp.where(j < n_scaled, scale, 1.0)).astype(o_ref.dtype)


def _project(h, wt, n_cols, n_scaled_cols, scale, out_dtype, tm=1024, tn=512):
    T, D = h.shape
    return pl.pallas_call(
        functools.partial(_proj_kernel, n_scaled=n_scaled_cols // tn, scale=scale),
        grid=(n_cols // tn, T // tm),
        in_specs=[pl.BlockSpec((tm, D), lambda j, i: (i, 0)),
                  pl.BlockSpec((tn, D), lambda j, i: (j, 0))],
        out_specs=pl.BlockSpec((tm, tn), lambda j, i: (i, j)),
        out_shape=jax.ShapeDtypeStruct((T, n_cols), out_dtype),
        scratch_shapes=[pltpu.VMEM((tn, D), BF16)],
        compiler_params=_cparams(("arbitrary", "arbitrary")),
        name="qkv_proj",
    )(h, wt)


def _cast_kernel(w_ref, o_ref):
    o_ref[...] = w_ref[...].astype(o_ref.dtype)


def _rows_bf16(wt, start, n_rows, tr=512):
    D = wt.shape[1]
    assert start % SUBLANES == 0 and n_rows % tr == 0
    return pl.pallas_call(
        _cast_kernel,
        grid=(n_rows // tr,),
        in_specs=[pl.BlockSpec((pl.Element(tr), pl.Element(D)),
                               lambda i: (pl.multiple_of(start + i * tr, SUBLANES), 0))],
        out_specs=pl.BlockSpec((tr, D), lambda i: (i, 0)),
        out_shape=jax.ShapeDtypeStruct((n_rows, D), BF16),
        compiler_params=_cparams(("parallel",)),
        name="rows_bf16",
    )(wt)


def _forget_kernel(h_ref, wft_ref, bf_ref, c_ref):
    ft = lax.dot_general(wft_ref[...].astype(BF16), h_ref[...], NT_DIMS,
                         preferred_element_type=F32)
    c = jax.nn.log_sigmoid(ft + bf_ref[...])
    S = c.shape[1]
    lane = lax.broadcasted_iota(I32, c.shape, 1)
    shift = 1
    while shift < S:
        c = c + jnp.where(lane >= shift, pltpu.roll(c, shift, axis=1), 0.0)
        shift *= 2
    c_ref[0] = c * LOG2E


def _forget_cumsum(h, wt, f_off, b_forget, B, S):
    T, D = h.shape
    H = b_forget.shape[0]
    assert f_off % H == 0 and H % SUBLANES == 0
    return pl.pallas_call(
        _forget_kernel,
        grid=(B,),
        in_specs=[pl.BlockSpec((S, D), lambda b: (b, 0)),
                  pl.BlockSpec((H, D), lambda b: (f_off // H, 0)),
                  pl.BlockSpec((H, 1), lambda b: (0, 0))],
        out_specs=pl.BlockSpec((1, H, S), lambda b: (b, 0, 0)),
        out_shape=jax.ShapeDtypeStruct((B, H, S), F32),
        compiler_params=_cparams(("parallel",)),
        name="forget_cumsum",
    )(h, wt, b_forget.reshape(H, 1))


def _attn_kernel(q_ref, k_ref, v_ref, crow_ref, o_ref, vaug_ref, m_ref, acc_ref, *, n_heads, tq):
    i = pl.program_id(1)

    @pl.when(i == 0)
    def _():
        ones = jnp.ones((v_ref.shape[0], HEAD_DIM), BF16)
        for h in range(n_heads):
            vaug_ref[h, :, :HEAD_DIM] = v_ref[:, h * HEAD_DIM:(h + 1) * HEAD_DIM]
            vaug_ref[h, :, HEAD_DIM:] = ones

    m_ref[...] = jnp.full(m_ref.shape, -jnp.inf, F32)
    acc_ref[...] = jnp.zeros(acc_ref.shape, F32)
    row = lax.broadcasted_iota(I32, (tq, tq), 0)
    col = lax.broadcasted_iota(I32, (tq, tq), 1)
    causal = col <= row

    def step(j, masked):
        keys = pl.ds(pl.multiple_of(j * tq, tq), tq)
        for h in range(n_heads):
            hs = slice(h * HEAD_DIM, (h + 1) * HEAD_DIM)
            s = lax.dot_general(q_ref[:, hs], k_ref[keys, hs], (((1,), (1,)), ((), ())),
                                preferred_element_type=F32) - crow_ref[0, h, j]
            if masked:
                s = jnp.where(causal, s, -jnp.inf)
            m_old = m_ref[h]
            m_new = jnp.maximum(m_old, jnp.max(s, axis=-1, keepdims=True))
            alpha = jnp.exp2(m_old - m_new)
            p = jnp.exp2(s - jnp.concatenate([m_new] * (tq // HEAD_DIM), axis=1))
            m_ref[h] = m_new
            pv = jnp.dot(p.astype(BF16), vaug_ref[h, keys, :], preferred_element_type=F32)
            acc_ref[h] = jnp.concatenate([alpha, alpha], axis=1) * acc_ref[h] + pv

    def body(j, _):
        step(j, False)
        return 0

    lax.fori_loop(0, i, body, 0)
    step(i, True)
    for h in range(n_heads):
        acc = acc_ref[h]
        o_ref[:, h * HEAD_DIM:(h + 1) * HEAD_DIM] = (acc[:, :HEAD_DIM] / acc[:, HEAD_DIM:]).astype(o_ref.dtype)


def _attention(qkv, c_row, B, S, n_heads, tq=256):
    T = qkv.shape[0]
    W = n_heads * HEAD_DIM
    nq = S // tq
    c_row5 = c_row.reshape(B, n_heads, nq, 1, tq)
    return pl.pallas_call(
        functools.partial(_attn_kernel, n_heads=n_heads, tq=tq),
        grid=(B, nq),
        in_specs=[pl.BlockSpec((tq, W), lambda b, i: (b * nq + i, 0)),
                  pl.BlockSpec((S, W), lambda b, i: (b, 1)),
                  pl.BlockSpec((S, W), lambda b, i: (b, 2)),
                  pl.BlockSpec((1, n_heads, nq, 1, tq), lambda b, i: (b, 0, 0, 0, 0))],
        out_specs=pl.BlockSpec((tq, W), lambda b, i: (b * nq + i, 0)),
        out_shape=jax.ShapeDtypeStruct((T, W), BF16),
        scratch_shapes=[pltpu.VMEM((n_heads, S, 2 * HEAD_DIM), BF16),
                        pltpu.VMEM((n_heads, tq, HEAD_DIM), F32),
                        pltpu.VMEM((n_heads, tq, 2 * HEAD_DIM), F32)],
        compiler_params=_cparams(("arbitrary", "arbitrary")),
        name="fox_attention",
    )(qkv, qkv, qkv, c_row5)


def _gmlp_kernel(h_ref, wz_ref, g_ref, b_ref, ws_ref, bst_ref, o_ref, *, n_groups):
    z = lax.dot_general(h_ref[...], wz_ref[...], NT_DIMS, preferred_element_type=F32)
    z = 0.5 * z * (1.0 + lax.erf(z * (1.0 / math.sqrt(2.0))))
    W = z.shape[1] // 2
    u = z[:, :W]
    v = z[:, W:]
    mu = jnp.mean(v, axis=-1, keepdims=True)
    var = jnp.mean(jnp.square(v - mu), axis=-1, keepdims=True)
    vn = (v - mu) * lax.rsqrt(var + NORM_EPS) * g_ref[...] + b_ref[...]
    row = lax.broadcasted_iota(I32, (CHUNK, CHUNK), 0)
    col = lax.broadcasted_iota(I32, (CHUNK, CHUNK), 1)
    tril = col <= row
    tg = z.shape[0]
    for g in range(n_groups):
        gs = slice(g * GROUP_DIM, (g + 1) * GROUP_DIM)
        wg = jnp.where(tril, ws_ref[g], 0.0).astype(BF16)
        bias = bst_ref[:, g:g + 1]
        for c in range(tg // CHUNK):
            cs = slice(c * CHUNK, (c + 1) * CHUNK)
            mixed = jnp.dot(wg, vn[cs, gs].astype(BF16), preferred_element_type=F32) + bias
            o_ref[cs, gs] = (u[cs, gs] * mixed).astype(o_ref.dtype)


def _gmlp(h, wz, g_v_ln, b_v_ln, w_spatial, b_spatial, tg=512):
    T, D = h.shape
    W = g_v_ln.shape[0]
    W2 = 2 * W
    G = w_spatial.shape[0]
    return pl.pallas_call(
        functools.partial(_gmlp_kernel, n_groups=G),
        grid=(T // tg,),
        in_specs=[pl.BlockSpec((tg, D), lambda i: (i, 0)),
                  pl.BlockSpec((W2, D), lambda i: (0, 0)),
                  pl.BlockSpec((1, W), lambda i: (0, 0)),
                  pl.BlockSpec((1, W), lambda i: (0, 0)),
                  pl.BlockSpec((G, CHUNK, CHUNK), lambda i: (0, 0, 0)),
                  pl.BlockSpec((CHUNK, G), lambda i: (0, 0))],
        out_specs=pl.BlockSpec((tg, W), lambda i: (i, 0)),
        out_shape=jax.ShapeDtypeStruct((T, W), BF16),
        compiler_params=_cparams(("parallel",)),
        name="gmlp",
    )(h, wz, g_v_ln.reshape(1, W), b_v_ln.reshape(1, W), w_spatial, b_spatial.T)


def _merge_kernel(attn_ref, sg_ref, h_ref, wa_ref, wb_ref, wga_ref, wgb_ref, o_ref):
    h = h_ref[...]
    a = jnp.dot(attn_ref[...], wa_ref[...], preferred_element_type=F32)
    ga = lax.dot_general(h, wga_ref[...], NT_DIMS, preferred_element_type=F32)
    m = jax.nn.sigmoid(ga) * a
    b = jnp.dot(sg_ref[...], wb_ref[...], preferred_element_type=F32)
    gb = lax.dot_general(h, wgb_ref[...], NT_DIMS, preferred_element_type=F32)
    o_ref[...] = (m + jax.nn.sigmoid(gb) * b).astype(o_ref.dtype)


def _merge(attn, sg, h, wa, wb, wg, g_off, tm=512, tn=512):
    T, D = h.shape
    Wa = attn.shape[1]
    Wb = sg.shape[1]
    nt = D // tn
    g0 = g_off // tn
    return pl.pallas_call(
        _merge_kernel,
        grid=(nt, T // tm),
        in_specs=[pl.BlockSpec((tm, Wa), lambda j, i: (i, 0)),
                  pl.BlockSpec((tm, Wb), lambda j, i: (i, 0)),
                  pl.BlockSpec((tm, D), lambda j, i: (i, 0)),
                  pl.BlockSpec((Wa, tn), lambda j, i: (0, j)),
                  pl.BlockSpec((Wb, tn), lambda j, i: (0, j)),
                  pl.BlockSpec((tn, D), lambda j, i: (g0 + j, 0)),
                  pl.BlockSpec((tn, D), lambda j, i: (g0 + nt + j, 0))],
        out_specs=pl.BlockSpec((tm, tn), lambda j, i: (i, j)),
        out_shape=jax.ShapeDtypeStruct((T, D), BF16),
        compiler_params=_cparams(("arbitrary", "arbitrary")),
        name="gated_merge",
    )(attn, sg, h, wa, wb, wg, wg)


def _out_router_kernel(m_ref, x_ref, wo_ref, g_ref, wr_ref, br_ref,
                       x1_ref, h2_ref, sel_ref, gate_ref, *, n_parts):
    E = br_ref.shape[1]
    part = m_ref.shape[0] // n_parts
    for p in range(n_parts):
        rows = pl.ds(p * part, part)
        x1 = x_ref[rows, :] + jnp.dot(m_ref[rows, :], wo_ref[...], preferred_element_type=F32)
        x1_ref[rows, :] = x1
        ms = jnp.mean(x1 * x1, axis=-1, keepdims=True)
        h2 = x1 * lax.rsqrt(ms + NORM_EPS) * g_ref[...]
        h2_ref[rows, :] = _pack_pairs(h2)
        h2_hi = h2.astype(BF16)
        h2_lo = (h2 - h2_hi.astype(F32)).astype(BF16)
        pa = jnp.dot(h2_hi, wr_ref[...], preferred_element_type=F32)
        pb = jnp.dot(h2_lo, wr_ref[:, :E], preferred_element_type=F32)
        logits = pa[:, :E] + (pa[:, E:] + pb) + br_ref[...]
        lane = lax.broadcasted_iota(I32, logits.shape, 1)
        work = logits
        sel = jnp.zeros(logits.shape, F32)
        num = jnp.zeros(logits.shape, F32)
        denom = jnp.zeros((part, 1), F32)
        m0 = None
        for _ in range(TOP_K):
            m = jnp.max(work, axis=-1, keepdims=True)
            idx = jnp.min(jnp.where(work == m, lane, E), axis=-1, keepdims=True)
            onehot = lane == idx
            if m0 is None:
                m0 = m
            e = jnp.exp(m - m0)
            sel = jnp.where(onehot, 1.0, sel)
            num = jnp.where(onehot, e, num)
            denom = denom + e
            work = jnp.where(onehot, -jnp.inf, work)
        sel_ref[rows, :] = sel
        gate_ref[rows, :] = num / denom


def _out_router(merged, x, wo, g_ffn, w_router, b_router, to=512, n_parts=2):
    T, D = x.shape
    E = w_router.shape[1]
    w_hi = w_router.astype(BF16)
    w_lo = (w_router - w_hi.astype(F32)).astype(BF16)
    w_router = jnp.concatenate([w_hi, w_lo], axis=1)
    row = lambda i: (i, 0)
    fixed = lambda i: (0, 0)
    return pl.pallas_call(
        functools.partial(_out_router_kernel, n_parts=n_parts),
        grid=(T // to,),
        in_specs=[pl.BlockSpec((to, D), row), pl.BlockSpec((to, D), row),
                  pl.BlockSpec((D, D), fixed), pl.BlockSpec((1, D), fixed),
                  pl.BlockSpec((D, 2 * E), fixed), pl.BlockSpec((1, E), fixed)],
        out_specs=[pl.BlockSpec((to, D), row), pl.BlockSpec((to, D // 2), row),
                   pl.BlockSpec((to, E), row), pl.BlockSpec((to, E), row)],
        out_shape=[jax.ShapeDtypeStruct((T, D), F32), jax.ShapeDtypeStruct((T, D // 2), jnp.uint32),
                   jax.ShapeDtypeStruct((T, E), F32), jax.ShapeDtypeStruct((T, E), F32)],
        compiler_params=_cparams(("parallel",)),
        name="out_router",
    )(merged, x, wo, g_ffn.reshape(1, D), w_router, b_router.reshape(1, E))


def _routing_kernel(sel_ref, gate_ref, dest_ref, g4_ref, nblk_ref, start_ref, rank_ref, *, tile):
    T, E = sel_ref.shape
    nt = T // tile
    r = lax.broadcasted_iota(I32, (tile, tile), 0)
    c = lax.broadcasted_iota(I32, (tile, tile), 1)
    strict_lower = (c < r).astype(BF16)
    er = lax.broadcasted_iota(I32, (E, E), 0)
    ec = lax.broadcasted_iota(I32, (E, E), 1)
    strict_upper = (er < ec).astype(BF16)

    def pass1(t, carry):
        rows = pl.ds(pl.multiple_of(t * tile, tile), tile)
        a = sel_ref[rows, :]
        rank_ref[rows, :] = jnp.dot(strict_lower, a.astype(BF16), preferred_element_type=F32) + carry
        return carry + jnp.sum(a, axis=0, keepdims=True)

    counts = lax.fori_loop(0, nt, pass1, jnp.zeros((1, E), F32))
    nblk = jnp.floor((counts + (ROW_BLOCK - 1)) * (1.0 / ROW_BLOCK))
    start_blk = jnp.dot(nblk.astype(BF16), strict_upper, preferred_element_type=F32)
    nblk_ref[...] = nblk.astype(I32)
    start_ref[...] = start_blk.astype(I32)
    start_row = start_blk * float(ROW_BLOCK)
    lane = lax.broadcasted_iota(I32, (tile, 128), 1)

    def pass2(t, _):
        rows = pl.ds(pl.multiple_of(t * tile, tile), tile)
        a = sel_ref[rows, :]
        g = gate_ref[rows, :]
        dest_e = rank_ref[rows, :] + start_row
        slot = jnp.dot(a.astype(BF16), strict_upper, preferred_element_type=F32)
        d4 = jnp.zeros((tile, 128), F32)
        g4 = jnp.zeros((tile, 128), F32)
        for s in range(TOP_K):
            pick = (a > 0.5) & (slot == float(s))
            d4 = jnp.where(lane == s, jnp.sum(jnp.where(pick, dest_e, 0.0), axis=-1, keepdims=True), d4)
            g4 = jnp.where(lane == s, jnp.sum(jnp.where(pick, g, 0.0), axis=-1, keepdims=True), g4)
        dest_ref[rows, :] = d4.astype(I32)
        g4_ref[rows, :] = g4
        return 0

    lax.fori_loop(0, nt, pass2, 0)


def _routing(sel, gate, tile=256):
    T, E = sel.shape
    return pl.pallas_call(
        functools.partial(_routing_kernel, tile=tile),
        out_shape=[jax.ShapeDtypeStruct((T, 128), I32), jax.ShapeDtypeStruct((T, 128), F32),
                   jax.ShapeDtypeStruct((1, E), I32), jax.ShapeDtypeStruct((1, E), I32)],
        scratch_shapes=[pltpu.VMEM((T, E), F32)],
        compiler_params=pltpu.CompilerParams(vmem_limit_bytes=VMEM_LIMIT_BYTES),
        name="routing_ranks",
    )(sel, gate)


SC_CORES = 2
SC_SUBCORES = 16
SC_CHUNK = 64


def _sc_gather_rows(table, idx):
    n = idx.shape[0]
    W = table.shape[1]
    n_workers = SC_CORES * SC_SUBCORES
    per_worker = n // n_workers
    assert per_worker * n_workers == n and per_worker % SC_CHUNK == 0
    mesh = plsc.VectorSubcoreMesh(core_axis_name="c", subcore_axis_name="s",
                                  num_cores=SC_CORES, num_subcores=SC_SUBCORES)

    @functools.partial(
        pl.kernel, mesh=mesh,
        out_type=jax.ShapeDtypeStruct((n, W), table.dtype),
        scratch_types=[pltpu.VMEM((SC_CHUNK,), I32), pltpu.VMEM((SC_CHUNK, W), table.dtype),
                       pltpu.SemaphoreType.DMA],
        name="sc_gather_rows",
    )
    def gather(table_hbm, idx_hbm, out_hbm, idx_v, rows_v, sem):
        wid = lax.axis_index("s") * SC_CORES + lax.axis_index("c")
        base = wid * per_worker

        @pl.loop(0, per_worker // SC_CHUNK)
        def _(c):
            off = pl.multiple_of(base + c * SC_CHUNK, 8)
            pltpu.sync_copy(idx_hbm.at[pl.ds(off, SC_CHUNK)], idx_v)
            pltpu.async_copy(table_hbm.at[idx_v], rows_v, sem).wait()
            pltpu.sync_copy(rows_v, out_hbm.at[pl.ds(off, SC_CHUNK)])

    return gather(table, idx)


def _sc_scatter_rows(rows, idx, n_out, n_slots):
    T, W = rows.shape
    n_workers = SC_CORES * SC_SUBCORES
    per_worker = T // n_workers
    assert per_worker * n_workers == T and per_worker % SC_CHUNK == 0
    mesh = plsc.VectorSubcoreMesh(core_axis_name="c", subcore_axis_name="s",
                                  num_cores=SC_CORES, num_subcores=SC_SUBCORES)

    @functools.partial(
        pl.kernel, mesh=mesh,
        out_type=jax.ShapeDtypeStruct((n_out, W), rows.dtype),
        scratch_types=[pltpu.VMEM((SC_CHUNK,), I32), pltpu.VMEM((SC_CHUNK, W), rows.dtype)],
        name="sc_scatter_rows",
    )
    def scatter(rows_hbm, idx_hbm, out_hbm, idx_v, rows_v):
        wid = lax.axis_index("s") * SC_CORES + lax.axis_index("c")
        base = wid * per_worker

        @pl.loop(0, per_worker // SC_CHUNK)
        def _(c):
            off = pl.multiple_of(base + c * SC_CHUNK, 8)
            pltpu.sync_copy(rows_hbm.at[pl.ds(off, SC_CHUNK)], rows_v)
            for k in range(n_slots):
                pltpu.sync_copy(idx_hbm.at[pl.ds(pl.multiple_of(k * T + off, 8), SC_CHUNK)], idx_v)
                pltpu.sync_copy(rows_v, out_hbm.at[idx_v])

    return scatter(rows, idx)


def _expert_kernel(ie_ref, ib_ref, ins_ref,
                   xs_hbm, wup_ref, bup_ref, wdn_ref, bdn_ref, perm_ref, ys_hbm,
                   xg_ref, acc_ref, yst_ref, wupb_ref, wdnb_ref, gsem, osem, *, n_ff_tiles):
    i = pl.program_id(0)
    j = pl.program_id(1)
    n_items = pl.num_programs(0)
    nsub = ins_ref[i]
    slot = i % 2

    def for_regions(n_blocks, fn):
        first = jnp.int32(0)
        for count in REGION_BLOCKS:
            present = (n_blocks & count) != 0
            pl.when(present)(functools.partial(fn, first, count))
            first = first + jnp.where(present, count, 0)

    def rows_of(first, count):
        return pl.ds(pl.multiple_of(first * ROW_BLOCK, ROW_BLOCK), count * ROW_BLOCK)

    def in_copy(item, dst_slot, first, count):
        src = pl.multiple_of((ib_ref[item] + first) * ROW_BLOCK, ROW_BLOCK)
        return pltpu.make_async_copy(xs_hbm.at[pl.ds(src, count * ROW_BLOCK)],
                                     xg_ref.at[dst_slot, rows_of(first, count)], gsem.at[dst_slot])

    def fetch_item(item, n_blocks, dst_slot):
        for_regions(n_blocks, lambda first, count: in_copy(item, dst_slot, first, count).start())

    @pl.when(j == 0)
    def _():
        @pl.when(i == 0)
        def _():
            fetch_item(0, nsub, 0)

        nxt = jnp.minimum(i + 1, n_items - 1)
        fetch_item(nxt, jnp.where(i + 1 < n_items, ins_ref[nxt], 0), 1 - slot)

        def arrive(first, count):
            in_copy(i, slot, first, count).wait()
            acc_ref[rows_of(first, count), :] = jnp.broadcast_to(
                bdn_ref[0], (count * ROW_BLOCK, acc_ref.shape[1]))

        for_regions(nsub, arrive)

    @pl.when(nsub > 0)
    def _():
        bup = bup_ref[0]
        perm = perm_ref[...]
        half = perm.shape[0] // 2

        def run_blocks(first, count):
            if first == 0:
                wup = wup_ref[0].astype(BF16)
                wdn = wdn_ref[0].astype(BF16)
                wupb_ref[...] = wup
                wdnb_ref[...] = wdn
            else:
                wup = wupb_ref[...]
                wdn = wdnb_ref[...]
            rows = pl.ds(first * ROW_BLOCK, count * ROW_BLOCK)
            xb = _unpack_pairs(xg_ref[slot, rows, :]).astype(BF16)
            gu = (jnp.dot(xb, wup, preferred_element_type=F32) + bup).astype(BF16)
            glu_parts, lin_parts = [], []
            for p in range(gu.shape[1] // perm.shape[0]):
                gp = jnp.dot(gu[:, p * perm.shape[0]:(p + 1) * perm.shape[0]], perm,
                             preferred_element_type=F32)
                glu_parts.append(gp[:, :half])
                lin_parts.append(gp[:, half:])
            x_glu = jnp.minimum(jnp.concatenate(glu_parts, axis=1), SWIGLU_LIMIT)
            x_lin = jnp.clip(jnp.concatenate(lin_parts, axis=1), -SWIGLU_LIMIT, SWIGLU_LIMIT)
            act = x_glu * jax.nn.sigmoid(SWIGLU_ALPHA * x_glu) * (x_lin + 1.0)
            acc_ref[rows, :] += jnp.dot(act.astype(BF16), wdn, preferred_element_type=F32)

        for n_blocks, plan in COMPUTE_PLANS.items():
            for first, count in plan:
                shared = [n for n, p in COMPUTE_PLANS.items() if (first, count) in p]
                if n_blocks == shared[0]:
                    cond = functools.reduce(jnp.logical_or, [nsub == n for n in shared])
                    pl.when(cond)(functools.partial(run_blocks, first, count))

    def out_copy(item, first, count):
        dst = pl.multiple_of((ib_ref[item] + first) * ROW_BLOCK, ROW_BLOCK)
        return pltpu.make_async_copy(yst_ref.at[rows_of(first, count)],
                                     ys_hbm.at[pl.ds(dst, count * ROW_BLOCK)], osem)

    @pl.when(j == n_ff_tiles - 1)
    def _():
        prev = jnp.maximum(i - 1, 0)
        for_regions(jnp.where(i > 0, ins_ref[prev], 0),
                    lambda first, count: out_copy(prev, first, count).wait())

        def leave(first, count):
            rows = rows_of(first, count)
            yst_ref[rows, :] = _pack_pairs(acc_ref[rows, :])
            out_copy(i, first, count).start()

        for_regions(nsub, leave)

        @pl.when(i == n_items - 1)
        def _():
            for_regions(nsub, lambda first, count: out_copy(i, first, count).wait())


def _experts(xs, item_e, item_b, item_n, n_active, w_up, b_up, w_down, b_down):
    n_rows, Dp = xs.shape
    D = 2 * Dp
    E, _, F2 = w_up.shape
    F = F2 // 2
    J = F // FF_TILE
    half = 128
    perm = np.zeros((2 * half, 2 * half), np.float32)
    perm[2 * np.arange(half), np.arange(half)] = 1.0
    perm[2 * np.arange(half) + 1, half + np.arange(half)] = 1.0

    def jj(i, j, ins):
        return jnp.where(ins[i] > 0, j, J - 1)

    grid_spec = pltpu.PrefetchScalarGridSpec(
        num_scalar_prefetch=3,
        grid=(n_active, J),
        in_specs=[pl.BlockSpec(memory_space=pl.ANY),
                  pl.BlockSpec((1, D, 2 * FF_TILE), lambda i, j, ie, ib, ins: (ie[i], 0, jj(i, j, ins))),
                  pl.BlockSpec((1, 1, 2 * FF_TILE), lambda i, j, ie, ib, ins: (ie[i], 0, jj(i, j, ins))),
                  pl.BlockSpec((1, FF_TILE, D), lambda i, j, ie, ib, ins: (ie[i], jj(i, j, ins), 0)),
                  pl.BlockSpec((1, 1, D), lambda i, j, ie, ib, ins: (ie[i], 0, 0)),
                  pl.BlockSpec((2 * half, 2 * half), lambda i, j, ie, ib, ins: (0, 0))],
        out_specs=pl.BlockSpec(memory_space=pl.ANY),
        scratch_shapes=[pltpu.VMEM((2, ITEM_ROWS, Dp), jnp.uint32),
                        pltpu.VMEM((ITEM_ROWS, D), F32),
                        pltpu.VMEM((ITEM_ROWS, Dp), jnp.uint32),
                        pltpu.VMEM((D, 2 * FF_TILE), BF16),
                        pltpu.VMEM((FF_TILE, D), BF16),
                        pltpu.SemaphoreType.DMA((2,)), pltpu.SemaphoreType.DMA(())],
    )
    return pl.pallas_call(
        functools.partial(_expert_kernel, n_ff_tiles=J),
        grid_spec=grid_spec,
        out_shape=jax.ShapeDtypeStruct((n_rows, Dp), jnp.uint32),
        compiler_params=_cparams(("arbitrary", "arbitrary")),
        name="expert_ffn",
    )(item_e, item_b, item_n,
      xs, w_up, b_up.reshape(E, 1, F2), w_down, b_down.reshape(E, 1, D), jnp.asarray(perm, BF16))


def _combine_kernel(*refs):
    yk_refs, (g4_ref, x1_ref, g_ref, o_ref) = refs[:TOP_K], refs[TOP_K:]
    y = x1_ref[...]
    g4 = g4_ref[...]
    for k in range(TOP_K):
        y = y + g4[:, k:k + 1] * _unpack_pairs(yk_refs[k][...])
    ms = jnp.mean(y * y, axis=-1, keepdims=True)
    o_ref[...] = y * lax.rsqrt(ms + NORM_EPS) * g_ref[...]


def _combine(yk, g4, x1, g_final, tc=256):
    T, D = x1.shape
    nt = T // tc
    slot_specs = [pl.BlockSpec((tc, D // 2), functools.partial(lambda i, k: (k * nt + i, 0), k=k))
                  for k in range(TOP_K)]
    return pl.pallas_call(
        _combine_kernel,
        grid=(nt,),
        in_specs=slot_specs + [pl.BlockSpec((tc, 128), lambda i: (i, 0)),
                               pl.BlockSpec((tc, D), lambda i: (i, 0)),
                               pl.BlockSpec((1, D), lambda i: (0, 0))],
        out_specs=pl.BlockSpec((tc, D), lambda i: (i, 0)),
        out_shape=jax.ShapeDtypeStruct((T, D), F32),
        compiler_params=_cparams(("parallel",)),
        name="combine_norm",
    )(*([yk] * TOP_K), g4, x1, g_final.reshape(1, D))


def _work_items(nblk, start_blk, n_items):
    E = nblk.shape[0]
    per_e = (nblk + ITEM_BLOCKS - 1) // ITEM_BLOCKS
    ends = jnp.cumsum(per_e)
    total = ends[-1]
    idx = jnp.arange(n_items, dtype=I32)
    e = jnp.minimum(jnp.searchsorted(ends, idx, side="right"), E - 1).astype(I32)
    local = idx - (ends[e] - per_e[e])
    active = idx < total
    last_e = e[jnp.maximum(total - 1, 0)]
    item_e = jnp.where(active, e, last_e).astype(I32)
    item_b = jnp.where(active, start_blk[e] + local * ITEM_BLOCKS, 0).astype(I32)
    item_n = jnp.where(active, jnp.clip(nblk[e] - local * ITEM_BLOCKS, 0, ITEM_BLOCKS), 0).astype(I32)
    return item_e, item_b, item_n, jnp.maximum(total, 1).astype(I32)


def kernel(x, g_mix, w_in, b_forget, g_v_ln, b_v_ln, w_spatial, b_spatial, w_branch_attn, w_branch_gmlp, w_out, g_ffn, w_router, b_router, w_expert_up, b_expert_up, w_expert_down, b_expert_down, g_final):
    B, S, D = x.shape
    T = B * S
    n_heads = b_forget.shape[0]
    attn_w = n_heads * HEAD_DIM
    gmlp_w = g_v_ln.shape[0]
    E = w_router.shape[1]
    off_f = 3 * attn_w
    off_z = off_f + n_heads
    off_g = off_z + 2 * gmlp_w

    x2 = x.reshape(T, D)
    w_in_t = w_in.T
    wzg = _rows_bf16(w_in_t, off_z, w_in.shape[1] - off_z)
    wa = w_branch_attn.astype(BF16)
    wb = w_branch_gmlp.astype(BF16)
    wo = w_out.astype(BF16)

    h = _rmsnorm(x2, g_mix, BF16)
    qkv = _project(h, w_in_t, 3 * attn_w, attn_w, LOG2E / math.sqrt(HEAD_DIM), BF16)
    c_row = _forget_cumsum(h, w_in_t, off_f, b_forget, B, S)
    attn = _attention(qkv, c_row, B, S, n_heads)
    sg = _gmlp(h, wzg, g_v_ln, b_v_ln, w_spatial, b_spatial)
    merged = _merge(attn, sg, h, wa, wb, wzg, off_g - off_z)
    x1, h2p, sel, gate = _out_router(merged, x2, wo, g_ffn, w_router, b_router)

    dest4, g4, nblk, start_blk = _routing(sel, gate)
    n_rows = T * TOP_K + E * ROW_BLOCK
    dest_slots = dest4[:, :TOP_K].T.reshape(TOP_K * T)
    max_blocks = n_rows // ROW_BLOCK
    n_items = E + -(-(max_blocks - E) // ITEM_BLOCKS) + 1
    item_e, item_b, item_n, n_active = _work_items(nblk[0], start_blk[0], n_items)
    xs = _sc_scatter_rows(h2p, dest_slots, n_rows, TOP_K)
    ys = _experts(xs, item_e, item_b, item_n, n_active,
                  w_expert_up, b_expert_up, w_expert_down, b_expert_down)
    yk = _sc_gather_rows(ys, dest_slots)
    out = _combine(yk, g4, x1, g_final)
    return out.reshape(B, S, D)
```

```python
import functools
import math

import jax
import jax.numpy as jnp
import numpy as np
from jax import lax
from jax.experimental import pallas as pl
from jax.experimental.pallas import tpu as pltpu
from jax.experimental.pallas import tpu_sc as plsc

F32 = jnp.float32
BF16 = jnp.bfloat16
I32 = jnp.int32

NORM_EPS = 1e-5
LANES = 128
SUBLANES = 8
NT_DIMS = (((1,), (1,)), ((), ()))
HEAD_DIM = 128
CHUNK = 128
GROUP_DIM = 128
TOP_K = 4
SWIGLU_ALPHA = 1.702
SWIGLU_LIMIT = 7.0
LOG2E = math.log2(math.e)

VMEM_LIMIT_BYTES = 56 * 1024 * 1024

ROW_BLOCK = 256
ITEM_BLOCKS = 5
ITEM_ROWS = ITEM_BLOCKS * ROW_BLOCK
REGION_BLOCKS = (4, 2, 1)
assert sum(REGION_BLOCKS) >= ITEM_BLOCKS
COMPUTE_PLANS = {5: ((0, 5),), 4: ((0, 4),), 3: ((0, 2), (2, 1)), 2: ((0, 2),), 1: ((0, 1),)}
assert set(COMPUTE_PLANS) == set(range(1, ITEM_BLOCKS + 1))
FF_TILE = 256
COMBINE_PARTS = 2


def _cparams(sem, **kw):
    return pltpu.CompilerParams(dimension_semantics=sem, vmem_limit_bytes=VMEM_LIMIT_BYTES, **kw)


def _pack_pairs(x):
    c = x.shape[1] // 2
    hi = lax.bitcast_convert_type(x[:, :c].astype(BF16).astype(F32), jnp.uint32)
    lo = lax.bitcast_convert_type(x[:, c:].astype(BF16).astype(F32), jnp.uint32)
    return hi | (lo >> 16)


def _unpack_pairs(w):
    hi = lax.bitcast_convert_type(w & jnp.uint32(0xFFFF0000), F32)
    lo = lax.bitcast_convert_type(w << 16, F32)
    return jnp.concatenate([hi, lo], axis=1)


def _norm_proj_kernel(x_ref, g_ref, w_ref, o_ref, h_ref, *, n_scaled, scale):
    j = pl.program_id(1)

    @pl.when(j == 0)
    def _():
        x = x_ref[...]
        ms = jnp.mean(x * x, axis=-1, keepdims=True)
        h_ref[...] = (x * lax.rsqrt(ms + NORM_EPS) * g_ref[...]).astype(h_ref.dtype)

    acc = lax.dot_general(h_ref[...], w_ref[...].astype(BF16), NT_DIMS, preferred_element_type=F32)
    o_ref[...] = (acc * jnp.where(j < n_scaled, scale, 1.0)).astype(o_ref.dtype)


def _norm_project(x, g, wt, n_cols, n_scaled_cols, scale, tm=1024, tn=512):
    T, D = x.shape
    return pl.pallas_call(
        functools.partial(_norm_proj_kernel, n_scaled=n_scaled_cols // tn, scale=scale),
        grid=(T // tm, n_cols // tn),
        in_specs=[pl.BlockSpec((tm, D), lambda i, j: (i, 0)),
                  pl.BlockSpec((1, D), lambda i, j: (0, 0)),
                  pl.BlockSpec((tn, D), lambda i, j: (j, 0))],
        out_specs=[pl.BlockSpec((tm, tn), lambda i, j: (i, j)),
                   pl.BlockSpec((tm, D), lambda i, j: (i, 0))],
        out_shape=[jax.ShapeDtypeStruct((T, n_cols), BF16), jax.ShapeDtypeStruct((T, D), BF16)],
        compiler_params=_cparams(("arbitrary", "arbitrary")),
        name="norm_qkv_proj",
    )(x, g.reshape(1, D), wt)


def _cast_kernel(w_ref, o_ref):
    o_ref[...] = w_ref[...].astype(o_ref.dtype)


def _rows_bf16(wt, start, n_rows, tr=512):
    D = wt.shape[1]
    assert start % SUBLANES == 0 and n_rows % tr == 0
    return pl.pallas_call(
        _cast_kernel,
        grid=(n_rows // tr,),
        in_specs=[pl.BlockSpec((pl.Element(tr), pl.Element(D)),
                               lambda i: (pl.multiple_of(start + i * tr, SUBLANES), 0))],
        out_specs=pl.BlockSpec((tr, D), lambda i: (i, 0)),
        out_shape=jax.ShapeDtypeStruct((n_rows, D), BF16),
        compiler_params=_cparams(("parallel",)),
        name="rows_bf16",
    )(wt)


def _forget_kernel(h_ref, wft_ref, bf_ref, c_ref):
    ft = lax.dot_general(wft_ref[...].astype(BF16), h_ref[...], NT_DIMS,
                         preferred_element_type=F32)
    c = jax.nn.log_sigmoid(ft + bf_ref[...])
    S = c.shape[1]
    lane = lax.broadcasted_iota(I32, c.shape, 1)
    shift = 1
    while shift < S:
        c = c + jnp.where(lane >= shift, pltpu.roll(c, shift, axis=1), 0.0)
        shift *= 2
    c_ref[0] = c * LOG2E


def _forget_cumsum(h, wt, f_off, b_forget, B, S):
    T, D = h.shape
    H = b_forget.shape[0]
    assert f_off % H == 0 and H % SUBLANES == 0
    return pl.pallas_call(
        _forget_kernel,
        grid=(B,),
        in_specs=[pl.BlockSpec((S, D), lambda b: (b, 0)),
                  pl.BlockSpec((H, D), lambda b: (f_off // H, 0)),
                  pl.BlockSpec((H, 1), lambda b: (0, 0))],
        out_specs=pl.BlockSpec((1, H, S), lambda b: (b, 0, 0)),
        out_shape=jax.ShapeDtypeStruct((B, H, S), F32),
        compiler_params=_cparams(("parallel",)),
        name="forget_cumsum",
    )(h, wt, b_forget.reshape(H, 1))


def _attn_kernel(q_ref, k_ref, v_ref, crow_ref, o_ref, vaug_ref, m_ref, acc_ref, *, n_heads, tq):
    i = pl.program_id(1)

    @pl.when(i == 0)
    def _():
        ones = jnp.ones((v_ref.shape[0], HEAD_DIM), BF16)
        for h in range(n_heads):
            vaug_ref[h, :, :HEAD_DIM] = v_ref[:, h * HEAD_DIM:(h + 1) * HEAD_DIM]
            vaug_ref[h, :, HEAD_DIM:] = ones

    m_ref[...] = jnp.full(m_ref.shape, -jnp.inf, F32)
    acc_ref[...] = jnp.zeros(acc_ref.shape, F32)
    row = lax.broadcasted_iota(I32, (tq, tq), 0)
    col = lax.broadcasted_iota(I32, (tq, tq), 1)
    causal = col <= row

    def step(j, masked):
        keys = pl.ds(pl.multiple_of(j * tq, tq), tq)
        for h in range(n_heads):
            hs = slice(h * HEAD_DIM, (h + 1) * HEAD_DIM)
            s = lax.dot_general(q_ref[:, hs], k_ref[keys, hs], (((1,), (1,)), ((), ())),
                                preferred_element_type=F32) - crow_ref[0, h, j]
            if masked:
                s = jnp.where(causal, s, -jnp.inf)
            m_old = m_ref[h]
            m_new = jnp.maximum(m_old, jnp.max(s, axis=-1, keepdims=True))
            alpha = jnp.exp2(m_old - m_new)
            p = jnp.exp2(s - jnp.concatenate([m_new] * (tq // HEAD_DIM), axis=1))
            m_ref[h] = m_new
            pv = jnp.dot(p.astype(BF16), vaug_ref[h, keys, :], preferred_element_type=F32)
            acc_ref[h] = jnp.concatenate([alpha, alpha], axis=1) * acc_ref[h] + pv

    def body(j, _):
        step(j, False)
        return 0

    lax.fori_loop(0, i, body, 0)
    step(i, True)
    for h in range(n_heads):
        acc = acc_ref[h]
        o_ref[:, h * HEAD_DIM:(h + 1) * HEAD_DIM] = (acc[:, :HEAD_DIM] / acc[:, HEAD_DIM:]).astype(o_ref.dtype)


def _attention(qkv, c_row, B, S, n_heads, tq=256):
    T = qkv.shape[0]
    W = n_heads * HEAD_DIM
    nq = S // tq
    c_row5 = c_row.reshape(B, n_heads, nq, 1, tq)
    return pl.pallas_call(
        functools.partial(_attn_kernel, n_heads=n_heads, tq=tq),
        grid=(B, nq),
        in_specs=[pl.BlockSpec((tq, W), lambda b, i: (b * nq + i, 0)),
                  pl.BlockSpec((S, W), lambda b, i: (b, 1)),
                  pl.BlockSpec((S, W), lambda b, i: (b, 2)),
                  pl.BlockSpec((1, n_heads, nq, 1, tq), lambda b, i: (b, 0, 0, 0, 0))],
        out_specs=pl.BlockSpec((tq, W), lambda b, i: (b * nq + i, 0)),
        out_shape=jax.ShapeDtypeStruct((T, W), BF16),
        scratch_shapes=[pltpu.VMEM((n_heads, S, 2 * HEAD_DIM), BF16),
                        pltpu.VMEM((n_heads, tq, HEAD_DIM), F32),
                        pltpu.VMEM((n_heads, tq, 2 * HEAD_DIM), F32)],
        compiler_params=_cparams(("arbitrary", "arbitrary")),
        name="fox_attention",
    )(qkv, qkv, qkv, c_row5)


def _gmlp_kernel(h_ref, wz_ref, g_ref, b_ref, ws_ref, bst_ref, o_ref, *, n_groups):
    z = lax.dot_general(h_ref[...], wz_ref[...], NT_DIMS, preferred_element_type=F32)
    z = 0.5 * z * (1.0 + lax.erf(z * (1.0 / math.sqrt(2.0))))
    W = z.shape[1] // 2
    u = z[:, :W]
    v = z[:, W:]
    mu = jnp.mean(v, axis=-1, keepdims=True)
    var = jnp.mean(jnp.square(v - mu), axis=-1, keepdims=True)
    vn = (v - mu) * lax.rsqrt(var + NORM_EPS) * g_ref[...] + b_ref[...]
    row = lax.broadcasted_iota(I32, (CHUNK, CHUNK), 0)
    col = lax.broadcasted_iota(I32, (CHUNK, CHUNK), 1)
    tril = col <= row
    tg = z.shape[0]
    for g in range(n_groups):
        gs = slice(g * GROUP_DIM, (g + 1) * GROUP_DIM)
        wg = jnp.where(tril, ws_ref[g], 0.0).astype(BF16)
        bias = bst_ref[:, g:g + 1]
        for c in range(tg // CHUNK):
            cs = slice(c * CHUNK, (c + 1) * CHUNK)
            mixed = jnp.dot(wg, vn[cs, gs].astype(BF16), preferred_element_type=F32) + bias
            o_ref[cs, gs] = (u[cs, gs] * mixed).astype(o_ref.dtype)


def _gmlp(h, wz, g_v_ln, b_v_ln, w_spatial, b_spatial, tg=512):
    T, D = h.shape
    W = g_v_ln.shape[0]
    W2 = 2 * W
    G = w_spatial.shape[0]
    return pl.pallas_call(
        functools.partial(_gmlp_kernel, n_groups=G),
        grid=(T // tg,),
        in_specs=[pl.BlockSpec((tg, D), lambda i: (i, 0)),
                  pl.BlockSpec((W2, D), lambda i: (0, 0)),
                  pl.BlockSpec((1, W), lambda i: (0, 0)),
                  pl.BlockSpec((1, W), lambda i: (0, 0)),
                  pl.BlockSpec((G, CHUNK, CHUNK), lambda i: (0, 0, 0)),
                  pl.BlockSpec((CHUNK, G), lambda i: (0, 0))],
        out_specs=pl.BlockSpec((tg, W), lambda i: (i, 0)),
        out_shape=jax.ShapeDtypeStruct((T, W), BF16),
        compiler_params=_cparams(("parallel",)),
        name="gmlp",
    )(h, wz, g_v_ln.reshape(1, W), b_v_ln.reshape(1, W), w_spatial, b_spatial.T)


def _merge_kernel(attn_ref, sg_ref, h_ref, wa_ref, wb_ref, wga_ref, wgb_ref, o_ref):
    h = h_ref[...]
    a = jnp.dot(attn_ref[...], wa_ref[...], preferred_element_type=F32)
    ga = lax.dot_general(h, wga_ref[...], NT_DIMS, preferred_element_type=F32)
    m = jax.nn.sigmoid(ga) * a
    b = jnp.dot(sg_ref[...], wb_ref[...], preferred_element_type=F32)
    gb = lax.dot_general(h, wgb_ref[...], NT_DIMS, preferred_element_type=F32)
    o_ref[...] = (m + jax.nn.sigmoid(gb) * b).astype(o_ref.dtype)


def _merge(attn, sg, h, wa, wb, wg, g_off, tm=512, tn=512):
    T, D = h.shape
    Wa = attn.shape[1]
    Wb = sg.shape[1]
    nt = D // tn
    g0 = g_off // tn
    return pl.pallas_call(
        _merge_kernel,
        grid=(nt, T // tm),
        in_specs=[pl.BlockSpec((tm, Wa), lambda j, i: (i, 0)),
                  pl.BlockSpec((tm, Wb), lambda j, i: (i, 0)),
                  pl.BlockSpec((tm, D), lambda j, i: (i, 0)),
                  pl.BlockSpec((Wa, tn), lambda j, i: (0, j)),
                  pl.BlockSpec((Wb, tn), lambda j, i: (0, j)),
                  pl.BlockSpec((tn, D), lambda j, i: (g0 + j, 0)),
                  pl.BlockSpec((tn, D), lambda j, i: (g0 + nt + j, 0))],
        out_specs=pl.BlockSpec((tm, tn), lambda j, i: (i, j)),
        out_shape=jax.ShapeDtypeStruct((T, D), BF16),
        compiler_params=_cparams(("arbitrary", "arbitrary")),
        name="gated_merge",
    )(attn, sg, h, wa, wb, wg, wg)


def _out_router_kernel(m_ref, x_ref, wo_ref, g_ref, wr_ref, br_ref,
                       x1_ref, h2_ref, sel_ref, gate_ref, *, n_parts):
    E = br_ref.shape[1]
    part = m_ref.shape[0] // n_parts
    for p in range(n_parts):
        rows = pl.ds(p * part, part)
        x1 = x_ref[rows, :] + jnp.dot(m_ref[rows, :], wo_ref[...], preferred_element_type=F32)
        x1_ref[rows, :] = x1
        ms = jnp.mean(x1 * x1, axis=-1, keepdims=True)
        h2 = x1 * lax.rsqrt(ms + NORM_EPS) * g_ref[...]
        h2_ref[rows, :] = _pack_pairs(h2)
        h2_hi = h2.astype(BF16)
        h2_lo = (h2 - h2_hi.astype(F32)).astype(BF16)
        pa = jnp.dot(h2_hi, wr_ref[...], preferred_element_type=F32)
        pb = jnp.dot(h2_lo, wr_ref[:, :E], preferred_element_type=F32)
        logits = pa[:, :E] + (pa[:, E:] + pb) + br_ref[...]
        lane = lax.broadcasted_iota(I32, logits.shape, 1)
        work = logits
        sel = jnp.zeros(logits.shape, F32)
        num = jnp.zeros(logits.shape, F32)
        denom = jnp.zeros((part, 1), F32)
        m0 = None
        for _ in range(TOP_K):
            m = jnp.max(work, axis=-1, keepdims=True)
            idx = jnp.min(jnp.where(work == m, lane, E), axis=-1, keepdims=True)
            onehot = lane == idx
            if m0 is None:
                m0 = m
            e = jnp.exp(m - m0)
            sel = jnp.where(onehot, 1.0, sel)
            num = jnp.where(onehot, e, num)
            denom = denom + e
            work = jnp.where(onehot, -jnp.inf, work)
        sel_ref[rows, :] = sel
        gate_ref[rows, :] = num / denom


def _out_router(merged, x, wo, g_ffn, w_router, b_router, to=512, n_parts=2):
    T, D = x.shape
    E = w_router.shape[1]
    w_hi = w_router.astype(BF16)
    w_lo = (w_router - w_hi.astype(F32)).astype(BF16)
    w_router = jnp.concatenate([w_hi, w_lo], axis=1)
    row = lambda i: (i, 0)
    fixed = lambda i: (0, 0)
    return pl.pallas_call(
        functools.partial(_out_router_kernel, n_parts=n_parts),
        grid=(T // to,),
        in_specs=[pl.BlockSpec((to, D), row), pl.BlockSpec((to, D), row),
                  pl.BlockSpec((D, D), fixed), pl.BlockSpec((1, D), fixed),
                  pl.BlockSpec((D, 2 * E), fixed), pl.BlockSpec((1, E), fixed)],
        out_specs=[pl.BlockSpec((to, D), row), pl.BlockSpec((to, D // 2), row),
                   pl.BlockSpec((to, E), row), pl.BlockSpec((to, E), row)],
        out_shape=[jax.ShapeDtypeStruct((T, D), F32), jax.ShapeDtypeStruct((T, D // 2), jnp.uint32),
                   jax.ShapeDtypeStruct((T, E), F32), jax.ShapeDtypeStruct((T, E), F32)],
        compiler_params=_cparams(("parallel",)),
        name="out_router",
    )(merged, x, wo, g_ffn.reshape(1, D), w_router, b_router.reshape(1, E))


def _routing_kernel(sel_ref, gate_ref, dest_ref, g4_ref, nblk_ref, start_ref, rank_ref, *, tile):
    T, E = sel_ref.shape
    nt = T // tile
    r = lax.broadcasted_iota(I32, (tile, tile), 0)
    c = lax.broadcasted_iota(I32, (tile, tile), 1)
    strict_lower = (c < r).astype(BF16)
    er = lax.broadcasted_iota(I32, (E, E), 0)
    ec = lax.broadcasted_iota(I32, (E, E), 1)
    strict_upper = (er < ec).astype(BF16)

    def pass1(t, carry):
        rows = pl.ds(pl.multiple_of(t * tile, tile), tile)
        a = sel_ref[rows, :]
        rank_ref[rows, :] = jnp.dot(strict_lower, a.astype(BF16), preferred_element_type=F32) + carry
        return carry + jnp.sum(a, axis=0, keepdims=True)

    counts = lax.fori_loop(0, nt, pass1, jnp.zeros((1, E), F32))
    nblk = jnp.floor((counts + (ROW_BLOCK - 1)) * (1.0 / ROW_BLOCK))
    start_blk = jnp.dot(nblk.astype(BF16), strict_upper, preferred_element_type=F32)
    nblk_ref[...] = nblk.astype(I32)
    start_ref[...] = start_blk.astype(I32)
    start_row = start_blk * float(ROW_BLOCK)
    lane = lax.broadcasted_iota(I32, (tile, 128), 1)

    def pass2(t, _):
        rows = pl.ds(pl.multiple_of(t * tile, tile), tile)
        a = sel_ref[rows, :]
        g = gate_ref[rows, :]
        dest_e = rank_ref[rows, :] + start_row
        slot = jnp.dot(a.astype(BF16), strict_upper, preferred_element_type=F32)
        d4 = jnp.zeros((tile, 128), F32)
        g4 = jnp.zeros((tile, 128), F32)
        for s in range(TOP_K):
            pick = (a > 0.5) & (slot == float(s))
            d4 = jnp.where(lane == s, jnp.sum(jnp.where(pick, dest_e, 0.0), axis=-1, keepdims=True), d4)
            g4 = jnp.where(lane == s, jnp.sum(jnp.where(pick, g, 0.0), axis=-1, keepdims=True), g4)
        dest_ref[rows, :] = d4.astype(I32)
        g4_ref[rows, :] = g4
        return 0

    lax.fori_loop(0, nt, pass2, 0)


def _routing(sel, gate, tile=256):
    T, E = sel.shape
    return pl.pallas_call(
        functools.partial(_routing_kernel, tile=tile),
        out_shape=[jax.ShapeDtypeStruct((T, 128), I32), jax.ShapeDtypeStruct((T, 128), F32),
                   jax.ShapeDtypeStruct((1, E), I32), jax.ShapeDtypeStruct((1, E), I32)],
        scratch_shapes=[pltpu.VMEM((T, E), F32)],
        compiler_params=pltpu.CompilerParams(vmem_limit_bytes=VMEM_LIMIT_BYTES),
        name="routing_ranks",
    )(sel, gate)


SC_CORES = 2
SC_SUBCORES = 16
SC_CHUNK = 64


def _sc_gather_rows(table, idx):
    n = idx.shape[0]
    W = table.shape[1]
    n_workers = SC_CORES * SC_SUBCORES
    per_worker = n // n_workers
    assert per_worker * n_workers == n and per_worker % SC_CHUNK == 0
    mesh = plsc.VectorSubcoreMesh(core_axis_name="c", subcore_axis_name="s",
                                  num_cores=SC_CORES, num_subcores=SC_SUBCORES)

    @functools.partial(
        pl.kernel, mesh=mesh,
        out_type=jax.ShapeDtypeStruct((n, W), table.dtype),
        scratch_types=[pltpu.VMEM((SC_CHUNK,), I32), pltpu.VMEM((SC_CHUNK, W), table.dtype),
                       pltpu.SemaphoreType.DMA],
        name="sc_gather_rows",
    )
    def gather(table_hbm, idx_hbm, out_hbm, idx_v, rows_v, sem):
        wid = lax.axis_index("s") * SC_CORES + lax.axis_index("c")
        base = wid * per_worker

        @pl.loop(0, per_worker // SC_CHUNK)
        def _(c):
            off = pl.multiple_of(base + c * SC_CHUNK, 8)
            pltpu.sync_copy(idx_hbm.at[pl.ds(off, SC_CHUNK)], idx_v)
            pltpu.async_copy(table_hbm.at[idx_v], rows_v, sem).wait()
            pltpu.sync_copy(rows_v, out_hbm.at[pl.ds(off, SC_CHUNK)])

    return gather(table, idx)


def _sc_scatter_rows(rows, idx, n_out, n_slots):
    T, W = rows.shape
    n_workers = SC_CORES * SC_SUBCORES
    per_worker = T // n_workers
    assert per_worker * n_workers == T and per_worker % SC_CHUNK == 0
    mesh = plsc.VectorSubcoreMesh(core_axis_name="c", subcore_axis_name="s",
                                  num_cores=SC_CORES, num_subcores=SC_SUBCORES)

    @functools.partial(
        pl.kernel, mesh=mesh,
        out_type=jax.ShapeDtypeStruct((n_out, W), rows.dtype),
        scratch_types=[pltpu.VMEM((SC_CHUNK,), I32), pltpu.VMEM((SC_CHUNK, W), rows.dtype)],
        name="sc_scatter_rows",
    )
    def scatter(rows_hbm, idx_hbm, out_hbm, idx_v, rows_v):
        wid = lax.axis_index("s") * SC_CORES + lax.axis_index("c")
        base = wid * per_worker

        @pl.loop(0, per_worker // SC_CHUNK)
        def _(c):
            off = pl.multiple_of(base + c * SC_CHUNK, 8)
            pltpu.sync_copy(rows_hbm.at[pl.ds(off, SC_CHUNK)], rows_v)
            for k in range(n_slots):
                pltpu.sync_copy(idx_hbm.at[pl.ds(pl.multiple_of(k * T + off, 8), SC_CHUNK)], idx_v)
                pltpu.sync_copy(rows_v, out_hbm.at[idx_v])

    return scatter(rows, idx)


def _expert_kernel(ie_ref, ib_ref, ins_ref,
                   xs_hbm, wup_ref, bup_ref, wdn_ref, bdn_ref, perm_ref, ys_hbm,
                   xg_ref, acc_ref, yst_ref, wupb_ref, wdnb_ref, gsem, osem, *, n_ff_tiles):
    i = pl.program_id(0)
    j = pl.program_id(1)
    n_items = pl.num_programs(0)
    nsub = ins_ref[i]
    slot = i % 2

    def for_regions(n_blocks, fn):
        first = jnp.int32(0)
        for count in REGION_BLOCKS:
            present = (n_blocks & count) != 0
            pl.when(present)(functools.partial(fn, first, count))
            first = first + jnp.where(present, count, 0)

    def rows_of(first, count):
        return pl.ds(pl.multiple_of(first * ROW_BLOCK, ROW_BLOCK), count * ROW_BLOCK)

    def in_copy(item, dst_slot, first, count):
        src = pl.multiple_of((ib_ref[item] + first) * ROW_BLOCK, ROW_BLOCK)
        return pltpu.make_async_copy(xs_hbm.at[pl.ds(src, count * ROW_BLOCK)],
                                     xg_ref.at[dst_slot, rows_of(first, count)], gsem.at[dst_slot])

    def fetch_item(item, n_blocks, dst_slot):
        for_regions(n_blocks, lambda first, count: in_copy(item, dst_slot, first, count).start())

    @pl.when(j == 0)
    def _():
        @pl.when(i == 0)
        def _():
            fetch_item(0, nsub, 0)

        nxt = jnp.minimum(i + 1, n_items - 1)
        fetch_item(nxt, jnp.where(i + 1 < n_items, ins_ref[nxt], 0), 1 - slot)

        def arrive(first, count):
            in_copy(i, slot, first, count).wait()
            acc_ref[rows_of(first, count), :] = jnp.broadcast_to(
                bdn_ref[0], (count * ROW_BLOCK, acc_ref.shape[1]))

        for_regions(nsub, arrive)

    @pl.when(nsub > 0)
    def _():
        bup = bup_ref[0]
        perm = perm_ref[...]
        half = perm.shape[0] // 2

        def run_blocks(first, count):
            if first == 0:
                wup = wup_ref[0].astype(BF16)
                wdn = wdn_ref[0].astype(BF16)
                wupb_ref[...] = wup
                wdnb_ref[...] = wdn
            else:
                wup = wupb_ref[...]
                wdn = wdnb_ref[...]
            rows = pl.ds(first * ROW_BLOCK, count * ROW_BLOCK)
            xb = _unpack_pairs(xg_ref[slot, rows, :]).astype(BF16)
            gu = (jnp.dot(xb, wup, preferred_element_type=F32) + bup).astype(BF16)
            glu_parts, lin_parts = [], []
            for p in range(gu.shape[1] // perm.shape[0]):
                gp = jnp.dot(gu[:, p * perm.shape[0]:(p + 1) * perm.shape[0]], perm,
                             preferred_element_type=F32)
                glu_parts.append(gp[:, :half])
                lin_parts.append(gp[:, half:])
            x_glu = jnp.minimum(jnp.concatenate(glu_parts, axis=1), SWIGLU_LIMIT)
            x_lin = jnp.clip(jnp.concatenate(lin_parts, axis=1), -SWIGLU_LIMIT, SWIGLU_LIMIT)
            act = x_glu * jax.nn.sigmoid(SWIGLU_ALPHA * x_glu) * (x_lin + 1.0)
            acc_ref[rows, :] += jnp.dot(act.astype(BF16), wdn, preferred_element_type=F32)

        for n_blocks, plan in COMPUTE_PLANS.items():
            for first, count in plan:
                shared = [n for n, p in COMPUTE_PLANS.items() if (first, count) in p]
                if n_blocks == shared[0]:
                    cond = functools.reduce(jnp.logical_or, [nsub == n for n in shared])
                    pl.when(cond)(functools.partial(run_blocks, first, count))

    def out_copy(item, first, count):
        dst = pl.multiple_of((ib_ref[item] + first) * ROW_BLOCK, ROW_BLOCK)
        return pltpu.make_async_copy(yst_ref.at[rows_of(first, count)],
                                     ys_hbm.at[pl.ds(dst, count * ROW_BLOCK)], osem)

    @pl.when(j == n_ff_tiles - 1)
    def _():
        prev = jnp.maximum(i - 1, 0)
        for_regions(jnp.where(i > 0, ins_ref[prev], 0),
                    lambda first, count: out_copy(prev, first, count).wait())

        def leave(first, count):
            rows = rows_of(first, count)
            yst_ref[rows, :] = _pack_pairs(acc_ref[rows, :])
            out_copy(i, first, count).start()

        for_regions(nsub, leave)

        @pl.when(i == n_items - 1)
        def _():
            for_regions(nsub, lambda first, count: out_copy(i, first, count).wait())


def _experts(xs, item_e, item_b, item_n, n_active, w_up, b_up, w_down, b_down):
    n_rows, Dp = xs.shape
    D = 2 * Dp
    E, _, F2 = w_up.shape
    F = F2 // 2
    J = F // FF_TILE
    half = 128
    perm = np.zeros((2 * half, 2 * half), np.float32)
    perm[2 * np.arange(half), np.arange(half)] = 1.0
    perm[2 * np.arange(half) + 1, half + np.arange(half)] = 1.0

    def jj(i, j, ins):
        return jnp.where(ins[i] > 0, j, J - 1)

    grid_spec = pltpu.PrefetchScalarGridSpec(
        num_scalar_prefetch=3,
        grid=(n_active, J),
        in_specs=[pl.BlockSpec(memory_space=pl.ANY),
                  pl.BlockSpec((1, D, 2 * FF_TILE), lambda i, j, ie, ib, ins: (ie[i], 0, jj(i, j, ins))),
                  pl.BlockSpec((1, 1, 2 * FF_TILE), lambda i, j, ie, ib, ins: (ie[i], 0, jj(i, j, ins))),
                  pl.BlockSpec((1, FF_TILE, D), lambda i, j, ie, ib, ins: (ie[i], jj(i, j, ins), 0)),
                  pl.BlockSpec((1, 1, D), lambda i, j, ie, ib, ins: (ie[i], 0, 0)),
                  pl.BlockSpec((2 * half, 2 * half), lambda i, j, ie, ib, ins: (0, 0))],
        out_specs=pl.BlockSpec(memory_space=pl.ANY),
        scratch_shapes=[pltpu.VMEM((2, ITEM_ROWS, Dp), jnp.uint32),
                        pltpu.VMEM((ITEM_ROWS, D), F32),
                        pltpu.VMEM((ITEM_ROWS, Dp), jnp.uint32),
                        pltpu.VMEM((D, 2 * FF_TILE), BF16),
                        pltpu.VMEM((FF_TILE, D), BF16),
                        pltpu.SemaphoreType.DMA((2,)), pltpu.SemaphoreType.DMA(())],
    )
    return pl.pallas_call(
        functools.partial(_expert_kernel, n_ff_tiles=J),
        grid_spec=grid_spec,
        out_shape=jax.ShapeDtypeStruct((n_rows, Dp), jnp.uint32),
        compiler_params=_cparams(("arbitrary", "arbitrary")),
        name="expert_ffn",
    )(item_e, item_b, item_n,
      xs, w_up, b_up.reshape(E, 1, F2), w_down, b_down.reshape(E, 1, D), jnp.asarray(perm, BF16))


def _combine_kernel(*refs):
    yk_refs, (g4_ref, x1_ref, g_ref), o_ref = refs[:TOP_K], refs[TOP_K:TOP_K + 3], refs[-1]
    y = x1_ref[...]
    g4 = g4_ref[...]
    for k in range(TOP_K):
        y = y + g4[:, k:k + 1] * _unpack_pairs(yk_refs[k][...])
    ms = jnp.mean(y * y, axis=-1, keepdims=True)
    o_ref[...] = y * lax.rsqrt(ms + NORM_EPS) * g_ref[...]


def _combine(yk, g4, x1, g_final, part, n_parts, out_so_far, tc=256):
    T, D = x1.shape
    nt = T // n_parts // tc
    first = part * nt
    slot_specs = [pl.BlockSpec((tc, D // 2), functools.partial(lambda i, k: (k * nt + i, 0), k=k))
                  for k in range(TOP_K)]
    operands = [*([yk] * TOP_K), g4, x1, g_final.reshape(1, D)]
    in_specs = slot_specs + [pl.BlockSpec((tc, 128), lambda i: (first + i, 0)),
                             pl.BlockSpec((tc, D), lambda i: (first + i, 0)),
                             pl.BlockSpec((1, D), lambda i: (0, 0))]
    aliases = {}
    if out_so_far is not None:
        aliases = {len(operands): 0}
        operands.append(out_so_far)
        in_specs.append(pl.BlockSpec(memory_space=pl.ANY))
    return pl.pallas_call(
        _combine_kernel,
        grid=(nt,),
        in_specs=in_specs,
        out_specs=pl.BlockSpec((tc, D), lambda i: (first + i, 0)),
        out_shape=jax.ShapeDtypeStruct((T, D), F32),
        input_output_aliases=aliases,
        compiler_params=_cparams(("parallel",)),
        name="combine_norm",
    )(*operands)


def _work_items(nblk, start_blk, n_items):
    E = nblk.shape[0]
    per_e = (nblk + ITEM_BLOCKS - 1) // ITEM_BLOCKS
    ends = jnp.cumsum(per_e)
    total = ends[-1]
    idx = jnp.arange(n_items, dtype=I32)
    e = jnp.minimum(jnp.searchsorted(ends, idx, side="right"), E - 1).astype(I32)
    local = idx - (ends[e] - per_e[e])
    active = idx < total
    last_e = e[jnp.maximum(total - 1, 0)]
    item_e = jnp.where(active, e, last_e).astype(I32)
    item_b = jnp.where(active, start_blk[e] + local * ITEM_BLOCKS, 0).astype(I32)
    item_n = jnp.where(active, jnp.clip(nblk[e] - local * ITEM_BLOCKS, 0, ITEM_BLOCKS), 0).astype(I32)
    return item_e, item_b, item_n, jnp.maximum(total, 1).astype(I32)


def kernel(x, g_mix, w_in, b_forget, g_v_ln, b_v_ln, w_spatial, b_spatial, w_branch_attn, w_branch_gmlp, w_out, g_ffn, w_router, b_router, w_expert_up, b_expert_up, w_expert_down, b_expert_down, g_final):
    B, S, D = x.shape
    T = B * S
    n_heads = b_forget.shape[0]
    attn_w = n_heads * HEAD_DIM
    gmlp_w = g_v_ln.shape[0]
    E = w_router.shape[1]
    off_f = 3 * attn_w
    off_z = off_f + n_heads
    off_g = off_z + 2 * gmlp_w

    x2 = x.reshape(T, D)
    w_in_t = w_in.T
    wzg = _rows_bf16(w_in_t, off_z, w_in.shape[1] - off_z)
    wa = w_branch_attn.astype(BF16)
    wb = w_branch_gmlp.astype(BF16)
    wo = w_out.astype(BF16)

    qkv, h = _norm_project(x2, g_mix, w_in_t, 3 * attn_w, attn_w, LOG2E / math.sqrt(HEAD_DIM))
    c_row = _forget_cumsum(h, w_in_t, off_f, b_forget, B, S)
    attn = _attention(qkv, c_row, B, S, n_heads)
    sg = _gmlp(h, wzg, g_v_ln, b_v_ln, w_spatial, b_spatial)
    merged = _merge(attn, sg, h, wa, wb, wzg, off_g - off_z)
    x1, h2p, sel, gate = _out_router(merged, x2, wo, g_ffn, w_router, b_router)

    dest4, g4, nblk, start_blk = _routing(sel, gate)
    n_rows = T * TOP_K + E * ROW_BLOCK
    dest_slots = dest4[:, :TOP_K].T.reshape(TOP_K * T)
    max_blocks = n_rows // ROW_BLOCK
    n_items = E + -(-(max_blocks - E) // ITEM_BLOCKS) + 1
    item_e, item_b, item_n, n_active = _work_items(nblk[0], start_blk[0], n_items)
    xs = _sc_scatter_rows(h2p, dest_slots, n_rows, TOP_K)
    ys = _experts(xs, item_e, item_b, item_n, n_active,
                  w_expert_up, b_expert_up, w_expert_down, b_expert_down)
    out = None
    t_part = T // COMBINE_PARTS
    for p in range(COMBINE_PARTS):
        idx = dest4[p * t_part:(p + 1) * t_part, :TOP_K].T.reshape(TOP_K * t_part)
        yk = _sc_gather_rows(ys, idx)
        out = _combine(yk, g4, x1, g_final, p, COMBINE_PARTS, out)
    return out.reshape(B, S, D)
```

```python
import functools
import math

import jax
import jax.numpy as jnp
import numpy as np
from jax import lax
from jax.experimental import pallas as pl
from jax.experimental.pallas import tpu as pltpu
from jax.experimental.pallas import tpu_sc as plsc

F32 = jnp.float32
BF16 = jnp.bfloat16
I32 = jnp.int32

NORM_EPS = 1e-5
LANES = 128
SUBLANES = 8
NT_DIMS = (((1,), (1,)), ((), ()))
HEAD_DIM = 128
CHUNK = 128
GROUP_DIM = 128
TOP_K = 4
SWIGLU_ALPHA = 1.702
SWIGLU_LIMIT = 7.0
LOG2E = math.log2(math.e)

VMEM_LIMIT_BYTES = 56 * 1024 * 1024

ROW_BLOCK = 128
ITEM_BLOCKS = 10
ITEM_ROWS = ITEM_BLOCKS * ROW_BLOCK
REGION_BLOCKS = (8, 4, 2, 1)
assert sum(REGION_BLOCKS) >= ITEM_BLOCKS
COMPUTE_PLANS = {10: ((0, 10),), 9: ((0, 9),), 8: ((0, 8),),
                 7: ((0, 4), (4, 2), (6, 1)), 6: ((0, 4), (4, 2)), 5: ((0, 4), (4, 1)), 4: ((0, 4),),
                 3: ((0, 2), (2, 1)), 2: ((0, 2),), 1: ((0, 1),)}
assert set(COMPUTE_PLANS) == set(range(1, ITEM_BLOCKS + 1))
FF_TILE = 256
COMBINE_PARTS = 1


def _cparams(sem, **kw):
    return pltpu.CompilerParams(dimension_semantics=sem, vmem_limit_bytes=VMEM_LIMIT_BYTES, **kw)


def _pack_pairs(x):
    c = x.shape[1] // 2
    hi = lax.bitcast_convert_type(x[:, :c].astype(BF16).astype(F32), jnp.uint32)
    lo = lax.bitcast_convert_type(x[:, c:].astype(BF16).astype(F32), jnp.uint32)
    return hi | (lo >> 16)


def _unpack_pairs(w):
    hi = lax.bitcast_convert_type(w & jnp.uint32(0xFFFF0000), F32)
    lo = lax.bitcast_convert_type(w << 16, F32)
    return jnp.concatenate([hi, lo], axis=1)


def _norm_proj_kernel(x_ref, g_ref, w_ref, o_ref, h_ref, *, n_scaled, scale):
    j = pl.program_id(1)

    @pl.when(j == 0)
    def _():
        x = x_ref[...]
        ms = jnp.mean(x * x, axis=-1, keepdims=True)
        h_ref[...] = (x * lax.rsqrt(ms + NORM_EPS) * g_ref[...]).astype(h_ref.dtype)

    acc = lax.dot_general(h_ref[...], w_ref[...].astype(BF16), NT_DIMS, preferred_element_type=F32)
    o_ref[...] = (acc * jnp.where(j < n_scaled, scale, 1.0)).astype(o_ref.dtype)


def _norm_project(x, g, wt, n_cols, n_scaled_cols, scale, tm=1024, tn=512):
    T, D = x.shape
    return pl.pallas_call(
        functools.partial(_norm_proj_kernel, n_scaled=n_scaled_cols // tn, scale=scale),
        grid=(T // tm, n_cols // tn),
        in_specs=[pl.BlockSpec((tm, D), lambda i, j: (i, 0)),
                  pl.BlockSpec((1, D), lambda i, j: (0, 0)),
                  pl.BlockSpec((tn, D), lambda i, j: (j, 0))],
        out_specs=[pl.BlockSpec((tm, tn), lambda i, j: (i, j)),
                   pl.BlockSpec((tm, D), lambda i, j: (i, 0))],
        out_shape=[jax.ShapeDtypeStruct((T, n_cols), BF16), jax.ShapeDtypeStruct((T, D), BF16)],
        compiler_params=_cparams(("arbitrary", "arbitrary")),
        name="norm_qkv_proj",
    )(x, g.reshape(1, D), wt)


def _cast_kernel(w_ref, o_ref):
    o_ref[...] = w_ref[...].astype(o_ref.dtype)


def _rows_bf16(wt, start, n_rows, tr=512):
    D = wt.shape[1]
    assert start % SUBLANES == 0 and n_rows % tr == 0
    return pl.pallas_call(
        _cast_kernel,
        grid=(n_rows // tr,),
        in_specs=[pl.BlockSpec((pl.Element(tr), pl.Element(D)),
                               lambda i: (pl.multiple_of(start + i * tr, SUBLANES), 0))],
        out_specs=pl.BlockSpec((tr, D), lambda i: (i, 0)),
        out_shape=jax.ShapeDtypeStruct((n_rows, D), BF16),
        compiler_params=_cparams(("parallel",)),
        name="rows_bf16",
    )(wt)


def _forget_kernel(h_ref, wft_ref, bf_ref, c_ref):
    ft = lax.dot_general(wft_ref[...].astype(BF16), h_ref[...], NT_DIMS,
                         preferred_element_type=F32)
    c = jax.nn.log_sigmoid(ft + bf_ref[...])
    S = c.shape[1]
    lane = lax.broadcasted_iota(I32, c.shape, 1)
    shift = 1
    while shift < S:
        c = c + jnp.where(lane >= shift, pltpu.roll(c, shift, axis=1), 0.0)
        shift *= 2
    c_ref[0] = c * LOG2E


def _forget_cumsum(h, wt, f_off, b_forget, B, S):
    T, D = h.shape
    H = b_forget.shape[0]
    assert f_off % H == 0 and H % SUBLANES == 0
    return pl.pallas_call(
        _forget_kernel,
        grid=(B,),
        in_specs=[pl.BlockSpec((S, D), lambda b: (b, 0)),
                  pl.BlockSpec((H, D), lambda b: (f_off // H, 0)),
                  pl.BlockSpec((H, 1), lambda b: (0, 0))],
        out_specs=pl.BlockSpec((1, H, S), lambda b: (b, 0, 0)),
        out_shape=jax.ShapeDtypeStruct((B, H, S), F32),
        compiler_params=_cparams(("parallel",)),
        name="forget_cumsum",
    )(h, wt, b_forget.reshape(H, 1))


def _attn_kernel(q_ref, k_ref, v_ref, crow_ref, o_ref, vaug_ref, m_ref, acc_ref, *, n_heads, tq):
    i = pl.program_id(1)

    @pl.when(i == 0)
    def _():
        ones = jnp.ones((v_ref.shape[0], HEAD_DIM), BF16)
        for h in range(n_heads):
            vaug_ref[h, :, :HEAD_DIM] = v_ref[:, h * HEAD_DIM:(h + 1) * HEAD_DIM]
            vaug_ref[h, :, HEAD_DIM:] = ones

    m_ref[...] = jnp.full(m_ref.shape, -jnp.inf, F32)
    acc_ref[...] = jnp.zeros(acc_ref.shape, F32)
    row = lax.broadcasted_iota(I32, (tq, tq), 0)
    col = lax.broadcasted_iota(I32, (tq, tq), 1)
    causal = col <= row

    def step(j, masked):
        keys = pl.ds(pl.multiple_of(j * tq, tq), tq)
        for h in range(n_heads):
            hs = slice(h * HEAD_DIM, (h + 1) * HEAD_DIM)
            s = lax.dot_general(q_ref[:, hs], k_ref[keys, hs], (((1,), (1,)), ((), ())),
                                preferred_element_type=F32) - crow_ref[0, h, j]
            if masked:
                s = jnp.where(causal, s, -jnp.inf)
            m_old = m_ref[h]
            m_new = jnp.maximum(m_old, jnp.max(s, axis=-1, keepdims=True))
            alpha = jnp.exp2(m_old - m_new)
            p = jnp.exp2(s - jnp.concatenate([m_new] * (tq // HEAD_DIM), axis=1))
            m_ref[h] = m_new
            pv = jnp.dot(p.astype(BF16), vaug_ref[h, keys, :], preferred_element_type=F32)
            acc_ref[h] = jnp.concatenate([alpha, alpha], axis=1) * acc_ref[h] + pv

    def body(j, _):
        step(j, False)
        return 0

    lax.fori_loop(0, i, body, 0)
    step(i, True)
    for h in range(n_heads):
        acc = acc_ref[h]
        o_ref[:, h * HEAD_DIM:(h + 1) * HEAD_DIM] = (acc[:, :HEAD_DIM] / acc[:, HEAD_DIM:]).astype(o_ref.dtype)


def _attention(qkv, c_row, B, S, n_heads, tq=256):
    T = qkv.shape[0]
    W = n_heads * HEAD_DIM
    nq = S // tq
    c_row5 = c_row.reshape(B, n_heads, nq, 1, tq)
    return pl.pallas_call(
        functools.partial(_attn_kernel, n_heads=n_heads, tq=tq),
        grid=(B, nq),
        in_specs=[pl.BlockSpec((tq, W), lambda b, i: (b * nq + i, 0)),
                  pl.BlockSpec((S, W), lambda b, i: (b, 1)),
                  pl.BlockSpec((S, W), lambda b, i: (b, 2)),
                  pl.BlockSpec((1, n_heads, nq, 1, tq), lambda b, i: (b, 0, 0, 0, 0))],
        out_specs=pl.BlockSpec((tq, W), lambda b, i: (b * nq + i, 0)),
        out_shape=jax.ShapeDtypeStruct((T, W), BF16),
        scratch_shapes=[pltpu.VMEM((n_heads, S, 2 * HEAD_DIM), BF16),
                        pltpu.VMEM((n_heads, tq, HEAD_DIM), F32),
                        pltpu.VMEM((n_heads, tq, 2 * HEAD_DIM), F32)],
        compiler_params=_cparams(("arbitrary", "arbitrary")),
        name="fox_attention",
    )(qkv, qkv, qkv, c_row5)


def _gmlp_kernel(h_ref, wz_ref, g_ref, b_ref, ws_ref, bst_ref, o_ref, *, n_groups):
    z = lax.dot_general(h_ref[...], wz_ref[...], NT_DIMS, preferred_element_type=F32)
    z = 0.5 * z * (1.0 + lax.erf(z * (1.0 / math.sqrt(2.0))))
    W = z.shape[1] // 2
    u = z[:, :W]
    v = z[:, W:]
    mu = jnp.mean(v, axis=-1, keepdims=True)
    var = jnp.mean(jnp.square(v - mu), axis=-1, keepdims=True)
    vn = (v - mu) * lax.rsqrt(var + NORM_EPS) * g_ref[...] + b_ref[...]
    row = lax.broadcasted_iota(I32, (CHUNK, CHUNK), 0)
    col = lax.broadcasted_iota(I32, (CHUNK, CHUNK), 1)
    tril = col <= row
    tg = z.shape[0]
    for g in range(n_groups):
        gs = slice(g * GROUP_DIM, (g + 1) * GROUP_DIM)
        wg = jnp.where(tril, ws_ref[g], 0.0).astype(BF16)
        bias = bst_ref[:, g:g + 1]
        for c in range(tg // CHUNK):
            cs = slice(c * CHUNK, (c + 1) * CHUNK)
            mixed = jnp.dot(wg, vn[cs, gs].astype(BF16), preferred_element_type=F32) + bias
            o_ref[cs, gs] = (u[cs, gs] * mixed).astype(o_ref.dtype)


def _gmlp(h, wz, g_v_ln, b_v_ln, w_spatial, b_spatial, tg=512):
    T, D = h.shape
    W = g_v_ln.shape[0]
    W2 = 2 * W
    G = w_spatial.shape[0]
    return pl.pallas_call(
        functools.partial(_gmlp_kernel, n_groups=G),
        grid=(T // tg,),
        in_specs=[pl.BlockSpec((tg, D), lambda i: (i, 0)),
                  pl.BlockSpec((W2, D), lambda i: (0, 0)),
                  pl.BlockSpec((1, W), lambda i: (0, 0)),
                  pl.BlockSpec((1, W), lambda i: (0, 0)),
                  pl.BlockSpec((G, CHUNK, CHUNK), lambda i: (0, 0, 0)),
                  pl.BlockSpec((CHUNK, G), lambda i: (0, 0))],
        out_specs=pl.BlockSpec((tg, W), lambda i: (i, 0)),
        out_shape=jax.ShapeDtypeStruct((T, W), BF16),
        compiler_params=_cparams(("parallel",)),
        name="gmlp",
    )(h, wz, g_v_ln.reshape(1, W), b_v_ln.reshape(1, W), w_spatial, b_spatial.T)


def _merge_kernel(attn_ref, sg_ref, h_ref, wa_ref, wb_ref, wga_ref, wgb_ref, o_ref):
    h = h_ref[...]
    a = jnp.dot(attn_ref[...], wa_ref[...], preferred_element_type=F32)
    ga = lax.dot_general(h, wga_ref[...], NT_DIMS, preferred_element_type=F32)
    m = jax.nn.sigmoid(ga) * a
    b = jnp.dot(sg_ref[...], wb_ref[...], preferred_element_type=F32)
    gb = lax.dot_general(h, wgb_ref[...], NT_DIMS, preferred_element_type=F32)
    o_ref[...] = (m + jax.nn.sigmoid(gb) * b).astype(o_ref.dtype)


def _merge(attn, sg, h, wa, wb, wg, g_off, tm=512, tn=512):
    T, D = h.shape
    Wa = attn.shape[1]
    Wb = sg.shape[1]
    nt = D // tn
    g0 = g_off // tn
    return pl.pallas_call(
        _merge_kernel,
        grid=(nt, T // tm),
        in_specs=[pl.BlockSpec((tm, Wa), lambda j, i: (i, 0)),
                  pl.BlockSpec((tm, Wb), lambda j, i: (i, 0)),
                  pl.BlockSpec((tm, D), lambda j, i: (i, 0)),
                  pl.BlockSpec((Wa, tn), lambda j, i: (0, j)),
                  pl.BlockSpec((Wb, tn), lambda j, i: (0, j)),
                  pl.BlockSpec((tn, D), lambda j, i: (g0 + j, 0)),
                  pl.BlockSpec((tn, D), lambda j, i: (g0 + nt + j, 0))],
        out_specs=pl.BlockSpec((tm, tn), lambda j, i: (i, j)),
        out_shape=jax.ShapeDtypeStruct((T, D), BF16),
        compiler_params=_cparams(("arbitrary", "arbitrary")),
        name="gated_merge",
    )(attn, sg, h, wa, wb, wg, wg)


def _out_router_kernel(m_ref, x_ref, wo_ref, g_ref, wr_ref, br_ref,
                       x1_ref, h2_ref, sel_ref, gate_ref, *, n_parts):
    E = br_ref.shape[1]
    part = m_ref.shape[0] // n_parts
    for p in range(n_parts):
        rows = pl.ds(p * part, part)
        x1 = x_ref[rows, :] + jnp.dot(m_ref[rows, :], wo_ref[...], preferred_element_type=F32)
        x1_ref[rows, :] = x1
        ms = jnp.mean(x1 * x1, axis=-1, keepdims=True)
        h2 = x1 * lax.rsqrt(ms + NORM_EPS) * g_ref[...]
        h2_ref[rows, :] = _pack_pairs(h2)
        h2_hi = h2.astype(BF16)
        h2_lo = (h2 - h2_hi.astype(F32)).astype(BF16)
        pa = jnp.dot(h2_hi, wr_ref[...], preferred_element_type=F32)
        pb = jnp.dot(h2_lo, wr_ref[:, :E], preferred_element_type=F32)
        logits = pa[:, :E] + (pa[:, E:] + pb) + br_ref[...]
        lane = lax.broadcasted_iota(I32, logits.shape, 1)
        work = logits
        sel = jnp.zeros(logits.shape, F32)
        num = jnp.zeros(logits.shape, F32)
        denom = jnp.zeros((part, 1), F32)
        m0 = None
        for _ in range(TOP_K):
            m = jnp.max(work, axis=-1, keepdims=True)
            idx = jnp.min(jnp.where(work == m, lane, E), axis=-1, keepdims=True)
            onehot = lane == idx
            if m0 is None:
                m0 = m
            e = jnp.exp(m - m0)
            sel = jnp.where(onehot, 1.0, sel)
            num = jnp.where(onehot, e, num)
            denom = denom + e
            work = jnp.where(onehot, -jnp.inf, work)
        sel_ref[rows, :] = sel
        gate_ref[rows, :] = num / denom


def _out_router(merged, x, wo, g_ffn, w_router, b_router, to=512, n_parts=2):
    T, D = x.shape
    E = w_router.shape[1]
    w_hi = w_router.astype(BF16)
    w_lo = (w_router - w_hi.astype(F32)).astype(BF16)
    w_router = jnp.concatenate([w_hi, w_lo], axis=1)
    row = lambda i: (i, 0)
    fixed = lambda i: (0, 0)
    return pl.pallas_call(
        functools.partial(_out_router_kernel, n_parts=n_parts),
        grid=(T // to,),
        in_specs=[pl.BlockSpec((to, D), row), pl.BlockSpec((to, D), row),
                  pl.BlockSpec((D, D), fixed), pl.BlockSpec((1, D), fixed),
                  pl.BlockSpec((D, 2 * E), fixed), pl.BlockSpec((1, E), fixed)],
        out_specs=[pl.BlockSpec((to, D), row), pl.BlockSpec((to, D // 2), row),
                   pl.BlockSpec((to, E), row), pl.BlockSpec((to, E), row)],
        out_shape=[jax.ShapeDtypeStruct((T, D), F32), jax.ShapeDtypeStruct((T, D // 2), jnp.uint32),
                   jax.ShapeDtypeStruct((T, E), F32), jax.ShapeDtypeStruct((T, E), F32)],
        compiler_params=_cparams(("parallel",)),
        name="out_router",
    )(merged, x, wo, g_ffn.reshape(1, D), w_router, b_router.reshape(1, E))


def _routing_kernel(sel_ref, gate_ref, dest_ref, g4_ref, nblk_ref, start_ref, rank_ref, *, tile):
    T, E = sel_ref.shape
    nt = T // tile
    r = lax.broadcasted_iota(I32, (tile, tile), 0)
    c = lax.broadcasted_iota(I32, (tile, tile), 1)
    strict_lower = (c < r).astype(BF16)
    er = lax.broadcasted_iota(I32, (E, E), 0)
    ec = lax.broadcasted_iota(I32, (E, E), 1)
    strict_upper = (er < ec).astype(BF16)

    def pass1(t, carry):
        rows = pl.ds(pl.multiple_of(t * tile, tile), tile)
        a = sel_ref[rows, :]
        rank_ref[rows, :] = jnp.dot(strict_lower, a.astype(BF16), preferred_element_type=F32) + carry
        return carry + jnp.sum(a, axis=0, keepdims=True)

    counts = lax.fori_loop(0, nt, pass1, jnp.zeros((1, E), F32))
    nblk = jnp.floor((counts + (ROW_BLOCK - 1)) * (1.0 / ROW_BLOCK))
    start_blk = jnp.dot(nblk.astype(BF16), strict_upper, preferred_element_type=F32)
    nblk_ref[...] = nblk.astype(I32)
    start_ref[...] = start_blk.astype(I32)
    start_row = start_blk * float(ROW_BLOCK)
    lane = lax.broadcasted_iota(I32, (tile, 128), 1)

    def pass2(t, _):
        rows = pl.ds(pl.multiple_of(t * tile, tile), tile)
        a = sel_ref[rows, :]
        g = gate_ref[rows, :]
        dest_e = rank_ref[rows, :] + start_row
        slot = jnp.dot(a.astype(BF16), strict_upper, preferred_element_type=F32)
        d4 = jnp.zeros((tile, 128), F32)
        g4 = jnp.zeros((tile, 128), F32)
        for s in range(TOP_K):
            pick = (a > 0.5) & (slot == float(s))
            d4 = jnp.where(lane == s, jnp.sum(jnp.where(pick, dest_e, 0.0), axis=-1, keepdims=True), d4)
            g4 = jnp.where(lane == s, jnp.sum(jnp.where(pick, g, 0.0), axis=-1, keepdims=True), g4)
        dest_ref[rows, :] = d4.astype(I32)
        g4_ref[rows, :] = g4
        return 0

    lax.fori_loop(0, nt, pass2, 0)


def _routing(sel, gate, tile=256):
    T, E = sel.shape
    return pl.pallas_call(
        functools.partial(_routing_kernel, tile=tile),
        out_shape=[jax.ShapeDtypeStruct((T, 128), I32), jax.ShapeDtypeStruct((T, 128), F32),
                   jax.ShapeDtypeStruct((1, E), I32), jax.ShapeDtypeStruct((1, E), I32)],
        scratch_shapes=[pltpu.VMEM((T, E), F32)],
        compiler_params=pltpu.CompilerParams(vmem_limit_bytes=VMEM_LIMIT_BYTES),
        name="routing_ranks",
    )(sel, gate)


SC_CORES = 2
SC_SUBCORES = 16
SC_CHUNK = 64


def _sc_gather_rows(table, idx):
    n = idx.shape[0]
    W = table.shape[1]
    n_workers = SC_CORES * SC_SUBCORES
    per_worker = n // n_workers
    assert per_worker * n_workers == n and per_worker % SC_CHUNK == 0
    mesh = plsc.VectorSubcoreMesh(core_axis_name="c", subcore_axis_name="s",
                                  num_cores=SC_CORES, num_subcores=SC_SUBCORES)

    @functools.partial(
        pl.kernel, mesh=mesh,
        out_type=jax.ShapeDtypeStruct((n, W), table.dtype),
        scratch_types=[pltpu.VMEM((SC_CHUNK,), I32), pltpu.VMEM((SC_CHUNK, W), table.dtype),
                       pltpu.SemaphoreType.DMA],
        name="sc_gather_rows",
    )
    def gather(table_hbm, idx_hbm, out_hbm, idx_v, rows_v, sem):
        wid = lax.axis_index("s") * SC_CORES + lax.axis_index("c")
        base = wid * per_worker

        @pl.loop(0, per_worker // SC_CHUNK)
        def _(c):
            off = pl.multiple_of(base + c * SC_CHUNK, 8)
            pltpu.sync_copy(idx_hbm.at[pl.ds(off, SC_CHUNK)], idx_v)
            pltpu.async_copy(table_hbm.at[idx_v], rows_v, sem).wait()
            pltpu.sync_copy(rows_v, out_hbm.at[pl.ds(off, SC_CHUNK)])

    return gather(table, idx)


def _sc_scatter_rows(rows, idx, n_out, n_slots):
    T, W = rows.shape
    n_workers = SC_CORES * SC_SUBCORES
    per_worker = T // n_workers
    assert per_worker * n_workers == T and per_worker % SC_CHUNK == 0
    mesh = plsc.VectorSubcoreMesh(core_axis_name="c", subcore_axis_name="s",
                                  num_cores=SC_CORES, num_subcores=SC_SUBCORES)

    @functools.partial(
        pl.kernel, mesh=mesh,
        out_type=jax.ShapeDtypeStruct((n_out, W), rows.dtype),
        scratch_types=[pltpu.VMEM((SC_CHUNK,), I32), pltpu.VMEM((SC_CHUNK, W), rows.dtype)],
        name="sc_scatter_rows",
    )
    def scatter(rows_hbm, idx_hbm, out_hbm, idx_v, rows_v):
        wid = lax.axis_index("s") * SC_CORES + lax.axis_index("c")
        base = wid * per_worker

        @pl.loop(0, per_worker // SC_CHUNK)
        def _(c):
            off = pl.multiple_of(base + c * SC_CHUNK, 8)
            pltpu.sync_copy(rows_hbm.at[pl.ds(off, SC_CHUNK)], rows_v)
            for k in range(n_slots):
                pltpu.sync_copy(idx_hbm.at[pl.ds(pl.multiple_of(k * T + off, 8), SC_CHUNK)], idx_v)
                pltpu.sync_copy(rows_v, out_hbm.at[idx_v])

    return scatter(rows, idx)


def _expert_kernel(ie_ref, ib_ref, ins_ref,
                   xs_hbm, wup_ref, bup_ref, wdn_ref, bdn_ref, perm_ref, ys_hbm,
                   xg_ref, acc_ref, yst_ref, wupb_ref, wdnb_ref, gsem, osem, *, n_ff_tiles):
    i = pl.program_id(0)
    j = pl.program_id(1)
    n_items = pl.num_programs(0)
    nsub = ins_ref[i]
    slot = i % 2

    def for_regions(n_blocks, fn):
        first = jnp.int32(0)
        for count in REGION_BLOCKS:
            present = (n_blocks & count) != 0
            pl.when(present)(functools.partial(fn, first, count))
            first = first + jnp.where(present, count, 0)

    def rows_of(first, count):
        return pl.ds(pl.multiple_of(first * ROW_BLOCK, ROW_BLOCK), count * ROW_BLOCK)

    def in_copy(item, dst_slot, first, count):
        src = pl.multiple_of((ib_ref[item] + first) * ROW_BLOCK, ROW_BLOCK)
        return pltpu.make_async_copy(xs_hbm.at[pl.ds(src, count * ROW_BLOCK)],
                                     xg_ref.at[dst_slot, rows_of(first, count)], gsem.at[dst_slot])

    def fetch_item(item, n_blocks, dst_slot):
        for_regions(n_blocks, lambda first, count: in_copy(item, dst_slot, first, count).start())

    @pl.when(j == 0)
    def _():
        @pl.when(i == 0)
        def _():
            fetch_item(0, nsub, 0)

        nxt = jnp.minimum(i + 1, n_items - 1)
        fetch_item(nxt, jnp.where(i + 1 < n_items, ins_ref[nxt], 0), 1 - slot)

        def arrive(first, count):
            in_copy(i, slot, first, count).wait()
            acc_ref[rows_of(first, count), :] = jnp.broadcast_to(
                bdn_ref[0], (count * ROW_BLOCK, acc_ref.shape[1]))

        for_regions(nsub, arrive)

    @pl.when(nsub > 0)
    def _():
        bup = bup_ref[0]
        perm = perm_ref[...]
        half = perm.shape[0] // 2

        def run_blocks(first, count):
            if first == 0:
                wup = wup_ref[0].astype(BF16)
                wdn = wdn_ref[0].astype(BF16)
                wupb_ref[...] = wup
                wdnb_ref[...] = wdn
            else:
                wup = wupb_ref[...]
                wdn = wdnb_ref[...]
            rows = pl.ds(first * ROW_BLOCK, count * ROW_BLOCK)
            xb = _unpack_pairs(xg_ref[slot, rows, :]).astype(BF16)
            gu = (jnp.dot(xb, wup, preferred_element_type=F32) + bup).astype(BF16)
            glu_parts, lin_parts = [], []
            for p in range(gu.shape[1] // perm.shape[0]):
                gp = jnp.dot(gu[:, p * perm.shape[0]:(p + 1) * perm.shape[0]], perm,
                             preferred_element_type=F32)
                glu_parts.append(gp[:, :half])
                lin_parts.append(gp[:, half:])
            x_glu = jnp.minimum(jnp.concatenate(glu_parts, axis=1), SWIGLU_LIMIT)
            x_lin = jnp.clip(jnp.concatenate(lin_parts, axis=1), -SWIGLU_LIMIT, SWIGLU_LIMIT)
            act = x_glu * jax.nn.sigmoid(SWIGLU_ALPHA * x_glu) * (x_lin + 1.0)
            acc_ref[rows, :] += jnp.dot(act.astype(BF16), wdn, preferred_element_type=F32)

        for n_blocks, plan in COMPUTE_PLANS.items():
            for first, count in plan:
                shared = [n for n, p in COMPUTE_PLANS.items() if (first, count) in p]
                if n_blocks == shared[0]:
                    cond = functools.reduce(jnp.logical_or, [nsub == n for n in shared])
                    pl.when(cond)(functools.partial(run_blocks, first, count))

    def out_copy(item, first, count):
        dst = pl.multiple_of((ib_ref[item] + first) * ROW_BLOCK, ROW_BLOCK)
        return pltpu.make_async_copy(yst_ref.at[rows_of(first, count)],
                                     ys_hbm.at[pl.ds(dst, count * ROW_BLOCK)], osem)

    @pl.when(j == n_ff_tiles - 1)
    def _():
        prev = jnp.maximum(i - 1, 0)
        for_regions(jnp.where(i > 0, ins_ref[prev], 0),
                    lambda first, count: out_copy(prev, first, count).wait())

        def leave(first, count):
            rows = rows_of(first, count)
            yst_ref[rows, :] = _pack_pairs(acc_ref[rows, :])
            out_copy(i, first, count).start()

        for_regions(nsub, leave)

        @pl.when(i == n_items - 1)
        def _():
            for_regions(nsub, lambda first, count: out_copy(i, first, count).wait())


def _experts(xs, item_e, item_b, item_n, n_active, w_up, b_up, w_down, b_down):
    n_rows, Dp = xs.shape
    D = 2 * Dp
    E, _, F2 = w_up.shape
    F = F2 // 2
    J = F // FF_TILE
    half = 128
    perm = np.zeros((2 * half, 2 * half), np.float32)
    perm[2 * np.arange(half), np.arange(half)] = 1.0
    perm[2 * np.arange(half) + 1, half + np.arange(half)] = 1.0

    def jj(i, j, ins):
        return jnp.where(ins[i] > 0, j, J - 1)

    grid_spec = pltpu.PrefetchScalarGridSpec(
        num_scalar_prefetch=3,
        grid=(n_active, J),
        in_specs=[pl.BlockSpec(memory_space=pl.ANY),
                  pl.BlockSpec((1, D, 2 * FF_TILE), lambda i, j, ie, ib, ins: (ie[i], 0, jj(i, j, ins))),
                  pl.BlockSpec((1, 1, 2 * FF_TILE), lambda i, j, ie, ib, ins: (ie[i], 0, jj(i, j, ins))),
                  pl.BlockSpec((1, FF_TILE, D), lambda i, j, ie, ib, ins: (ie[i], jj(i, j, ins), 0)),
                  pl.BlockSpec((1, 1, D), lambda i, j, ie, ib, ins: (ie[i], 0, 0)),
                  pl.BlockSpec((2 * half, 2 * half), lambda i, j, ie, ib, ins: (0, 0))],
        out_specs=pl.BlockSpec(memory_space=pl.ANY),
        scratch_shapes=[pltpu.VMEM((2, ITEM_ROWS, Dp), jnp.uint32),
                        pltpu.VMEM((ITEM_ROWS, D), F32),
                        pltpu.VMEM((ITEM_ROWS, Dp), jnp.uint32),
                        pltpu.VMEM((D, 2 * FF_TILE), BF16),
                        pltpu.VMEM((FF_TILE, D), BF16),
                        pltpu.SemaphoreType.DMA((2,)), pltpu.SemaphoreType.DMA(())],
    )
    return pl.pallas_call(
        functools.partial(_expert_kernel, n_ff_tiles=J),
        grid_spec=grid_spec,
        out_shape=jax.ShapeDtypeStruct((n_rows, Dp), jnp.uint32),
        compiler_params=_cparams(("arbitrary", "arbitrary")),
        name="expert_ffn",
    )(item_e, item_b, item_n,
      xs, w_up, b_up.reshape(E, 1, F2), w_down, b_down.reshape(E, 1, D), jnp.asarray(perm, BF16))


def _combine_kernel(*refs):
    yk_refs, (g4_ref, x1_ref, g_ref), o_ref = refs[:TOP_K], refs[TOP_K:TOP_K + 3], refs[-1]
    y = x1_ref[...]
    g4 = g4_ref[...]
    for k in range(TOP_K):
        y = y + g4[:, k:k + 1] * _unpack_pairs(yk_refs[k][...])
    ms = jnp.mean(y * y, axis=-1, keepdims=True)
    o_ref[...] = y * lax.rsqrt(ms + NORM_EPS) * g_ref[...]


def _combine(yk, g4, x1, g_final, part, n_parts, out_so_far, tc=256):
    T, D = x1.shape
    nt = T // n_parts // tc
    first = part * nt
    slot_specs = [pl.BlockSpec((tc, D // 2), functools.partial(lambda i, k: (k * nt + i, 0), k=k))
                  for k in range(TOP_K)]
    operands = [*([yk] * TOP_K), g4, x1, g_final.reshape(1, D)]
    in_specs = slot_specs + [pl.BlockSpec((tc, 128), lambda i: (first + i, 0)),
                             pl.BlockSpec((tc, D), lambda i: (first + i, 0)),
                             pl.BlockSpec((1, D), lambda i: (0, 0))]
    aliases = {}
    if out_so_far is not None:
        aliases = {len(operands): 0}
        operands.append(out_so_far)
        in_specs.append(pl.BlockSpec(memory_space=pl.ANY))
    return pl.pallas_call(
        _combine_kernel,
        grid=(nt,),
        in_specs=in_specs,
        out_specs=pl.BlockSpec((tc, D), lambda i: (first + i, 0)),
        out_shape=jax.ShapeDtypeStruct((T, D), F32),
        input_output_aliases=aliases,
        compiler_params=_cparams(("parallel",)),
        name="combine_norm",
    )(*operands)


def _work_items(nblk, start_blk, n_items):
    E = nblk.shape[0]
    per_e = (nblk + ITEM_BLOCKS - 1) // ITEM_BLOCKS
    ends = jnp.cumsum(per_e)
    total = ends[-1]
    idx = jnp.arange(n_items, dtype=I32)
    e = jnp.minimum(jnp.searchsorted(ends, idx, side="right"), E - 1).astype(I32)
    local = idx - (ends[e] - per_e[e])
    active = idx < total
    last_e = e[jnp.maximum(total - 1, 0)]
    item_e = jnp.where(active, e, last_e).astype(I32)
    item_b = jnp.where(active, start_blk[e] + local * ITEM_BLOCKS, 0).astype(I32)
    item_n = jnp.where(active, jnp.clip(nblk[e] - local * ITEM_BLOCKS, 0, ITEM_BLOCKS), 0).astype(I32)
    return item_e, item_b, item_n, jnp.maximum(total, 1).astype(I32)


def kernel(x, g_mix, w_in, b_forget, g_v_ln, b_v_ln, w_spatial, b_spatial, w_branch_attn, w_branch_gmlp, w_out, g_ffn, w_router, b_router, w_expert_up, b_expert_up, w_expert_down, b_expert_down, g_final):
    B, S, D = x.shape
    T = B * S
    n_heads = b_forget.shape[0]
    attn_w = n_heads * HEAD_DIM
    gmlp_w = g_v_ln.shape[0]
    E = w_router.shape[1]
    off_f = 3 * attn_w
    off_z = off_f + n_heads
    off_g = off_z + 2 * gmlp_w

    x2 = x.reshape(T, D)
    w_in_t = w_in.T
    wzg = _rows_bf16(w_in_t, off_z, w_in.shape[1] - off_z)
    wa = w_branch_attn.astype(BF16)
    wb = w_branch_gmlp.astype(BF16)
    wo = w_out.astype(BF16)

    qkv, h = _norm_project(x2, g_mix, w_in_t, 3 * attn_w, attn_w, LOG2E / math.sqrt(HEAD_DIM))
    c_row = _forget_cumsum(h, w_in_t, off_f, b_forget, B, S)
    attn = _attention(qkv, c_row, B, S, n_heads)
    sg = _gmlp(h, wzg, g_v_ln, b_v_ln, w_spatial, b_spatial)
    merged = _merge(attn, sg, h, wa, wb, wzg, off_g - off_z)
    x1, h2p, sel, gate = _out_router(merged, x2, wo, g_ffn, w_router, b_router)

    dest4, g4, nblk, start_blk = _routing(sel, gate)
    n_rows = T * TOP_K + E * ROW_BLOCK
    dest_slots = dest4[:, :TOP_K].T.reshape(TOP_K * T)
    max_blocks = n_rows // ROW_BLOCK
    n_items = E + -(-(max_blocks - E) // ITEM_BLOCKS) + 1
    item_e, item_b, item_n, n_active = _work_items(nblk[0], start_blk[0], n_items)
    xs = _sc_scatter_rows(h2p, dest_slots, n_rows, TOP_K)
    ys = _experts(xs, item_e, item_b, item_n, n_active,
                  w_expert_up, b_expert_up, w_expert_down, b_expert_down)
    out = None
    t_part = T // COMBINE_PARTS
    for p in range(COMBINE_PARTS):
        idx = dest4[p * t_part:(p + 1) * t_part, :TOP_K].T.reshape(TOP_K * t_part)
        yk = _sc_gather_rows(ys, idx)
        out = _combine(yk, g4, x1, g_final, p, COMBINE_PARTS, out)
    return out.reshape(B, S, D)
```

```python
import functools
import math

import jax
import jax.numpy as jnp
import numpy as np
from jax import lax
from jax.experimental import pallas as pl
from jax.experimental.pallas import tpu as pltpu
from jax.experimental.pallas import tpu_sc as plsc

F32 = jnp.float32
BF16 = jnp.bfloat16
I32 = jnp.int32

NORM_EPS = 1e-5
LANES = 128
SUBLANES = 8
NT_DIMS = (((1,), (1,)), ((), ()))
HEAD_DIM = 128
CHUNK = 128
GROUP_DIM = 128
TOP_K = 4
SWIGLU_ALPHA = 1.702
SWIGLU_LIMIT = 7.0
LOG2E = math.log2(math.e)

VMEM_LIMIT_BYTES = 56 * 1024 * 1024

ROW_BLOCK = 128
ITEM_BLOCKS = 10
ITEM_ROWS = ITEM_BLOCKS * ROW_BLOCK
REGION_BLOCKS = (8, 4, 2, 1)
assert sum(REGION_BLOCKS) >= ITEM_BLOCKS
COMPUTE_PLANS = {**{n: ((0, n),) for n in range(4, ITEM_BLOCKS + 1)},
                 3: ((0, 2), (2, 1)), 2: ((0, 2),), 1: ((0, 1),)}
assert set(COMPUTE_PLANS) == set(range(1, ITEM_BLOCKS + 1))
FF_TILE = 256
COMBINE_PARTS = 1


def _cparams(sem, **kw):
    return pltpu.CompilerParams(dimension_semantics=sem, vmem_limit_bytes=VMEM_LIMIT_BYTES, **kw)


def _pack_pairs(x):
    c = x.shape[1] // 2
    hi = lax.bitcast_convert_type(x[:, :c].astype(BF16).astype(F32), jnp.uint32)
    lo = lax.bitcast_convert_type(x[:, c:].astype(BF16).astype(F32), jnp.uint32)
    return hi | (lo >> 16)


def _unpack_pairs(w):
    hi = lax.bitcast_convert_type(w & jnp.uint32(0xFFFF0000), F32)
    lo = lax.bitcast_convert_type(w << 16, F32)
    return jnp.concatenate([hi, lo], axis=1)


def _norm_proj_kernel(x_ref, g_ref, w_ref, o_ref, h_ref, *, n_scaled, scale):
    j = pl.program_id(1)

    @pl.when(j == 0)
    def _():
        x = x_ref[...]
        ms = jnp.mean(x * x, axis=-1, keepdims=True)
        h_ref[...] = (x * lax.rsqrt(ms + NORM_EPS) * g_ref[...]).astype(h_ref.dtype)

    acc = lax.dot_general(h_ref[...], w_ref[...].astype(BF16), NT_DIMS, preferred_element_type=F32)
    o_ref[...] = (acc * jnp.where(j < n_scaled, scale, 1.0)).astype(o_ref.dtype)


def _norm_project(x, g, wt, n_cols, n_scaled_cols, scale, tm=1024, tn=512):
    T, D = x.shape
    return pl.pallas_call(
        functools.partial(_norm_proj_kernel, n_scaled=n_scaled_cols // tn, scale=scale),
        grid=(T // tm, n_cols // tn),
        in_specs=[pl.BlockSpec((tm, D), lambda i, j: (i, 0)),
                  pl.BlockSpec((1, D), lambda i, j: (0, 0)),
                  pl.BlockSpec((tn, D), lambda i, j: (j, 0))],
        out_specs=[pl.BlockSpec((tm, tn), lambda i, j: (i, j)),
                   pl.BlockSpec((tm, D), lambda i, j: (i, 0))],
        out_shape=[jax.ShapeDtypeStruct((T, n_cols), BF16), jax.ShapeDtypeStruct((T, D), BF16)],
        compiler_params=_cparams(("arbitrary", "arbitrary")),
        name="norm_qkv_proj",
    )(x, g.reshape(1, D), wt)


def _cast_kernel(w_ref, o_ref):
    o_ref[...] = w_ref[...].astype(o_ref.dtype)


def _rows_bf16(wt, start, n_rows, tr=512):
    D = wt.shape[1]
    assert start % SUBLANES == 0 and n_rows % tr == 0
    return pl.pallas_call(
        _cast_kernel,
        grid=(n_rows // tr,),
        in_specs=[pl.BlockSpec((pl.Element(tr), pl.Element(D)),
                               lambda i: (pl.multiple_of(start + i * tr, SUBLANES), 0))],
        out_specs=pl.BlockSpec((tr, D), lambda i: (i, 0)),
        out_shape=jax.ShapeDtypeStruct((n_rows, D), BF16),
        compiler_params=_cparams(("parallel",)),
        name="rows_bf16",
    )(wt)


def _forget_kernel(h_ref, wft_ref, bf_ref, c_ref):
    ft = lax.dot_general(wft_ref[...].astype(BF16), h_ref[...], NT_DIMS,
                         preferred_element_type=F32)
    c = jax.nn.log_sigmoid(ft + bf_ref[...])
    S = c.shape[1]
    lane = lax.broadcasted_iota(I32, c.shape, 1)
    shift = 1
    while shift < S:
        c = c + jnp.where(lane >= shift, pltpu.roll(c, shift, axis=1), 0.0)
        shift *= 2
    c_ref[0] = c * LOG2E


def _forget_cumsum(h, wt, f_off, b_forget, B, S):
    T, D = h.shape
    H = b_forget.shape[0]
    assert f_off % H == 0 and H % SUBLANES == 0
    return pl.pallas_call(
        _forget_kernel,
        grid=(B,),
        in_specs=[pl.BlockSpec((S, D), lambda b: (b, 0)),
                  pl.BlockSpec((H, D), lambda b: (f_off // H, 0)),
                  pl.BlockSpec((H, 1), lambda b: (0, 0))],
        out_specs=pl.BlockSpec((1, H, S), lambda b: (b, 0, 0)),
        out_shape=jax.ShapeDtypeStruct((B, H, S), F32),
        compiler_params=_cparams(("parallel",)),
        name="forget_cumsum",
    )(h, wt, b_forget.reshape(H, 1))


def _attn_kernel(q_ref, k_ref, v_ref, crow_ref, o_ref, vaug_ref, m_ref, acc_ref, *, n_heads, tq):
    i = pl.program_id(1)

    @pl.when(i == 0)
    def _():
        ones = jnp.ones((v_ref.shape[0], HEAD_DIM), BF16)
        for h in range(n_heads):
            vaug_ref[h, :, :HEAD_DIM] = v_ref[:, h * HEAD_DIM:(h + 1) * HEAD_DIM]
            vaug_ref[h, :, HEAD_DIM:] = ones

    m_ref[...] = jnp.full(m_ref.shape, -jnp.inf, F32)
    acc_ref[...] = jnp.zeros(acc_ref.shape, F32)
    row = lax.broadcasted_iota(I32, (tq, tq), 0)
    col = lax.broadcasted_iota(I32, (tq, tq), 1)
    causal = col <= row

    def step(j, masked):
        keys = pl.ds(pl.multiple_of(j * tq, tq), tq)
        for h in range(n_heads):
            hs = slice(h * HEAD_DIM, (h + 1) * HEAD_DIM)
            s = lax.dot_general(q_ref[:, hs], k_ref[keys, hs], (((1,), (1,)), ((), ())),
                                preferred_element_type=F32) - crow_ref[0, h, j]
            if masked:
                s = jnp.where(causal, s, -jnp.inf)
            m_old = m_ref[h]
            m_new = jnp.maximum(m_old, jnp.max(s, axis=-1, keepdims=True))
            alpha = jnp.exp2(m_old - m_new)
            p = jnp.exp2(s - jnp.concatenate([m_new] * (tq // HEAD_DIM), axis=1))
            m_ref[h] = m_new
            pv = jnp.dot(p.astype(BF16), vaug_ref[h, keys, :], preferred_element_type=F32)
            acc_ref[h] = jnp.concatenate([alpha, alpha], axis=1) * acc_ref[h] + pv

    def body(j, _):
        step(j, False)
        return 0

    lax.fori_loop(0, i, body, 0)
    step(i, True)
    for h in range(n_heads):
        acc = acc_ref[h]
        o_ref[:, h * HEAD_DIM:(h + 1) * HEAD_DIM] = (acc[:, :HEAD_DIM] / acc[:, HEAD_DIM:]).astype(o_ref.dtype)


def _attention(qkv, c_row, B, S, n_heads, tq=256):
    T = qkv.shape[0]
    W = n_heads * HEAD_DIM
    nq = S // tq
    c_row5 = c_row.reshape(B, n_heads, nq, 1, tq)
    return pl.pallas_call(
        functools.partial(_attn_kernel, n_heads=n_heads, tq=tq),
        grid=(B, nq),
        in_specs=[pl.BlockSpec((tq, W), lambda b, i: (b * nq + i, 0)),
                  pl.BlockSpec((S, W), lambda b, i: (b, 1)),
                  pl.BlockSpec((S, W), lambda b, i: (b, 2)),
                  pl.BlockSpec((1, n_heads, nq, 1, tq), lambda b, i: (b, 0, 0, 0, 0))],
        out_specs=pl.BlockSpec((tq, W), lambda b, i: (b * nq + i, 0)),
        out_shape=jax.ShapeDtypeStruct((T, W), BF16),
        scratch_shapes=[pltpu.VMEM((n_heads, S, 2 * HEAD_DIM), BF16),
                        pltpu.VMEM((n_heads, tq, HEAD_DIM), F32),
                        pltpu.VMEM((n_heads, tq, 2 * HEAD_DIM), F32)],
        compiler_params=_cparams(("arbitrary", "arbitrary")),
        name="fox_attention",
    )(qkv, qkv, qkv, c_row5)


def _gmlp_kernel(h_ref, wz_ref, g_ref, b_ref, ws_ref, bst_ref, o_ref, *, n_groups):
    z = lax.dot_general(h_ref[...], wz_ref[...], NT_DIMS, preferred_element_type=F32)
    z = 0.5 * z * (1.0 + lax.erf(z * (1.0 / math.sqrt(2.0))))
    W = z.shape[1] // 2
    u = z[:, :W]
    v = z[:, W:]
    mu = jnp.mean(v, axis=-1, keepdims=True)
    var = jnp.mean(jnp.square(v - mu), axis=-1, keepdims=True)
    vn = (v - mu) * lax.rsqrt(var + NORM_EPS) * g_ref[...] + b_ref[...]
    row = lax.broadcasted_iota(I32, (CHUNK, CHUNK), 0)
    col = lax.broadcasted_iota(I32, (CHUNK, CHUNK), 1)
    tril = col <= row
    tg = z.shape[0]
    for g in range(n_groups):
        gs = slice(g * GROUP_DIM, (g + 1) * GROUP_DIM)
        wg = jnp.where(tril, ws_ref[g], 0.0).astype(BF16)
        bias = bst_ref[:, g:g + 1]
        for c in range(tg // CHUNK):
            cs = slice(c * CHUNK, (c + 1) * CHUNK)
            mixed = jnp.dot(wg, vn[cs, gs].astype(BF16), preferred_element_type=F32) + bias
            o_ref[cs, gs] = (u[cs, gs] * mixed).astype(o_ref.dtype)


def _gmlp(h, wz, g_v_ln, b_v_ln, w_spatial, b_spatial, tg=512):
    T, D = h.shape
    W = g_v_ln.shape[0]
    W2 = 2 * W
    G = w_spatial.shape[0]
    return pl.pallas_call(
        functools.partial(_gmlp_kernel, n_groups=G),
        grid=(T // tg,),
        in_specs=[pl.BlockSpec((tg, D), lambda i: (i, 0)),
                  pl.BlockSpec((W2, D), lambda i: (0, 0)),
                  pl.BlockSpec((1, W), lambda i: (0, 0)),
                  pl.BlockSpec((1, W), lambda i: (0, 0)),
                  pl.BlockSpec((G, CHUNK, CHUNK), lambda i: (0, 0, 0)),
                  pl.BlockSpec((CHUNK, G), lambda i: (0, 0))],
        out_specs=pl.BlockSpec((tg, W), lambda i: (i, 0)),
        out_shape=jax.ShapeDtypeStruct((T, W), BF16),
        compiler_params=_cparams(("parallel",)),
        name="gmlp",
    )(h, wz, g_v_ln.reshape(1, W), b_v_ln.reshape(1, W), w_spatial, b_spatial.T)


def _merge_kernel(attn_ref, sg_ref, h_ref, wa_ref, wb_ref, wga_ref, wgb_ref, o_ref):
    h = h_ref[...]
    a = jnp.dot(attn_ref[...], wa_ref[...], preferred_element_type=F32)
    ga = lax.dot_general(h, wga_ref[...], NT_DIMS, preferred_element_type=F32)
    m = jax.nn.sigmoid(ga) * a
    b = jnp.dot(sg_ref[...], wb_ref[...], preferred_element_type=F32)
    gb = lax.dot_general(h, wgb_ref[...], NT_DIMS, preferred_element_type=F32)
    o_ref[...] = (m + jax.nn.sigmoid(gb) * b).astype(o_ref.dtype)


def _merge(attn, sg, h, wa, wb, wg, g_off, tm=512, tn=512):
    T, D = h.shape
    Wa = attn.shape[1]
    Wb = sg.shape[1]
    nt = D // tn
    g0 = g_off // tn
    return pl.pallas_call(
        _merge_kernel,
        grid=(nt, T // tm),
        in_specs=[pl.BlockSpec((tm, Wa), lambda j, i: (i, 0)),
                  pl.BlockSpec((tm, Wb), lambda j, i: (i, 0)),
                  pl.BlockSpec((tm, D), lambda j, i: (i, 0)),
                  pl.BlockSpec((Wa, tn), lambda j, i: (0, j)),
                  pl.BlockSpec((Wb, tn), lambda j, i: (0, j)),
                  pl.BlockSpec((tn, D), lambda j, i: (g0 + j, 0)),
                  pl.BlockSpec((tn, D), lambda j, i: (g0 + nt + j, 0))],
        out_specs=pl.BlockSpec((tm, tn), lambda j, i: (i, j)),
        out_shape=jax.ShapeDtypeStruct((T, D), BF16),
        compiler_params=_cparams(("arbitrary", "arbitrary")),
        name="gated_merge",
    )(attn, sg, h, wa, wb, wg, wg)


def _out_router_kernel(m_ref, x_ref, wo_ref, g_ref, wr_ref, br_ref,
                       x1_ref, h2_ref, sel_ref, gate_ref, *, n_parts):
    E = br_ref.shape[1]
    part = m_ref.shape[0] // n_parts
    for p in range(n_parts):
        rows = pl.ds(p * part, part)
        x1 = x_ref[rows, :] + jnp.dot(m_ref[rows, :], wo_ref[...], preferred_element_type=F32)
        x1_ref[rows, :] = x1
        ms = jnp.mean(x1 * x1, axis=-1, keepdims=True)
        h2 = x1 * lax.rsqrt(ms + NORM_EPS) * g_ref[...]
        h2_ref[rows, :] = _pack_pairs(h2)
        h2_hi = h2.astype(BF16)
        h2_lo = (h2 - h2_hi.astype(F32)).astype(BF16)
        pa = jnp.dot(h2_hi, wr_ref[...], preferred_element_type=F32)
        pb = jnp.dot(h2_lo, wr_ref[:, :E], preferred_element_type=F32)
        logits = pa[:, :E] + (pa[:, E:] + pb) + br_ref[...]
        lane = lax.broadcasted_iota(I32, logits.shape, 1)
        work = logits
        sel = jnp.zeros(logits.shape, F32)
        num = jnp.zeros(logits.shape, F32)
        denom = jnp.zeros((part, 1), F32)
        m0 = None
        for _ in range(TOP_K):
            m = jnp.max(work, axis=-1, keepdims=True)
            idx = jnp.min(jnp.where(work == m, lane, E), axis=-1, keepdims=True)
            onehot = lane == idx
            if m0 is None:
                m0 = m
            e = jnp.exp(m - m0)
            sel = jnp.where(onehot, 1.0, sel)
            num = jnp.where(onehot, e, num)
            denom = denom + e
            work = jnp.where(onehot, -jnp.inf, work)
        sel_ref[rows, :] = sel
        gate_ref[rows, :] = num / denom


def _out_router(merged, x, wo, g_ffn, w_router, b_router, to=512, n_parts=2):
    T, D = x.shape
    E = w_router.shape[1]
    w_hi = w_router.astype(BF16)
    w_lo = (w_router - w_hi.astype(F32)).astype(BF16)
    w_router = jnp.concatenate([w_hi, w_lo], axis=1)
    row = lambda i: (i, 0)
    fixed = lambda i: (0, 0)
    return pl.pallas_call(
        functools.partial(_out_router_kernel, n_parts=n_parts),
        grid=(T // to,),
        in_specs=[pl.BlockSpec((to, D), row), pl.BlockSpec((to, D), row),
                  pl.BlockSpec((D, D), fixed), pl.BlockSpec((1, D), fixed),
                  pl.BlockSpec((D, 2 * E), fixed), pl.BlockSpec((1, E), fixed)],
        out_specs=[pl.BlockSpec((to, D), row), pl.BlockSpec((to, D // 2), row),
                   pl.BlockSpec((to, E), row), pl.BlockSpec((to, E), row)],
        out_shape=[jax.ShapeDtypeStruct((T, D), F32), jax.ShapeDtypeStruct((T, D // 2), jnp.uint32),
                   jax.ShapeDtypeStruct((T, E), F32), jax.ShapeDtypeStruct((T, E), F32)],
        compiler_params=_cparams(("parallel",)),
        name="out_router",
    )(merged, x, wo, g_ffn.reshape(1, D), w_router, b_router.reshape(1, E))


def _routing_kernel(sel_ref, gate_ref, dest_ref, g4_ref, nblk_ref, start_ref, rank_ref, *, tile):
    T, E = sel_ref.shape
    nt = T // tile
    r = lax.broadcasted_iota(I32, (tile, tile), 0)
    c = lax.broadcasted_iota(I32, (tile, tile), 1)
    strict_lower = (c < r).astype(BF16)
    er = lax.broadcasted_iota(I32, (E, E), 0)
    ec = lax.broadcasted_iota(I32, (E, E), 1)
    strict_upper = (er < ec).astype(BF16)

    def pass1(t, carry):
        rows = pl.ds(pl.multiple_of(t * tile, tile), tile)
        a = sel_ref[rows, :]
        rank_ref[rows, :] = jnp.dot(strict_lower, a.astype(BF16), preferred_element_type=F32) + carry
        return carry + jnp.sum(a, axis=0, keepdims=True)

    counts = lax.fori_loop(0, nt, pass1, jnp.zeros((1, E), F32))
    nblk = jnp.floor((counts + (ROW_BLOCK - 1)) * (1.0 / ROW_BLOCK))
    start_blk = jnp.dot(nblk.astype(BF16), strict_upper, preferred_element_type=F32)
    nblk_ref[...] = nblk.astype(I32)
    start_ref[...] = start_blk.astype(I32)
    start_row = start_blk * float(ROW_BLOCK)
    lane = lax.broadcasted_iota(I32, (tile, 128), 1)

    def pass2(t, _):
        rows = pl.ds(pl.multiple_of(t * tile, tile), tile)
        a = sel_ref[rows, :]
        g = gate_ref[rows, :]
        dest_e = rank_ref[rows, :] + start_row
        slot = jnp.dot(a.astype(BF16), strict_upper, preferred_element_type=F32)
        d4 = jnp.zeros((tile, 128), F32)
        g4 = jnp.zeros((tile, 128), F32)
        for s in range(TOP_K):
            pick = (a > 0.5) & (slot == float(s))
            d4 = jnp.where(lane == s, jnp.sum(jnp.where(pick, dest_e, 0.0), axis=-1, keepdims=True), d4)
            g4 = jnp.where(lane == s, jnp.sum(jnp.where(pick, g, 0.0), axis=-1, keepdims=True), g4)
        dest_ref[rows, :] = d4.astype(I32)
        g4_ref[rows, :] = g4
        return 0

    lax.fori_loop(0, nt, pass2, 0)


def _routing(sel, gate, tile=256):
    T, E = sel.shape
    return pl.pallas_call(
        functools.partial(_routing_kernel, tile=tile),
        out_shape=[jax.ShapeDtypeStruct((T, 128), I32), jax.ShapeDtypeStruct((T, 128), F32),
                   jax.ShapeDtypeStruct((1, E), I32), jax.ShapeDtypeStruct((1, E), I32)],
        scratch_shapes=[pltpu.VMEM((T, E), F32)],
        compiler_params=pltpu.CompilerParams(vmem_limit_bytes=VMEM_LIMIT_BYTES),
        name="routing_ranks",
    )(sel, gate)


SC_CORES = 2
SC_SUBCORES = 16
SC_CHUNK = 64


def _sc_gather_rows(table, idx):
    n = idx.shape[0]
    W = table.shape[1]
    n_workers = SC_CORES * SC_SUBCORES
    per_worker = n // n_workers
    assert per_worker * n_workers == n and per_worker % SC_CHUNK == 0
    mesh = plsc.VectorSubcoreMesh(core_axis_name="c", subcore_axis_name="s",
                                  num_cores=SC_CORES, num_subcores=SC_SUBCORES)

    @functools.partial(
        pl.kernel, mesh=mesh,
        out_type=jax.ShapeDtypeStruct((n, W), table.dtype),
        scratch_types=[pltpu.VMEM((SC_CHUNK,), I32), pltpu.VMEM((SC_CHUNK, W), table.dtype),
                       pltpu.SemaphoreType.DMA],
        name="sc_gather_rows",
    )
    def gather(table_hbm, idx_hbm, out_hbm, idx_v, rows_v, sem):
        wid = lax.axis_index("s") * SC_CORES + lax.axis_index("c")
        base = wid * per_worker

        @pl.loop(0, per_worker // SC_CHUNK)
        def _(c):
            off = pl.multiple_of(base + c * SC_CHUNK, 8)
            pltpu.sync_copy(idx_hbm.at[pl.ds(off, SC_CHUNK)], idx_v)
            pltpu.async_copy(table_hbm.at[idx_v], rows_v, sem).wait()
            pltpu.sync_copy(rows_v, out_hbm.at[pl.ds(off, SC_CHUNK)])

    return gather(table, idx)


def _sc_scatter_rows(rows, idx, n_out, n_slots):
    T, W = rows.shape
    n_workers = SC_CORES * SC_SUBCORES
    per_worker = T // n_workers
    assert per_worker * n_workers == T and per_worker % SC_CHUNK == 0
    mesh = plsc.VectorSubcoreMesh(core_axis_name="c", subcore_axis_name="s",
                                  num_cores=SC_CORES, num_subcores=SC_SUBCORES)

    @functools.partial(
        pl.kernel, mesh=mesh,
        out_type=jax.ShapeDtypeStruct((n_out, W), rows.dtype),
        scratch_types=[pltpu.VMEM((SC_CHUNK,), I32), pltpu.VMEM((SC_CHUNK, W), rows.dtype)],
        name="sc_scatter_rows",
    )
    def scatter(rows_hbm, idx_hbm, out_hbm, idx_v, rows_v):
        wid = lax.axis_index("s") * SC_CORES + lax.axis_index("c")
        base = wid * per_worker

        @pl.loop(0, per_worker // SC_CHUNK)
        def _(c):
            off = pl.multiple_of(base + c * SC_CHUNK, 8)
            pltpu.sync_copy(rows_hbm.at[pl.ds(off, SC_CHUNK)], rows_v)
            for k in range(n_slots):
                pltpu.sync_copy(idx_hbm.at[pl.ds(pl.multiple_of(k * T + off, 8), SC_CHUNK)], idx_v)
                pltpu.sync_copy(rows_v, out_hbm.at[idx_v])

    return scatter(rows, idx)


def _expert_kernel(ie_ref, ib_ref, ins_ref,
                   xs_hbm, wup_ref, bup_ref, wdn_ref, bdn_ref, perm_ref, ys_hbm,
                   xg_ref, acc_ref, yst_ref, wupb_ref, wdnb_ref, gsem, osem, *, n_ff_tiles):
    i = pl.program_id(0)
    j = pl.program_id(1)
    n_items = pl.num_programs(0)
    nsub = ins_ref[i]
    slot = i % 2

    def for_regions(n_blocks, fn):
        first = jnp.int32(0)
        for count in REGION_BLOCKS:
            present = (n_blocks & count) != 0
            pl.when(present)(functools.partial(fn, first, count))
            first = first + jnp.where(present, count, 0)

    def rows_of(first, count):
        return pl.ds(pl.multiple_of(first * ROW_BLOCK, ROW_BLOCK), count * ROW_BLOCK)

    def in_copy(item, dst_slot, first, count):
        src = pl.multiple_of((ib_ref[item] + first) * ROW_BLOCK, ROW_BLOCK)
        return pltpu.make_async_copy(xs_hbm.at[pl.ds(src, count * ROW_BLOCK)],
                                     xg_ref.at[dst_slot, rows_of(first, count)], gsem.at[dst_slot])

    def fetch_item(item, n_blocks, dst_slot):
        for_regions(n_blocks, lambda first, count: in_copy(item, dst_slot, first, count).start())

    @pl.when(j == 0)
    def _():
        @pl.when(i == 0)
        def _():
            fetch_item(0, nsub, 0)

        nxt = jnp.minimum(i + 1, n_items - 1)
        fetch_item(nxt, jnp.where(i + 1 < n_items, ins_ref[nxt], 0), 1 - slot)

        def arrive(first, count):
            in_copy(i, slot, first, count).wait()
            acc_ref[rows_of(first, count), :] = jnp.broadcast_to(
                bdn_ref[0], (count * ROW_BLOCK, acc_ref.shape[1]))

        for_regions(nsub, arrive)

    @pl.when(nsub > 0)
    def _():
        bup = bup_ref[0]
        perm = perm_ref[...]
        half = perm.shape[0] // 2

        def run_blocks(first, count):
            if first == 0:
                wup = wup_ref[0].astype(BF16)
                wdn = wdn_ref[0].astype(BF16)
                wupb_ref[...] = wup
                wdnb_ref[...] = wdn
            else:
                wup = wupb_ref[...]
                wdn = wdnb_ref[...]
            rows = pl.ds(first * ROW_BLOCK, count * ROW_BLOCK)
            xb = _unpack_pairs(xg_ref[slot, rows, :]).astype(BF16)
            gu = (jnp.dot(xb, wup, preferred_element_type=F32) + bup).astype(BF16)
            glu_parts, lin_parts = [], []
            for p in range(gu.shape[1] // perm.shape[0]):
                gp = jnp.dot(gu[:, p * perm.shape[0]:(p + 1) * perm.shape[0]], perm,
                             preferred_element_type=F32)
                glu_parts.append(gp[:, :half])
                lin_parts.append(gp[:, half:])
            x_glu = jnp.minimum(jnp.concatenate(glu_parts, axis=1), SWIGLU_LIMIT)
            x_lin = jnp.clip(jnp.concatenate(lin_parts, axis=1), -SWIGLU_LIMIT, SWIGLU_LIMIT)
            act = x_glu * jax.nn.sigmoid(SWIGLU_ALPHA * x_glu) * (x_lin + 1.0)
            acc_ref[rows, :] += jnp.dot(act.astype(BF16), wdn, preferred_element_type=F32)

        for n_blocks, plan in COMPUTE_PLANS.items():
            for first, count in plan:
                shared = [n for n, p in COMPUTE_PLANS.items() if (first, count) in p]
                if n_blocks == shared[0]:
                    cond = functools.reduce(jnp.logical_or, [nsub == n for n in shared])
                    pl.when(cond)(functools.partial(run_blocks, first, count))

    def out_copy(item, first, count):
        dst = pl.multiple_of((ib_ref[item] + first) * ROW_BLOCK, ROW_BLOCK)
        return pltpu.make_async_copy(yst_ref.at[rows_of(first, count)],
                                     ys_hbm.at[pl.ds(dst, count * ROW_BLOCK)], osem)

    @pl.when(j == n_ff_tiles - 1)
    def _():
        prev = jnp.maximum(i - 1, 0)
        for_regions(jnp.where(i > 0, ins_ref[prev], 0),
                    lambda first, count: out_copy(prev, first, count).wait())

        def leave(first, count):
            rows = rows_of(first, count)
            yst_ref[rows, :] = _pack_pairs(acc_ref[rows, :])
            out_copy(i, first, count).start()

        for_regions(nsub, leave)

        @pl.when(i == n_items - 1)
        def _():
            for_regions(nsub, lambda first, count: out_copy(i, first, count).wait())


def _experts(xs, item_e, item_b, item_n, n_active, w_up, b_up, w_down, b_down):
    n_rows, Dp = xs.shape
    D = 2 * Dp
    E, _, F2 = w_up.shape
    F = F2 // 2
    J = F // FF_TILE
    half = 128
    perm = np.zeros((2 * half, 2 * half), np.float32)
    perm[2 * np.arange(half), np.arange(half)] = 1.0
    perm[2 * np.arange(half) + 1, half + np.arange(half)] = 1.0

    def jj(i, j, ins):
        return jnp.where(ins[i] > 0, j, J - 1)

    grid_spec = pltpu.PrefetchScalarGridSpec(
        num_scalar_prefetch=3,
        grid=(n_active, J),
        in_specs=[pl.BlockSpec(memory_space=pl.ANY),
                  pl.BlockSpec((1, D, 2 * FF_TILE), lambda i, j, ie, ib, ins: (ie[i], 0, jj(i, j, ins))),
                  pl.BlockSpec((1, 1, 2 * FF_TILE), lambda i, j, ie, ib, ins: (ie[i], 0, jj(i, j, ins))),
                  pl.BlockSpec((1, FF_TILE, D), lambda i, j, ie, ib, ins: (ie[i], jj(i, j, ins), 0)),
                  pl.BlockSpec((1, 1, D), lambda i, j, ie, ib, ins: (ie[i], 0, 0)),
                  pl.BlockSpec((2 * half, 2 * half), lambda i, j, ie, ib, ins: (0, 0))],
        out_specs=pl.BlockSpec(memory_space=pl.ANY),
        scratch_shapes=[pltpu.VMEM((2, ITEM_ROWS, Dp), jnp.uint32),
                        pltpu.VMEM((ITEM_ROWS, D), F32),
                        pltpu.VMEM((ITEM_ROWS, Dp), jnp.uint32),
                        pltpu.VMEM((D, 2 * FF_TILE), BF16),
                        pltpu.VMEM((FF_TILE, D), BF16),
                        pltpu.SemaphoreType.DMA((2,)), pltpu.SemaphoreType.DMA(())],
    )
    return pl.pallas_call(
        functools.partial(_expert_kernel, n_ff_tiles=J),
        grid_spec=grid_spec,
        out_shape=jax.ShapeDtypeStruct((n_rows, Dp), jnp.uint32),
        compiler_params=_cparams(("arbitrary", "arbitrary")),
        name="expert_ffn",
    )(item_e, item_b, item_n,
      xs, w_up, b_up.reshape(E, 1, F2), w_down, b_down.reshape(E, 1, D), jnp.asarray(perm, BF16))


def _combine_kernel(*refs):
    yk_refs, (g4_ref, x1_ref, g_ref), o_ref = refs[:TOP_K], refs[TOP_K:TOP_K + 3], refs[-1]
    y = x1_ref[...]
    g4 = g4_ref[...]
    for k in range(TOP_K):
        y = y + g4[:, k:k + 1] * _unpack_pairs(yk_refs[k][...])
    ms = jnp.mean(y * y, axis=-1, keepdims=True)
    o_ref[...] = y * lax.rsqrt(ms + NORM_EPS) * g_ref[...]


def _combine(yk, g4, x1, g_final, part, n_parts, out_so_far, tc=256):
    T, D = x1.shape
    nt = T // n_parts // tc
    first = part * nt
    slot_specs = [pl.BlockSpec((tc, D // 2), functools.partial(lambda i, k: (k * nt + i, 0), k=k))
                  for k in range(TOP_K)]
    operands = [*([yk] * TOP_K), g4, x1, g_final.reshape(1, D)]
    in_specs = slot_specs + [pl.BlockSpec((tc, 128), lambda i: (first + i, 0)),
                             pl.BlockSpec((tc, D), lambda i: (first + i, 0)),
                             pl.BlockSpec((1, D), lambda i: (0, 0))]
    aliases = {}
    if out_so_far is not None:
        aliases = {len(operands): 0}
        operands.append(out_so_far)
        in_specs.append(pl.BlockSpec(memory_space=pl.ANY))
    return pl.pallas_call(
        _combine_kernel,
        grid=(nt,),
        in_specs=in_specs,
        out_specs=pl.BlockSpec((tc, D), lambda i: (first + i, 0)),
        out_shape=jax.ShapeDtypeStruct((T, D), F32),
        input_output_aliases=aliases,
        compiler_params=_cparams(("parallel",)),
        name="combine_norm",
    )(*operands)


def _work_items(nblk, start_blk, n_items):
    E = nblk.shape[0]
    per_e = (nblk + ITEM_BLOCKS - 1) // ITEM_BLOCKS
    ends = jnp.cumsum(per_e)
    total = ends[-1]
    idx = jnp.arange(n_items, dtype=I32)
    e = jnp.minimum(jnp.searchsorted(ends, idx, side="right"), E - 1).astype(I32)
    local = idx - (ends[e] - per_e[e])
    active = idx < total
    last_e = e[jnp.maximum(total - 1, 0)]
    item_e = jnp.where(active, e, last_e).astype(I32)
    item_b = jnp.where(active, start_blk[e] + local * ITEM_BLOCKS, 0).astype(I32)
    item_n = jnp.where(active, jnp.clip(nblk[e] - local * ITEM_BLOCKS, 0, ITEM_BLOCKS), 0).astype(I32)
    return item_e, item_b, item_n, jnp.maximum(total, 1).astype(I32)


def kernel(x, g_mix, w_in, b_forget, g_v_ln, b_v_ln, w_spatial, b_spatial, w_branch_attn, w_branch_gmlp, w_out, g_ffn, w_router, b_router, w_expert_up, b_expert_up, w_expert_down, b_expert_down, g_final):
    B, S, D = x.shape
    T = B * S
    n_heads = b_forget.shape[0]
    attn_w = n_heads * HEAD_DIM
    gmlp_w = g_v_ln.shape[0]
    E = w_router.shape[1]
    off_f = 3 * attn_w
    off_z = off_f + n_heads
    off_g = off_z + 2 * gmlp_w

    x2 = x.reshape(T, D)
    w_in_t = w_in.T
    wzg = _rows_bf16(w_in_t, off_z, w_in.shape[1] - off_z)
    wa = w_branch_attn.astype(BF16)
    wb = w_branch_gmlp.astype(BF16)
    wo = w_out.astype(BF16)

    qkv, h = _norm_project(x2, g_mix, w_in_t, 3 * attn_w, attn_w, LOG2E / math.sqrt(HEAD_DIM))
    c_row = _forget_cumsum(h, w_in_t, off_f, b_forget, B, S)
    attn = _attention(qkv, c_row, B, S, n_heads)
    sg = _gmlp(h, wzg, g_v_ln, b_v_ln, w_spatial, b_spatial)
    merged = _merge(attn, sg, h, wa, wb, wzg, off_g - off_z)
    x1, h2p, sel, gate = _out_router(merged, x2, wo, g_ffn, w_router, b_router)

    dest4, g4, nblk, start_blk = _routing(sel, gate)
    n_rows = T * TOP_K + E * ROW_BLOCK
    dest_slots = dest4[:, :TOP_K].T.reshape(TOP_K * T)
    max_blocks = n_rows // ROW_BLOCK
    n_items = E + -(-(max_blocks - E) // ITEM_BLOCKS) + 1
    item_e, item_b, item_n, n_active = _work_items(nblk[0], start_blk[0], n_items)
    xs = _sc_scatter_rows(h2p, dest_slots, n_rows, TOP_K)
    ys = _experts(xs, item_e, item_b, item_n, n_active,
                  w_expert_up, b_expert_up, w_expert_down, b_expert_down)
    out = None
    t_part = T // COMBINE_PARTS
    for p in range(COMBINE_PARTS):
        idx = dest4[p * t_part:(p + 1) * t_part, :TOP_K].T.reshape(TOP_K * t_part)
        yk = _sc_gather_rows(ys, idx)
        out = _combine(yk, g4, x1, g_final, p, COMBINE_PARTS, out)
    return out.reshape(B, S, D)
```

```python
import functools
import math

import jax
import jax.numpy as jnp
import numpy as np
from jax import lax
from jax.experimental import pallas as pl
from jax.experimental.pallas import tpu as pltpu
from jax.experimental.pallas import tpu_sc as plsc

F32 = jnp.float32
BF16 = jnp.bfloat16
I32 = jnp.int32

NORM_EPS = 1e-5
LANES = 128
SUBLANES = 8
NT_DIMS = (((1,), (1,)), ((), ()))
HEAD_DIM = 128
CHUNK = 128
GROUP_DIM = 128
TOP_K = 4
SWIGLU_ALPHA = 1.702
SWIGLU_LIMIT = 7.0
LOG2E = math.log2(math.e)

VMEM_LIMIT_BYTES = 56 * 1024 * 1024

ROW_BLOCK = 128
ITEM_BLOCKS = 10
ITEM_ROWS = ITEM_BLOCKS * ROW_BLOCK
REGION_BLOCKS = (8, 4, 2, 1)
assert sum(REGION_BLOCKS) >= ITEM_BLOCKS
COMPUTE_PLANS = {**{n: ((0, n),) for n in range(4, ITEM_BLOCKS + 1)},
                 3: ((0, 2), (2, 1)), 2: ((0, 2),), 1: ((0, 1),)}
assert set(COMPUTE_PLANS) == set(range(1, ITEM_BLOCKS + 1))
FF_TILE = 256
COMBINE_PARTS = 1


def _cparams(sem, **kw):
    return pltpu.CompilerParams(dimension_semantics=sem, vmem_limit_bytes=VMEM_LIMIT_BYTES, **kw)


def _pack_pairs(x):
    c = x.shape[1] // 2
    hi = lax.bitcast_convert_type(x[:, :c].astype(BF16).astype(F32), jnp.uint32)
    lo = lax.bitcast_convert_type(x[:, c:].astype(BF16).astype(F32), jnp.uint32)
    return hi | (lo >> 16)


def _unpack_pairs(w):
    hi = lax.bitcast_convert_type(w & jnp.uint32(0xFFFF0000), F32)
    lo = lax.bitcast_convert_type(w << 16, F32)
    return jnp.concatenate([hi, lo], axis=1)


def _norm_proj_kernel(x_ref, g_ref, w_ref, o_ref, h_ref, *, n_scaled, scale):
    j = pl.program_id(1)

    @pl.when(j == 0)
    def _():
        x = x_ref[...]
        ms = jnp.mean(x * x, axis=-1, keepdims=True)
        h_ref[...] = (x * lax.rsqrt(ms + NORM_EPS) * g_ref[...]).astype(h_ref.dtype)

    acc = lax.dot_general(h_ref[...], w_ref[...], NT_DIMS, preferred_element_type=F32)
    o_ref[...] = (acc * jnp.where(j < n_scaled, scale, 1.0)).astype(o_ref.dtype)


def _norm_project(x, g, wt, n_cols, n_scaled_cols, scale, tm=1024, tn=512):
    T, D = x.shape
    return pl.pallas_call(
        functools.partial(_norm_proj_kernel, n_scaled=n_scaled_cols // tn, scale=scale),
        grid=(T // tm, n_cols // tn),
        in_specs=[pl.BlockSpec((tm, D), lambda i, j: (i, 0)),
                  pl.BlockSpec((1, D), lambda i, j: (0, 0)),
                  pl.BlockSpec((tn, D), lambda i, j: (j, 0))],
        out_specs=[pl.BlockSpec((tm, tn), lambda i, j: (i, j)),
                   pl.BlockSpec((tm, D), lambda i, j: (i, 0))],
        out_shape=[jax.ShapeDtypeStruct((T, n_cols), BF16), jax.ShapeDtypeStruct((T, D), BF16)],
        compiler_params=_cparams(("arbitrary", "arbitrary")),
        name="norm_qkv_proj",
    )(x, g.reshape(1, D), wt)


def _cast_kernel(w_ref, o_ref):
    o_ref[...] = w_ref[...].astype(o_ref.dtype)


def _rows_bf16(wt, start, n_rows, tr=512):
    D = wt.shape[1]
    assert start % SUBLANES == 0 and n_rows % tr == 0
    return pl.pallas_call(
        _cast_kernel,
        grid=(n_rows // tr,),
        in_specs=[pl.BlockSpec((pl.Element(tr), pl.Element(D)),
                               lambda i: (pl.multiple_of(start + i * tr, SUBLANES), 0))],
        out_specs=pl.BlockSpec((tr, D), lambda i: (i, 0)),
        out_shape=jax.ShapeDtypeStruct((n_rows, D), BF16),
        compiler_params=_cparams(("parallel",)),
        name="rows_bf16",
    )(wt)


def _forget_kernel(h_ref, wft_ref, bf_ref, c_ref):
    ft = lax.dot_general(wft_ref[...].astype(BF16), h_ref[...], NT_DIMS,
                         preferred_element_type=F32)
    c = jax.nn.log_sigmoid(ft + bf_ref[...])
    S = c.shape[1]
    lane = lax.broadcasted_iota(I32, c.shape, 1)
    shift = 1
    while shift < S:
        c = c + jnp.where(lane >= shift, pltpu.roll(c, shift, axis=1), 0.0)
        shift *= 2
    c_ref[0] = c * LOG2E


def _forget_cumsum(h, wt, f_off, b_forget, B, S):
    T, D = h.shape
    H = b_forget.shape[0]
    assert f_off % H == 0 and H % SUBLANES == 0
    return pl.pallas_call(
        _forget_kernel,
        grid=(B,),
        in_specs=[pl.BlockSpec((S, D), lambda b: (b, 0)),
                  pl.BlockSpec((H, D), lambda b: (f_off // H, 0)),
                  pl.BlockSpec((H, 1), lambda b: (0, 0))],
        out_specs=pl.BlockSpec((1, H, S), lambda b: (b, 0, 0)),
        out_shape=jax.ShapeDtypeStruct((B, H, S), F32),
        compiler_params=_cparams(("parallel",)),
        name="forget_cumsum",
    )(h, wt, b_forget.reshape(H, 1))


def _attn_kernel(q_ref, k_ref, v_ref, crow_ref, o_ref, vaug_ref, m_ref, acc_ref, *, n_heads, tq):
    i = pl.program_id(1)

    @pl.when(i == 0)
    def _():
        ones = jnp.ones((v_ref.shape[0], HEAD_DIM), BF16)
        for h in range(n_heads):
            vaug_ref[h, :, :HEAD_DIM] = v_ref[:, h * HEAD_DIM:(h + 1) * HEAD_DIM]
            vaug_ref[h, :, HEAD_DIM:] = ones

    m_ref[...] = jnp.full(m_ref.shape, -jnp.inf, F32)
    acc_ref[...] = jnp.zeros(acc_ref.shape, F32)
    row = lax.broadcasted_iota(I32, (tq, tq), 0)
    col = lax.broadcasted_iota(I32, (tq, tq), 1)
    causal = col <= row

    def step(j, masked):
        keys = pl.ds(pl.multiple_of(j * tq, tq), tq)
        for h in range(n_heads):
            hs = slice(h * HEAD_DIM, (h + 1) * HEAD_DIM)
            s = lax.dot_general(q_ref[:, hs], k_ref[keys, hs], (((1,), (1,)), ((), ())),
                                preferred_element_type=F32) - crow_ref[0, h, j]
            if masked:
                s = jnp.where(causal, s, -jnp.inf)
            m_old = m_ref[h]
            m_new = jnp.maximum(m_old, jnp.max(s, axis=-1, keepdims=True))
            alpha = jnp.exp2(m_old - m_new)
            p = jnp.exp2(s - jnp.concatenate([m_new] * (tq // HEAD_DIM), axis=1))
            m_ref[h] = m_new
            pv = jnp.dot(p.astype(BF16), vaug_ref[h, keys, :], preferred_element_type=F32)
            acc_ref[h] = jnp.concatenate([alpha, alpha], axis=1) * acc_ref[h] + pv

    def body(j, _):
        step(j, False)
        return 0

    lax.fori_loop(0, i, body, 0)
    step(i, True)
    for h in range(n_heads):
        acc = acc_ref[h]
        o_ref[:, h * HEAD_DIM:(h + 1) * HEAD_DIM] = (acc[:, :HEAD_DIM] / acc[:, HEAD_DIM:]).astype(o_ref.dtype)


def _attention(qkv, c_row, B, S, n_heads, tq=256):
    T = qkv.shape[0]
    W = n_heads * HEAD_DIM
    nq = S // tq
    c_row5 = c_row.reshape(B, n_heads, nq, 1, tq)
    return pl.pallas_call(
        functools.partial(_attn_kernel, n_heads=n_heads, tq=tq),
        grid=(B, nq),
        in_specs=[pl.BlockSpec((tq, W), lambda b, i: (b * nq + i, 0)),
                  pl.BlockSpec((S, W), lambda b, i: (b, 1)),
                  pl.BlockSpec((S, W), lambda b, i: (b, 2)),
                  pl.BlockSpec((1, n_heads, nq, 1, tq), lambda b, i: (b, 0, 0, 0, 0))],
        out_specs=pl.BlockSpec((tq, W), lambda b, i: (b * nq + i, 0)),
        out_shape=jax.ShapeDtypeStruct((T, W), BF16),
        scratch_shapes=[pltpu.VMEM((n_heads, S, 2 * HEAD_DIM), BF16),
                        pltpu.VMEM((n_heads, tq, HEAD_DIM), F32),
                        pltpu.VMEM((n_heads, tq, 2 * HEAD_DIM), F32)],
        compiler_params=_cparams(("arbitrary", "arbitrary")),
        name="fox_attention",
    )(qkv, qkv, qkv, c_row5)


def _gmlp_kernel(h_ref, wz_ref, g_ref, b_ref, ws_ref, bst_ref, o_ref, *, n_groups):
    z = lax.dot_general(h_ref[...], wz_ref[...], NT_DIMS, preferred_element_type=F32)
    z = 0.5 * z * (1.0 + lax.erf(z * (1.0 / math.sqrt(2.0))))
    W = z.shape[1] // 2
    u = z[:, :W]
    v = z[:, W:]
    mu = jnp.mean(v, axis=-1, keepdims=True)
    var = jnp.mean(jnp.square(v - mu), axis=-1, keepdims=True)
    vn = (v - mu) * lax.rsqrt(var + NORM_EPS) * g_ref[...] + b_ref[...]
    row = lax.broadcasted_iota(I32, (CHUNK, CHUNK), 0)
    col = lax.broadcasted_iota(I32, (CHUNK, CHUNK), 1)
    tril = col <= row
    tg = z.shape[0]
    for g in range(n_groups):
        gs = slice(g * GROUP_DIM, (g + 1) * GROUP_DIM)
        wg = jnp.where(tril, ws_ref[g], 0.0).astype(BF16)
        bias = bst_ref[:, g:g + 1]
        for c in range(tg // CHUNK):
            cs = slice(c * CHUNK, (c + 1) * CHUNK)
            mixed = jnp.dot(wg, vn[cs, gs].astype(BF16), preferred_element_type=F32) + bias
            o_ref[cs, gs] = (u[cs, gs] * mixed).astype(o_ref.dtype)


def _gmlp(h, wz, g_v_ln, b_v_ln, w_spatial, b_spatial, tg=512):
    T, D = h.shape
    W = g_v_ln.shape[0]
    W2 = 2 * W
    G = w_spatial.shape[0]
    return pl.pallas_call(
        functools.partial(_gmlp_kernel, n_groups=G),
        grid=(T // tg,),
        in_specs=[pl.BlockSpec((tg, D), lambda i: (i, 0)),
                  pl.BlockSpec((W2, D), lambda i: (0, 0)),
                  pl.BlockSpec((1, W), lambda i: (0, 0)),
                  pl.BlockSpec((1, W), lambda i: (0, 0)),
                  pl.BlockSpec((G, CHUNK, CHUNK), lambda i: (0, 0, 0)),
                  pl.BlockSpec((CHUNK, G), lambda i: (0, 0))],
        out_specs=pl.BlockSpec((tg, W), lambda i: (i, 0)),
        out_shape=jax.ShapeDtypeStruct((T, W), BF16),
        compiler_params=_cparams(("parallel",)),
        name="gmlp",
    )(h, wz, g_v_ln.reshape(1, W), b_v_ln.reshape(1, W), w_spatial, b_spatial.T)


def _merge_kernel(attn_ref, sg_ref, h_ref, wa_ref, wb_ref, wga_ref, wgb_ref, o_ref):
    h = h_ref[...]
    a = jnp.dot(attn_ref[...], wa_ref[...], preferred_element_type=F32)
    ga = lax.dot_general(h, wga_ref[...], NT_DIMS, preferred_element_type=F32)
    m = jax.nn.sigmoid(ga) * a
    b = jnp.dot(sg_ref[...], wb_ref[...], preferred_element_type=F32)
    gb = lax.dot_general(h, wgb_ref[...], NT_DIMS, preferred_element_type=F32)
    o_ref[...] = (m + jax.nn.sigmoid(gb) * b).astype(o_ref.dtype)


def _merge(attn, sg, h, wa, wb, wg, g_off, tm=512, tn=512):
    T, D = h.shape
    Wa = attn.shape[1]
    Wb = sg.shape[1]
    nt = D // tn
    g0 = g_off // tn
    return pl.pallas_call(
        _merge_kernel,
        grid=(nt, T // tm),
        in_specs=[pl.BlockSpec((tm, Wa), lambda j, i: (i, 0)),
                  pl.BlockSpec((tm, Wb), lambda j, i: (i, 0)),
                  pl.BlockSpec((tm, D), lambda j, i: (i, 0)),
                  pl.BlockSpec((Wa, tn), lambda j, i: (0, j)),
                  pl.BlockSpec((Wb, tn), lambda j, i: (0, j)),
                  pl.BlockSpec((tn, D), lambda j, i: (g0 + j, 0)),
                  pl.BlockSpec((tn, D), lambda j, i: (g0 + nt + j, 0))],
        out_specs=pl.BlockSpec((tm, tn), lambda j, i: (i, j)),
        out_shape=jax.ShapeDtypeStruct((T, D), BF16),
        compiler_params=_cparams(("arbitrary", "arbitrary")),
        name="gated_merge",
    )(attn, sg, h, wa, wb, wg, wg)


def _out_router_kernel(m_ref, x_ref, wo_ref, g_ref, wr_ref, br_ref,
                       x1_ref, h2_ref, sel_ref, gate_ref, *, n_parts):
    E = br_ref.shape[1]
    part = m_ref.shape[0] // n_parts
    for p in range(n_parts):
        rows = pl.ds(p * part, part)
        x1 = x_ref[rows, :] + jnp.dot(m_ref[rows, :], wo_ref[...], preferred_element_type=F32)
        x1_ref[rows, :] = x1
        ms = jnp.mean(x1 * x1, axis=-1, keepdims=True)
        h2 = x1 * lax.rsqrt(ms + NORM_EPS) * g_ref[...]
        h2_ref[rows, :] = _pack_pairs(h2)
        h2_hi = h2.astype(BF16)
        h2_lo = (h2 - h2_hi.astype(F32)).astype(BF16)
        pa = jnp.dot(h2_hi, wr_ref[...], preferred_element_type=F32)
        pb = jnp.dot(h2_lo, wr_ref[:, :E], preferred_element_type=F32)
        logits = pa[:, :E] + (pa[:, E:] + pb) + br_ref[...]
        lane = lax.broadcasted_iota(I32, logits.shape, 1)
        work = logits
        sel = jnp.zeros(logits.shape, F32)
        num = jnp.zeros(logits.shape, F32)
        denom = jnp.zeros((part, 1), F32)
        m0 = None
        for _ in range(TOP_K):
            m = jnp.max(work, axis=-1, keepdims=True)
            idx = jnp.min(jnp.where(work == m, lane, E), axis=-1, keepdims=True)
            onehot = lane == idx
            if m0 is None:
                m0 = m
            e = jnp.exp(m - m0)
            sel = jnp.where(onehot, 1.0, sel)
            num = jnp.where(onehot, e, num)
            denom = denom + e
            work = jnp.where(onehot, -jnp.inf, work)
        sel_ref[rows, :] = sel
        gate_ref[rows, :] = num / denom


def _out_router(merged, x, wo, g_ffn, w_router, b_router, to=512, n_parts=2):
    T, D = x.shape
    E = w_router.shape[1]
    w_hi = w_router.astype(BF16)
    w_lo = (w_router - w_hi.astype(F32)).astype(BF16)
    w_router = jnp.concatenate([w_hi, w_lo], axis=1)
    row = lambda i: (i, 0)
    fixed = lambda i: (0, 0)
    return pl.pallas_call(
        functools.partial(_out_router_kernel, n_parts=n_parts),
        grid=(T // to,),
        in_specs=[pl.BlockSpec((to, D), row), pl.BlockSpec((to, D), row),
                  pl.BlockSpec((D, D), fixed), pl.BlockSpec((1, D), fixed),
                  pl.BlockSpec((D, 2 * E), fixed), pl.BlockSpec((1, E), fixed)],
        out_specs=[pl.BlockSpec((to, D), row), pl.BlockSpec((to, D // 2), row),
                   pl.BlockSpec((to, E), row), pl.BlockSpec((to, E), row)],
        out_shape=[jax.ShapeDtypeStruct((T, D), F32), jax.ShapeDtypeStruct((T, D // 2), jnp.uint32),
                   jax.ShapeDtypeStruct((T, E), F32), jax.ShapeDtypeStruct((T, E), F32)],
        compiler_params=_cparams(("parallel",)),
        name="out_router",
    )(merged, x, wo, g_ffn.reshape(1, D), w_router, b_router.reshape(1, E))


def _routing_kernel(sel_ref, gate_ref, dest_ref, g4_ref, nblk_ref, start_ref, rank_ref, *, tile):
    T, E = sel_ref.shape
    nt = T // tile
    r = lax.broadcasted_iota(I32, (tile, tile), 0)
    c = lax.broadcasted_iota(I32, (tile, tile), 1)
    strict_lower = (c < r).astype(BF16)
    er = lax.broadcasted_iota(I32, (E, E), 0)
    ec = lax.broadcasted_iota(I32, (E, E), 1)
    strict_upper = (er < ec).astype(BF16)

    def pass1(t, carry):
        rows = pl.ds(pl.multiple_of(t * tile, tile), tile)
        a = sel_ref[rows, :]
        rank_ref[rows, :] = jnp.dot(strict_lower, a.astype(BF16), preferred_element_type=F32) + carry
        return carry + jnp.sum(a, axis=0, keepdims=True)

    counts = lax.fori_loop(0, nt, pass1, jnp.zeros((1, E), F32))
    nblk = jnp.floor((counts + (ROW_BLOCK - 1)) * (1.0 / ROW_BLOCK))
    start_blk = jnp.dot(nblk.astype(BF16), strict_upper, preferred_element_type=F32)
    nblk_ref[...] = nblk.astype(I32)
    start_ref[...] = start_blk.astype(I32)
    start_row = start_blk * float(ROW_BLOCK)
    lane = lax.broadcasted_iota(I32, (tile, 128), 1)

    def pass2(t, _):
        rows = pl.ds(pl.multiple_of(t * tile, tile), tile)
        a = sel_ref[rows, :]
        g = gate_ref[rows, :]
        dest_e = rank_ref[rows, :] + start_row
        slot = jnp.dot(a.astype(BF16), strict_upper, preferred_element_type=F32)
        d4 = jnp.zeros((tile, 128), F32)
        g4 = jnp.zeros((tile, 128), F32)
        for s in range(TOP_K):
            pick = (a > 0.5) & (slot == float(s))
            d4 = jnp.where(lane == s, jnp.sum(jnp.where(pick, dest_e, 0.0), axis=-1, keepdims=True), d4)
            g4 = jnp.where(lane == s, jnp.sum(jnp.where(pick, g, 0.0), axis=-1, keepdims=True), g4)
        dest_ref[rows, :] = d4.astype(I32)
        g4_ref[rows, :] = g4
        return 0

    lax.fori_loop(0, nt, pass2, 0)


def _routing(sel, gate, tile=256):
    T, E = sel.shape
    return pl.pallas_call(
        functools.partial(_routing_kernel, tile=tile),
        out_shape=[jax.ShapeDtypeStruct((T, 128), I32), jax.ShapeDtypeStruct((T, 128), F32),
                   jax.ShapeDtypeStruct((1, E), I32), jax.ShapeDtypeStruct((1, E), I32)],
        scratch_shapes=[pltpu.VMEM((T, E), F32)],
        compiler_params=pltpu.CompilerParams(vmem_limit_bytes=VMEM_LIMIT_BYTES),
        name="routing_ranks",
    )(sel, gate)


SC_CORES = 2
SC_SUBCORES = 16
SC_CHUNK = 64


def _sc_gather_rows(table, idx):
    n = idx.shape[0]
    W = table.shape[1]
    n_workers = SC_CORES * SC_SUBCORES
    per_worker = n // n_workers
    assert per_worker * n_workers == n and per_worker % SC_CHUNK == 0
    mesh = plsc.VectorSubcoreMesh(core_axis_name="c", subcore_axis_name="s",
                                  num_cores=SC_CORES, num_subcores=SC_SUBCORES)

    @functools.partial(
        pl.kernel, mesh=mesh,
        out_type=jax.ShapeDtypeStruct((n, W), table.dtype),
        scratch_types=[pltpu.VMEM((SC_CHUNK,), I32), pltpu.VMEM((SC_CHUNK, W), table.dtype),
                       pltpu.SemaphoreType.DMA],
        name="sc_gather_rows",
    )
    def gather(table_hbm, idx_hbm, out_hbm, idx_v, rows_v, sem):
        wid = lax.axis_index("s") * SC_CORES + lax.axis_index("c")
        base = wid * per_worker

        @pl.loop(0, per_worker // SC_CHUNK)
        def _(c):
            off = pl.multiple_of(base + c * SC_CHUNK, 8)
            pltpu.sync_copy(idx_hbm.at[pl.ds(off, SC_CHUNK)], idx_v)
            pltpu.async_copy(table_hbm.at[idx_v], rows_v, sem).wait()
            pltpu.sync_copy(rows_v, out_hbm.at[pl.ds(off, SC_CHUNK)])

    return gather(table, idx)


def _sc_scatter_rows(rows, idx, n_out, n_slots):
    T, W = rows.shape
    n_workers = SC_CORES * SC_SUBCORES
    per_worker = T // n_workers
    assert per_worker * n_workers == T and per_worker % SC_CHUNK == 0
    mesh = plsc.VectorSubcoreMesh(core_axis_name="c", subcore_axis_name="s",
                                  num_cores=SC_CORES, num_subcores=SC_SUBCORES)

    @functools.partial(
        pl.kernel, mesh=mesh,
        out_type=jax.ShapeDtypeStruct((n_out, W), rows.dtype),
        scratch_types=[pltpu.VMEM((SC_CHUNK,), I32), pltpu.VMEM((SC_CHUNK, W), rows.dtype)],
        name="sc_scatter_rows",
    )
    def scatter(rows_hbm, idx_hbm, out_hbm, idx_v, rows_v):
        wid = lax.axis_index("s") * SC_CORES + lax.axis_index("c")
        base = wid * per_worker

        @pl.loop(0, per_worker // SC_CHUNK)
        def _(c):
            off = pl.multiple_of(base + c * SC_CHUNK, 8)
            pltpu.sync_copy(rows_hbm.at[pl.ds(off, SC_CHUNK)], rows_v)
            for k in range(n_slots):
                pltpu.sync_copy(idx_hbm.at[pl.ds(pl.multiple_of(k * T + off, 8), SC_CHUNK)], idx_v)
                pltpu.sync_copy(rows_v, out_hbm.at[idx_v])

    return scatter(rows, idx)


def _expert_kernel(ie_ref, ib_ref, ins_ref,
                   xs_hbm, wup_ref, bup_ref, wdn_ref, bdn_ref, perm_ref, ys_hbm,
                   xg_ref, acc_ref, yst_ref, wupb_ref, wdnb_ref, gsem, osem, *, n_ff_tiles):
    i = pl.program_id(0)
    j = pl.program_id(1)
    n_items = pl.num_programs(0)
    nsub = ins_ref[i]
    slot = i % 2

    def for_regions(n_blocks, fn):
        first = jnp.int32(0)
        for count in REGION_BLOCKS:
            present = (n_blocks & count) != 0
            pl.when(present)(functools.partial(fn, first, count))
            first = first + jnp.where(present, count, 0)

    def rows_of(first, count):
        return pl.ds(pl.multiple_of(first * ROW_BLOCK, ROW_BLOCK), count * ROW_BLOCK)

    def in_copy(item, dst_slot, first, count):
        src = pl.multiple_of((ib_ref[item] + first) * ROW_BLOCK, ROW_BLOCK)
        return pltpu.make_async_copy(xs_hbm.at[pl.ds(src, count * ROW_BLOCK)],
                                     xg_ref.at[dst_slot, rows_of(first, count)], gsem.at[dst_slot])

    def fetch_item(item, n_blocks, dst_slot):
        for_regions(n_blocks, lambda first, count: in_copy(item, dst_slot, first, count).start())

    @pl.when(j == 0)
    def _():
        @pl.when(i == 0)
        def _():
            fetch_item(0, nsub, 0)

        nxt = jnp.minimum(i + 1, n_items - 1)
        fetch_item(nxt, jnp.where(i + 1 < n_items, ins_ref[nxt], 0), 1 - slot)

        def arrive(first, count):
            in_copy(i, slot, first, count).wait()
            acc_ref[rows_of(first, count), :] = jnp.broadcast_to(
                bdn_ref[0], (count * ROW_BLOCK, acc_ref.shape[1]))

        for_regions(nsub, arrive)

    @pl.when(nsub > 0)
    def _():
        bup = bup_ref[0]
        perm = perm_ref[...]
        half = perm.shape[0] // 2

        def run_blocks(first, count):
            if first == 0:
                wup = wup_ref[0].astype(BF16)
                wdn = wdn_ref[0].astype(BF16)
                wupb_ref[...] = wup
                wdnb_ref[...] = wdn
            else:
                wup = wupb_ref[...]
                wdn = wdnb_ref[...]
            rows = pl.ds(first * ROW_BLOCK, count * ROW_BLOCK)
            xb = _unpack_pairs(xg_ref[slot, rows, :]).astype(BF16)
            gu = (jnp.dot(xb, wup, preferred_element_type=F32) + bup).astype(BF16)
            glu_parts, lin_parts = [], []
            for p in range(gu.shape[1] // perm.shape[0]):
                gp = jnp.dot(gu[:, p * perm.shape[0]:(p + 1) * perm.shape[0]], perm,
                             preferred_element_type=F32)
                glu_parts.append(gp[:, :half])
                lin_parts.append(gp[:, half:])
            x_glu = jnp.minimum(jnp.concatenate(glu_parts, axis=1), SWIGLU_LIMIT)
            x_lin = jnp.clip(jnp.concatenate(lin_parts, axis=1), -SWIGLU_LIMIT, SWIGLU_LIMIT)
            act = x_glu * jax.nn.sigmoid(SWIGLU_ALPHA * x_glu) * (x_lin + 1.0)
            acc_ref[rows, :] += jnp.dot(act.astype(BF16), wdn, preferred_element_type=F32)

        for n_blocks, plan in COMPUTE_PLANS.items():
            for first, count in plan:
                shared = [n for n, p in COMPUTE_PLANS.items() if (first, count) in p]
                if n_blocks == shared[0]:
                    cond = functools.reduce(jnp.logical_or, [nsub == n for n in shared])
                    pl.when(cond)(functools.partial(run_blocks, first, count))

    def out_copy(item, first, count):
        dst = pl.multiple_of((ib_ref[item] + first) * ROW_BLOCK, ROW_BLOCK)
        return pltpu.make_async_copy(yst_ref.at[rows_of(first, count)],
                                     ys_hbm.at[pl.ds(dst, count * ROW_BLOCK)], osem)

    @pl.when(j == n_ff_tiles - 1)
    def _():
        prev = jnp.maximum(i - 1, 0)
        for_regions(jnp.where(i > 0, ins_ref[prev], 0),
                    lambda first, count: out_copy(prev, first, count).wait())

        def leave(first, count):
            rows = rows_of(first, count)
            yst_ref[rows, :] = _pack_pairs(acc_ref[rows, :])
            out_copy(i, first, count).start()

        for_regions(nsub, leave)

        @pl.when(i == n_items - 1)
        def _():
            for_regions(nsub, lambda first, count: out_copy(i, first, count).wait())


def _experts(xs, item_e, item_b, item_n, n_active, w_up, b_up, w_down, b_down):
    n_rows, Dp = xs.shape
    D = 2 * Dp
    E, _, F2 = w_up.shape
    F = F2 // 2
    J = F // FF_TILE
    half = 128
    perm = np.zeros((2 * half, 2 * half), np.float32)
    perm[2 * np.arange(half), np.arange(half)] = 1.0
    perm[2 * np.arange(half) + 1, half + np.arange(half)] = 1.0

    def jj(i, j, ins):
        return jnp.where(ins[i] > 0, j, J - 1)

    grid_spec = pltpu.PrefetchScalarGridSpec(
        num_scalar_prefetch=3,
        grid=(n_active, J),
        in_specs=[pl.BlockSpec(memory_space=pl.ANY),
                  pl.BlockSpec((1, D, 2 * FF_TILE), lambda i, j, ie, ib, ins: (ie[i], 0, jj(i, j, ins))),
                  pl.BlockSpec((1, 1, 2 * FF_TILE), lambda i, j, ie, ib, ins: (ie[i], 0, jj(i, j, ins))),
                  pl.BlockSpec((1, FF_TILE, D), lambda i, j, ie, ib, ins: (ie[i], jj(i, j, ins), 0)),
                  pl.BlockSpec((1, 1, D), lambda i, j, ie, ib, ins: (ie[i], 0, 0)),
                  pl.BlockSpec((2 * half, 2 * half), lambda i, j, ie, ib, ins: (0, 0))],
        out_specs=pl.BlockSpec(memory_space=pl.ANY),
        scratch_shapes=[pltpu.VMEM((2, ITEM_ROWS, Dp), jnp.uint32),
                        pltpu.VMEM((ITEM_ROWS, D), F32),
                        pltpu.VMEM((ITEM_ROWS, Dp), jnp.uint32),
                        pltpu.VMEM((D, 2 * FF_TILE), BF16),
                        pltpu.VMEM((FF_TILE, D), BF16),
                        pltpu.SemaphoreType.DMA((2,)), pltpu.SemaphoreType.DMA(())],
    )
    return pl.pallas_call(
        functools.partial(_expert_kernel, n_ff_tiles=J),
        grid_spec=grid_spec,
        out_shape=jax.ShapeDtypeStruct((n_rows, Dp), jnp.uint32),
        compiler_params=_cparams(("arbitrary", "arbitrary")),
        name="expert_ffn",
    )(item_e, item_b, item_n,
      xs, w_up, b_up.reshape(E, 1, F2), w_down, b_down.reshape(E, 1, D), jnp.asarray(perm, BF16))


def _combine_kernel(*refs):
    yk_refs, (g4_ref, x1_ref, g_ref), o_ref = refs[:TOP_K], refs[TOP_K:TOP_K + 3], refs[-1]
    y = x1_ref[...]
    g4 = g4_ref[...]
    for k in range(TOP_K):
        y = y + g4[:, k:k + 1] * _unpack_pairs(yk_refs[k][...])
    ms = jnp.mean(y * y, axis=-1, keepdims=True)
    o_ref[...] = y * lax.rsqrt(ms + NORM_EPS) * g_ref[...]


def _combine(yk, g4, x1, g_final, part, n_parts, out_so_far, tc=256):
    T, D = x1.shape
    nt = T // n_parts // tc
    first = part * nt
    slot_specs = [pl.BlockSpec((tc, D // 2), functools.partial(lambda i, k: (k * nt + i, 0), k=k))
                  for k in range(TOP_K)]
    operands = [*([yk] * TOP_K), g4, x1, g_final.reshape(1, D)]
    in_specs = slot_specs + [pl.BlockSpec((tc, 128), lambda i: (first + i, 0)),
                             pl.BlockSpec((tc, D), lambda i: (first + i, 0)),
                             pl.BlockSpec((1, D), lambda i: (0, 0))]
    aliases = {}
    if out_so_far is not None:
        aliases = {len(operands): 0}
        operands.append(out_so_far)
        in_specs.append(pl.BlockSpec(memory_space=pl.ANY))
    return pl.pallas_call(
        _combine_kernel,
        grid=(nt,),
        in_specs=in_specs,
        out_specs=pl.BlockSpec((tc, D), lambda i: (first + i, 0)),
        out_shape=jax.ShapeDtypeStruct((T, D), F32),
        input_output_aliases=aliases,
        compiler_params=_cparams(("parallel",)),
        name="combine_norm",
    )(*operands)


def _work_items(nblk, start_blk, n_items):
    E = nblk.shape[0]
    per_e = (nblk + ITEM_BLOCKS - 1) // ITEM_BLOCKS
    ends = jnp.cumsum(per_e)
    total = ends[-1]
    idx = jnp.arange(n_items, dtype=I32)
    e = jnp.minimum(jnp.searchsorted(ends, idx, side="right"), E - 1).astype(I32)
    local = idx - (ends[e] - per_e[e])
    active = idx < total
    last_e = e[jnp.maximum(total - 1, 0)]
    item_e = jnp.where(active, e, last_e).astype(I32)
    item_b = jnp.where(active, start_blk[e] + local * ITEM_BLOCKS, 0).astype(I32)
    item_n = jnp.where(active, jnp.clip(nblk[e] - local * ITEM_BLOCKS, 0, ITEM_BLOCKS), 0).astype(I32)
    return item_e, item_b, item_n, jnp.maximum(total, 1).astype(I32)


def kernel(x, g_mix, w_in, b_forget, g_v_ln, b_v_ln, w_spatial, b_spatial, w_branch_attn, w_branch_gmlp, w_out, g_ffn, w_router, b_router, w_expert_up, b_expert_up, w_expert_down, b_expert_down, g_final):
    B, S, D = x.shape
    T = B * S
    n_heads = b_forget.shape[0]
    attn_w = n_heads * HEAD_DIM
    gmlp_w = g_v_ln.shape[0]
    E = w_router.shape[1]
    off_f = 3 * attn_w
    off_z = off_f + n_heads
    off_g = off_z + 2 * gmlp_w

    x2 = x.reshape(T, D)
    w_in_t = w_in.T
    wqkv = _rows_bf16(w_in_t, 0, off_f)
    wzg = _rows_bf16(w_in_t, off_z, w_in.shape[1] - off_z)
    wa = w_branch_attn.astype(BF16)
    wb = w_branch_gmlp.astype(BF16)
    wo = w_out.astype(BF16)

    qkv, h = _norm_project(x2, g_mix, wqkv, 3 * attn_w, attn_w, LOG2E / math.sqrt(HEAD_DIM))
    c_row = _forget_cumsum(h, w_in_t, off_f, b_forget, B, S)
    attn = _attention(qkv, c_row, B, S, n_heads)
    sg = _gmlp(h, wzg, g_v_ln, b_v_ln, w_spatial, b_spatial)
    merged = _merge(attn, sg, h, wa, wb, wzg, off_g - off_z)
    x1, h2p, sel, gate = _out_router(merged, x2, wo, g_ffn, w_router, b_router)

    dest4, g4, nblk, start_blk = _routing(sel, gate)
    n_rows = T * TOP_K + E * ROW_BLOCK
    dest_slots = dest4[:, :TOP_K].T.reshape(TOP_K * T)
    max_blocks = n_rows // ROW_BLOCK
    n_items = E + -(-(max_blocks - E) // ITEM_BLOCKS) + 1
    item_e, item_b, item_n, n_active = _work_items(nblk[0], start_blk[0], n_items)
    xs = _sc_scatter_rows(h2p, dest_slots, n_rows, TOP_K)
    ys = _experts(xs, item_e, item_b, item_n, n_active,
                  w_expert_up, b_expert_up, w_expert_down, b_expert_down)
    out = None
    t_part = T // COMBINE_PARTS
    for p in range(COMBINE_PARTS):
        idx = dest4[p * t_part:(p + 1) * t_part, :TOP_K].T.reshape(TOP_K * t_part)
        yk = _sc_gather_rows(ys, idx)
        out = _combine(yk, g4, x1, g_final, p, COMBINE_PARTS, out)
    return out.reshape(B, S, D)
```

```python
import functools
import math

import jax
import jax.numpy as jnp
import numpy as np
from jax import lax
from jax.experimental import pallas as pl
from jax.experimental.pallas import tpu as pltpu
from jax.experimental.pallas import tpu_sc as plsc

F32 = jnp.float32
BF16 = jnp.bfloat16
I32 = jnp.int32

NORM_EPS = 1e-5
LANES = 128
SUBLANES = 8
NT_DIMS = (((1,), (1,)), ((), ()))
HEAD_DIM = 128
CHUNK = 128
GROUP_DIM = 128
TOP_K = 4
SWIGLU_ALPHA = 1.702
SWIGLU_LIMIT = 7.0
LOG2E = math.log2(math.e)

VMEM_LIMIT_BYTES = 56 * 1024 * 1024

ROW_BLOCK = 128
ITEM_BLOCKS = 10
ITEM_ROWS = ITEM_BLOCKS * ROW_BLOCK
REGION_BLOCKS = (8, 4, 2, 1)
assert sum(REGION_BLOCKS) >= ITEM_BLOCKS
COMPUTE_PLANS = {**{n: ((0, n),) for n in range(4, ITEM_BLOCKS + 1)},
                 3: ((0, 2), (2, 1)), 2: ((0, 2),), 1: ((0, 1),)}
assert set(COMPUTE_PLANS) == set(range(1, ITEM_BLOCKS + 1))
FF_TILE = 256
COMBINE_PARTS = 1


def _cparams(sem, **kw):
    return pltpu.CompilerParams(dimension_semantics=sem, vmem_limit_bytes=VMEM_LIMIT_BYTES, **kw)


def _pack_pairs(x):
    c = x.shape[1] // 2
    hi = lax.bitcast_convert_type(x[:, :c].astype(BF16).astype(F32), jnp.uint32)
    lo = lax.bitcast_convert_type(x[:, c:].astype(BF16).astype(F32), jnp.uint32)
    return hi | (lo >> 16)


def _unpack_pairs(w):
    hi = lax.bitcast_convert_type(w & jnp.uint32(0xFFFF0000), F32)
    lo = lax.bitcast_convert_type(w << 16, F32)
    return jnp.concatenate([hi, lo], axis=1)


def _norm_proj_kernel(x_ref, g_ref, w_ref, o_ref, h_ref, *, n_scaled, scale):
    j = pl.program_id(1)

    @pl.when(j == 0)
    def _():
        x = x_ref[...]
        ms = jnp.mean(x * x, axis=-1, keepdims=True)
        h_ref[...] = (x * lax.rsqrt(ms + NORM_EPS) * g_ref[...]).astype(h_ref.dtype)

    acc = lax.dot_general(h_ref[...], w_ref[...].astype(BF16), NT_DIMS, preferred_element_type=F32)
    o_ref[...] = (acc * jnp.where(j < n_scaled, scale, 1.0)).astype(o_ref.dtype)


def _norm_project(x, g, wt, n_cols, n_scaled_cols, scale, tm=1024, tn=512):
    T, D = x.shape
    return pl.pallas_call(
        functools.partial(_norm_proj_kernel, n_scaled=n_scaled_cols // tn, scale=scale),
        grid=(T // tm, n_cols // tn),
        in_specs=[pl.BlockSpec((tm, D), lambda i, j: (i, 0)),
                  pl.BlockSpec((1, D), lambda i, j: (0, 0)),
                  pl.BlockSpec((tn, D), lambda i, j: (j, 0))],
        out_specs=[pl.BlockSpec((tm, tn), lambda i, j: (i, j)),
                   pl.BlockSpec((tm, D), lambda i, j: (i, 0))],
        out_shape=[jax.ShapeDtypeStruct((T, n_cols), BF16), jax.ShapeDtypeStruct((T, D), BF16)],
        compiler_params=_cparams(("arbitrary", "arbitrary")),
        name="norm_qkv_proj",
    )(x, g.reshape(1, D), wt)


def _cast_kernel(w_ref, o_ref):
    o_ref[...] = w_ref[...].astype(o_ref.dtype)


def _rows_bf16(wt, start, n_rows, tr=512):
    D = wt.shape[1]
    assert start % SUBLANES == 0 and n_rows % tr == 0
    return pl.pallas_call(
        _cast_kernel,
        grid=(n_rows // tr,),
        in_specs=[pl.BlockSpec((pl.Element(tr), pl.Element(D)),
                               lambda i: (pl.multiple_of(start + i * tr, SUBLANES), 0))],
        out_specs=pl.BlockSpec((tr, D), lambda i: (i, 0)),
        out_shape=jax.ShapeDtypeStruct((n_rows, D), BF16),
        compiler_params=_cparams(("parallel",)),
        name="rows_bf16",
    )(wt)


def _forget_kernel(h_ref, wft_ref, bf_ref, c_ref):
    ft = lax.dot_general(wft_ref[...].astype(BF16), h_ref[...], NT_DIMS,
                         preferred_element_type=F32)
    c = jax.nn.log_sigmoid(ft + bf_ref[...])
    S = c.shape[1]
    lane = lax.broadcasted_iota(I32, c.shape, 1)
    shift = 1
    while shift < S:
        c = c + jnp.where(lane >= shift, pltpu.roll(c, shift, axis=1), 0.0)
        shift *= 2
    c_ref[0] = c * LOG2E


def _forget_cumsum(h, wt, f_off, b_forget, B, S):
    T, D = h.shape
    H = b_forget.shape[0]
    assert f_off % H == 0 and H % SUBLANES == 0
    return pl.pallas_call(
        _forget_kernel,
        grid=(B,),
        in_specs=[pl.BlockSpec((S, D), lambda b: (b, 0)),
                  pl.BlockSpec((H, D), lambda b: (f_off // H, 0)),
                  pl.BlockSpec((H, 1), lambda b: (0, 0))],
        out_specs=pl.BlockSpec((1, H, S), lambda b: (b, 0, 0)),
        out_shape=jax.ShapeDtypeStruct((B, H, S), F32),
        compiler_params=_cparams(("parallel",)),
        name="forget_cumsum",
    )(h, wt, b_forget.reshape(H, 1))


def _attn_kernel(q_ref, k_ref, v_ref, crow_ref, o_ref, vaug_ref, m_ref, acc_ref, *, n_heads, tq):
    i = pl.program_id(1)

    @pl.when(i == 0)
    def _():
        ones = jnp.ones((v_ref.shape[0], HEAD_DIM), BF16)
        for h in range(n_heads):
            vaug_ref[h, :, :HEAD_DIM] = v_ref[:, h * HEAD_DIM:(h + 1) * HEAD_DIM]
            vaug_ref[h, :, HEAD_DIM:] = ones

    m_ref[...] = jnp.full(m_ref.shape, -jnp.inf, F32)
    acc_ref[...] = jnp.zeros(acc_ref.shape, F32)
    row = lax.broadcasted_iota(I32, (tq, tq), 0)
    col = lax.broadcasted_iota(I32, (tq, tq), 1)
    causal = col <= row

    def step(j, masked):
        keys = pl.ds(pl.multiple_of(j * tq, tq), tq)
        for h in range(n_heads):
            hs = slice(h * HEAD_DIM, (h + 1) * HEAD_DIM)
            s = lax.dot_general(q_ref[:, hs], k_ref[keys, hs], (((1,), (1,)), ((), ())),
                                preferred_element_type=F32) - crow_ref[0, h, j]
            if masked:
                s = jnp.where(causal, s, -jnp.inf)
            m_old = m_ref[h]
            m_new = jnp.maximum(m_old, jnp.max(s, axis=-1, keepdims=True))
            alpha = jnp.exp2(m_old - m_new)
            p = jnp.exp2(s - jnp.concatenate([m_new] * (tq // HEAD_DIM), axis=1))
            m_ref[h] = m_new
            pv = jnp.dot(p.astype(BF16), vaug_ref[h, keys, :], preferred_element_type=F32)
            acc_ref[h] = jnp.concatenate([alpha, alpha], axis=1) * acc_ref[h] + pv

    def body(j, _):
        step(j, False)
        return 0

    lax.fori_loop(0, i, body, 0)
    step(i, True)
    for h in range(n_heads):
        acc = acc_ref[h]
        o_ref[:, h * HEAD_DIM:(h + 1) * HEAD_DIM] = (acc[:, :HEAD_DIM] / acc[:, HEAD_DIM:]).astype(o_ref.dtype)


def _attention(qkv, c_row, B, S, n_heads, tq=256):
    T = qkv.shape[0]
    W = n_heads * HEAD_DIM
    nq = S // tq
    c_row5 = c_row.reshape(B, n_heads, nq, 1, tq)
    return pl.pallas_call(
        functools.partial(_attn_kernel, n_heads=n_heads, tq=tq),
        grid=(B, nq),
        in_specs=[pl.BlockSpec((tq, W), lambda b, i: (b * nq + i, 0)),
                  pl.BlockSpec((S, W), lambda b, i: (b, 1)),
                  pl.BlockSpec((S, W), lambda b, i: (b, 2)),
                  pl.BlockSpec((1, n_heads, nq, 1, tq), lambda b, i: (b, 0, 0, 0, 0))],
        out_specs=pl.BlockSpec((tq, W), lambda b, i: (b * nq + i, 0)),
        out_shape=jax.ShapeDtypeStruct((T, W), BF16),
        scratch_shapes=[pltpu.VMEM((n_heads, S, 2 * HEAD_DIM), BF16),
                        pltpu.VMEM((n_heads, tq, HEAD_DIM), F32),
                        pltpu.VMEM((n_heads, tq, 2 * HEAD_DIM), F32)],
        compiler_params=_cparams(("arbitrary", "arbitrary")),
        name="fox_attention",
    )(qkv, qkv, qkv, c_row5)


def _gmlp_kernel(h_ref, wz_ref, g_ref, b_ref, ws_ref, bst_ref, o_ref, *, n_groups):
    z = lax.dot_general(h_ref[...], wz_ref[...], NT_DIMS, preferred_element_type=F32)
    z = 0.5 * z * (1.0 + lax.erf(z * (1.0 / math.sqrt(2.0))))
    W = z.shape[1] // 2
    u = z[:, :W]
    v = z[:, W:]
    mu = jnp.mean(v, axis=-1, keepdims=True)
    var = jnp.mean(jnp.square(v - mu), axis=-1, keepdims=True)
    vn = (v - mu) * lax.rsqrt(var + NORM_EPS) * g_ref[...] + b_ref[...]
    row = lax.broadcasted_iota(I32, (CHUNK, CHUNK), 0)
    col = lax.broadcasted_iota(I32, (CHUNK, CHUNK), 1)
    tril = col <= row
    tg = z.shape[0]
    for g in range(n_groups):
        gs = slice(g * GROUP_DIM, (g + 1) * GROUP_DIM)
        wg = jnp.where(tril, ws_ref[g], 0.0).astype(BF16)
        bias = bst_ref[:, g:g + 1]
        for c in range(tg // CHUNK):
            cs = slice(c * CHUNK, (c + 1) * CHUNK)
            mixed = jnp.dot(wg, vn[cs, gs].astype(BF16), preferred_element_type=F32) + bias
            o_ref[cs, gs] = (u[cs, gs] * mixed).astype(o_ref.dtype)


def _gmlp(h, wz, g_v_ln, b_v_ln, w_spatial, b_spatial, tg=512):
    T, D = h.shape
    W = g_v_ln.shape[0]
    W2 = 2 * W
    G = w_spatial.shape[0]
    return pl.pallas_call(
        functools.partial(_gmlp_kernel, n_groups=G),
        grid=(T // tg,),
        in_specs=[pl.BlockSpec((tg, D), lambda i: (i, 0)),
                  pl.BlockSpec((W2, D), lambda i: (0, 0)),
                  pl.BlockSpec((1, W), lambda i: (0, 0)),
                  pl.BlockSpec((1, W), lambda i: (0, 0)),
                  pl.BlockSpec((G, CHUNK, CHUNK), lambda i: (0, 0, 0)),
                  pl.BlockSpec((CHUNK, G), lambda i: (0, 0))],
        out_specs=pl.BlockSpec((tg, W), lambda i: (i, 0)),
        out_shape=jax.ShapeDtypeStruct((T, W), BF16),
        compiler_params=_cparams(("parallel",)),
        name="gmlp",
    )(h, wz, g_v_ln.reshape(1, W), b_v_ln.reshape(1, W), w_spatial, b_spatial.T)


def _merge_kernel(attn_ref, sg_ref, h_ref, wa_ref, wb_ref, wga_ref, wgb_ref, o_ref):
    h = h_ref[...]
    a = jnp.dot(attn_ref[...], wa_ref[...], preferred_element_type=F32)
    ga = lax.dot_general(h, wga_ref[...], NT_DIMS, preferred_element_type=F32)
    m = jax.nn.sigmoid(ga) * a
    b = jnp.dot(sg_ref[...], wb_ref[...], preferred_element_type=F32)
    gb = lax.dot_general(h, wgb_ref[...], NT_DIMS, preferred_element_type=F32)
    o_ref[...] = (m + jax.nn.sigmoid(gb) * b).astype(o_ref.dtype)


def _merge(attn, sg, h, wa, wb, wg, g_off, tm=512, tn=512):
    T, D = h.shape
    Wa = attn.shape[1]
    Wb = sg.shape[1]
    nt = D // tn
    g0 = g_off // tn
    return pl.pallas_call(
        _merge_kernel,
        grid=(nt, T // tm),
        in_specs=[pl.BlockSpec((tm, Wa), lambda j, i: (i, 0)),
                  pl.BlockSpec((tm, Wb), lambda j, i: (i, 0)),
                  pl.BlockSpec((tm, D), lambda j, i: (i, 0)),
                  pl.BlockSpec((Wa, tn), lambda j, i: (0, j)),
                  pl.BlockSpec((Wb, tn), lambda j, i: (0, j)),
                  pl.BlockSpec((tn, D), lambda j, i: (g0 + j, 0)),
                  pl.BlockSpec((tn, D), lambda j, i: (g0 + nt + j, 0))],
        out_specs=pl.BlockSpec((tm, tn), lambda j, i: (i, j)),
        out_shape=jax.ShapeDtypeStruct((T, D), BF16),
        compiler_params=_cparams(("arbitrary", "arbitrary")),
        name="gated_merge",
    )(attn, sg, h, wa, wb, wg, wg)


def _out_router_kernel(m_ref, x_ref, wo_ref, g_ref, wr_ref, br_ref,
                       x1_ref, h2_ref, sel_ref, gate_ref, *, n_parts):
    E = br_ref.shape[1]
    part = m_ref.shape[0] // n_parts
    for p in range(n_parts):
        rows = pl.ds(p * part, part)
        x1 = x_ref[rows, :] + jnp.dot(m_ref[rows, :], wo_ref[...], preferred_element_type=F32)
        x1_ref[rows, :] = x1
        ms = jnp.mean(x1 * x1, axis=-1, keepdims=True)
        h2 = x1 * lax.rsqrt(ms + NORM_EPS) * g_ref[...]
        h2_ref[rows, :] = _pack_pairs(h2)
        h2_hi = h2.astype(BF16)
        h2_lo = (h2 - h2_hi.astype(F32)).astype(BF16)
        pa = jnp.dot(h2_hi, wr_ref[...], preferred_element_type=F32)
        pb = jnp.dot(h2_lo, wr_ref[:, :E], preferred_element_type=F32)
        logits = pa[:, :E] + (pa[:, E:] + pb) + br_ref[...]
        lane = lax.broadcasted_iota(I32, logits.shape, 1)
        work = logits
        sel = jnp.zeros(logits.shape, F32)
        num = jnp.zeros(logits.shape, F32)
        denom = jnp.zeros((part, 1), F32)
        m0 = None
        for _ in range(TOP_K):
            m = jnp.max(work, axis=-1, keepdims=True)
            idx = jnp.argmax(work, axis=-1, keepdims=True).astype(I32)
            onehot = lane == idx
            if m0 is None:
                m0 = m
            e = jnp.exp(m - m0)
            sel = jnp.where(onehot, 1.0, sel)
            num = jnp.where(onehot, e, num)
            denom = denom + e
            work = jnp.where(onehot, -jnp.inf, work)
        sel_ref[rows, :] = sel
        gate_ref[rows, :] = num / denom


def _out_router(merged, x, wo, g_ffn, w_router, b_router, to=512, n_parts=2):
    T, D = x.shape
    E = w_router.shape[1]
    w_hi = w_router.astype(BF16)
    w_lo = (w_router - w_hi.astype(F32)).astype(BF16)
    w_router = jnp.concatenate([w_hi, w_lo], axis=1)
    row = lambda i: (i, 0)
    fixed = lambda i: (0, 0)
    return pl.pallas_call(
        functools.partial(_out_router_kernel, n_parts=n_parts),
        grid=(T // to,),
        in_specs=[pl.BlockSpec((to, D), row), pl.BlockSpec((to, D), row),
                  pl.BlockSpec((D, D), fixed), pl.BlockSpec((1, D), fixed),
                  pl.BlockSpec((D, 2 * E), fixed), pl.BlockSpec((1, E), fixed)],
        out_specs=[pl.BlockSpec((to, D), row), pl.BlockSpec((to, D // 2), row),
                   pl.BlockSpec((to, E), row), pl.BlockSpec((to, E), row)],
        out_shape=[jax.ShapeDtypeStruct((T, D), F32), jax.ShapeDtypeStruct((T, D // 2), jnp.uint32),
                   jax.ShapeDtypeStruct((T, E), F32), jax.ShapeDtypeStruct((T, E), F32)],
        compiler_params=_cparams(("parallel",)),
        name="out_router",
    )(merged, x, wo, g_ffn.reshape(1, D), w_router, b_router.reshape(1, E))


def _routing_kernel(sel_ref, gate_ref, dest_ref, g4_ref, nblk_ref, start_ref, rank_ref, *, tile):
    T, E = sel_ref.shape
    nt = T // tile
    r = lax.broadcasted_iota(I32, (tile, tile), 0)
    c = lax.broadcasted_iota(I32, (tile, tile), 1)
    strict_lower = (c < r).astype(BF16)
    er = lax.broadcasted_iota(I32, (E, E), 0)
    ec = lax.broadcasted_iota(I32, (E, E), 1)
    strict_upper = (er < ec).astype(BF16)

    def pass1(t, carry):
        rows = pl.ds(pl.multiple_of(t * tile, tile), tile)
        a = sel_ref[rows, :]
        rank_ref[rows, :] = jnp.dot(strict_lower, a.astype(BF16), preferred_element_type=F32) + carry
        return carry + jnp.sum(a, axis=0, keepdims=True)

    counts = lax.fori_loop(0, nt, pass1, jnp.zeros((1, E), F32))
    nblk = jnp.floor((counts + (ROW_BLOCK - 1)) * (1.0 / ROW_BLOCK))
    start_blk = jnp.dot(nblk.astype(BF16), strict_upper, preferred_element_type=F32)
    nblk_ref[...] = nblk.astype(I32)
    start_ref[...] = start_blk.astype(I32)
    start_row = start_blk * float(ROW_BLOCK)
    lane = lax.broadcasted_iota(I32, (tile, 128), 1)

    def pass2(t, _):
        rows = pl.ds(pl.multiple_of(t * tile, tile), tile)
        a = sel_ref[rows, :]
        g = gate_ref[rows, :]
        dest_e = rank_ref[rows, :] + start_row
        slot = jnp.dot(a.astype(BF16), strict_upper, preferred_element_type=F32)
        d4 = jnp.zeros((tile, 128), F32)
        g4 = jnp.zeros((tile, 128), F32)
        for s in range(TOP_K):
            pick = (a > 0.5) & (slot == float(s))
            d4 = jnp.where(lane == s, jnp.sum(jnp.where(pick, dest_e, 0.0), axis=-1, keepdims=True), d4)
            g4 = jnp.where(lane == s, jnp.sum(jnp.where(pick, g, 0.0), axis=-1, keepdims=True), g4)
        dest_ref[rows, :] = d4.astype(I32)
        g4_ref[rows, :] = g4
        return 0

    lax.fori_loop(0, nt, pass2, 0)


def _routing(sel, gate, tile=256):
    T, E = sel.shape
    return pl.pallas_call(
        functools.partial(_routing_kernel, tile=tile),
        out_shape=[jax.ShapeDtypeStruct((T, 128), I32), jax.ShapeDtypeStruct((T, 128), F32),
                   jax.ShapeDtypeStruct((1, E), I32), jax.ShapeDtypeStruct((1, E), I32)],
        scratch_shapes=[pltpu.VMEM((T, E), F32)],
        compiler_params=pltpu.CompilerParams(vmem_limit_bytes=VMEM_LIMIT_BYTES),
        name="routing_ranks",
    )(sel, gate)


SC_CORES = 2
SC_SUBCORES = 16
SC_CHUNK = 64


def _sc_gather_rows(table, idx):
    n = idx.shape[0]
    W = table.shape[1]
    n_workers = SC_CORES * SC_SUBCORES
    per_worker = n // n_workers
    assert per_worker * n_workers == n and per_worker % SC_CHUNK == 0
    mesh = plsc.VectorSubcoreMesh(core_axis_name="c", subcore_axis_name="s",
                                  num_cores=SC_CORES, num_subcores=SC_SUBCORES)

    @functools.partial(
        pl.kernel, mesh=mesh,
        out_type=jax.ShapeDtypeStruct((n, W), table.dtype),
        scratch_types=[pltpu.VMEM((SC_CHUNK,), I32), pltpu.VMEM((SC_CHUNK, W), table.dtype),
                       pltpu.SemaphoreType.DMA],
        name="sc_gather_rows",
    )
    def gather(table_hbm, idx_hbm, out_hbm, idx_v, rows_v, sem):
        wid = lax.axis_index("s") * SC_CORES + lax.axis_index("c")
        base = wid * per_worker

        @pl.loop(0, per_worker // SC_CHUNK)
        def _(c):
            off = pl.multiple_of(base + c * SC_CHUNK, 8)
            pltpu.sync_copy(idx_hbm.at[pl.ds(off, SC_CHUNK)], idx_v)
            pltpu.async_copy(table_hbm.at[idx_v], rows_v, sem).wait()
            pltpu.sync_copy(rows_v, out_hbm.at[pl.ds(off, SC_CHUNK)])

    return gather(table, idx)


def _sc_scatter_rows(rows, idx, n_out, n_slots):
    T, W = rows.shape
    n_workers = SC_CORES * SC_SUBCORES
    per_worker = T // n_workers
    assert per_worker * n_workers == T and per_worker % SC_CHUNK == 0
    mesh = plsc.VectorSubcoreMesh(core_axis_name="c", subcore_axis_name="s",
                                  num_cores=SC_CORES, num_subcores=SC_SUBCORES)

    @functools.partial(
        pl.kernel, mesh=mesh,
        out_type=jax.ShapeDtypeStruct((n_out, W), rows.dtype),
        scratch_types=[pltpu.VMEM((SC_CHUNK,), I32), pltpu.VMEM((SC_CHUNK, W), rows.dtype)],
        name="sc_scatter_rows",
    )
    def scatter(rows_hbm, idx_hbm, out_hbm, idx_v, rows_v):
        wid = lax.axis_index("s") * SC_CORES + lax.axis_index("c")
        base = wid * per_worker

        @pl.loop(0, per_worker // SC_CHUNK)
        def _(c):
            off = pl.multiple_of(base + c * SC_CHUNK, 8)
            pltpu.sync_copy(rows_hbm.at[pl.ds(off, SC_CHUNK)], rows_v)
            for k in range(n_slots):
                pltpu.sync_copy(idx_hbm.at[pl.ds(pl.multiple_of(k * T + off, 8), SC_CHUNK)], idx_v)
                pltpu.sync_copy(rows_v, out_hbm.at[idx_v])

    return scatter(rows, idx)


def _expert_kernel(ie_ref, ib_ref, ins_ref,
                   xs_hbm, wup_ref, bup_ref, wdn_ref, bdn_ref, perm_ref, ys_hbm,
                   xg_ref, acc_ref, yst_ref, wupb_ref, wdnb_ref, gsem, osem, *, n_ff_tiles):
    i = pl.program_id(0)
    j = pl.program_id(1)
    n_items = pl.num_programs(0)
    nsub = ins_ref[i]
    slot = i % 2

    def for_regions(n_blocks, fn):
        first = jnp.int32(0)
        for count in REGION_BLOCKS:
            present = (n_blocks & count) != 0
            pl.when(present)(functools.partial(fn, first, count))
            first = first + jnp.where(present, count, 0)

    def rows_of(first, count):
        return pl.ds(pl.multiple_of(first * ROW_BLOCK, ROW_BLOCK), count * ROW_BLOCK)

    def in_copy(item, dst_slot, first, count):
        src = pl.multiple_of((ib_ref[item] + first) * ROW_BLOCK, ROW_BLOCK)
        return pltpu.make_async_copy(xs_hbm.at[pl.ds(src, count * ROW_BLOCK)],
                                     xg_ref.at[dst_slot, rows_of(first, count)], gsem.at[dst_slot])

    def fetch_item(item, n_blocks, dst_slot):
        for_regions(n_blocks, lambda first, count: in_copy(item, dst_slot, first, count).start())

    @pl.when(j == 0)
    def _():
        @pl.when(i == 0)
        def _():
            fetch_item(0, nsub, 0)
            acc_ref[...] = jnp.zeros(acc_ref.shape, F32)

        nxt = jnp.minimum(i + 1, n_items - 1)
        fetch_item(nxt, jnp.where(i + 1 < n_items, ins_ref[nxt], 0), 1 - slot)
        for_regions(nsub, lambda first, count: in_copy(i, slot, first, count).wait())

    @pl.when(nsub > 0)
    def _():
        bup = bup_ref[0]
        perm = perm_ref[...]
        half = perm.shape[0] // 2

        def run_blocks(first, count):
            if first == 0:
                wup = wup_ref[0].astype(BF16)
                wdn = wdn_ref[0].astype(BF16)
                wupb_ref[...] = wup
                wdnb_ref[...] = wdn
            else:
                wup = wupb_ref[...]
                wdn = wdnb_ref[...]
            rows = pl.ds(first * ROW_BLOCK, count * ROW_BLOCK)
            xb = _unpack_pairs(xg_ref[slot, rows, :]).astype(BF16)
            gu = (jnp.dot(xb, wup, preferred_element_type=F32) + bup).astype(BF16)
            glu_parts, lin_parts = [], []
            for p in range(gu.shape[1] // perm.shape[0]):
                gp = jnp.dot(gu[:, p * perm.shape[0]:(p + 1) * perm.shape[0]], perm,
                             preferred_element_type=F32)
                glu_parts.append(gp[:, :half])
                lin_parts.append(gp[:, half:])
            x_glu = jnp.minimum(jnp.concatenate(glu_parts, axis=1), SWIGLU_LIMIT)
            x_lin = jnp.clip(jnp.concatenate(lin_parts, axis=1), -SWIGLU_LIMIT, SWIGLU_LIMIT)
            act = x_glu * jax.nn.sigmoid(SWIGLU_ALPHA * x_glu) * (x_lin + 1.0)
            start = jnp.where(j == 0, bdn_ref[0], acc_ref[rows, :])
            acc_ref[rows, :] = start + jnp.dot(act.astype(BF16), wdn, preferred_element_type=F32)

        for n_blocks, plan in COMPUTE_PLANS.items():
            for first, count in plan:
                shared = [n for n, p in COMPUTE_PLANS.items() if (first, count) in p]
                if n_blocks == shared[0]:
                    cond = functools.reduce(jnp.logical_or, [nsub == n for n in shared])
                    pl.when(cond)(functools.partial(run_blocks, first, count))

    def out_copy(item, first, count):
        dst = pl.multiple_of((ib_ref[item] + first) * ROW_BLOCK, ROW_BLOCK)
        return pltpu.make_async_copy(yst_ref.at[rows_of(first, count)],
                                     ys_hbm.at[pl.ds(dst, count * ROW_BLOCK)], osem)

    @pl.when(j == n_ff_tiles - 1)
    def _():
        prev = jnp.maximum(i - 1, 0)
        for_regions(jnp.where(i > 0, ins_ref[prev], 0),
                    lambda first, count: out_copy(prev, first, count).wait())

        def leave(first, count):
            rows = rows_of(first, count)
            yst_ref[rows, :] = _pack_pairs(acc_ref[rows, :])
            out_copy(i, first, count).start()

        for_regions(nsub, leave)

        @pl.when(i == n_items - 1)
        def _():
            for_regions(nsub, lambda first, count: out_copy(i, first, count).wait())


def _experts(xs, item_e, item_b, item_n, n_active, w_up, b_up, w_down, b_down):
    n_rows, Dp = xs.shape
    D = 2 * Dp
    E, _, F2 = w_up.shape
    F = F2 // 2
    J = F // FF_TILE
    half = 128
    perm = np.zeros((2 * half, 2 * half), np.float32)
    perm[2 * np.arange(half), np.arange(half)] = 1.0
    perm[2 * np.arange(half) + 1, half + np.arange(half)] = 1.0

    def jj(i, j, ins):
        return jnp.where(ins[i] > 0, j, J - 1)

    grid_spec = pltpu.PrefetchScalarGridSpec(
        num_scalar_prefetch=3,
        grid=(n_active, J),
        in_specs=[pl.BlockSpec(memory_space=pl.ANY),
                  pl.BlockSpec((1, D, 2 * FF_TILE), lambda i, j, ie, ib, ins: (ie[i], 0, jj(i, j, ins))),
                  pl.BlockSpec((1, 1, 2 * FF_TILE), lambda i, j, ie, ib, ins: (ie[i], 0, jj(i, j, ins))),
                  pl.BlockSpec((1, FF_TILE, D), lambda i, j, ie, ib, ins: (ie[i], jj(i, j, ins), 0)),
                  pl.BlockSpec((1, 1, D), lambda i, j, ie, ib, ins: (ie[i], 0, 0)),
                  pl.BlockSpec((2 * half, 2 * half), lambda i, j, ie, ib, ins: (0, 0))],
        out_specs=pl.BlockSpec(memory_space=pl.ANY),
        scratch_shapes=[pltpu.VMEM((2, ITEM_ROWS, Dp), jnp.uint32),
                        pltpu.VMEM((ITEM_ROWS, D), F32),
                        pltpu.VMEM((ITEM_ROWS, Dp), jnp.uint32),
                        pltpu.VMEM((D, 2 * FF_TILE), BF16),
                        pltpu.VMEM((FF_TILE, D), BF16),
                        pltpu.SemaphoreType.DMA((2,)), pltpu.SemaphoreType.DMA(())],
    )
    return pl.pallas_call(
        functools.partial(_expert_kernel, n_ff_tiles=J),
        grid_spec=grid_spec,
        out_shape=jax.ShapeDtypeStruct((n_rows, Dp), jnp.uint32),
        compiler_params=_cparams(("arbitrary", "arbitrary")),
        name="expert_ffn",
    )(item_e, item_b, item_n,
      xs, w_up, b_up.reshape(E, 1, F2), w_down, b_down.reshape(E, 1, D), jnp.asarray(perm, BF16))


def _combine_kernel(*refs):
    yk_refs, (g4_ref, x1_ref, g_ref), o_ref = refs[:TOP_K], refs[TOP_K:TOP_K + 3], refs[-1]
    y = x1_ref[...]
    g4 = g4_ref[...]
    for k in range(TOP_K):
        y = y + g4[:, k:k + 1] * _unpack_pairs(yk_refs[k][...])
    ms = jnp.mean(y * y, axis=-1, keepdims=True)
    o_ref[...] = y * lax.rsqrt(ms + NORM_EPS) * g_ref[...]


def _combine(yk, g4, x1, g_final, part, n_parts, out_so_far, tc=256):
    T, D = x1.shape
    nt = T // n_parts // tc
    first = part * nt
    slot_specs = [pl.BlockSpec((tc, D // 2), functools.partial(lambda i, k: (k * nt + i, 0), k=k))
                  for k in range(TOP_K)]
    operands = [*([yk] * TOP_K), g4, x1, g_final.reshape(1, D)]
    in_specs = slot_specs + [pl.BlockSpec((tc, 128), lambda i: (first + i, 0)),
                             pl.BlockSpec((tc, D), lambda i: (first + i, 0)),
                             pl.BlockSpec((1, D), lambda i: (0, 0))]
    aliases = {}
    if out_so_far is not None:
        aliases = {len(operands): 0}
        operands.append(out_so_far)
        in_specs.append(pl.BlockSpec(memory_space=pl.ANY))
    return pl.pallas_call(
        _combine_kernel,
        grid=(nt,),
        in_specs=in_specs,
        out_specs=pl.BlockSpec((tc, D), lambda i: (first + i, 0)),
        out_shape=jax.ShapeDtypeStruct((T, D), F32),
        input_output_aliases=aliases,
        compiler_params=_cparams(("parallel",)),
        name="combine_norm",
    )(*operands)


def _work_items(nblk, start_blk, n_items):
    E = nblk.shape[0]
    per_e = (nblk + ITEM_BLOCKS - 1) // ITEM_BLOCKS
    ends = jnp.cumsum(per_e)
    total = ends[-1]
    idx = jnp.arange(n_items, dtype=I32)
    e = jnp.minimum(jnp.searchsorted(ends, idx, side="right"), E - 1).astype(I32)
    local = idx - (ends[e] - per_e[e])
    active = idx < total
    last_e = e[jnp.maximum(total - 1, 0)]
    item_e = jnp.where(active, e, last_e).astype(I32)
    item_b = jnp.where(active, start_blk[e] + local * ITEM_BLOCKS, 0).astype(I32)
    item_n = jnp.where(active, jnp.clip(nblk[e] - local * ITEM_BLOCKS, 0, ITEM_BLOCKS), 0).astype(I32)
    return item_e, item_b, item_n, jnp.maximum(total, 1).astype(I32)


def kernel(x, g_mix, w_in, b_forget, g_v_ln, b_v_ln, w_spatial, b_spatial, w_branch_attn, w_branch_gmlp, w_out, g_ffn, w_router, b_router, w_expert_up, b_expert_up, w_expert_down, b_expert_down, g_final):
    B, S, D = x.shape
    T = B * S
    n_heads = b_forget.shape[0]
    attn_w = n_heads * HEAD_DIM
    gmlp_w = g_v_ln.shape[0]
    E = w_router.shape[1]
    off_f = 3 * attn_w
    off_z = off_f + n_heads
    off_g = off_z + 2 * gmlp_w

    x2 = x.reshape(T, D)
    w_in_t = w_in.T
    wzg = _rows_bf16(w_in_t, off_z, w_in.shape[1] - off_z)
    wa = w_branch_attn.astype(BF16)
    wb = w_branch_gmlp.astype(BF16)
    wo = w_out.astype(BF16)

    qkv, h = _norm_project(x2, g_mix, w_in_t, 3 * attn_w, attn_w, LOG2E / math.sqrt(HEAD_DIM))
    c_row = _forget_cumsum(h, w_in_t, off_f, b_forget, B, S)
    attn = _attention(qkv, c_row, B, S, n_heads)
    sg = _gmlp(h, wzg, g_v_ln, b_v_ln, w_spatial, b_spatial)
    merged = _merge(attn, sg, h, wa, wb, wzg, off_g - off_z)
    x1, h2p, sel, gate = _out_router(merged, x2, wo, g_ffn, w_router, b_router)

    dest4, g4, nblk, start_blk = _routing(sel, gate)
    n_rows = T * TOP_K + E * ROW_BLOCK
    dest_slots = dest4[:, :TOP_K].T.reshape(TOP_K * T)
    max_blocks = n_rows // ROW_BLOCK
    n_items = E + -(-(max_blocks - E) // ITEM_BLOCKS) + 1
    item_e, item_b, item_n, n_active = _work_items(nblk[0], start_blk[0], n_items)
    xs = _sc_scatter_rows(h2p, dest_slots, n_rows, TOP_K)
    ys = _experts(xs, item_e, item_b, item_n, n_active,
                  w_expert_up, b_expert_up, w_expert_down, b_expert_down)
    out = None
    t_part = T // COMBINE_PARTS
    for p in range(COMBINE_PARTS):
        idx = dest4[p * t_part:(p + 1) * t_part, :TOP_K].T.reshape(TOP_K * t_part)
        yk = _sc_gather_rows(ys, idx)
        out = _combine(yk, g4, x1, g_final, p, COMBINE_PARTS, out)
    return out.reshape(B, S, D)
```

```python
import functools
import math

import jax
import jax.numpy as jnp
import numpy as np
from jax import lax
from jax.experimental import pallas as pl
from jax.experimental.pallas import tpu as pltpu
from jax.experimental.pallas import tpu_sc as plsc

F32 = jnp.float32
BF16 = jnp.bfloat16
I32 = jnp.int32

NORM_EPS = 1e-5
LANES = 128
SUBLANES = 8
NT_DIMS = (((1,), (1,)), ((), ()))
HEAD_DIM = 128
CHUNK = 128
GROUP_DIM = 128
TOP_K = 4
SWIGLU_ALPHA = 1.702
SWIGLU_LIMIT = 7.0
LOG2E = math.log2(math.e)

VMEM_LIMIT_BYTES = 56 * 1024 * 1024

ROW_BLOCK = 128
ITEM_BLOCKS = 10
ITEM_ROWS = ITEM_BLOCKS * ROW_BLOCK
REGION_BLOCKS = (8, 4, 2, 1)
assert sum(REGION_BLOCKS) >= ITEM_BLOCKS
COMPUTE_PLANS = {**{n: ((0, n),) for n in range(4, ITEM_BLOCKS + 1)},
                 3: ((0, 2), (2, 1)), 2: ((0, 2),), 1: ((0, 1),)}
assert set(COMPUTE_PLANS) == set(range(1, ITEM_BLOCKS + 1))
FF_TILE = 256
COMBINE_PARTS = 1


def _cparams(sem, **kw):
    return pltpu.CompilerParams(dimension_semantics=sem, vmem_limit_bytes=VMEM_LIMIT_BYTES, **kw)


def _pack_pairs(x):
    c = x.shape[1] // 2
    hi = lax.bitcast_convert_type(x[:, :c].astype(BF16).astype(F32), jnp.uint32)
    lo = lax.bitcast_convert_type(x[:, c:].astype(BF16).astype(F32), jnp.uint32)
    return hi | (lo >> 16)


def _unpack_pairs(w):
    hi = lax.bitcast_convert_type(w & jnp.uint32(0xFFFF0000), F32)
    lo = lax.bitcast_convert_type(w << 16, F32)
    return jnp.concatenate([hi, lo], axis=1)


def _norm_proj_kernel(x_ref, g_ref, w_ref, o_ref, h_ref, *, n_scaled, scale):
    j = pl.program_id(1)

    @pl.when(j == 0)
    def _():
        x = x_ref[...]
        ms = jnp.mean(x * x, axis=-1, keepdims=True)
        h_ref[...] = (x * lax.rsqrt(ms + NORM_EPS) * g_ref[...]).astype(h_ref.dtype)

    acc = lax.dot_general(h_ref[...], w_ref[...].astype(BF16), NT_DIMS, preferred_element_type=F32)
    o_ref[...] = (acc * jnp.where(j < n_scaled, scale, 1.0)).astype(o_ref.dtype)


def _norm_project(x, g, wt, n_cols, n_scaled_cols, scale, tm=1024, tn=512):
    T, D = x.shape
    return pl.pallas_call(
        functools.partial(_norm_proj_kernel, n_scaled=n_scaled_cols // tn, scale=scale),
        grid=(T // tm, n_cols // tn),
        in_specs=[pl.BlockSpec((tm, D), lambda i, j: (i, 0)),
                  pl.BlockSpec((1, D), lambda i, j: (0, 0)),
                  pl.BlockSpec((tn, D), lambda i, j: (j, 0))],
        out_specs=[pl.BlockSpec((tm, tn), lambda i, j: (i, j)),
                   pl.BlockSpec((tm, D), lambda i, j: (i, 0))],
        out_shape=[jax.ShapeDtypeStruct((T, n_cols), BF16), jax.ShapeDtypeStruct((T, D), BF16)],
        compiler_params=_cparams(("arbitrary", "arbitrary")),
        name="norm_qkv_proj",
    )(x, g.reshape(1, D), wt)


def _cast_kernel(w_ref, o_ref):
    o_ref[...] = w_ref[...].astype(o_ref.dtype)


def _rows_bf16(wt, start, n_rows, tr=512):
    D = wt.shape[1]
    assert start % SUBLANES == 0 and n_rows % tr == 0
    return pl.pallas_call(
        _cast_kernel,
        grid=(n_rows // tr,),
        in_specs=[pl.BlockSpec((pl.Element(tr), pl.Element(D)),
                               lambda i: (pl.multiple_of(start + i * tr, SUBLANES), 0))],
        out_specs=pl.BlockSpec((tr, D), lambda i: (i, 0)),
        out_shape=jax.ShapeDtypeStruct((n_rows, D), BF16),
        compiler_params=_cparams(("parallel",)),
        name="rows_bf16",
    )(wt)


def _forget_kernel(h_ref, wft_ref, bf_ref, c_ref):
    ft = lax.dot_general(wft_ref[...].astype(BF16), h_ref[...], NT_DIMS,
                         preferred_element_type=F32)
    c = jax.nn.log_sigmoid(ft + bf_ref[...])
    S = c.shape[1]
    lane = lax.broadcasted_iota(I32, c.shape, 1)
    shift = 1
    while shift < S:
        c = c + jnp.where(lane >= shift, pltpu.roll(c, shift, axis=1), 0.0)
        shift *= 2
    c_ref[0] = c * LOG2E


def _forget_cumsum(h, wt, f_off, b_forget, B, S):
    T, D = h.shape
    H = b_forget.shape[0]
    assert f_off % H == 0 and H % SUBLANES == 0
    return pl.pallas_call(
        _forget_kernel,
        grid=(B,),
        in_specs=[pl.BlockSpec((S, D), lambda b: (b, 0)),
                  pl.BlockSpec((H, D), lambda b: (f_off // H, 0)),
                  pl.BlockSpec((H, 1), lambda b: (0, 0))],
        out_specs=pl.BlockSpec((1, H, S), lambda b: (b, 0, 0)),
        out_shape=jax.ShapeDtypeStruct((B, H, S), F32),
        compiler_params=_cparams(("parallel",)),
        name="forget_cumsum",
    )(h, wt, b_forget.reshape(H, 1))


def _attn_kernel(q_ref, k_ref, v_ref, crow_ref, o_ref, vaug_ref, m_ref, acc_ref, *, n_heads, tq):
    i = pl.program_id(1)

    @pl.when(i == 0)
    def _():
        ones = jnp.ones((v_ref.shape[0], HEAD_DIM), BF16)
        for h in range(n_heads):
            vaug_ref[h, :, :HEAD_DIM] = v_ref[:, h * HEAD_DIM:(h + 1) * HEAD_DIM]
            vaug_ref[h, :, HEAD_DIM:] = ones

    m_ref[...] = jnp.full(m_ref.shape, -jnp.inf, F32)
    acc_ref[...] = jnp.zeros(acc_ref.shape, F32)
    row = lax.broadcasted_iota(I32, (tq, tq), 0)
    col = lax.broadcasted_iota(I32, (tq, tq), 1)
    causal = col <= row

    def step(j, masked):
        keys = pl.ds(pl.multiple_of(j * tq, tq), tq)
        for h in range(n_heads):
            hs = slice(h * HEAD_DIM, (h + 1) * HEAD_DIM)
            s = lax.dot_general(q_ref[:, hs], k_ref[keys, hs], (((1,), (1,)), ((), ())),
                                preferred_element_type=F32) - crow_ref[0, h, j]
            if masked:
                s = jnp.where(causal, s, -jnp.inf)
            m_old = m_ref[h]
            m_new = jnp.maximum(m_old, jnp.max(s, axis=-1, keepdims=True))
            alpha = jnp.exp2(m_old - m_new)
            p = jnp.exp2(s - jnp.concatenate([m_new] * (tq // HEAD_DIM), axis=1))
            m_ref[h] = m_new
            pv = jnp.dot(p.astype(BF16), vaug_ref[h, keys, :], preferred_element_type=F32)
            acc_ref[h] = jnp.concatenate([alpha, alpha], axis=1) * acc_ref[h] + pv

    def body(j, _):
        step(j, False)
        return 0

    lax.fori_loop(0, i, body, 0)
    step(i, True)
    for h in range(n_heads):
        acc = acc_ref[h]
        o_ref[:, h * HEAD_DIM:(h + 1) * HEAD_DIM] = (acc[:, :HEAD_DIM] / acc[:, HEAD_DIM:]).astype(o_ref.dtype)


def _attention(qkv, c_row, B, S, n_heads, tq=256):
    T = qkv.shape[0]
    W = n_heads * HEAD_DIM
    nq = S // tq
    c_row5 = c_row.reshape(B, n_heads, nq, 1, tq)
    return pl.pallas_call(
        functools.partial(_attn_kernel, n_heads=n_heads, tq=tq),
        grid=(B, nq),
        in_specs=[pl.BlockSpec((tq, W), lambda b, i: (b * nq + i, 0)),
                  pl.BlockSpec((S, W), lambda b, i: (b, 1)),
                  pl.BlockSpec((S, W), lambda b, i: (b, 2)),
                  pl.BlockSpec((1, n_heads, nq, 1, tq), lambda b, i: (b, 0, 0, 0, 0))],
        out_specs=pl.BlockSpec((tq, W), lambda b, i: (b * nq + i, 0)),
        out_shape=jax.ShapeDtypeStruct((T, W), BF16),
        scratch_shapes=[pltpu.VMEM((n_heads, S, 2 * HEAD_DIM), BF16),
                        pltpu.VMEM((n_heads, tq, HEAD_DIM), F32),
                        pltpu.VMEM((n_heads, tq, 2 * HEAD_DIM), F32)],
        compiler_params=_cparams(("arbitrary", "arbitrary")),
        name="fox_attention",
    )(qkv, qkv, qkv, c_row5)


def _gmlp_kernel(h_ref, wz_ref, g_ref, b_ref, ws_ref, bst_ref, o_ref, *, n_groups):
    z = lax.dot_general(h_ref[...], wz_ref[...], NT_DIMS, preferred_element_type=F32)
    z = 0.5 * z * (1.0 + lax.erf(z * (1.0 / math.sqrt(2.0))))
    W = z.shape[1] // 2
    u = z[:, :W]
    v = z[:, W:]
    mu = jnp.mean(v, axis=-1, keepdims=True)
    var = jnp.mean(jnp.square(v - mu), axis=-1, keepdims=True)
    vn = (v - mu) * lax.rsqrt(var + NORM_EPS) * g_ref[...] + b_ref[...]
    row = lax.broadcasted_iota(I32, (CHUNK, CHUNK), 0)
    col = lax.broadcasted_iota(I32, (CHUNK, CHUNK), 1)
    tril = col <= row
    tg = z.shape[0]
    for g in range(n_groups):
        gs = slice(g * GROUP_DIM, (g + 1) * GROUP_DIM)
        wg = jnp.where(tril, ws_ref[g], 0.0).astype(BF16)
        bias = bst_ref[:, g:g + 1]
        for c in range(tg // CHUNK):
            cs = slice(c * CHUNK, (c + 1) * CHUNK)
            mixed = jnp.dot(wg, vn[cs, gs].astype(BF16), preferred_element_type=F32) + bias
            o_ref[cs, gs] = (u[cs, gs] * mixed).astype(o_ref.dtype)


def _gmlp(h, wz, g_v_ln, b_v_ln, w_spatial, b_spatial, tg=512):
    T, D = h.shape
    W = g_v_ln.shape[0]
    W2 = 2 * W
    G = w_spatial.shape[0]
    return pl.pallas_call(
        functools.partial(_gmlp_kernel, n_groups=G),
        grid=(T // tg,),
        in_specs=[pl.BlockSpec((tg, D), lambda i: (i, 0)),
                  pl.BlockSpec((W2, D), lambda i: (0, 0)),
                  pl.BlockSpec((1, W), lambda i: (0, 0)),
                  pl.BlockSpec((1, W), lambda i: (0, 0)),
                  pl.BlockSpec((G, CHUNK, CHUNK), lambda i: (0, 0, 0)),
                  pl.BlockSpec((CHUNK, G), lambda i: (0, 0))],
        out_specs=pl.BlockSpec((tg, W), lambda i: (i, 0)),
        out_shape=jax.ShapeDtypeStruct((T, W), BF16),
        compiler_params=_cparams(("parallel",)),
        name="gmlp",
    )(h, wz, g_v_ln.reshape(1, W), b_v_ln.reshape(1, W), w_spatial, b_spatial.T)


def _merge_kernel(attn_ref, sg_ref, h_ref, wa_ref, wb_ref, wga_ref, wgb_ref, o_ref):
    h = h_ref[...]
    a = jnp.dot(attn_ref[...], wa_ref[...], preferred_element_type=F32)
    ga = lax.dot_general(h, wga_ref[...], NT_DIMS, preferred_element_type=F32)
    m = jax.nn.sigmoid(ga) * a
    b = jnp.dot(sg_ref[...], wb_ref[...], preferred_element_type=F32)
    gb = lax.dot_general(h, wgb_ref[...], NT_DIMS, preferred_element_type=F32)
    o_ref[...] = (m + jax.nn.sigmoid(gb) * b).astype(o_ref.dtype)


def _merge(attn, sg, h, wa, wb, wg, g_off, tm=512, tn=512):
    T, D = h.shape
    Wa = attn.shape[1]
    Wb = sg.shape[1]
    nt = D // tn
    g0 = g_off // tn
    return pl.pallas_call(
        _merge_kernel,
        grid=(nt, T // tm),
        in_specs=[pl.BlockSpec((tm, Wa), lambda j, i: (i, 0)),
                  pl.BlockSpec((tm, Wb), lambda j, i: (i, 0)),
                  pl.BlockSpec((tm, D), lambda j, i: (i, 0)),
                  pl.BlockSpec((Wa, tn), lambda j, i: (0, j)),
                  pl.BlockSpec((Wb, tn), lambda j, i: (0, j)),
                  pl.BlockSpec((tn, D), lambda j, i: (g0 + j, 0)),
                  pl.BlockSpec((tn, D), lambda j, i: (g0 + nt + j, 0))],
        out_specs=pl.BlockSpec((tm, tn), lambda j, i: (i, j)),
        out_shape=jax.ShapeDtypeStruct((T, D), BF16),
        compiler_params=_cparams(("arbitrary", "arbitrary")),
        name="gated_merge",
    )(attn, sg, h, wa, wb, wg, wg)


def _out_router_kernel(m_ref, x_ref, wo_ref, g_ref, wr_ref, br_ref,
                       x1_ref, h2_ref, sel_ref, gate_ref, *, n_parts):
    E = br_ref.shape[1]
    part = m_ref.shape[0] // n_parts
    for p in range(n_parts):
        rows = pl.ds(p * part, part)
        x1 = x_ref[rows, :] + jnp.dot(m_ref[rows, :], wo_ref[...], preferred_element_type=F32)
        x1_ref[rows, :] = x1
        ms = jnp.mean(x1 * x1, axis=-1, keepdims=True)
        h2 = x1 * lax.rsqrt(ms + NORM_EPS) * g_ref[...]
        h2_ref[rows, :] = _pack_pairs(h2)
        h2_hi = h2.astype(BF16)
        h2_lo = (h2 - h2_hi.astype(F32)).astype(BF16)
        pa = jnp.dot(h2_hi, wr_ref[...], preferred_element_type=F32)
        pb = jnp.dot(h2_lo, wr_ref[:, :E], preferred_element_type=F32)
        logits = pa[:, :E] + (pa[:, E:] + pb) + br_ref[...]
        lane = lax.broadcasted_iota(I32, logits.shape, 1)
        work = logits
        sel = jnp.zeros(logits.shape, F32)
        num = jnp.zeros(logits.shape, F32)
        denom = jnp.zeros((part, 1), F32)
        m0 = None
        for _ in range(TOP_K):
            m = jnp.max(work, axis=-1, keepdims=True)
            idx = jnp.min(jnp.where(work == m, lane, E), axis=-1, keepdims=True)
            onehot = lane == idx
            if m0 is None:
                m0 = m
            e = jnp.exp(m - m0)
            sel = jnp.where(onehot, 1.0, sel)
            num = jnp.where(onehot, e, num)
            denom = denom + e
            work = jnp.where(onehot, -jnp.inf, work)
        sel_ref[rows, :] = sel
        gate_ref[rows, :] = num / denom


def _out_router(merged, x, wo, g_ffn, w_router, b_router, to=512, n_parts=2):
    T, D = x.shape
    E = w_router.shape[1]
    w_hi = w_router.astype(BF16)
    w_lo = (w_router - w_hi.astype(F32)).astype(BF16)
    w_router = jnp.concatenate([w_hi, w_lo], axis=1)
    row = lambda i: (i, 0)
    fixed = lambda i: (0, 0)
    return pl.pallas_call(
        functools.partial(_out_router_kernel, n_parts=n_parts),
        grid=(T // to,),
        in_specs=[pl.BlockSpec((to, D), row), pl.BlockSpec((to, D), row),
                  pl.BlockSpec((D, D), fixed), pl.BlockSpec((1, D), fixed),
                  pl.BlockSpec((D, 2 * E), fixed), pl.BlockSpec((1, E), fixed)],
        out_specs=[pl.BlockSpec((to, D), row), pl.BlockSpec((to, D // 2), row),
                   pl.BlockSpec((to, E), row), pl.BlockSpec((to, E), row)],
        out_shape=[jax.ShapeDtypeStruct((T, D), F32), jax.ShapeDtypeStruct((T, D // 2), jnp.uint32),
                   jax.ShapeDtypeStruct((T, E), F32), jax.ShapeDtypeStruct((T, E), F32)],
        compiler_params=_cparams(("parallel",)),
        name="out_router",
    )(merged, x, wo, g_ffn.reshape(1, D), w_router, b_router.reshape(1, E))


def _routing_kernel(sel_ref, gate_ref, dest_ref, g4_ref, nblk_ref, start_ref, rank_ref, *, tile):
    T, E = sel_ref.shape
    nt = T // tile
    r = lax.broadcasted_iota(I32, (tile, tile), 0)
    c = lax.broadcasted_iota(I32, (tile, tile), 1)
    strict_lower = (c < r).astype(BF16)
    er = lax.broadcasted_iota(I32, (E, E), 0)
    ec = lax.broadcasted_iota(I32, (E, E), 1)
    strict_upper = (er < ec).astype(BF16)

    def pass1(t, carry):
        rows = pl.ds(pl.multiple_of(t * tile, tile), tile)
        a = sel_ref[rows, :]
        rank_ref[rows, :] = jnp.dot(strict_lower, a.astype(BF16), preferred_element_type=F32) + carry
        return carry + jnp.sum(a, axis=0, keepdims=True)

    counts = lax.fori_loop(0, nt, pass1, jnp.zeros((1, E), F32))
    nblk = jnp.floor((counts + (ROW_BLOCK - 1)) * (1.0 / ROW_BLOCK))
    start_blk = jnp.dot(nblk.astype(BF16), strict_upper, preferred_element_type=F32)
    nblk_ref[...] = nblk.astype(I32)
    start_ref[...] = start_blk.astype(I32)
    start_row = start_blk * float(ROW_BLOCK)
    lane = lax.broadcasted_iota(I32, (tile, 128), 1)

    def pass2(t, _):
        rows = pl.ds(pl.multiple_of(t * tile, tile), tile)
        a = sel_ref[rows, :]
        g = gate_ref[rows, :]
        dest_e = rank_ref[rows, :] + start_row
        slot = jnp.dot(a.astype(BF16), strict_upper, preferred_element_type=F32)
        d4 = jnp.zeros((tile, 128), F32)
        g4 = jnp.zeros((tile, 128), F32)
        for s in range(TOP_K):
            pick = (a > 0.5) & (slot == float(s))
            d4 = jnp.where(lane == s, jnp.sum(jnp.where(pick, dest_e, 0.0), axis=-1, keepdims=True), d4)
            g4 = jnp.where(lane == s, jnp.sum(jnp.where(pick, g, 0.0), axis=-1, keepdims=True), g4)
        dest_ref[rows, :] = d4.astype(I32)
        g4_ref[rows, :] = g4
        return 0

    lax.fori_loop(0, nt, pass2, 0)


def _routing(sel, gate, tile=256):
    T, E = sel.shape
    return pl.pallas_call(
        functools.partial(_routing_kernel, tile=tile),
        out_shape=[jax.ShapeDtypeStruct((T, 128), I32), jax.ShapeDtypeStruct((T, 128), F32),
                   jax.ShapeDtypeStruct((1, E), I32), jax.ShapeDtypeStruct((1, E), I32)],
        scratch_shapes=[pltpu.VMEM((T, E), F32)],
        compiler_params=pltpu.CompilerParams(vmem_limit_bytes=VMEM_LIMIT_BYTES),
        name="routing_ranks",
    )(sel, gate)


SC_CORES = 2
SC_SUBCORES = 16
SC_CHUNK = 64


def _sc_gather_rows(table, idx):
    n = idx.shape[0]
    W = table.shape[1]
    n_workers = SC_CORES * SC_SUBCORES
    per_worker = n // n_workers
    assert per_worker * n_workers == n and per_worker % SC_CHUNK == 0
    mesh = plsc.VectorSubcoreMesh(core_axis_name="c", subcore_axis_name="s",
                                  num_cores=SC_CORES, num_subcores=SC_SUBCORES)

    @functools.partial(
        pl.kernel, mesh=mesh,
        out_type=jax.ShapeDtypeStruct((n, W), table.dtype),
        scratch_types=[pltpu.VMEM((SC_CHUNK,), I32), pltpu.VMEM((SC_CHUNK, W), table.dtype),
                       pltpu.SemaphoreType.DMA],
        name="sc_gather_rows",
    )
    def gather(table_hbm, idx_hbm, out_hbm, idx_v, rows_v, sem):
        wid = lax.axis_index("s") * SC_CORES + lax.axis_index("c")
        base = wid * per_worker

        @pl.loop(0, per_worker // SC_CHUNK)
        def _(c):
            off = pl.multiple_of(base + c * SC_CHUNK, 8)
            pltpu.sync_copy(idx_hbm.at[pl.ds(off, SC_CHUNK)], idx_v)
            pltpu.async_copy(table_hbm.at[idx_v], rows_v, sem).wait()
            pltpu.sync_copy(rows_v, out_hbm.at[pl.ds(off, SC_CHUNK)])

    return gather(table, idx)


def _sc_scatter_rows(rows, idx, n_out, n_slots):
    T, W = rows.shape
    n_workers = SC_CORES * SC_SUBCORES
    per_worker = T // n_workers
    assert per_worker * n_workers == T and per_worker % SC_CHUNK == 0
    mesh = plsc.VectorSubcoreMesh(core_axis_name="c", subcore_axis_name="s",
                                  num_cores=SC_CORES, num_subcores=SC_SUBCORES)

    @functools.partial(
        pl.kernel, mesh=mesh,
        out_type=jax.ShapeDtypeStruct((n_out, W), rows.dtype),
        scratch_types=[pltpu.VMEM((SC_CHUNK,), I32), pltpu.VMEM((SC_CHUNK, W), rows.dtype)],
        name="sc_scatter_rows",
    )
    def scatter(rows_hbm, idx_hbm, out_hbm, idx_v, rows_v):
        wid = lax.axis_index("s") * SC_CORES + lax.axis_index("c")
        base = wid * per_worker

        @pl.loop(0, per_worker // SC_CHUNK)
        def _(c):
            off = pl.multiple_of(base + c * SC_CHUNK, 8)
            pltpu.sync_copy(rows_hbm.at[pl.ds(off, SC_CHUNK)], rows_v)
            for k in range(n_slots):
                pltpu.sync_copy(idx_hbm.at[pl.ds(pl.multiple_of(k * T + off, 8), SC_CHUNK)], idx_v)
                pltpu.sync_copy(rows_v, out_hbm.at[idx_v])

    return scatter(rows, idx)


def _expert_kernel(ie_ref, ib_ref, ins_ref,
                   xs_hbm, wup_ref, bup_ref, wdn_ref, bdn_ref, perm_ref, ys_hbm,
                   xg_ref, acc_ref, yst_ref, wupb_ref, wdnb_ref, gsem, osem, *, n_ff_tiles):
    i = pl.program_id(0)
    j = pl.program_id(1)
    n_items = pl.num_programs(0)
    nsub = ins_ref[i]
    slot = i % 2

    def for_regions(n_blocks, fn):
        first = jnp.int32(0)
        for count in REGION_BLOCKS:
            present = (n_blocks & count) != 0
            pl.when(present)(functools.partial(fn, first, count))
            first = first + jnp.where(present, count, 0)

    def rows_of(first, count):
        return pl.ds(pl.multiple_of(first * ROW_BLOCK, ROW_BLOCK), count * ROW_BLOCK)

    def in_copy(item, dst_slot, first, count):
        src = pl.multiple_of((ib_ref[item] + first) * ROW_BLOCK, ROW_BLOCK)
        return pltpu.make_async_copy(xs_hbm.at[pl.ds(src, count * ROW_BLOCK)],
                                     xg_ref.at[dst_slot, rows_of(first, count)], gsem.at[dst_slot])

    def fetch_item(item, n_blocks, dst_slot):
        for_regions(n_blocks, lambda first, count: in_copy(item, dst_slot, first, count).start())

    @pl.when(j == 0)
    def _():
        @pl.when(i == 0)
        def _():
            fetch_item(0, nsub, 0)
            acc_ref[...] = jnp.zeros(acc_ref.shape, F32)

        nxt = jnp.minimum(i + 1, n_items - 1)
        fetch_item(nxt, jnp.where(i + 1 < n_items, ins_ref[nxt], 0), 1 - slot)
        for_regions(nsub, lambda first, count: in_copy(i, slot, first, count).wait())

    @pl.when(nsub > 0)
    def _():
        bup = bup_ref[0]
        perm = perm_ref[...]
        half = perm.shape[0] // 2

        def run_blocks(first, count):
            if first == 0:
                wup = wup_ref[0].astype(BF16)
                wdn = wdn_ref[0].astype(BF16)
                wupb_ref[...] = wup
                wdnb_ref[...] = wdn
            else:
                wup = wupb_ref[...]
                wdn = wdnb_ref[...]
            rows = pl.ds(first * ROW_BLOCK, count * ROW_BLOCK)
            xb = _unpack_pairs(xg_ref[slot, rows, :]).astype(BF16)
            gu = (jnp.dot(xb, wup, preferred_element_type=F32) + bup).astype(BF16)
            glu_parts, lin_parts = [], []
            for p in range(gu.shape[1] // perm.shape[0]):
                gp = jnp.dot(gu[:, p * perm.shape[0]:(p + 1) * perm.shape[0]], perm,
                             preferred_element_type=F32)
                glu_parts.append(gp[:, :half])
                lin_parts.append(gp[:, half:])
            x_glu = jnp.minimum(jnp.concatenate(glu_parts, axis=1), SWIGLU_LIMIT)
            x_lin = jnp.clip(jnp.concatenate(lin_parts, axis=1), -SWIGLU_LIMIT, SWIGLU_LIMIT)
            act = x_glu * jax.nn.sigmoid(SWIGLU_ALPHA * x_glu) * (x_lin + 1.0)
            start = jnp.where(j == 0, bdn_ref[0], acc_ref[rows, :])
            acc_ref[rows, :] = start + jnp.dot(act.astype(BF16), wdn, preferred_element_type=F32)

        for n_blocks, plan in COMPUTE_PLANS.items():
            for first, count in plan:
                shared = [n for n, p in COMPUTE_PLANS.items() if (first, count) in p]
                if n_blocks == shared[0]:
                    cond = functools.reduce(jnp.logical_or, [nsub == n for n in shared])
                    pl.when(cond)(functools.partial(run_blocks, first, count))

    def out_copy(item, first, count):
        dst = pl.multiple_of((ib_ref[item] + first) * ROW_BLOCK, ROW_BLOCK)
        return pltpu.make_async_copy(yst_ref.at[rows_of(first, count)],
                                     ys_hbm.at[pl.ds(dst, count * ROW_BLOCK)], osem)

    @pl.when(j == n_ff_tiles - 1)
    def _():
        prev = jnp.maximum(i - 1, 0)
        for_regions(jnp.where(i > 0, ins_ref[prev], 0),
                    lambda first, count: out_copy(prev, first, count).wait())

        def leave(first, count):
            rows = rows_of(first, count)
            yst_ref[rows, :] = _pack_pairs(acc_ref[rows, :])
            out_copy(i, first, count).start()

        for_regions(nsub, leave)

        @pl.when(i == n_items - 1)
        def _():
            for_regions(nsub, lambda first, count: out_copy(i, first, count).wait())


def _experts(xs, item_e, item_b, item_n, n_active, w_up, b_up, w_down, b_down):
    n_rows, Dp = xs.shape
    D = 2 * Dp
    E, _, F2 = w_up.shape
    F = F2 // 2
    J = F // FF_TILE
    half = 128
    perm = np.zeros((2 * half, 2 * half), np.float32)
    perm[2 * np.arange(half), np.arange(half)] = 1.0
    perm[2 * np.arange(half) + 1, half + np.arange(half)] = 1.0

    def jj(i, j, ins):
        return jnp.where(ins[i] > 0, j, J - 1)

    grid_spec = pltpu.PrefetchScalarGridSpec(
        num_scalar_prefetch=3,
        grid=(n_active, J),
        in_specs=[pl.BlockSpec(memory_space=pl.ANY),
                  pl.BlockSpec((1, D, 2 * FF_TILE), lambda i, j, ie, ib, ins: (ie[i], 0, jj(i, j, ins))),
                  pl.BlockSpec((1, 1, 2 * FF_TILE), lambda i, j, ie, ib, ins: (ie[i], 0, jj(i, j, ins))),
                  pl.BlockSpec((1, FF_TILE, D), lambda i, j, ie, ib, ins: (ie[i], jj(i, j, ins), 0)),
                  pl.BlockSpec((1, 1, D), lambda i, j, ie, ib, ins: (ie[i], 0, 0)),
                  pl.BlockSpec((2 * half, 2 * half), lambda i, j, ie, ib, ins: (0, 0))],
        out_specs=pl.BlockSpec(memory_space=pl.ANY),
        scratch_shapes=[pltpu.VMEM((2, ITEM_ROWS, Dp), jnp.uint32),
                        pltpu.VMEM((ITEM_ROWS, D), F32),
                        pltpu.VMEM((ITEM_ROWS, Dp), jnp.uint32),
                        pltpu.VMEM((D, 2 * FF_TILE), BF16),
                        pltpu.VMEM((FF_TILE, D), BF16),
                        pltpu.SemaphoreType.DMA((2,)), pltpu.SemaphoreType.DMA(())],
    )
    return pl.pallas_call(
        functools.partial(_expert_kernel, n_ff_tiles=J),
        grid_spec=grid_spec,
        out_shape=jax.ShapeDtypeStruct((n_rows, Dp), jnp.uint32),
        compiler_params=_cparams(("arbitrary", "arbitrary")),
        name="expert_ffn",
    )(item_e, item_b, item_n,
      xs, w_up, b_up.reshape(E, 1, F2), w_down, b_down.reshape(E, 1, D), jnp.asarray(perm, BF16))


def _combine_kernel(*refs):
    yk_refs, (g4_ref, x1_ref, g_ref), o_ref = refs[:TOP_K], refs[TOP_K:TOP_K + 3], refs[-1]
    y = x1_ref[...]
    g4 = g4_ref[...]
    for k in range(TOP_K):
        y = y + g4[:, k:k + 1] * _unpack_pairs(yk_refs[k][...])
    ms = jnp.mean(y * y, axis=-1, keepdims=True)
    o_ref[...] = y * lax.rsqrt(ms + NORM_EPS) * g_ref[...]


def _combine(yk, g4, x1, g_final, part, n_parts, out_so_far, tc=512):
    T, D = x1.shape
    nt = T // n_parts // tc
    first = part * nt
    slot_specs = [pl.BlockSpec((tc, D // 2), functools.partial(lambda i, k: (k * nt + i, 0), k=k))
                  for k in range(TOP_K)]
    operands = [*([yk] * TOP_K), g4, x1, g_final.reshape(1, D)]
    in_specs = slot_specs + [pl.BlockSpec((tc, 128), lambda i: (first + i, 0)),
                             pl.BlockSpec((tc, D), lambda i: (first + i, 0)),
                             pl.BlockSpec((1, D), lambda i: (0, 0))]
    aliases = {}
    if out_so_far is not None:
        aliases = {len(operands): 0}
        operands.append(out_so_far)
        in_specs.append(pl.BlockSpec(memory_space=pl.ANY))
    return pl.pallas_call(
        _combine_kernel,
        grid=(nt,),
        in_specs=in_specs,
        out_specs=pl.BlockSpec((tc, D), lambda i: (first + i, 0)),
        out_shape=jax.ShapeDtypeStruct((T, D), F32),
        input_output_aliases=aliases,
        compiler_params=_cparams(("parallel",)),
        name="combine_norm",
    )(*operands)


def _work_items(nblk, start_blk, n_items):
    E = nblk.shape[0]
    per_e = (nblk + ITEM_BLOCKS - 1) // ITEM_BLOCKS
    ends = jnp.cumsum(per_e)
    total = ends[-1]
    idx = jnp.arange(n_items, dtype=I32)
    e = jnp.minimum(jnp.searchsorted(ends, idx, side="right"), E - 1).astype(I32)
    local = idx - (ends[e] - per_e[e])
    active = idx < total
    last_e = e[jnp.maximum(total - 1, 0)]
    item_e = jnp.where(active, e, last_e).astype(I32)
    item_b = jnp.where(active, start_blk[e] + local * ITEM_BLOCKS, 0).astype(I32)
    item_n = jnp.where(active, jnp.clip(nblk[e] - local * ITEM_BLOCKS, 0, ITEM_BLOCKS), 0).astype(I32)
    return item_e, item_b, item_n, jnp.maximum(total, 1).astype(I32)


def kernel(x, g_mix, w_in, b_forget, g_v_ln, b_v_ln, w_spatial, b_spatial, w_branch_attn, w_branch_gmlp, w_out, g_ffn, w_router, b_router, w_expert_up, b_expert_up, w_expert_down, b_expert_down, g_final):
    B, S, D = x.shape
    T = B * S
    n_heads = b_forget.shape[0]
    attn_w = n_heads * HEAD_DIM
    gmlp_w = g_v_ln.shape[0]
    E = w_router.shape[1]
    off_f = 3 * attn_w
    off_z = off_f + n_heads
    off_g = off_z + 2 * gmlp_w

    x2 = x.reshape(T, D)
    w_in_t = w_in.T
    wzg = _rows_bf16(w_in_t, off_z, w_in.shape[1] - off_z)
    wa = w_branch_attn.astype(BF16)
    wb = w_branch_gmlp.astype(BF16)
    wo = w_out.astype(BF16)

    qkv, h = _norm_project(x2, g_mix, w_in_t, 3 * attn_w, attn_w, LOG2E / math.sqrt(HEAD_DIM))
    c_row = _forget_cumsum(h, w_in_t, off_f, b_forget, B, S)
    attn = _attention(qkv, c_row, B, S, n_heads)
    sg = _gmlp(h, wzg, g_v_ln, b_v_ln, w_spatial, b_spatial)
    merged = _merge(attn, sg, h, wa, wb, wzg, off_g - off_z)
    x1, h2p, sel, gate = _out_router(merged, x2, wo, g_ffn, w_router, b_router)

    dest4, g4, nblk, start_blk = _routing(sel, gate)
    n_rows = T * TOP_K + E * ROW_BLOCK
    dest_slots = dest4[:, :TOP_K].T.reshape(TOP_K * T)
    max_blocks = n_rows // ROW_BLOCK
    n_items = E + -(-(max_blocks - E) // ITEM_BLOCKS) + 1
    item_e, item_b, item_n, n_active = _work_items(nblk[0], start_blk[0], n_items)
    xs = _sc_scatter_rows(h2p, dest_slots, n_rows, TOP_K)
    ys = _experts(xs, item_e, item_b, item_n, n_active,
                  w_expert_up, b_expert_up, w_expert_down, b_expert_down)
    out = None
    t_part = T // COMBINE_PARTS
    for p in range(COMBINE_PARTS):
        idx = dest4[p * t_part:(p + 1) * t_part, :TOP_K].T.reshape(TOP_K * t_part)
        yk = _sc_gather_rows(ys, idx)
        out = _combine(yk, g4, x1, g_final, p, COMBINE_PARTS, out)
    return out.reshape(B, S, D)
```

```python
import functools
import math

import jax
import jax.numpy as jnp
import numpy as np
from jax import lax
from jax.experimental import pallas as pl
from jax.experimental.pallas import tpu as pltpu
from jax.experimental.pallas import tpu_sc as plsc

F32 = jnp.float32
BF16 = jnp.bfloat16
I32 = jnp.int32

NORM_EPS = 1e-5
LANES = 128
SUBLANES = 8
NT_DIMS = (((1,), (1,)), ((), ()))
HEAD_DIM = 128
CHUNK = 128
GROUP_DIM = 128
TOP_K = 4
SWIGLU_ALPHA = 1.702
SWIGLU_LIMIT = 7.0
LOG2E = math.log2(math.e)

VMEM_LIMIT_BYTES = 56 * 1024 * 1024

ROW_BLOCK = 128
ITEM_BLOCKS = 10
ITEM_ROWS = ITEM_BLOCKS * ROW_BLOCK
REGION_BLOCKS = (8, 4, 2, 1)
assert sum(REGION_BLOCKS) >= ITEM_BLOCKS
COMPUTE_PLANS = {**{n: ((0, n),) for n in range(4, ITEM_BLOCKS + 1)},
                 3: ((0, 2), (2, 1)), 2: ((0, 2),), 1: ((0, 1),)}
assert set(COMPUTE_PLANS) == set(range(1, ITEM_BLOCKS + 1))
FF_TILE = 256
COMBINE_PARTS = 1


def _cparams(sem, **kw):
    return pltpu.CompilerParams(dimension_semantics=sem, vmem_limit_bytes=VMEM_LIMIT_BYTES, **kw)


def _pack_pairs(x):
    c = x.shape[1] // 2
    hi = lax.bitcast_convert_type(x[:, :c].astype(BF16).astype(F32), jnp.uint32)
    lo = lax.bitcast_convert_type(x[:, c:].astype(BF16).astype(F32), jnp.uint32)
    return hi | (lo >> 16)


def _unpack_pairs(w):
    hi = lax.bitcast_convert_type(w & jnp.uint32(0xFFFF0000), F32)
    lo = lax.bitcast_convert_type(w << 16, F32)
    return jnp.concatenate([hi, lo], axis=1)


def _norm_proj_kernel(x_ref, g_ref, w_ref, o_ref, h_ref, *, n_scaled, scale):
    j = pl.program_id(1)

    @pl.when(j == 0)
    def _():
        x = x_ref[...]
        ms = jnp.mean(x * x, axis=-1, keepdims=True)
        h_ref[...] = (x * lax.rsqrt(ms + NORM_EPS) * g_ref[...]).astype(h_ref.dtype)

    acc = lax.dot_general(h_ref[...], w_ref[...].astype(BF16), NT_DIMS, preferred_element_type=F32)
    o_ref[...] = (acc * jnp.where(j < n_scaled, scale, 1.0)).astype(o_ref.dtype)


def _norm_project(x, g, wt, n_cols, n_scaled_cols, scale, tm=1024, tn=1024):
    T, D = x.shape
    return pl.pallas_call(
        functools.partial(_norm_proj_kernel, n_scaled=n_scaled_cols // tn, scale=scale),
        grid=(T // tm, n_cols // tn),
        in_specs=[pl.BlockSpec((tm, D), lambda i, j: (i, 0)),
                  pl.BlockSpec((1, D), lambda i, j: (0, 0)),
                  pl.BlockSpec((tn, D), lambda i, j: (j, 0))],
        out_specs=[pl.BlockSpec((tm, tn), lambda i, j: (i, j)),
                   pl.BlockSpec((tm, D), lambda i, j: (i, 0))],
        out_shape=[jax.ShapeDtypeStruct((T, n_cols), BF16), jax.ShapeDtypeStruct((T, D), BF16)],
        compiler_params=_cparams(("arbitrary", "arbitrary")),
        name="norm_qkv_proj",
    )(x, g.reshape(1, D), wt)


def _cast_kernel(w_ref, o_ref):
    o_ref[...] = w_ref[...].astype(o_ref.dtype)


def _rows_bf16(wt, start, n_rows, tr=1024):
    D = wt.shape[1]
    assert start % SUBLANES == 0 and n_rows % tr == 0
    return pl.pallas_call(
        _cast_kernel,
        grid=(n_rows // tr,),
        in_specs=[pl.BlockSpec((pl.Element(tr), pl.Element(D)),
                               lambda i: (pl.multiple_of(start + i * tr, SUBLANES), 0))],
        out_specs=pl.BlockSpec((tr, D), lambda i: (i, 0)),
        out_shape=jax.ShapeDtypeStruct((n_rows, D), BF16),
        compiler_params=_cparams(("parallel",)),
        name="rows_bf16",
    )(wt)


def _forget_kernel(h_ref, wft_ref, bf_ref, c_ref):
    ft = lax.dot_general(wft_ref[...].astype(BF16), h_ref[...], NT_DIMS,
                         preferred_element_type=F32)
    c = jax.nn.log_sigmoid(ft + bf_ref[...])
    S = c.shape[1]
    lane = lax.broadcasted_iota(I32, c.shape, 1)
    shift = 1
    while shift < S:
        c = c + jnp.where(lane >= shift, pltpu.roll(c, shift, axis=1), 0.0)
        shift *= 2
    c_ref[0] = c * LOG2E


def _forget_cumsum(h, wt, f_off, b_forget, B, S):
    T, D = h.shape
    H = b_forget.shape[0]
    assert f_off % H == 0 and H % SUBLANES == 0
    return pl.pallas_call(
        _forget_kernel,
        grid=(B,),
        in_specs=[pl.BlockSpec((S, D), lambda b: (b, 0)),
                  pl.BlockSpec((H, D), lambda b: (f_off // H, 0)),
                  pl.BlockSpec((H, 1), lambda b: (0, 0))],
        out_specs=pl.BlockSpec((1, H, S), lambda b: (b, 0, 0)),
        out_shape=jax.ShapeDtypeStruct((B, H, S), F32),
        compiler_params=_cparams(("parallel",)),
        name="forget_cumsum",
    )(h, wt, b_forget.reshape(H, 1))


def _attn_kernel(q_ref, k_ref, v_ref, crow_ref, o_ref, vaug_ref, m_ref, acc_ref, *, n_heads, tq):
    i = pl.program_id(1)

    @pl.when(i == 0)
    def _():
        ones = jnp.ones((v_ref.shape[0], HEAD_DIM), BF16)
        for h in range(n_heads):
            vaug_ref[h, :, :HEAD_DIM] = v_ref[:, h * HEAD_DIM:(h + 1) * HEAD_DIM]
            vaug_ref[h, :, HEAD_DIM:] = ones

    m_ref[...] = jnp.full(m_ref.shape, -jnp.inf, F32)
    acc_ref[...] = jnp.zeros(acc_ref.shape, F32)
    row = lax.broadcasted_iota(I32, (tq, tq), 0)
    col = lax.broadcasted_iota(I32, (tq, tq), 1)
    causal = col <= row

    def step(j, masked):
        keys = pl.ds(pl.multiple_of(j * tq, tq), tq)
        for h in range(n_heads):
            hs = slice(h * HEAD_DIM, (h + 1) * HEAD_DIM)
            s = lax.dot_general(q_ref[:, hs], k_ref[keys, hs], (((1,), (1,)), ((), ())),
                                preferred_element_type=F32) - crow_ref[0, h, j]
            if masked:
                s = jnp.where(causal, s, -jnp.inf)
            m_old = m_ref[h]
            m_new = jnp.maximum(m_old, jnp.max(s, axis=-1, keepdims=True))
            alpha = jnp.exp2(m_old - m_new)
            p = jnp.exp2(s - jnp.concatenate([m_new] * (tq // HEAD_DIM), axis=1))
            m_ref[h] = m_new
            pv = jnp.dot(p.astype(BF16), vaug_ref[h, keys, :], preferred_element_type=F32)
            acc_ref[h] = jnp.concatenate([alpha, alpha], axis=1) * acc_ref[h] + pv

    def body(j, _):
        step(j, False)
        return 0

    lax.fori_loop(0, i, body, 0)
    step(i, True)
    for h in range(n_heads):
        acc = acc_ref[h]
        o_ref[:, h * HEAD_DIM:(h + 1) * HEAD_DIM] = (acc[:, :HEAD_DIM] / acc[:, HEAD_DIM:]).astype(o_ref.dtype)


def _attention(qkv, c_row, B, S, n_heads, tq=512):
    T = qkv.shape[0]
    W = n_heads * HEAD_DIM
    nq = S // tq
    c_row5 = c_row.reshape(B, n_heads, nq, 1, tq)
    return pl.pallas_call(
        functools.partial(_attn_kernel, n_heads=n_heads, tq=tq),
        grid=(B, nq),
        in_specs=[pl.BlockSpec((tq, W), lambda b, i: (b * nq + i, 0)),
                  pl.BlockSpec((S, W), lambda b, i: (b, 1)),
                  pl.BlockSpec((S, W), lambda b, i: (b, 2)),
                  pl.BlockSpec((1, n_heads, nq, 1, tq), lambda b, i: (b, 0, 0, 0, 0))],
        out_specs=pl.BlockSpec((tq, W), lambda b, i: (b * nq + i, 0)),
        out_shape=jax.ShapeDtypeStruct((T, W), BF16),
        scratch_shapes=[pltpu.VMEM((n_heads, S, 2 * HEAD_DIM), BF16),
                        pltpu.VMEM((n_heads, tq, HEAD_DIM), F32),
                        pltpu.VMEM((n_heads, tq, 2 * HEAD_DIM), F32)],
        compiler_params=_cparams(("arbitrary", "arbitrary")),
        name="fox_attention",
    )(qkv, qkv, qkv, c_row5)


def _gmlp_kernel(h_ref, wz_ref, g_ref, b_ref, ws_ref, bst_ref, o_ref, *, n_groups):
    z = lax.dot_general(h_ref[...], wz_ref[...], NT_DIMS, preferred_element_type=F32)
    z = 0.5 * z * (1.0 + lax.erf(z * (1.0 / math.sqrt(2.0))))
    W = z.shape[1] // 2
    u = z[:, :W]
    v = z[:, W:]
    mu = jnp.mean(v, axis=-1, keepdims=True)
    var = jnp.mean(jnp.square(v - mu), axis=-1, keepdims=True)
    vn = (v - mu) * lax.rsqrt(var + NORM_EPS) * g_ref[...] + b_ref[...]
    row = lax.broadcasted_iota(I32, (CHUNK, CHUNK), 0)
    col = lax.broadcasted_iota(I32, (CHUNK, CHUNK), 1)
    tril = col <= row
    tg = z.shape[0]
    for g in range(n_groups):
        gs = slice(g * GROUP_DIM, (g + 1) * GROUP_DIM)
        wg = jnp.where(tril, ws_ref[g], 0.0).astype(BF16)
        bias = bst_ref[:, g:g + 1]
        for c in range(tg // CHUNK):
            cs = slice(c * CHUNK, (c + 1) * CHUNK)
            mixed = jnp.dot(wg, vn[cs, gs].astype(BF16), preferred_element_type=F32) + bias
            o_ref[cs, gs] = (u[cs, gs] * mixed).astype(o_ref.dtype)


def _gmlp(h, wz, g_v_ln, b_v_ln, w_spatial, b_spatial, tg=512):
    T, D = h.shape
    W = g_v_ln.shape[0]
    W2 = 2 * W
    G = w_spatial.shape[0]
    return pl.pallas_call(
        functools.partial(_gmlp_kernel, n_groups=G),
        grid=(T // tg,),
        in_specs=[pl.BlockSpec((tg, D), lambda i: (i, 0)),
                  pl.BlockSpec((W2, D), lambda i: (0, 0)),
                  pl.BlockSpec((1, W), lambda i: (0, 0)),
                  pl.BlockSpec((1, W), lambda i: (0, 0)),
                  pl.BlockSpec((G, CHUNK, CHUNK), lambda i: (0, 0, 0)),
                  pl.BlockSpec((CHUNK, G), lambda i: (0, 0))],
        out_specs=pl.BlockSpec((tg, W), lambda i: (i, 0)),
        out_shape=jax.ShapeDtypeStruct((T, W), BF16),
        compiler_params=_cparams(("parallel",)),
        name="gmlp",
    )(h, wz, g_v_ln.reshape(1, W), b_v_ln.reshape(1, W), w_spatial, b_spatial.T)


def _merge_kernel(attn_ref, sg_ref, h_ref, wa_ref, wb_ref, wga_ref, wgb_ref, o_ref):
    h = h_ref[...]
    a = jnp.dot(attn_ref[...], wa_ref[...], preferred_element_type=F32)
    ga = lax.dot_general(h, wga_ref[...], NT_DIMS, preferred_element_type=F32)
    m = jax.nn.sigmoid(ga) * a
    b = jnp.dot(sg_ref[...], wb_ref[...], preferred_element_type=F32)
    gb = lax.dot_general(h, wgb_ref[...], NT_DIMS, preferred_element_type=F32)
    o_ref[...] = (m + jax.nn.sigmoid(gb) * b).astype(o_ref.dtype)


def _merge(attn, sg, h, wa, wb, wg, g_off, tm=512, tn=512):
    T, D = h.shape
    Wa = attn.shape[1]
    Wb = sg.shape[1]
    nt = D // tn
    g0 = g_off // tn
    return pl.pallas_call(
        _merge_kernel,
        grid=(nt, T // tm),
        in_specs=[pl.BlockSpec((tm, Wa), lambda j, i: (i, 0)),
                  pl.BlockSpec((tm, Wb), lambda j, i: (i, 0)),
                  pl.BlockSpec((tm, D), lambda j, i: (i, 0)),
                  pl.BlockSpec((Wa, tn), lambda j, i: (0, j)),
                  pl.BlockSpec((Wb, tn), lambda j, i: (0, j)),
                  pl.BlockSpec((tn, D), lambda j, i: (g0 + j, 0)),
                  pl.BlockSpec((tn, D), lambda j, i: (g0 + nt + j, 0))],
        out_specs=pl.BlockSpec((tm, tn), lambda j, i: (i, j)),
        out_shape=jax.ShapeDtypeStruct((T, D), BF16),
        compiler_params=_cparams(("arbitrary", "arbitrary")),
        name="gated_merge",
    )(attn, sg, h, wa, wb, wg, wg)


def _out_router_kernel(m_ref, x_ref, wo_ref, g_ref, wr_ref, br_ref,
                       x1_ref, h2_ref, sel_ref, gate_ref, *, n_parts):
    E = br_ref.shape[1]
    part = m_ref.shape[0] // n_parts
    for p in range(n_parts):
        rows = pl.ds(p * part, part)
        x1 = x_ref[rows, :] + jnp.dot(m_ref[rows, :], wo_ref[...], preferred_element_type=F32)
        x1_ref[rows, :] = x1
        ms = jnp.mean(x1 * x1, axis=-1, keepdims=True)
        h2 = x1 * lax.rsqrt(ms + NORM_EPS) * g_ref[...]
        h2_ref[rows, :] = _pack_pairs(h2)
        h2_hi = h2.astype(BF16)
        h2_lo = (h2 - h2_hi.astype(F32)).astype(BF16)
        pa = jnp.dot(h2_hi, wr_ref[...], preferred_element_type=F32)
        pb = jnp.dot(h2_lo, wr_ref[:, :E], preferred_element_type=F32)
        logits = pa[:, :E] + (pa[:, E:] + pb) + br_ref[...]
        lane = lax.broadcasted_iota(I32, logits.shape, 1)
        work = logits
        sel = jnp.zeros(logits.shape, F32)
        num = jnp.zeros(logits.shape, F32)
        denom = jnp.zeros((part, 1), F32)
        m0 = None
        for _ in range(TOP_K):
            m = jnp.max(work, axis=-1, keepdims=True)
            idx = jnp.min(jnp.where(work == m, lane, E), axis=-1, keepdims=True)
            onehot = lane == idx
            if m0 is None:
                m0 = m
            e = jnp.exp(m - m0)
            sel = jnp.where(onehot, 1.0, sel)
            num = jnp.where(onehot, e, num)
            denom = denom + e
            work = jnp.where(onehot, -jnp.inf, work)
        sel_ref[rows, :] = sel
        gate_ref[rows, :] = num / denom


def _out_router(merged, x, wo, g_ffn, w_router, b_router, to=512, n_parts=2):
    T, D = x.shape
    E = w_router.shape[1]
    w_hi = w_router.astype(BF16)
    w_lo = (w_router - w_hi.astype(F32)).astype(BF16)
    w_router = jnp.concatenate([w_hi, w_lo], axis=1)
    row = lambda i: (i, 0)
    fixed = lambda i: (0, 0)
    return pl.pallas_call(
        functools.partial(_out_router_kernel, n_parts=n_parts),
        grid=(T // to,),
        in_specs=[pl.BlockSpec((to, D), row), pl.BlockSpec((to, D), row),
                  pl.BlockSpec((D, D), fixed), pl.BlockSpec((1, D), fixed),
                  pl.BlockSpec((D, 2 * E), fixed), pl.BlockSpec((1, E), fixed)],
        out_specs=[pl.BlockSpec((to, D), row), pl.BlockSpec((to, D // 2), row),
                   pl.BlockSpec((to, E), row), pl.BlockSpec((to, E), row)],
        out_shape=[jax.ShapeDtypeStruct((T, D), F32), jax.ShapeDtypeStruct((T, D // 2), jnp.uint32),
                   jax.ShapeDtypeStruct((T, E), F32), jax.ShapeDtypeStruct((T, E), F32)],
        compiler_params=_cparams(("parallel",)),
        name="out_router",
    )(merged, x, wo, g_ffn.reshape(1, D), w_router, b_router.reshape(1, E))


def _routing_kernel(sel_ref, gate_ref, dest_ref, g4_ref, nblk_ref, start_ref, rank_ref, *, tile):
    T, E = sel_ref.shape
    nt = T // tile
    r = lax.broadcasted_iota(I32, (tile, tile), 0)
    c = lax.broadcasted_iota(I32, (tile, tile), 1)
    strict_lower = (c < r).astype(BF16)
    er = lax.broadcasted_iota(I32, (E, E), 0)
    ec = lax.broadcasted_iota(I32, (E, E), 1)
    strict_upper = (er < ec).astype(BF16)

    def pass1(t, carry):
        rows = pl.ds(pl.multiple_of(t * tile, tile), tile)
        a = sel_ref[rows, :]
        rank_ref[rows, :] = jnp.dot(strict_lower, a.astype(BF16), preferred_element_type=F32) + carry
        return carry + jnp.sum(a, axis=0, keepdims=True)

    counts = lax.fori_loop(0, nt, pass1, jnp.zeros((1, E), F32))
    nblk = jnp.floor((counts + (ROW_BLOCK - 1)) * (1.0 / ROW_BLOCK))
    start_blk = jnp.dot(nblk.astype(BF16), strict_upper, preferred_element_type=F32)
    nblk_ref[...] = nblk.astype(I32)
    start_ref[...] = start_blk.astype(I32)
    start_row = start_blk * float(ROW_BLOCK)
    lane = lax.broadcasted_iota(I32, (tile, 128), 1)

    def pass2(t, _):
        rows = pl.ds(pl.multiple_of(t * tile, tile), tile)
        a = sel_ref[rows, :]
        g = gate_ref[rows, :]
        dest_e = rank_ref[rows, :] + start_row
        slot = jnp.dot(a.astype(BF16), strict_upper, preferred_element_type=F32)
        d4 = jnp.zeros((tile, 128), F32)
        g4 = jnp.zeros((tile, 128), F32)
        for s in range(TOP_K):
            pick = (a > 0.5) & (slot == float(s))
            d4 = jnp.where(lane == s, jnp.sum(jnp.where(pick, dest_e, 0.0), axis=-1, keepdims=True), d4)
            g4 = jnp.where(lane == s, jnp.sum(jnp.where(pick, g, 0.0), axis=-1, keepdims=True), g4)
        dest_ref[rows, :] = d4.astype(I32)
        g4_ref[rows, :] = g4
        return 0

    lax.fori_loop(0, nt, pass2, 0)


def _routing(sel, gate, tile=512):
    T, E = sel.shape
    return pl.pallas_call(
        functools.partial(_routing_kernel, tile=tile),
        out_shape=[jax.ShapeDtypeStruct((T, 128), I32), jax.ShapeDtypeStruct((T, 128), F32),
                   jax.ShapeDtypeStruct((1, E), I32), jax.ShapeDtypeStruct((1, E), I32)],
        scratch_shapes=[pltpu.VMEM((T, E), F32)],
        compiler_params=pltpu.CompilerParams(vmem_limit_bytes=VMEM_LIMIT_BYTES),
        name="routing_ranks",
    )(sel, gate)


SC_CORES = 2
SC_SUBCORES = 16
SC_CHUNK = 64


def _sc_gather_rows(table, idx):
    n = idx.shape[0]
    W = table.shape[1]
    n_workers = SC_CORES * SC_SUBCORES
    per_worker = n // n_workers
    assert per_worker * n_workers == n and per_worker % SC_CHUNK == 0
    mesh = plsc.VectorSubcoreMesh(core_axis_name="c", subcore_axis_name="s",
                                  num_cores=SC_CORES, num_subcores=SC_SUBCORES)

    @functools.partial(
        pl.kernel, mesh=mesh,
        out_type=jax.ShapeDtypeStruct((n, W), table.dtype),
        scratch_types=[pltpu.VMEM((SC_CHUNK,), I32), pltpu.VMEM((SC_CHUNK, W), table.dtype),
                       pltpu.SemaphoreType.DMA],
        name="sc_gather_rows",
    )
    def gather(table_hbm, idx_hbm, out_hbm, idx_v, rows_v, sem):
        wid = lax.axis_index("s") * SC_CORES + lax.axis_index("c")
        base = wid * per_worker

        @pl.loop(0, per_worker // SC_CHUNK)
        def _(c):
            off = pl.multiple_of(base + c * SC_CHUNK, 8)
            pltpu.sync_copy(idx_hbm.at[pl.ds(off, SC_CHUNK)], idx_v)
            pltpu.async_copy(table_hbm.at[idx_v], rows_v, sem).wait()
            pltpu.sync_copy(rows_v, out_hbm.at[pl.ds(off, SC_CHUNK)])

    return gather(table, idx)


def _sc_scatter_rows(rows, idx, n_out, n_slots):
    T, W = rows.shape
    n_workers = SC_CORES * SC_SUBCORES
    per_worker = T // n_workers
    assert per_worker * n_workers == T and per_worker % SC_CHUNK == 0
    mesh = plsc.VectorSubcoreMesh(core_axis_name="c", subcore_axis_name="s",
                                  num_cores=SC_CORES, num_subcores=SC_SUBCORES)

    @functools.partial(
        pl.kernel, mesh=mesh,
        out_type=jax.ShapeDtypeStruct((n_out, W), rows.dtype),
        scratch_types=[pltpu.VMEM((SC_CHUNK,), I32), pltpu.VMEM((SC_CHUNK, W), rows.dtype)],
        name="sc_scatter_rows",
    )
    def scatter(rows_hbm, idx_hbm, out_hbm, idx_v, rows_v):
        wid = lax.axis_index("s") * SC_CORES + lax.axis_index("c")
        base = wid * per_worker

        @pl.loop(0, per_worker // SC_CHUNK)
        def _(c):
            off = pl.multiple_of(base + c * SC_CHUNK, 8)
            pltpu.sync_copy(rows_hbm.at[pl.ds(off, SC_CHUNK)], rows_v)
            for k in range(n_slots):
                pltpu.sync_copy(idx_hbm.at[pl.ds(pl.multiple_of(k * T + off, 8), SC_CHUNK)], idx_v)
                pltpu.sync_copy(rows_v, out_hbm.at[idx_v])

    return scatter(rows, idx)


def _expert_kernel(ie_ref, ib_ref, ins_ref,
                   xs_hbm, wup_ref, bup_ref, wdn_ref, bdn_ref, perm_ref, ys_hbm,
                   xg_ref, acc_ref, yst_ref, wupb_ref, wdnb_ref, gsem, osem, *, n_ff_tiles):
    i = pl.program_id(0)
    j = pl.program_id(1)
    n_items = pl.num_programs(0)
    nsub = ins_ref[i]
    slot = i % 2

    def for_regions(n_blocks, fn):
        first = jnp.int32(0)
        for count in REGION_BLOCKS:
            present = (n_blocks & count) != 0
            pl.when(present)(functools.partial(fn, first, count))
            first = first + jnp.where(present, count, 0)

    def rows_of(first, count):
        return pl.ds(pl.multiple_of(first * ROW_BLOCK, ROW_BLOCK), count * ROW_BLOCK)

    def in_copy(item, dst_slot, first, count):
        src = pl.multiple_of((ib_ref[item] + first) * ROW_BLOCK, ROW_BLOCK)
        return pltpu.make_async_copy(xs_hbm.at[pl.ds(src, count * ROW_BLOCK)],
                                     xg_ref.at[dst_slot, rows_of(first, count)], gsem.at[dst_slot])

    def fetch_item(item, n_blocks, dst_slot):
        for_regions(n_blocks, lambda first, count: in_copy(item, dst_slot, first, count).start())

    @pl.when(j == 0)
    def _():
        @pl.when(i == 0)
        def _():
            fetch_item(0, nsub, 0)
            acc_ref[...] = jnp.zeros(acc_ref.shape, F32)

        nxt = jnp.minimum(i + 1, n_items - 1)
        fetch_item(nxt, jnp.where(i + 1 < n_items, ins_ref[nxt], 0), 1 - slot)
        for_regions(nsub, lambda first, count: in_copy(i, slot, first, count).wait())

    @pl.when(nsub > 0)
    def _():
        bup = bup_ref[0]
        perm = perm_ref[...]
        half = perm.shape[0] // 2

        def run_blocks(first, count):
            if first == 0:
                wup = wup_ref[0].astype(BF16)
                wdn = wdn_ref[0].astype(BF16)
                wupb_ref[...] = wup
                wdnb_ref[...] = wdn
            else:
                wup = wupb_ref[...]
                wdn = wdnb_ref[...]
            rows = pl.ds(first * ROW_BLOCK, count * ROW_BLOCK)
            xb = _unpack_pairs(xg_ref[slot, rows, :]).astype(BF16)
            gu = (jnp.dot(xb, wup, preferred_element_type=F32) + bup).astype(BF16)
            glu_parts, lin_parts = [], []
            for p in range(gu.shape[1] // perm.shape[0]):
                gp = jnp.dot(gu[:, p * perm.shape[0]:(p + 1) * perm.shape[0]], perm,
                             preferred_element_type=F32)
                glu_parts.append(gp[:, :half])
                lin_parts.append(gp[:, half:])
            x_glu = jnp.minimum(jnp.concatenate(glu_parts, axis=1), SWIGLU_LIMIT)
            x_lin = jnp.clip(jnp.concatenate(lin_parts, axis=1), -SWIGLU_LIMIT, SWIGLU_LIMIT)
            act = x_glu * jax.nn.sigmoid(SWIGLU_ALPHA * x_glu) * (x_lin + 1.0)
            start = jnp.where(j == 0, bdn_ref[0], acc_ref[rows, :])
            acc_ref[rows, :] = start + jnp.dot(act.astype(BF16), wdn, preferred_element_type=F32)

        for n_blocks, plan in COMPUTE_PLANS.items():
            for first, count in plan:
                shared = [n for n, p in COMPUTE_PLANS.items() if (first, count) in p]
                if n_blocks == shared[0]:
                    cond = functools.reduce(jnp.logical_or, [nsub == n for n in shared])
                    pl.when(cond)(functools.partial(run_blocks, first, count))

    def out_copy(item, first, count):
        dst = pl.multiple_of((ib_ref[item] + first) * ROW_BLOCK, ROW_BLOCK)
        return pltpu.make_async_copy(yst_ref.at[rows_of(first, count)],
                                     ys_hbm.at[pl.ds(dst, count * ROW_BLOCK)], osem)

    @pl.when(j == n_ff_tiles - 1)
    def _():
        prev = jnp.maximum(i - 1, 0)
        for_regions(jnp.where(i > 0, ins_ref[prev], 0),
                    lambda first, count: out_copy(prev, first, count).wait())

        def leave(first, count):
            rows = rows_of(first, count)
            yst_ref[rows, :] = _pack_pairs(acc_ref[rows, :])
            out_copy(i, first, count).start()

        for_regions(nsub, leave)

        @pl.when(i == n_items - 1)
        def _():
            for_regions(nsub, lambda first, count: out_copy(i, first, count).wait())


def _experts(xs, item_e, item_b, item_n, n_active, w_up, b_up, w_down, b_down):
    n_rows, Dp = xs.shape
    D = 2 * Dp
    E, _, F2 = w_up.shape
    F = F2 // 2
    J = F // FF_TILE
    half = 128
    perm = np.zeros((2 * half, 2 * half), np.float32)
    perm[2 * np.arange(half), np.arange(half)] = 1.0
    perm[2 * np.arange(half) + 1, half + np.arange(half)] = 1.0

    def jj(i, j, ins):
        return jnp.where(ins[i] > 0, j, J - 1)

    grid_spec = pltpu.PrefetchScalarGridSpec(
        num_scalar_prefetch=3,
        grid=(n_active, J),
        in_specs=[pl.BlockSpec(memory_space=pl.ANY),
                  pl.BlockSpec((1, D, 2 * FF_TILE), lambda i, j, ie, ib, ins: (ie[i], 0, jj(i, j, ins))),
                  pl.BlockSpec((1, 1, 2 * FF_TILE), lambda i, j, ie, ib, ins: (ie[i], 0, jj(i, j, ins))),
                  pl.BlockSpec((1, FF_TILE, D), lambda i, j, ie, ib, ins: (ie[i], jj(i, j, ins), 0)),
                  pl.BlockSpec((1, 1, D), lambda i, j, ie, ib, ins: (ie[i], 0, 0)),
                  pl.BlockSpec((2 * half, 2 * half), lambda i, j, ie, ib, ins: (0, 0))],
        out_specs=pl.BlockSpec(memory_space=pl.ANY),
        scratch_shapes=[pltpu.VMEM((2, ITEM_ROWS, Dp), jnp.uint32),
                        pltpu.VMEM((ITEM_ROWS, D), F32),
                        pltpu.VMEM((ITEM_ROWS, Dp), jnp.uint32),
                        pltpu.VMEM((D, 2 * FF_TILE), BF16),
                        pltpu.VMEM((FF_TILE, D), BF16),
                        pltpu.SemaphoreType.DMA((2,)), pltpu.SemaphoreType.DMA(())],
    )
    return pl.pallas_call(
        functools.partial(_expert_kernel, n_ff_tiles=J),
        grid_spec=grid_spec,
        out_shape=jax.ShapeDtypeStruct((n_rows, Dp), jnp.uint32),
        compiler_params=_cparams(("arbitrary", "arbitrary")),
        name="expert_ffn",
    )(item_e, item_b, item_n,
      xs, w_up, b_up.reshape(E, 1, F2), w_down, b_down.reshape(E, 1, D), jnp.asarray(perm, BF16))


def _combine_kernel(*refs):
    yk_refs, (g4_ref, x1_ref, g_ref), o_ref = refs[:TOP_K], refs[TOP_K:TOP_K + 3], refs[-1]
    y = x1_ref[...]
    g4 = g4_ref[...]
    for k in range(TOP_K):
        y = y + g4[:, k:k + 1] * _unpack_pairs(yk_refs[k][...])
    ms = jnp.mean(y * y, axis=-1, keepdims=True)
    o_ref[...] = y * lax.rsqrt(ms + NORM_EPS) * g_ref[...]


def _combine(yk, g4, x1, g_final, part, n_parts, out_so_far, tc=512):
    T, D = x1.shape
    nt = T // n_parts // tc
    first = part * nt
    slot_specs = [pl.BlockSpec((tc, D // 2), functools.partial(lambda i, k: (k * nt + i, 0), k=k))
                  for k in range(TOP_K)]
    operands = [*([yk] * TOP_K), g4, x1, g_final.reshape(1, D)]
    in_specs = slot_specs + [pl.BlockSpec((tc, 128), lambda i: (first + i, 0)),
                             pl.BlockSpec((tc, D), lambda i: (first + i, 0)),
                             pl.BlockSpec((1, D), lambda i: (0, 0))]
    aliases = {}
    if out_so_far is not None:
        aliases = {len(operands): 0}
        operands.append(out_so_far)
        in_specs.append(pl.BlockSpec(memory_space=pl.ANY))
    return pl.pallas_call(
        _combine_kernel,
        grid=(nt,),
        in_specs=in_specs,
        out_specs=pl.BlockSpec((tc, D), lambda i: (first + i, 0)),
        out_shape=jax.ShapeDtypeStruct((T, D), F32),
        input_output_aliases=aliases,
        compiler_params=_cparams(("parallel",)),
        name="combine_norm",
    )(*operands)


def _work_items(nblk, start_blk, n_items):
    E = nblk.shape[0]
    per_e = (nblk + ITEM_BLOCKS - 1) // ITEM_BLOCKS
    ends = jnp.cumsum(per_e)
    total = ends[-1]
    idx = jnp.arange(n_items, dtype=I32)
    e = jnp.minimum(jnp.searchsorted(ends, idx, side="right"), E - 1).astype(I32)
    local = idx - (ends[e] - per_e[e])
    active = idx < total
    last_e = e[jnp.maximum(total - 1, 0)]
    item_e = jnp.where(active, e, last_e).astype(I32)
    item_b = jnp.where(active, start_blk[e] + local * ITEM_BLOCKS, 0).astype(I32)
    item_n = jnp.where(active, jnp.clip(nblk[e] - local * ITEM_BLOCKS, 0, ITEM_BLOCKS), 0).astype(I32)
    return item_e, item_b, item_n, jnp.maximum(total, 1).astype(I32)


def kernel(x, g_mix, w_in, b_forget, g_v_ln, b_v_ln, w_spatial, b_spatial, w_branch_attn, w_branch_gmlp, w_out, g_ffn, w_router, b_router, w_expert_up, b_expert_up, w_expert_down, b_expert_down, g_final):
    B, S, D = x.shape
    T = B * S
    n_heads = b_forget.shape[0]
    attn_w = n_heads * HEAD_DIM
    gmlp_w = g_v_ln.shape[0]
    E = w_router.shape[1]
    off_f = 3 * attn_w
    off_z = off_f + n_heads
    off_g = off_z + 2 * gmlp_w

    x2 = x.reshape(T, D)
    w_in_t = w_in.T
    wzg = _rows_bf16(w_in_t, off_z, w_in.shape[1] - off_z)
    wa = w_branch_attn.astype(BF16)
    wb = w_branch_gmlp.astype(BF16)
    wo = w_out.astype(BF16)

    qkv, h = _norm_project(x2, g_mix, w_in_t, 3 * attn_w, attn_w, LOG2E / math.sqrt(HEAD_DIM))
    c_row = _forget_cumsum(h, w_in_t, off_f, b_forget, B, S)
    attn = _attention(qkv, c_row, B, S, n_heads)
    sg = _gmlp(h, wzg, g_v_ln, b_v_ln, w_spatial, b_spatial)
    merged = _merge(attn, sg, h, wa, wb, wzg, off_g - off_z)
    x1, h2p, sel, gate = _out_router(merged, x2, wo, g_ffn, w_router, b_router)

    dest4, g4, nblk, start_blk = _routing(sel, gate)
    n_rows = T * TOP_K + E * ROW_BLOCK
    dest_slots = dest4[:, :TOP_K].T.reshape(TOP_K * T)
    max_blocks = n_rows // ROW_BLOCK
    n_items = E + -(-(max_blocks - E) // ITEM_BLOCKS) + 1
    item_e, item_b, item_n, n_active = _work_items(nblk[0], start_blk[0], n_items)
    xs = _sc_scatter_rows(h2p, dest_slots, n_rows, TOP_K)
    ys = _experts(xs, item_e, item_b, item_n, n_active,
                  w_expert_up, b_expert_up, w_expert_down, b_expert_down)
    out = None
    t_part = T // COMBINE_PARTS
    for p in range(COMBINE_PARTS):
        idx = dest4[p * t_part:(p + 1) * t_part, :TOP_K].T.reshape(TOP_K * t_part)
        yk = _sc_gather_rows(ys, idx)
        out = _combine(yk, g4, x1, g_final, p, COMBINE_PARTS, out)
    return out.reshape(B, S, D)
```

```python
import functools
import math

import jax
import jax.numpy as jnp
import numpy as np
from jax import lax
from jax.experimental import pallas as pl
from jax.experimental.pallas import tpu as pltpu
from jax.experimental.pallas import tpu_sc as plsc

F32 = jnp.float32
BF16 = jnp.bfloat16
I32 = jnp.int32

NORM_EPS = 1e-5
LANES = 128
SUBLANES = 8
NT_DIMS = (((1,), (1,)), ((), ()))
HEAD_DIM = 128
CHUNK = 128
GROUP_DIM = 128
TOP_K = 4
SWIGLU_ALPHA = 1.702
SWIGLU_LIMIT = 7.0
LOG2E = math.log2(math.e)

VMEM_LIMIT_BYTES = 56 * 1024 * 1024

ROW_BLOCK = 128
ITEM_BLOCKS = 10
ITEM_ROWS = ITEM_BLOCKS * ROW_BLOCK
REGION_BLOCKS = (8, 4, 2, 1)
assert sum(REGION_BLOCKS) >= ITEM_BLOCKS
COMPUTE_PLANS = {**{n: ((0, n),) for n in range(4, ITEM_BLOCKS + 1)},
                 3: ((0, 2), (2, 1)), 2: ((0, 2),), 1: ((0, 1),)}
assert set(COMPUTE_PLANS) == set(range(1, ITEM_BLOCKS + 1))
FF_TILE = 256
COMBINE_PARTS = 1


def _cparams(sem, **kw):
    return pltpu.CompilerParams(dimension_semantics=sem, vmem_limit_bytes=VMEM_LIMIT_BYTES, **kw)


def _pack_pairs(x):
    c = x.shape[1] // 2
    hi = lax.bitcast_convert_type(x[:, :c].astype(BF16).astype(F32), jnp.uint32)
    lo = lax.bitcast_convert_type(x[:, c:].astype(BF16).astype(F32), jnp.uint32)
    return hi | (lo >> 16)


def _unpack_pairs(w):
    hi = lax.bitcast_convert_type(w & jnp.uint32(0xFFFF0000), F32)
    lo = lax.bitcast_convert_type(w << 16, F32)
    return jnp.concatenate([hi, lo], axis=1)


def _norm_proj_kernel(x_ref, g_ref, w_ref, o_ref, h_ref, *, n_scaled, scale):
    j = pl.program_id(1)

    @pl.when(j == 0)
    def _():
        x = x_ref[...]
        ms = jnp.mean(x * x, axis=-1, keepdims=True)
        h_ref[...] = (x * lax.rsqrt(ms + NORM_EPS) * g_ref[...]).astype(h_ref.dtype)

    acc = lax.dot_general(h_ref[...], w_ref[...].astype(BF16), NT_DIMS, preferred_element_type=F32)
    o_ref[...] = (acc * jnp.where(j < n_scaled, scale, 1.0)).astype(o_ref.dtype)


def _norm_project(x, g, wt, n_cols, n_scaled_cols, scale, tm=1024, tn=1024):
    T, D = x.shape
    return pl.pallas_call(
        functools.partial(_norm_proj_kernel, n_scaled=n_scaled_cols // tn, scale=scale),
        grid=(T // tm, n_cols // tn),
        in_specs=[pl.BlockSpec((tm, D), lambda i, j: (i, 0)),
                  pl.BlockSpec((1, D), lambda i, j: (0, 0)),
                  pl.BlockSpec((tn, D), lambda i, j: (j, 0))],
        out_specs=[pl.BlockSpec((tm, tn), lambda i, j: (i, j)),
                   pl.BlockSpec((tm, D), lambda i, j: (i, 0))],
        out_shape=[jax.ShapeDtypeStruct((T, n_cols), BF16), jax.ShapeDtypeStruct((T, D), BF16)],
        compiler_params=_cparams(("arbitrary", "arbitrary")),
        name="norm_qkv_proj",
    )(x, g.reshape(1, D), wt)


def _cast_kernel(w_ref, o_ref):
    o_ref[...] = w_ref[...].astype(o_ref.dtype)


def _rows_bf16(wt, start, n_rows, tr=1024):
    D = wt.shape[1]
    assert start % SUBLANES == 0 and n_rows % tr == 0
    return pl.pallas_call(
        _cast_kernel,
        grid=(n_rows // tr,),
        in_specs=[pl.BlockSpec((pl.Element(tr), pl.Element(D)),
                               lambda i: (pl.multiple_of(start + i * tr, SUBLANES), 0))],
        out_specs=pl.BlockSpec((tr, D), lambda i: (i, 0)),
        out_shape=jax.ShapeDtypeStruct((n_rows, D), BF16),
        compiler_params=_cparams(("parallel",)),
        name="rows_bf16",
    )(wt)


def _forget_kernel(h_ref, wft_ref, bf_ref, c_ref):
    ft = lax.dot_general(wft_ref[...].astype(BF16), h_ref[...], NT_DIMS,
                         preferred_element_type=F32)
    c = jax.nn.log_sigmoid(ft + bf_ref[...])
    S = c.shape[1]
    lane = lax.broadcasted_iota(I32, c.shape, 1)
    shift = 1
    while shift < S:
        c = c + jnp.where(lane >= shift, pltpu.roll(c, shift, axis=1), 0.0)
        shift *= 2
    c_ref[0] = c * LOG2E


def _forget_cumsum(h, wt, f_off, b_forget, B, S):
    T, D = h.shape
    H = b_forget.shape[0]
    assert f_off % H == 0 and H % SUBLANES == 0
    return pl.pallas_call(
        _forget_kernel,
        grid=(B,),
        in_specs=[pl.BlockSpec((S, D), lambda b: (b, 0)),
                  pl.BlockSpec((H, D), lambda b: (f_off // H, 0)),
                  pl.BlockSpec((H, 1), lambda b: (0, 0))],
        out_specs=pl.BlockSpec((1, H, S), lambda b: (b, 0, 0)),
        out_shape=jax.ShapeDtypeStruct((B, H, S), F32),
        compiler_params=_cparams(("parallel",)),
        name="forget_cumsum",
    )(h, wt, b_forget.reshape(H, 1))


def _attn_kernel(q_ref, k_ref, v_ref, crow_ref, o_ref, vaug_ref, m_ref, acc_ref, *, n_heads, tq):
    i = pl.program_id(1)

    @pl.when(i == 0)
    def _():
        ones = jnp.ones((v_ref.shape[0], HEAD_DIM), BF16)
        for h in range(n_heads):
            vaug_ref[h, :, :HEAD_DIM] = v_ref[:, h * HEAD_DIM:(h + 1) * HEAD_DIM]
            vaug_ref[h, :, HEAD_DIM:] = ones

    m_ref[...] = jnp.full(m_ref.shape, -jnp.inf, F32)
    acc_ref[...] = jnp.zeros(acc_ref.shape, F32)
    row = lax.broadcasted_iota(I32, (tq, tq), 0)
    col = lax.broadcasted_iota(I32, (tq, tq), 1)
    causal = col <= row

    def step(j, masked):
        keys = pl.ds(pl.multiple_of(j * tq, tq), tq)
        for h in range(n_heads):
            hs = slice(h * HEAD_DIM, (h + 1) * HEAD_DIM)
            s = lax.dot_general(q_ref[:, hs], k_ref[keys, hs], (((1,), (1,)), ((), ())),
                                preferred_element_type=F32) - crow_ref[0, h, j]
            if masked:
                s = jnp.where(causal, s, -jnp.inf)
            m_old = m_ref[h]
            m_new = jnp.maximum(m_old, jnp.max(s, axis=-1, keepdims=True))
            alpha = jnp.exp2(m_old - m_new)
            p = jnp.exp2(s - jnp.concatenate([m_new] * (tq // HEAD_DIM), axis=1))
            m_ref[h] = m_new
            pv = jnp.dot(p.astype(BF16), vaug_ref[h, keys, :], preferred_element_type=F32)
            acc_ref[h] = jnp.concatenate([alpha, alpha], axis=1) * acc_ref[h] + pv

    def body(j, _):
        step(j, False)
        return 0

    lax.fori_loop(0, i, body, 0)
    step(i, True)
    for h in range(n_heads):
        acc = acc_ref[h]
        o_ref[:, h * HEAD_DIM:(h + 1) * HEAD_DIM] = (acc[:, :HEAD_DIM] / acc[:, HEAD_DIM:]).astype(o_ref.dtype)


def _attention(qkv, c_row, B, S, n_heads, tq=512):
    T = qkv.shape[0]
    W = n_heads * HEAD_DIM
    nq = S // tq
    c_row5 = c_row.reshape(B, n_heads, nq, 1, tq)
    return pl.pallas_call(
        functools.partial(_attn_kernel, n_heads=n_heads, tq=tq),
        grid=(B, nq),
        in_specs=[pl.BlockSpec((tq, W), lambda b, i: (b * nq + i, 0)),
                  pl.BlockSpec((S, W), lambda b, i: (b, 1)),
                  pl.BlockSpec((S, W), lambda b, i: (b, 2)),
                  pl.BlockSpec((1, n_heads, nq, 1, tq), lambda b, i: (b, 0, 0, 0, 0))],
        out_specs=pl.BlockSpec((tq, W), lambda b, i: (b * nq + i, 0)),
        out_shape=jax.ShapeDtypeStruct((T, W), BF16),
        scratch_shapes=[pltpu.VMEM((n_heads, S, 2 * HEAD_DIM), BF16),
                        pltpu.VMEM((n_heads, tq, HEAD_DIM), F32),
                        pltpu.VMEM((n_heads, tq, 2 * HEAD_DIM), F32)],
        compiler_params=_cparams(("arbitrary", "arbitrary")),
        name="fox_attention",
    )(qkv, qkv, qkv, c_row5)


def _gmlp_kernel(h_ref, wz_ref, g_ref, b_ref, ws_ref, bst_ref, o_ref, *, n_groups):
    z = lax.dot_general(h_ref[...], wz_ref[...], NT_DIMS, preferred_element_type=F32)
    z = 0.5 * z * (1.0 + lax.erf(z * (1.0 / math.sqrt(2.0))))
    W = z.shape[1] // 2
    u = z[:, :W]
    v = z[:, W:]
    mu = jnp.mean(v, axis=-1, keepdims=True)
    var = jnp.mean(jnp.square(v - mu), axis=-1, keepdims=True)
    vn = (v - mu) * lax.rsqrt(var + NORM_EPS) * g_ref[...] + b_ref[...]
    row = lax.broadcasted_iota(I32, (CHUNK, CHUNK), 0)
    col = lax.broadcasted_iota(I32, (CHUNK, CHUNK), 1)
    tril = col <= row
    tg = z.shape[0]
    for g in range(n_groups):
        gs = slice(g * GROUP_DIM, (g + 1) * GROUP_DIM)
        wg = jnp.where(tril, ws_ref[g], 0.0).astype(BF16)
        bias = bst_ref[:, g:g + 1]
        for c in range(tg // CHUNK):
            cs = slice(c * CHUNK, (c + 1) * CHUNK)
            mixed = jnp.dot(wg, vn[cs, gs].astype(BF16), preferred_element_type=F32) + bias
            o_ref[cs, gs] = (u[cs, gs] * mixed).astype(o_ref.dtype)


def _gmlp(h, wz, g_v_ln, b_v_ln, w_spatial, b_spatial, tg=1024):
    T, D = h.shape
    W = g_v_ln.shape[0]
    W2 = 2 * W
    G = w_spatial.shape[0]
    return pl.pallas_call(
        functools.partial(_gmlp_kernel, n_groups=G),
        grid=(T // tg,),
        in_specs=[pl.BlockSpec((tg, D), lambda i: (i, 0)),
                  pl.BlockSpec((W2, D), lambda i: (0, 0)),
                  pl.BlockSpec((1, W), lambda i: (0, 0)),
                  pl.BlockSpec((1, W), lambda i: (0, 0)),
                  pl.BlockSpec((G, CHUNK, CHUNK), lambda i: (0, 0, 0)),
                  pl.BlockSpec((CHUNK, G), lambda i: (0, 0))],
        out_specs=pl.BlockSpec((tg, W), lambda i: (i, 0)),
        out_shape=jax.ShapeDtypeStruct((T, W), BF16),
        compiler_params=_cparams(("parallel",)),
        name="gmlp",
    )(h, wz, g_v_ln.reshape(1, W), b_v_ln.reshape(1, W), w_spatial, b_spatial.T)


def _merge_kernel(attn_ref, sg_ref, h_ref, wa_ref, wb_ref, wga_ref, wgb_ref, o_ref):
    h = h_ref[...]
    a = jnp.dot(attn_ref[...], wa_ref[...], preferred_element_type=F32)
    ga = lax.dot_general(h, wga_ref[...], NT_DIMS, preferred_element_type=F32)
    m = jax.nn.sigmoid(ga) * a
    b = jnp.dot(sg_ref[...], wb_ref[...], preferred_element_type=F32)
    gb = lax.dot_general(h, wgb_ref[...], NT_DIMS, preferred_element_type=F32)
    o_ref[...] = (m + jax.nn.sigmoid(gb) * b).astype(o_ref.dtype)


def _merge(attn, sg, h, wa, wb, wg, g_off, tm=1024, tn=512):
    T, D = h.shape
    Wa = attn.shape[1]
    Wb = sg.shape[1]
    nt = D // tn
    g0 = g_off // tn
    return pl.pallas_call(
        _merge_kernel,
        grid=(nt, T // tm),
        in_specs=[pl.BlockSpec((tm, Wa), lambda j, i: (i, 0)),
                  pl.BlockSpec((tm, Wb), lambda j, i: (i, 0)),
                  pl.BlockSpec((tm, D), lambda j, i: (i, 0)),
                  pl.BlockSpec((Wa, tn), lambda j, i: (0, j)),
                  pl.BlockSpec((Wb, tn), lambda j, i: (0, j)),
                  pl.BlockSpec((tn, D), lambda j, i: (g0 + j, 0)),
                  pl.BlockSpec((tn, D), lambda j, i: (g0 + nt + j, 0))],
        out_specs=pl.BlockSpec((tm, tn), lambda j, i: (i, j)),
        out_shape=jax.ShapeDtypeStruct((T, D), BF16),
        compiler_params=_cparams(("arbitrary", "arbitrary")),
        name="gated_merge",
    )(attn, sg, h, wa, wb, wg, wg)


def _out_router_kernel(m_ref, x_ref, wo_ref, g_ref, wr_ref, br_ref,
                       x1_ref, h2_ref, sel_ref, gate_ref, *, n_parts):
    E = br_ref.shape[1]
    part = m_ref.shape[0] // n_parts
    for p in range(n_parts):
        rows = pl.ds(p * part, part)
        x1 = x_ref[rows, :] + jnp.dot(m_ref[rows, :], wo_ref[...], preferred_element_type=F32)
        x1_ref[rows, :] = x1
        ms = jnp.mean(x1 * x1, axis=-1, keepdims=True)
        h2 = x1 * lax.rsqrt(ms + NORM_EPS) * g_ref[...]
        h2_ref[rows, :] = _pack_pairs(h2)
        h2_hi = h2.astype(BF16)
        h2_lo = (h2 - h2_hi.astype(F32)).astype(BF16)
        pa = jnp.dot(h2_hi, wr_ref[...], preferred_element_type=F32)
        pb = jnp.dot(h2_lo, wr_ref[:, :E], preferred_element_type=F32)
        logits = pa[:, :E] + (pa[:, E:] + pb) + br_ref[...]
        lane = lax.broadcasted_iota(I32, logits.shape, 1)
        work = logits
        sel = jnp.zeros(logits.shape, F32)
        num = jnp.zeros(logits.shape, F32)
        denom = jnp.zeros((part, 1), F32)
        m0 = None
        for _ in range(TOP_K):
            m = jnp.max(work, axis=-1, keepdims=True)
            idx = jnp.min(jnp.where(work == m, lane, E), axis=-1, keepdims=True)
            onehot = lane == idx
            if m0 is None:
                m0 = m
            e = jnp.exp(m - m0)
            sel = jnp.where(onehot, 1.0, sel)
            num = jnp.where(onehot, e, num)
            denom = denom + e
            work = jnp.where(onehot, -jnp.inf, work)
        sel_ref[rows, :] = sel
        gate_ref[rows, :] = num / denom


def _out_router(merged, x, wo, g_ffn, w_router, b_router, to=512, n_parts=2):
    T, D = x.shape
    E = w_router.shape[1]
    w_hi = w_router.astype(BF16)
    w_lo = (w_router - w_hi.astype(F32)).astype(BF16)
    w_router = jnp.concatenate([w_hi, w_lo], axis=1)
    row = lambda i: (i, 0)
    fixed = lambda i: (0, 0)
    return pl.pallas_call(
        functools.partial(_out_router_kernel, n_parts=n_parts),
        grid=(T // to,),
        in_specs=[pl.BlockSpec((to, D), row), pl.BlockSpec((to, D), row),
                  pl.BlockSpec((D, D), fixed), pl.BlockSpec((1, D), fixed),
                  pl.BlockSpec((D, 2 * E), fixed), pl.BlockSpec((1, E), fixed)],
        out_specs=[pl.BlockSpec((to, D), row), pl.BlockSpec((to, D // 2), row),
                   pl.BlockSpec((to, E), row), pl.BlockSpec((to, E), row)],
        out_shape=[jax.ShapeDtypeStruct((T, D), F32), jax.ShapeDtypeStruct((T, D // 2), jnp.uint32),
                   jax.ShapeDtypeStruct((T, E), F32), jax.ShapeDtypeStruct((T, E), F32)],
        compiler_params=_cparams(("parallel",)),
        name="out_router",
    )(merged, x, wo, g_ffn.reshape(1, D), w_router, b_router.reshape(1, E))


def _routing_kernel(sel_ref, gate_ref, dest_ref, g4_ref, nblk_ref, start_ref, rank_ref, *, tile):
    T, E = sel_ref.shape
    nt = T // tile
    r = lax.broadcasted_iota(I32, (tile, tile), 0)
    c = lax.broadcasted_iota(I32, (tile, tile), 1)
    strict_lower = (c < r).astype(BF16)
    er = lax.broadcasted_iota(I32, (E, E), 0)
    ec = lax.broadcasted_iota(I32, (E, E), 1)
    strict_upper = (er < ec).astype(BF16)

    def pass1(t, carry):
        rows = pl.ds(pl.multiple_of(t * tile, tile), tile)
        a = sel_ref[rows, :]
        rank_ref[rows, :] = jnp.dot(strict_lower, a.astype(BF16), preferred_element_type=F32) + carry
        return carry + jnp.sum(a, axis=0, keepdims=True)

    counts = lax.fori_loop(0, nt, pass1, jnp.zeros((1, E), F32))
    nblk = jnp.floor((counts + (ROW_BLOCK - 1)) * (1.0 / ROW_BLOCK))
    start_blk = jnp.dot(nblk.astype(BF16), strict_upper, preferred_element_type=F32)
    nblk_ref[...] = nblk.astype(I32)
    start_ref[...] = start_blk.astype(I32)
    start_row = start_blk * float(ROW_BLOCK)
    lane = lax.broadcasted_iota(I32, (tile, 128), 1)

    def pass2(t, _):
        rows = pl.ds(pl.multiple_of(t * tile, tile), tile)
        a = sel_ref[rows, :]
        g = gate_ref[rows, :]
        dest_e = rank_ref[rows, :] + start_row
        slot = jnp.dot(a.astype(BF16), strict_upper, preferred_element_type=F32)
        d4 = jnp.zeros((tile, 128), F32)
        g4 = jnp.zeros((tile, 128), F32)
        for s in range(TOP_K):
            pick = (a > 0.5) & (slot == float(s))
            d4 = jnp.where(lane == s, jnp.sum(jnp.where(pick, dest_e, 0.0), axis=-1, keepdims=True), d4)
            g4 = jnp.where(lane == s, jnp.sum(jnp.where(pick, g, 0.0), axis=-1, keepdims=True), g4)
        dest_ref[rows, :] = d4.astype(I32)
        g4_ref[rows, :] = g4
        return 0

    lax.fori_loop(0, nt, pass2, 0)


def _routing(sel, gate, tile=512):
    T, E = sel.shape
    return pl.pallas_call(
        functools.partial(_routing_kernel, tile=tile),
        out_shape=[jax.ShapeDtypeStruct((T, 128), I32), jax.ShapeDtypeStruct((T, 128), F32),
                   jax.ShapeDtypeStruct((1, E), I32), jax.ShapeDtypeStruct((1, E), I32)],
        scratch_shapes=[pltpu.VMEM((T, E), F32)],
        compiler_params=pltpu.CompilerParams(vmem_limit_bytes=VMEM_LIMIT_BYTES),
        name="routing_ranks",
    )(sel, gate)


SC_CORES = 2
SC_SUBCORES = 16
SC_CHUNK = 64


def _sc_gather_rows(table, idx):
    n = idx.shape[0]
    W = table.shape[1]
    n_workers = SC_CORES * SC_SUBCORES
    per_worker = n // n_workers
    assert per_worker * n_workers == n and per_worker % SC_CHUNK == 0
    mesh = plsc.VectorSubcoreMesh(core_axis_name="c", subcore_axis_name="s",
                                  num_cores=SC_CORES, num_subcores=SC_SUBCORES)

    @functools.partial(
        pl.kernel, mesh=mesh,
        out_type=jax.ShapeDtypeStruct((n, W), table.dtype),
        scratch_types=[pltpu.VMEM((SC_CHUNK,), I32), pltpu.VMEM((SC_CHUNK, W), table.dtype),
                       pltpu.SemaphoreType.DMA],
        name="sc_gather_rows",
    )
    def gather(table_hbm, idx_hbm, out_hbm, idx_v, rows_v, sem):
        wid = lax.axis_index("s") * SC_CORES + lax.axis_index("c")
        base = wid * per_worker

        @pl.loop(0, per_worker // SC_CHUNK)
        def _(c):
            off = pl.multiple_of(base + c * SC_CHUNK, 8)
            pltpu.sync_copy(idx_hbm.at[pl.ds(off, SC_CHUNK)], idx_v)
            pltpu.async_copy(table_hbm.at[idx_v], rows_v, sem).wait()
            pltpu.sync_copy(rows_v, out_hbm.at[pl.ds(off, SC_CHUNK)])

    return gather(table, idx)


def _sc_scatter_rows(rows, idx, n_out, n_slots):
    T, W = rows.shape
    n_workers = SC_CORES * SC_SUBCORES
    per_worker = T // n_workers
    assert per_worker * n_workers == T and per_worker % SC_CHUNK == 0
    mesh = plsc.VectorSubcoreMesh(core_axis_name="c", subcore_axis_name="s",
                                  num_cores=SC_CORES, num_subcores=SC_SUBCORES)

    @functools.partial(
        pl.kernel, mesh=mesh,
        out_type=jax.ShapeDtypeStruct((n_out, W), rows.dtype),
        scratch_types=[pltpu.VMEM((SC_CHUNK,), I32), pltpu.VMEM((SC_CHUNK, W), rows.dtype)],
        name="sc_scatter_rows",
    )
    def scatter(rows_hbm, idx_hbm, out_hbm, idx_v, rows_v):
        wid = lax.axis_index("s") * SC_CORES + lax.axis_index("c")
        base = wid * per_worker

        @pl.loop(0, per_worker // SC_CHUNK)
        def _(c):
            off = pl.multiple_of(base + c * SC_CHUNK, 8)
            pltpu.sync_copy(rows_hbm.at[pl.ds(off, SC_CHUNK)], rows_v)
            for k in range(n_slots):
                pltpu.sync_copy(idx_hbm.at[pl.ds(pl.multiple_of(k * T + off, 8), SC_CHUNK)], idx_v)
                pltpu.sync_copy(rows_v, out_hbm.at[idx_v])

    return scatter(rows, idx)


def _expert_kernel(ie_ref, ib_ref, ins_ref,
                   xs_hbm, wup_ref, bup_ref, wdn_ref, bdn_ref, perm_ref, ys_hbm,
                   xg_ref, acc_ref, yst_ref, wupb_ref, wdnb_ref, gsem, osem, *, n_ff_tiles):
    i = pl.program_id(0)
    j = pl.program_id(1)
    n_items = pl.num_programs(0)
    nsub = ins_ref[i]
    slot = i % 2

    def for_regions(n_blocks, fn):
        first = jnp.int32(0)
        for count in REGION_BLOCKS:
            present = (n_blocks & count) != 0
            pl.when(present)(functools.partial(fn, first, count))
            first = first + jnp.where(present, count, 0)

    def rows_of(first, count):
        return pl.ds(pl.multiple_of(first * ROW_BLOCK, ROW_BLOCK), count * ROW_BLOCK)

    def in_copy(item, dst_slot, first, count):
        src = pl.multiple_of((ib_ref[item] + first) * ROW_BLOCK, ROW_BLOCK)
        return pltpu.make_async_copy(xs_hbm.at[pl.ds(src, count * ROW_BLOCK)],
                                     xg_ref.at[dst_slot, rows_of(first, count)], gsem.at[dst_slot])

    def fetch_item(item, n_blocks, dst_slot):
        for_regions(n_blocks, lambda first, count: in_copy(item, dst_slot, first, count).start())

    @pl.when(j == 0)
    def _():
        @pl.when(i == 0)
        def _():
            fetch_item(0, nsub, 0)
            acc_ref[...] = jnp.zeros(acc_ref.shape, F32)

        nxt = jnp.minimum(i + 1, n_items - 1)
        fetch_item(nxt, jnp.where(i + 1 < n_items, ins_ref[nxt], 0), 1 - slot)
        for_regions(nsub, lambda first, count: in_copy(i, slot, first, count).wait())

    @pl.when(nsub > 0)
    def _():
        bup = bup_ref[0]
        perm = perm_ref[...]
        half = perm.shape[0] // 2

        def run_blocks(first, count):
            if first == 0:
                wup = wup_ref[0].astype(BF16)
                wdn = wdn_ref[0].astype(BF16)
                wupb_ref[...] = wup
                wdnb_ref[...] = wdn
            else:
                wup = wupb_ref[...]
                wdn = wdnb_ref[...]
            rows = pl.ds(first * ROW_BLOCK, count * ROW_BLOCK)
            xb = _unpack_pairs(xg_ref[slot, rows, :]).astype(BF16)
            gu = (jnp.dot(xb, wup, preferred_element_type=F32) + bup).astype(BF16)
            glu_parts, lin_parts = [], []
            for p in range(gu.shape[1] // perm.shape[0]):
                gp = jnp.dot(gu[:, p * perm.shape[0]:(p + 1) * perm.shape[0]], perm,
                             preferred_element_type=F32)
                glu_parts.append(gp[:, :half])
                lin_parts.append(gp[:, half:])
            x_glu = jnp.minimum(jnp.concatenate(glu_parts, axis=1), SWIGLU_LIMIT)
            x_lin = jnp.clip(jnp.concatenate(lin_parts, axis=1), -SWIGLU_LIMIT, SWIGLU_LIMIT)
            act = x_glu * jax.nn.sigmoid(SWIGLU_ALPHA * x_glu) * (x_lin + 1.0)
            start = jnp.where(j == 0, bdn_ref[0], acc_ref[rows, :])
            acc_ref[rows, :] = start + jnp.dot(act.astype(BF16), wdn, preferred_element_type=F32)

        for n_blocks, plan in COMPUTE_PLANS.items():
            for first, count in plan:
                shared = [n for n, p in COMPUTE_PLANS.items() if (first, count) in p]
                if n_blocks == shared[0]:
                    cond = functools.reduce(jnp.logical_or, [nsub == n for n in shared])
                    pl.when(cond)(functools.partial(run_blocks, first, count))

    def out_copy(item, first, count):
        dst = pl.multiple_of((ib_ref[item] + first) * ROW_BLOCK, ROW_BLOCK)
        return pltpu.make_async_copy(yst_ref.at[rows_of(first, count)],
                                     ys_hbm.at[pl.ds(dst, count * ROW_BLOCK)], osem)

    @pl.when(j == n_ff_tiles - 1)
    def _():
        prev = jnp.maximum(i - 1, 0)
        for_regions(jnp.where(i > 0, ins_ref[prev], 0),
                    lambda first, count: out_copy(prev, first, count).wait())

        def leave(first, count):
            rows = rows_of(first, count)
            yst_ref[rows, :] = _pack_pairs(acc_ref[rows, :])
            out_copy(i, first, count).start()

        for_regions(nsub, leave)

        @pl.when(i == n_items - 1)
        def _():
            for_regions(nsub, lambda first, count: out_copy(i, first, count).wait())


def _experts(xs, item_e, item_b, item_n, n_active, w_up, b_up, w_down, b_down):
    n_rows, Dp = xs.shape
    D = 2 * Dp
    E, _, F2 = w_up.shape
    F = F2 // 2
    J = F // FF_TILE
    half = 128
    perm = np.zeros((2 * half, 2 * half), np.float32)
    perm[2 * np.arange(half), np.arange(half)] = 1.0
    perm[2 * np.arange(half) + 1, half + np.arange(half)] = 1.0

    def jj(i, j, ins):
        return jnp.where(ins[i] > 0, j, J - 1)

    grid_spec = pltpu.PrefetchScalarGridSpec(
        num_scalar_prefetch=3,
        grid=(n_active, J),
        in_specs=[pl.BlockSpec(memory_space=pl.ANY),
                  pl.BlockSpec((1, D, 2 * FF_TILE), lambda i, j, ie, ib, ins: (ie[i], 0, jj(i, j, ins))),
                  pl.BlockSpec((1, 1, 2 * FF_TILE), lambda i, j, ie, ib, ins: (ie[i], 0, jj(i, j, ins))),
                  pl.BlockSpec((1, FF_TILE, D), lambda i, j, ie, ib, ins: (ie[i], jj(i, j, ins), 0)),
                  pl.BlockSpec((1, 1, D), lambda i, j, ie, ib, ins: (ie[i], 0, 0)),
                  pl.BlockSpec((2 * half, 2 * half), lambda i, j, ie, ib, ins: (0, 0))],
        out_specs=pl.BlockSpec(memory_space=pl.ANY),
        scratch_shapes=[pltpu.VMEM((2, ITEM_ROWS, Dp), jnp.uint32),
                        pltpu.VMEM((ITEM_ROWS, D), F32),
                        pltpu.VMEM((ITEM_ROWS, Dp), jnp.uint32),
                        pltpu.VMEM((D, 2 * FF_TILE), BF16),
                        pltpu.VMEM((FF_TILE, D), BF16),
                        pltpu.SemaphoreType.DMA((2,)), pltpu.SemaphoreType.DMA(())],
    )
    return pl.pallas_call(
        functools.partial(_expert_kernel, n_ff_tiles=J),
        grid_spec=grid_spec,
        out_shape=jax.ShapeDtypeStruct((n_rows, Dp), jnp.uint32),
        compiler_params=_cparams(("arbitrary", "arbitrary")),
        name="expert_ffn",
    )(item_e, item_b, item_n,
      xs, w_up, b_up.reshape(E, 1, F2), w_down, b_down.reshape(E, 1, D), jnp.asarray(perm, BF16))


def _combine_kernel(*refs):
    yk_refs, (g4_ref, x1_ref, g_ref), o_ref = refs[:TOP_K], refs[TOP_K:TOP_K + 3], refs[-1]
    y = x1_ref[...]
    g4 = g4_ref[...]
    for k in range(TOP_K):
        y = y + g4[:, k:k + 1] * _unpack_pairs(yk_refs[k][...])
    ms = jnp.mean(y * y, axis=-1, keepdims=True)
    o_ref[...] = y * lax.rsqrt(ms + NORM_EPS) * g_ref[...]


def _combine(yk, g4, x1, g_final, part, n_parts, out_so_far, tc=512):
    T, D = x1.shape
    nt = T // n_parts // tc
    first = part * nt
    slot_specs = [pl.BlockSpec((tc, D // 2), functools.partial(lambda i, k: (k * nt + i, 0), k=k))
                  for k in range(TOP_K)]
    operands = [*([yk] * TOP_K), g4, x1, g_final.reshape(1, D)]
    in_specs = slot_specs + [pl.BlockSpec((tc, 128), lambda i: (first + i, 0)),
                             pl.BlockSpec((tc, D), lambda i: (first + i, 0)),
                             pl.BlockSpec((1, D), lambda i: (0, 0))]
    aliases = {}
    if out_so_far is not None:
        aliases = {len(operands): 0}
        operands.append(out_so_far)
        in_specs.append(pl.BlockSpec(memory_space=pl.ANY))
    return pl.pallas_call(
        _combine_kernel,
        grid=(nt,),
        in_specs=in_specs,
        out_specs=pl.BlockSpec((tc, D), lambda i: (first + i, 0)),
        out_shape=jax.ShapeDtypeStruct((T, D), F32),
        input_output_aliases=aliases,
        compiler_params=_cparams(("parallel",)),
        name="combine_norm",
    )(*operands)


def _work_items(nblk, start_blk, n_items):
    E = nblk.shape[0]
    per_e = (nblk + ITEM_BLOCKS - 1) // ITEM_BLOCKS
    ends = jnp.cumsum(per_e)
    total = ends[-1]
    idx = jnp.arange(n_items, dtype=I32)
    e = jnp.minimum(jnp.searchsorted(ends, idx, side="right"), E - 1).astype(I32)
    local = idx - (ends[e] - per_e[e])
    active = idx < total
    last_e = e[jnp.maximum(total - 1, 0)]
    item_e = jnp.where(active, e, last_e).astype(I32)
    item_b = jnp.where(active, start_blk[e] + local * ITEM_BLOCKS, 0).astype(I32)
    item_n = jnp.where(active, jnp.clip(nblk[e] - local * ITEM_BLOCKS, 0, ITEM_BLOCKS), 0).astype(I32)
    return item_e, item_b, item_n, jnp.maximum(total, 1).astype(I32)


def kernel(x, g_mix, w_in, b_forget, g_v_ln, b_v_ln, w_spatial, b_spatial, w_branch_attn, w_branch_gmlp, w_out, g_ffn, w_router, b_router, w_expert_up, b_expert_up, w_expert_down, b_expert_down, g_final):
    B, S, D = x.shape
    T = B * S
    n_heads = b_forget.shape[0]
    attn_w = n_heads * HEAD_DIM
    gmlp_w = g_v_ln.shape[0]
    E = w_router.shape[1]
    off_f = 3 * attn_w
    off_z = off_f + n_heads
    off_g = off_z + 2 * gmlp_w

    x2 = x.reshape(T, D)
    w_in_t = w_in.T
    wzg = _rows_bf16(w_in_t, off_z, w_in.shape[1] - off_z)
    wa = w_branch_attn.astype(BF16)
    wb = w_branch_gmlp.astype(BF16)
    wo = w_out.astype(BF16)

    qkv, h = _norm_project(x2, g_mix, w_in_t, 3 * attn_w, attn_w, LOG2E / math.sqrt(HEAD_DIM))
    c_row = _forget_cumsum(h, w_in_t, off_f, b_forget, B, S)
    attn = _attention(qkv, c_row, B, S, n_heads)
    sg = _gmlp(h, wzg, g_v_ln, b_v_ln, w_spatial, b_spatial)
    merged = _merge(attn, sg, h, wa, wb, wzg, off_g - off_z)
    x1, h2p, sel, gate = _out_router(merged, x2, wo, g_ffn, w_router, b_router)

    dest4, g4, nblk, start_blk = _routing(sel, gate)
    n_rows = T * TOP_K + E * ROW_BLOCK
    dest_slots = dest4[:, :TOP_K].T.reshape(TOP_K * T)
    max_blocks = n_rows // ROW_BLOCK
    n_items = E + -(-(max_blocks - E) // ITEM_BLOCKS) + 1
    item_e, item_b, item_n, n_active = _work_items(nblk[0], start_blk[0], n_items)
    xs = _sc_scatter_rows(h2p, dest_slots, n_rows, TOP_K)
    ys = _experts(xs, item_e, item_b, item_n, n_active,
                  w_expert_up, b_expert_up, w_expert_down, b_expert_down)
    out = None
    t_part = T // COMBINE_PARTS
    for p in range(COMBINE_PARTS):
        idx = dest4[p * t_part:(p + 1) * t_part, :TOP_K].T.reshape(TOP_K * t_part)
        yk = _sc_gather_rows(ys, idx)
        out = _combine(yk, g4, x1, g_final, p, COMBINE_PARTS, out)
    return out.reshape(B, S, D)
```

```python
import functools
import math

import jax
import jax.numpy as jnp
import numpy as np
from jax import lax
from jax.experimental import pallas as pl
from jax.experimental.pallas import tpu as pltpu
from jax.experimental.pallas import tpu_sc as plsc

F32 = jnp.float32
BF16 = jnp.bfloat16
I32 = jnp.int32

NORM_EPS = 1e-5
LANES = 128
SUBLANES = 8
NT_DIMS = (((1,), (1,)), ((), ()))
HEAD_DIM = 128
CHUNK = 128
GROUP_DIM = 128
TOP_K = 4
SWIGLU_ALPHA = 1.702
SWIGLU_LIMIT = 7.0
LOG2E = math.log2(math.e)

VMEM_LIMIT_BYTES = 56 * 1024 * 1024

ROW_BLOCK = 128
ITEM_BLOCKS = 10
ITEM_ROWS = ITEM_BLOCKS * ROW_BLOCK
REGION_BLOCKS = (8, 4, 2, 1)
assert sum(REGION_BLOCKS) >= ITEM_BLOCKS
COMPUTE_PLANS = {**{n: ((0, n),) for n in range(4, ITEM_BLOCKS + 1)},
                 3: ((0, 2), (2, 1)), 2: ((0, 2),), 1: ((0, 1),)}
assert set(COMPUTE_PLANS) == set(range(1, ITEM_BLOCKS + 1))
FF_TILE = 256
COMBINE_PARTS = 1


def _cparams(sem, **kw):
    return pltpu.CompilerParams(dimension_semantics=sem, vmem_limit_bytes=VMEM_LIMIT_BYTES, **kw)


def _pack_pairs(x):
    c = x.shape[1] // 2
    hi = lax.bitcast_convert_type(x[:, :c].astype(BF16).astype(F32), jnp.uint32)
    lo = lax.bitcast_convert_type(x[:, c:].astype(BF16).astype(F32), jnp.uint32)
    return hi | (lo >> 16)


def _unpack_pairs(w):
    hi = lax.bitcast_convert_type(w & jnp.uint32(0xFFFF0000), F32)
    lo = lax.bitcast_convert_type(w << 16, F32)
    return jnp.concatenate([hi, lo], axis=1)


def _norm_proj_kernel(x_ref, g_ref, w_ref, o_ref, h_ref, *, n_scaled, scale):
    j = pl.program_id(1)

    @pl.when(j == 0)
    def _():
        x = x_ref[...]
        ms = jnp.mean(x * x, axis=-1, keepdims=True)
        h_ref[...] = (x * lax.rsqrt(ms + NORM_EPS) * g_ref[...]).astype(h_ref.dtype)

    acc = lax.dot_general(h_ref[...], w_ref[...].astype(BF16), NT_DIMS, preferred_element_type=F32)
    o_ref[...] = (acc * jnp.where(j < n_scaled, scale, 1.0)).astype(o_ref.dtype)


def _norm_project(x, g, wt, n_cols, n_scaled_cols, scale, tm=1024, tn=1024):
    T, D = x.shape
    return pl.pallas_call(
        functools.partial(_norm_proj_kernel, n_scaled=n_scaled_cols // tn, scale=scale),
        grid=(T // tm, n_cols // tn),
        in_specs=[pl.BlockSpec((tm, D), lambda i, j: (i, 0)),
                  pl.BlockSpec((1, D), lambda i, j: (0, 0)),
                  pl.BlockSpec((tn, D), lambda i, j: (j, 0))],
        out_specs=[pl.BlockSpec((tm, tn), lambda i, j: (i, j)),
                   pl.BlockSpec((tm, D), lambda i, j: (i, 0))],
        out_shape=[jax.ShapeDtypeStruct((T, n_cols), BF16), jax.ShapeDtypeStruct((T, D), BF16)],
        compiler_params=_cparams(("arbitrary", "arbitrary")),
        name="norm_qkv_proj",
    )(x, g.reshape(1, D), wt)


def _cast_kernel(w_ref, o_ref):
    o_ref[...] = w_ref[...].astype(o_ref.dtype)


def _rows_bf16(wt, start, n_rows, tr=1024):
    D = wt.shape[1]
    assert start % SUBLANES == 0 and n_rows % tr == 0
    return pl.pallas_call(
        _cast_kernel,
        grid=(n_rows // tr,),
        in_specs=[pl.BlockSpec((pl.Element(tr), pl.Element(D)),
                               lambda i: (pl.multiple_of(start + i * tr, SUBLANES), 0))],
        out_specs=pl.BlockSpec((tr, D), lambda i: (i, 0)),
        out_shape=jax.ShapeDtypeStruct((n_rows, D), BF16),
        compiler_params=_cparams(("parallel",)),
        name="rows_bf16",
    )(wt)


def _forget_kernel(h_ref, wft_ref, bf_ref, c_ref):
    ft = lax.dot_general(wft_ref[...].astype(BF16), h_ref[...], NT_DIMS,
                         preferred_element_type=F32)
    c = jax.nn.log_sigmoid(ft + bf_ref[...])
    S = c.shape[1]
    lane = lax.broadcasted_iota(I32, c.shape, 1)
    shift = 1
    while shift < S:
        c = c + jnp.where(lane >= shift, pltpu.roll(c, shift, axis=1), 0.0)
        shift *= 2
    c_ref[0] = c * LOG2E


def _forget_cumsum(h, wt, f_off, b_forget, B, S):
    T, D = h.shape
    H = b_forget.shape[0]
    assert f_off % H == 0 and H % SUBLANES == 0
    return pl.pallas_call(
        _forget_kernel,
        grid=(B,),
        in_specs=[pl.BlockSpec((S, D), lambda b: (b, 0)),
                  pl.BlockSpec((H, D), lambda b: (f_off // H, 0)),
                  pl.BlockSpec((H, 1), lambda b: (0, 0))],
        out_specs=pl.BlockSpec((1, H, S), lambda b: (b, 0, 0)),
        out_shape=jax.ShapeDtypeStruct((B, H, S), F32),
        compiler_params=_cparams(("parallel",)),
        name="forget_cumsum",
    )(h, wt, b_forget.reshape(H, 1))


def _attn_kernel(q_ref, k_ref, v_ref, crow_ref, o_ref, vaug_ref, m_ref, acc_ref, *, n_heads, tq):
    i = pl.program_id(1)

    @pl.when(i == 0)
    def _():
        ones = jnp.ones((v_ref.shape[0], HEAD_DIM), BF16)
        for h in range(n_heads):
            vaug_ref[h, :, :HEAD_DIM] = v_ref[:, h * HEAD_DIM:(h + 1) * HEAD_DIM]
            vaug_ref[h, :, HEAD_DIM:] = ones

    m_ref[...] = jnp.full(m_ref.shape, -jnp.inf, F32)
    acc_ref[...] = jnp.zeros(acc_ref.shape, F32)
    row = lax.broadcasted_iota(I32, (tq, tq), 0)
    col = lax.broadcasted_iota(I32, (tq, tq), 1)
    causal = col <= row

    def step(j, masked):
        keys = pl.ds(pl.multiple_of(j * tq, tq), tq)
        for h in range(n_heads):
            hs = slice(h * HEAD_DIM, (h + 1) * HEAD_DIM)
            s = lax.dot_general(q_ref[:, hs], k_ref[keys, hs], (((1,), (1,)), ((), ())),
                                preferred_element_type=F32) - crow_ref[0, h, j]
            if masked:
                s = jnp.where(causal, s, -jnp.inf)
            m_old = m_ref[h]
            m_new = jnp.maximum(m_old, jnp.max(s, axis=-1, keepdims=True))
            alpha = jnp.exp2(m_old - m_new)
            p = jnp.exp2(s - jnp.concatenate([m_new] * (tq // HEAD_DIM), axis=1))
            m_ref[h] = m_new
            pv = jnp.dot(p.astype(BF16), vaug_ref[h, keys, :], preferred_element_type=F32)
            acc_ref[h] = jnp.concatenate([alpha, alpha], axis=1) * acc_ref[h] + pv

    def body(j, _):
        step(j, False)
        return 0

    lax.fori_loop(0, i, body, 0)
    step(i, True)
    for h in range(n_heads):
        acc = acc_ref[h]
        o_ref[:, h * HEAD_DIM:(h + 1) * HEAD_DIM] = (acc[:, :HEAD_DIM] / acc[:, HEAD_DIM:]).astype(o_ref.dtype)


def _attention(qkv, c_row, B, S, n_heads, tq=512):
    T = qkv.shape[0]
    W = n_heads * HEAD_DIM
    nq = S // tq
    c_row5 = c_row.reshape(B, n_heads, nq, 1, tq)
    return pl.pallas_call(
        functools.partial(_attn_kernel, n_heads=n_heads, tq=tq),
        grid=(B, nq),
        in_specs=[pl.BlockSpec((tq, W), lambda b, i: (b * nq + i, 0)),
                  pl.BlockSpec((S, W), lambda b, i: (b, 1)),
                  pl.BlockSpec((S, W), lambda b, i: (b, 2)),
                  pl.BlockSpec((1, n_heads, nq, 1, tq), lambda b, i: (b, 0, 0, 0, 0))],
        out_specs=pl.BlockSpec((tq, W), lambda b, i: (b * nq + i, 0)),
        out_shape=jax.ShapeDtypeStruct((T, W), BF16),
        scratch_shapes=[pltpu.VMEM((n_heads, S, 2 * HEAD_DIM), BF16),
                        pltpu.VMEM((n_heads, tq, HEAD_DIM), F32),
                        pltpu.VMEM((n_heads, tq, 2 * HEAD_DIM), F32)],
        compiler_params=_cparams(("arbitrary", "arbitrary")),
        name="fox_attention",
    )(qkv, qkv, qkv, c_row5)


def _gmlp_kernel(h_ref, wz_ref, g_ref, b_ref, ws_ref, bst_ref, o_ref, *, n_groups):
    z = lax.dot_general(h_ref[...], wz_ref[...], NT_DIMS, preferred_element_type=F32)
    z = 0.5 * z * (1.0 + lax.erf(z * (1.0 / math.sqrt(2.0))))
    W = z.shape[1] // 2
    u = z[:, :W]
    v = z[:, W:]
    mu = jnp.mean(v, axis=-1, keepdims=True)
    var = jnp.mean(jnp.square(v - mu), axis=-1, keepdims=True)
    vn = (v - mu) * lax.rsqrt(var + NORM_EPS) * g_ref[...] + b_ref[...]
    row = lax.broadcasted_iota(I32, (CHUNK, CHUNK), 0)
    col = lax.broadcasted_iota(I32, (CHUNK, CHUNK), 1)
    tril = col <= row
    tg = z.shape[0]
    for g in range(n_groups):
        gs = slice(g * GROUP_DIM, (g + 1) * GROUP_DIM)
        wg = jnp.where(tril, ws_ref[g], 0.0).astype(BF16)
        bias = bst_ref[:, g:g + 1]
        for c in range(tg // CHUNK):
            cs = slice(c * CHUNK, (c + 1) * CHUNK)
            mixed = jnp.dot(wg, vn[cs, gs].astype(BF16), preferred_element_type=F32) + bias
            o_ref[cs, gs] = (u[cs, gs] * mixed).astype(o_ref.dtype)


def _gmlp(h, wz, g_v_ln, b_v_ln, w_spatial, b_spatial, tg=1024):
    T, D = h.shape
    W = g_v_ln.shape[0]
    W2 = 2 * W
    G = w_spatial.shape[0]
    return pl.pallas_call(
        functools.partial(_gmlp_kernel, n_groups=G),
        grid=(T // tg,),
        in_specs=[pl.BlockSpec((tg, D), lambda i: (i, 0)),
                  pl.BlockSpec((W2, D), lambda i: (0, 0)),
                  pl.BlockSpec((1, W), lambda i: (0, 0)),
                  pl.BlockSpec((1, W), lambda i: (0, 0)),
                  pl.BlockSpec((G, CHUNK, CHUNK), lambda i: (0, 0, 0)),
                  pl.BlockSpec((CHUNK, G), lambda i: (0, 0))],
        out_specs=pl.BlockSpec((tg, W), lambda i: (i, 0)),
        out_shape=jax.ShapeDtypeStruct((T, W), BF16),
        compiler_params=_cparams(("parallel",)),
        name="gmlp",
    )(h, wz, g_v_ln.reshape(1, W), b_v_ln.reshape(1, W), w_spatial, b_spatial.T)


def _merge_kernel(attn_ref, sg_ref, h_ref, wa_ref, wb_ref, wga_ref, wgb_ref, o_ref):
    h = h_ref[...]
    a = jnp.dot(attn_ref[...], wa_ref[...], preferred_element_type=F32)
    ga = lax.dot_general(h, wga_ref[...], NT_DIMS, preferred_element_type=F32)
    m = jax.nn.sigmoid(ga) * a
    b = jnp.dot(sg_ref[...], wb_ref[...], preferred_element_type=F32)
    gb = lax.dot_general(h, wgb_ref[...], NT_DIMS, preferred_element_type=F32)
    o_ref[...] = (m + jax.nn.sigmoid(gb) * b).astype(o_ref.dtype)


def _merge(attn, sg, h, wa, wb, wg, g_off, tm=1024, tn=512):
    T, D = h.shape
    Wa = attn.shape[1]
    Wb = sg.shape[1]
    nt = D // tn
    g0 = g_off // tn
    return pl.pallas_call(
        _merge_kernel,
        grid=(nt, T // tm),
        in_specs=[pl.BlockSpec((tm, Wa), lambda j, i: (i, 0)),
                  pl.BlockSpec((tm, Wb), lambda j, i: (i, 0)),
                  pl.BlockSpec((tm, D), lambda j, i: (i, 0)),
                  pl.BlockSpec((Wa, tn), lambda j, i: (0, j)),
                  pl.BlockSpec((Wb, tn), lambda j, i: (0, j)),
                  pl.BlockSpec((tn, D), lambda j, i: (g0 + j, 0)),
                  pl.BlockSpec((tn, D), lambda j, i: (g0 + nt + j, 0))],
        out_specs=pl.BlockSpec((tm, tn), lambda j, i: (i, j)),
        out_shape=jax.ShapeDtypeStruct((T, D), BF16),
        compiler_params=_cparams(("arbitrary", "arbitrary")),
        name="gated_merge",
    )(attn, sg, h, wa, wb, wg, wg)


def _out_router_kernel(m_ref, x_ref, wo_ref, g_ref, wr_ref, br_ref,
                       x1_ref, h2_ref, sel_ref, gate_ref, *, n_parts):
    E = br_ref.shape[1]
    part = m_ref.shape[0] // n_parts
    for p in range(n_parts):
        rows = pl.ds(p * part, part)
        x1 = x_ref[rows, :] + jnp.dot(m_ref[rows, :], wo_ref[...], preferred_element_type=F32)
        x1_ref[rows, :] = x1
        ms = jnp.mean(x1 * x1, axis=-1, keepdims=True)
        h2 = x1 * lax.rsqrt(ms + NORM_EPS) * g_ref[...]
        h2_ref[rows, :] = _pack_pairs(h2)
        h2_hi = h2.astype(BF16)
        h2_lo = (h2 - h2_hi.astype(F32)).astype(BF16)
        pa = jnp.dot(h2_hi, wr_ref[...], preferred_element_type=F32)
        pb = jnp.dot(h2_lo, wr_ref[:, :E], preferred_element_type=F32)
        logits = pa[:, :E] + (pa[:, E:] + pb) + br_ref[...]
        lt = jnp.concatenate([logits, jnp.full((part, LANES - E), -jnp.inf, F32)], axis=1).T[:E]
        expert = lax.broadcasted_iota(I32, lt.shape, 0)
        beaten = jnp.zeros(lt.shape, F32)
        for e in range(E):
            le = lt[e:e + 1, :]
            beaten = beaten + jnp.where((le > lt) | ((le == lt) & (expert > e)), 1.0, 0.0)
        chosen = beaten < TOP_K
        top = jnp.max(jnp.where(chosen, lt, -jnp.inf), axis=0, keepdims=True)
        ex = jnp.where(chosen, jnp.exp(lt - top), 0.0)
        gate_t = ex / jnp.sum(ex, axis=0, keepdims=True)
        zeros = jnp.zeros((LANES - E, part), F32)
        sel_ref[rows, :] = jnp.concatenate([jnp.where(chosen, 1.0, 0.0), zeros], axis=0).T[:, :E]
        gate_ref[rows, :] = jnp.concatenate([gate_t, zeros], axis=0).T[:, :E]


def _out_router(merged, x, wo, g_ffn, w_router, b_router, to=512, n_parts=1):
    T, D = x.shape
    E = w_router.shape[1]
    w_hi = w_router.astype(BF16)
    w_lo = (w_router - w_hi.astype(F32)).astype(BF16)
    w_router = jnp.concatenate([w_hi, w_lo], axis=1)
    row = lambda i: (i, 0)
    fixed = lambda i: (0, 0)
    return pl.pallas_call(
        functools.partial(_out_router_kernel, n_parts=n_parts),
        grid=(T // to,),
        in_specs=[pl.BlockSpec((to, D), row), pl.BlockSpec((to, D), row),
                  pl.BlockSpec((D, D), fixed), pl.BlockSpec((1, D), fixed),
                  pl.BlockSpec((D, 2 * E), fixed), pl.BlockSpec((1, E), fixed)],
        out_specs=[pl.BlockSpec((to, D), row), pl.BlockSpec((to, D // 2), row),
                   pl.BlockSpec((to, E), row), pl.BlockSpec((to, E), row)],
        out_shape=[jax.ShapeDtypeStruct((T, D), F32), jax.ShapeDtypeStruct((T, D // 2), jnp.uint32),
                   jax.ShapeDtypeStruct((T, E), F32), jax.ShapeDtypeStruct((T, E), F32)],
        compiler_params=_cparams(("parallel",)),
        name="out_router",
    )(merged, x, wo, g_ffn.reshape(1, D), w_router, b_router.reshape(1, E))


def _routing_kernel(sel_ref, gate_ref, dest_ref, g4_ref, nblk_ref, start_ref, rank_ref, *, tile):
    T, E = sel_ref.shape
    nt = T // tile
    r = lax.broadcasted_iota(I32, (tile, tile), 0)
    c = lax.broadcasted_iota(I32, (tile, tile), 1)
    strict_lower = (c < r).astype(BF16)
    er = lax.broadcasted_iota(I32, (E, E), 0)
    ec = lax.broadcasted_iota(I32, (E, E), 1)
    strict_upper = (er < ec).astype(BF16)

    def pass1(t, carry):
        rows = pl.ds(pl.multiple_of(t * tile, tile), tile)
        a = sel_ref[rows, :]
        rank_ref[rows, :] = jnp.dot(strict_lower, a.astype(BF16), preferred_element_type=F32) + carry
        return carry + jnp.sum(a, axis=0, keepdims=True)

    counts = lax.fori_loop(0, nt, pass1, jnp.zeros((1, E), F32))
    nblk = jnp.floor((counts + (ROW_BLOCK - 1)) * (1.0 / ROW_BLOCK))
    start_blk = jnp.dot(nblk.astype(BF16), strict_upper, preferred_element_type=F32)
    nblk_ref[...] = nblk.astype(I32)
    start_ref[...] = start_blk.astype(I32)
    start_row = start_blk * float(ROW_BLOCK)
    lane = lax.broadcasted_iota(I32, (tile, 128), 1)

    def pass2(t, _):
        rows = pl.ds(pl.multiple_of(t * tile, tile), tile)
        a = sel_ref[rows, :]
        g = gate_ref[rows, :]
        dest_e = rank_ref[rows, :] + start_row
        slot = jnp.dot(a.astype(BF16), strict_upper, preferred_element_type=F32)
        d4 = jnp.zeros((tile, 128), F32)
        g4 = jnp.zeros((tile, 128), F32)
        for s in range(TOP_K):
            pick = (a > 0.5) & (slot == float(s))
            d4 = jnp.where(lane == s, jnp.sum(jnp.where(pick, dest_e, 0.0), axis=-1, keepdims=True), d4)
            g4 = jnp.where(lane == s, jnp.sum(jnp.where(pick, g, 0.0), axis=-1, keepdims=True), g4)
        dest_ref[rows, :] = d4.astype(I32)
        g4_ref[rows, :] = g4
        return 0

    lax.fori_loop(0, nt, pass2, 0)


def _routing(sel, gate, tile=512):
    T, E = sel.shape
    return pl.pallas_call(
        functools.partial(_routing_kernel, tile=tile),
        out_shape=[jax.ShapeDtypeStruct((T, 128), I32), jax.ShapeDtypeStruct((T, 128), F32),
                   jax.ShapeDtypeStruct((1, E), I32), jax.ShapeDtypeStruct((1, E), I32)],
        scratch_shapes=[pltpu.VMEM((T, E), F32)],
        compiler_params=pltpu.CompilerParams(vmem_limit_bytes=VMEM_LIMIT_BYTES),
        name="routing_ranks",
    )(sel, gate)


SC_CORES = 2
SC_SUBCORES = 16
SC_CHUNK = 64


def _sc_gather_rows(table, idx):
    n = idx.shape[0]
    W = table.shape[1]
    n_workers = SC_CORES * SC_SUBCORES
    per_worker = n // n_workers
    assert per_worker * n_workers == n and per_worker % SC_CHUNK == 0
    mesh = plsc.VectorSubcoreMesh(core_axis_name="c", subcore_axis_name="s",
                                  num_cores=SC_CORES, num_subcores=SC_SUBCORES)

    @functools.partial(
        pl.kernel, mesh=mesh,
        out_type=jax.ShapeDtypeStruct((n, W), table.dtype),
        scratch_types=[pltpu.VMEM((SC_CHUNK,), I32), pltpu.VMEM((SC_CHUNK, W), table.dtype),
                       pltpu.SemaphoreType.DMA],
        name="sc_gather_rows",
    )
    def gather(table_hbm, idx_hbm, out_hbm, idx_v, rows_v, sem):
        wid = lax.axis_index("s") * SC_CORES + lax.axis_index("c")
        base = wid * per_worker

        @pl.loop(0, per_worker // SC_CHUNK)
        def _(c):
            off = pl.multiple_of(base + c * SC_CHUNK, 8)
            pltpu.sync_copy(idx_hbm.at[pl.ds(off, SC_CHUNK)], idx_v)
            pltpu.async_copy(table_hbm.at[idx_v], rows_v, sem).wait()
            pltpu.sync_copy(rows_v, out_hbm.at[pl.ds(off, SC_CHUNK)])

    return gather(table, idx)


def _sc_scatter_rows(rows, idx, n_out, n_slots):
    T, W = rows.shape
    n_workers = SC_CORES * SC_SUBCORES
    per_worker = T // n_workers
    assert per_worker * n_workers == T and per_worker % SC_CHUNK == 0
    mesh = plsc.VectorSubcoreMesh(core_axis_name="c", subcore_axis_name="s",
                                  num_cores=SC_CORES, num_subcores=SC_SUBCORES)

    @functools.partial(
        pl.kernel, mesh=mesh,
        out_type=jax.ShapeDtypeStruct((n_out, W), rows.dtype),
        scratch_types=[pltpu.VMEM((SC_CHUNK,), I32), pltpu.VMEM((SC_CHUNK, W), rows.dtype)],
        name="sc_scatter_rows",
    )
    def scatter(rows_hbm, idx_hbm, out_hbm, idx_v, rows_v):
        wid = lax.axis_index("s") * SC_CORES + lax.axis_index("c")
        base = wid * per_worker

        @pl.loop(0, per_worker // SC_CHUNK)
        def _(c):
            off = pl.multiple_of(base + c * SC_CHUNK, 8)
            pltpu.sync_copy(rows_hbm.at[pl.ds(off, SC_CHUNK)], rows_v)
            for k in range(n_slots):
                pltpu.sync_copy(idx_hbm.at[pl.ds(pl.multiple_of(k * T + off, 8), SC_CHUNK)], idx_v)
                pltpu.sync_copy(rows_v, out_hbm.at[idx_v])

    return scatter(rows, idx)


def _expert_kernel(ie_ref, ib_ref, ins_ref,
                   xs_hbm, wup_ref, bup_ref, wdn_ref, bdn_ref, perm_ref, ys_hbm,
                   xg_ref, acc_ref, yst_ref, wupb_ref, wdnb_ref, gsem, osem, *, n_ff_tiles):
    i = pl.program_id(0)
    j = pl.program_id(1)
    n_items = pl.num_programs(0)
    nsub = ins_ref[i]
    slot = i % 2

    def for_regions(n_blocks, fn):
        first = jnp.int32(0)
        for count in REGION_BLOCKS:
            present = (n_blocks & count) != 0
            pl.when(present)(functools.partial(fn, first, count))
            first = first + jnp.where(present, count, 0)

    def rows_of(first, count):
        return pl.ds(pl.multiple_of(first * ROW_BLOCK, ROW_BLOCK), count * ROW_BLOCK)

    def in_copy(item, dst_slot, first, count):
        src = pl.multiple_of((ib_ref[item] + first) * ROW_BLOCK, ROW_BLOCK)
        return pltpu.make_async_copy(xs_hbm.at[pl.ds(src, count * ROW_BLOCK)],
                                     xg_ref.at[dst_slot, rows_of(first, count)], gsem.at[dst_slot])

    def fetch_item(item, n_blocks, dst_slot):
        for_regions(n_blocks, lambda first, count: in_copy(item, dst_slot, first, count).start())

    @pl.when(j == 0)
    def _():
        @pl.when(i == 0)
        def _():
            fetch_item(0, nsub, 0)
            acc_ref[...] = jnp.zeros(acc_ref.shape, F32)

        nxt = jnp.minimum(i + 1, n_items - 1)
        fetch_item(nxt, jnp.where(i + 1 < n_items, ins_ref[nxt], 0), 1 - slot)
        for_regions(nsub, lambda first, count: in_copy(i, slot, first, count).wait())

    @pl.when(nsub > 0)
    def _():
        bup = bup_ref[0]
        perm = perm_ref[...]
        half = perm.shape[0] // 2

        def run_blocks(first, count):
            if first == 0:
                wup = wup_ref[0].astype(BF16)
                wdn = wdn_ref[0].astype(BF16)
                wupb_ref[...] = wup
                wdnb_ref[...] = wdn
            else:
                wup = wupb_ref[...]
                wdn = wdnb_ref[...]
            rows = pl.ds(first * ROW_BLOCK, count * ROW_BLOCK)
            xb = _unpack_pairs(xg_ref[slot, rows, :]).astype(BF16)
            gu = (jnp.dot(xb, wup, preferred_element_type=F32) + bup).astype(BF16)
            glu_parts, lin_parts = [], []
            for p in range(gu.shape[1] // perm.shape[0]):
                gp = jnp.dot(gu[:, p * perm.shape[0]:(p + 1) * perm.shape[0]], perm,
                             preferred_element_type=F32)
                glu_parts.append(gp[:, :half])
                lin_parts.append(gp[:, half:])
            x_glu = jnp.minimum(jnp.concatenate(glu_parts, axis=1), SWIGLU_LIMIT)
            x_lin = jnp.clip(jnp.concatenate(lin_parts, axis=1), -SWIGLU_LIMIT, SWIGLU_LIMIT)
            act = x_glu * jax.nn.sigmoid(SWIGLU_ALPHA * x_glu) * (x_lin + 1.0)
            start = jnp.where(j == 0, bdn_ref[0], acc_ref[rows, :])
            acc_ref[rows, :] = start + jnp.dot(act.astype(BF16), wdn, preferred_element_type=F32)

        for n_blocks, plan in COMPUTE_PLANS.items():
            for first, count in plan:
                shared = [n for n, p in COMPUTE_PLANS.items() if (first, count) in p]
                if n_blocks == shared[0]:
                    cond = functools.reduce(jnp.logical_or, [nsub == n for n in shared])
                    pl.when(cond)(functools.partial(run_blocks, first, count))

    def out_copy(item, first, count):
        dst = pl.multiple_of((ib_ref[item] + first) * ROW_BLOCK, ROW_BLOCK)
        return pltpu.make_async_copy(yst_ref.at[rows_of(first, count)],
                                     ys_hbm.at[pl.ds(dst, count * ROW_BLOCK)], osem)

    @pl.when(j == n_ff_tiles - 1)
    def _():
        prev = jnp.maximum(i - 1, 0)
        for_regions(jnp.where(i > 0, ins_ref[prev], 0),
                    lambda first, count: out_copy(prev, first, count).wait())

        def leave(first, count):
            rows = rows_of(first, count)
            yst_ref[rows, :] = _pack_pairs(acc_ref[rows, :])
            out_copy(i, first, count).start()

        for_regions(nsub, leave)

        @pl.when(i == n_items - 1)
        def _():
            for_regions(nsub, lambda first, count: out_copy(i, first, count).wait())


def _experts(xs, item_e, item_b, item_n, n_active, w_up, b_up, w_down, b_down):
    n_rows, Dp = xs.shape
    D = 2 * Dp
    E, _, F2 = w_up.shape
    F = F2 // 2
    J = F // FF_TILE
    half = 128
    perm = np.zeros((2 * half, 2 * half), np.float32)
    perm[2 * np.arange(half), np.arange(half)] = 1.0
    perm[2 * np.arange(half) + 1, half + np.arange(half)] = 1.0

    def jj(i, j, ins):
        return jnp.where(ins[i] > 0, j, J - 1)

    grid_spec = pltpu.PrefetchScalarGridSpec(
        num_scalar_prefetch=3,
        grid=(n_active, J),
        in_specs=[pl.BlockSpec(memory_space=pl.ANY),
                  pl.BlockSpec((1, D, 2 * FF_TILE), lambda i, j, ie, ib, ins: (ie[i], 0, jj(i, j, ins))),
                  pl.BlockSpec((1, 1, 2 * FF_TILE), lambda i, j, ie, ib, ins: (ie[i], 0, jj(i, j, ins))),
                  pl.BlockSpec((1, FF_TILE, D), lambda i, j, ie, ib, ins: (ie[i], jj(i, j, ins), 0)),
                  pl.BlockSpec((1, 1, D), lambda i, j, ie, ib, ins: (ie[i], 0, 0)),
                  pl.BlockSpec((2 * half, 2 * half), lambda i, j, ie, ib, ins: (0, 0))],
        out_specs=pl.BlockSpec(memory_space=pl.ANY),
        scratch_shapes=[pltpu.VMEM((2, ITEM_ROWS, Dp), jnp.uint32),
                        pltpu.VMEM((ITEM_ROWS, D), F32),
                        pltpu.VMEM((ITEM_ROWS, Dp), jnp.uint32),
                        pltpu.VMEM((D, 2 * FF_TILE), BF16),
                        pltpu.VMEM((FF_TILE, D), BF16),
                        pltpu.SemaphoreType.DMA((2,)), pltpu.SemaphoreType.DMA(())],
    )
    return pl.pallas_call(
        functools.partial(_expert_kernel, n_ff_tiles=J),
        grid_spec=grid_spec,
        out_shape=jax.ShapeDtypeStruct((n_rows, Dp), jnp.uint32),
        compiler_params=_cparams(("arbitrary", "arbitrary")),
        name="expert_ffn",
    )(item_e, item_b, item_n,
      xs, w_up, b_up.reshape(E, 1, F2), w_down, b_down.reshape(E, 1, D), jnp.asarray(perm, BF16))


def _combine_kernel(*refs):
    yk_refs, (g4_ref, x1_ref, g_ref), o_ref = refs[:TOP_K], refs[TOP_K:TOP_K + 3], refs[-1]
    y = x1_ref[...]
    g4 = g4_ref[...]
    for k in range(TOP_K):
        y = y + g4[:, k:k + 1] * _unpack_pairs(yk_refs[k][...])
    ms = jnp.mean(y * y, axis=-1, keepdims=True)
    o_ref[...] = y * lax.rsqrt(ms + NORM_EPS) * g_ref[...]


def _combine(yk, g4, x1, g_final, part, n_parts, out_so_far, tc=512):
    T, D = x1.shape
    nt = T // n_parts // tc
    first = part * nt
    slot_specs = [pl.BlockSpec((tc, D // 2), functools.partial(lambda i, k: (k * nt + i, 0), k=k))
                  for k in range(TOP_K)]
    operands = [*([yk] * TOP_K), g4, x1, g_final.reshape(1, D)]
    in_specs = slot_specs + [pl.BlockSpec((tc, 128), lambda i: (first + i, 0)),
                             pl.BlockSpec((tc, D), lambda i: (first + i, 0)),
                             pl.BlockSpec((1, D), lambda i: (0, 0))]
    aliases = {}
    if out_so_far is not None:
        aliases = {len(operands): 0}
        operands.append(out_so_far)
        in_specs.append(pl.BlockSpec(memory_space=pl.ANY))
    return pl.pallas_call(
        _combine_kernel,
        grid=(nt,),
        in_specs=in_specs,
        out_specs=pl.BlockSpec((tc, D), lambda i: (first + i, 0)),
        out_shape=jax.ShapeDtypeStruct((T, D), F32),
        input_output_aliases=aliases,
        compiler_params=_cparams(("parallel",)),
        name="combine_norm",
    )(*operands)


def _work_items(nblk, start_blk, n_items):
    E = nblk.shape[0]
    per_e = (nblk + ITEM_BLOCKS - 1) // ITEM_BLOCKS
    ends = jnp.cumsum(per_e)
    total = ends[-1]
    idx = jnp.arange(n_items, dtype=I32)
    e = jnp.minimum(jnp.searchsorted(ends, idx, side="right"), E - 1).astype(I32)
    local = idx - (ends[e] - per_e[e])
    active = idx < total
    last_e = e[jnp.maximum(total - 1, 0)]
    item_e = jnp.where(active, e, last_e).astype(I32)
    item_b = jnp.where(active, start_blk[e] + local * ITEM_BLOCKS, 0).astype(I32)
    item_n = jnp.where(active, jnp.clip(nblk[e] - local * ITEM_BLOCKS, 0, ITEM_BLOCKS), 0).astype(I32)
    return item_e, item_b, item_n, jnp.maximum(total, 1).astype(I32)


def kernel(x, g_mix, w_in, b_forget, g_v_ln, b_v_ln, w_spatial, b_spatial, w_branch_attn, w_branch_gmlp, w_out, g_ffn, w_router, b_router, w_expert_up, b_expert_up, w_expert_down, b_expert_down, g_final):
    B, S, D = x.shape
    T = B * S
    n_heads = b_forget.shape[0]
    attn_w = n_heads * HEAD_DIM
    gmlp_w = g_v_ln.shape[0]
    E = w_router.shape[1]
    off_f = 3 * attn_w
    off_z = off_f + n_heads
    off_g = off_z + 2 * gmlp_w

    x2 = x.reshape(T, D)
    w_in_t = w_in.T
    wzg = _rows_bf16(w_in_t, off_z, w_in.shape[1] - off_z)
    wa = w_branch_attn.astype(BF16)
    wb = w_branch_gmlp.astype(BF16)
    wo = w_out.astype(BF16)

    qkv, h = _norm_project(x2, g_mix, w_in_t, 3 * attn_w, attn_w, LOG2E / math.sqrt(HEAD_DIM))
    c_row = _forget_cumsum(h, w_in_t, off_f, b_forget, B, S)
    attn = _attention(qkv, c_row, B, S, n_heads)
    sg = _gmlp(h, wzg, g_v_ln, b_v_ln, w_spatial, b_spatial)
    merged = _merge(attn, sg, h, wa, wb, wzg, off_g - off_z)
    x1, h2p, sel, gate = _out_router(merged, x2, wo, g_ffn, w_router, b_router)

    dest4, g4, nblk, start_blk = _routing(sel, gate)
    n_rows = T * TOP_K + E * ROW_BLOCK
    dest_slots = dest4[:, :TOP_K].T.reshape(TOP_K * T)
    max_blocks = n_rows // ROW_BLOCK
    n_items = E + -(-(max_blocks - E) // ITEM_BLOCKS) + 1
    item_e, item_b, item_n, n_active = _work_items(nblk[0], start_blk[0], n_items)
    xs = _sc_scatter_rows(h2p, dest_slots, n_rows, TOP_K)
    ys = _experts(xs, item_e, item_b, item_n, n_active,
                  w_expert_up, b_expert_up, w_expert_down, b_expert_down)
    out = None
    t_part = T // COMBINE_PARTS
    for p in range(COMBINE_PARTS):
        idx = dest4[p * t_part:(p + 1) * t_part, :TOP_K].T.reshape(TOP_K * t_part)
        yk = _sc_gather_rows(ys, idx)
        out = _combine(yk, g4, x1, g_final, p, COMBINE_PARTS, out)
    return out.reshape(B, S, D)
```

```python
import functools
import math

import jax
import jax.numpy as jnp
import numpy as np
from jax import lax
from jax.experimental import pallas as pl
from jax.experimental.pallas import tpu as pltpu
from jax.experimental.pallas import tpu_sc as plsc

F32 = jnp.float32
BF16 = jnp.bfloat16
I32 = jnp.int32

NORM_EPS = 1e-5
LANES = 128
SUBLANES = 8
NT_DIMS = (((1,), (1,)), ((), ()))
HEAD_DIM = 128
CHUNK = 128
GROUP_DIM = 128
TOP_K = 4
SWIGLU_ALPHA = 1.702
SWIGLU_LIMIT = 7.0
LOG2E = math.log2(math.e)

VMEM_LIMIT_BYTES = 56 * 1024 * 1024

ROW_BLOCK = 128
ITEM_BLOCKS = 12
ITEM_ROWS = ITEM_BLOCKS * ROW_BLOCK
REGION_BLOCKS = (8, 4, 2, 1)
assert sum(REGION_BLOCKS) >= ITEM_BLOCKS
COMPUTE_PLANS = {**{n: ((0, n),) for n in range(4, ITEM_BLOCKS + 1)},
                 3: ((0, 2), (2, 1)), 2: ((0, 2),), 1: ((0, 1),)}
assert set(COMPUTE_PLANS) == set(range(1, ITEM_BLOCKS + 1))
FF_TILE = 256
COMBINE_PARTS = 1


def _cparams(sem, **kw):
    return pltpu.CompilerParams(dimension_semantics=sem, vmem_limit_bytes=VMEM_LIMIT_BYTES, **kw)


def _pack_pairs(x):
    c = x.shape[1] // 2
    hi = lax.bitcast_convert_type(x[:, :c].astype(BF16).astype(F32), jnp.uint32)
    lo = lax.bitcast_convert_type(x[:, c:].astype(BF16).astype(F32), jnp.uint32)
    return hi | (lo >> 16)


def _unpack_pairs(w):
    hi = lax.bitcast_convert_type(w & jnp.uint32(0xFFFF0000), F32)
    lo = lax.bitcast_convert_type(w << 16, F32)
    return jnp.concatenate([hi, lo], axis=1)


def _norm_proj_kernel(x_ref, g_ref, w_ref, o_ref, h_ref, *, n_scaled, scale):
    j = pl.program_id(1)

    @pl.when(j == 0)
    def _():
        x = x_ref[...]
        ms = jnp.mean(x * x, axis=-1, keepdims=True)
        h_ref[...] = (x * lax.rsqrt(ms + NORM_EPS) * g_ref[...]).astype(h_ref.dtype)

    acc = lax.dot_general(h_ref[...], w_ref[...].astype(BF16), NT_DIMS, preferred_element_type=F32)
    o_ref[...] = (acc * jnp.where(j < n_scaled, scale, 1.0)).astype(o_ref.dtype)


def _norm_project(x, g, wt, n_cols, n_scaled_cols, scale, tm=1024, tn=1024):
    T, D = x.shape
    return pl.pallas_call(
        functools.partial(_norm_proj_kernel, n_scaled=n_scaled_cols // tn, scale=scale),
        grid=(T // tm, n_cols // tn),
        in_specs=[pl.BlockSpec((tm, D), lambda i, j: (i, 0)),
                  pl.BlockSpec((1, D), lambda i, j: (0, 0)),
                  pl.BlockSpec((tn, D), lambda i, j: (j, 0))],
        out_specs=[pl.BlockSpec((tm, tn), lambda i, j: (i, j)),
                   pl.BlockSpec((tm, D), lambda i, j: (i, 0))],
        out_shape=[jax.ShapeDtypeStruct((T, n_cols), BF16), jax.ShapeDtypeStruct((T, D), BF16)],
        compiler_params=_cparams(("arbitrary", "arbitrary")),
        name="norm_qkv_proj",
    )(x, g.reshape(1, D), wt)


def _cast_kernel(w_ref, o_ref):
    o_ref[...] = w_ref[...].astype(o_ref.dtype)


def _rows_bf16(wt, start, n_rows, tr=1024):
    D = wt.shape[1]
    assert start % SUBLANES == 0 and n_rows % tr == 0
    return pl.pallas_call(
        _cast_kernel,
        grid=(n_rows // tr,),
        in_specs=[pl.BlockSpec((pl.Element(tr), pl.Element(D)),
                               lambda i: (pl.multiple_of(start + i * tr, SUBLANES), 0))],
        out_specs=pl.BlockSpec((tr, D), lambda i: (i, 0)),
        out_shape=jax.ShapeDtypeStruct((n_rows, D), BF16),
        compiler_params=_cparams(("parallel",)),
        name="rows_bf16",
    )(wt)


def _forget_kernel(h_ref, wft_ref, bf_ref, c_ref):
    ft = lax.dot_general(wft_ref[...].astype(BF16), h_ref[...], NT_DIMS,
                         preferred_element_type=F32)
    c = jax.nn.log_sigmoid(ft + bf_ref[...])
    S = c.shape[1]
    lane = lax.broadcasted_iota(I32, c.shape, 1)
    shift = 1
    while shift < S:
        c = c + jnp.where(lane >= shift, pltpu.roll(c, shift, axis=1), 0.0)
        shift *= 2
    c_ref[0] = c * LOG2E


def _forget_cumsum(h, wt, f_off, b_forget, B, S):
    T, D = h.shape
    H = b_forget.shape[0]
    assert f_off % H == 0 and H % SUBLANES == 0
    return pl.pallas_call(
        _forget_kernel,
        grid=(B,),
        in_specs=[pl.BlockSpec((S, D), lambda b: (b, 0)),
                  pl.BlockSpec((H, D), lambda b: (f_off // H, 0)),
                  pl.BlockSpec((H, 1), lambda b: (0, 0))],
        out_specs=pl.BlockSpec((1, H, S), lambda b: (b, 0, 0)),
        out_shape=jax.ShapeDtypeStruct((B, H, S), F32),
        compiler_params=_cparams(("parallel",)),
        name="forget_cumsum",
    )(h, wt, b_forget.reshape(H, 1))


def _attn_kernel(q_ref, k_ref, v_ref, crow_ref, o_ref, vaug_ref, m_ref, acc_ref, *, n_heads, tq):
    i = pl.program_id(1)

    @pl.when(i == 0)
    def _():
        ones = jnp.ones((v_ref.shape[0], HEAD_DIM), BF16)
        for h in range(n_heads):
            vaug_ref[h, :, :HEAD_DIM] = v_ref[:, h * HEAD_DIM:(h + 1) * HEAD_DIM]
            vaug_ref[h, :, HEAD_DIM:] = ones

    m_ref[...] = jnp.full(m_ref.shape, -jnp.inf, F32)
    acc_ref[...] = jnp.zeros(acc_ref.shape, F32)
    row = lax.broadcasted_iota(I32, (tq, tq), 0)
    col = lax.broadcasted_iota(I32, (tq, tq), 1)
    causal = col <= row

    def step(j, masked):
        keys = pl.ds(pl.multiple_of(j * tq, tq), tq)
        for h in range(n_heads):
            hs = slice(h * HEAD_DIM, (h + 1) * HEAD_DIM)
            s = lax.dot_general(q_ref[:, hs], k_ref[keys, hs], (((1,), (1,)), ((), ())),
                                preferred_element_type=F32) - crow_ref[0, h, j]
            if masked:
                s = jnp.where(causal, s, -jnp.inf)
            m_old = m_ref[h]
            m_new = jnp.maximum(m_old, jnp.max(s, axis=-1, keepdims=True))
            alpha = jnp.exp2(m_old - m_new)
            p = jnp.exp2(s - jnp.concatenate([m_new] * (tq // HEAD_DIM), axis=1))
            m_ref[h] = m_new
            pv = jnp.dot(p.astype(BF16), vaug_ref[h, keys, :], preferred_element_type=F32)
            acc_ref[h] = jnp.concatenate([alpha, alpha], axis=1) * acc_ref[h] + pv

    def body(j, _):
        step(j, False)
        return 0

    lax.fori_loop(0, i, body, 0)
    step(i, True)
    for h in range(n_heads):
        acc = acc_ref[h]
        o_ref[:, h * HEAD_DIM:(h + 1) * HEAD_DIM] = (acc[:, :HEAD_DIM] / acc[:, HEAD_DIM:]).astype(o_ref.dtype)


def _attention(qkv, c_row, B, S, n_heads, tq=512):
    T = qkv.shape[0]
    W = n_heads * HEAD_DIM
    nq = S // tq
    c_row5 = c_row.reshape(B, n_heads, nq, 1, tq)
    return pl.pallas_call(
        functools.partial(_attn_kernel, n_heads=n_heads, tq=tq),
        grid=(B, nq),
        in_specs=[pl.BlockSpec((tq, W), lambda b, i: (b * nq + i, 0)),
                  pl.BlockSpec((S, W), lambda b, i: (b, 1)),
                  pl.BlockSpec((S, W), lambda b, i: (b, 2)),
                  pl.BlockSpec((1, n_heads, nq, 1, tq), lambda b, i: (b, 0, 0, 0, 0))],
        out_specs=pl.BlockSpec((tq, W), lambda b, i: (b * nq + i, 0)),
        out_shape=jax.ShapeDtypeStruct((T, W), BF16),
        scratch_shapes=[pltpu.VMEM((n_heads, S, 2 * HEAD_DIM), BF16),
                        pltpu.VMEM((n_heads, tq, HEAD_DIM), F32),
                        pltpu.VMEM((n_heads, tq, 2 * HEAD_DIM), F32)],
        compiler_params=_cparams(("arbitrary", "arbitrary")),
        name="fox_attention",
    )(qkv, qkv, qkv, c_row5)


def _gmlp_kernel(h_ref, wz_ref, g_ref, b_ref, ws_ref, bst_ref, o_ref, *, n_groups):
    z = lax.dot_general(h_ref[...], wz_ref[...], NT_DIMS, preferred_element_type=F32)
    z = 0.5 * z * (1.0 + lax.erf(z * (1.0 / math.sqrt(2.0))))
    W = z.shape[1] // 2
    u = z[:, :W]
    v = z[:, W:]
    mu = jnp.mean(v, axis=-1, keepdims=True)
    var = jnp.mean(jnp.square(v - mu), axis=-1, keepdims=True)
    vn = (v - mu) * lax.rsqrt(var + NORM_EPS) * g_ref[...] + b_ref[...]
    row = lax.broadcasted_iota(I32, (CHUNK, CHUNK), 0)
    col = lax.broadcasted_iota(I32, (CHUNK, CHUNK), 1)
    tril = col <= row
    tg = z.shape[0]
    for g in range(n_groups):
        gs = slice(g * GROUP_DIM, (g + 1) * GROUP_DIM)
        wg = jnp.where(tril, ws_ref[g], 0.0).astype(BF16)
        bias = bst_ref[:, g:g + 1]
        for c in range(tg // CHUNK):
            cs = slice(c * CHUNK, (c + 1) * CHUNK)
            mixed = jnp.dot(wg, vn[cs, gs].astype(BF16), preferred_element_type=F32) + bias
            o_ref[cs, gs] = (u[cs, gs] * mixed).astype(o_ref.dtype)


def _gmlp(h, wz, g_v_ln, b_v_ln, w_spatial, b_spatial, tg=1024):
    T, D = h.shape
    W = g_v_ln.shape[0]
    W2 = 2 * W
    G = w_spatial.shape[0]
    return pl.pallas_call(
        functools.partial(_gmlp_kernel, n_groups=G),
        grid=(T // tg,),
        in_specs=[pl.BlockSpec((tg, D), lambda i: (i, 0)),
                  pl.BlockSpec((W2, D), lambda i: (0, 0)),
                  pl.BlockSpec((1, W), lambda i: (0, 0)),
                  pl.BlockSpec((1, W), lambda i: (0, 0)),
                  pl.BlockSpec((G, CHUNK, CHUNK), lambda i: (0, 0, 0)),
                  pl.BlockSpec((CHUNK, G), lambda i: (0, 0))],
        out_specs=pl.BlockSpec((tg, W), lambda i: (i, 0)),
        out_shape=jax.ShapeDtypeStruct((T, W), BF16),
        compiler_params=_cparams(("parallel",)),
        name="gmlp",
    )(h, wz, g_v_ln.reshape(1, W), b_v_ln.reshape(1, W), w_spatial, b_spatial.T)


def _merge_kernel(attn_ref, sg_ref, h_ref, wa_ref, wb_ref, wga_ref, wgb_ref, o_ref):
    h = h_ref[...]
    a = jnp.dot(attn_ref[...], wa_ref[...], preferred_element_type=F32)
    ga = lax.dot_general(h, wga_ref[...], NT_DIMS, preferred_element_type=F32)
    m = jax.nn.sigmoid(ga) * a
    b = jnp.dot(sg_ref[...], wb_ref[...], preferred_element_type=F32)
    gb = lax.dot_general(h, wgb_ref[...], NT_DIMS, preferred_element_type=F32)
    o_ref[...] = (m + jax.nn.sigmoid(gb) * b).astype(o_ref.dtype)


def _merge(attn, sg, h, wa, wb, wg, g_off, tm=1024, tn=512):
    T, D = h.shape
    Wa = attn.shape[1]
    Wb = sg.shape[1]
    nt = D // tn
    g0 = g_off // tn
    return pl.pallas_call(
        _merge_kernel,
        grid=(nt, T // tm),
        in_specs=[pl.BlockSpec((tm, Wa), lambda j, i: (i, 0)),
                  pl.BlockSpec((tm, Wb), lambda j, i: (i, 0)),
                  pl.BlockSpec((tm, D), lambda j, i: (i, 0)),
                  pl.BlockSpec((Wa, tn), lambda j, i: (0, j)),
                  pl.BlockSpec((Wb, tn), lambda j, i: (0, j)),
                  pl.BlockSpec((tn, D), lambda j, i: (g0 + j, 0)),
                  pl.BlockSpec((tn, D), lambda j, i: (g0 + nt + j, 0))],
        out_specs=pl.BlockSpec((tm, tn), lambda j, i: (i, j)),
        out_shape=jax.ShapeDtypeStruct((T, D), BF16),
        compiler_params=_cparams(("arbitrary", "arbitrary")),
        name="gated_merge",
    )(attn, sg, h, wa, wb, wg, wg)


def _out_router_kernel(m_ref, x_ref, wo_ref, g_ref, wr_ref, br_ref,
                       x1_ref, h2_ref, sel_ref, gate_ref, *, n_parts):
    E = br_ref.shape[1]
    part = m_ref.shape[0] // n_parts
    for p in range(n_parts):
        rows = pl.ds(p * part, part)
        x1 = x_ref[rows, :] + jnp.dot(m_ref[rows, :], wo_ref[...], preferred_element_type=F32)
        x1_ref[rows, :] = x1
        ms = jnp.mean(x1 * x1, axis=-1, keepdims=True)
        h2 = x1 * lax.rsqrt(ms + NORM_EPS) * g_ref[...]
        h2_ref[rows, :] = _pack_pairs(h2)
        h2_hi = h2.astype(BF16)
        h2_lo = (h2 - h2_hi.astype(F32)).astype(BF16)
        pa = jnp.dot(h2_hi, wr_ref[...], preferred_element_type=F32)
        pb = jnp.dot(h2_lo, wr_ref[:, :E], preferred_element_type=F32)
        logits = pa[:, :E] + (pa[:, E:] + pb) + br_ref[...]
        lt = jnp.concatenate([logits, jnp.full((part, LANES - E), -jnp.inf, F32)], axis=1).T[:E]
        expert = lax.broadcasted_iota(I32, lt.shape, 0)
        beaten = jnp.zeros(lt.shape, F32)
        for e in range(E):
            le = lt[e:e + 1, :]
            beaten = beaten + jnp.where((le > lt) | ((le == lt) & (expert > e)), 1.0, 0.0)
        chosen = beaten < TOP_K
        top = jnp.max(jnp.where(chosen, lt, -jnp.inf), axis=0, keepdims=True)
        ex = jnp.where(chosen, jnp.exp(lt - top), 0.0)
        gate_t = ex / jnp.sum(ex, axis=0, keepdims=True)
        zeros = jnp.zeros((LANES - E, part), F32)
        sel_ref[rows, :] = jnp.concatenate([jnp.where(chosen, 1.0, 0.0), zeros], axis=0).T[:, :E]
        gate_ref[rows, :] = jnp.concatenate([gate_t, zeros], axis=0).T[:, :E]


def _out_router(merged, x, wo, g_ffn, w_router, b_router, to=512, n_parts=1):
    T, D = x.shape
    E = w_router.shape[1]
    w_hi = w_router.astype(BF16)
    w_lo = (w_router - w_hi.astype(F32)).astype(BF16)
    w_router = jnp.concatenate([w_hi, w_lo], axis=1)
    row = lambda i: (i, 0)
    fixed = lambda i: (0, 0)
    return pl.pallas_call(
        functools.partial(_out_router_kernel, n_parts=n_parts),
        grid=(T // to,),
        in_specs=[pl.BlockSpec((to, D), row), pl.BlockSpec((to, D), row),
                  pl.BlockSpec((D, D), fixed), pl.BlockSpec((1, D), fixed),
                  pl.BlockSpec((D, 2 * E), fixed), pl.BlockSpec((1, E), fixed)],
        out_specs=[pl.BlockSpec((to, D), row), pl.BlockSpec((to, D // 2), row),
                   pl.BlockSpec((to, E), row), pl.BlockSpec((to, E), row)],
        out_shape=[jax.ShapeDtypeStruct((T, D), F32), jax.ShapeDtypeStruct((T, D // 2), jnp.uint32),
                   jax.ShapeDtypeStruct((T, E), F32), jax.ShapeDtypeStruct((T, E), F32)],
        compiler_params=_cparams(("parallel",)),
        name="out_router",
    )(merged, x, wo, g_ffn.reshape(1, D), w_router, b_router.reshape(1, E))


def _routing_kernel(sel_ref, gate_ref, dest_ref, g4_ref, nblk_ref, start_ref, rank_ref, *, tile):
    T, E = sel_ref.shape
    nt = T // tile
    r = lax.broadcasted_iota(I32, (tile, tile), 0)
    c = lax.broadcasted_iota(I32, (tile, tile), 1)
    strict_lower = (c < r).astype(BF16)
    er = lax.broadcasted_iota(I32, (E, E), 0)
    ec = lax.broadcasted_iota(I32, (E, E), 1)
    strict_upper = (er < ec).astype(BF16)

    def pass1(t, carry):
        rows = pl.ds(pl.multiple_of(t * tile, tile), tile)
        a = sel_ref[rows, :]
        rank_ref[rows, :] = jnp.dot(strict_lower, a.astype(BF16), preferred_element_type=F32) + carry
        return carry + jnp.sum(a, axis=0, keepdims=True)

    counts = lax.fori_loop(0, nt, pass1, jnp.zeros((1, E), F32))
    nblk = jnp.floor((counts + (ROW_BLOCK - 1)) * (1.0 / ROW_BLOCK))
    start_blk = jnp.dot(nblk.astype(BF16), strict_upper, preferred_element_type=F32)
    nblk_ref[...] = nblk.astype(I32)
    start_ref[...] = start_blk.astype(I32)
    start_row = start_blk * float(ROW_BLOCK)
    lane = lax.broadcasted_iota(I32, (tile, 128), 1)

    def pass2(t, _):
        rows = pl.ds(pl.multiple_of(t * tile, tile), tile)
        a = sel_ref[rows, :]
        g = gate_ref[rows, :]
        dest_e = rank_ref[rows, :] + start_row
        slot = jnp.dot(a.astype(BF16), strict_upper, preferred_element_type=F32)
        d4 = jnp.zeros((tile, 128), F32)
        g4 = jnp.zeros((tile, 128), F32)
        for s in range(TOP_K):
            pick = (a > 0.5) & (slot == float(s))
            d4 = jnp.where(lane == s, jnp.sum(jnp.where(pick, dest_e, 0.0), axis=-1, keepdims=True), d4)
            g4 = jnp.where(lane == s, jnp.sum(jnp.where(pick, g, 0.0), axis=-1, keepdims=True), g4)
        dest_ref[rows, :] = d4.astype(I32)
        g4_ref[rows, :] = g4
        return 0

    lax.fori_loop(0, nt, pass2, 0)


def _routing(sel, gate, tile=512):
    T, E = sel.shape
    return pl.pallas_call(
        functools.partial(_routing_kernel, tile=tile),
        out_shape=[jax.ShapeDtypeStruct((T, 128), I32), jax.ShapeDtypeStruct((T, 128), F32),
                   jax.ShapeDtypeStruct((1, E), I32), jax.ShapeDtypeStruct((1, E), I32)],
        scratch_shapes=[pltpu.VMEM((T, E), F32)],
        compiler_params=pltpu.CompilerParams(vmem_limit_bytes=VMEM_LIMIT_BYTES),
        name="routing_ranks",
    )(sel, gate)


SC_CORES = 2
SC_SUBCORES = 16
SC_CHUNK = 64


def _sc_gather_rows(table, idx):
    n = idx.shape[0]
    W = table.shape[1]
    n_workers = SC_CORES * SC_SUBCORES
    per_worker = n // n_workers
    assert per_worker * n_workers == n and per_worker % SC_CHUNK == 0
    mesh = plsc.VectorSubcoreMesh(core_axis_name="c", subcore_axis_name="s",
                                  num_cores=SC_CORES, num_subcores=SC_SUBCORES)

    @functools.partial(
        pl.kernel, mesh=mesh,
        out_type=jax.ShapeDtypeStruct((n, W), table.dtype),
        scratch_types=[pltpu.VMEM((SC_CHUNK,), I32), pltpu.VMEM((SC_CHUNK, W), table.dtype),
                       pltpu.SemaphoreType.DMA],
        name="sc_gather_rows",
    )
    def gather(table_hbm, idx_hbm, out_hbm, idx_v, rows_v, sem):
        wid = lax.axis_index("s") * SC_CORES + lax.axis_index("c")
        base = wid * per_worker

        @pl.loop(0, per_worker // SC_CHUNK)
        def _(c):
            off = pl.multiple_of(base + c * SC_CHUNK, 8)
            pltpu.sync_copy(idx_hbm.at[pl.ds(off, SC_CHUNK)], idx_v)
            pltpu.async_copy(table_hbm.at[idx_v], rows_v, sem).wait()
            pltpu.sync_copy(rows_v, out_hbm.at[pl.ds(off, SC_CHUNK)])

    return gather(table, idx)


def _sc_scatter_rows(rows, idx, n_out, n_slots):
    T, W = rows.shape
    n_workers = SC_CORES * SC_SUBCORES
    per_worker = T // n_workers
    assert per_worker * n_workers == T and per_worker % SC_CHUNK == 0
    mesh = plsc.VectorSubcoreMesh(core_axis_name="c", subcore_axis_name="s",
                                  num_cores=SC_CORES, num_subcores=SC_SUBCORES)

    @functools.partial(
        pl.kernel, mesh=mesh,
        out_type=jax.ShapeDtypeStruct((n_out, W), rows.dtype),
        scratch_types=[pltpu.VMEM((SC_CHUNK,), I32), pltpu.VMEM((SC_CHUNK, W), rows.dtype)],
        name="sc_scatter_rows",
    )
    def scatter(rows_hbm, idx_hbm, out_hbm, idx_v, rows_v):
        wid = lax.axis_index("s") * SC_CORES + lax.axis_index("c")
        base = wid * per_worker

        @pl.loop(0, per_worker // SC_CHUNK)
        def _(c):
            off = pl.multiple_of(base + c * SC_CHUNK, 8)
            pltpu.sync_copy(rows_hbm.at[pl.ds(off, SC_CHUNK)], rows_v)
            for k in range(n_slots):
                pltpu.sync_copy(idx_hbm.at[pl.ds(pl.multiple_of(k * T + off, 8), SC_CHUNK)], idx_v)
                pltpu.sync_copy(rows_v, out_hbm.at[idx_v])

    return scatter(rows, idx)


def _expert_kernel(ie_ref, ib_ref, ins_ref,
                   xs_hbm, wup_ref, bup_ref, wdn_ref, bdn_ref, perm_ref, ys_hbm,
                   xg_ref, acc_ref, yst_ref, wupb_ref, wdnb_ref, gsem, osem, *, n_ff_tiles):
    i = pl.program_id(0)
    j = pl.program_id(1)
    n_items = pl.num_programs(0)
    nsub = ins_ref[i]
    slot = i % 2

    def for_regions(n_blocks, fn):
        first = jnp.int32(0)
        for count in REGION_BLOCKS:
            present = (n_blocks & count) != 0
            pl.when(present)(functools.partial(fn, first, count))
            first = first + jnp.where(present, count, 0)

    def rows_of(first, count):
        return pl.ds(pl.multiple_of(first * ROW_BLOCK, ROW_BLOCK), count * ROW_BLOCK)

    def in_copy(item, dst_slot, first, count):
        src = pl.multiple_of((ib_ref[item] + first) * ROW_BLOCK, ROW_BLOCK)
        return pltpu.make_async_copy(xs_hbm.at[pl.ds(src, count * ROW_BLOCK)],
                                     xg_ref.at[dst_slot, rows_of(first, count)], gsem.at[dst_slot])

    def fetch_item(item, n_blocks, dst_slot):
        for_regions(n_blocks, lambda first, count: in_copy(item, dst_slot, first, count).start())

    @pl.when(j == 0)
    def _():
        @pl.when(i == 0)
        def _():
            fetch_item(0, nsub, 0)
            acc_ref[...] = jnp.zeros(acc_ref.shape, F32)

        nxt = jnp.minimum(i + 1, n_items - 1)
        fetch_item(nxt, jnp.where(i + 1 < n_items, ins_ref[nxt], 0), 1 - slot)
        for_regions(nsub, lambda first, count: in_copy(i, slot, first, count).wait())

    @pl.when(nsub > 0)
    def _():
        bup = bup_ref[0]
        perm = perm_ref[...]
        half = perm.shape[0] // 2

        def run_blocks(first, count):
            if first == 0:
                wup = wup_ref[0].astype(BF16)
                wdn = wdn_ref[0].astype(BF16)
                wupb_ref[...] = wup
                wdnb_ref[...] = wdn
            else:
                wup = wupb_ref[...]
                wdn = wdnb_ref[...]
            rows = pl.ds(first * ROW_BLOCK, count * ROW_BLOCK)
            xb = _unpack_pairs(xg_ref[slot, rows, :]).astype(BF16)
            gu = (jnp.dot(xb, wup, preferred_element_type=F32) + bup).astype(BF16)
            glu_parts, lin_parts = [], []
            for p in range(gu.shape[1] // perm.shape[0]):
                gp = jnp.dot(gu[:, p * perm.shape[0]:(p + 1) * perm.shape[0]], perm,
                             preferred_element_type=F32)
                glu_parts.append(gp[:, :half])
                lin_parts.append(gp[:, half:])
            x_glu = jnp.minimum(jnp.concatenate(glu_parts, axis=1), SWIGLU_LIMIT)
            x_lin = jnp.clip(jnp.concatenate(lin_parts, axis=1), -SWIGLU_LIMIT, SWIGLU_LIMIT)
            act = x_glu * jax.nn.sigmoid(SWIGLU_ALPHA * x_glu) * (x_lin + 1.0)
            start = jnp.where(j == 0, bdn_ref[0], acc_ref[rows, :])
            acc_ref[rows, :] = start + jnp.dot(act.astype(BF16), wdn, preferred_element_type=F32)

        for n_blocks, plan in COMPUTE_PLANS.items():
            for first, count in plan:
                shared = [n for n, p in COMPUTE_PLANS.items() if (first, count) in p]
                if n_blocks == shared[0]:
                    cond = functools.reduce(jnp.logical_or, [nsub == n for n in shared])
                    pl.when(cond)(functools.partial(run_blocks, first, count))

    def out_copy(item, first, count):
        dst = pl.multiple_of((ib_ref[item] + first) * ROW_BLOCK, ROW_BLOCK)
        return pltpu.make_async_copy(yst_ref.at[rows_of(first, count)],
                                     ys_hbm.at[pl.ds(dst, count * ROW_BLOCK)], osem)

    @pl.when(j == n_ff_tiles - 1)
    def _():
        prev = jnp.maximum(i - 1, 0)
        for_regions(jnp.where(i > 0, ins_ref[prev], 0),
                    lambda first, count: out_copy(prev, first, count).wait())

        def leave(first, count):
            rows = rows_of(first, count)
            yst_ref[rows, :] = _pack_pairs(acc_ref[rows, :])
            out_copy(i, first, count).start()

        for_regions(nsub, leave)

        @pl.when(i == n_items - 1)
        def _():
            for_regions(nsub, lambda first, count: out_copy(i, first, count).wait())


def _experts(xs, item_e, item_b, item_n, n_active, w_up, b_up, w_down, b_down):
    n_rows, Dp = xs.shape
    D = 2 * Dp
    E, _, F2 = w_up.shape
    F = F2 // 2
    J = F // FF_TILE
    half = 128
    perm = np.zeros((2 * half, 2 * half), np.float32)
    perm[2 * np.arange(half), np.arange(half)] = 1.0
    perm[2 * np.arange(half) + 1, half + np.arange(half)] = 1.0

    def jj(i, j, ins):
        return jnp.where(ins[i] > 0, j, J - 1)

    grid_spec = pltpu.PrefetchScalarGridSpec(
        num_scalar_prefetch=3,
        grid=(n_active, J),
        in_specs=[pl.BlockSpec(memory_space=pl.ANY),
                  pl.BlockSpec((1, D, 2 * FF_TILE), lambda i, j, ie, ib, ins: (ie[i], 0, jj(i, j, ins))),
                  pl.BlockSpec((1, 1, 2 * FF_TILE), lambda i, j, ie, ib, ins: (ie[i], 0, jj(i, j, ins))),
                  pl.BlockSpec((1, FF_TILE, D), lambda i, j, ie, ib, ins: (ie[i], jj(i, j, ins), 0)),
                  pl.BlockSpec((1, 1, D), lambda i, j, ie, ib, ins: (ie[i], 0, 0)),
                  pl.BlockSpec((2 * half, 2 * half), lambda i, j, ie, ib, ins: (0, 0))],
        out_specs=pl.BlockSpec(memory_space=pl.ANY),
        scratch_shapes=[pltpu.VMEM((2, ITEM_ROWS, Dp), jnp.uint32),
                        pltpu.VMEM((ITEM_ROWS, D), F32),
                        pltpu.VMEM((ITEM_ROWS, Dp), jnp.uint32),
                        pltpu.VMEM((D, 2 * FF_TILE), BF16),
                        pltpu.VMEM((FF_TILE, D), BF16),
                        pltpu.SemaphoreType.DMA((2,)), pltpu.SemaphoreType.DMA(())],
    )
    return pl.pallas_call(
        functools.partial(_expert_kernel, n_ff_tiles=J),
        grid_spec=grid_spec,
        out_shape=jax.ShapeDtypeStruct((n_rows, Dp), jnp.uint32),
        compiler_params=_cparams(("arbitrary", "arbitrary")),
        name="expert_ffn",
    )(item_e, item_b, item_n,
      xs, w_up, b_up.reshape(E, 1, F2), w_down, b_down.reshape(E, 1, D), jnp.asarray(perm, BF16))


def _combine_kernel(*refs):
    yk_refs, (g4_ref, x1_ref, g_ref), o_ref = refs[:TOP_K], refs[TOP_K:TOP_K + 3], refs[-1]
    y = x1_ref[...]
    g4 = g4_ref[...]
    for k in range(TOP_K):
        y = y + g4[:, k:k + 1] * _unpack_pairs(yk_refs[k][...])
    ms = jnp.mean(y * y, axis=-1, keepdims=True)
    o_ref[...] = y * lax.rsqrt(ms + NORM_EPS) * g_ref[...]


def _combine(yk, g4, x1, g_final, part, n_parts, out_so_far, tc=512):
    T, D = x1.shape
    nt = T // n_parts // tc
    first = part * nt
    slot_specs = [pl.BlockSpec((tc, D // 2), functools.partial(lambda i, k: (k * nt + i, 0), k=k))
                  for k in range(TOP_K)]
    operands = [*([yk] * TOP_K), g4, x1, g_final.reshape(1, D)]
    in_specs = slot_specs + [pl.BlockSpec((tc, 128), lambda i: (first + i, 0)),
                             pl.BlockSpec((tc, D), lambda i: (first + i, 0)),
                             pl.BlockSpec((1, D), lambda i: (0, 0))]
    aliases = {}
    if out_so_far is not None:
        aliases = {len(operands): 0}
        operands.append(out_so_far)
        in_specs.append(pl.BlockSpec(memory_space=pl.ANY))
    return pl.pallas_call(
        _combine_kernel,
        grid=(nt,),
        in_specs=in_specs,
        out_specs=pl.BlockSpec((tc, D), lambda i: (first + i, 0)),
        out_shape=jax.ShapeDtypeStruct((T, D), F32),
        input_output_aliases=aliases,
        compiler_params=_cparams(("parallel",)),
        name="combine_norm",
    )(*operands)


def _work_items(nblk, start_blk, n_items):
    E = nblk.shape[0]
    per_e = (nblk + ITEM_BLOCKS - 1) // ITEM_BLOCKS
    ends = jnp.cumsum(per_e)
    total = ends[-1]
    idx = jnp.arange(n_items, dtype=I32)
    e = jnp.minimum(jnp.searchsorted(ends, idx, side="right"), E - 1).astype(I32)
    local = idx - (ends[e] - per_e[e])
    active = idx < total
    last_e = e[jnp.maximum(total - 1, 0)]
    item_e = jnp.where(active, e, last_e).astype(I32)
    item_b = jnp.where(active, start_blk[e] + local * ITEM_BLOCKS, 0).astype(I32)
    item_n = jnp.where(active, jnp.clip(nblk[e] - local * ITEM_BLOCKS, 0, ITEM_BLOCKS), 0).astype(I32)
    return item_e, item_b, item_n, jnp.maximum(total, 1).astype(I32)


def kernel(x, g_mix, w_in, b_forget, g_v_ln, b_v_ln, w_spatial, b_spatial, w_branch_attn, w_branch_gmlp, w_out, g_ffn, w_router, b_router, w_expert_up, b_expert_up, w_expert_down, b_expert_down, g_final):
    B, S, D = x.shape
    T = B * S
    n_heads = b_forget.shape[0]
    attn_w = n_heads * HEAD_DIM
    gmlp_w = g_v_ln.shape[0]
    E = w_router.shape[1]
    off_f = 3 * attn_w
    off_z = off_f + n_heads
    off_g = off_z + 2 * gmlp_w

    x2 = x.reshape(T, D)
    w_in_t = w_in.T
    wzg = _rows_bf16(w_in_t, off_z, w_in.shape[1] - off_z)
    wa = w_branch_attn.astype(BF16)
    wb = w_branch_gmlp.astype(BF16)
    wo = w_out.astype(BF16)

    qkv, h = _norm_project(x2, g_mix, w_in_t, 3 * attn_w, attn_w, LOG2E / math.sqrt(HEAD_DIM))
    c_row = _forget_cumsum(h, w_in_t, off_f, b_forget, B, S)
    attn = _attention(qkv, c_row, B, S, n_heads)
    sg = _gmlp(h, wzg, g_v_ln, b_v_ln, w_spatial, b_spatial)
    merged = _merge(attn, sg, h, wa, wb, wzg, off_g - off_z)
    x1, h2p, sel, gate = _out_router(merged, x2, wo, g_ffn, w_router, b_router)

    dest4, g4, nblk, start_blk = _routing(sel, gate)
    n_rows = T * TOP_K + E * ROW_BLOCK
    dest_slots = dest4[:, :TOP_K].T.reshape(TOP_K * T)
    max_blocks = n_rows // ROW_BLOCK
    n_items = E + -(-(max_blocks - E) // ITEM_BLOCKS) + 1
    item_e, item_b, item_n, n_active = _work_items(nblk[0], start_blk[0], n_items)
    xs = _sc_scatter_rows(h2p, dest_slots, n_rows, TOP_K)
    ys = _experts(xs, item_e, item_b, item_n, n_active,
                  w_expert_up, b_expert_up, w_expert_down, b_expert_down)
    out = None
    t_part = T // COMBINE_PARTS
    for p in range(COMBINE_PARTS):
        idx = dest4[p * t_part:(p + 1) * t_part, :TOP_K].T.reshape(TOP_K * t_part)
        yk = _sc_gather_rows(ys, idx)
        out = _combine(yk, g4, x1, g_final, p, COMBINE_PARTS, out)
    return out.reshape(B, S, D)
```

```python
import functools
import math

import jax
import jax.numpy as jnp
import numpy as np
from jax import lax
from jax.experimental import pallas as pl
from jax.experimental.pallas import tpu as pltpu
from jax.experimental.pallas import tpu_sc as plsc

F32 = jnp.float32
BF16 = jnp.bfloat16
I32 = jnp.int32

NORM_EPS = 1e-5
LANES = 128
SUBLANES = 8
NT_DIMS = (((1,), (1,)), ((), ()))
HEAD_DIM = 128
CHUNK = 128
GROUP_DIM = 128
TOP_K = 4
SWIGLU_ALPHA = 1.702
SWIGLU_LIMIT = 7.0
LOG2E = math.log2(math.e)

VMEM_LIMIT_BYTES = 56 * 1024 * 1024

ROW_BLOCK = 128
ITEM_BLOCKS = 10
ITEM_ROWS = ITEM_BLOCKS * ROW_BLOCK
REGION_BLOCKS = (8, 4, 2, 1)
assert sum(REGION_BLOCKS) >= ITEM_BLOCKS
COMPUTE_PLANS = {**{n: ((0, n),) for n in range(4, ITEM_BLOCKS + 1)},
                 3: ((0, 2), (2, 1)), 2: ((0, 2),), 1: ((0, 1),)}
assert set(COMPUTE_PLANS) == set(range(1, ITEM_BLOCKS + 1))
FF_TILE = 256
COMBINE_PARTS = 1


def _cparams(sem, **kw):
    return pltpu.CompilerParams(dimension_semantics=sem, vmem_limit_bytes=VMEM_LIMIT_BYTES, **kw)


def _pack_pairs(x):
    c = x.shape[1] // 2
    hi = lax.bitcast_convert_type(x[:, :c].astype(BF16).astype(F32), jnp.uint32)
    lo = lax.bitcast_convert_type(x[:, c:].astype(BF16).astype(F32), jnp.uint32)
    return hi | (lo >> 16)


def _unpack_pairs(w):
    hi = lax.bitcast_convert_type(w & jnp.uint32(0xFFFF0000), F32)
    lo = lax.bitcast_convert_type(w << 16, F32)
    return jnp.concatenate([hi, lo], axis=1)


def _norm_proj_kernel(x_ref, g_ref, w_ref, o_ref, h_ref, *, n_scaled, scale):
    j = pl.program_id(1)

    def project(h):
        acc = lax.dot_general(h, w_ref[...].astype(BF16), NT_DIMS, preferred_element_type=F32)
        o_ref[...] = (acc * jnp.where(j < n_scaled, scale, 1.0)).astype(o_ref.dtype)

    @pl.when(j == 0)
    def _():
        x = x_ref[...]
        ms = jnp.mean(x * x, axis=-1, keepdims=True)
        h = (x * lax.rsqrt(ms + NORM_EPS) * g_ref[...]).astype(h_ref.dtype)
        h_ref[...] = h
        project(h)

    @pl.when(j > 0)
    def _():
        project(h_ref[...])


def _norm_project(x, g, wt, n_cols, n_scaled_cols, scale, tm=1024, tn=1024):
    T, D = x.shape
    return pl.pallas_call(
        functools.partial(_norm_proj_kernel, n_scaled=n_scaled_cols // tn, scale=scale),
        grid=(T // tm, n_cols // tn),
        in_specs=[pl.BlockSpec((tm, D), lambda i, j: (i, 0)),
                  pl.BlockSpec((1, D), lambda i, j: (0, 0)),
                  pl.BlockSpec((tn, D), lambda i, j: (j, 0))],
        out_specs=[pl.BlockSpec((tm, tn), lambda i, j: (i, j)),
                   pl.BlockSpec((tm, D), lambda i, j: (i, 0))],
        out_shape=[jax.ShapeDtypeStruct((T, n_cols), BF16), jax.ShapeDtypeStruct((T, D), BF16)],
        compiler_params=_cparams(("arbitrary", "arbitrary")),
        name="norm_qkv_proj",
    )(x, g.reshape(1, D), wt)


def _cast_kernel(w_ref, o_ref):
    o_ref[...] = w_ref[...].astype(o_ref.dtype)


def _rows_bf16(wt, start, n_rows, tr=1024):
    D = wt.shape[1]
    assert start % SUBLANES == 0 and n_rows % tr == 0
    return pl.pallas_call(
        _cast_kernel,
        grid=(n_rows // tr,),
        in_specs=[pl.BlockSpec((pl.Element(tr), pl.Element(D)),
                               lambda i: (pl.multiple_of(start + i * tr, SUBLANES), 0))],
        out_specs=pl.BlockSpec((tr, D), lambda i: (i, 0)),
        out_shape=jax.ShapeDtypeStruct((n_rows, D), BF16),
        compiler_params=_cparams(("parallel",)),
        name="rows_bf16",
    )(wt)


def _forget_kernel(h_ref, wft_ref, bf_ref, c_ref):
    ft = lax.dot_general(wft_ref[...].astype(BF16), h_ref[...], NT_DIMS,
                         preferred_element_type=F32)
    c = jax.nn.log_sigmoid(ft + bf_ref[...])
    S = c.shape[1]
    lane = lax.broadcasted_iota(I32, c.shape, 1)
    shift = 1
    while shift < S:
        c = c + jnp.where(lane >= shift, pltpu.roll(c, shift, axis=1), 0.0)
        shift *= 2
    c_ref[0] = c * LOG2E


def _forget_cumsum(h, wt, f_off, b_forget, B, S):
    T, D = h.shape
    H = b_forget.shape[0]
    assert f_off % H == 0 and H % SUBLANES == 0
    return pl.pallas_call(
        _forget_kernel,
        grid=(B,),
        in_specs=[pl.BlockSpec((S, D), lambda b: (b, 0)),
                  pl.BlockSpec((H, D), lambda b: (f_off // H, 0)),
                  pl.BlockSpec((H, 1), lambda b: (0, 0))],
        out_specs=pl.BlockSpec((1, H, S), lambda b: (b, 0, 0)),
        out_shape=jax.ShapeDtypeStruct((B, H, S), F32),
        compiler_params=_cparams(("parallel",)),
        name="forget_cumsum",
    )(h, wt, b_forget.reshape(H, 1))


def _attn_kernel(q_ref, k_ref, v_ref, crow_ref, o_ref, vaug_ref, m_ref, acc_ref, *, n_heads, tq):
    i = pl.program_id(1)

    @pl.when(i == 0)
    def _():
        ones = jnp.ones((v_ref.shape[0], HEAD_DIM), BF16)
        for h in range(n_heads):
            vaug_ref[h, :, :HEAD_DIM] = v_ref[:, h * HEAD_DIM:(h + 1) * HEAD_DIM]
            vaug_ref[h, :, HEAD_DIM:] = ones

    m_ref[...] = jnp.full(m_ref.shape, -jnp.inf, F32)
    acc_ref[...] = jnp.zeros(acc_ref.shape, F32)
    row = lax.broadcasted_iota(I32, (tq, tq), 0)
    col = lax.broadcasted_iota(I32, (tq, tq), 1)
    causal = col <= row

    def step(j, masked):
        keys = pl.ds(pl.multiple_of(j * tq, tq), tq)
        for h in range(n_heads):
            hs = slice(h * HEAD_DIM, (h + 1) * HEAD_DIM)
            s = lax.dot_general(q_ref[:, hs], k_ref[keys, hs], (((1,), (1,)), ((), ())),
                                preferred_element_type=F32) - crow_ref[0, h, j]
            if masked:
                s = jnp.where(causal, s, -jnp.inf)
            m_old = m_ref[h]
            m_new = jnp.maximum(m_old, jnp.max(s, axis=-1, keepdims=True))
            alpha = jnp.exp2(m_old - m_new)
            p = jnp.exp2(s - jnp.concatenate([m_new] * (tq // HEAD_DIM), axis=1))
            m_ref[h] = m_new
            pv = jnp.dot(p.astype(BF16), vaug_ref[h, keys, :], preferred_element_type=F32)
            acc_ref[h] = jnp.concatenate([alpha, alpha], axis=1) * acc_ref[h] + pv

    def body(j, _):
        step(j, False)
        return 0

    lax.fori_loop(0, i, body, 0)
    step(i, True)
    for h in range(n_heads):
        acc = acc_ref[h]
        o_ref[:, h * HEAD_DIM:(h + 1) * HEAD_DIM] = (acc[:, :HEAD_DIM] / acc[:, HEAD_DIM:]).astype(o_ref.dtype)


def _attention(qkv, c_row, B, S, n_heads, tq=512):
    T = qkv.shape[0]
    W = n_heads * HEAD_DIM
    nq = S // tq
    c_row5 = c_row.reshape(B, n_heads, nq, 1, tq)
    return pl.pallas_call(
        functools.partial(_attn_kernel, n_heads=n_heads, tq=tq),
        grid=(B, nq),
        in_specs=[pl.BlockSpec((tq, W), lambda b, i: (b * nq + i, 0)),
                  pl.BlockSpec((S, W), lambda b, i: (b, 1)),
                  pl.BlockSpec((S, W), lambda b, i: (b, 2)),
                  pl.BlockSpec((1, n_heads, nq, 1, tq), lambda b, i: (b, 0, 0, 0, 0))],
        out_specs=pl.BlockSpec((tq, W), lambda b, i: (b * nq + i, 0)),
        out_shape=jax.ShapeDtypeStruct((T, W), BF16),
        scratch_shapes=[pltpu.VMEM((n_heads, S, 2 * HEAD_DIM), BF16),
                        pltpu.VMEM((n_heads, tq, HEAD_DIM), F32),
                        pltpu.VMEM((n_heads, tq, 2 * HEAD_DIM), F32)],
        compiler_params=_cparams(("arbitrary", "arbitrary")),
        name="fox_attention",
    )(qkv, qkv, qkv, c_row5)


def _gmlp_kernel(h_ref, wz_ref, g_ref, b_ref, ws_ref, bst_ref, o_ref, *, n_groups):
    z = lax.dot_general(h_ref[...], wz_ref[...], NT_DIMS, preferred_element_type=F32)
    z = 0.5 * z * (1.0 + lax.erf(z * (1.0 / math.sqrt(2.0))))
    W = z.shape[1] // 2
    u = z[:, :W]
    v = z[:, W:]
    mu = jnp.mean(v, axis=-1, keepdims=True)
    var = jnp.mean(jnp.square(v - mu), axis=-1, keepdims=True)
    vn = (v - mu) * lax.rsqrt(var + NORM_EPS) * g_ref[...] + b_ref[...]
    row = lax.broadcasted_iota(I32, (CHUNK, CHUNK), 0)
    col = lax.broadcasted_iota(I32, (CHUNK, CHUNK), 1)
    tril = col <= row
    tg = z.shape[0]
    for g in range(n_groups):
        gs = slice(g * GROUP_DIM, (g + 1) * GROUP_DIM)
        wg = jnp.where(tril, ws_ref[g], 0.0).astype(BF16)
        bias = bst_ref[:, g:g + 1]
        for c in range(tg // CHUNK):
            cs = slice(c * CHUNK, (c + 1) * CHUNK)
            mixed = jnp.dot(wg, vn[cs, gs].astype(BF16), preferred_element_type=F32) + bias
            o_ref[cs, gs] = (u[cs, gs] * mixed).astype(o_ref.dtype)


def _gmlp(h, wz, g_v_ln, b_v_ln, w_spatial, b_spatial, tg=1024):
    T, D = h.shape
    W = g_v_ln.shape[0]
    W2 = 2 * W
    G = w_spatial.shape[0]
    return pl.pallas_call(
        functools.partial(_gmlp_kernel, n_groups=G),
        grid=(T // tg,),
        in_specs=[pl.BlockSpec((tg, D), lambda i: (i, 0)),
                  pl.BlockSpec((W2, D), lambda i: (0, 0)),
                  pl.BlockSpec((1, W), lambda i: (0, 0)),
                  pl.BlockSpec((1, W), lambda i: (0, 0)),
                  pl.BlockSpec((G, CHUNK, CHUNK), lambda i: (0, 0, 0)),
                  pl.BlockSpec((CHUNK, G), lambda i: (0, 0))],
        out_specs=pl.BlockSpec((tg, W), lambda i: (i, 0)),
        out_shape=jax.ShapeDtypeStruct((T, W), BF16),
        compiler_params=_cparams(("parallel",)),
        name="gmlp",
    )(h, wz, g_v_ln.reshape(1, W), b_v_ln.reshape(1, W), w_spatial, b_spatial.T)


def _merge_kernel(attn_ref, sg_ref, h_ref, wa_ref, wb_ref, wga_ref, wgb_ref, o_ref):
    h = h_ref[...]
    a = jnp.dot(attn_ref[...], wa_ref[...], preferred_element_type=F32)
    ga = lax.dot_general(h, wga_ref[...], NT_DIMS, preferred_element_type=F32)
    m = jax.nn.sigmoid(ga) * a
    b = jnp.dot(sg_ref[...], wb_ref[...], preferred_element_type=F32)
    gb = lax.dot_general(h, wgb_ref[...], NT_DIMS, preferred_element_type=F32)
    o_ref[...] = (m + jax.nn.sigmoid(gb) * b).astype(o_ref.dtype)


def _merge(attn, sg, h, wa, wb, wg, g_off, tm=1024, tn=512):
    T, D = h.shape
    Wa = attn.shape[1]
    Wb = sg.shape[1]
    nt = D // tn
    g0 = g_off // tn
    return pl.pallas_call(
        _merge_kernel,
        grid=(nt, T // tm),
        in_specs=[pl.BlockSpec((tm, Wa), lambda j, i: (i, 0)),
                  pl.BlockSpec((tm, Wb), lambda j, i: (i, 0)),
                  pl.BlockSpec((tm, D), lambda j, i: (i, 0)),
                  pl.BlockSpec((Wa, tn), lambda j, i: (0, j)),
                  pl.BlockSpec((Wb, tn), lambda j, i: (0, j)),
                  pl.BlockSpec((tn, D), lambda j, i: (g0 + j, 0)),
                  pl.BlockSpec((tn, D), lambda j, i: (g0 + nt + j, 0))],
        out_specs=pl.BlockSpec((tm, tn), lambda j, i: (i, j)),
        out_shape=jax.ShapeDtypeStruct((T, D), BF16),
        compiler_params=_cparams(("arbitrary", "arbitrary")),
        name="gated_merge",
    )(attn, sg, h, wa, wb, wg, wg)


def _out_router_kernel(m_ref, x_ref, wo_ref, g_ref, wr_ref, br_ref,
                       x1_ref, h2_ref, sel_ref, gate_ref, *, n_parts):
    E = br_ref.shape[1]
    part = m_ref.shape[0] // n_parts
    for p in range(n_parts):
        rows = pl.ds(p * part, part)
        x1 = x_ref[rows, :] + jnp.dot(m_ref[rows, :], wo_ref[...], preferred_element_type=F32)
        x1_ref[rows, :] = x1
        ms = jnp.mean(x1 * x1, axis=-1, keepdims=True)
        h2 = x1 * lax.rsqrt(ms + NORM_EPS) * g_ref[...]
        h2_ref[rows, :] = _pack_pairs(h2)
        h2_hi = h2.astype(BF16)
        h2_lo = (h2 - h2_hi.astype(F32)).astype(BF16)
        pa = jnp.dot(h2_hi, wr_ref[...], preferred_element_type=F32)
        pb = jnp.dot(h2_lo, wr_ref[:, :E], preferred_element_type=F32)
        logits = pa[:, :E] + (pa[:, E:] + pb) + br_ref[...]
        lt = jnp.concatenate([logits, jnp.full((part, LANES - E), -jnp.inf, F32)], axis=1).T[:E]
        expert = lax.broadcasted_iota(I32, lt.shape, 0)
        beaten = jnp.zeros(lt.shape, F32)
        for e in range(E):
            le = lt[e:e + 1, :]
            beaten = beaten + jnp.where((le > lt) | ((le == lt) & (expert > e)), 1.0, 0.0)
        chosen = beaten < TOP_K
        top = jnp.max(jnp.where(chosen, lt, -jnp.inf), axis=0, keepdims=True)
        ex = jnp.where(chosen, jnp.exp(lt - top), 0.0)
        gate_t = ex / jnp.sum(ex, axis=0, keepdims=True)
        zeros = jnp.zeros((LANES - E, part), F32)
        sel_ref[rows, :] = jnp.concatenate([jnp.where(chosen, 1.0, 0.0), zeros], axis=0).T[:, :E]
        gate_ref[rows, :] = jnp.concatenate([gate_t, zeros], axis=0).T[:, :E]


def _out_router(merged, x, wo, g_ffn, w_router, b_router, to=512, n_parts=1):
    T, D = x.shape
    E = w_router.shape[1]
    w_hi = w_router.astype(BF16)
    w_lo = (w_router - w_hi.astype(F32)).astype(BF16)
    w_router = jnp.concatenate([w_hi, w_lo], axis=1)
    row = lambda i: (i, 0)
    fixed = lambda i: (0, 0)
    return pl.pallas_call(
        functools.partial(_out_router_kernel, n_parts=n_parts),
        grid=(T // to,),
        in_specs=[pl.BlockSpec((to, D), row), pl.BlockSpec((to, D), row),
                  pl.BlockSpec((D, D), fixed), pl.BlockSpec((1, D), fixed),
                  pl.BlockSpec((D, 2 * E), fixed), pl.BlockSpec((1, E), fixed)],
        out_specs=[pl.BlockSpec((to, D), row), pl.BlockSpec((to, D // 2), row),
                   pl.BlockSpec((to, E), row), pl.BlockSpec((to, E), row)],
        out_shape=[jax.ShapeDtypeStruct((T, D), F32), jax.ShapeDtypeStruct((T, D // 2), jnp.uint32),
                   jax.ShapeDtypeStruct((T, E), F32), jax.ShapeDtypeStruct((T, E), F32)],
        compiler_params=_cparams(("parallel",)),
        name="out_router",
    )(merged, x, wo, g_ffn.reshape(1, D), w_router, b_router.reshape(1, E))


def _routing_kernel(sel_ref, gate_ref, dest_ref, g4_ref, nblk_ref, start_ref, rank_ref, *, tile):
    T, E = sel_ref.shape
    nt = T // tile
    r = lax.broadcasted_iota(I32, (tile, tile), 0)
    c = lax.broadcasted_iota(I32, (tile, tile), 1)
    strict_lower = (c < r).astype(BF16)
    er = lax.broadcasted_iota(I32, (E, E), 0)
    ec = lax.broadcasted_iota(I32, (E, E), 1)
    strict_upper = (er < ec).astype(BF16)

    def pass1(t, carry):
        rows = pl.ds(pl.multiple_of(t * tile, tile), tile)
        a = sel_ref[rows, :]
        rank_ref[rows, :] = jnp.dot(strict_lower, a.astype(BF16), preferred_element_type=F32) + carry
        return carry + jnp.sum(a, axis=0, keepdims=True)

    counts = lax.fori_loop(0, nt, pass1, jnp.zeros((1, E), F32))
    nblk = jnp.floor((counts + (ROW_BLOCK - 1)) * (1.0 / ROW_BLOCK))
    start_blk = jnp.dot(nblk.astype(BF16), strict_upper, preferred_element_type=F32)
    nblk_ref[...] = nblk.astype(I32)
    start_ref[...] = start_blk.astype(I32)
    start_row = start_blk * float(ROW_BLOCK)
    lane = lax.broadcasted_iota(I32, (tile, 128), 1)

    def pass2(t, _):
        rows = pl.ds(pl.multiple_of(t * tile, tile), tile)
        a = sel_ref[rows, :]
        g = gate_ref[rows, :]
        dest_e = rank_ref[rows, :] + start_row
        slot = jnp.dot(a.astype(BF16), strict_upper, preferred_element_type=F32)
        d4 = jnp.zeros((tile, 128), F32)
        g4 = jnp.zeros((tile, 128), F32)
        for s in range(TOP_K):
            pick = (a > 0.5) & (slot == float(s))
            d4 = jnp.where(lane == s, jnp.sum(jnp.where(pick, dest_e, 0.0), axis=-1, keepdims=True), d4)
            g4 = jnp.where(lane == s, jnp.sum(jnp.where(pick, g, 0.0), axis=-1, keepdims=True), g4)
        dest_ref[rows, :] = d4.astype(I32)
        g4_ref[rows, :] = g4
        return 0

    lax.fori_loop(0, nt, pass2, 0)


def _routing(sel, gate, tile=512):
    T, E = sel.shape
    return pl.pallas_call(
        functools.partial(_routing_kernel, tile=tile),
        out_shape=[jax.ShapeDtypeStruct((T, 128), I32), jax.ShapeDtypeStruct((T, 128), F32),
                   jax.ShapeDtypeStruct((1, E), I32), jax.ShapeDtypeStruct((1, E), I32)],
        scratch_shapes=[pltpu.VMEM((T, E), F32)],
        compiler_params=pltpu.CompilerParams(vmem_limit_bytes=VMEM_LIMIT_BYTES),
        name="routing_ranks",
    )(sel, gate)


SC_CORES = 2
SC_SUBCORES = 16
SC_CHUNK = 64


def _sc_gather_rows(table, idx):
    n = idx.shape[0]
    W = table.shape[1]
    n_workers = SC_CORES * SC_SUBCORES
    per_worker = n // n_workers
    assert per_worker * n_workers == n and per_worker % SC_CHUNK == 0
    mesh = plsc.VectorSubcoreMesh(core_axis_name="c", subcore_axis_name="s",
                                  num_cores=SC_CORES, num_subcores=SC_SUBCORES)

    @functools.partial(
        pl.kernel, mesh=mesh,
        out_type=jax.ShapeDtypeStruct((n, W), table.dtype),
        scratch_types=[pltpu.VMEM((SC_CHUNK,), I32), pltpu.VMEM((SC_CHUNK, W), table.dtype),
                       pltpu.SemaphoreType.DMA],
        name="sc_gather_rows",
    )
    def gather(table_hbm, idx_hbm, out_hbm, idx_v, rows_v, sem):
        wid = lax.axis_index("s") * SC_CORES + lax.axis_index("c")
        base = wid * per_worker

        @pl.loop(0, per_worker // SC_CHUNK)
        def _(c):
            off = pl.multiple_of(base + c * SC_CHUNK, 8)
            pltpu.sync_copy(idx_hbm.at[pl.ds(off, SC_CHUNK)], idx_v)
            pltpu.async_copy(table_hbm.at[idx_v], rows_v, sem).wait()
            pltpu.sync_copy(rows_v, out_hbm.at[pl.ds(off, SC_CHUNK)])

    return gather(table, idx)


def _sc_scatter_rows(rows, idx, n_out, n_slots):
    T, W = rows.shape
    n_workers = SC_CORES * SC_SUBCORES
    per_worker = T // n_workers
    assert per_worker * n_workers == T and per_worker % SC_CHUNK == 0
    mesh = plsc.VectorSubcoreMesh(core_axis_name="c", subcore_axis_name="s",
                                  num_cores=SC_CORES, num_subcores=SC_SUBCORES)

    @functools.partial(
        pl.kernel, mesh=mesh,
        out_type=jax.ShapeDtypeStruct((n_out, W), rows.dtype),
        scratch_types=[pltpu.VMEM((SC_CHUNK,), I32), pltpu.VMEM((SC_CHUNK, W), rows.dtype)],
        name="sc_scatter_rows",
    )
    def scatter(rows_hbm, idx_hbm, out_hbm, idx_v, rows_v):
        wid = lax.axis_index("s") * SC_CORES + lax.axis_index("c")
        base = wid * per_worker

        @pl.loop(0, per_worker // SC_CHUNK)
        def _(c):
            off = pl.multiple_of(base + c * SC_CHUNK, 8)
            pltpu.sync_copy(rows_hbm.at[pl.ds(off, SC_CHUNK)], rows_v)
            for k in range(n_slots):
                pltpu.sync_copy(idx_hbm.at[pl.ds(pl.multiple_of(k * T + off, 8), SC_CHUNK)], idx_v)
                pltpu.sync_copy(rows_v, out_hbm.at[idx_v])

    return scatter(rows, idx)


def _expert_kernel(ie_ref, ib_ref, ins_ref,
                   xs_hbm, wup_ref, bup_ref, wdn_ref, bdn_ref, perm_ref, ys_hbm,
                   xg_ref, acc_ref, yst_ref, wupb_ref, wdnb_ref, gsem, osem, *, n_ff_tiles):
    i = pl.program_id(0)
    j = pl.program_id(1)
    n_items = pl.num_programs(0)
    nsub = ins_ref[i]
    slot = i % 2

    def for_regions(n_blocks, fn):
        first = jnp.int32(0)
        for count in REGION_BLOCKS:
            present = (n_blocks & count) != 0
            pl.when(present)(functools.partial(fn, first, count))
            first = first + jnp.where(present, count, 0)

    def rows_of(first, count):
        return pl.ds(pl.multiple_of(first * ROW_BLOCK, ROW_BLOCK), count * ROW_BLOCK)

    def in_copy(item, dst_slot, first, count):
        src = pl.multiple_of((ib_ref[item] + first) * ROW_BLOCK, ROW_BLOCK)
        return pltpu.make_async_copy(xs_hbm.at[pl.ds(src, count * ROW_BLOCK)],
                                     xg_ref.at[dst_slot, rows_of(first, count)], gsem.at[dst_slot])

    def fetch_item(item, n_blocks, dst_slot):
        for_regions(n_blocks, lambda first, count: in_copy(item, dst_slot, first, count).start())

    @pl.when(j == 0)
    def _():
        @pl.when(i == 0)
        def _():
            fetch_item(0, nsub, 0)
            acc_ref[...] = jnp.zeros(acc_ref.shape, F32)

        nxt = jnp.minimum(i + 1, n_items - 1)
        fetch_item(nxt, jnp.where(i + 1 < n_items, ins_ref[nxt], 0), 1 - slot)
        for_regions(nsub, lambda first, count: in_copy(i, slot, first, count).wait())

    @pl.when(nsub > 0)
    def _():
        bup = bup_ref[0]
        perm = perm_ref[...]
        half = perm.shape[0] // 2

        def run_blocks(first, count):
            if first == 0:
                wup = wup_ref[0].astype(BF16)
                wdn = wdn_ref[0].astype(BF16)
                wupb_ref[...] = wup
                wdnb_ref[...] = wdn
            else:
                wup = wupb_ref[...]
                wdn = wdnb_ref[...]
            rows = pl.ds(first * ROW_BLOCK, count * ROW_BLOCK)
            xb = _unpack_pairs(xg_ref[slot, rows, :]).astype(BF16)
            gu = (jnp.dot(xb, wup, preferred_element_type=F32) + bup).astype(BF16)
            glu_parts, lin_parts = [], []
            for p in range(gu.shape[1] // perm.shape[0]):
                gp = jnp.dot(gu[:, p * perm.shape[0]:(p + 1) * perm.shape[0]], perm,
                             preferred_element_type=F32)
                glu_parts.append(gp[:, :half])
                lin_parts.append(gp[:, half:])
            x_glu = jnp.minimum(jnp.concatenate(glu_parts, axis=1), SWIGLU_LIMIT)
            x_lin = jnp.clip(jnp.concatenate(lin_parts, axis=1), -SWIGLU_LIMIT, SWIGLU_LIMIT)
            act = x_glu * jax.nn.sigmoid(SWIGLU_ALPHA * x_glu) * (x_lin + 1.0)
            start = jnp.where(j == 0, bdn_ref[0], acc_ref[rows, :])
            acc_ref[rows, :] = start + jnp.dot(act.astype(BF16), wdn, preferred_element_type=F32)

        for n_blocks, plan in COMPUTE_PLANS.items():
            for first, count in plan:
                shared = [n for n, p in COMPUTE_PLANS.items() if (first, count) in p]
                if n_blocks == shared[0]:
                    cond = functools.reduce(jnp.logical_or, [nsub == n for n in shared])
                    pl.when(cond)(functools.partial(run_blocks, first, count))

    def out_copy(item, first, count):
        dst = pl.multiple_of((ib_ref[item] + first) * ROW_BLOCK, ROW_BLOCK)
        return pltpu.make_async_copy(yst_ref.at[rows_of(first, count)],
                                     ys_hbm.at[pl.ds(dst, count * ROW_BLOCK)], osem)

    @pl.when(j == n_ff_tiles - 1)
    def _():
        prev = jnp.maximum(i - 1, 0)
        for_regions(jnp.where(i > 0, ins_ref[prev], 0),
                    lambda first, count: out_copy(prev, first, count).wait())

        def leave(first, count):
            rows = rows_of(first, count)
            yst_ref[rows, :] = _pack_pairs(acc_ref[rows, :])
            out_copy(i, first, count).start()

        for_regions(nsub, leave)

        @pl.when(i == n_items - 1)
        def _():
            for_regions(nsub, lambda first, count: out_copy(i, first, count).wait())


def _experts(xs, item_e, item_b, item_n, n_active, w_up, b_up, w_down, b_down):
    n_rows, Dp = xs.shape
    D = 2 * Dp
    E, _, F2 = w_up.shape
    F = F2 // 2
    J = F // FF_TILE
    half = 128
    perm = np.zeros((2 * half, 2 * half), np.float32)
    perm[2 * np.arange(half), np.arange(half)] = 1.0
    perm[2 * np.arange(half) + 1, half + np.arange(half)] = 1.0

    def jj(i, j, ins):
        return jnp.where(ins[i] > 0, j, J - 1)

    grid_spec = pltpu.PrefetchScalarGridSpec(
        num_scalar_prefetch=3,
        grid=(n_active, J),
        in_specs=[pl.BlockSpec(memory_space=pl.ANY),
                  pl.BlockSpec((1, D, 2 * FF_TILE), lambda i, j, ie, ib, ins: (ie[i], 0, jj(i, j, ins))),
                  pl.BlockSpec((1, 1, 2 * FF_TILE), lambda i, j, ie, ib, ins: (ie[i], 0, jj(i, j, ins))),
                  pl.BlockSpec((1, FF_TILE, D), lambda i, j, ie, ib, ins: (ie[i], jj(i, j, ins), 0)),
                  pl.BlockSpec((1, 1, D), lambda i, j, ie, ib, ins: (ie[i], 0, 0)),
                  pl.BlockSpec((2 * half, 2 * half), lambda i, j, ie, ib, ins: (0, 0))],
        out_specs=pl.BlockSpec(memory_space=pl.ANY),
        scratch_shapes=[pltpu.VMEM((2, ITEM_ROWS, Dp), jnp.uint32),
                        pltpu.VMEM((ITEM_ROWS, D), F32),
                        pltpu.VMEM((ITEM_ROWS, Dp), jnp.uint32),
                        pltpu.VMEM((D, 2 * FF_TILE), BF16),
                        pltpu.VMEM((FF_TILE, D), BF16),
                        pltpu.SemaphoreType.DMA((2,)), pltpu.SemaphoreType.DMA(())],
    )
    return pl.pallas_call(
        functools.partial(_expert_kernel, n_ff_tiles=J),
        grid_spec=grid_spec,
        out_shape=jax.ShapeDtypeStruct((n_rows, Dp), jnp.uint32),
        compiler_params=_cparams(("arbitrary", "arbitrary")),
        name="expert_ffn",
    )(item_e, item_b, item_n,
      xs, w_up, b_up.reshape(E, 1, F2), w_down, b_down.reshape(E, 1, D), jnp.asarray(perm, BF16))


def _combine_kernel(*refs):
    yk_refs, (g4_ref, x1_ref, g_ref), o_ref = refs[:TOP_K], refs[TOP_K:TOP_K + 3], refs[-1]
    y = x1_ref[...]
    g4 = g4_ref[...]
    for k in range(TOP_K):
        y = y + g4[:, k:k + 1] * _unpack_pairs(yk_refs[k][...])
    ms = jnp.mean(y * y, axis=-1, keepdims=True)
    o_ref[...] = y * lax.rsqrt(ms + NORM_EPS) * g_ref[...]


def _combine(yk, g4, x1, g_final, part, n_parts, out_so_far, tc=512):
    T, D = x1.shape
    nt = T // n_parts // tc
    first = part * nt
    slot_specs = [pl.BlockSpec((tc, D // 2), functools.partial(lambda i, k: (k * nt + i, 0), k=k))
                  for k in range(TOP_K)]
    operands = [*([yk] * TOP_K), g4, x1, g_final.reshape(1, D)]
    in_specs = slot_specs + [pl.BlockSpec((tc, 128), lambda i: (first + i, 0)),
                             pl.BlockSpec((tc, D), lambda i: (first + i, 0)),
                             pl.BlockSpec((1, D), lambda i: (0, 0))]
    aliases = {}
    if out_so_far is not None:
        aliases = {len(operands): 0}
        operands.append(out_so_far)
        in_specs.append(pl.BlockSpec(memory_space=pl.ANY))
    return pl.pallas_call(
        _combine_kernel,
        grid=(nt,),
        in_specs=in_specs,
        out_specs=pl.BlockSpec((tc, D), lambda i: (first + i, 0)),
        out_shape=jax.ShapeDtypeStruct((T, D), F32),
        input_output_aliases=aliases,
        compiler_params=_cparams(("parallel",)),
        name="combine_norm",
    )(*operands)


def _work_items(nblk, start_blk, n_items):
    E = nblk.shape[0]
    per_e = (nblk + ITEM_BLOCKS - 1) // ITEM_BLOCKS
    ends = jnp.cumsum(per_e)
    total = ends[-1]
    idx = jnp.arange(n_items, dtype=I32)
    e = jnp.minimum(jnp.searchsorted(ends, idx, side="right"), E - 1).astype(I32)
    local = idx - (ends[e] - per_e[e])
    active = idx < total
    last_e = e[jnp.maximum(total - 1, 0)]
    item_e = jnp.where(active, e, last_e).astype(I32)
    item_b = jnp.where(active, start_blk[e] + local * ITEM_BLOCKS, 0).astype(I32)
    item_n = jnp.where(active, jnp.clip(nblk[e] - local * ITEM_BLOCKS, 0, ITEM_BLOCKS), 0).astype(I32)
    return item_e, item_b, item_n, jnp.maximum(total, 1).astype(I32)


def kernel(x, g_mix, w_in, b_forget, g_v_ln, b_v_ln, w_spatial, b_spatial, w_branch_attn, w_branch_gmlp, w_out, g_ffn, w_router, b_router, w_expert_up, b_expert_up, w_expert_down, b_expert_down, g_final):
    B, S, D = x.shape
    T = B * S
    n_heads = b_forget.shape[0]
    attn_w = n_heads * HEAD_DIM
    gmlp_w = g_v_ln.shape[0]
    E = w_router.shape[1]
    off_f = 3 * attn_w
    off_z = off_f + n_heads
    off_g = off_z + 2 * gmlp_w

    x2 = x.reshape(T, D)
    w_in_t = w_in.T
    wzg = _rows_bf16(w_in_t, off_z, w_in.shape[1] - off_z)
    wa = w_branch_attn.astype(BF16)
    wb = w_branch_gmlp.astype(BF16)
    wo = w_out.astype(BF16)

    qkv, h = _norm_project(x2, g_mix, w_in_t, 3 * attn_w, attn_w, LOG2E / math.sqrt(HEAD_DIM))
    c_row = _forget_cumsum(h, w_in_t, off_f, b_forget, B, S)
    attn = _attention(qkv, c_row, B, S, n_heads)
    sg = _gmlp(h, wzg, g_v_ln, b_v_ln, w_spatial, b_spatial)
    merged = _merge(attn, sg, h, wa, wb, wzg, off_g - off_z)
    x1, h2p, sel, gate = _out_router(merged, x2, wo, g_ffn, w_router, b_router)

    dest4, g4, nblk, start_blk = _routing(sel, gate)
    n_rows = T * TOP_K + E * ROW_BLOCK
    dest_slots = dest4[:, :TOP_K].T.reshape(TOP_K * T)
    max_blocks = n_rows // ROW_BLOCK
    n_items = E + -(-(max_blocks - E) // ITEM_BLOCKS) + 1
    item_e, item_b, item_n, n_active = _work_items(nblk[0], start_blk[0], n_items)
    xs = _sc_scatter_rows(h2p, dest_slots, n_rows, TOP_K)
    ys = _experts(xs, item_e, item_b, item_n, n_active,
                  w_expert_up, b_expert_up, w_expert_down, b_expert_down)
    out = None
    t_part = T // COMBINE_PARTS
    for p in range(COMBINE_PARTS):
        idx = dest4[p * t_part:(p + 1) * t_part, :TOP_K].T.reshape(TOP_K * t_part)
        yk = _sc_gather_rows(ys, idx)
        out = _combine(yk, g4, x1, g_final, p, COMBINE_PARTS, out)
    return out.reshape(B, S, D)
```

```python
import functools
import math

import jax
import jax.numpy as jnp
import numpy as np
from jax import lax
from jax.experimental import pallas as pl
from jax.experimental.pallas import tpu as pltpu
from jax.experimental.pallas import tpu_sc as plsc

F32 = jnp.float32
BF16 = jnp.bfloat16
I32 = jnp.int32

NORM_EPS = 1e-5
LANES = 128
SUBLANES = 8
NT_DIMS = (((1,), (1,)), ((), ()))
HEAD_DIM = 128
CHUNK = 128
GROUP_DIM = 128
TOP_K = 4
SWIGLU_ALPHA = 1.702
SWIGLU_LIMIT = 7.0
LOG2E = math.log2(math.e)

VMEM_LIMIT_BYTES = 56 * 1024 * 1024

ROW_BLOCK = 128
ITEM_BLOCKS = 10
ITEM_ROWS = ITEM_BLOCKS * ROW_BLOCK
REGION_BLOCKS = (8, 4, 2, 1)
assert sum(REGION_BLOCKS) >= ITEM_BLOCKS
COMPUTE_PLANS = {**{n: ((0, n),) for n in range(4, ITEM_BLOCKS + 1)},
                 3: ((0, 2), (2, 1)), 2: ((0, 2),), 1: ((0, 1),)}
assert set(COMPUTE_PLANS) == set(range(1, ITEM_BLOCKS + 1))
FF_TILE = 256
COMBINE_PARTS = 1


def _cparams(sem, **kw):
    return pltpu.CompilerParams(dimension_semantics=sem, vmem_limit_bytes=VMEM_LIMIT_BYTES, **kw)


def _pack_pairs(x):
    c = x.shape[1] // 2
    hi = lax.bitcast_convert_type(x[:, :c].astype(BF16).astype(F32), jnp.uint32)
    lo = lax.bitcast_convert_type(x[:, c:].astype(BF16).astype(F32), jnp.uint32)
    return hi | (lo >> 16)


def _unpack_pairs(w):
    hi = lax.bitcast_convert_type(w & jnp.uint32(0xFFFF0000), F32)
    lo = lax.bitcast_convert_type(w << 16, F32)
    return jnp.concatenate([hi, lo], axis=1)


def _norm_proj_kernel(x_ref, g_ref, w_ref, o_ref, h_ref, *, n_scaled, scale):
    j = pl.program_id(1)

    def project(h):
        acc = lax.dot_general(h, w_ref[...].astype(BF16), NT_DIMS, preferred_element_type=F32)
        o_ref[...] = (acc * jnp.where(j < n_scaled, scale, 1.0)).astype(o_ref.dtype)

    @pl.when(j == 0)
    def _():
        x = x_ref[...]
        ms = jnp.mean(x * x, axis=-1, keepdims=True)
        h = (x * lax.rsqrt(ms + NORM_EPS) * g_ref[...]).astype(h_ref.dtype)
        h_ref[...] = h
        project(h)

    @pl.when(j > 0)
    def _():
        project(h_ref[...])


def _norm_project(x, g, wt, n_cols, n_scaled_cols, scale, tm=1024, tn=1024):
    T, D = x.shape
    return pl.pallas_call(
        functools.partial(_norm_proj_kernel, n_scaled=n_scaled_cols // tn, scale=scale),
        grid=(T // tm, n_cols // tn),
        in_specs=[pl.BlockSpec((tm, D), lambda i, j: (i, 0)),
                  pl.BlockSpec((1, D), lambda i, j: (0, 0)),
                  pl.BlockSpec((tn, D), lambda i, j: (j, 0))],
        out_specs=[pl.BlockSpec((tm, tn), lambda i, j: (i, j)),
                   pl.BlockSpec((tm, D), lambda i, j: (i, 0))],
        out_shape=[jax.ShapeDtypeStruct((T, n_cols), BF16), jax.ShapeDtypeStruct((T, D), BF16)],
        compiler_params=_cparams(("arbitrary", "arbitrary")),
        name="norm_qkv_proj",
    )(x, g.reshape(1, D), wt)


def _cast_kernel(w_ref, o_ref):
    o_ref[...] = w_ref[...].astype(o_ref.dtype)


def _rows_bf16(wt, start, n_rows, tr=1024):
    D = wt.shape[1]
    assert start % SUBLANES == 0 and n_rows % tr == 0
    return pl.pallas_call(
        _cast_kernel,
        grid=(n_rows // tr,),
        in_specs=[pl.BlockSpec((pl.Element(tr), pl.Element(D)),
                               lambda i: (pl.multiple_of(start + i * tr, SUBLANES), 0))],
        out_specs=pl.BlockSpec((tr, D), lambda i: (i, 0)),
        out_shape=jax.ShapeDtypeStruct((n_rows, D), BF16),
        compiler_params=_cparams(("parallel",)),
        name="rows_bf16",
    )(wt)


def _forget_kernel(h_ref, wft_ref, bf_ref, c_ref):
    ft = lax.dot_general(wft_ref[...].astype(BF16), h_ref[...], NT_DIMS,
                         preferred_element_type=F32)
    c = jax.nn.log_sigmoid(ft + bf_ref[...])
    S = c.shape[1]
    lane = lax.broadcasted_iota(I32, c.shape, 1)
    shift = 1
    while shift < S:
        c = c + jnp.where(lane >= shift, pltpu.roll(c, shift, axis=1), 0.0)
        shift *= 2
    c_ref[0] = c * LOG2E


def _forget_cumsum(h, wt, f_off, b_forget, B, S):
    T, D = h.shape
    H = b_forget.shape[0]
    assert f_off % H == 0 and H % SUBLANES == 0
    return pl.pallas_call(
        _forget_kernel,
        grid=(B,),
        in_specs=[pl.BlockSpec((S, D), lambda b: (b, 0)),
                  pl.BlockSpec((H, D), lambda b: (f_off // H, 0)),
                  pl.BlockSpec((H, 1), lambda b: (0, 0))],
        out_specs=pl.BlockSpec((1, H, S), lambda b: (b, 0, 0)),
        out_shape=jax.ShapeDtypeStruct((B, H, S), F32),
        compiler_params=_cparams(("parallel",)),
        name="forget_cumsum",
    )(h, wt, b_forget.reshape(H, 1))


def _attn_kernel(q_ref, k_ref, v_ref, crow_ref, o_ref, vaug_ref, m_ref, acc_ref, *, n_heads, tq):
    i = pl.program_id(1)

    @pl.when(i == 0)
    def _():
        ones = jnp.ones((v_ref.shape[0], HEAD_DIM), BF16)
        for h in range(n_heads):
            vaug_ref[h, :, :HEAD_DIM] = v_ref[:, h * HEAD_DIM:(h + 1) * HEAD_DIM]
            vaug_ref[h, :, HEAD_DIM:] = ones

    m_ref[...] = jnp.full(m_ref.shape, -jnp.inf, F32)
    acc_ref[...] = jnp.zeros(acc_ref.shape, F32)
    row = lax.broadcasted_iota(I32, (tq, tq), 0)
    col = lax.broadcasted_iota(I32, (tq, tq), 1)
    causal = col <= row

    def step(j, masked):
        keys = pl.ds(pl.multiple_of(j * tq, tq), tq)
        for h in range(n_heads):
            hs = slice(h * HEAD_DIM, (h + 1) * HEAD_DIM)
            s = lax.dot_general(q_ref[:, hs], k_ref[keys, hs], (((1,), (1,)), ((), ())),
                                preferred_element_type=F32) - crow_ref[0, h, j]
            if masked:
                s = jnp.where(causal, s, -jnp.inf)
            m_old = m_ref[h]
            m_new = jnp.maximum(m_old, jnp.max(s, axis=-1, keepdims=True))
            alpha = jnp.exp2(m_old - m_new)
            p = jnp.exp2(s - jnp.concatenate([m_new] * (tq // HEAD_DIM), axis=1))
            m_ref[h] = m_new
            pv = jnp.dot(p.astype(BF16), vaug_ref[h, keys, :], preferred_element_type=F32)
            acc_ref[h] = jnp.concatenate([alpha, alpha], axis=1) * acc_ref[h] + pv

    def body(j, _):
        step(j, False)
        return 0

    lax.fori_loop(0, i, body, 0)
    step(i, True)
    for h in range(n_heads):
        acc = acc_ref[h]
        o_ref[:, h * HEAD_DIM:(h + 1) * HEAD_DIM] = (acc[:, :HEAD_DIM] / acc[:, HEAD_DIM:]).astype(o_ref.dtype)


def _attention(qkv, c_row, B, S, n_heads, tq=512):
    T = qkv.shape[0]
    W = n_heads * HEAD_DIM
    nq = S // tq
    c_row5 = c_row.reshape(B, n_heads, nq, 1, tq)
    return pl.pallas_call(
        functools.partial(_attn_kernel, n_heads=n_heads, tq=tq),
        grid=(B, nq),
        in_specs=[pl.BlockSpec((tq, W), lambda b, i: (b * nq + i, 0)),
                  pl.BlockSpec((S, W), lambda b, i: (b, 1)),
                  pl.BlockSpec((S, W), lambda b, i: (b, 2)),
                  pl.BlockSpec((1, n_heads, nq, 1, tq), lambda b, i: (b, 0, 0, 0, 0))],
        out_specs=pl.BlockSpec((tq, W), lambda b, i: (b * nq + i, 0)),
        out_shape=jax.ShapeDtypeStruct((T, W), BF16),
        scratch_shapes=[pltpu.VMEM((n_heads, S, 2 * HEAD_DIM), BF16),
                        pltpu.VMEM((n_heads, tq, HEAD_DIM), F32),
                        pltpu.VMEM((n_heads, tq, 2 * HEAD_DIM), F32)],
        compiler_params=_cparams(("arbitrary", "arbitrary")),
        name="fox_attention",
    )(qkv, qkv, qkv, c_row5)


def _gmlp_kernel(h_ref, wz_ref, g_ref, b_ref, ws_ref, bst_ref, o_ref, *, n_groups):
    z = lax.dot_general(h_ref[...], wz_ref[...], NT_DIMS, preferred_element_type=F32)
    z = 0.5 * z * (1.0 + lax.erf(z * (1.0 / math.sqrt(2.0))))
    W = z.shape[1] // 2
    u = z[:, :W]
    v = z[:, W:]
    mu = jnp.mean(v, axis=-1, keepdims=True)
    var = jnp.mean(jnp.square(v - mu), axis=-1, keepdims=True)
    vn = (v - mu) * lax.rsqrt(var + NORM_EPS) * g_ref[...] + b_ref[...]
    row = lax.broadcasted_iota(I32, (CHUNK, CHUNK), 0)
    col = lax.broadcasted_iota(I32, (CHUNK, CHUNK), 1)
    tril = col <= row
    tg = z.shape[0]
    for g in range(n_groups):
        gs = slice(g * GROUP_DIM, (g + 1) * GROUP_DIM)
        wg = jnp.where(tril, ws_ref[g], 0.0).astype(BF16)
        bias = bst_ref[:, g:g + 1]
        for c in range(tg // CHUNK):
            cs = slice(c * CHUNK, (c + 1) * CHUNK)
            mixed = jnp.dot(wg, vn[cs, gs].astype(BF16), preferred_element_type=F32) + bias
            o_ref[cs, gs] = (u[cs, gs] * mixed).astype(o_ref.dtype)


def _gmlp(h, wz, g_v_ln, b_v_ln, w_spatial, b_spatial, tg=1024):
    T, D = h.shape
    W = g_v_ln.shape[0]
    W2 = 2 * W
    G = w_spatial.shape[0]
    return pl.pallas_call(
        functools.partial(_gmlp_kernel, n_groups=G),
        grid=(T // tg,),
        in_specs=[pl.BlockSpec((tg, D), lambda i: (i, 0)),
                  pl.BlockSpec((W2, D), lambda i: (0, 0)),
                  pl.BlockSpec((1, W), lambda i: (0, 0)),
                  pl.BlockSpec((1, W), lambda i: (0, 0)),
                  pl.BlockSpec((G, CHUNK, CHUNK), lambda i: (0, 0, 0)),
                  pl.BlockSpec((CHUNK, G), lambda i: (0, 0))],
        out_specs=pl.BlockSpec((tg, W), lambda i: (i, 0)),
        out_shape=jax.ShapeDtypeStruct((T, W), BF16),
        compiler_params=_cparams(("parallel",)),
        name="gmlp",
    )(h, wz, g_v_ln.reshape(1, W), b_v_ln.reshape(1, W), w_spatial, b_spatial.T)


def _merge_kernel(attn_ref, sg_ref, h_ref, wa_ref, wb_ref, wga_ref, wgb_ref, o_ref,
                  wa_b, wb_b, wga_b, wgb_b):
    @pl.when(pl.program_id(1) == 0)
    def _():
        wa_b[...] = wa_ref[...].astype(BF16)
        wb_b[...] = wb_ref[...].astype(BF16)
        wga_b[...] = wga_ref[...].astype(BF16)
        wgb_b[...] = wgb_ref[...].astype(BF16)

    h = h_ref[...]
    a = jnp.dot(attn_ref[...], wa_b[...], preferred_element_type=F32)
    ga = lax.dot_general(h, wga_b[...], NT_DIMS, preferred_element_type=F32)
    m = jax.nn.sigmoid(ga) * a
    b = jnp.dot(sg_ref[...], wb_b[...], preferred_element_type=F32)
    gb = lax.dot_general(h, wgb_b[...], NT_DIMS, preferred_element_type=F32)
    o_ref[...] = (m + jax.nn.sigmoid(gb) * b).astype(o_ref.dtype)


def _merge(attn, sg, h, wa, wb, wt, g_off, tm=1024, tn=512):
    T, D = h.shape
    Wa = attn.shape[1]
    Wb = sg.shape[1]
    nt = D // tn
    assert g_off % SUBLANES == 0

    def gate_rows(first):
        return pl.BlockSpec((pl.Element(tn), pl.Element(D)),
                            lambda j, i: (pl.multiple_of(g_off + (first + j) * tn, SUBLANES), 0))

    return pl.pallas_call(
        _merge_kernel,
        grid=(nt, T // tm),
        in_specs=[pl.BlockSpec((tm, Wa), lambda j, i: (i, 0)),
                  pl.BlockSpec((tm, Wb), lambda j, i: (i, 0)),
                  pl.BlockSpec((tm, D), lambda j, i: (i, 0)),
                  pl.BlockSpec((Wa, tn), lambda j, i: (0, j)),
                  pl.BlockSpec((Wb, tn), lambda j, i: (0, j)),
                  gate_rows(0), gate_rows(nt)],
        out_specs=pl.BlockSpec((tm, tn), lambda j, i: (i, j)),
        out_shape=jax.ShapeDtypeStruct((T, D), BF16),
        scratch_shapes=[pltpu.VMEM((Wa, tn), BF16), pltpu.VMEM((Wb, tn), BF16),
                        pltpu.VMEM((tn, D), BF16), pltpu.VMEM((tn, D), BF16)],
        compiler_params=_cparams(("arbitrary", "arbitrary")),
        name="gated_merge",
    )(attn, sg, h, wa, wb, wt, wt)


def _out_router_kernel(m_ref, x_ref, wo_ref, g_ref, wr_ref, br_ref,
                       x1_ref, h2_ref, sel_ref, gate_ref, *, n_parts):
    E = br_ref.shape[1]
    part = m_ref.shape[0] // n_parts
    for p in range(n_parts):
        rows = pl.ds(p * part, part)
        x1 = x_ref[rows, :] + jnp.dot(m_ref[rows, :], wo_ref[...], preferred_element_type=F32)
        x1_ref[rows, :] = x1
        ms = jnp.mean(x1 * x1, axis=-1, keepdims=True)
        h2 = x1 * lax.rsqrt(ms + NORM_EPS) * g_ref[...]
        h2_ref[rows, :] = _pack_pairs(h2)
        h2_hi = h2.astype(BF16)
        h2_lo = (h2 - h2_hi.astype(F32)).astype(BF16)
        pa = jnp.dot(h2_hi, wr_ref[...], preferred_element_type=F32)
        pb = jnp.dot(h2_lo, wr_ref[:, :E], preferred_element_type=F32)
        logits = pa[:, :E] + (pa[:, E:] + pb) + br_ref[...]
        lt = jnp.concatenate([logits, jnp.full((part, LANES - E), -jnp.inf, F32)], axis=1).T[:E]
        expert = lax.broadcasted_iota(I32, lt.shape, 0)
        beaten = jnp.zeros(lt.shape, F32)
        for e in range(E):
            le = lt[e:e + 1, :]
            beaten = beaten + jnp.where((le > lt) | ((le == lt) & (expert > e)), 1.0, 0.0)
        chosen = beaten < TOP_K
        top = jnp.max(jnp.where(chosen, lt, -jnp.inf), axis=0, keepdims=True)
        ex = jnp.where(chosen, jnp.exp(lt - top), 0.0)
        gate_t = ex / jnp.sum(ex, axis=0, keepdims=True)
        zeros = jnp.zeros((LANES - E, part), F32)
        sel_ref[rows, :] = jnp.concatenate([jnp.where(chosen, 1.0, 0.0), zeros], axis=0).T[:, :E]
        gate_ref[rows, :] = jnp.concatenate([gate_t, zeros], axis=0).T[:, :E]


def _out_router(merged, x, wo, g_ffn, w_router, b_router, to=512, n_parts=1):
    T, D = x.shape
    E = w_router.shape[1]
    w_hi = w_router.astype(BF16)
    w_lo = (w_router - w_hi.astype(F32)).astype(BF16)
    w_router = jnp.concatenate([w_hi, w_lo], axis=1)
    row = lambda i: (i, 0)
    fixed = lambda i: (0, 0)
    return pl.pallas_call(
        functools.partial(_out_router_kernel, n_parts=n_parts),
        grid=(T // to,),
        in_specs=[pl.BlockSpec((to, D), row), pl.BlockSpec((to, D), row),
                  pl.BlockSpec((D, D), fixed), pl.BlockSpec((1, D), fixed),
                  pl.BlockSpec((D, 2 * E), fixed), pl.BlockSpec((1, E), fixed)],
        out_specs=[pl.BlockSpec((to, D), row), pl.BlockSpec((to, D // 2), row),
                   pl.BlockSpec((to, E), row), pl.BlockSpec((to, E), row)],
        out_shape=[jax.ShapeDtypeStruct((T, D), F32), jax.ShapeDtypeStruct((T, D // 2), jnp.uint32),
                   jax.ShapeDtypeStruct((T, E), F32), jax.ShapeDtypeStruct((T, E), F32)],
        compiler_params=_cparams(("parallel",)),
        name="out_router",
    )(merged, x, wo, g_ffn.reshape(1, D), w_router, b_router.reshape(1, E))


def _routing_kernel(sel_ref, gate_ref, dest_ref, g4_ref, nblk_ref, start_ref, rank_ref, *, tile):
    T, E = sel_ref.shape
    nt = T // tile
    r = lax.broadcasted_iota(I32, (tile, tile), 0)
    c = lax.broadcasted_iota(I32, (tile, tile), 1)
    strict_lower = (c < r).astype(BF16)
    er = lax.broadcasted_iota(I32, (E, E), 0)
    ec = lax.broadcasted_iota(I32, (E, E), 1)
    strict_upper = (er < ec).astype(BF16)

    def pass1(t, carry):
        rows = pl.ds(pl.multiple_of(t * tile, tile), tile)
        a = sel_ref[rows, :]
        rank_ref[rows, :] = jnp.dot(strict_lower, a.astype(BF16), preferred_element_type=F32) + carry
        return carry + jnp.sum(a, axis=0, keepdims=True)

    counts = lax.fori_loop(0, nt, pass1, jnp.zeros((1, E), F32))
    nblk = jnp.floor((counts + (ROW_BLOCK - 1)) * (1.0 / ROW_BLOCK))
    start_blk = jnp.dot(nblk.astype(BF16), strict_upper, preferred_element_type=F32)
    nblk_ref[...] = nblk.astype(I32)
    start_ref[...] = start_blk.astype(I32)
    start_row = start_blk * float(ROW_BLOCK)
    lane = lax.broadcasted_iota(I32, (tile, 128), 1)

    def pass2(t, _):
        rows = pl.ds(pl.multiple_of(t * tile, tile), tile)
        a = sel_ref[rows, :]
        g = gate_ref[rows, :]
        dest_e = rank_ref[rows, :] + start_row
        slot = jnp.dot(a.astype(BF16), strict_upper, preferred_element_type=F32)
        d4 = jnp.zeros((tile, 128), F32)
        g4 = jnp.zeros((tile, 128), F32)
        for s in range(TOP_K):
            pick = (a > 0.5) & (slot == float(s))
            d4 = jnp.where(lane == s, jnp.sum(jnp.where(pick, dest_e, 0.0), axis=-1, keepdims=True), d4)
            g4 = jnp.where(lane == s, jnp.sum(jnp.where(pick, g, 0.0), axis=-1, keepdims=True), g4)
        dest_ref[rows, :] = d4.astype(I32)
        g4_ref[rows, :] = g4
        return 0

    lax.fori_loop(0, nt, pass2, 0)


def _routing(sel, gate, tile=512):
    T, E = sel.shape
    return pl.pallas_call(
        functools.partial(_routing_kernel, tile=tile),
        out_shape=[jax.ShapeDtypeStruct((T, 128), I32), jax.ShapeDtypeStruct((T, 128), F32),
                   jax.ShapeDtypeStruct((1, E), I32), jax.ShapeDtypeStruct((1, E), I32)],
        scratch_shapes=[pltpu.VMEM((T, E), F32)],
        compiler_params=pltpu.CompilerParams(vmem_limit_bytes=VMEM_LIMIT_BYTES),
        name="routing_ranks",
    )(sel, gate)


SC_CORES = 2
SC_SUBCORES = 16
SC_CHUNK = 64


def _sc_gather_rows(table, idx):
    n = idx.shape[0]
    W = table.shape[1]
    n_workers = SC_CORES * SC_SUBCORES
    per_worker = n // n_workers
    assert per_worker * n_workers == n and per_worker % SC_CHUNK == 0
    mesh = plsc.VectorSubcoreMesh(core_axis_name="c", subcore_axis_name="s",
                                  num_cores=SC_CORES, num_subcores=SC_SUBCORES)

    @functools.partial(
        pl.kernel, mesh=mesh,
        out_type=jax.ShapeDtypeStruct((n, W), table.dtype),
        scratch_types=[pltpu.VMEM((SC_CHUNK,), I32), pltpu.VMEM((SC_CHUNK, W), table.dtype),
                       pltpu.SemaphoreType.DMA],
        name="sc_gather_rows",
    )
    def gather(table_hbm, idx_hbm, out_hbm, idx_v, rows_v, sem):
        wid = lax.axis_index("s") * SC_CORES + lax.axis_index("c")
        base = wid * per_worker

        @pl.loop(0, per_worker // SC_CHUNK)
        def _(c):
            off = pl.multiple_of(base + c * SC_CHUNK, 8)
            pltpu.sync_copy(idx_hbm.at[pl.ds(off, SC_CHUNK)], idx_v)
            pltpu.async_copy(table_hbm.at[idx_v], rows_v, sem).wait()
            pltpu.sync_copy(rows_v, out_hbm.at[pl.ds(off, SC_CHUNK)])

    return gather(table, idx)


def _sc_scatter_rows(rows, idx, n_out, n_slots):
    T, W = rows.shape
    n_workers = SC_CORES * SC_SUBCORES
    per_worker = T // n_workers
    assert per_worker * n_workers == T and per_worker % SC_CHUNK == 0
    mesh = plsc.VectorSubcoreMesh(core_axis_name="c", subcore_axis_name="s",
                                  num_cores=SC_CORES, num_subcores=SC_SUBCORES)

    @functools.partial(
        pl.kernel, mesh=mesh,
        out_type=jax.ShapeDtypeStruct((n_out, W), rows.dtype),
        scratch_types=[pltpu.VMEM((SC_CHUNK,), I32), pltpu.VMEM((SC_CHUNK, W), rows.dtype)],
        name="sc_scatter_rows",
    )
    def scatter(rows_hbm, idx_hbm, out_hbm, idx_v, rows_v):
        wid = lax.axis_index("s") * SC_CORES + lax.axis_index("c")
        base = wid * per_worker

        @pl.loop(0, per_worker // SC_CHUNK)
        def _(c):
            off = pl.multiple_of(base + c * SC_CHUNK, 8)
            pltpu.sync_copy(rows_hbm.at[pl.ds(off, SC_CHUNK)], rows_v)
            for k in range(n_slots):
                pltpu.sync_copy(idx_hbm.at[pl.ds(pl.multiple_of(k * T + off, 8), SC_CHUNK)], idx_v)
                pltpu.sync_copy(rows_v, out_hbm.at[idx_v])

    return scatter(rows, idx)


def _expert_kernel(ie_ref, ib_ref, ins_ref,
                   xs_hbm, wup_ref, bup_ref, wdn_ref, bdn_ref, perm_ref, ys_hbm,
                   xg_ref, acc_ref, yst_ref, wupb_ref, wdnb_ref, gsem, osem, *, n_ff_tiles):
    i = pl.program_id(0)
    j = pl.program_id(1)
    n_items = pl.num_programs(0)
    nsub = ins_ref[i]
    slot = i % 2

    def for_regions(n_blocks, fn):
        first = jnp.int32(0)
        for count in REGION_BLOCKS:
            present = (n_blocks & count) != 0
            pl.when(present)(functools.partial(fn, first, count))
            first = first + jnp.where(present, count, 0)

    def rows_of(first, count):
        return pl.ds(pl.multiple_of(first * ROW_BLOCK, ROW_BLOCK), count * ROW_BLOCK)

    def in_copy(item, dst_slot, first, count):
        src = pl.multiple_of((ib_ref[item] + first) * ROW_BLOCK, ROW_BLOCK)
        return pltpu.make_async_copy(xs_hbm.at[pl.ds(src, count * ROW_BLOCK)],
                                     xg_ref.at[dst_slot, rows_of(first, count)], gsem.at[dst_slot])

    def fetch_item(item, n_blocks, dst_slot):
        for_regions(n_blocks, lambda first, count: in_copy(item, dst_slot, first, count).start())

    @pl.when(j == 0)
    def _():
        @pl.when(i == 0)
        def _():
            fetch_item(0, nsub, 0)
            acc_ref[...] = jnp.zeros(acc_ref.shape, F32)

        nxt = jnp.minimum(i + 1, n_items - 1)
        fetch_item(nxt, jnp.where(i + 1 < n_items, ins_ref[nxt], 0), 1 - slot)
        for_regions(nsub, lambda first, count: in_copy(i, slot, first, count).wait())

    @pl.when(nsub > 0)
    def _():
        bup = bup_ref[0]
        perm = perm_ref[...]
        half = perm.shape[0] // 2

        def run_blocks(first, count):
            if first == 0:
                wup = wup_ref[0].astype(BF16)
                wdn = wdn_ref[0].astype(BF16)
                wupb_ref[...] = wup
                wdnb_ref[...] = wdn
            else:
                wup = wupb_ref[...]
                wdn = wdnb_ref[...]
            rows = pl.ds(first * ROW_BLOCK, count * ROW_BLOCK)
            xb = _unpack_pairs(xg_ref[slot, rows, :]).astype(BF16)
            gu = (jnp.dot(xb, wup, preferred_element_type=F32) + bup).astype(BF16)
            glu_parts, lin_parts = [], []
            for p in range(gu.shape[1] // perm.shape[0]):
                gp = jnp.dot(gu[:, p * perm.shape[0]:(p + 1) * perm.shape[0]], perm,
                             preferred_element_type=F32)
                glu_parts.append(gp[:, :half])
                lin_parts.append(gp[:, half:])
            x_glu = jnp.minimum(jnp.concatenate(glu_parts, axis=1), SWIGLU_LIMIT)
            x_lin = jnp.clip(jnp.concatenate(lin_parts, axis=1), -SWIGLU_LIMIT, SWIGLU_LIMIT)
            act = x_glu * jax.nn.sigmoid(SWIGLU_ALPHA * x_glu) * (x_lin + 1.0)
            start = jnp.where(j == 0, bdn_ref[0], acc_ref[rows, :])
            acc_ref[rows, :] = start + jnp.dot(act.astype(BF16), wdn, preferred_element_type=F32)

        for n_blocks, plan in COMPUTE_PLANS.items():
            for first, count in plan:
                shared = [n for n, p in COMPUTE_PLANS.items() if (first, count) in p]
                if n_blocks == shared[0]:
                    cond = functools.reduce(jnp.logical_or, [nsub == n for n in shared])
                    pl.when(cond)(functools.partial(run_blocks, first, count))

    def out_copy(item, first, count):
        dst = pl.multiple_of((ib_ref[item] + first) * ROW_BLOCK, ROW_BLOCK)
        return pltpu.make_async_copy(yst_ref.at[rows_of(first, count)],
                                     ys_hbm.at[pl.ds(dst, count * ROW_BLOCK)], osem)

    @pl.when(j == n_ff_tiles - 1)
    def _():
        prev = jnp.maximum(i - 1, 0)
        for_regions(jnp.where(i > 0, ins_ref[prev], 0),
                    lambda first, count: out_copy(prev, first, count).wait())

        def leave(first, count):
            rows = rows_of(first, count)
            yst_ref[rows, :] = _pack_pairs(acc_ref[rows, :])
            out_copy(i, first, count).start()

        for_regions(nsub, leave)

        @pl.when(i == n_items - 1)
        def _():
            for_regions(nsub, lambda first, count: out_copy(i, first, count).wait())


def _experts(xs, item_e, item_b, item_n, n_active, w_up, b_up, w_down, b_down):
    n_rows, Dp = xs.shape
    D = 2 * Dp
    E, _, F2 = w_up.shape
    F = F2 // 2
    J = F // FF_TILE
    half = 128
    perm = np.zeros((2 * half, 2 * half), np.float32)
    perm[2 * np.arange(half), np.arange(half)] = 1.0
    perm[2 * np.arange(half) + 1, half + np.arange(half)] = 1.0

    def jj(i, j, ins):
        return jnp.where(ins[i] > 0, j, J - 1)

    grid_spec = pltpu.PrefetchScalarGridSpec(
        num_scalar_prefetch=3,
        grid=(n_active, J),
        in_specs=[pl.BlockSpec(memory_space=pl.ANY),
                  pl.BlockSpec((1, D, 2 * FF_TILE), lambda i, j, ie, ib, ins: (ie[i], 0, jj(i, j, ins))),
                  pl.BlockSpec((1, 1, 2 * FF_TILE), lambda i, j, ie, ib, ins: (ie[i], 0, jj(i, j, ins))),
                  pl.BlockSpec((1, FF_TILE, D), lambda i, j, ie, ib, ins: (ie[i], jj(i, j, ins), 0)),
                  pl.BlockSpec((1, 1, D), lambda i, j, ie, ib, ins: (ie[i], 0, 0)),
                  pl.BlockSpec((2 * half, 2 * half), lambda i, j, ie, ib, ins: (0, 0))],
        out_specs=pl.BlockSpec(memory_space=pl.ANY),
        scratch_shapes=[pltpu.VMEM((2, ITEM_ROWS, Dp), jnp.uint32),
                        pltpu.VMEM((ITEM_ROWS, D), F32),
                        pltpu.VMEM((ITEM_ROWS, Dp), jnp.uint32),
                        pltpu.VMEM((D, 2 * FF_TILE), BF16),
                        pltpu.VMEM((FF_TILE, D), BF16),
                        pltpu.SemaphoreType.DMA((2,)), pltpu.SemaphoreType.DMA(())],
    )
    return pl.pallas_call(
        functools.partial(_expert_kernel, n_ff_tiles=J),
        grid_spec=grid_spec,
        out_shape=jax.ShapeDtypeStruct((n_rows, Dp), jnp.uint32),
        compiler_params=_cparams(("arbitrary", "arbitrary")),
        name="expert_ffn",
    )(item_e, item_b, item_n,
      xs, w_up, b_up.reshape(E, 1, F2), w_down, b_down.reshape(E, 1, D), jnp.asarray(perm, BF16))


def _combine_kernel(*refs):
    yk_refs, (g4_ref, x1_ref, g_ref), o_ref = refs[:TOP_K], refs[TOP_K:TOP_K + 3], refs[-1]
    y = x1_ref[...]
    g4 = g4_ref[...]
    for k in range(TOP_K):
        y = y + g4[:, k:k + 1] * _unpack_pairs(yk_refs[k][...])
    ms = jnp.mean(y * y, axis=-1, keepdims=True)
    o_ref[...] = y * lax.rsqrt(ms + NORM_EPS) * g_ref[...]


def _combine(yk, g4, x1, g_final, part, n_parts, out_so_far, tc=512):
    T, D = x1.shape
    nt = T // n_parts // tc
    first = part * nt
    slot_specs = [pl.BlockSpec((tc, D // 2), functools.partial(lambda i, k: (k * nt + i, 0), k=k))
                  for k in range(TOP_K)]
    operands = [*([yk] * TOP_K), g4, x1, g_final.reshape(1, D)]
    in_specs = slot_specs + [pl.BlockSpec((tc, 128), lambda i: (first + i, 0)),
                             pl.BlockSpec((tc, D), lambda i: (first + i, 0)),
                             pl.BlockSpec((1, D), lambda i: (0, 0))]
    aliases = {}
    if out_so_far is not None:
        aliases = {len(operands): 0}
        operands.append(out_so_far)
        in_specs.append(pl.BlockSpec(memory_space=pl.ANY))
    return pl.pallas_call(
        _combine_kernel,
        grid=(nt,),
        in_specs=in_specs,
        out_specs=pl.BlockSpec((tc, D), lambda i: (first + i, 0)),
        out_shape=jax.ShapeDtypeStruct((T, D), F32),
        input_output_aliases=aliases,
        compiler_params=_cparams(("parallel",)),
        name="combine_norm",
    )(*operands)


def _work_items(nblk, start_blk, n_items):
    E = nblk.shape[0]
    per_e = (nblk + ITEM_BLOCKS - 1) // ITEM_BLOCKS
    ends = jnp.cumsum(per_e)
    total = ends[-1]
    idx = jnp.arange(n_items, dtype=I32)
    e = jnp.minimum(jnp.searchsorted(ends, idx, side="right"), E - 1).astype(I32)
    local = idx - (ends[e] - per_e[e])
    active = idx < total
    last_e = e[jnp.maximum(total - 1, 0)]
    item_e = jnp.where(active, e, last_e).astype(I32)
    item_b = jnp.where(active, start_blk[e] + local * ITEM_BLOCKS, 0).astype(I32)
    item_n = jnp.where(active, jnp.clip(nblk[e] - local * ITEM_BLOCKS, 0, ITEM_BLOCKS), 0).astype(I32)
    return item_e, item_b, item_n, jnp.maximum(total, 1).astype(I32)


def kernel(x, g_mix, w_in, b_forget, g_v_ln, b_v_ln, w_spatial, b_spatial, w_branch_attn, w_branch_gmlp, w_out, g_ffn, w_router, b_router, w_expert_up, b_expert_up, w_expert_down, b_expert_down, g_final):
    B, S, D = x.shape
    T = B * S
    n_heads = b_forget.shape[0]
    attn_w = n_heads * HEAD_DIM
    gmlp_w = g_v_ln.shape[0]
    E = w_router.shape[1]
    off_f = 3 * attn_w
    off_z = off_f + n_heads
    off_g = off_z + 2 * gmlp_w

    x2 = x.reshape(T, D)
    w_in_t = w_in.T
    wz = _rows_bf16(w_in_t, off_z, off_g - off_z)
    wo = w_out.astype(BF16)

    qkv, h = _norm_project(x2, g_mix, w_in_t, 3 * attn_w, attn_w, LOG2E / math.sqrt(HEAD_DIM))
    c_row = _forget_cumsum(h, w_in_t, off_f, b_forget, B, S)
    attn = _attention(qkv, c_row, B, S, n_heads)
    sg = _gmlp(h, wz, g_v_ln, b_v_ln, w_spatial, b_spatial)
    merged = _merge(attn, sg, h, w_branch_attn, w_branch_gmlp, w_in_t, off_g)
    x1, h2p, sel, gate = _out_router(merged, x2, wo, g_ffn, w_router, b_router)

    dest4, g4, nblk, start_blk = _routing(sel, gate)
    n_rows = T * TOP_K + E * ROW_BLOCK
    dest_slots = dest4[:, :TOP_K].T.reshape(TOP_K * T)
    max_blocks = n_rows // ROW_BLOCK
    n_items = E + -(-(max_blocks - E) // ITEM_BLOCKS) + 1
    item_e, item_b, item_n, n_active = _work_items(nblk[0], start_blk[0], n_items)
    xs = _sc_scatter_rows(h2p, dest_slots, n_rows, TOP_K)
    ys = _experts(xs, item_e, item_b, item_n, n_active,
                  w_expert_up, b_expert_up, w_expert_down, b_expert_down)
    out = None
    t_part = T // COMBINE_PARTS
    for p in range(COMBINE_PARTS):
        idx = dest4[p * t_part:(p + 1) * t_part, :TOP_K].T.reshape(TOP_K * t_part)
        yk = _sc_gather_rows(ys, idx)
        out = _combine(yk, g4, x1, g_final, p, COMBINE_PARTS, out)
    return out.reshape(B, S, D)
```

```python
import functools
import math

import jax
import jax.numpy as jnp
import numpy as np
from jax import lax
from jax.experimental import pallas as pl
from jax.experimental.pallas import tpu as pltpu
from jax.experimental.pallas import tpu_sc as plsc

F32 = jnp.float32
BF16 = jnp.bfloat16
I32 = jnp.int32

NORM_EPS = 1e-5
LANES = 128
SUBLANES = 8
NT_DIMS = (((1,), (1,)), ((), ()))
HEAD_DIM = 128
CHUNK = 128
GROUP_DIM = 128
TOP_K = 4
SWIGLU_ALPHA = 1.702
SWIGLU_LIMIT = 7.0
LOG2E = math.log2(math.e)

VMEM_LIMIT_BYTES = 56 * 1024 * 1024

ROW_BLOCK = 128
ITEM_BLOCKS = 10
ITEM_ROWS = ITEM_BLOCKS * ROW_BLOCK
REGION_BLOCKS = (8, 4, 2, 1)
assert sum(REGION_BLOCKS) >= ITEM_BLOCKS
COMPUTE_PLANS = {**{n: ((0, n),) for n in range(4, ITEM_BLOCKS + 1)},
                 3: ((0, 2), (2, 1)), 2: ((0, 2),), 1: ((0, 1),)}
assert set(COMPUTE_PLANS) == set(range(1, ITEM_BLOCKS + 1))
FF_TILE = 256
COMBINE_PARTS = 1


def _cparams(sem, **kw):
    return pltpu.CompilerParams(dimension_semantics=sem, vmem_limit_bytes=VMEM_LIMIT_BYTES, **kw)


def _pack_pairs(x):
    c = x.shape[1] // 2
    hi = lax.bitcast_convert_type(x[:, :c].astype(BF16).astype(F32), jnp.uint32)
    lo = lax.bitcast_convert_type(x[:, c:].astype(BF16).astype(F32), jnp.uint32)
    return hi | (lo >> 16)


def _unpack_pairs(w):
    hi = lax.bitcast_convert_type(w & jnp.uint32(0xFFFF0000), F32)
    lo = lax.bitcast_convert_type(w << 16, F32)
    return jnp.concatenate([hi, lo], axis=1)


def _norm_proj_kernel(x_ref, g_ref, w_ref, wft_ref, bf_ref, o_ref, h_ref, c_ref, carry_ref,
                      *, n_scaled, scale, tiles_per_seq):
    i = pl.program_id(0)
    j = pl.program_id(1)

    def project(h):
        acc = lax.dot_general(h, w_ref[...].astype(BF16), NT_DIMS, preferred_element_type=F32)
        o_ref[...] = (acc * jnp.where(j < n_scaled, scale, 1.0)).astype(o_ref.dtype)

    @pl.when((i == 0) & (j == 0))
    def _():
        carry_ref[...] = jnp.zeros(carry_ref.shape, F32)

    @pl.when(j == 0)
    def _():
        x = x_ref[...]
        ms = jnp.mean(x * x, axis=-1, keepdims=True)
        h = (x * lax.rsqrt(ms + NORM_EPS) * g_ref[...]).astype(h_ref.dtype)
        h_ref[...] = h
        ft = lax.dot_general(wft_ref[...].astype(BF16), h, NT_DIMS, preferred_element_type=F32)
        c = jax.nn.log_sigmoid(ft + bf_ref[...])
        tm = c.shape[1]
        lane = lax.broadcasted_iota(I32, c.shape, 1)
        shift = 1
        while shift < tm:
            c = c + jnp.where(lane >= shift, pltpu.roll(c, shift, axis=1), 0.0)
            shift *= 2
        c = c + jnp.where(i % tiles_per_seq == 0, 0.0, carry_ref[:, :1])
        carry_ref[...] = jnp.broadcast_to(c[:, tm - 1:tm], carry_ref.shape)
        c_ref[0] = c * LOG2E
        project(h)

    @pl.when(j > 0)
    def _():
        project(h_ref[...])


def _norm_project(x, g, wt, n_cols, n_scaled_cols, scale, f_off, b_forget, B, S, tm=1024, tn=1024):
    T, D = x.shape
    H = b_forget.shape[0]
    tiles_per_seq = S // tm
    assert f_off % H == 0 and H % SUBLANES == 0 and tiles_per_seq * tm == S
    return pl.pallas_call(
        functools.partial(_norm_proj_kernel, n_scaled=n_scaled_cols // tn, scale=scale,
                          tiles_per_seq=tiles_per_seq),
        grid=(T // tm, n_cols // tn),
        in_specs=[pl.BlockSpec((tm, D), lambda i, j: (i, 0)),
                  pl.BlockSpec((1, D), lambda i, j: (0, 0)),
                  pl.BlockSpec((tn, D), lambda i, j: (j, 0)),
                  pl.BlockSpec((H, D), lambda i, j: (f_off // H, 0)),
                  pl.BlockSpec((H, 1), lambda i, j: (0, 0))],
        out_specs=[pl.BlockSpec((tm, tn), lambda i, j: (i, j)),
                   pl.BlockSpec((tm, D), lambda i, j: (i, 0)),
                   pl.BlockSpec((1, H, tm), lambda i, j: (i // tiles_per_seq, 0, i % tiles_per_seq))],
        out_shape=[jax.ShapeDtypeStruct((T, n_cols), BF16), jax.ShapeDtypeStruct((T, D), BF16),
                   jax.ShapeDtypeStruct((B, H, S), F32)],
        scratch_shapes=[pltpu.VMEM((H, LANES), F32)],
        compiler_params=_cparams(("arbitrary", "arbitrary")),
        name="norm_qkv_proj",
    )(x, g.reshape(1, D), wt, wt, b_forget.reshape(H, 1))


def _cast_kernel(w_ref, o_ref):
    o_ref[...] = w_ref[...].astype(o_ref.dtype)


def _rows_bf16(wt, start, n_rows, tr=1024):
    D = wt.shape[1]
    assert start % SUBLANES == 0 and n_rows % tr == 0
    return pl.pallas_call(
        _cast_kernel,
        grid=(n_rows // tr,),
        in_specs=[pl.BlockSpec((pl.Element(tr), pl.Element(D)),
                               lambda i: (pl.multiple_of(start + i * tr, SUBLANES), 0))],
        out_specs=pl.BlockSpec((tr, D), lambda i: (i, 0)),
        out_shape=jax.ShapeDtypeStruct((n_rows, D), BF16),
        compiler_params=_cparams(("parallel",)),
        name="rows_bf16",
    )(wt)


def _attn_kernel(q_ref, k_ref, v_ref, crow_ref, o_ref, vaug_ref, m_ref, acc_ref, *, n_heads, tq):
    i = pl.program_id(1)

    @pl.when(i == 0)
    def _():
        ones = jnp.ones((v_ref.shape[0], HEAD_DIM), BF16)
        for h in range(n_heads):
            vaug_ref[h, :, :HEAD_DIM] = v_ref[:, h * HEAD_DIM:(h + 1) * HEAD_DIM]
            vaug_ref[h, :, HEAD_DIM:] = ones

    m_ref[...] = jnp.full(m_ref.shape, -jnp.inf, F32)
    acc_ref[...] = jnp.zeros(acc_ref.shape, F32)
    row = lax.broadcasted_iota(I32, (tq, tq), 0)
    col = lax.broadcasted_iota(I32, (tq, tq), 1)
    causal = col <= row

    def step(j, masked):
        keys = pl.ds(pl.multiple_of(j * tq, tq), tq)
        for h in range(n_heads):
            hs = slice(h * HEAD_DIM, (h + 1) * HEAD_DIM)
            s = lax.dot_general(q_ref[:, hs], k_ref[keys, hs], (((1,), (1,)), ((), ())),
                                preferred_element_type=F32) - crow_ref[0, h, j]
            if masked:
                s = jnp.where(causal, s, -jnp.inf)
            m_old = m_ref[h]
            m_new = jnp.maximum(m_old, jnp.max(s, axis=-1, keepdims=True))
            alpha = jnp.exp2(m_old - m_new)
            p = jnp.exp2(s - jnp.concatenate([m_new] * (tq // HEAD_DIM), axis=1))
            m_ref[h] = m_new
            pv = jnp.dot(p.astype(BF16), vaug_ref[h, keys, :], preferred_element_type=F32)
            acc_ref[h] = jnp.concatenate([alpha, alpha], axis=1) * acc_ref[h] + pv

    def body(j, _):
        step(j, False)
        return 0

    lax.fori_loop(0, i, body, 0)
    step(i, True)
    for h in range(n_heads):
        acc = acc_ref[h]
        o_ref[:, h * HEAD_DIM:(h + 1) * HEAD_DIM] = (acc[:, :HEAD_DIM] / acc[:, HEAD_DIM:]).astype(o_ref.dtype)


def _attention(qkv, c_row, B, S, n_heads, tq=512):
    T = qkv.shape[0]
    W = n_heads * HEAD_DIM
    nq = S // tq
    c_row5 = c_row.reshape(B, n_heads, nq, 1, tq)
    return pl.pallas_call(
        functools.partial(_attn_kernel, n_heads=n_heads, tq=tq),
        grid=(B, nq),
        in_specs=[pl.BlockSpec((tq, W), lambda b, i: (b * nq + i, 0)),
                  pl.BlockSpec((S, W), lambda b, i: (b, 1)),
                  pl.BlockSpec((S, W), lambda b, i: (b, 2)),
                  pl.BlockSpec((1, n_heads, nq, 1, tq), lambda b, i: (b, 0, 0, 0, 0))],
        out_specs=pl.BlockSpec((tq, W), lambda b, i: (b * nq + i, 0)),
        out_shape=jax.ShapeDtypeStruct((T, W), BF16),
        scratch_shapes=[pltpu.VMEM((n_heads, S, 2 * HEAD_DIM), BF16),
                        pltpu.VMEM((n_heads, tq, HEAD_DIM), F32),
                        pltpu.VMEM((n_heads, tq, 2 * HEAD_DIM), F32)],
        compiler_params=_cparams(("arbitrary", "arbitrary")),
        name="fox_attention",
    )(qkv, qkv, qkv, c_row5)


def _gmlp_kernel(h_ref, wz_ref, g_ref, b_ref, ws_ref, bst_ref, o_ref, *, n_groups):
    z = lax.dot_general(h_ref[...], wz_ref[...], NT_DIMS, preferred_element_type=F32)
    z = 0.5 * z * (1.0 + lax.erf(z * (1.0 / math.sqrt(2.0))))
    W = z.shape[1] // 2
    u = z[:, :W]
    v = z[:, W:]
    mu = jnp.mean(v, axis=-1, keepdims=True)
    var = jnp.mean(jnp.square(v - mu), axis=-1, keepdims=True)
    vn = (v - mu) * lax.rsqrt(var + NORM_EPS) * g_ref[...] + b_ref[...]
    row = lax.broadcasted_iota(I32, (CHUNK, CHUNK), 0)
    col = lax.broadcasted_iota(I32, (CHUNK, CHUNK), 1)
    tril = col <= row
    tg = z.shape[0]
    for g in range(n_groups):
        gs = slice(g * GROUP_DIM, (g + 1) * GROUP_DIM)
        wg = jnp.where(tril, ws_ref[g], 0.0).astype(BF16)
        bias = bst_ref[:, g:g + 1]
        for c in range(tg // CHUNK):
            cs = slice(c * CHUNK, (c + 1) * CHUNK)
            mixed = jnp.dot(wg, vn[cs, gs].astype(BF16), preferred_element_type=F32) + bias
            o_ref[cs, gs] = (u[cs, gs] * mixed).astype(o_ref.dtype)


def _gmlp(h, wz, g_v_ln, b_v_ln, w_spatial, b_spatial, tg=1024):
    T, D = h.shape
    W = g_v_ln.shape[0]
    W2 = 2 * W
    G = w_spatial.shape[0]
    return pl.pallas_call(
        functools.partial(_gmlp_kernel, n_groups=G),
        grid=(T // tg,),
        in_specs=[pl.BlockSpec((tg, D), lambda i: (i, 0)),
                  pl.BlockSpec((W2, D), lambda i: (0, 0)),
                  pl.BlockSpec((1, W), lambda i: (0, 0)),
                  pl.BlockSpec((1, W), lambda i: (0, 0)),
                  pl.BlockSpec((G, CHUNK, CHUNK), lambda i: (0, 0, 0)),
                  pl.BlockSpec((CHUNK, G), lambda i: (0, 0))],
        out_specs=pl.BlockSpec((tg, W), lambda i: (i, 0)),
        out_shape=jax.ShapeDtypeStruct((T, W), BF16),
        compiler_params=_cparams(("parallel",)),
        name="gmlp",
    )(h, wz, g_v_ln.reshape(1, W), b_v_ln.reshape(1, W), w_spatial, b_spatial.T)


def _merge_kernel(attn_ref, sg_ref, h_ref, wa_ref, wb_ref, wga_ref, wgb_ref, o_ref,
                  wa_b, wb_b, wga_b, wgb_b):
    @pl.when(pl.program_id(1) == 0)
    def _():
        wa_b[...] = wa_ref[...].astype(BF16)
        wb_b[...] = wb_ref[...].astype(BF16)
        wga_b[...] = wga_ref[...].astype(BF16)
        wgb_b[...] = wgb_ref[...].astype(BF16)

    h = h_ref[...]
    a = jnp.dot(attn_ref[...], wa_b[...], preferred_element_type=F32)
    ga = lax.dot_general(h, wga_b[...], NT_DIMS, preferred_element_type=F32)
    m = jax.nn.sigmoid(ga) * a
    b = jnp.dot(sg_ref[...], wb_b[...], preferred_element_type=F32)
    gb = lax.dot_general(h, wgb_b[...], NT_DIMS, preferred_element_type=F32)
    o_ref[...] = (m + jax.nn.sigmoid(gb) * b).astype(o_ref.dtype)


def _merge(attn, sg, h, wa, wb, wt, g_off, tm=1024, tn=512):
    T, D = h.shape
    Wa = attn.shape[1]
    Wb = sg.shape[1]
    nt = D // tn
    assert g_off % SUBLANES == 0

    def gate_rows(first):
        return pl.BlockSpec((pl.Element(tn), pl.Element(D)),
                            lambda j, i: (pl.multiple_of(g_off + (first + j) * tn, SUBLANES), 0))

    return pl.pallas_call(
        _merge_kernel,
        grid=(nt, T // tm),
        in_specs=[pl.BlockSpec((tm, Wa), lambda j, i: (i, 0)),
                  pl.BlockSpec((tm, Wb), lambda j, i: (i, 0)),
                  pl.BlockSpec((tm, D), lambda j, i: (i, 0)),
                  pl.BlockSpec((Wa, tn), lambda j, i: (0, j)),
                  pl.BlockSpec((Wb, tn), lambda j, i: (0, j)),
                  gate_rows(0), gate_rows(nt)],
        out_specs=pl.BlockSpec((tm, tn), lambda j, i: (i, j)),
        out_shape=jax.ShapeDtypeStruct((T, D), BF16),
        scratch_shapes=[pltpu.VMEM((Wa, tn), BF16), pltpu.VMEM((Wb, tn), BF16),
                        pltpu.VMEM((tn, D), BF16), pltpu.VMEM((tn, D), BF16)],
        compiler_params=_cparams(("arbitrary", "arbitrary")),
        name="gated_merge",
    )(attn, sg, h, wa, wb, wt, wt)


def _out_router_kernel(m_ref, x_ref, wo_ref, g_ref, wr_ref, br_ref,
                       x1_ref, h2_ref, sel_ref, gate_ref, *, n_parts):
    E = br_ref.shape[1]
    part = m_ref.shape[0] // n_parts
    for p in range(n_parts):
        rows = pl.ds(p * part, part)
        x1 = x_ref[rows, :] + jnp.dot(m_ref[rows, :], wo_ref[...], preferred_element_type=F32)
        x1_ref[rows, :] = x1
        ms = jnp.mean(x1 * x1, axis=-1, keepdims=True)
        h2 = x1 * lax.rsqrt(ms + NORM_EPS) * g_ref[...]
        h2_ref[rows, :] = _pack_pairs(h2)
        h2_hi = h2.astype(BF16)
        h2_lo = (h2 - h2_hi.astype(F32)).astype(BF16)
        pa = jnp.dot(h2_hi, wr_ref[...], preferred_element_type=F32)
        pb = jnp.dot(h2_lo, wr_ref[:, :E], preferred_element_type=F32)
        logits = pa[:, :E] + (pa[:, E:] + pb) + br_ref[...]
        lt = jnp.concatenate([logits, jnp.full((part, LANES - E), -jnp.inf, F32)], axis=1).T[:E]
        expert = lax.broadcasted_iota(I32, lt.shape, 0)
        beaten = jnp.zeros(lt.shape, F32)
        for e in range(E):
            le = lt[e:e + 1, :]
            beaten = beaten + jnp.where((le > lt) | ((le == lt) & (expert > e)), 1.0, 0.0)
        chosen = beaten < TOP_K
        top = jnp.max(jnp.where(chosen, lt, -jnp.inf), axis=0, keepdims=True)
        ex = jnp.where(chosen, jnp.exp(lt - top), 0.0)
        gate_t = ex / jnp.sum(ex, axis=0, keepdims=True)
        zeros = jnp.zeros((LANES - E, part), F32)
        sel_ref[rows, :] = jnp.concatenate([jnp.where(chosen, 1.0, 0.0), zeros], axis=0).T[:, :E]
        gate_ref[rows, :] = jnp.concatenate([gate_t, zeros], axis=0).T[:, :E]


def _out_router(merged, x, wo, g_ffn, w_router, b_router, to=512, n_parts=1):
    T, D = x.shape
    E = w_router.shape[1]
    w_hi = w_router.astype(BF16)
    w_lo = (w_router - w_hi.astype(F32)).astype(BF16)
    w_router = jnp.concatenate([w_hi, w_lo], axis=1)
    row = lambda i: (i, 0)
    fixed = lambda i: (0, 0)
    return pl.pallas_call(
        functools.partial(_out_router_kernel, n_parts=n_parts),
        grid=(T // to,),
        in_specs=[pl.BlockSpec((to, D), row), pl.BlockSpec((to, D), row),
                  pl.BlockSpec((D, D), fixed), pl.BlockSpec((1, D), fixed),
                  pl.BlockSpec((D, 2 * E), fixed), pl.BlockSpec((1, E), fixed)],
        out_specs=[pl.BlockSpec((to, D), row), pl.BlockSpec((to, D // 2), row),
                   pl.BlockSpec((to, E), row), pl.BlockSpec((to, E), row)],
        out_shape=[jax.ShapeDtypeStruct((T, D), F32), jax.ShapeDtypeStruct((T, D // 2), jnp.uint32),
                   jax.ShapeDtypeStruct((T, E), F32), jax.ShapeDtypeStruct((T, E), F32)],
        compiler_params=_cparams(("parallel",)),
        name="out_router",
    )(merged, x, wo, g_ffn.reshape(1, D), w_router, b_router.reshape(1, E))


def _routing_kernel(sel_ref, gate_ref, dest_ref, g4_ref, nblk_ref, start_ref, rank_ref, *, tile):
    T, E = sel_ref.shape
    nt = T // tile
    r = lax.broadcasted_iota(I32, (tile, tile), 0)
    c = lax.broadcasted_iota(I32, (tile, tile), 1)
    strict_lower = (c < r).astype(BF16)
    er = lax.broadcasted_iota(I32, (E, E), 0)
    ec = lax.broadcasted_iota(I32, (E, E), 1)
    strict_upper = (er < ec).astype(BF16)

    def pass1(t, carry):
        rows = pl.ds(pl.multiple_of(t * tile, tile), tile)
        a = sel_ref[rows, :]
        rank_ref[rows, :] = jnp.dot(strict_lower, a.astype(BF16), preferred_element_type=F32) + carry
        return carry + jnp.sum(a, axis=0, keepdims=True)

    counts = lax.fori_loop(0, nt, pass1, jnp.zeros((1, E), F32))
    nblk = jnp.floor((counts + (ROW_BLOCK - 1)) * (1.0 / ROW_BLOCK))
    start_blk = jnp.dot(nblk.astype(BF16), strict_upper, preferred_element_type=F32)
    nblk_ref[...] = nblk.astype(I32)
    start_ref[...] = start_blk.astype(I32)
    start_row = start_blk * float(ROW_BLOCK)
    lane = lax.broadcasted_iota(I32, (tile, 128), 1)

    def pass2(t, _):
        rows = pl.ds(pl.multiple_of(t * tile, tile), tile)
        a = sel_ref[rows, :]
        g = gate_ref[rows, :]
        dest_e = rank_ref[rows, :] + start_row
        slot = jnp.dot(a.astype(BF16), strict_upper, preferred_element_type=F32)
        d4 = jnp.zeros((tile, 128), F32)
        g4 = jnp.zeros((tile, 128), F32)
        for s in range(TOP_K):
            pick = (a > 0.5) & (slot == float(s))
            d4 = jnp.where(lane == s, jnp.sum(jnp.where(pick, dest_e, 0.0), axis=-1, keepdims=True), d4)
            g4 = jnp.where(lane == s, jnp.sum(jnp.where(pick, g, 0.0), axis=-1, keepdims=True), g4)
        dest_ref[rows, :] = d4.astype(I32)
        g4_ref[rows, :] = g4
        return 0

    lax.fori_loop(0, nt, pass2, 0)


def _routing(sel, gate, tile=512):
    T, E = sel.shape
    return pl.pallas_call(
        functools.partial(_routing_kernel, tile=tile),
        out_shape=[jax.ShapeDtypeStruct((T, 128), I32), jax.ShapeDtypeStruct((T, 128), F32),
                   jax.ShapeDtypeStruct((1, E), I32), jax.ShapeDtypeStruct((1, E), I32)],
        scratch_shapes=[pltpu.VMEM((T, E), F32)],
        compiler_params=pltpu.CompilerParams(vmem_limit_bytes=VMEM_LIMIT_BYTES),
        name="routing_ranks",
    )(sel, gate)


SC_CORES = 2
SC_SUBCORES = 16
SC_CHUNK = 64


def _sc_gather_rows(table, idx):
    n = idx.shape[0]
    W = table.shape[1]
    n_workers = SC_CORES * SC_SUBCORES
    per_worker = n // n_workers
    assert per_worker * n_workers == n and per_worker % SC_CHUNK == 0
    mesh = plsc.VectorSubcoreMesh(core_axis_name="c", subcore_axis_name="s",
                                  num_cores=SC_CORES, num_subcores=SC_SUBCORES)

    @functools.partial(
        pl.kernel, mesh=mesh,
        out_type=jax.ShapeDtypeStruct((n, W), table.dtype),
        scratch_types=[pltpu.VMEM((SC_CHUNK,), I32), pltpu.VMEM((SC_CHUNK, W), table.dtype),
                       pltpu.SemaphoreType.DMA],
        name="sc_gather_rows",
    )
    def gather(table_hbm, idx_hbm, out_hbm, idx_v, rows_v, sem):
        wid = lax.axis_index("s") * SC_CORES + lax.axis_index("c")
        base = wid * per_worker

        @pl.loop(0, per_worker // SC_CHUNK)
        def _(c):
            off = pl.multiple_of(base + c * SC_CHUNK, 8)
            pltpu.sync_copy(idx_hbm.at[pl.ds(off, SC_CHUNK)], idx_v)
            pltpu.async_copy(table_hbm.at[idx_v], rows_v, sem).wait()
            pltpu.sync_copy(rows_v, out_hbm.at[pl.ds(off, SC_CHUNK)])

    return gather(table, idx)


def _sc_scatter_rows(rows, idx, n_out, n_slots):
    T, W = rows.shape
    n_workers = SC_CORES * SC_SUBCORES
    per_worker = T // n_workers
    assert per_worker * n_workers == T and per_worker % SC_CHUNK == 0
    mesh = plsc.VectorSubcoreMesh(core_axis_name="c", subcore_axis_name="s",
                                  num_cores=SC_CORES, num_subcores=SC_SUBCORES)

    @functools.partial(
        pl.kernel, mesh=mesh,
        out_type=jax.ShapeDtypeStruct((n_out, W), rows.dtype),
        scratch_types=[pltpu.VMEM((SC_CHUNK,), I32), pltpu.VMEM((SC_CHUNK, W), rows.dtype)],
        name="sc_scatter_rows",
    )
    def scatter(rows_hbm, idx_hbm, out_hbm, idx_v, rows_v):
        wid = lax.axis_index("s") * SC_CORES + lax.axis_index("c")
        base = wid * per_worker

        @pl.loop(0, per_worker // SC_CHUNK)
        def _(c):
            off = pl.multiple_of(base + c * SC_CHUNK, 8)
            pltpu.sync_copy(rows_hbm.at[pl.ds(off, SC_CHUNK)], rows_v)
            for k in range(n_slots):
                pltpu.sync_copy(idx_hbm.at[pl.ds(pl.multiple_of(k * T + off, 8), SC_CHUNK)], idx_v)
                pltpu.sync_copy(rows_v, out_hbm.at[idx_v])

    return scatter(rows, idx)


def _expert_kernel(ie_ref, ib_ref, ins_ref,
                   xs_hbm, wup_ref, bup_ref, wdn_ref, bdn_ref, perm_ref, ys_hbm,
                   xg_ref, acc_ref, yst_ref, wupb_ref, wdnb_ref, gsem, osem, *, n_ff_tiles):
    i = pl.program_id(0)
    j = pl.program_id(1)
    n_items = pl.num_programs(0)
    nsub = ins_ref[i]
    slot = i % 2

    def for_regions(n_blocks, fn):
        first = jnp.int32(0)
        for count in REGION_BLOCKS:
            present = (n_blocks & count) != 0
            pl.when(present)(functools.partial(fn, first, count))
            first = first + jnp.where(present, count, 0)

    def rows_of(first, count):
        return pl.ds(pl.multiple_of(first * ROW_BLOCK, ROW_BLOCK), count * ROW_BLOCK)

    def in_copy(item, dst_slot, first, count):
        src = pl.multiple_of((ib_ref[item] + first) * ROW_BLOCK, ROW_BLOCK)
        return pltpu.make_async_copy(xs_hbm.at[pl.ds(src, count * ROW_BLOCK)],
                                     xg_ref.at[dst_slot, rows_of(first, count)], gsem.at[dst_slot])

    def fetch_item(item, n_blocks, dst_slot):
        for_regions(n_blocks, lambda first, count: in_copy(item, dst_slot, first, count).start())

    @pl.when(j == 0)
    def _():
        @pl.when(i == 0)
        def _():
            fetch_item(0, nsub, 0)
            acc_ref[...] = jnp.zeros(acc_ref.shape, F32)

        nxt = jnp.minimum(i + 1, n_items - 1)
        fetch_item(nxt, jnp.where(i + 1 < n_items, ins_ref[nxt], 0), 1 - slot)
        for_regions(nsub, lambda first, count: in_copy(i, slot, first, count).wait())

    @pl.when(nsub > 0)
    def _():
        bup = bup_ref[0]
        perm = perm_ref[...]
        half = perm.shape[0] // 2

        def run_blocks(first, count):
            if first == 0:
                wup = wup_ref[0].astype(BF16)
                wdn = wdn_ref[0].astype(BF16)
                wupb_ref[...] = wup
                wdnb_ref[...] = wdn
            else:
                wup = wupb_ref[...]
                wdn = wdnb_ref[...]
            rows = pl.ds(first * ROW_BLOCK, count * ROW_BLOCK)
            xb = _unpack_pairs(xg_ref[slot, rows, :]).astype(BF16)
            gu = (jnp.dot(xb, wup, preferred_element_type=F32) + bup).astype(BF16)
            glu_parts, lin_parts = [], []
            for p in range(gu.shape[1] // perm.shape[0]):
                gp = jnp.dot(gu[:, p * perm.shape[0]:(p + 1) * perm.shape[0]], perm,
                             preferred_element_type=F32)
                glu_parts.append(gp[:, :half])
                lin_parts.append(gp[:, half:])
            x_glu = jnp.minimum(jnp.concatenate(glu_parts, axis=1), SWIGLU_LIMIT)
            x_lin = jnp.clip(jnp.concatenate(lin_parts, axis=1), -SWIGLU_LIMIT, SWIGLU_LIMIT)
            act = x_glu * jax.nn.sigmoid(SWIGLU_ALPHA * x_glu) * (x_lin + 1.0)
            start = jnp.where(j == 0, bdn_ref[0], acc_ref[rows, :])
            acc_ref[rows, :] = start + jnp.dot(act.astype(BF16), wdn, preferred_element_type=F32)

        for n_blocks, plan in COMPUTE_PLANS.items():
            for first, count in plan:
                shared = [n for n, p in COMPUTE_PLANS.items() if (first, count) in p]
                if n_blocks == shared[0]:
                    cond = functools.reduce(jnp.logical_or, [nsub == n for n in shared])
                    pl.when(cond)(functools.partial(run_blocks, first, count))

    def out_copy(item, first, count):
        dst = pl.multiple_of((ib_ref[item] + first) * ROW_BLOCK, ROW_BLOCK)
        return pltpu.make_async_copy(yst_ref.at[rows_of(first, count)],
                                     ys_hbm.at[pl.ds(dst, count * ROW_BLOCK)], osem)

    @pl.when(j == n_ff_tiles - 1)
    def _():
        prev = jnp.maximum(i - 1, 0)
        for_regions(jnp.where(i > 0, ins_ref[prev], 0),
                    lambda first, count: out_copy(prev, first, count).wait())

        def leave(first, count):
            rows = rows_of(first, count)
            yst_ref[rows, :] = _pack_pairs(acc_ref[rows, :])
            out_copy(i, first, count).start()

        for_regions(nsub, leave)

        @pl.when(i == n_items - 1)
        def _():
            for_regions(nsub, lambda first, count: out_copy(i, first, count).wait())


def _experts(xs, item_e, item_b, item_n, n_active, w_up, b_up, w_down, b_down):
    n_rows, Dp = xs.shape
    D = 2 * Dp
    E, _, F2 = w_up.shape
    F = F2 // 2
    J = F // FF_TILE
    half = 128
    perm = np.zeros((2 * half, 2 * half), np.float32)
    perm[2 * np.arange(half), np.arange(half)] = 1.0
    perm[2 * np.arange(half) + 1, half + np.arange(half)] = 1.0

    def jj(i, j, ins):
        return jnp.where(ins[i] > 0, j, J - 1)

    grid_spec = pltpu.PrefetchScalarGridSpec(
        num_scalar_prefetch=3,
        grid=(n_active, J),
        in_specs=[pl.BlockSpec(memory_space=pl.ANY),
                  pl.BlockSpec((1, D, 2 * FF_TILE), lambda i, j, ie, ib, ins: (ie[i], 0, jj(i, j, ins))),
                  pl.BlockSpec((1, 1, 2 * FF_TILE), lambda i, j, ie, ib, ins: (ie[i], 0, jj(i, j, ins))),
                  pl.BlockSpec((1, FF_TILE, D), lambda i, j, ie, ib, ins: (ie[i], jj(i, j, ins), 0)),
                  pl.BlockSpec((1, 1, D), lambda i, j, ie, ib, ins: (ie[i], 0, 0)),
                  pl.BlockSpec((2 * half, 2 * half), lambda i, j, ie, ib, ins: (0, 0))],
        out_specs=pl.BlockSpec(memory_space=pl.ANY),
        scratch_shapes=[pltpu.VMEM((2, ITEM_ROWS, Dp), jnp.uint32),
                        pltpu.VMEM((ITEM_ROWS, D), F32),
                        pltpu.VMEM((ITEM_ROWS, Dp), jnp.uint32),
                        pltpu.VMEM((D, 2 * FF_TILE), BF16),
                        pltpu.VMEM((FF_TILE, D), BF16),
                        pltpu.SemaphoreType.DMA((2,)), pltpu.SemaphoreType.DMA(())],
    )
    return pl.pallas_call(
        functools.partial(_expert_kernel, n_ff_tiles=J),
        grid_spec=grid_spec,
        out_shape=jax.ShapeDtypeStruct((n_rows, Dp), jnp.uint32),
        compiler_params=_cparams(("arbitrary", "arbitrary")),
        name="expert_ffn",
    )(item_e, item_b, item_n,
      xs, w_up, b_up.reshape(E, 1, F2), w_down, b_down.reshape(E, 1, D), jnp.asarray(perm, BF16))


def _combine_kernel(*refs):
    yk_refs, (g4_ref, x1_ref, g_ref), o_ref = refs[:TOP_K], refs[TOP_K:TOP_K + 3], refs[-1]
    y = x1_ref[...]
    g4 = g4_ref[...]
    for k in range(TOP_K):
        y = y + g4[:, k:k + 1] * _unpack_pairs(yk_refs[k][...])
    ms = jnp.mean(y * y, axis=-1, keepdims=True)
    o_ref[...] = y * lax.rsqrt(ms + NORM_EPS) * g_ref[...]


def _combine(yk, g4, x1, g_final, part, n_parts, out_so_far, tc=512):
    T, D = x1.shape
    nt = T // n_parts // tc
    first = part * nt
    slot_specs = [pl.BlockSpec((tc, D // 2), functools.partial(lambda i, k: (k * nt + i, 0), k=k))
                  for k in range(TOP_K)]
    operands = [*([yk] * TOP_K), g4, x1, g_final.reshape(1, D)]
    in_specs = slot_specs + [pl.BlockSpec((tc, 128), lambda i: (first + i, 0)),
                             pl.BlockSpec((tc, D), lambda i: (first + i, 0)),
                             pl.BlockSpec((1, D), lambda i: (0, 0))]
    aliases = {}
    if out_so_far is not None:
        aliases = {len(operands): 0}
        operands.append(out_so_far)
        in_specs.append(pl.BlockSpec(memory_space=pl.ANY))
    return pl.pallas_call(
        _combine_kernel,
        grid=(nt,),
        in_specs=in_specs,
        out_specs=pl.BlockSpec((tc, D), lambda i: (first + i, 0)),
        out_shape=jax.ShapeDtypeStruct((T, D), F32),
        input_output_aliases=aliases,
        compiler_params=_cparams(("parallel",)),
        name="combine_norm",
    )(*operands)


def _work_items(nblk, start_blk, n_items):
    E = nblk.shape[0]
    per_e = (nblk + ITEM_BLOCKS - 1) // ITEM_BLOCKS
    ends = jnp.cumsum(per_e)
    total = ends[-1]
    idx = jnp.arange(n_items, dtype=I32)
    e = jnp.minimum(jnp.searchsorted(ends, idx, side="right"), E - 1).astype(I32)
    local = idx - (ends[e] - per_e[e])
    active = idx < total
    last_e = e[jnp.maximum(total - 1, 0)]
    item_e = jnp.where(active, e, last_e).astype(I32)
    item_b = jnp.where(active, start_blk[e] + local * ITEM_BLOCKS, 0).astype(I32)
    item_n = jnp.where(active, jnp.clip(nblk[e] - local * ITEM_BLOCKS, 0, ITEM_BLOCKS), 0).astype(I32)
    return item_e, item_b, item_n, jnp.maximum(total, 1).astype(I32)


def kernel(x, g_mix, w_in, b_forget, g_v_ln, b_v_ln, w_spatial, b_spatial, w_branch_attn, w_branch_gmlp, w_out, g_ffn, w_router, b_router, w_expert_up, b_expert_up, w_expert_down, b_expert_down, g_final):
    B, S, D = x.shape
    T = B * S
    n_heads = b_forget.shape[0]
    attn_w = n_heads * HEAD_DIM
    gmlp_w = g_v_ln.shape[0]
    E = w_router.shape[1]
    off_f = 3 * attn_w
    off_z = off_f + n_heads
    off_g = off_z + 2 * gmlp_w

    x2 = x.reshape(T, D)
    w_in_t = w_in.T
    wz = _rows_bf16(w_in_t, off_z, off_g - off_z)
    wo = w_out.astype(BF16)

    qkv, h, c_row = _norm_project(x2, g_mix, w_in_t, 3 * attn_w, attn_w, LOG2E / math.sqrt(HEAD_DIM),
                                  off_f, b_forget, B, S)
    attn = _attention(qkv, c_row, B, S, n_heads)
    sg = _gmlp(h, wz, g_v_ln, b_v_ln, w_spatial, b_spatial)
    merged = _merge(attn, sg, h, w_branch_attn, w_branch_gmlp, w_in_t, off_g)
    x1, h2p, sel, gate = _out_router(merged, x2, wo, g_ffn, w_router, b_router)

    dest4, g4, nblk, start_blk = _routing(sel, gate)
    n_rows = T * TOP_K + E * ROW_BLOCK
    dest_slots = dest4[:, :TOP_K].T.reshape(TOP_K * T)
    max_blocks = n_rows // ROW_BLOCK
    n_items = E + -(-(max_blocks - E) // ITEM_BLOCKS) + 1
    item_e, item_b, item_n, n_active = _work_items(nblk[0], start_blk[0], n_items)
    xs = _sc_scatter_rows(h2p, dest_slots, n_rows, TOP_K)
    ys = _experts(xs, item_e, item_b, item_n, n_active,
                  w_expert_up, b_expert_up, w_expert_down, b_expert_down)
    out = None
    t_part = T // COMBINE_PARTS
    for p in range(COMBINE_PARTS):
        idx = dest4[p * t_part:(p + 1) * t_part, :TOP_K].T.reshape(TOP_K * t_part)
        yk = _sc_gather_rows(ys, idx)
        out = _combine(yk, g4, x1, g_final, p, COMBINE_PARTS, out)
    return out.reshape(B, S, D)
```
